```python
import jax, jax.numpy as jnp
from jax import lax
import numpy as np


D_MODEL = 1024
BATCH = 8
SEQ = 8192
DEPTH = 1

D_MIX = D_MODEL
D_CONV = D_MIX // 2
D_GMLP = D_MIX - D_CONV
HEAD_DIM = 64
N_CONV_HEADS = D_CONV // HEAD_DIM
N_GMLP_HEADS = D_GMLP // HEAD_DIM
CONV_WIDTH = 31
CHUNK = 128
D_FF = 2816
FFN_CONV_WIDTH = 3
N_MOD = 6
RMS_EPS = 1e-6
LN_EPS = 1e-5

kernel_name = "hybrid_conv_gmlp_adaln_block"


def rms_norm(x, g):
    xf = x.astype(jnp.float32)
    y = xf * lax.rsqrt(jnp.mean(xf * xf, axis=-1, keepdims=True) + RMS_EPS)
    return (y * g.astype(jnp.float32)).astype(x.dtype)


def layer_norm(x, g, b):
    xf = x.astype(jnp.float32)
    mu = jnp.mean(xf, axis=-1, keepdims=True)
    var = jnp.mean(jnp.square(xf - mu), axis=-1, keepdims=True)
    y = (xf - mu) * lax.rsqrt(var + LN_EPS)
    return (y * g.astype(jnp.float32) + b.astype(jnp.float32)).astype(x.dtype)


def causal_dwconv(x, w, b):
    k = w.shape[0]
    ch = x.shape[-1]
    y = lax.conv_general_dilated(
        x, w[:, None, :].astype(x.dtype), window_strides=(1,), padding=[(k - 1, 0)],
        dimension_numbers=("NWC", "WIO", "NWC"), feature_group_count=ch)
    return y + b.astype(x.dtype)


def modulate(h, shift, scale):
    return h * (1.0 + scale[:, None, :]) + shift[:, None, :]


def _fwd_setup_inputs(seed: int = 0) -> dict:
    key = jax.random.key(seed)
    ks = jax.random.split(key, 24)
    f32 = jnp.float32
    L, D = DEPTH, D_MODEL

    def nrm(k, shape, scale):
        return jax.random.normal(k, shape, f32) * scale

    return {
        "x": nrm(ks[0], (BATCH, SEQ, D), 1.0),
        "c": nrm(ks[1], (BATCH, D), 1.0),
        "w_ada": nrm(ks[2], (L, D, N_MOD * D), 0.5 * D ** -0.5),
        "b_ada": nrm(ks[3], (L, N_MOD * D), 0.01),
        "norm1_gain": 1.0 + nrm(ks[4], (L, D), 0.02),
        "w_in": nrm(ks[5], (L, D, 2 * D_MIX), D ** -0.5),
        "conv_dw_w": nrm(ks[6], (L, CONV_WIDTH, D_CONV), CONV_WIDTH ** -0.5),
        "conv_dw_b": nrm(ks[7], (L, D_CONV), 0.01),
        "conv_ln_g": 1.0 + nrm(ks[8], (L, D_CONV), 0.02),
        "conv_ln_b": nrm(ks[9], (L, D_CONV), 0.01),
        "gm_ln_g": 1.0 + nrm(ks[10], (L, D_GMLP), 0.02),
        "gm_ln_b": nrm(ks[11], (L, D_GMLP), 0.01),
        "gm_ws": nrm(ks[12], (L, N_GMLP_HEADS, CHUNK, CHUNK), CHUNK ** -0.5),
        "gm_bs": 1.0 + nrm(ks[13], (L, N_GMLP_HEADS, CHUNK), 0.01),
        "mix_out_gain": 1.0 + nrm(ks[14], (L, D_MIX), 0.02),
        "w_out": nrm(ks[15], (L, D_MIX, D), D_MIX ** -0.5),
        "norm2_gain": 1.0 + nrm(ks[16], (L, D), 0.02),
        "w_up": nrm(ks[17], (L, D, 2 * D_FF), D ** -0.5),
        "ffn_dw_w": nrm(ks[18], (L, FFN_CONV_WIDTH, 2 * D_FF), FFN_CONV_WIDTH ** -0.5),
        "ffn_dw_b": nrm(ks[19], (L, 2 * D_FF), 0.01),
        "w_down": nrm(ks[20], (L, D_FF, D), D_FF ** -0.5),
        "final_gain": 1.0 + nrm(ks[21], (D,), 0.02),
    }


def _fwd_reference(x, c, w_ada, b_ada, norm1_gain, w_in, conv_dw_w, conv_dw_b, conv_ln_g, conv_ln_b,
              gm_ln_g, gm_ln_b, gm_ws, gm_bs, mix_out_gain, w_out, norm2_gain, w_up,
              ffn_dw_w, ffn_dw_b, w_down, final_gain):
    bsz, seq, _ = x.shape
    n_chunks = seq // CHUNK
    causal_mask = jnp.tril(jnp.ones((CHUNK, CHUNK), dtype=x.dtype))
    c_act = jax.nn.silu(c)

    for l in range(DEPTH):
        mod = c_act @ w_ada[l] + b_ada[l]
        sh1, sc1, gt1, sh2, sc2, gt2 = jnp.split(mod, N_MOD, axis=-1)

        h = modulate(rms_norm(x, norm1_gain[l]), sh1, sc1)
        z = h @ w_in[l]
        ca, cg, gu, gv = jnp.split(z, [D_CONV, 2 * D_CONV, 2 * D_CONV + D_GMLP], axis=-1)

        a = ca * jax.nn.sigmoid(cg)
        a = causal_dwconv(a, conv_dw_w[l], conv_dw_b[l])
        a = jax.nn.silu(layer_norm(a, conv_ln_g[l], conv_ln_b[l]))

        gu = jax.nn.gelu(gu)
        gv = layer_norm(jax.nn.gelu(gv), gm_ln_g[l], gm_ln_b[l])
        gv = gv.reshape(bsz, n_chunks, CHUNK, N_GMLP_HEADS, HEAD_DIM)
        ws = gm_ws[l] * causal_mask[None]
        sp = jnp.einsum("hts,bnshc->bnthc", ws, gv)
        sp = sp + jnp.transpose(gm_bs[l])[None, None, :, :, None]
        g = gu * sp.reshape(bsz, seq, D_GMLP)

        y = jnp.concatenate([rms_norm(a, mix_out_gain[l, :D_CONV]),
                             rms_norm(g, mix_out_gain[l, D_CONV:])], axis=-1)
        x = x + gt1[:, None, :] * (y @ w_out[l])

        h = modulate(rms_norm(x, norm2_gain[l]), sh2, sc2)
        up = causal_dwconv(h @ w_up[l], ffn_dw_w[l], ffn_dw_b[l])
        val, gate = jnp.split(up, 2, axis=-1)
        x = x + gt2[:, None, :] * ((jax.nn.silu(gate) * val) @ w_down[l])

    return rms_norm(x, final_gain)


import jax as _jax
import jax.numpy as _jnp

TWIN_FORMAT = 'train_step'
FWD_PARAMS = ['x', 'c', 'w_ada', 'b_ada', 'norm1_gain', 'w_in', 'conv_dw_w', 'conv_dw_b', 'conv_ln_g', 'conv_ln_b', 'gm_ln_g', 'gm_ln_b', 'gm_ws', 'gm_bs', 'mix_out_gain', 'w_out', 'norm2_gain', 'w_up', 'ffn_dw_w', 'ffn_dw_b', 'w_down', 'final_gain']
TWIN_WEIGHTS = ['w_ada', 'b_ada', 'norm1_gain', 'w_in', 'conv_dw_w', 'conv_dw_b', 'conv_ln_g', 'conv_ln_b', 'gm_ln_g', 'gm_ln_b', 'gm_ws', 'gm_bs', 'mix_out_gain', 'w_out', 'norm2_gain', 'w_up', 'ffn_dw_w', 'ffn_dw_b', 'w_down', 'final_gain']
TWIN_DIFF_INPUT = 'x'
TWIN_INPUTS = ['x', 'c', 'w_ada', 'b_ada', 'norm1_gain', 'w_in', 'conv_dw_w', 'conv_dw_b', 'conv_ln_g', 'conv_ln_b', 'gm_ln_g', 'gm_ln_b', 'gm_ws', 'gm_bs', 'mix_out_gain', 'w_out', 'norm2_gain', 'w_up', 'ffn_dw_w', 'ffn_dw_b', 'w_down', 'final_gain', 'loss_target', 'm_w_ada', 'm_b_ada', 'm_norm1_gain', 'm_w_in', 'm_conv_dw_w', 'm_conv_dw_b', 'm_conv_ln_g', 'm_conv_ln_b', 'm_gm_ln_g', 'm_gm_ln_b', 'm_gm_ws', 'm_gm_bs', 'm_mix_out_gain', 'm_w_out', 'm_norm2_gain', 'm_w_up', 'm_ffn_dw_w', 'm_ffn_dw_b', 'm_w_down', 'm_final_gain', 'v_w_ada', 'v_b_ada', 'v_norm1_gain', 'v_w_in', 'v_conv_dw_w', 'v_conv_dw_b', 'v_conv_ln_g', 'v_conv_ln_b', 'v_gm_ln_g', 'v_gm_ln_b', 'v_gm_ws', 'v_gm_bs', 'v_mix_out_gain', 'v_w_out', 'v_norm2_gain', 'v_w_up', 'v_ffn_dw_w', 'v_ffn_dw_b', 'v_w_down', 'v_final_gain']
TWIN_OUTPUTS = ['loss', 'grad_x', 'grad_w_ada', 'grad_b_ada', 'grad_norm1_gain', 'grad_w_in', 'grad_conv_dw_w', 'grad_conv_dw_b', 'grad_conv_ln_g', 'grad_conv_ln_b', 'grad_gm_ln_g', 'grad_gm_ln_b', 'grad_gm_ws', 'grad_gm_bs', 'grad_mix_out_gain', 'grad_w_out', 'grad_norm2_gain', 'grad_w_up', 'grad_ffn_dw_w', 'grad_ffn_dw_b', 'grad_w_down', 'grad_final_gain', 'delta_w_ada', 'delta_b_ada', 'delta_norm1_gain', 'delta_w_in', 'delta_conv_dw_w', 'delta_conv_dw_b', 'delta_conv_ln_g', 'delta_conv_ln_b', 'delta_gm_ln_g', 'delta_gm_ln_b', 'delta_gm_ws', 'delta_gm_bs', 'delta_mix_out_gain', 'delta_w_out', 'delta_norm2_gain', 'delta_w_up', 'delta_ffn_dw_w', 'delta_ffn_dw_b', 'delta_w_down', 'delta_final_gain', 'new_m_w_ada', 'new_m_b_ada', 'new_m_norm1_gain', 'new_m_w_in', 'new_m_conv_dw_w', 'new_m_conv_dw_b', 'new_m_conv_ln_g', 'new_m_conv_ln_b', 'new_m_gm_ln_g', 'new_m_gm_ln_b', 'new_m_gm_ws', 'new_m_gm_bs', 'new_m_mix_out_gain', 'new_m_w_out', 'new_m_norm2_gain', 'new_m_w_up', 'new_m_ffn_dw_w', 'new_m_ffn_dw_b', 'new_m_w_down', 'new_m_final_gain', 'new_v_w_ada', 'new_v_b_ada', 'new_v_norm1_gain', 'new_v_w_in', 'new_v_conv_dw_w', 'new_v_conv_dw_b', 'new_v_conv_ln_g', 'new_v_conv_ln_b', 'new_v_gm_ln_g', 'new_v_gm_ln_b', 'new_v_gm_ws', 'new_v_gm_bs', 'new_v_mix_out_gain', 'new_v_w_out', 'new_v_norm2_gain', 'new_v_w_up', 'new_v_ffn_dw_w', 'new_v_ffn_dw_b', 'new_v_w_down', 'new_v_final_gain']
TWIN_LEAF_KINDS = {'loss': 'loss', 'grad_x': 'grad_x', 'grad_w_ada': 'grad_w', 'grad_b_ada': 'grad_w', 'grad_norm1_gain': 'grad_w', 'grad_w_in': 'grad_w', 'grad_conv_dw_w': 'grad_w', 'grad_conv_dw_b': 'grad_w', 'grad_conv_ln_g': 'grad_w', 'grad_conv_ln_b': 'grad_w', 'grad_gm_ln_g': 'grad_w', 'grad_gm_ln_b': 'grad_w', 'grad_gm_ws': 'grad_w', 'grad_gm_bs': 'grad_w', 'grad_mix_out_gain': 'grad_w', 'grad_w_out': 'grad_w', 'grad_norm2_gain': 'grad_w', 'grad_w_up': 'grad_w', 'grad_ffn_dw_w': 'grad_w', 'grad_ffn_dw_b': 'grad_w', 'grad_w_down': 'grad_w', 'grad_final_gain': 'grad_w', 'delta_w_ada': 'delta_w', 'delta_b_ada': 'delta_w', 'delta_norm1_gain': 'delta_w', 'delta_w_in': 'delta_w', 'delta_conv_dw_w': 'delta_w', 'delta_conv_dw_b': 'delta_w', 'delta_conv_ln_g': 'delta_w', 'delta_conv_ln_b': 'delta_w', 'delta_gm_ln_g': 'delta_w', 'delta_gm_ln_b': 'delta_w', 'delta_gm_ws': 'delta_w', 'delta_gm_bs': 'delta_w', 'delta_mix_out_gain': 'delta_w', 'delta_w_out': 'delta_w', 'delta_norm2_gain': 'delta_w', 'delta_w_up': 'delta_w', 'delta_ffn_dw_w': 'delta_w', 'delta_ffn_dw_b': 'delta_w', 'delta_w_down': 'delta_w', 'delta_final_gain': 'delta_w', 'new_m_w_ada': 'new_m', 'new_m_b_ada': 'new_m', 'new_m_norm1_gain': 'new_m', 'new_m_w_in': 'new_m', 'new_m_conv_dw_w': 'new_m', 'new_m_conv_dw_b': 'new_m', 'new_m_conv_ln_g': 'new_m', 'new_m_conv_ln_b': 'new_m', 'new_m_gm_ln_g': 'new_m', 'new_m_gm_ln_b': 'new_m', 'new_m_gm_ws': 'new_m', 'new_m_gm_bs': 'new_m', 'new_m_mix_out_gain': 'new_m', 'new_m_w_out': 'new_m', 'new_m_norm2_gain': 'new_m', 'new_m_w_up': 'new_m', 'new_m_ffn_dw_w': 'new_m', 'new_m_ffn_dw_b': 'new_m', 'new_m_w_down': 'new_m', 'new_m_final_gain': 'new_m', 'new_v_w_ada': 'new_v', 'new_v_b_ada': 'new_v', 'new_v_norm1_gain': 'new_v', 'new_v_w_in': 'new_v', 'new_v_conv_dw_w': 'new_v', 'new_v_conv_dw_b': 'new_v', 'new_v_conv_ln_g': 'new_v', 'new_v_conv_ln_b': 'new_v', 'new_v_gm_ln_g': 'new_v', 'new_v_gm_ln_b': 'new_v', 'new_v_gm_ws': 'new_v', 'new_v_gm_bs': 'new_v', 'new_v_mix_out_gain': 'new_v', 'new_v_w_out': 'new_v', 'new_v_norm2_gain': 'new_v', 'new_v_w_up': 'new_v', 'new_v_ffn_dw_w': 'new_v', 'new_v_ffn_dw_b': 'new_v', 'new_v_w_down': 'new_v', 'new_v_final_gain': 'new_v'}


def _forward(args):
    return _fwd_reference(*[args[k] for k in FWD_PARAMS])


def _output_shape():
    def fwd():
        inp = _fwd_setup_inputs(0)
        return _fwd_reference(*[inp[k] for k in FWD_PARAMS])
    out = _jax.eval_shape(fwd)
    return out.shape, out.dtype

N_MICROBATCH = 1
ADAM_LR = 0.001
ADAM_B1 = 0.9
ADAM_B2 = 0.999
ADAM_EPS = 1e-08
ADAM_WD = 0.01
ADAM_STEP = 10
PER_EXAMPLE_BATCH_AXIS = {'x': 0, 'c': 0, 'loss_target': 0}
SHARED_INPUTS = []
_WEIGHT_DTYPES = {'w_ada': _jnp.float32, 'b_ada': _jnp.float32, 'norm1_gain': _jnp.float32, 'w_in': _jnp.float32, 'conv_dw_w': _jnp.float32, 'conv_dw_b': _jnp.float32, 'conv_ln_g': _jnp.float32, 'conv_ln_b': _jnp.float32, 'gm_ln_g': _jnp.float32, 'gm_ln_b': _jnp.float32, 'gm_ws': _jnp.float32, 'gm_bs': _jnp.float32, 'mix_out_gain': _jnp.float32, 'w_out': _jnp.float32, 'norm2_gain': _jnp.float32, 'w_up': _jnp.float32, 'ffn_dw_w': _jnp.float32, 'ffn_dw_b': _jnp.float32, 'w_down': _jnp.float32, 'final_gain': _jnp.float32}
MOMENT_SCALE = {'w_ada': 8.913741e-02, 'b_ada': 1.533361e-01, 'norm1_gain': 8.509972e-02, 'w_in': 6.143622e-02, 'conv_dw_w': 7.759549e-02, 'conv_dw_b': 1.599219e-01, 'conv_ln_g': 8.910516e-02, 'conv_ln_b': 8.176382e-02, 'gm_ln_g': 4.126624e-02, 'gm_ln_b': 3.947475e-02, 'gm_ws': 2.944649e-02, 'gm_bs': 4.204182e-02, 'mix_out_gain': 7.447234e-02, 'w_out': 7.533999e-02, 'norm2_gain': 7.161722e-02, 'w_up': 3.151081e-02, 'ffn_dw_w': 3.274886e-02, 'ffn_dw_b': 2.936446e-02, 'w_down': 5.194111e-02, 'final_gain': 6.412748e+01}


def _to_microbatches(a, axis):
    t = _jnp.moveaxis(a, axis, 0)
    t = t.reshape((N_MICROBATCH, t.shape[0] // N_MICROBATCH) + t.shape[1:])
    return _jnp.moveaxis(t, 1, axis + 1)


def setup_inputs(seed: int = 0) -> dict:
    inp = _fwd_setup_inputs(seed)
    key = _jax.random.fold_in(_jax.random.key(seed), 7919)
    shape, _ = _output_shape()
    out = dict(inp)
    out["loss_target"] = _jax.random.normal(_jax.random.fold_in(key, 0), shape, _jnp.float32)
    for i, name in enumerate(TWIN_WEIGHTS):
        w = inp[name].astype(_jnp.float32)
        if MOMENT_SCALE is None:
            s = _jnp.sqrt(_jnp.mean(_jnp.square(w)) + 1e-30)
        else:
            s = MOMENT_SCALE[name]
        km, kv = _jax.random.split(_jax.random.fold_in(key, i + 1))
        out[name] = w
        out["m_" + name] = s * _jax.random.normal(km, w.shape, _jnp.float32)
        out["v_" + name] = (s * s) * _jax.random.uniform(kv, w.shape, _jnp.float32, 0.5, 1.5)
    if N_MICROBATCH > 1:
        for name, axis in PER_EXAMPLE_BATCH_AXIS.items():
            out[name] = _to_microbatches(out[name], axis)
    return {'x': out['x'], 'c': out['c'], 'w_ada': out['w_ada'], 'b_ada': out['b_ada'], 'norm1_gain': out['norm1_gain'], 'w_in': out['w_in'], 'conv_dw_w': out['conv_dw_w'], 'conv_dw_b': out['conv_dw_b'], 'conv_ln_g': out['conv_ln_g'], 'conv_ln_b': out['conv_ln_b'], 'gm_ln_g': out['gm_ln_g'], 'gm_ln_b': out['gm_ln_b'], 'gm_ws': out['gm_ws'], 'gm_bs': out['gm_bs'], 'mix_out_gain': out['mix_out_gain'], 'w_out': out['w_out'], 'norm2_gain': out['norm2_gain'], 'w_up': out['w_up'], 'ffn_dw_w': out['ffn_dw_w'], 'ffn_dw_b': out['ffn_dw_b'], 'w_down': out['w_down'], 'final_gain': out['final_gain'], 'loss_target': out['loss_target'], 'm_w_ada': out['m_w_ada'], 'm_b_ada': out['m_b_ada'], 'm_norm1_gain': out['m_norm1_gain'], 'm_w_in': out['m_w_in'], 'm_conv_dw_w': out['m_conv_dw_w'], 'm_conv_dw_b': out['m_conv_dw_b'], 'm_conv_ln_g': out['m_conv_ln_g'], 'm_conv_ln_b': out['m_conv_ln_b'], 'm_gm_ln_g': out['m_gm_ln_g'], 'm_gm_ln_b': out['m_gm_ln_b'], 'm_gm_ws': out['m_gm_ws'], 'm_gm_bs': out['m_gm_bs'], 'm_mix_out_gain': out['m_mix_out_gain'], 'm_w_out': out['m_w_out'], 'm_norm2_gain': out['m_norm2_gain'], 'm_w_up': out['m_w_up'], 'm_ffn_dw_w': out['m_ffn_dw_w'], 'm_ffn_dw_b': out['m_ffn_dw_b'], 'm_w_down': out['m_w_down'], 'm_final_gain': out['m_final_gain'], 'v_w_ada': out['v_w_ada'], 'v_b_ada': out['v_b_ada'], 'v_norm1_gain': out['v_norm1_gain'], 'v_w_in': out['v_w_in'], 'v_conv_dw_w': out['v_conv_dw_w'], 'v_conv_dw_b': out['v_conv_dw_b'], 'v_conv_ln_g': out['v_conv_ln_g'], 'v_conv_ln_b': out['v_conv_ln_b'], 'v_gm_ln_g': out['v_gm_ln_g'], 'v_gm_ln_b': out['v_gm_ln_b'], 'v_gm_ws': out['v_gm_ws'], 'v_gm_bs': out['v_gm_bs'], 'v_mix_out_gain': out['v_mix_out_gain'], 'v_w_out': out['v_w_out'], 'v_norm2_gain': out['v_norm2_gain'], 'v_w_up': out['v_w_up'], 'v_ffn_dw_w': out['v_ffn_dw_w'], 'v_ffn_dw_b': out['v_ffn_dw_b'], 'v_w_down': out['v_w_down'], 'v_final_gain': out['v_final_gain']}


def _loss(weights, diff, rest, loss_target):
    with _jax.named_scope("forward"):
        args = {**rest, TWIN_DIFF_INPUT: diff, **{k: w.astype(_WEIGHT_DTYPES[k]) for k, w in weights.items()}}
        y = _forward(args)
    with _jax.named_scope("loss_head"):
        err = _jnp.square(y.astype(_jnp.float32) - loss_target)
        return 0.5 * _jnp.sum(_jnp.mean(err, axis=-1)) if err.ndim else 0.5 * err


def _adamw(w, g, m, v):
    m = ADAM_B1 * m + (1.0 - ADAM_B1) * g
    v = ADAM_B2 * v + (1.0 - ADAM_B2) * _jnp.square(g)
    m_hat = m / (1.0 - ADAM_B1 ** ADAM_STEP)
    v_hat = v / (1.0 - ADAM_B2 ** ADAM_STEP)
    delta = -ADAM_LR * (m_hat / (_jnp.sqrt(v_hat) + ADAM_EPS) + ADAM_WD * w)
    return delta, m, v


def reference(x, c, w_ada, b_ada, norm1_gain, w_in, conv_dw_w, conv_dw_b, conv_ln_g, conv_ln_b, gm_ln_g, gm_ln_b, gm_ws, gm_bs, mix_out_gain, w_out, norm2_gain, w_up, ffn_dw_w, ffn_dw_b, w_down, final_gain, loss_target, m_w_ada, m_b_ada, m_norm1_gain, m_w_in, m_conv_dw_w, m_conv_dw_b, m_conv_ln_g, m_conv_ln_b, m_gm_ln_g, m_gm_ln_b, m_gm_ws, m_gm_bs, m_mix_out_gain, m_w_out, m_norm2_gain, m_w_up, m_ffn_dw_w, m_ffn_dw_b, m_w_down, m_final_gain, v_w_ada, v_b_ada, v_norm1_gain, v_w_in, v_conv_dw_w, v_conv_dw_b, v_conv_ln_g, v_conv_ln_b, v_gm_ln_g, v_gm_ln_b, v_gm_ws, v_gm_bs, v_mix_out_gain, v_w_out, v_norm2_gain, v_w_up, v_ffn_dw_w, v_ffn_dw_b, v_w_down, v_final_gain):
    given = dict(x=x, c=c, w_ada=w_ada, b_ada=b_ada, norm1_gain=norm1_gain, w_in=w_in, conv_dw_w=conv_dw_w, conv_dw_b=conv_dw_b, conv_ln_g=conv_ln_g, conv_ln_b=conv_ln_b, gm_ln_g=gm_ln_g, gm_ln_b=gm_ln_b, gm_ws=gm_ws, gm_bs=gm_bs, mix_out_gain=mix_out_gain, w_out=w_out, norm2_gain=norm2_gain, w_up=w_up, ffn_dw_w=ffn_dw_w, ffn_dw_b=ffn_dw_b, w_down=w_down, final_gain=final_gain, loss_target=loss_target, m_w_ada=m_w_ada, m_b_ada=m_b_ada, m_norm1_gain=m_norm1_gain, m_w_in=m_w_in, m_conv_dw_w=m_conv_dw_w, m_conv_dw_b=m_conv_dw_b, m_conv_ln_g=m_conv_ln_g, m_conv_ln_b=m_conv_ln_b, m_gm_ln_g=m_gm_ln_g, m_gm_ln_b=m_gm_ln_b, m_gm_ws=m_gm_ws, m_gm_bs=m_gm_bs, m_mix_out_gain=m_mix_out_gain, m_w_out=m_w_out, m_norm2_gain=m_norm2_gain, m_w_up=m_w_up, m_ffn_dw_w=m_ffn_dw_w, m_ffn_dw_b=m_ffn_dw_b, m_w_down=m_w_down, m_final_gain=m_final_gain, v_w_ada=v_w_ada, v_b_ada=v_b_ada, v_norm1_gain=v_norm1_gain, v_w_in=v_w_in, v_conv_dw_w=v_conv_dw_w, v_conv_dw_b=v_conv_dw_b, v_conv_ln_g=v_conv_ln_g, v_conv_ln_b=v_conv_ln_b, v_gm_ln_g=v_gm_ln_g, v_gm_ln_b=v_gm_ln_b, v_gm_ws=v_gm_ws, v_gm_bs=v_gm_bs, v_mix_out_gain=v_mix_out_gain, v_w_out=v_w_out, v_norm2_gain=v_norm2_gain, v_w_up=v_w_up, v_ffn_dw_w=v_ffn_dw_w, v_ffn_dw_b=v_ffn_dw_b, v_w_down=v_w_down, v_final_gain=v_final_gain)
    weights = {n: given[n] for n in TWIN_WEIGHTS}
    shared = {n: given[n] for n in SHARED_INPUTS}
    per_example = {n: given[n] for n in ['x', 'c']}
    grad_fn = _jax.value_and_grad(_loss, argnums=(0, 1))

    def one_microbatch(ex, loss_target):
        ex = dict(ex)
        diff = ex.pop(TWIN_DIFF_INPUT)
        return grad_fn(weights, diff, {**shared, **ex}, loss_target)

    if N_MICROBATCH == 1:
        loss, (grad_w, grad_x) = one_microbatch(per_example, given["loss_target"])
    else:
        def body(carry, xs):
            loss_sum, grad_sum = carry
            l_k, (gw_k, gx_k) = one_microbatch(xs[0], xs[1])
            with _jax.named_scope("update"):
                return (loss_sum + l_k, _jax.tree.map(_jnp.add, grad_sum, gw_k)), gx_k

        init = (_jnp.zeros((), _jnp.float32), _jax.tree.map(_jnp.zeros_like, weights))
        (loss, grad_w), grad_x = _jax.lax.scan(body, init, (per_example, given["loss_target"]))
    with _jax.named_scope("update"):
        delta_w, new_m, new_v = {}, {}, {}
        for n in TWIN_WEIGHTS:
            delta_w[n], new_m[n], new_v[n] = _adamw(weights[n], grad_w[n], given["m_" + n], given["v_" + n])
    return (loss, grad_x, *[grad_w[n] for n in TWIN_WEIGHTS], *[delta_w[n] for n in TWIN_WEIGHTS],
            *[new_m[n] for n in TWIN_WEIGHTS], *[new_v[n] for n in TWIN_WEIGHTS])
```

```python
import jax
import jax.numpy as jnp
from jax import lax
from jax.experimental import pallas as pl
from jax.experimental.pallas import tpu as pltpu

F32 = jnp.float32
BF16 = jnp.bfloat16
NDEV = 8
D = 1024
DC = 512
DFF = 2816
KC = 31
KF = 3
CHUNK = 128
NH = 8
HD = 64
FCH = 256
HALO_C = 32
HALO_F = 8
RMS_EPS = 1e-6
LN_EPS = 1e-5
ADAM_LR = 0.001
ADAM_B1 = 0.9
ADAM_B2 = 0.999
ADAM_EPS = 1e-08
ADAM_WD = 0.01
ADAM_STEP = 10
GELU_K = 0.7978845608028654
GELU_C = 0.044715

MESH = pl.DeviceIdType.MESH
ANY = pl.BlockSpec(memory_space=pl.ANY)

G1, SH1, SC1, GT1, G2, SH2, SC2, GT2, MOG, GF = range(10)
CB, CLG, CLB, GLG, GLB = range(5)


def _full(shape):
    return pl.BlockSpec(shape, lambda *_: (0,) * len(shape))


def _arb(n=1):
    return pltpu.CompilerParams(dimension_semantics=("arbitrary",) * n)


def _row(ref, r):
    return ref[pl.ds(r, 1), :]


def _colsum(v):
    return jnp.sum(v, axis=0, keepdims=True)


def _rowmean(v):
    return jnp.mean(v, axis=-1, keepdims=True)


def _rms(x):
    r = lax.rsqrt(_rowmean(x * x) + RMS_EPS)
    return x * r, r


def _rms_bwd(dxn, xn, r):
    return r * (dxn - xn * _rowmean(dxn * xn))


def _ln(x):
    mu = _rowmean(x)
    xc = x - mu
    rstd = lax.rsqrt(_rowmean(xc * xc) + LN_EPS)
    return xc * rstd, rstd


def _ln_bwd(dxh, xhat, rstd):
    return rstd * (dxh - _rowmean(dxh) - xhat * _rowmean(dxh * xhat))


def _sigmoid(x):
    return 1.0 / (1.0 + jnp.exp(-x))


def _gelu(x):
    t = jnp.tanh(GELU_K * (x + GELU_C * x * x * x))
    return 0.5 * x * (1.0 + t), t


def _gelu_grad(x, t):
    return 0.5 * (1.0 + t) + 0.5 * x * (1.0 - t * t) * (GELU_K * (1.0 + 3.0 * GELU_C * x * x))


def _dot(a, b):
    return jnp.dot(a, b, preferred_element_type=F32)


def _dot_nt(a, b):
    return lax.dot_general(a, b, (((1,), (1,)), ((), ())), preferred_element_type=F32)


def _dot_tn(a, b):
    return lax.dot_general(a, b, (((0,), (0,)), ((), ())), preferred_element_type=F32)


def _place():
    return lax.axis_index("x"), lax.axis_index("y"), lax.axis_index("c")


def _all_gather(name, xs):
    rows, cols = xs.shape

    def body(x_ref, out_ref, send_sems, recv_sems, local_sem):
        x, y, c = _place()
        me, sibling = (x, y, c), (x, y, 1 - c)
        chips = [(1 - x, y), (x, 1 - y), (1 - x, 1 - y)]

        def slot(px, py, pc):
            return out_ref.at[4 * px + 2 * py + pc]

        def copy(k, block, to, src=None):
            return pltpu.make_async_remote_copy(
                src_ref=slot(*block) if src is None else src, dst_ref=slot(*block),
                send_sem=send_sems.at[k], recv_sem=recv_sems.at[k], device_id=to, device_id_type=MESH)

        mine = pltpu.make_async_copy(x_ref, slot(*me), local_sem)
        mine.start()
        first = [copy(0, me, sibling, src=x_ref)]
        first += [copy(1 + j, me, (*chip, c), src=x_ref) for j, chip in enumerate(chips)]
        for cp in first:
            cp.start()
        passed = [copy(4 + j, (*chip, c), sibling) for j, chip in enumerate(chips)]
        for j, chip in enumerate(chips):
            copy(1 + j, (*chip, c), me).wait_recv()
            passed[j].start()
        copy(0, sibling, me).wait_recv()
        for j, chip in enumerate(chips):
            copy(4 + j, (*chip, 1 - c), me).wait_recv()
        for cp in first + passed:
            cp.wait_send()
        mine.wait()

    return pl.pallas_call(
        body, name=name, out_shape=jax.ShapeDtypeStruct((NDEV, rows, cols), xs.dtype),
        in_specs=[ANY], out_specs=ANY,
        scratch_shapes=[pltpu.SemaphoreType.DMA((7,)), pltpu.SemaphoreType.DMA((7,)), pltpu.SemaphoreType.DMA],
    )(xs)


def _sibling_swap(name, g4):
    _, _, rows, cols = g4.shape

    def body(g_ref, out_ref, send_sems, recv_sems):
        x, y, c = _place()
        cps = [pltpu.make_async_remote_copy(
            src_ref=g_ref.at[k, 1 - c], dst_ref=out_ref.at[k], send_sem=send_sems.at[k], recv_sem=recv_sems.at[k],
            device_id=(x, y, 1 - c), device_id_type=MESH) for k in range(4)]
        for cp in cps:
            cp.start()
        for cp in cps:
            cp.wait()

    return pl.pallas_call(
        body, name=name, out_shape=jax.ShapeDtypeStruct((4, rows, cols), g4.dtype),
        in_specs=[ANY], out_specs=ANY,
        scratch_shapes=[pltpu.SemaphoreType.DMA((4,)), pltpu.SemaphoreType.DMA((4,))],
    )(g4)


def _chip_exchange(name, h):
    _, rows, cols = h.shape

    def body(h_ref, out_ref, send_sems, recv_sems, local_sem):
        x, y, c = _place()
        mychip = 2 * x + y
        chips = [(1 - x, y), (x, 1 - y), (1 - x, 1 - y)]

        def copy(j):
            px, py = chips[j]
            return pltpu.make_async_remote_copy(
                src_ref=h_ref.at[2 * px + py], dst_ref=out_ref.at[mychip],
                send_sem=send_sems.at[j], recv_sem=recv_sems.at[j], device_id=(px, py, c), device_id_type=MESH)

        def arrival(j):
            px, py = chips[j]
            return pltpu.make_async_remote_copy(
                src_ref=h_ref.at[mychip], dst_ref=out_ref.at[2 * px + py],
                send_sem=send_sems.at[j], recv_sem=recv_sems.at[j], device_id=(px, py, c), device_id_type=MESH)

        mine = pltpu.make_async_copy(h_ref.at[mychip], out_ref.at[mychip], local_sem)
        mine.start()
        cps = [copy(j) for j in range(3)]
        for cp in cps:
            cp.start()
        for j in range(3):
            arrival(j).wait_recv()
        for cp in cps:
            cp.wait_send()
        mine.wait()

    return pl.pallas_call(
        body, name=name, out_shape=jax.ShapeDtypeStruct((4, rows, cols), h.dtype),
        in_specs=[ANY], out_specs=ANY,
        scratch_shapes=[pltpu.SemaphoreType.DMA((3,)), pltpu.SemaphoreType.DMA((3,)), pltpu.SemaphoreType.DMA],
    )(h)


def _mod_part(c_all, w_shard, b_shard):
    ncol = w_shard.shape[1]

    def body(c_ref, w_ref, b_ref, o_ref):
        cv = c_ref[...]
        ca = cv * _sigmoid(cv)
        o_ref[...] = _dot(ca.astype(BF16), w_ref[...].astype(BF16)) + b_ref[...]

    return pl.pallas_call(body, name="mod_part", out_shape=jax.ShapeDtypeStruct((NDEV, ncol), F32))(
        c_all, w_shard, b_shard)


def _ada_grad(c_all_t, dmod_cols):
    ncol = dmod_cols.shape[1]

    def body(ct_ref, dm_ref, o_ref):
        ct = ct_ref[...]
        ca = ct * _sigmoid(ct)
        acc = jnp.zeros((D, ncol), F32)
        for b in range(NDEV):
            acc = acc + ca[:, b:b + 1] * dm_ref[pl.ds(b, 1), :]
        o_ref[...] = acc

    return pl.pallas_call(body, name="ada_grad", out_shape=jax.ShapeDtypeStruct((D, ncol), F32))(
        c_all_t, dmod_cols)


def _fwd_in(x2d, vecs, w_in_b, tm):
    s = x2d.shape[0]

    def body(x_ref, v_ref, w_ref, z_ref, a0_ref, h1_ref):
        xn, _ = _rms(x_ref[...])
        h = (xn * _row(v_ref, G1)) * (1.0 + _row(v_ref, SC1)) + _row(v_ref, SH1)
        hb = h.astype(BF16)
        h1_ref[...] = hb
        z = _dot(hb, w_ref[...])
        z_ref[...] = z
        a0_ref[...] = z[:, :DC] * _sigmoid(z[:, DC:2 * DC])

    return pl.pallas_call(
        body, name="fwd_in", grid=(s // tm,),
        in_specs=[pl.BlockSpec((tm, D), lambda i: (i, 0)), _full((16, D)), _full((D, 4 * DC))],
        out_specs=[pl.BlockSpec((tm, 4 * DC), lambda i: (i, 0)), pl.BlockSpec((tm, DC), lambda i: (i, 0)),
                   pl.BlockSpec((tm, D), lambda i: (i, 0))],
        out_shape=[jax.ShapeDtypeStruct((s, 4 * DC), F32), jax.ShapeDtypeStruct((s, DC), F32),
                   jax.ShapeDtypeStruct((s, D), BF16)],
        compiler_params=_arb(),
    )(x2d, vecs, w_in_b)


def _causal_mask(lower):
    r = lax.broadcasted_iota(jnp.int32, (CHUNK, CHUNK), 0)
    c = lax.broadcasted_iota(jnp.int32, (CHUNK, CHUNK), 1)
    return (r >= c) if lower else (r <= c)


def _first_head_lanes():
    return lax.broadcasted_iota(jnp.int32, (CHUNK, CHUNK), 1) < HD


def _fwd_mid(a0, z, x2d, vecs, v512, conv_w, gm_ws, bs_exp, w_out_b, tm):
    s = x2d.shape[0]
    hb = tm // HALO_C

    def body(a0_ref, halo_ref, zg_ref, x_ref, v_ref, p_ref, cw_ref, ws_ref, bs_ref, wo_ref,
             a1_ref, sp_ref, x2_ref, o1_ref, h2_ref, ext):
        i = pl.program_id(0)
        halo = halo_ref[...]
        ext[pl.ds(0, HALO_C), :] = jnp.where(i > 0, halo, jnp.zeros_like(halo))
        ext[pl.ds(HALO_C, tm), :] = a0_ref[...]
        for r0 in range(0, tm, 32):
            acc = jnp.broadcast_to(_row(p_ref, CB), (32, DC))
            for k in range(KC):
                acc = acc + ext[pl.ds(r0 + HALO_C - (KC - 1) + k, 32), :] * _row(cw_ref, k)
            a1_ref[pl.ds(r0, 32), :] = acc
        xh, _ = _ln(a1_ref[...])
        a2 = xh * _row(p_ref, CLG) + _row(p_ref, CLB)
        a3 = a2 * _sigmoid(a2)
        gu, _ = _gelu(zg_ref[:, :DC])
        gvg, _ = _gelu(zg_ref[:, DC:])
        vh, _ = _ln(gvg)
        gvn = (vh * _row(p_ref, GLG) + _row(p_ref, GLB)).astype(BF16)
        low = _causal_mask(True)
        first = _first_head_lanes()
        wm = [jnp.where(low, ws_ref[h], 0.0).astype(BF16) for h in range(NH)]
        for n in range(tm // CHUNK):
            for p in range(NH // 2):
                v = gvn[n * CHUNK:(n + 1) * CHUNK, p * CHUNK:(p + 1) * CHUNK]
                blk = jnp.where(first, _dot(wm[2 * p], v), _dot(wm[2 * p + 1], v))
                sp_ref[pl.ds(n * CHUNK, CHUNK), pl.ds(p * CHUNK, CHUNK)] = blk + bs_ref[:, pl.ds(p * CHUNK, CHUNK)]
        g = gu * sp_ref[...]
        an, _ = _rms(a3)
        gn, _ = _rms(g)
        mog = _row(v_ref, MOG)
        y = jnp.concatenate([an * mog[:, :DC], gn * mog[:, DC:]], axis=1).astype(BF16)
        o1 = _dot(y, wo_ref[...])
        o1_ref[...] = o1
        x2 = x_ref[...] + _row(v_ref, GT1) * o1
        x2_ref[...] = x2
        xn2, _ = _rms(x2)
        h2 = (xn2 * _row(v_ref, G2)) * (1.0 + _row(v_ref, SC2)) + _row(v_ref, SH2)
        h2_ref[...] = h2.astype(BF16)

    tile = lambda w: pl.BlockSpec((tm, w), lambda i: (i, 0))
    return pl.pallas_call(
        body, name="fwd_mid", grid=(s // tm,),
        in_specs=[tile(DC), pl.BlockSpec((HALO_C, DC), lambda i: (jnp.maximum(i * hb - 1, 0), 0)),
                  pl.BlockSpec((tm, 2 * DC), lambda i: (i, 1)), tile(D), _full((16, D)), _full((8, DC)),
                  _full((32, DC)), _full((NH, CHUNK, CHUNK)), _full((CHUNK, DC)), _full((D, D))],
        out_specs=[tile(DC), tile(DC), tile(D), tile(D), tile(D)],
        out_shape=[jax.ShapeDtypeStruct((s, DC), F32), jax.ShapeDtypeStruct((s, DC), F32),
                   jax.ShapeDtypeStruct((s, D), F32), jax.ShapeDtypeStruct((s, D), F32),
                   jax.ShapeDtypeStruct((s, D), BF16)],
        scratch_shapes=[pltpu.VMEM((tm + HALO_C, DC), F32)],
        compiler_params=_arb(),
    )(a0, a0, z, x2d, vecs, v512, conv_w, gm_ws, bs_exp, w_out_b)


def _fwd_ffn(h2, x2, target, vecs, ffn_wb, w_up_b, w_down_b, tm):
    s = x2.shape[0]
    nch = DFF // FCH

    def body(h2_ref, x2_ref, t_ref, v_ref, fw_ref, wu_hbm, wd_hbm,
             up_ref, f_ref, o2_ref, dx3_ref, acc_ref, wu, wd, carry):
        i = pl.program_id(0)

        @pl.when(i == 0)
        def _():
            pltpu.sync_copy(wu_hbm, wu)
            pltpu.sync_copy(wd_hbm, wd)
            carry[...] = jnp.zeros_like(carry)
            acc_ref[...] = jnp.zeros_like(acc_ref)

        h2v = h2_ref[...]
        o2 = jnp.zeros((tm, D), F32)
        for j in range(nch):
            conv = []
            for base in (0, DFF):
                cols = pl.ds(base + j * FCH, FCH)
                pre = _dot(h2v, wu[:, cols])
                up_ref[:, cols] = pre
                e = jnp.concatenate([carry[:, cols], pre], axis=0)
                carry[:, cols] = pre[tm - HALO_F:, :]
                conv.append(fw_ref[pl.ds(3, 1), cols]
                            + fw_ref[pl.ds(0, 1), cols] * pltpu.roll(e, 2, 0)[HALO_F:, :]
                            + fw_ref[pl.ds(1, 1), cols] * pltpu.roll(e, 1, 0)[HALO_F:, :]
                            + fw_ref[pl.ds(2, 1), cols] * pre)
            val, gate = conv
            f = ((gate * _sigmoid(gate)) * val).astype(BF16)
            f_ref[:, pl.ds(j * FCH, FCH)] = f
            o2 = o2 + _dot(f, wd[pl.ds(j * FCH, FCH), :])
        o2_ref[...] = o2
        x3 = x2_ref[...] + _row(v_ref, GT2) * o2
        xn3, r3 = _rms(x3)
        gf = _row(v_ref, GF)
        diff = xn3 * gf - t_ref[...]
        acc_ref[pl.ds(1, 1), :] += _colsum(diff * diff) * (0.5 / D)
        dout = diff * (1.0 / D)
        acc_ref[pl.ds(0, 1), :] += _colsum(dout * xn3)
        dx3_ref[...] = _rms_bwd(dout * gf, xn3, r3)

    tile = lambda w: pl.BlockSpec((tm, w), lambda i: (i, 0))
    return pl.pallas_call(
        body, name="fwd_ffn", grid=(s // tm,),
        in_specs=[tile(D), tile(D), tile(D), _full((16, D)), _full((8, 2 * DFF)), ANY, ANY],
        out_specs=[tile(2 * DFF), tile(DFF), tile(D), tile(D), _full((8, D))],
        out_shape=[jax.ShapeDtypeStruct((s, 2 * DFF), F32), jax.ShapeDtypeStruct((s, DFF), BF16),
                   jax.ShapeDtypeStruct((s, D), F32), jax.ShapeDtypeStruct((s, D), F32),
                   jax.ShapeDtypeStruct((8, D), F32)],
        scratch_shapes=[pltpu.VMEM((D, 2 * DFF), BF16), pltpu.VMEM((DFF, D), BF16),
                        pltpu.VMEM((HALO_F, 2 * DFF), F32)],
        compiler_params=_arb(),
    )(h2, x2, target, vecs, ffn_wb, w_up_b, w_down_b)


def _bwd_ffn(dx3, o2, x2, up_pre, vecs, ffn_wb, w_up_b, w_down_b, tm):
    s = x2.shape[0]
    nt = s // tm
    nch = DFF // FCH
    hb = tm // HALO_F

    def body(dx3_ref, o2_ref, x2_ref, up_ref, halo_ref, v_ref, fw_ref, wu_hbm, wd_hbm,
             dx2_ref, dup_ref, do2_ref, acc_ref, accf_ref, wu, wd, carry):
        i = pl.program_id(0)
        r = nt - 1 - i

        @pl.when(i == 0)
        def _():
            pltpu.sync_copy(wu_hbm, wu)
            pltpu.sync_copy(wd_hbm, wd)
            carry[...] = jnp.zeros_like(carry)
            acc_ref[...] = jnp.zeros_like(acc_ref)
            accf_ref[...] = jnp.zeros_like(accf_ref)

        dx3v = dx3_ref[...]
        do2 = (dx3v * _row(v_ref, GT2)).astype(BF16)
        do2_ref[...] = do2
        acc_ref[pl.ds(0, 1), :] += _colsum(dx3v * o2_ref[...])
        dh2 = jnp.zeros((tm, D), F32)
        n_e = tm + HALO_F
        for j in range(nch):
            df = _dot_nt(do2, wd[pl.ds(j * FCH, FCH), :])
            shifted, conv = [], []
            for base in (0, DFF):
                cols = pl.ds(base + j * FCH, FCH)
                pre = up_ref[:, cols]
                hl = halo_ref[:, cols]
                e = jnp.concatenate([jnp.where(r > 0, hl, jnp.zeros_like(hl)), pre], axis=0)
                p2 = pltpu.roll(e, 2, 0)[HALO_F:, :]
                p1 = pltpu.roll(e, 1, 0)[HALO_F:, :]
                shifted.append((p2, p1, pre))
                conv.append(fw_ref[pl.ds(3, 1), cols] + fw_ref[pl.ds(0, 1), cols] * p2
                            + fw_ref[pl.ds(1, 1), cols] * p1 + fw_ref[pl.ds(2, 1), cols] * pre)
            val, gate = conv
            sg = _sigmoid(gate)
            dups = (df * (gate * sg), df * val * (sg * (1.0 + gate * (1.0 - sg))))
            for (base, dup, (p2, p1, pre)) in zip((0, DFF), dups, shifted):
                cols = pl.ds(base + j * FCH, FCH)
                accf_ref[pl.ds(3, 1), cols] += _colsum(dup)
                accf_ref[pl.ds(0, 1), cols] += _colsum(dup * p2)
                accf_ref[pl.ds(1, 1), cols] += _colsum(dup * p1)
                accf_ref[pl.ds(2, 1), cols] += _colsum(dup * pre)
                e = jnp.concatenate([dup, carry[:, cols]], axis=0)
                carry[:, cols] = dup[:HALO_F, :]
                dpre = (fw_ref[pl.ds(0, 1), cols] * pltpu.roll(e, n_e - 2, 0)[:tm, :]
                        + fw_ref[pl.ds(1, 1), cols] * pltpu.roll(e, n_e - 1, 0)[:tm, :]
                        + fw_ref[pl.ds(2, 1), cols] * dup).astype(BF16)
                dup_ref[:, cols] = dpre
                dh2 = dh2 + _dot_nt(dpre, wu[:, cols])
        xn2, r2 = _rms(x2_ref[...])
        g2 = _row(v_ref, G2)
        sc = 1.0 + _row(v_ref, SC2)
        acc_ref[pl.ds(1, 1), :] += _colsum(dh2)
        acc_ref[pl.ds(2, 1), :] += _colsum(dh2 * (xn2 * g2))
        acc_ref[pl.ds(3, 1), :] += _colsum(dh2 * sc * xn2)
        dx2_ref[...] = dx3v + _rms_bwd(dh2 * sc * g2, xn2, r2)

    tile = lambda w: pl.BlockSpec((tm, w), lambda i: (nt - 1 - i, 0))
    return pl.pallas_call(
        body, name="bwd_ffn", grid=(nt,),
        in_specs=[tile(D), tile(D), tile(D), tile(2 * DFF),
                  pl.BlockSpec((HALO_F, 2 * DFF), lambda i: (jnp.maximum((nt - 1 - i) * hb - 1, 0), 0)),
                  _full((16, D)), _full((8, 2 * DFF)), ANY, ANY],
        out_specs=[tile(D), tile(2 * DFF), tile(D), _full((8, D)), _full((8, 2 * DFF))],
        out_shape=[jax.ShapeDtypeStruct((s, D), F32), jax.ShapeDtypeStruct((s, 2 * DFF), BF16),
                   jax.ShapeDtypeStruct((s, D), BF16), jax.ShapeDtypeStruct((8, D), F32),
                   jax.ShapeDtypeStruct((8, 2 * DFF), F32)],
        scratch_shapes=[pltpu.VMEM((D, 2 * DFF), BF16), pltpu.VMEM((DFF, D), BF16),
                        pltpu.VMEM((HALO_F, 2 * DFF), F32)],
        compiler_params=_arb(),
    )(dx3, o2, x2, up_pre, up_pre, vecs, ffn_wb, w_up_b, w_down_b)


def _bwd_mid(dx2, x2d, o1, z, a0, a1, sp, vecs, v512, conv_w, gm_ws, gm_ws_t, w_out_b, w_in_b, tm):
    s = x2d.shape[0]
    nt = s // tm
    hb = tm // HALO_C

    def body(dx2_ref, x_ref, o1_ref, z_ref, a0_ref, halo_ref, a1_ref, sp_ref, v_ref, p_ref, cw_ref, ws_ref, wst_ref,
             wo_ref, wi_ref, gx_ref, dz_ref, y_ref, do1_ref, acc_ref, accp_ref, dcw_ref, dws_ref, dbs_ref, dbst_ref,
             exta, extd, dsp_s, dgvn_s):
        i = pl.program_id(0)
        r = nt - 1 - i

        @pl.when(i == 0)
        def _():
            extd[...] = jnp.zeros_like(extd)
            for ref in (acc_ref, accp_ref, dcw_ref, dws_ref, dbs_ref, dbst_ref):
                ref[...] = jnp.zeros_like(ref)

        dx2v = dx2_ref[...]
        do1 = (dx2v * _row(v_ref, GT1)).astype(BF16)
        do1_ref[...] = do1
        acc_ref[pl.ds(0, 1), :] += _colsum(dx2v * o1_ref[...])
        dy = _dot_nt(do1, wo_ref[...])
        mog = _row(v_ref, MOG)

        xh, rstd = _ln(a1_ref[...])
        clg = _row(p_ref, CLG)
        a2 = xh * clg + _row(p_ref, CLB)
        s2 = _sigmoid(a2)
        a3 = a2 * s2
        an, ra = _rms(a3)
        dya = dy[:, :DC]
        da3 = _rms_bwd(dya * mog[:, :DC], an, ra)
        da2 = da3 * (s2 * (1.0 + a2 * (1.0 - s2)))
        accp_ref[pl.ds(CLB, 1), :] += _colsum(da2)
        accp_ref[pl.ds(CLG, 1), :] += _colsum(da2 * xh)
        da1 = _ln_bwd(da2 * clg, xh, rstd)
        accp_ref[pl.ds(CB, 1), :] += _colsum(da1)
        extd[pl.ds(tm, HALO_C), :] = extd[pl.ds(0, HALO_C), :]
        extd[pl.ds(0, tm), :] = da1
        halo = halo_ref[...]
        exta[pl.ds(0, HALO_C), :] = jnp.where(r > 0, halo, jnp.zeros_like(halo))
        exta[pl.ds(HALO_C, tm), :] = a0_ref[...]
        for k in range(KC):
            dcw_ref[pl.ds(k, 1), :] += _colsum(da1 * exta[pl.ds(HALO_C - (KC - 1) + k, tm), :])
        ca = z_ref[:, :DC]
        sg = _sigmoid(z_ref[:, DC:2 * DC])
        for r0 in range(0, tm, 32):
            acc = jnp.zeros((32, DC), F32)
            for j in range(KC):
                acc = acc + extd[pl.ds(r0 + j, 32), :] * _row(cw_ref, KC - 1 - j)
            sgb = sg[r0:r0 + 32, :]
            dz_ref[pl.ds(r0, 32), pl.ds(0, DC)] = (acc * sgb).astype(BF16)
            dz_ref[pl.ds(r0, 32), pl.ds(DC, DC)] = (acc * ca[r0:r0 + 32, :] * sgb * (1.0 - sgb)).astype(BF16)

        gu_pre = z_ref[:, 2 * DC:3 * DC]
        gv_pre = z_ref[:, 3 * DC:]
        gu, tu = _gelu(gu_pre)
        gvg, tv = _gelu(gv_pre)
        vh, vrstd = _ln(gvg)
        glg = _row(p_ref, GLG)
        gvn = (vh * glg + _row(p_ref, GLB)).astype(BF16)
        spv = sp_ref[...]
        g = gu * spv
        gn, rg = _rms(g)
        y_ref[...] = jnp.concatenate([an * mog[:, :DC], gn * mog[:, DC:]], axis=1).astype(BF16)
        acc_ref[pl.ds(4, 1), :] += jnp.concatenate([_colsum(dya * an), _colsum(dy[:, DC:] * gn)], axis=1)
        dg = _rms_bwd(dy[:, DC:] * mog[:, DC:], gn, rg)
        dz_ref[:, pl.ds(2 * DC, DC)] = (dg * spv * _gelu_grad(gu_pre, tu)).astype(BF16)
        dsp_s[...] = dg * gu
        upper = _causal_mask(False)
        first = _first_head_lanes()
        wmt = [jnp.where(upper, wst_ref[h], 0.0).astype(BF16) for h in range(NH)]
        for n in range(tm // CHUNK):
            rows = pl.ds(n * CHUNK, CHUNK)
            for p in range(NH // 2):
                cols = pl.ds(p * CHUNK, CHUNK)
                dsp = dsp_s[rows, cols]
                dbs_ref[:, cols] += dsp
                da = jnp.where(first, dsp, 0.0).astype(BF16)
                db = jnp.where(first, 0.0, dsp).astype(BF16)
                v = gvn[n * CHUNK:(n + 1) * CHUNK, p * CHUNK:(p + 1) * CHUNK]
                dws_ref[2 * p] += _dot_nt(da, v)
                dws_ref[2 * p + 1] += _dot_nt(db, v)
                dgvn_s[rows, cols] = _dot(wmt[2 * p], da) + _dot(wmt[2 * p + 1], db)
        dgvn = dgvn_s[...]
        accp_ref[pl.ds(GLB, 1), :] += _colsum(dgvn)
        accp_ref[pl.ds(GLG, 1), :] += _colsum(dgvn * vh)
        dgvg = _ln_bwd(dgvn * glg, vh, vrstd)
        dz_ref[:, pl.ds(3 * DC, DC)] = (dgvg * _gelu_grad(gv_pre, tv)).astype(BF16)

        dh1 = _dot_nt(dz_ref[...], wi_ref[...])
        xn, r1 = _rms(x_ref[...])
        g1 = _row(v_ref, G1)
        sc = 1.0 + _row(v_ref, SC1)
        acc_ref[pl.ds(1, 1), :] += _colsum(dh1)
        acc_ref[pl.ds(2, 1), :] += _colsum(dh1 * (xn * g1))
        acc_ref[pl.ds(3, 1), :] += _colsum(dh1 * sc * xn)
        gx_ref[...] = dx2v + _rms_bwd(dh1 * sc * g1, xn, r1)

        @pl.when(i == nt - 1)
        def _():
            low = _causal_mask(True)
            for h in range(NH):
                dws_ref[h] = jnp.where(low, dws_ref[h], 0.0)
            lane = lax.broadcasted_iota(jnp.int32, (CHUNK, CHUNK), 1)
            out = jnp.zeros((CHUNK, CHUNK), F32)
            for h in range(NH):
                hs = jnp.sum(dbs_ref[:, pl.ds((h // 2) * CHUNK, CHUNK)]
                             * ((lane >= (h % 2) * HD) & (lane < (h % 2 + 1) * HD)).astype(F32),
                             axis=1, keepdims=True)
                out = jnp.where(lane == h, hs, out)
            dbst_ref[...] = out

    tile = lambda w: pl.BlockSpec((tm, w), lambda i: (nt - 1 - i, 0))
    return pl.pallas_call(
        body, name="bwd_mid", grid=(nt,),
        in_specs=[tile(D), tile(D), tile(D), tile(4 * DC), tile(DC),
                  pl.BlockSpec((HALO_C, DC), lambda i: (jnp.maximum((nt - 1 - i) * hb - 1, 0), 0)),
                  tile(DC), tile(DC), _full((16, D)), _full((8, DC)), _full((32, DC)),
                  _full((NH, CHUNK, CHUNK)), _full((NH, CHUNK, CHUNK)), _full((D, D)), _full((D, 4 * DC))],
        out_specs=[tile(D), tile(4 * DC), tile(D), tile(D), _full((8, D)), _full((8, DC)), _full((32, DC)),
                   _full((NH, CHUNK, CHUNK)), _full((CHUNK, DC)), _full((CHUNK, CHUNK))],
        out_shape=[jax.ShapeDtypeStruct((s, D), F32), jax.ShapeDtypeStruct((s, 4 * DC), BF16),
                   jax.ShapeDtypeStruct((s, D), BF16), jax.ShapeDtypeStruct((s, D), BF16),
                   jax.ShapeDtypeStruct((8, D), F32), jax.ShapeDtypeStruct((8, DC), F32),
                   jax.ShapeDtypeStruct((32, DC), F32), jax.ShapeDtypeStruct((NH, CHUNK, CHUNK), F32),
                   jax.ShapeDtypeStruct((CHUNK, DC), F32), jax.ShapeDtypeStruct((CHUNK, CHUNK), F32)],
        scratch_shapes=[pltpu.VMEM((tm + HALO_C, DC), F32), pltpu.VMEM((tm + HALO_C, DC), F32),
                        pltpu.VMEM((tm, DC), F32), pltpu.VMEM((tm, DC), F32)],
        compiler_params=_arb(),
    )(dx2, x2d, o1, z, a0, a0, a1, sp, vecs, v512, conv_w, gm_ws, gm_ws_t, w_out_b, w_in_b)


def _mm_tn(name, a, b, tm, tn, tk):
    s, k1 = a.shape
    k2 = b.shape[1]

    def body(a_ref, b_ref, o_ref):
        @pl.when(pl.program_id(2) == 0)
        def _():
            o_ref[...] = jnp.zeros_like(o_ref)

        o_ref[...] += _dot_tn(a_ref[...], b_ref[...])

    return pl.pallas_call(
        body, name=name, grid=(k1 // tm, k2 // tn, s // tk),
        in_specs=[pl.BlockSpec((tk, tm), lambda i, j, k: (k, i)), pl.BlockSpec((tk, tn), lambda i, j, k: (k, j))],
        out_specs=pl.BlockSpec((tm, tn), lambda i, j, k: (i, j)),
        out_shape=jax.ShapeDtypeStruct((k1, k2), F32),
        compiler_params=pltpu.CompilerParams(dimension_semantics=("parallel", "parallel", "arbitrary")),
    )(a, b)


def _adam_math(w, g, m, v):
    m = ADAM_B1 * m + (1.0 - ADAM_B1) * g
    v = ADAM_B2 * v + (1.0 - ADAM_B2) * (g * g)
    m_hat = m / (1.0 - ADAM_B1 ** ADAM_STEP)
    v_hat = v / (1.0 - ADAM_B2 ** ADAM_STEP)
    delta = -ADAM_LR * (m_hat / (jnp.sqrt(v_hat) + ADAM_EPS) + ADAM_WD * w)
    return delta, m, v


def _row_block(rows, cap=2048):
    tr = rows
    while tr > cap and tr % 2 == 0:
        tr //= 2
    return tr


def _adam(name, w, g, m, v):
    rows = w.shape[0]
    tr = _row_block(rows)

    def body(w_ref, g_ref, m_ref, v_ref, d_ref, mo_ref, vo_ref):
        d_ref[...], mo_ref[...], vo_ref[...] = _adam_math(w_ref[...], g_ref[...], m_ref[...], v_ref[...])

    spec = pl.BlockSpec((tr, 128), lambda i: (i, 0))
    return pl.pallas_call(
        body, name=name, grid=(rows // tr,), in_specs=[spec] * 4, out_specs=[spec] * 3,
        out_shape=[jax.ShapeDtypeStruct((rows, 128), F32)] * 3, compiler_params=_arb(),
    )(w, g, m, v)


def _sum_adam(name, parts, w, m, v):
    n, rows, _ = parts.shape
    tr = _row_block(rows)

    def body(p_ref, w_ref, m_ref, v_ref, g_ref, d_ref, mo_ref, vo_ref):
        g = p_ref[0]
        for k in range(1, n):
            g = g + p_ref[k]
        g_ref[...] = g
        d_ref[...], mo_ref[...], vo_ref[...] = _adam_math(w_ref[...], g, m_ref[...], v_ref[...])

    spec = pl.BlockSpec((tr, 128), lambda i: (i, 0))
    return pl.pallas_call(
        body, name=name, grid=(rows // tr,),
        in_specs=[pl.BlockSpec((n, tr, 128), lambda i: (0, i, 0))] + [spec] * 3, out_specs=[spec] * 4,
        out_shape=[jax.ShapeDtypeStruct((rows, 128), F32)] * 4, compiler_params=_arb(),
    )(parts, w, m, v)


def _sum_parts(name, parts):
    n, rows, _ = parts.shape
    tr = _row_block(rows)

    def body(p_ref, o_ref):
        g = p_ref[0]
        for k in range(1, n):
            g = g + p_ref[k]
        o_ref[...] = g

    return pl.pallas_call(
        body, name=name, grid=(rows // tr,),
        in_specs=[pl.BlockSpec((n, tr, 128), lambda i: (0, i, 0))],
        out_specs=pl.BlockSpec((tr, 128), lambda i: (i, 0)),
        out_shape=jax.ShapeDtypeStruct((rows, 128), F32), compiler_params=_arb(),
    )(parts)


def _pair_add(name, g4, recv, core):
    _, _, rows, _ = g4.shape
    tr = _row_block(rows)

    def body(c_ref, a_ref, b_ref, o_ref):
        o_ref[...] = a_ref[...] + b_ref[...]

    return pl.pallas_call(
        body, name=name,
        grid_spec=pltpu.PrefetchScalarGridSpec(
            num_scalar_prefetch=1, grid=(4, rows // tr),
            in_specs=[pl.BlockSpec((None, None, tr, 128), lambda k, i, c_ref: (k, c_ref[0], i, 0)),
                      pl.BlockSpec((None, tr, 128), lambda k, i, c_ref: (k, i, 0))],
            out_specs=pl.BlockSpec((None, tr, 128), lambda k, i, c_ref: (k, i, 0))),
        out_shape=jax.ShapeDtypeStruct((4, rows, 128), F32), compiler_params=_arb(2),
    )(core, g4, recv)


def _pack(pieces, dtype=F32, align=1024):
    flat, offs, pos = [], [], 0
    for p in pieces:
        n = p.size
        padded = -(-n // align) * align
        f = p.reshape(-1).astype(dtype)
        if padded != n:
            f = jnp.concatenate([f, jnp.zeros((padded - n,), dtype)])
        flat.append(f)
        offs.append((pos, n, p.shape))
        pos += padded
    return jnp.concatenate(flat).reshape(-1, 128), offs


def _unpack(pack, offs, lead=()):
    flat = pack.reshape(lead + (-1,))
    return [flat[..., pos:pos + n].reshape(lead + shape) for pos, n, shape in offs]


def kernel(x, c, w_ada, b_ada, norm1_gain, w_in, conv_dw_w, conv_dw_b, conv_ln_g, conv_ln_b, gm_ln_g, gm_ln_b, gm_ws, gm_bs, mix_out_gain, w_out, norm2_gain, w_up, ffn_dw_w, ffn_dw_b, w_down, final_gain, loss_target, m_w_ada, m_b_ada, m_norm1_gain, m_w_in, m_conv_dw_w, m_conv_dw_b, m_conv_ln_g, m_conv_ln_b, m_gm_ln_g, m_gm_ln_b, m_gm_ws, m_gm_bs, m_mix_out_gain, m_w_out, m_norm2_gain, m_w_up, m_ffn_dw_w, m_ffn_dw_b, m_w_down, m_final_gain, v_w_ada, v_b_ada, v_norm1_gain, v_w_in, v_conv_dw_w, v_conv_dw_b, v_conv_ln_g, v_conv_ln_b, v_gm_ln_g, v_gm_ln_b, v_gm_ws, v_gm_bs, v_mix_out_gain, v_w_out, v_norm2_gain, v_w_up, v_ffn_dw_w, v_ffn_dw_b, v_w_down, v_final_gain):
    s = x.shape[1]
    ax, ay, ac = _place()
    me = 4 * ax + 2 * ay + ac
    n_ada = w_ada.shape[2]
    n_in = w_in.shape[2]
    n_cw = conv_dw_w.shape[2]
    n_up = w_up.shape[2]
    r_out = w_out.shape[1]
    r_down = w_down.shape[1]
    x2d = x[0]
    target = loss_target[0]

    small_in, small_offs = _pack([c[0], conv_dw_w[0], ffn_dw_w[0]])
    small_all = _all_gather("gather_small", small_in)
    c_all, cw_all, fw_all = _unpack(small_all, small_offs, lead=(NDEV,))
    conv_w = jnp.transpose(cw_all, (1, 0, 2)).reshape(KC, DC)
    conv_w = jnp.concatenate([conv_w, jnp.zeros((32 - KC, DC), F32)], axis=0)
    ffn_w = jnp.transpose(fw_all, (1, 0, 2)).reshape(KF, 2 * DFF)
    ffn_wb = jnp.concatenate([ffn_w, ffn_dw_b, jnp.zeros((8 - KF - 1, 2 * DFF), F32)], axis=0)

    b_cols = lax.dynamic_slice(b_ada, (0, me * n_ada), (1, n_ada))
    mod_all = _all_gather("gather_mod", _mod_part(c_all, w_ada[0], b_cols))
    mod = lax.dynamic_index_in_dim(mod_all, me, axis=1, keepdims=False).reshape(6, D)
    sh1, sc1, gt1, sh2, sc2, gt2 = [mod[k:k + 1] for k in range(6)]
    vecs = jnp.concatenate([norm1_gain, sh1, sc1, gt1, norm2_gain, sh2, sc2, gt2, mix_out_gain,
                            final_gain.reshape(1, D), jnp.zeros((6, D), F32)], axis=0)
    v512 = jnp.concatenate([conv_dw_b, conv_ln_g, conv_ln_b, gm_ln_g, gm_ln_b, jnp.zeros((3, DC), F32)], axis=0)
    bs_exp = jnp.repeat(jnp.transpose(gm_bs[0]), HD, axis=1)
    gm_ws_t = jnp.swapaxes(gm_ws[0], 1, 2)

    big_in, big_offs = _pack([w_in[0], w_out[0], w_up[0], w_down[0]], dtype=BF16, align=2048)
    big_all = _all_gather("gather_weights", big_in)
    wi_all, wo_all, wu_all, wd_all = _unpack(big_all, big_offs, lead=(NDEV,))
    w_in_b = jnp.transpose(wi_all, (1, 0, 2)).reshape(D, NDEV * n_in)
    w_out_b = wo_all.reshape(NDEV * r_out, D)
    w_up_b = jnp.transpose(wu_all, (1, 0, 2)).reshape(D, NDEV * n_up)
    w_down_b = wd_all.reshape(NDEV * r_down, D)

    tm_in = min(512, s)
    tm = min(256, s)
    z, a0, h1 = _fwd_in(x2d, vecs, w_in_b, tm_in)
    a1, sp, x2, o1, h2 = _fwd_mid(a0, z, x2d, vecs, v512, conv_w, gm_ws[0], bs_exp, w_out_b, tm)
    up_pre, f, o2, dx3, acc_f = _fwd_ffn(h2, x2, target, vecs, ffn_wb, w_up_b, w_down_b, tm)
    loss = lax.psum(jnp.sum(acc_f[1]), ("x", "y", "c"))

    dx2, dup_pre, do2, acc_b, acc_fw = _bwd_ffn(dx3, o2, x2, up_pre, vecs, ffn_wb, w_up_b, w_down_b, tm)
    gx, dz, y, do1, acc_m, acc_p, dcw, dws, _, dbs_t = _bwd_mid(
        dx2, x2d, o1, z, a0, a1, sp, vecs, v512, conv_w, gm_ws[0], gm_ws_t, w_out_b, w_in_b, tm)
    tk = min(512, s)
    dw_in = _mm_tn("dw_in", h1, dz, D, 4 * DC, tk)
    dw_out = _mm_tn("dw_out", y, do1, D, D, tk)
    dw_up = _mm_tn("dw_up", h2, dup_pre, D, 2 * DFF // 4, tk)
    dw_down = _mm_tn("dw_down", f, do2, DFF // 2, D, tk)

    dmod = jnp.concatenate([acc_m[1], acc_m[2], acc_m[0], acc_b[1], acc_b[2], acc_b[0]])
    small_g, sg_offs = _pack([
        dmod, acc_m[3], acc_p[CB], acc_p[CLG], acc_p[CLB], acc_p[GLG], acc_p[GLB], dws,
        jnp.transpose(dbs_t[:, :NH]), acc_m[4], acc_b[3], acc_fw[3], acc_f[0], dcw[:KC], acc_fw[:KF]])
    small_parts = _all_gather("gather_small_grads", small_g)
    dmod_all = small_parts[:, :6 * D // 128].reshape(NDEV, 6 * D)
    small_sum = _sum_parts("sum_small_grads", small_parts)
    (g_b_ada, g_n1, g_cb, g_clg, g_clb, g_glg, g_glb, g_ws, g_bs, g_mog, g_n2, g_fb, g_gf, g_cw, g_fw) = _unpack(
        small_sum, sg_offs)
    g_cw_mine = lax.dynamic_slice(g_cw, (0, me * n_cw), (KC, n_cw))
    g_fw_mine = lax.dynamic_slice(g_fw, (0, me * n_up), (KF, n_up))
    small_names = [
        (b_ada, m_b_ada, v_b_ada, g_b_ada), (norm1_gain, m_norm1_gain, v_norm1_gain, g_n1),
        (conv_dw_b, m_conv_dw_b, v_conv_dw_b, g_cb), (conv_ln_g, m_conv_ln_g, v_conv_ln_g, g_clg),
        (conv_ln_b, m_conv_ln_b, v_conv_ln_b, g_clb), (gm_ln_g, m_gm_ln_g, v_gm_ln_g, g_glg),
        (gm_ln_b, m_gm_ln_b, v_gm_ln_b, g_glb), (gm_ws, m_gm_ws, v_gm_ws, g_ws), (gm_bs, m_gm_bs, v_gm_bs, g_bs),
        (mix_out_gain, m_mix_out_gain, v_mix_out_gain, g_mog), (norm2_gain, m_norm2_gain, v_norm2_gain, g_n2),
        (ffn_dw_b, m_ffn_dw_b, v_ffn_dw_b, g_fb), (final_gain, m_final_gain, v_final_gain, g_gf),
        (conv_dw_w, m_conv_dw_w, v_conv_dw_w, g_cw_mine), (ffn_dw_w, m_ffn_dw_w, v_ffn_dw_w, g_fw_mine)]
    sw, s_offs = _pack([t[0] for t in small_names])
    sm, _ = _pack([t[1] for t in small_names])
    sv, _ = _pack([t[2] for t in small_names])
    sgr, _ = _pack([t[3] for t in small_names])
    s_delta, s_m, s_v = _adam("adam_small", sw, sgr, sm, sv)
    s_grads = [t[3].reshape(t[0].shape) for t in small_names]
    s_deltas, s_ms, s_vs = _unpack(s_delta, s_offs), _unpack(s_m, s_offs), _unpack(s_v, s_offs)

    dm_cols = lax.dynamic_slice(dmod_all, (0, me * n_ada), (NDEV, n_ada))
    g_ada = _ada_grad(jnp.transpose(c_all), dm_cols)
    a_delta, a_m, a_v = _adam("adam_ada", w_ada.reshape(-1, 128), g_ada.reshape(-1, 128),
                              m_w_ada.reshape(-1, 128), v_w_ada.reshape(-1, 128))

    slots = jnp.concatenate([
        jnp.transpose(dw_in.reshape(D, NDEV, n_in), (1, 0, 2)).reshape(NDEV, -1, 128),
        dw_out.reshape(NDEV, -1, 128),
        jnp.transpose(dw_up.reshape(D, NDEV, n_up), (1, 0, 2)).reshape(NDEV, -1, 128),
        dw_down.reshape(NDEV, -1, 128)], axis=1)
    rows = slots.shape[1]
    g4 = slots.reshape(4, 2, rows, 128)
    from_sibling = _sibling_swap("rs_sibling", g4)
    pair = _pair_add("rs_pair_add", g4, from_sibling, ac.reshape(1).astype(jnp.int32))
    from_chips = _chip_exchange("rs_chips", pair)
    bw, b_offs = _pack([w_in[0], w_out[0], w_up[0], w_down[0]], align=2048)
    bm, _ = _pack([m_w_in[0], m_w_out[0], m_w_up[0], m_w_down[0]], align=2048)
    bv, _ = _pack([v_w_in[0], v_w_out[0], v_w_up[0], v_w_down[0]], align=2048)
    b_grad, b_delta, b_m, b_v = _sum_adam("rs_sum_adam", from_chips, bw, bm, bv)
    b_grads, b_deltas, b_ms, b_vs = [_unpack(t, b_offs) for t in (b_grad, b_delta, b_m, b_v)]

    def ordered(ada, big, small):
        (s_b_ada, s_n1, s_cb, s_clg, s_clb, s_glg, s_glb, s_ws, s_bs, s_mog, s_n2, s_fb, s_gf, s_cw, s_fw) = small
        b_in, b_out, b_up, b_down = big
        return [ada.reshape(w_ada.shape), s_b_ada.reshape(b_ada.shape), s_n1.reshape(norm1_gain.shape),
                b_in.reshape(w_in.shape), s_cw.reshape(conv_dw_w.shape), s_cb.reshape(conv_dw_b.shape),
                s_clg.reshape(conv_ln_g.shape), s_clb.reshape(conv_ln_b.shape), s_glg.reshape(gm_ln_g.shape),
                s_glb.reshape(gm_ln_b.shape), s_ws.reshape(gm_ws.shape), s_bs.reshape(gm_bs.shape),
                s_mog.reshape(mix_out_gain.shape), b_out.reshape(w_out.shape), s_n2.reshape(norm2_gain.shape),
                b_up.reshape(w_up.shape), s_fw.reshape(ffn_dw_w.shape), s_fb.reshape(ffn_dw_b.shape),
                b_down.reshape(w_down.shape), s_gf.reshape(final_gain.shape)]

    grads = ordered(g_ada, b_grads, s_grads)
    deltas = ordered(a_delta, b_deltas, s_deltas)
    new_m = ordered(a_m, b_ms, s_ms)
    new_v = ordered(a_v, b_vs, s_vs)
    return (loss, gx.reshape(x.shape), *grads, *deltas, *new_m, *new_v)
```

```python
import jax
import jax.numpy as jnp
from jax import lax
from jax.experimental import pallas as pl
from jax.experimental.pallas import tpu as pltpu

F32 = jnp.float32
BF16 = jnp.bfloat16
NDEV = 8
D = 1024
DC = 512
DFF = 2816
NSH = 704
PSH = 768
PFF = 4 * PSH
KC = 31
KF = 3
CHUNK = 128
NH = 8
HD = 64
HALO_C = 32
HALO_F = 8
LANES = 128
RMS_EPS = 1e-6
LN_EPS = 1e-5
ADAM_LR = 0.001
ADAM_B1 = 0.9
ADAM_B2 = 0.999
ADAM_EPS = 1e-08
ADAM_WD = 0.01
ADAM_STEP = 10
GELU_K = 0.7978845608028654
GELU_C = 0.044715

MESH = pl.DeviceIdType.MESH
ANY = pl.BlockSpec(memory_space=pl.ANY)

G1, SH1, SC1, GT1, G2, SH2, SC2, GT2, MOG, GF = range(10)
CB, CLG, CLB, GLG, GLB = range(5)


def _full(shape):
    return pl.BlockSpec(shape, lambda *_: (0,) * len(shape))


def _arb(n=1):
    return pltpu.CompilerParams(dimension_semantics=("arbitrary",) * n)


def _row(ref, r):
    return ref[pl.ds(r, 1), :]


def _colsum(v):
    return jnp.sum(v, axis=0, keepdims=True)


def _rowmean(v):
    return jnp.mean(v, axis=-1, keepdims=True)


def _rms(x):
    r = lax.rsqrt(_rowmean(x * x) + RMS_EPS)
    return x * r, r


def _rms_bwd(dxn, xn, r):
    return r * (dxn - xn * _rowmean(dxn * xn))


def _ln(x):
    mu = _rowmean(x)
    xc = x - mu
    rstd = lax.rsqrt(_rowmean(xc * xc) + LN_EPS)
    return xc * rstd, rstd


def _ln_bwd(dxh, xhat, rstd):
    return rstd * (dxh - _rowmean(dxh) - xhat * _rowmean(dxh * xhat))


def _sigmoid(x):
    return 0.5 * jnp.tanh(0.5 * x) + 0.5


def _gelu(x):
    t = jnp.tanh(GELU_K * (x + GELU_C * x * x * x))
    return 0.5 * x * (1.0 + t), t


def _gelu_grad(x, t):
    return 0.5 * (1.0 + t) + 0.5 * x * (1.0 - t * t) * (GELU_K * (1.0 + 3.0 * GELU_C * x * x))


def _dot(a, b):
    return jnp.dot(a, b, preferred_element_type=F32)


def _dot_nt(a, b):
    return lax.dot_general(a, b, (((1,), (1,)), ((), ())), preferred_element_type=F32)


def _dot_tn(a, b):
    return lax.dot_general(a, b, (((0,), (0,)), ((), ())), preferred_element_type=F32)


def _shift_up(e, s):
    n = e.shape[0]
    return pltpu.roll(e, (n - s) % n, 0)


def _place():
    return lax.axis_index("x"), lax.axis_index("y"), lax.axis_index("c")


def _all_gather(name, xs):
    n = len(xs)

    def body(*refs):
        x_refs, out_refs = refs[:n], refs[n:2 * n]
        send_sems, recv_sems, local_sems = refs[2 * n:]
        x, y, c = _place()
        me, sibling = (x, y, c), (x, y, 1 - c)
        chips = [(1 - x, y), (x, 1 - y), (1 - x, 1 - y)]

        def copy(a, k, block, to, own=False):
            px, py, pc = block
            slot = out_refs[a].at[4 * px + 2 * py + pc]
            return pltpu.make_async_remote_copy(
                src_ref=x_refs[a] if own else slot, dst_ref=slot,
                send_sem=send_sems.at[7 * a + k], recv_sem=recv_sems.at[7 * a + k], device_id=to, device_id_type=MESH)

        mine = [pltpu.make_async_copy(x_refs[a], out_refs[a].at[4 * x + 2 * y + c], local_sems.at[a]) for a in range(n)]
        for cp in mine:
            cp.start()
        first = []
        for a in range(n):
            first.append(copy(a, 0, me, sibling, own=True))
            first += [copy(a, 1 + j, me, (*chip, c), own=True) for j, chip in enumerate(chips)]
        for cp in first:
            cp.start()
        passed = []
        for j, chip in enumerate(chips):
            for a in range(n):
                copy(a, 1 + j, (*chip, c), me).wait_recv()
                cp = copy(a, 4 + j, (*chip, c), sibling)
                cp.start()
                passed.append(cp)
        for a in range(n):
            copy(a, 0, sibling, me).wait_recv()
            for j, chip in enumerate(chips):
                copy(a, 4 + j, (*chip, 1 - c), me).wait_recv()
        for cp in first + passed:
            cp.wait_send()
        for cp in mine:
            cp.wait()

    return pl.pallas_call(
        body, name=name, out_shape=[jax.ShapeDtypeStruct((NDEV,) + a.shape, a.dtype) for a in xs],
        in_specs=[ANY] * n, out_specs=[ANY] * n,
        scratch_shapes=[pltpu.SemaphoreType.DMA((7 * n,)), pltpu.SemaphoreType.DMA((7 * n,)),
                        pltpu.SemaphoreType.DMA((n,))],
    )(*xs)


def _sibling_swap(name, g4s):
    n = len(g4s)

    def body(*refs):
        g_refs, out_refs = refs[:n], refs[n:2 * n]
        send_sems, recv_sems = refs[2 * n:]
        x, y, c = _place()
        cps = [pltpu.make_async_remote_copy(
            src_ref=g_refs[a].at[k, 1 - c], dst_ref=out_refs[a].at[k],
            send_sem=send_sems.at[4 * a + k], recv_sem=recv_sems.at[4 * a + k],
            device_id=(x, y, 1 - c), device_id_type=MESH) for a in range(n) for k in range(4)]
        for cp in cps:
            cp.start()
        for cp in cps:
            cp.wait()

    return pl.pallas_call(
        body, name=name, out_shape=[jax.ShapeDtypeStruct((4,) + g.shape[2:], g.dtype) for g in g4s],
        in_specs=[ANY] * n, out_specs=[ANY] * n,
        scratch_shapes=[pltpu.SemaphoreType.DMA((4 * n,)), pltpu.SemaphoreType.DMA((4 * n,))],
    )(*g4s)


def _chip_exchange(name, hs):
    n = len(hs)

    def body(*refs):
        h_refs, out_refs = refs[:n], refs[n:2 * n]
        send_sems, recv_sems, local_sems = refs[2 * n:]
        x, y, c = _place()
        mychip = 2 * x + y
        chips = [(1 - x, y), (x, 1 - y), (1 - x, 1 - y)]

        def copy(a, j, arriving):
            px, py = chips[j]
            theirs = 2 * px + py
            return pltpu.make_async_remote_copy(
                src_ref=h_refs[a].at[mychip if arriving else theirs],
                dst_ref=out_refs[a].at[theirs if arriving else mychip],
                send_sem=send_sems.at[3 * a + j], recv_sem=recv_sems.at[3 * a + j],
                device_id=(px, py, c), device_id_type=MESH)

        mine = [pltpu.make_async_copy(h_refs[a].at[mychip], out_refs[a].at[mychip], local_sems.at[a]) for a in range(n)]
        for cp in mine:
            cp.start()
        cps = [copy(a, j, False) for a in range(n) for j in range(3)]
        for cp in cps:
            cp.start()
        for a in range(n):
            for j in range(3):
                copy(a, j, True).wait_recv()
        for cp in cps:
            cp.wait_send()
        for cp in mine:
            cp.wait()

    return pl.pallas_call(
        body, name=name, out_shape=[jax.ShapeDtypeStruct(h.shape, h.dtype) for h in hs],
        in_specs=[ANY] * n, out_specs=[ANY] * n,
        scratch_shapes=[pltpu.SemaphoreType.DMA((3 * n,)), pltpu.SemaphoreType.DMA((3 * n,)),
                        pltpu.SemaphoreType.DMA((n,))],
    )(*hs)


def _mod_part(c_all, w_ada, b_cols):
    ncol = w_ada.shape[2]

    def body(c_ref, w_ref, b_ref, o_ref):
        cv = c_ref[:, 0, :]
        ca = cv * _sigmoid(cv)
        o_ref[...] = _dot(ca.astype(BF16), w_ref[0].astype(BF16)) + b_ref[...]

    return pl.pallas_call(body, name="mod_part", out_shape=jax.ShapeDtypeStruct((NDEV, ncol), F32))(
        c_all, w_ada, b_cols)


def _ada_grad(c_all_t, dmod_cols):
    ncol = dmod_cols.shape[1]

    def body(ct_ref, dm_ref, o_ref):
        ct = ct_ref[...]
        ca = ct * _sigmoid(ct)
        acc = jnp.zeros((D, ncol), F32)
        for b in range(NDEV):
            acc = acc + ca[:, b:b + 1] * dm_ref[pl.ds(b, 1), :]
        o_ref[0] = acc

    return pl.pallas_call(body, name="ada_grad", out_shape=jax.ShapeDtypeStruct((1, D, ncol), F32))(
        c_all_t, dmod_cols)


def _fwd_in(x2d, vecs, w_in_g, tm):
    s = x2d.shape[0]
    nc = w_in_g.shape[2]

    def body(x_ref, v_ref, w_ref, z_ref, a0_ref, h1_ref):
        xn, _ = _rms(x_ref[...])
        h = (xn * _row(v_ref, G1)) * (1.0 + _row(v_ref, SC1)) + _row(v_ref, SH1)
        hb = h.astype(BF16)
        h1_ref[...] = hb
        for d in range(NDEV):
            z_ref[:, pl.ds(d * nc, nc)] = _dot(hb, w_ref[d])
        a0_ref[...] = z_ref[:, :DC] * _sigmoid(z_ref[:, DC:2 * DC])

    return pl.pallas_call(
        body, name="fwd_in", grid=(s // tm,),
        in_specs=[pl.BlockSpec((tm, D), lambda i: (i, 0)), _full((16, D)), _full((NDEV, D, nc))],
        out_specs=[pl.BlockSpec((tm, 4 * DC), lambda i: (i, 0)), pl.BlockSpec((tm, DC), lambda i: (i, 0)),
                   pl.BlockSpec((tm, D), lambda i: (i, 0))],
        out_shape=[jax.ShapeDtypeStruct((s, 4 * DC), F32), jax.ShapeDtypeStruct((s, DC), F32),
                   jax.ShapeDtypeStruct((s, D), BF16)],
        compiler_params=_arb(),
    )(x2d, vecs, w_in_g)


def _causal_mask(lower):
    r = lax.broadcasted_iota(jnp.int32, (CHUNK, CHUNK), 0)
    c = lax.broadcasted_iota(jnp.int32, (CHUNK, CHUNK), 1)
    return (r >= c) if lower else (r <= c)


def _first_head_lanes():
    return lax.broadcasted_iota(jnp.int32, (CHUNK, CHUNK), 1) < HD


def _fwd_mid(a0, z, x2d, vecs, v512, conv_w, gm_ws, bs_exp, w_out_b, tm):
    s = x2d.shape[0]
    hb = tm // HALO_C

    def body(a0_ref, halo_ref, zg_ref, x_ref, v_ref, p_ref, cw_ref, ws_ref, bs_ref, wo_ref,
             a1_ref, sp_ref, x2_ref, o1_ref, h2_ref):
        i = pl.program_id(0)
        for c0 in range(0, DC, LANES):
            cols = pl.ds(c0, LANES)
            halo = halo_ref[:, cols]
            e = jnp.concatenate([jnp.where(i > 0, halo, jnp.zeros_like(halo)), a0_ref[:, cols]], axis=0)
            acc = jnp.broadcast_to(p_ref[pl.ds(CB, 1), cols], (tm, LANES))
            for k in range(KC):
                acc = acc + _shift_up(e, HALO_C - (KC - 1) + k)[:tm, :] * cw_ref[pl.ds(k, 1), cols]
            a1_ref[:, cols] = acc
        xh, _ = _ln(a1_ref[...])
        a2 = xh * _row(p_ref, CLG) + _row(p_ref, CLB)
        a3 = a2 * _sigmoid(a2)
        gu, _ = _gelu(zg_ref[:, :DC])
        gvg, _ = _gelu(zg_ref[:, DC:])
        vh, _ = _ln(gvg)
        gvn = (vh * _row(p_ref, GLG) + _row(p_ref, GLB)).astype(BF16)
        low = _causal_mask(True)
        first = _first_head_lanes()
        wm = [jnp.where(low, ws_ref[0, h], 0.0).astype(BF16) for h in range(NH)]
        for n in range(tm // CHUNK):
            for p in range(NH // 2):
                v = gvn[n * CHUNK:(n + 1) * CHUNK, p * CHUNK:(p + 1) * CHUNK]
                blk = jnp.where(first, _dot(wm[2 * p], v), _dot(wm[2 * p + 1], v))
                sp_ref[pl.ds(n * CHUNK, CHUNK), pl.ds(p * CHUNK, CHUNK)] = blk + bs_ref[:, pl.ds(p * CHUNK, CHUNK)]
        g = gu * sp_ref[...]
        an, _ = _rms(a3)
        gn, _ = _rms(g)
        mog = _row(v_ref, MOG)
        y = jnp.concatenate([an * mog[:, :DC], gn * mog[:, DC:]], axis=1).astype(BF16)
        o1 = _dot(y, wo_ref[...])
        o1_ref[...] = o1
        x2 = x_ref[...] + _row(v_ref, GT1) * o1
        x2_ref[...] = x2
        xn2, _ = _rms(x2)
        h2 = (xn2 * _row(v_ref, G2)) * (1.0 + _row(v_ref, SC2)) + _row(v_ref, SH2)
        h2_ref[...] = h2.astype(BF16)

    tile = lambda w: pl.BlockSpec((tm, w), lambda i: (i, 0))
    return pl.pallas_call(
        body, name="fwd_mid", grid=(s // tm,),
        in_specs=[tile(DC), pl.BlockSpec((HALO_C, DC), lambda i: (jnp.maximum(i * hb - 1, 0), 0)),
                  pl.BlockSpec((tm, 2 * DC), lambda i: (i, 1)), tile(D), _full((16, D)), _full((8, DC)),
                  _full((32, DC)), _full((1, NH, CHUNK, CHUNK)), _full((CHUNK, DC)), _full((D, D))],
        out_specs=[tile(DC), tile(DC), tile(D), tile(D), tile(D)],
        out_shape=[jax.ShapeDtypeStruct((s, DC), F32), jax.ShapeDtypeStruct((s, DC), F32),
                   jax.ShapeDtypeStruct((s, D), F32), jax.ShapeDtypeStruct((s, D), F32),
                   jax.ShapeDtypeStruct((s, D), BF16)],
        compiler_params=_arb(),
    )(a0, a0, z, x2d, vecs, v512, conv_w, gm_ws, bs_exp, w_out_b)


def _ffn_conv(fw_ref, cols, p2, p1, pre):
    return (fw_ref[pl.ds(3, 1), cols] + fw_ref[pl.ds(0, 1), cols] * p2
            + fw_ref[pl.ds(1, 1), cols] * p1 + fw_ref[pl.ds(2, 1), cols] * pre)


def _fwd_ffn(h2, x2, target, vecs, ffn_wb, w_up_g, w_down_p, tm):
    s = x2.shape[0]

    def body(h2_ref, x2_ref, t_ref, v_ref, fw_ref, wu_hbm, wd_hbm,
             up_ref, f_ref, o2_ref, dx3_ref, acc_ref, wu, wd, carry):
        i = pl.program_id(0)

        @pl.when(i == 0)
        def _():
            pltpu.sync_copy(wu_hbm, wu)
            pltpu.sync_copy(wd_hbm, wd)
            carry[...] = jnp.zeros_like(carry)
            acc_ref[...] = jnp.zeros_like(acc_ref)

        h2v = h2_ref[...]
        o2 = jnp.zeros((tm, D), F32)
        for j in range(4):
            conv = []
            for sh in (j, 4 + j):
                cols = pl.ds(sh * PSH, PSH)
                pre = _dot(h2v, wu[sh])
                up_ref[:, cols] = pre
                e = jnp.concatenate([carry[:, cols], pre], axis=0)
                carry[:, cols] = pre[tm - HALO_F:, :]
                conv.append(_ffn_conv(fw_ref, cols, pltpu.roll(e, 2, 0)[HALO_F:, :],
                                      pltpu.roll(e, 1, 0)[HALO_F:, :], pre))
            val, gate = conv
            f = ((gate * _sigmoid(gate)) * val).astype(BF16)
            f_ref[:, pl.ds(j * PSH, PSH)] = f
            o2 = o2 + _dot(f, wd[j])
        o2_ref[...] = o2
        x3 = x2_ref[...] + _row(v_ref, GT2) * o2
        xn3, r3 = _rms(x3)
        gf = _row(v_ref, GF)
        diff = xn3 * gf - t_ref[...]
        acc_ref[pl.ds(1, 1), :] += _colsum(diff * diff) * (0.5 / D)
        dout = diff * (1.0 / D)
        acc_ref[pl.ds(0, 1), :] += _colsum(dout * xn3)
        dx3_ref[...] = _rms_bwd(dout * gf, xn3, r3)

    tile = lambda w: pl.BlockSpec((tm, w), lambda i: (i, 0))
    return pl.pallas_call(
        body, name="fwd_ffn", grid=(s // tm,),
        in_specs=[tile(D), tile(D), tile(D), _full((16, D)), _full((8, 2 * PFF)), ANY, ANY],
        out_specs=[tile(2 * PFF), tile(PFF), tile(D), tile(D), _full((8, D))],
        out_shape=[jax.ShapeDtypeStruct((s, 2 * PFF), F32), jax.ShapeDtypeStruct((s, PFF), BF16),
                   jax.ShapeDtypeStruct((s, D), F32), jax.ShapeDtypeStruct((s, D), F32),
                   jax.ShapeDtypeStruct((8, D), F32)],
        scratch_shapes=[pltpu.VMEM((NDEV, D, PSH), BF16), pltpu.VMEM((4, PSH, D), BF16),
                        pltpu.VMEM((HALO_F, 2 * PFF), F32)],
        compiler_params=_arb(),
    )(h2, x2, target, vecs, ffn_wb, w_up_g, w_down_p)


def _bwd_ffn(dx3, o2, x2, up_pre, vecs, ffn_wb, w_up_g, w_down_p, tm):
    s = x2.shape[0]
    nt = s // tm
    hb = tm // HALO_F

    def body(dx3_ref, o2_ref, x2_ref, up_ref, halo_ref, v_ref, fw_ref, wu_hbm, wd_hbm,
             dx2_ref, dup_ref, do2_ref, acc_ref, accf_ref, wu, wd, carry):
        i = pl.program_id(0)
        r = nt - 1 - i

        @pl.when(i == 0)
        def _():
            pltpu.sync_copy(wu_hbm, wu)
            pltpu.sync_copy(wd_hbm, wd)
            carry[...] = jnp.zeros_like(carry)
            acc_ref[...] = jnp.zeros_like(acc_ref)
            accf_ref[...] = jnp.zeros_like(accf_ref)

        dx3v = dx3_ref[...]
        do2 = (dx3v * _row(v_ref, GT2)).astype(BF16)
        do2_ref[...] = do2
        acc_ref[pl.ds(0, 1), :] += _colsum(dx3v * o2_ref[...])
        dh2 = jnp.zeros((tm, D), F32)
        for j in range(4):
            df = _dot_nt(do2, wd[j])
            shifted, conv = [], []
            for sh in (j, 4 + j):
                cols = pl.ds(sh * PSH, PSH)
                pre = up_ref[:, cols]
                hl = halo_ref[:, cols]
                e = jnp.concatenate([jnp.where(r > 0, hl, jnp.zeros_like(hl)), pre], axis=0)
                p2 = pltpu.roll(e, 2, 0)[HALO_F:, :]
                p1 = pltpu.roll(e, 1, 0)[HALO_F:, :]
                shifted.append((p2, p1, pre))
                conv.append(_ffn_conv(fw_ref, cols, p2, p1, pre))
            val, gate = conv
            sg = _sigmoid(gate)
            dups = (df * (gate * sg), df * val * (sg * (1.0 + gate * (1.0 - sg))))
            for (sh, dup, (p2, p1, pre)) in zip((j, 4 + j), dups, shifted):
                cols = pl.ds(sh * PSH, PSH)
                accf_ref[pl.ds(3, 1), cols] += _colsum(dup)
                accf_ref[pl.ds(0, 1), cols] += _colsum(dup * p2)
                accf_ref[pl.ds(1, 1), cols] += _colsum(dup * p1)
                accf_ref[pl.ds(2, 1), cols] += _colsum(dup * pre)
                e = jnp.concatenate([dup, carry[:, cols]], axis=0)
                carry[:, cols] = dup[:HALO_F, :]
                dpre = (fw_ref[pl.ds(0, 1), cols] * _shift_up(e, 2)[:tm, :]
                        + fw_ref[pl.ds(1, 1), cols] * _shift_up(e, 1)[:tm, :]
                        + fw_ref[pl.ds(2, 1), cols] * dup).astype(BF16)
                dup_ref[:, cols] = dpre
                dh2 = dh2 + _dot_nt(dpre, wu[sh])
        xn2, r2 = _rms(x2_ref[...])
        g2 = _row(v_ref, G2)
        sc = 1.0 + _row(v_ref, SC2)
        acc_ref[pl.ds(1, 1), :] += _colsum(dh2)
        acc_ref[pl.ds(2, 1), :] += _colsum(dh2 * (xn2 * g2))
        acc_ref[pl.ds(3, 1), :] += _colsum(dh2 * sc * xn2)
        dx2_ref[...] = dx3v + _rms_bwd(dh2 * sc * g2, xn2, r2)

    tile = lambda w: pl.BlockSpec((tm, w), lambda i: (nt - 1 - i, 0))
    return pl.pallas_call(
        body, name="bwd_ffn", grid=(nt,),
        in_specs=[tile(D), tile(D), tile(D), tile(2 * PFF),
                  pl.BlockSpec((HALO_F, 2 * PFF), lambda i: (jnp.maximum((nt - 1 - i) * hb - 1, 0), 0)),
                  _full((16, D)), _full((8, 2 * PFF)), ANY, ANY],
        out_specs=[tile(D), tile(2 * PFF), tile(D), _full((8, D)), _full((8, 2 * PFF))],
        out_shape=[jax.ShapeDtypeStruct((s, D), F32), jax.ShapeDtypeStruct((s, 2 * PFF), BF16),
                   jax.ShapeDtypeStruct((s, D), BF16), jax.ShapeDtypeStruct((8, D), F32),
                   jax.ShapeDtypeStruct((8, 2 * PFF), F32)],
        scratch_shapes=[pltpu.VMEM((NDEV, D, PSH), BF16), pltpu.VMEM((4, PSH, D), BF16),
                        pltpu.VMEM((HALO_F, 2 * PFF), F32)],
        compiler_params=_arb(),
    )(dx3, o2, x2, up_pre, up_pre, vecs, ffn_wb, w_up_g, w_down_p)


def _bwd_mid(dx2, x2d, o1, z, a0, a1, sp, vecs, v512, conv_w, gm_ws, gm_ws_t, w_out_b, w_in_g, tm):
    s = x2d.shape[0]
    nt = s // tm
    hb = tm // HALO_C
    nc = w_in_g.shape[2]

    def body(dx2_ref, x_ref, o1_ref, z_ref, a0_ref, halo_ref, a1_ref, sp_ref, v_ref, p_ref, cw_ref, ws_ref, wst_ref,
             wo_ref, wi_ref, gx_ref, dz_ref, y_ref, do1_ref, acc_ref, accp_ref, dcw_ref, dws_ref, dbst_ref,
             dbs_s, carry, da1_s, dsp_s, dgvn_s):
        i = pl.program_id(0)
        r = nt - 1 - i

        @pl.when(i == 0)
        def _():
            for ref in (carry, dbs_s, acc_ref, accp_ref, dcw_ref, dws_ref, dbst_ref):
                ref[...] = jnp.zeros_like(ref)

        dx2v = dx2_ref[...]
        do1 = (dx2v * _row(v_ref, GT1)).astype(BF16)
        do1_ref[...] = do1
        acc_ref[pl.ds(0, 1), :] += _colsum(dx2v * o1_ref[...])
        dy = _dot_nt(do1, wo_ref[...])
        mog = _row(v_ref, MOG)

        xh, rstd = _ln(a1_ref[...])
        clg = _row(p_ref, CLG)
        a2 = xh * clg + _row(p_ref, CLB)
        s2 = _sigmoid(a2)
        a3 = a2 * s2
        an, ra = _rms(a3)
        dya = dy[:, :DC]
        da3 = _rms_bwd(dya * mog[:, :DC], an, ra)
        da2 = da3 * (s2 * (1.0 + a2 * (1.0 - s2)))
        accp_ref[pl.ds(CLB, 1), :] += _colsum(da2)
        accp_ref[pl.ds(CLG, 1), :] += _colsum(da2 * xh)
        da1 = _ln_bwd(da2 * clg, xh, rstd)
        accp_ref[pl.ds(CB, 1), :] += _colsum(da1)
        da1_s[...] = da1
        for c0 in range(0, DC, LANES):
            cols = pl.ds(c0, LANES)
            d = da1_s[:, cols]
            e = jnp.concatenate([d, carry[:, cols]], axis=0)
            carry[:, cols] = d[:HALO_C, :]
            acc = jnp.zeros((tm, LANES), F32)
            for j in range(KC):
                acc = acc + _shift_up(e, j)[:tm, :] * cw_ref[pl.ds(KC - 1 - j, 1), cols]
            sgc = _sigmoid(z_ref[:, pl.ds(DC + c0, LANES)])
            dz_ref[:, cols] = (acc * sgc).astype(BF16)
            dz_ref[:, pl.ds(DC + c0, LANES)] = (acc * z_ref[:, cols] * sgc * (1.0 - sgc)).astype(BF16)
            halo = halo_ref[:, cols]
            ea = jnp.concatenate([jnp.where(r > 0, halo, jnp.zeros_like(halo)), a0_ref[:, cols]], axis=0)
            for k in range(KC):
                dcw_ref[pl.ds(k, 1), cols] += _colsum(d * _shift_up(ea, HALO_C - (KC - 1) + k)[:tm, :])

        gu_pre = z_ref[:, 2 * DC:3 * DC]
        gv_pre = z_ref[:, 3 * DC:]
        gu, tu = _gelu(gu_pre)
        gvg, tv = _gelu(gv_pre)
        vh, vrstd = _ln(gvg)
        glg = _row(p_ref, GLG)
        gvn = (vh * glg + _row(p_ref, GLB)).astype(BF16)
        spv = sp_ref[...]
        g = gu * spv
        gn, rg = _rms(g)
        y_ref[...] = jnp.concatenate([an * mog[:, :DC], gn * mog[:, DC:]], axis=1).astype(BF16)
        acc_ref[pl.ds(4, 1), :] += jnp.concatenate([_colsum(dya * an), _colsum(dy[:, DC:] * gn)], axis=1)
        dg = _rms_bwd(dy[:, DC:] * mog[:, DC:], gn, rg)
        dz_ref[:, pl.ds(2 * DC, DC)] = (dg * spv * _gelu_grad(gu_pre, tu)).astype(BF16)
        dsp_s[...] = dg * gu
        upper = _causal_mask(False)
        first = _first_head_lanes()
        wmt = [jnp.where(upper, wst_ref[h], 0.0).astype(BF16) for h in range(NH)]
        for n in range(tm // CHUNK):
            rows = pl.ds(n * CHUNK, CHUNK)
            for p in range(NH // 2):
                cols = pl.ds(p * CHUNK, CHUNK)
                dsp = dsp_s[rows, cols]
                dbs_s[:, cols] += dsp
                da = jnp.where(first, dsp, 0.0).astype(BF16)
                db = jnp.where(first, 0.0, dsp).astype(BF16)
                v = gvn[n * CHUNK:(n + 1) * CHUNK, p * CHUNK:(p + 1) * CHUNK]
                dws_ref[2 * p] += _dot_nt(da, v)
                dws_ref[2 * p + 1] += _dot_nt(db, v)
                dgvn_s[rows, cols] = _dot(wmt[2 * p], da) + _dot(wmt[2 * p + 1], db)
        dgvn = dgvn_s[...]
        accp_ref[pl.ds(GLB, 1), :] += _colsum(dgvn)
        accp_ref[pl.ds(GLG, 1), :] += _colsum(dgvn * vh)
        dgvg = _ln_bwd(dgvn * glg, vh, vrstd)
        dz_ref[:, pl.ds(3 * DC, DC)] = (dgvg * _gelu_grad(gv_pre, tv)).astype(BF16)

        dh1 = jnp.zeros((tm, D), F32)
        for d in range(NDEV):
            dh1 = dh1 + _dot_nt(dz_ref[:, pl.ds(d * nc, nc)], wi_ref[d])
        xn, r1 = _rms(x_ref[...])
        g1 = _row(v_ref, G1)
        sc = 1.0 + _row(v_ref, SC1)
        acc_ref[pl.ds(1, 1), :] += _colsum(dh1)
        acc_ref[pl.ds(2, 1), :] += _colsum(dh1 * (xn * g1))
        acc_ref[pl.ds(3, 1), :] += _colsum(dh1 * sc * xn)
        gx_ref[...] = dx2v + _rms_bwd(dh1 * sc * g1, xn, r1)

        @pl.when(i == nt - 1)
        def _():
            low = _causal_mask(True)
            for h in range(NH):
                dws_ref[h] = jnp.where(low, dws_ref[h], 0.0)
            lane = lax.broadcasted_iota(jnp.int32, (CHUNK, CHUNK), 1)
            out = jnp.zeros((CHUNK, CHUNK), F32)
            for h in range(NH):
                hs = jnp.sum(dbs_s[:, pl.ds((h // 2) * CHUNK, CHUNK)]
                             * ((lane >= (h % 2) * HD) & (lane < (h % 2 + 1) * HD)).astype(F32),
                             axis=1, keepdims=True)
                out = jnp.where(lane == h, hs, out)
            dbst_ref[...] = out

    tile = lambda w: pl.BlockSpec((tm, w), lambda i: (nt - 1 - i, 0))
    return pl.pallas_call(
        body, name="bwd_mid", grid=(nt,),
        in_specs=[tile(D), tile(D), tile(D), tile(4 * DC), tile(DC),
                  pl.BlockSpec((HALO_C, DC), lambda i: (jnp.maximum((nt - 1 - i) * hb - 1, 0), 0)),
                  tile(DC), tile(DC), _full((16, D)), _full((8, DC)), _full((32, DC)),
                  _full((NH, CHUNK, CHUNK)), _full((NH, CHUNK, CHUNK)), _full((D, D)), _full((NDEV, D, nc))],
        out_specs=[tile(D), tile(4 * DC), tile(D), tile(D), _full((16, D)), _full((8, DC)), _full((32, DC)),
                   _full((NH, CHUNK, CHUNK)), _full((CHUNK, CHUNK))],
        out_shape=[jax.ShapeDtypeStruct((s, D), F32), jax.ShapeDtypeStruct((s, 4 * DC), BF16),
                   jax.ShapeDtypeStruct((s, D), BF16), jax.ShapeDtypeStruct((s, D), BF16),
                   jax.ShapeDtypeStruct((16, D), F32), jax.ShapeDtypeStruct((8, DC), F32),
                   jax.ShapeDtypeStruct((32, DC), F32), jax.ShapeDtypeStruct((NH, CHUNK, CHUNK), F32),
                   jax.ShapeDtypeStruct((CHUNK, CHUNK), F32)],
        scratch_shapes=[pltpu.VMEM((CHUNK, DC), F32), pltpu.VMEM((HALO_C, DC), F32), pltpu.VMEM((tm, DC), F32),
                        pltpu.VMEM((tm, DC), F32), pltpu.VMEM((tm, DC), F32)],
        compiler_params=_arb(),
    )(dx2, x2d, o1, z, a0, a0, a1, sp, vecs, v512, conv_w, gm_ws, gm_ws_t, w_out_b, w_in_g)


def _mm_tn(name, a, b, tk):
    s, k1 = a.shape
    k2 = b.shape[1]

    def body(a_ref, b_ref, o_ref):
        @pl.when(pl.program_id(0) == 0)
        def _():
            o_ref[...] = jnp.zeros_like(o_ref)

        o_ref[...] += _dot_tn(a_ref[...], b_ref[...])

    return pl.pallas_call(
        body, name=name, grid=(s // tk,),
        in_specs=[pl.BlockSpec((tk, k1), lambda k: (k, 0)), pl.BlockSpec((tk, k2), lambda k: (k, 0))],
        out_specs=_full((k1, k2)), out_shape=jax.ShapeDtypeStruct((k1, k2), F32), compiler_params=_arb(),
    )(a, b)


def _mm_tn_col_slots(name, a, b, bw, ow, tk):
    s, k1 = a.shape
    nslot = b.shape[1] // bw

    def body(a_ref, b_ref, o_ref):
        @pl.when(pl.program_id(1) == 0)
        def _():
            o_ref[...] = jnp.zeros_like(o_ref)

        o_ref[...] += _dot_tn(a_ref[...], b_ref[...])[:, :ow]

    return pl.pallas_call(
        body, name=name, grid=(nslot, s // tk),
        in_specs=[pl.BlockSpec((tk, k1), lambda j, k: (k, 0)), pl.BlockSpec((tk, bw), lambda j, k: (k, j))],
        out_specs=pl.BlockSpec((None, k1, ow), lambda j, k: (j, 0, 0)),
        out_shape=jax.ShapeDtypeStruct((nslot, k1, ow), F32),
        compiler_params=pltpu.CompilerParams(dimension_semantics=("parallel", "arbitrary")),
    )(a, b)


def _mm_tn_row_slots(name, a, b, aw, oh, tk):
    s, k2 = b.shape
    nslot = a.shape[1] // aw

    def body(a_ref, b_ref, o_ref):
        @pl.when(pl.program_id(1) == 0)
        def _():
            o_ref[...] = jnp.zeros_like(o_ref)

        o_ref[...] += _dot_tn(a_ref[...], b_ref[...])[:oh, :]

    return pl.pallas_call(
        body, name=name, grid=(nslot, s // tk),
        in_specs=[pl.BlockSpec((tk, aw), lambda i, k: (k, i)), pl.BlockSpec((tk, k2), lambda i, k: (k, 0))],
        out_specs=pl.BlockSpec((None, oh, k2), lambda i, k: (i, 0, 0)),
        out_shape=jax.ShapeDtypeStruct((nslot, oh, k2), F32),
        compiler_params=pltpu.CompilerParams(dimension_semantics=("parallel", "arbitrary")),
    )(a, b)


def _adam_math(w, g, m, v):
    m = ADAM_B1 * m + (1.0 - ADAM_B1) * g
    v = ADAM_B2 * v + (1.0 - ADAM_B2) * (g * g)
    m_hat = m / (1.0 - ADAM_B1 ** ADAM_STEP)
    v_hat = v / (1.0 - ADAM_B2 ** ADAM_STEP)
    delta = -ADAM_LR * (m_hat / (jnp.sqrt(v_hat) + ADAM_EPS) + ADAM_WD * w)
    return delta, m, v


def _row_block(rows, cols):
    tr = rows
    while tr * cols * 4 > (2 << 20) and tr % 32 == 0:
        tr //= 2
    return tr


def _adam3(name, w, g, m, v):
    _, rows, cols = w.shape
    tr = _row_block(rows, cols)

    def body(w_ref, g_ref, m_ref, v_ref, d_ref, mo_ref, vo_ref):
        d_ref[...], mo_ref[...], vo_ref[...] = _adam_math(w_ref[...], g_ref[...], m_ref[...], v_ref[...])

    spec = pl.BlockSpec((1, tr, cols), lambda i: (0, i, 0))
    return pl.pallas_call(
        body, name=name, grid=(rows // tr,), in_specs=[spec] * 4, out_specs=[spec] * 3,
        out_shape=[jax.ShapeDtypeStruct(w.shape, F32)] * 3, compiler_params=_arb(),
    )(w, g, m, v)


def _sum_adam(name, parts, w, m, v):
    n, rows, cols = parts.shape
    tr = _row_block(rows, cols)

    def body(p_ref, w_ref, m_ref, v_ref, g_ref, d_ref, mo_ref, vo_ref):
        g = p_ref[0].astype(F32)
        for k in range(1, n):
            g = g + p_ref[k].astype(F32)
        g_ref[0] = g
        d_ref[0], mo_ref[0], vo_ref[0] = _adam_math(w_ref[0], g, m_ref[0], v_ref[0])

    spec = pl.BlockSpec((1, tr, cols), lambda i: (0, i, 0))
    return pl.pallas_call(
        body, name=name, grid=(rows // tr,),
        in_specs=[pl.BlockSpec((n, tr, cols), lambda i: (0, i, 0))] + [spec] * 3, out_specs=[spec] * 4,
        out_shape=[jax.ShapeDtypeStruct(w.shape, F32)] * 4, compiler_params=_arb(),
    )(parts, w, m, v)


def _pair_add(name, g4, recv, core):
    _, _, rows, cols = g4.shape
    tr = _row_block(rows, cols)

    def body(c_ref, a_ref, b_ref, o_ref):
        o_ref[...] = (a_ref[...] + b_ref[...]).astype(BF16)

    return pl.pallas_call(
        body, name=name,
        grid_spec=pltpu.PrefetchScalarGridSpec(
            num_scalar_prefetch=1, grid=(4, rows // tr),
            in_specs=[pl.BlockSpec((None, None, tr, cols), lambda k, i, c_ref: (k, c_ref[0], i, 0)),
                      pl.BlockSpec((None, tr, cols), lambda k, i, c_ref: (k, i, 0))],
            out_specs=pl.BlockSpec((None, tr, cols), lambda k, i, c_ref: (k, i, 0))),
        out_shape=jax.ShapeDtypeStruct((4, rows, cols), BF16), compiler_params=_arb(2),
    )(core, g4, recv)


def _sum_small(rows_all, p_all, ws_all, bst_all, fw_all, cw_all):
    def body(a_ref, p_ref, ws_ref, bst_ref, fw_ref, cw_ref,
             g_b_ada, g_n1, g_mog, g_n2, g_gf, g_cb, g_clg, g_clb, g_glg, g_glb, g_ws, g_bs, fw_sum, cw_sum):
        def total(ref):
            t = ref[0]
            for k in range(1, NDEV):
                t = t + ref[k]
            return t

        a = total(a_ref)
        g_b_ada[...] = jnp.concatenate([a[k:k + 1, :] for k in range(6)], axis=1)
        g_n1[...] = a[6:7, :]
        g_mog[...] = a[7:8, :]
        g_n2[...] = a[8:9, :]
        g_gf[...] = a[9:10, :].reshape(D)
        p = total(p_ref)
        for k, ref in zip((CB, CLG, CLB, GLG, GLB), (g_cb, g_clg, g_clb, g_glg, g_glb)):
            ref[...] = p[k:k + 1, :]
        g_ws[0] = total(ws_ref)
        g_bs[0] = jnp.transpose(total(bst_ref))[:NH, :]
        fw_sum[...] = total(fw_ref)
        cw_sum[...] = total(cw_ref)

    vec = lambda n: jax.ShapeDtypeStruct((1, n), F32)
    return pl.pallas_call(
        body, name="sum_small_grads",
        out_shape=[vec(6 * D), vec(D), vec(D), vec(D), jax.ShapeDtypeStruct((D,), F32),
                   vec(DC), vec(DC), vec(DC), vec(DC), vec(DC),
                   jax.ShapeDtypeStruct((1, NH, CHUNK, CHUNK), F32), jax.ShapeDtypeStruct((1, NH, CHUNK), F32),
                   jax.ShapeDtypeStruct((8, 2 * PFF), F32), jax.ShapeDtypeStruct((32, DC), F32)],
    )(rows_all, p_all, ws_all, bst_all, fw_all, cw_all)


def _adam_small(quads):
    n = len(quads)

    def body(*refs):
        ins, outs = refs[:4 * n], refs[4 * n:]
        for q in range(n):
            w, g, m, v = (r[...] for r in ins[4 * q:4 * q + 4])
            outs[3 * q][...], outs[3 * q + 1][...], outs[3 * q + 2][...] = _adam_math(w, g, m, v)

    flat = [a for q in quads for a in q]
    outs = pl.pallas_call(
        body, name="adam_small",
        out_shape=[jax.ShapeDtypeStruct(q[0].shape, F32) for q in quads for _ in range(3)],
    )(*flat)
    return [tuple(outs[3 * q:3 * q + 3]) for q in range(n)]


def kernel(x, c, w_ada, b_ada, norm1_gain, w_in, conv_dw_w, conv_dw_b, conv_ln_g, conv_ln_b, gm_ln_g, gm_ln_b, gm_ws, gm_bs, mix_out_gain, w_out, norm2_gain, w_up, ffn_dw_w, ffn_dw_b, w_down, final_gain, loss_target, m_w_ada, m_b_ada, m_norm1_gain, m_w_in, m_conv_dw_w, m_conv_dw_b, m_conv_ln_g, m_conv_ln_b, m_gm_ln_g, m_gm_ln_b, m_gm_ws, m_gm_bs, m_mix_out_gain, m_w_out, m_norm2_gain, m_w_up, m_ffn_dw_w, m_ffn_dw_b, m_w_down, m_final_gain, v_w_ada, v_b_ada, v_norm1_gain, v_w_in, v_conv_dw_w, v_conv_dw_b, v_conv_ln_g, v_conv_ln_b, v_gm_ln_g, v_gm_ln_b, v_gm_ws, v_gm_bs, v_mix_out_gain, v_w_out, v_norm2_gain, v_w_up, v_ffn_dw_w, v_ffn_dw_b, v_w_down, v_final_gain):
    s = x.shape[1]
    ax, ay, ac = _place()
    me = 4 * ax + 2 * ay + ac
    n_ada = w_ada.shape[2]
    n_cw = conv_dw_w.shape[2]
    x2d = x[0]
    target = loss_target[0]
    pad_sh = lambda a: jnp.pad(a, [(0, 0)] * (a.ndim - 1) + [(0, PSH - NSH)])

    c_all, cw_all, fw_all = _all_gather("gather_small", [c, conv_dw_w[0], ffn_dw_w[0]])
    conv_w = jnp.pad(jnp.transpose(cw_all, (1, 0, 2)).reshape(KC, DC), ((0, 32 - KC), (0, 0)))
    ffn_w = jnp.transpose(pad_sh(fw_all), (1, 0, 2)).reshape(KF, 2 * PFF)
    ffn_b = pad_sh(ffn_dw_b.reshape(NDEV, NSH)).reshape(1, 2 * PFF)
    ffn_wb = jnp.concatenate([ffn_w, ffn_b, jnp.zeros((8 - KF - 1, 2 * PFF), F32)], axis=0)
    w_in_g, w_out_g, w_up_g, w_down_g = _all_gather(
        "gather_weights", [w_in[0].astype(BF16), w_out[0].astype(BF16), pad_sh(w_up[0].astype(BF16)),
                           w_down[0].astype(BF16)])
    w_out_b = w_out_g.reshape(D, D)
    w_down_p = jnp.pad(w_down_g.reshape(4, NSH, D), ((0, 0), (0, PSH - NSH), (0, 0)))

    b_cols = lax.dynamic_slice(b_ada, (0, me * n_ada), (1, n_ada))
    (mod_all,) = _all_gather("gather_mod", [_mod_part(c_all, w_ada, b_cols)])
    mod = lax.dynamic_index_in_dim(mod_all, me, axis=1, keepdims=False).reshape(6, D)
    sh1, sc1, gt1, sh2, sc2, gt2 = [mod[k:k + 1] for k in range(6)]
    vecs = jnp.concatenate([norm1_gain, sh1, sc1, gt1, norm2_gain, sh2, sc2, gt2, mix_out_gain,
                            final_gain.reshape(1, D), jnp.zeros((6, D), F32)], axis=0)
    v512 = jnp.concatenate([conv_dw_b, conv_ln_g, conv_ln_b, gm_ln_g, gm_ln_b, jnp.zeros((3, DC), F32)], axis=0)
    bs_exp = jnp.repeat(jnp.transpose(gm_bs[0]), HD, axis=1)
    gm_ws_t = jnp.swapaxes(gm_ws[0], 1, 2)

    tm_in = min(512, s)
    tm = min(256, s)
    z, a0, h1 = _fwd_in(x2d, vecs, w_in_g, tm_in)
    a1, sp, x2, o1, h2 = _fwd_mid(a0, z, x2d, vecs, v512, conv_w, gm_ws, bs_exp, w_out_b, tm)
    up_pre, f, o2, dx3, acc_f = _fwd_ffn(h2, x2, target, vecs, ffn_wb, w_up_g, w_down_p, tm)
    loss = lax.psum(jnp.sum(acc_f[1]), ("x", "y", "c"))

    dx2, dup_pre, do2, acc_b, acc_fw = _bwd_ffn(dx3, o2, x2, up_pre, vecs, ffn_wb, w_up_g, w_down_p, tm)
    gx, dz, y, do1, acc_m, acc_p, dcw, dws, dbs_t = _bwd_mid(
        dx2, x2d, o1, z, a0, a1, sp, vecs, v512, conv_w, gm_ws[0], gm_ws_t, w_out_b, w_in_g, tm)
    tk = min(512, s)
    dw_in = _mm_tn_col_slots("dw_in", h1, dz, w_in.shape[2], w_in.shape[2], tk)
    dw_out = _mm_tn("dw_out", y, do1, tk).reshape(NDEV, w_out.shape[1], D)
    dw_up = _mm_tn_col_slots("dw_up", h2, dup_pre, PSH, NSH, tk)
    dw_down = _mm_tn_row_slots("dw_down", f, do2, PSH, NSH, tk).reshape(NDEV, w_down.shape[1], D)

    rows = jnp.concatenate([acc_m[1:3], acc_m[0:1], acc_b[1:3], acc_b[0:1], acc_m[3:5], acc_b[3:4], acc_f[0:1],
                            jnp.zeros((6, D), F32)], axis=0)
    rows_all, p_all, ws_all, bst_all, fwg_all, cwg_all = _all_gather(
        "gather_small_grads", [rows, acc_p, dws, dbs_t, acc_fw, dcw])
    (g_b_ada, g_n1, g_mog, g_n2, g_gf, g_cb, g_clg, g_clb, g_glg, g_glb, g_ws, g_bs, fw_sum, cw_sum) = _sum_small(
        rows_all, p_all, ws_all, bst_all, fwg_all, cwg_all)
    g_fb = fw_sum[3].reshape(NDEV, PSH)[:, :NSH].reshape(ffn_dw_b.shape)
    g_fw = lax.dynamic_index_in_dim(fw_sum[:KF].reshape(KF, NDEV, PSH), me, axis=1, keepdims=False)[:, :NSH]
    g_fw = g_fw.reshape(ffn_dw_w.shape)
    g_cw = lax.dynamic_slice(cw_sum, (0, me * n_cw), (KC, n_cw)).reshape(conv_dw_w.shape)
    small = [
        (b_ada, g_b_ada, m_b_ada, v_b_ada), (norm1_gain, g_n1, m_norm1_gain, v_norm1_gain),
        (conv_dw_w, g_cw, m_conv_dw_w, v_conv_dw_w), (conv_dw_b, g_cb, m_conv_dw_b, v_conv_dw_b),
        (conv_ln_g, g_clg, m_conv_ln_g, v_conv_ln_g), (conv_ln_b, g_clb, m_conv_ln_b, v_conv_ln_b),
        (gm_ln_g, g_glg, m_gm_ln_g, v_gm_ln_g), (gm_ln_b, g_glb, m_gm_ln_b, v_gm_ln_b),
        (gm_ws, g_ws, m_gm_ws, v_gm_ws), (gm_bs, g_bs, m_gm_bs, v_gm_bs),
        (mix_out_gain, g_mog, m_mix_out_gain, v_mix_out_gain), (norm2_gain, g_n2, m_norm2_gain, v_norm2_gain),
        (ffn_dw_w, g_fw, m_ffn_dw_w, v_ffn_dw_w), (ffn_dw_b, g_fb, m_ffn_dw_b, v_ffn_dw_b),
        (final_gain, g_gf, m_final_gain, v_final_gain)]
    small_out = _adam_small(small)
    res = {}
    for name, q, o in zip(("b_ada", "norm1_gain", "conv_dw_w", "conv_dw_b", "conv_ln_g", "conv_ln_b", "gm_ln_g",
                           "gm_ln_b", "gm_ws", "gm_bs", "mix_out_gain", "norm2_gain", "ffn_dw_w", "ffn_dw_b",
                           "final_gain"), small, small_out):
        res[name] = (q[1],) + o

    dmod_all = rows_all[:, :6].reshape(NDEV, 6 * D)
    dm_cols = lax.dynamic_slice(dmod_all, (0, me * n_ada), (NDEV, n_ada))
    g_ada = _ada_grad(jnp.transpose(c_all[:, 0, :]), dm_cols)
    res["w_ada"] = (g_ada,) + tuple(_adam3("adam_ada", w_ada, g_ada, m_w_ada, v_w_ada))

    big = [("w_in", dw_in, w_in, m_w_in, v_w_in), ("w_out", dw_out, w_out, m_w_out, v_w_out),
           ("w_up", dw_up, w_up, m_w_up, v_w_up), ("w_down", dw_down, w_down, m_w_down, v_w_down)]
    g4s = [t[1].reshape((4, 2) + t[1].shape[1:]) for t in big]
    from_sibling = _sibling_swap("rs_sibling", g4s)
    core = ac.reshape(1).astype(jnp.int32)
    pairs = [_pair_add("rs_pair_add_" + t[0], g4, rv, core) for t, g4, rv in zip(big, g4s, from_sibling)]
    from_chips = _chip_exchange("rs_chips", pairs)
    for t, parts in zip(big, from_chips):
        res[t[0]] = tuple(_sum_adam("rs_sum_adam_" + t[0], parts, t[2], t[3], t[4]))

    order = ("w_ada", "b_ada", "norm1_gain", "w_in", "conv_dw_w", "conv_dw_b", "conv_ln_g", "conv_ln_b", "gm_ln_g",
             "gm_ln_b", "gm_ws", "gm_bs", "mix_out_gain", "w_out", "norm2_gain", "w_up", "ffn_dw_w", "ffn_dw_b",
             "w_down", "final_gain")
    return (loss, gx.reshape(x.shape), *[res[n][0] for n in order], *[res[n][1] for n in order],
            *[res[n][2] for n in order], *[res[n][3] for n in order])
```

```python
import jax
import jax.numpy as jnp
from jax import lax
from jax.experimental import pallas as pl
from jax.experimental.pallas import tpu as pltpu

F32 = jnp.float32
BF16 = jnp.bfloat16
NDEV = 8
D = 1024
DC = 512
DFF = 2816
NSH = 704
PSH = 768
PFF = 4 * PSH
KC = 31
KF = 3
CHUNK = 128
NH = 8
HD = 64
HALO_C = 32
HALO_F = 8
LANES = 128
RMS_EPS = 1e-6
LN_EPS = 1e-5
ADAM_LR = 0.001
ADAM_B1 = 0.9
ADAM_B2 = 0.999
ADAM_EPS = 1e-08
ADAM_WD = 0.01
ADAM_STEP = 10
GELU_K = 0.7978845608028654
GELU_C = 0.044715

MESH = pl.DeviceIdType.MESH
ANY = pl.BlockSpec(memory_space=pl.ANY)

G1, SH1, SC1, GT1, G2, SH2, SC2, GT2, MOG, GF = range(10)
CB, CLG, CLB, GLG, GLB = range(5)


def _full(shape):
    return pl.BlockSpec(shape, lambda *_: (0,) * len(shape))


def _arb(n=1):
    return pltpu.CompilerParams(dimension_semantics=("arbitrary",) * n)


def _row(ref, r):
    return ref[pl.ds(r, 1), :]


def _colsum(v):
    return jnp.sum(v, axis=0, keepdims=True)


def _rowmean(v):
    return jnp.mean(v, axis=-1, keepdims=True)


def _rms(x):
    r = lax.rsqrt(_rowmean(x * x) + RMS_EPS)
    return x * r, r


def _rms_bwd(dxn, xn, r):
    return r * (dxn - xn * _rowmean(dxn * xn))


def _ln(x):
    mu = _rowmean(x)
    xc = x - mu
    rstd = lax.rsqrt(_rowmean(xc * xc) + LN_EPS)
    return xc * rstd, rstd


def _ln_bwd(dxh, xhat, rstd):
    return rstd * (dxh - _rowmean(dxh) - xhat * _rowmean(dxh * xhat))


def _sigmoid(x):
    return 0.5 * jnp.tanh(0.5 * x) + 0.5


def _gelu(x):
    t = jnp.tanh(GELU_K * (x + GELU_C * x * x * x))
    return 0.5 * x * (1.0 + t), t


def _gelu_grad(x, t):
    return 0.5 * (1.0 + t) + 0.5 * x * (1.0 - t * t) * (GELU_K * (1.0 + 3.0 * GELU_C * x * x))


def _dot(a, b):
    return jnp.dot(a, b, preferred_element_type=F32)


def _dot_nt(a, b):
    return lax.dot_general(a, b, (((1,), (1,)), ((), ())), preferred_element_type=F32)


def _dot_tn(a, b):
    return lax.dot_general(a, b, (((0,), (0,)), ((), ())), preferred_element_type=F32)


def _shift_up(e, s):
    n = e.shape[0]
    return pltpu.roll(e, (n - s) % n, 0)


def _place():
    return lax.axis_index("x"), lax.axis_index("y"), lax.axis_index("c")


def _all_gather(name, xs):
    n = len(xs)

    def body(*refs):
        x_refs, out_refs = refs[:n], refs[n:2 * n]
        send_sems, recv_sems, local_sems = refs[2 * n:]
        x, y, c = _place()
        me, sibling = (x, y, c), (x, y, 1 - c)
        chips = [(1 - x, y), (x, 1 - y), (1 - x, 1 - y)]

        def copy(a, k, block, to, own=False):
            px, py, pc = block
            slot = out_refs[a].at[4 * px + 2 * py + pc]
            return pltpu.make_async_remote_copy(
                src_ref=x_refs[a] if own else slot, dst_ref=slot,
                send_sem=send_sems.at[7 * a + k], recv_sem=recv_sems.at[7 * a + k], device_id=to, device_id_type=MESH)

        mine = [pltpu.make_async_copy(x_refs[a], out_refs[a].at[4 * x + 2 * y + c], local_sems.at[a]) for a in range(n)]
        for cp in mine:
            cp.start()
        first = []
        for a in range(n):
            first.append(copy(a, 0, me, sibling, own=True))
            first += [copy(a, 1 + j, me, (*chip, c), own=True) for j, chip in enumerate(chips)]
        for cp in first:
            cp.start()
        passed = []
        for j, chip in enumerate(chips):
            for a in range(n):
                copy(a, 1 + j, (*chip, c), me).wait_recv()
                cp = copy(a, 4 + j, (*chip, c), sibling)
                cp.start()
                passed.append(cp)
        for a in range(n):
            copy(a, 0, sibling, me).wait_recv()
            for j, chip in enumerate(chips):
                copy(a, 4 + j, (*chip, 1 - c), me).wait_recv()
        for cp in first + passed:
            cp.wait_send()
        for cp in mine:
            cp.wait()

    return pl.pallas_call(
        body, name=name, out_shape=[jax.ShapeDtypeStruct((NDEV,) + a.shape, a.dtype) for a in xs],
        in_specs=[ANY] * n, out_specs=[ANY] * n,
        scratch_shapes=[pltpu.SemaphoreType.DMA((7 * n,)), pltpu.SemaphoreType.DMA((7 * n,)),
                        pltpu.SemaphoreType.DMA((n,))],
    )(*xs)


def _sibling_swap(name, g4s):
    n = len(g4s)

    def body(*refs):
        g_refs, out_refs = refs[:n], refs[n:2 * n]
        send_sems, recv_sems = refs[2 * n:]
        x, y, c = _place()
        cps = [pltpu.make_async_remote_copy(
            src_ref=g_refs[a].at[k, 1 - c], dst_ref=out_refs[a].at[k],
            send_sem=send_sems.at[4 * a + k], recv_sem=recv_sems.at[4 * a + k],
            device_id=(x, y, 1 - c), device_id_type=MESH) for a in range(n) for k in range(4)]
        for cp in cps:
            cp.start()
        for cp in cps:
            cp.wait()

    return pl.pallas_call(
        body, name=name, out_shape=[jax.ShapeDtypeStruct((4,) + g.shape[2:], g.dtype) for g in g4s],
        in_specs=[ANY] * n, out_specs=[ANY] * n,
        scratch_shapes=[pltpu.SemaphoreType.DMA((4 * n,)), pltpu.SemaphoreType.DMA((4 * n,))],
    )(*g4s)


def _chip_exchange(name, hs):
    n = len(hs)

    def body(*refs):
        h_refs, out_refs = refs[:n], refs[n:2 * n]
        send_sems, recv_sems, local_sems = refs[2 * n:]
        x, y, c = _place()
        mychip = 2 * x + y
        chips = [(1 - x, y), (x, 1 - y), (1 - x, 1 - y)]

        def copy(a, j, arriving):
            px, py = chips[j]
            theirs = 2 * px + py
            return pltpu.make_async_remote_copy(
                src_ref=h_refs[a].at[mychip if arriving else theirs],
                dst_ref=out_refs[a].at[theirs if arriving else mychip],
                send_sem=send_sems.at[3 * a + j], recv_sem=recv_sems.at[3 * a + j],
                device_id=(px, py, c), device_id_type=MESH)

        mine = [pltpu.make_async_copy(h_refs[a].at[mychip], out_refs[a].at[mychip], local_sems.at[a]) for a in range(n)]
        for cp in mine:
            cp.start()
        cps = [copy(a, j, False) for a in range(n) for j in range(3)]
        for cp in cps:
            cp.start()
        for a in range(n):
            for j in range(3):
                copy(a, j, True).wait_recv()
        for cp in cps:
            cp.wait_send()
        for cp in mine:
            cp.wait()

    return pl.pallas_call(
        body, name=name, out_shape=[jax.ShapeDtypeStruct(h.shape, h.dtype) for h in hs],
        in_specs=[ANY] * n, out_specs=[ANY] * n,
        scratch_shapes=[pltpu.SemaphoreType.DMA((3 * n,)), pltpu.SemaphoreType.DMA((3 * n,)),
                        pltpu.SemaphoreType.DMA((n,))],
    )(*hs)


HBM = pl.BlockSpec(memory_space=pltpu.HBM)
SEM = pl.BlockSpec(memory_space=pltpu.SEMAPHORE)
EFFECT = pltpu.SideEffectType.DATAFLOW_SIDE_EFFECTING


def _in_hbm(a):
    return pltpu.with_memory_space_constraint(a, pltpu.HBM)


def _split_start(name, bufs, copies):
    n = len(bufs)

    def body(*refs):
        for cp in copies(refs[:n], refs[n], refs[n + 1]):
            cp.start()
        refs[-1][...] = jnp.zeros_like(refs[-1])

    out = pl.pallas_call(
        body, name=name,
        out_shape=(pltpu.SemaphoreType.DMA((copies.count,)), pltpu.SemaphoreType.DMA((copies.count,)),
                   *[pltpu.HBM(a.shape, a.dtype) for a in bufs], jax.ShapeDtypeStruct((8, LANES), F32)),
        in_specs=[HBM] * n, out_specs=(SEM, SEM, *[HBM] * n, pl.BlockSpec(memory_space=pltpu.VMEM)),
        input_output_aliases={i: 2 + i for i in range(n)},
        compiler_params=pltpu.CompilerParams(has_side_effects=EFFECT),
    )(*[_in_hbm(a) for a in bufs])
    return (out[0], out[1], list(out[2:2 + n])), out[-1]


def _split_wait(name, handle, copies, after):
    send_sems, recv_sems, bufs = handle
    n = len(bufs)

    def body(*refs):
        for cp in copies(refs[:n], refs[n], refs[n + 1]):
            cp.wait_send()
            cp.wait_recv()

    out = pl.pallas_call(
        body, name=name, out_shape=tuple(pltpu.HBM(a.shape, a.dtype) for a in bufs),
        in_specs=[HBM] * n + [SEM, SEM, pl.BlockSpec(memory_space=pl.ANY)], out_specs=tuple([HBM] * n),
        input_output_aliases={i: i for i in range(n)},
        compiler_params=pltpu.CompilerParams(has_side_effects=EFFECT),
    )(*bufs, send_sems, recv_sems, after)
    return list(out)


class _GatherFirstCopies:
    def __init__(self, n):
        self.n, self.count = n, 4 * n

    def __call__(self, refs, send_sems, recv_sems):
        x, y, c = _place()
        peers = [(x, y, 1 - c), (1 - x, y, c), (x, 1 - y, c), (1 - x, 1 - y, c)]
        return [pltpu.make_async_remote_copy(
            src_ref=refs[a], dst_ref=refs[self.n + a].at[4 * x + 2 * y + c],
            send_sem=send_sems.at[4 * a + k], recv_sem=recv_sems.at[4 * a + k], device_id=peer, device_id_type=MESH)
            for a in range(self.n) for k, peer in enumerate(peers)]


class _GatherPassCopies:
    def __init__(self, n):
        self.n, self.count = n, 3 * n

    def __call__(self, refs, send_sems, recv_sems):
        x, y, c = _place()
        cps = []
        for a in range(self.n):
            for j, (px, py) in enumerate([(1 - x, y), (x, 1 - y), (1 - x, 1 - y)]):
                slot = refs[a].at[4 * px + 2 * py + c]
                cps.append(pltpu.make_async_remote_copy(
                    src_ref=slot, dst_ref=slot, send_sem=send_sems.at[3 * a + j], recv_sem=recv_sems.at[3 * a + j],
                    device_id=(x, y, 1 - c), device_id_type=MESH))
        return cps


class _ExchangeCopies:
    def __init__(self, n):
        self.n, self.count = n, 3 * n

    def __call__(self, refs, send_sems, recv_sems):
        x, y, c = _place()
        cps = []
        for a in range(self.n):
            for j, (px, py) in enumerate([(1 - x, y), (x, 1 - y), (1 - x, 1 - y)]):
                cps.append(pltpu.make_async_remote_copy(
                    src_ref=refs[a].at[2 * px + py], dst_ref=refs[self.n + a].at[2 * x + y],
                    send_sem=send_sems.at[3 * a + j], recv_sem=recv_sems.at[3 * a + j],
                    device_id=(px, py, c), device_id_type=MESH))
        return cps


def _own_slot(nslot, src, index):
    land = lax.empty((nslot,) + src.shape, src.dtype)
    return lax.dynamic_update_slice(land, src[None], (index,) + (0,) * src.ndim)


def _start_gather(tag, xs, me):
    lands = [_own_slot(NDEV, a, me) for a in xs]
    return _split_start(tag + "_start", list(xs) + lands, _GatherFirstCopies(len(xs)))


def _finish_gather(tag, handle, after):
    n = len(handle[2]) // 2
    lands = _split_wait(tag + "_wait", handle, _GatherFirstCopies(n), after)[n:]
    passing, token = _split_start(tag + "_pass", lands, _GatherPassCopies(n))
    return _split_wait(tag + "_pass_wait", passing, _GatherPassCopies(n), token)


def _start_exchange(tag, hs, mychip):
    lands = [_own_slot(4, lax.dynamic_index_in_dim(h, mychip, 0, keepdims=False), mychip) for h in hs]
    return _split_start(tag + "_start", list(hs) + lands, _ExchangeCopies(len(hs)))


def _finish_exchange(tag, handle, after):
    n = len(handle[2]) // 2
    return _split_wait(tag + "_wait", handle, _ExchangeCopies(n), after)[n:]


def _mod_part(c_all, w_ada, b_cols):
    ncol = w_ada.shape[2]

    def body(c_ref, w_ref, b_ref, o_ref):
        cv = c_ref[:, 0, :]
        ca = cv * _sigmoid(cv)
        o_ref[...] = _dot(ca.astype(BF16), w_ref[0].astype(BF16)) + b_ref[...]

    return pl.pallas_call(body, name="mod_part", out_shape=jax.ShapeDtypeStruct((NDEV, ncol), F32))(
        c_all, w_ada, b_cols)


def _ada_grad(c_all_t, dmod_cols):
    ncol = dmod_cols.shape[1]

    def body(ct_ref, dm_ref, o_ref):
        ct = ct_ref[...]
        ca = ct * _sigmoid(ct)
        acc = jnp.zeros((D, ncol), F32)
        for b in range(NDEV):
            acc = acc + ca[:, b:b + 1] * dm_ref[pl.ds(b, 1), :]
        o_ref[0] = acc

    return pl.pallas_call(body, name="ada_grad", out_shape=jax.ShapeDtypeStruct((1, D, ncol), F32))(
        c_all_t, dmod_cols)


def _fwd_in(x2d, vecs, w_in_g, tm):
    s = x2d.shape[0]
    nc = w_in_g.shape[2]

    def body(x_ref, v_ref, w_ref, z_ref, a0_ref, h1t_ref):
        xn, _ = _rms(x_ref[...])
        h = (xn * _row(v_ref, G1)) * (1.0 + _row(v_ref, SC1)) + _row(v_ref, SH1)
        hb = h.astype(BF16)
        h1t_ref[...] = hb.T
        for d in range(NDEV):
            z_ref[:, pl.ds(d * nc, nc)] = _dot(hb, w_ref[d])
        a0_ref[...] = z_ref[:, :DC] * _sigmoid(z_ref[:, DC:2 * DC])

    return pl.pallas_call(
        body, name="fwd_in", grid=(s // tm,),
        in_specs=[pl.BlockSpec((tm, D), lambda i: (i, 0)), _full((16, D)), _full((NDEV, D, nc))],
        out_specs=[pl.BlockSpec((tm, 4 * DC), lambda i: (i, 0)), pl.BlockSpec((tm, DC), lambda i: (i, 0)),
                   pl.BlockSpec((D, tm), lambda i: (0, i))],
        out_shape=[jax.ShapeDtypeStruct((s, 4 * DC), F32), jax.ShapeDtypeStruct((s, DC), F32),
                   jax.ShapeDtypeStruct((D, s), BF16)],
        compiler_params=_arb(),
    )(x2d, vecs, w_in_g)


def _causal_mask(lower):
    r = lax.broadcasted_iota(jnp.int32, (CHUNK, CHUNK), 0)
    c = lax.broadcasted_iota(jnp.int32, (CHUNK, CHUNK), 1)
    return (r >= c) if lower else (r <= c)


def _first_head_lanes():
    return lax.broadcasted_iota(jnp.int32, (CHUNK, CHUNK), 1) < HD


def _fwd_mid(a0, z, x2d, vecs, v512, conv_w, gm_ws, bs_exp, w_out_b, tm):
    s = x2d.shape[0]
    hb = tm // HALO_C

    def body(a0_ref, halo_ref, zg_ref, x_ref, v_ref, p_ref, cw_ref, ws_ref, bs_ref, wo_ref,
             a1_ref, sp_ref, x2_ref, o1_ref, h2_ref, h2t_ref):
        i = pl.program_id(0)
        for c0 in range(0, DC, LANES):
            cols = pl.ds(c0, LANES)
            halo = halo_ref[:, cols]
            e = jnp.concatenate([jnp.where(i > 0, halo, jnp.zeros_like(halo)), a0_ref[:, cols]], axis=0)
            acc = jnp.broadcast_to(p_ref[pl.ds(CB, 1), cols], (tm, LANES))
            for k in range(KC):
                acc = acc + _shift_up(e, HALO_C - (KC - 1) + k)[:tm, :] * cw_ref[pl.ds(k, 1), cols]
            a1_ref[:, cols] = acc
        xh, _ = _ln(a1_ref[...])
        a2 = xh * _row(p_ref, CLG) + _row(p_ref, CLB)
        a3 = a2 * _sigmoid(a2)
        gu, _ = _gelu(zg_ref[:, :DC])
        gvg, _ = _gelu(zg_ref[:, DC:])
        vh, _ = _ln(gvg)
        gvn = (vh * _row(p_ref, GLG) + _row(p_ref, GLB)).astype(BF16)
        low = _causal_mask(True)
        first = _first_head_lanes()
        wm = [jnp.where(low, ws_ref[0, h], 0.0).astype(BF16) for h in range(NH)]
        for n in range(tm // CHUNK):
            for p in range(NH // 2):
                v = gvn[n * CHUNK:(n + 1) * CHUNK, p * CHUNK:(p + 1) * CHUNK]
                blk = jnp.where(first, _dot(wm[2 * p], v), _dot(wm[2 * p + 1], v))
                sp_ref[pl.ds(n * CHUNK, CHUNK), pl.ds(p * CHUNK, CHUNK)] = blk + bs_ref[:, pl.ds(p * CHUNK, CHUNK)]
        g = gu * sp_ref[...]
        an, _ = _rms(a3)
        gn, _ = _rms(g)
        mog = _row(v_ref, MOG)
        y = jnp.concatenate([an * mog[:, :DC], gn * mog[:, DC:]], axis=1).astype(BF16)
        o1 = _dot(y, wo_ref[...])
        o1_ref[...] = o1
        x2 = x_ref[...] + _row(v_ref, GT1) * o1
        x2_ref[...] = x2
        xn2, _ = _rms(x2)
        h2 = (xn2 * _row(v_ref, G2)) * (1.0 + _row(v_ref, SC2)) + _row(v_ref, SH2)
        h2b = h2.astype(BF16)
        h2_ref[...] = h2b
        h2t_ref[...] = h2b.T

    tile = lambda w: pl.BlockSpec((tm, w), lambda i: (i, 0))
    return pl.pallas_call(
        body, name="fwd_mid", grid=(s // tm,),
        in_specs=[tile(DC), pl.BlockSpec((HALO_C, DC), lambda i: (jnp.maximum(i * hb - 1, 0), 0)),
                  pl.BlockSpec((tm, 2 * DC), lambda i: (i, 1)), tile(D), _full((16, D)), _full((8, DC)),
                  _full((32, DC)), _full((1, NH, CHUNK, CHUNK)), _full((CHUNK, DC)), _full((D, D))],
        out_specs=[tile(DC), tile(DC), tile(D), tile(D), tile(D), pl.BlockSpec((D, tm), lambda i: (0, i))],
        out_shape=[jax.ShapeDtypeStruct((s, DC), F32), jax.ShapeDtypeStruct((s, DC), F32),
                   jax.ShapeDtypeStruct((s, D), F32), jax.ShapeDtypeStruct((s, D), F32),
                   jax.ShapeDtypeStruct((s, D), BF16), jax.ShapeDtypeStruct((D, s), BF16)],
        compiler_params=_arb(),
    )(a0, a0, z, x2d, vecs, v512, conv_w, gm_ws, bs_exp, w_out_b)


def _ffn_conv(fw_ref, cols, p2, p1, pre):
    return (fw_ref[pl.ds(3, 1), cols] + fw_ref[pl.ds(0, 1), cols] * p2
            + fw_ref[pl.ds(1, 1), cols] * p1 + fw_ref[pl.ds(2, 1), cols] * pre)


def _fwd_ffn(h2, x2, target, vecs, ffn_wb, w_up_g, w_down_p, tm):
    s = x2.shape[0]

    def body(h2_ref, x2_ref, t_ref, v_ref, fw_ref, wu_hbm, wd_hbm,
             up_ref, ft_ref, o2_ref, dx3_ref, acc_ref, wu, wd, carry):
        i = pl.program_id(0)

        @pl.when(i == 0)
        def _():
            pltpu.sync_copy(wu_hbm, wu)
            pltpu.sync_copy(wd_hbm, wd)
            carry[...] = jnp.zeros_like(carry)
            acc_ref[...] = jnp.zeros_like(acc_ref)

        h2v = h2_ref[...]
        o2 = jnp.zeros((tm, D), F32)
        for j in range(4):
            conv = []
            for sh in (j, 4 + j):
                cols = pl.ds(sh * PSH, PSH)
                pre = _dot(h2v, wu[sh])
                up_ref[:, cols] = pre
                e = jnp.concatenate([carry[:, cols], pre], axis=0)
                carry[:, cols] = pre[tm - HALO_F:, :]
                conv.append(_ffn_conv(fw_ref, cols, pltpu.roll(e, 2, 0)[HALO_F:, :],
                                      pltpu.roll(e, 1, 0)[HALO_F:, :], pre))
            val, gate = conv
            f = ((gate * _sigmoid(gate)) * val).astype(BF16)
            ft_ref[pl.ds(j * PSH, PSH), :] = f.T
            o2 = o2 + _dot(f, wd[j])
        o2_ref[...] = o2
        x3 = x2_ref[...] + _row(v_ref, GT2) * o2
        xn3, r3 = _rms(x3)
        gf = _row(v_ref, GF)
        diff = xn3 * gf - t_ref[...]
        acc_ref[pl.ds(1, 1), :] += _colsum(diff * diff) * (0.5 / D)
        dout = diff * (1.0 / D)
        acc_ref[pl.ds(0, 1), :] += _colsum(dout * xn3)
        dx3_ref[...] = _rms_bwd(dout * gf, xn3, r3)

    tile = lambda w: pl.BlockSpec((tm, w), lambda i: (i, 0))
    return pl.pallas_call(
        body, name="fwd_ffn", grid=(s // tm,),
        in_specs=[tile(D), tile(D), tile(D), _full((16, D)), _full((8, 2 * PFF)), ANY, ANY],
        out_specs=[tile(2 * PFF), pl.BlockSpec((PFF, tm), lambda i: (0, i)), tile(D), tile(D), _full((8, D))],
        out_shape=[jax.ShapeDtypeStruct((s, 2 * PFF), F32), jax.ShapeDtypeStruct((PFF, s), BF16),
                   jax.ShapeDtypeStruct((s, D), F32), jax.ShapeDtypeStruct((s, D), F32),
                   jax.ShapeDtypeStruct((8, D), F32)],
        scratch_shapes=[pltpu.VMEM((NDEV, D, PSH), BF16), pltpu.VMEM((4, PSH, D), BF16),
                        pltpu.VMEM((HALO_F, 2 * PFF), F32)],
        compiler_params=_arb(),
    )(h2, x2, target, vecs, ffn_wb, w_up_g, w_down_p)


def _bwd_ffn(dx3, o2, x2, up_pre, vecs, ffn_wb, w_up_g, w_down_p, tm):
    s = x2.shape[0]
    nt = s // tm
    hb = tm // HALO_F

    def body(dx3_ref, o2_ref, x2_ref, up_ref, halo_ref, v_ref, fw_ref, wu_hbm, wd_hbm,
             dx2_ref, dup_ref, do2_ref, acc_ref, accf_ref, wu, wd, carry):
        i = pl.program_id(0)
        r = nt - 1 - i

        @pl.when(i == 0)
        def _():
            pltpu.sync_copy(wu_hbm, wu)
            pltpu.sync_copy(wd_hbm, wd)
            carry[...] = jnp.zeros_like(carry)
            acc_ref[...] = jnp.zeros_like(acc_ref)
            accf_ref[...] = jnp.zeros_like(accf_ref)

        dx3v = dx3_ref[...]
        do2 = (dx3v * _row(v_ref, GT2)).astype(BF16)
        do2_ref[...] = do2
        acc_ref[pl.ds(0, 1), :] += _colsum(dx3v * o2_ref[...])
        dh2 = jnp.zeros((tm, D), F32)
        for j in range(4):
            df = _dot_nt(do2, wd[j])
            shifted, conv = [], []
            for sh in (j, 4 + j):
                cols = pl.ds(sh * PSH, PSH)
                pre = up_ref[:, cols]
                hl = halo_ref[:, cols]
                e = jnp.concatenate([jnp.where(r > 0, hl, jnp.zeros_like(hl)), pre], axis=0)
                p2 = pltpu.roll(e, 2, 0)[HALO_F:, :]
                p1 = pltpu.roll(e, 1, 0)[HALO_F:, :]
                shifted.append((p2, p1, pre))
                conv.append(_ffn_conv(fw_ref, cols, p2, p1, pre))
            val, gate = conv
            sg = _sigmoid(gate)
            dups = (df * (gate * sg), df * val * (sg * (1.0 + gate * (1.0 - sg))))
            for (sh, dup, (p2, p1, pre)) in zip((j, 4 + j), dups, shifted):
                cols = pl.ds(sh * PSH, PSH)
                accf_ref[pl.ds(3, 1), cols] += _colsum(dup)
                accf_ref[pl.ds(0, 1), cols] += _colsum(dup * p2)
                accf_ref[pl.ds(1, 1), cols] += _colsum(dup * p1)
                accf_ref[pl.ds(2, 1), cols] += _colsum(dup * pre)
                e = jnp.concatenate([dup, carry[:, cols]], axis=0)
                carry[:, cols] = dup[:HALO_F, :]
                dpre = (fw_ref[pl.ds(0, 1), cols] * _shift_up(e, 2)[:tm, :]
                        + fw_ref[pl.ds(1, 1), cols] * _shift_up(e, 1)[:tm, :]
                        + fw_ref[pl.ds(2, 1), cols] * dup).astype(BF16)
                dup_ref[:, cols] = dpre
                dh2 = dh2 + _dot_nt(dpre, wu[sh])
        xn2, r2 = _rms(x2_ref[...])
        g2 = _row(v_ref, G2)
        sc = 1.0 + _row(v_ref, SC2)
        acc_ref[pl.ds(1, 1), :] += _colsum(dh2)
        acc_ref[pl.ds(2, 1), :] += _colsum(dh2 * (xn2 * g2))
        acc_ref[pl.ds(3, 1), :] += _colsum(dh2 * sc * xn2)
        dx2_ref[...] = dx3v + _rms_bwd(dh2 * sc * g2, xn2, r2)

    tile = lambda w: pl.BlockSpec((tm, w), lambda i: (nt - 1 - i, 0))
    return pl.pallas_call(
        body, name="bwd_ffn", grid=(nt,),
        in_specs=[tile(D), tile(D), tile(D), tile(2 * PFF),
                  pl.BlockSpec((HALO_F, 2 * PFF), lambda i: (jnp.maximum((nt - 1 - i) * hb - 1, 0), 0)),
                  _full((16, D)), _full((8, 2 * PFF)), ANY, ANY],
        out_specs=[tile(D), tile(2 * PFF), tile(D), _full((8, D)), _full((8, 2 * PFF))],
        out_shape=[jax.ShapeDtypeStruct((s, D), F32), jax.ShapeDtypeStruct((s, 2 * PFF), BF16),
                   jax.ShapeDtypeStruct((s, D), BF16), jax.ShapeDtypeStruct((8, D), F32),
                   jax.ShapeDtypeStruct((8, 2 * PFF), F32)],
        scratch_shapes=[pltpu.VMEM((NDEV, D, PSH), BF16), pltpu.VMEM((4, PSH, D), BF16),
                        pltpu.VMEM((HALO_F, 2 * PFF), F32)],
        compiler_params=_arb(),
    )(dx3, o2, x2, up_pre, up_pre, vecs, ffn_wb, w_up_g, w_down_p)


def _bwd_mid(dx2, x2d, o1, z, a0, a1, sp, vecs, v512, conv_w, gm_ws, gm_ws_t, w_out_b, w_in_g, tm):
    s = x2d.shape[0]
    nt = s // tm
    hb = tm // HALO_C
    nc = w_in_g.shape[2]

    def body(dx2_ref, x_ref, o1_ref, z_ref, a0_ref, halo_ref, a1_ref, sp_ref, v_ref, p_ref, cw_ref, ws_ref, wst_ref,
             wo_ref, wi_ref, gx_ref, dz_ref, yt_ref, do1_ref, acc_ref, accp_ref, dcw_ref, dws_ref, dbst_ref,
             dbs_s, carry, da1_s, dsp_s, dgvn_s):
        i = pl.program_id(0)
        r = nt - 1 - i

        @pl.when(i == 0)
        def _():
            for ref in (carry, dbs_s, acc_ref, accp_ref, dcw_ref, dws_ref, dbst_ref):
                ref[...] = jnp.zeros_like(ref)

        dx2v = dx2_ref[...]
        do1 = (dx2v * _row(v_ref, GT1)).astype(BF16)
        do1_ref[...] = do1
        acc_ref[pl.ds(0, 1), :] += _colsum(dx2v * o1_ref[...])
        dy = _dot_nt(do1, wo_ref[...])
        mog = _row(v_ref, MOG)

        xh, rstd = _ln(a1_ref[...])
        clg = _row(p_ref, CLG)
        a2 = xh * clg + _row(p_ref, CLB)
        s2 = _sigmoid(a2)
        a3 = a2 * s2
        an, ra = _rms(a3)
        dya = dy[:, :DC]
        da3 = _rms_bwd(dya * mog[:, :DC], an, ra)
        da2 = da3 * (s2 * (1.0 + a2 * (1.0 - s2)))
        accp_ref[pl.ds(CLB, 1), :] += _colsum(da2)
        accp_ref[pl.ds(CLG, 1), :] += _colsum(da2 * xh)
        da1 = _ln_bwd(da2 * clg, xh, rstd)
        accp_ref[pl.ds(CB, 1), :] += _colsum(da1)
        da1_s[...] = da1
        for c0 in range(0, DC, LANES):
            cols = pl.ds(c0, LANES)
            d = da1_s[:, cols]
            e = jnp.concatenate([d, carry[:, cols]], axis=0)
            carry[:, cols] = d[:HALO_C, :]
            acc = jnp.zeros((tm, LANES), F32)
            for j in range(KC):
                acc = acc + _shift_up(e, j)[:tm, :] * cw_ref[pl.ds(KC - 1 - j, 1), cols]
            sgc = _sigmoid(z_ref[:, pl.ds(DC + c0, LANES)])
            dz_ref[:, cols] = (acc * sgc).astype(BF16)
            dz_ref[:, pl.ds(DC + c0, LANES)] = (acc * z_ref[:, cols] * sgc * (1.0 - sgc)).astype(BF16)
            halo = halo_ref[:, cols]
            ea = jnp.concatenate([jnp.where(r > 0, halo, jnp.zeros_like(halo)), a0_ref[:, cols]], axis=0)
            for k in range(KC):
                dcw_ref[pl.ds(k, 1), cols] += _colsum(d * _shift_up(ea, HALO_C - (KC - 1) + k)[:tm, :])

        gu_pre = z_ref[:, 2 * DC:3 * DC]
        gv_pre = z_ref[:, 3 * DC:]
        gu, tu = _gelu(gu_pre)
        gvg, tv = _gelu(gv_pre)
        vh, vrstd = _ln(gvg)
        glg = _row(p_ref, GLG)
        gvn = (vh * glg + _row(p_ref, GLB)).astype(BF16)
        spv = sp_ref[...]
        g = gu * spv
        gn, rg = _rms(g)
        yt_ref[...] = jnp.concatenate([an * mog[:, :DC], gn * mog[:, DC:]], axis=1).astype(BF16).T
        acc_ref[pl.ds(4, 1), :] += jnp.concatenate([_colsum(dya * an), _colsum(dy[:, DC:] * gn)], axis=1)
        dg = _rms_bwd(dy[:, DC:] * mog[:, DC:], gn, rg)
        dz_ref[:, pl.ds(2 * DC, DC)] = (dg * spv * _gelu_grad(gu_pre, tu)).astype(BF16)
        dsp_s[...] = dg * gu
        upper = _causal_mask(False)
        first = _first_head_lanes()
        wmt = [jnp.where(upper, wst_ref[h], 0.0).astype(BF16) for h in range(NH)]
        for n in range(tm // CHUNK):
            rows = pl.ds(n * CHUNK, CHUNK)
            for p in range(NH // 2):
                cols = pl.ds(p * CHUNK, CHUNK)
                dsp = dsp_s[rows, cols]
                dbs_s[:, cols] += dsp
                da = jnp.where(first, dsp, 0.0).astype(BF16)
                db = jnp.where(first, 0.0, dsp).astype(BF16)
                v = gvn[n * CHUNK:(n + 1) * CHUNK, p * CHUNK:(p + 1) * CHUNK]
                dws_ref[2 * p] += _dot_nt(da, v)
                dws_ref[2 * p + 1] += _dot_nt(db, v)
                dgvn_s[rows, cols] = _dot(wmt[2 * p], da) + _dot(wmt[2 * p + 1], db)
        dgvn = dgvn_s[...]
        accp_ref[pl.ds(GLB, 1), :] += _colsum(dgvn)
        accp_ref[pl.ds(GLG, 1), :] += _colsum(dgvn * vh)
        dgvg = _ln_bwd(dgvn * glg, vh, vrstd)
        dz_ref[:, pl.ds(3 * DC, DC)] = (dgvg * _gelu_grad(gv_pre, tv)).astype(BF16)

        dh1 = jnp.zeros((tm, D), F32)
        for d in range(NDEV):
            dh1 = dh1 + _dot_nt(dz_ref[:, pl.ds(d * nc, nc)], wi_ref[d])
        xn, r1 = _rms(x_ref[...])
        g1 = _row(v_ref, G1)
        sc = 1.0 + _row(v_ref, SC1)
        acc_ref[pl.ds(1, 1), :] += _colsum(dh1)
        acc_ref[pl.ds(2, 1), :] += _colsum(dh1 * (xn * g1))
        acc_ref[pl.ds(3, 1), :] += _colsum(dh1 * sc * xn)
        gx_ref[...] = dx2v + _rms_bwd(dh1 * sc * g1, xn, r1)

        @pl.when(i == nt - 1)
        def _():
            low = _causal_mask(True)
            for h in range(NH):
                dws_ref[h] = jnp.where(low, dws_ref[h], 0.0)
            lane = lax.broadcasted_iota(jnp.int32, (CHUNK, CHUNK), 1)
            out = jnp.zeros((CHUNK, CHUNK), F32)
            for h in range(NH):
                hs = jnp.sum(dbs_s[:, pl.ds((h // 2) * CHUNK, CHUNK)]
                             * ((lane >= (h % 2) * HD) & (lane < (h % 2 + 1) * HD)).astype(F32),
                             axis=1, keepdims=True)
                out = jnp.where(lane == h, hs, out)
            dbst_ref[...] = out

    tile = lambda w: pl.BlockSpec((tm, w), lambda i: (nt - 1 - i, 0))
    return pl.pallas_call(
        body, name="bwd_mid", grid=(nt,),
        in_specs=[tile(D), tile(D), tile(D), tile(4 * DC), tile(DC),
                  pl.BlockSpec((HALO_C, DC), lambda i: (jnp.maximum((nt - 1 - i) * hb - 1, 0), 0)),
                  tile(DC), tile(DC), _full((16, D)), _full((8, DC)), _full((32, DC)),
                  _full((NH, CHUNK, CHUNK)), _full((NH, CHUNK, CHUNK)), _full((D, D)), _full((NDEV, D, nc))],
        out_specs=[tile(D), tile(4 * DC), pl.BlockSpec((D, tm), lambda i: (0, nt - 1 - i)), tile(D),
                   _full((16, D)), _full((8, DC)), _full((32, DC)),
                   _full((NH, CHUNK, CHUNK)), _full((CHUNK, CHUNK))],
        out_shape=[jax.ShapeDtypeStruct((s, D), F32), jax.ShapeDtypeStruct((s, 4 * DC), BF16),
                   jax.ShapeDtypeStruct((D, s), BF16), jax.ShapeDtypeStruct((s, D), BF16),
                   jax.ShapeDtypeStruct((16, D), F32), jax.ShapeDtypeStruct((8, DC), F32),
                   jax.ShapeDtypeStruct((32, DC), F32), jax.ShapeDtypeStruct((NH, CHUNK, CHUNK), F32),
                   jax.ShapeDtypeStruct((CHUNK, CHUNK), F32)],
        scratch_shapes=[pltpu.VMEM((CHUNK, DC), F32), pltpu.VMEM((HALO_C, DC), F32), pltpu.VMEM((tm, DC), F32),
                        pltpu.VMEM((tm, DC), F32), pltpu.VMEM((tm, DC), F32)],
        compiler_params=_arb(),
    )(dx2, x2d, o1, z, a0, a0, a1, sp, vecs, v512, conv_w, gm_ws, gm_ws_t, w_out_b, w_in_g)


def _mm_all_slots(name, at, b, bw, tk):
    k1, s = at.shape
    nslot = b.shape[1] // bw

    def body(a_ref, b_ref, o_ref):
        @pl.when(pl.program_id(0) == 0)
        def _():
            o_ref[...] = jnp.zeros_like(o_ref)

        t = _dot(a_ref[...], b_ref[...])
        for j in range(nslot):
            o_ref[j] += t[:, j * bw:(j + 1) * bw]

    return pl.pallas_call(
        body, name=name, grid=(s // tk,),
        in_specs=[pl.BlockSpec((k1, tk), lambda k: (0, k)), pl.BlockSpec((tk, nslot * bw), lambda k: (k, 0))],
        out_specs=_full((nslot, k1, bw)), out_shape=jax.ShapeDtypeStruct((nslot, k1, bw), F32),
        compiler_params=_arb(),
    )(at, b)


def _mm_col_slots(name, at, b, bw, ow, tk):
    k1, s = at.shape
    nslot = b.shape[1] // bw

    def body(a_ref, b_ref, o_ref):
        @pl.when(pl.program_id(1) == 0)
        def _():
            o_ref[...] = jnp.zeros_like(o_ref)

        o_ref[...] += _dot(a_ref[...], b_ref[...])[:, :ow]

    return pl.pallas_call(
        body, name=name, grid=(nslot, s // tk),
        in_specs=[pl.BlockSpec((k1, tk), lambda j, k: (0, k)), pl.BlockSpec((tk, bw), lambda j, k: (k, j))],
        out_specs=pl.BlockSpec((None, k1, ow), lambda j, k: (j, 0, 0)),
        out_shape=jax.ShapeDtypeStruct((nslot, k1, ow), F32),
        compiler_params=pltpu.CompilerParams(dimension_semantics=("parallel", "arbitrary")),
    )(at, b)


def _mm_row_slots(name, at, b, ah, oh, tk):
    s, k2 = b.shape
    nslot = at.shape[0] // ah

    def body(a_ref, b_ref, o_ref):
        @pl.when(pl.program_id(1) == 0)
        def _():
            o_ref[...] = jnp.zeros_like(o_ref)

        o_ref[...] += _dot(a_ref[...], b_ref[...])[:oh, :]

    return pl.pallas_call(
        body, name=name, grid=(nslot, s // tk),
        in_specs=[pl.BlockSpec((ah, tk), lambda i, k: (i, k)), pl.BlockSpec((tk, k2), lambda i, k: (k, 0))],
        out_specs=pl.BlockSpec((None, oh, k2), lambda i, k: (i, 0, 0)),
        out_shape=jax.ShapeDtypeStruct((nslot, oh, k2), F32),
        compiler_params=pltpu.CompilerParams(dimension_semantics=("parallel", "arbitrary")),
    )(at, b)


def _adam_math(w, g, m, v):
    m = ADAM_B1 * m + (1.0 - ADAM_B1) * g
    v = ADAM_B2 * v + (1.0 - ADAM_B2) * (g * g)
    m_hat = m / (1.0 - ADAM_B1 ** ADAM_STEP)
    v_hat = v / (1.0 - ADAM_B2 ** ADAM_STEP)
    delta = -ADAM_LR * (m_hat / (jnp.sqrt(v_hat) + ADAM_EPS) + ADAM_WD * w)
    return delta, m, v


def _row_block(rows, cols):
    tr = rows
    while tr * cols * 4 > (2 << 20) and tr % 32 == 0:
        tr //= 2
    return tr


def _adam3(name, w, g, m, v):
    _, rows, cols = w.shape
    tr = _row_block(rows, cols)

    def body(w_ref, g_ref, m_ref, v_ref, d_ref, mo_ref, vo_ref):
        d_ref[...], mo_ref[...], vo_ref[...] = _adam_math(w_ref[...], g_ref[...], m_ref[...], v_ref[...])

    spec = pl.BlockSpec((1, tr, cols), lambda i: (0, i, 0))
    return pl.pallas_call(
        body, name=name, grid=(rows // tr,), in_specs=[spec] * 4, out_specs=[spec] * 3,
        out_shape=[jax.ShapeDtypeStruct(w.shape, F32)] * 3, compiler_params=_arb(),
    )(w, g, m, v)


def _sum_adam(name, parts, w, m, v):
    n, rows, cols = parts.shape
    tr = _row_block(rows, cols)

    def body(p_ref, w_ref, m_ref, v_ref, g_ref, d_ref, mo_ref, vo_ref):
        g = p_ref[0].astype(F32)
        for k in range(1, n):
            g = g + p_ref[k].astype(F32)
        g_ref[0] = g
        d_ref[0], mo_ref[0], vo_ref[0] = _adam_math(w_ref[0], g, m_ref[0], v_ref[0])

    spec = pl.BlockSpec((1, tr, cols), lambda i: (0, i, 0))
    return pl.pallas_call(
        body, name=name, grid=(rows // tr,),
        in_specs=[pl.BlockSpec((n, tr, cols), lambda i: (0, i, 0))] + [spec] * 3, out_specs=[spec] * 4,
        out_shape=[jax.ShapeDtypeStruct(w.shape, F32)] * 4, compiler_params=_arb(),
    )(parts, w, m, v)


def _pair_add(name, g4, recv, core):
    _, _, rows, cols = g4.shape
    tr = _row_block(rows, cols)

    def body(c_ref, a_ref, b_ref, o_ref):
        o_ref[...] = (a_ref[...] + b_ref[...]).astype(BF16)

    return pl.pallas_call(
        body, name=name,
        grid_spec=pltpu.PrefetchScalarGridSpec(
            num_scalar_prefetch=1, grid=(4, rows // tr),
            in_specs=[pl.BlockSpec((None, None, tr, cols), lambda k, i, c_ref: (k, c_ref[0], i, 0)),
                      pl.BlockSpec((None, tr, cols), lambda k, i, c_ref: (k, i, 0))],
            out_specs=pl.BlockSpec((None, tr, cols), lambda k, i, c_ref: (k, i, 0))),
        out_shape=jax.ShapeDtypeStruct((4, rows, cols), BF16), compiler_params=_arb(2),
    )(core, g4, recv)


def _sum_small(rows_all, p_all, ws_all, bst_all, fw_all, cw_all):
    def body(a_ref, p_ref, ws_ref, bst_ref, fw_ref, cw_ref,
             g_b_ada, g_n1, g_mog, g_n2, g_gf, g_cb, g_clg, g_clb, g_glg, g_glb, g_ws, g_bs, fw_sum, cw_sum):
        def total(ref):
            t = ref[0]
            for k in range(1, NDEV):
                t = t + ref[k]
            return t

        a = total(a_ref)
        g_b_ada[...] = jnp.concatenate([a[k:k + 1, :] for k in range(6)], axis=1)
        g_n1[...] = a[6:7, :]
        g_mog[...] = a[7:8, :]
        g_n2[...] = a[8:9, :]
        g_gf[...] = a[9:10, :].reshape(D)
        p = total(p_ref)
        for k, ref in zip((CB, CLG, CLB, GLG, GLB), (g_cb, g_clg, g_clb, g_glg, g_glb)):
            ref[...] = p[k:k + 1, :]
        g_ws[0] = total(ws_ref)
        g_bs[0] = jnp.transpose(total(bst_ref))[:NH, :]
        fw_sum[...] = total(fw_ref)
        cw_sum[...] = total(cw_ref)

    vec = lambda n: jax.ShapeDtypeStruct((1, n), F32)
    return pl.pallas_call(
        body, name="sum_small_grads",
        out_shape=[vec(6 * D), vec(D), vec(D), vec(D), jax.ShapeDtypeStruct((D,), F32),
                   vec(DC), vec(DC), vec(DC), vec(DC), vec(DC),
                   jax.ShapeDtypeStruct((1, NH, CHUNK, CHUNK), F32), jax.ShapeDtypeStruct((1, NH, CHUNK), F32),
                   jax.ShapeDtypeStruct((8, 2 * PFF), F32), jax.ShapeDtypeStruct((32, DC), F32)],
    )(rows_all, p_all, ws_all, bst_all, fw_all, cw_all)


def _adam_small(quads):
    n = len(quads)

    def body(*refs):
        ins, outs = refs[:4 * n], refs[4 * n:]
        for q in range(n):
            w, g, m, v = (r[...] for r in ins[4 * q:4 * q + 4])
            outs[3 * q][...], outs[3 * q + 1][...], outs[3 * q + 2][...] = _adam_math(w, g, m, v)

    flat = [a for q in quads for a in q]
    outs = pl.pallas_call(
        body, name="adam_small",
        out_shape=[jax.ShapeDtypeStruct(q[0].shape, F32) for q in quads for _ in range(3)],
    )(*flat)
    return [tuple(outs[3 * q:3 * q + 3]) for q in range(n)]


def kernel(x, c, w_ada, b_ada, norm1_gain, w_in, conv_dw_w, conv_dw_b, conv_ln_g, conv_ln_b, gm_ln_g, gm_ln_b, gm_ws, gm_bs, mix_out_gain, w_out, norm2_gain, w_up, ffn_dw_w, ffn_dw_b, w_down, final_gain, loss_target, m_w_ada, m_b_ada, m_norm1_gain, m_w_in, m_conv_dw_w, m_conv_dw_b, m_conv_ln_g, m_conv_ln_b, m_gm_ln_g, m_gm_ln_b, m_gm_ws, m_gm_bs, m_mix_out_gain, m_w_out, m_norm2_gain, m_w_up, m_ffn_dw_w, m_ffn_dw_b, m_w_down, m_final_gain, v_w_ada, v_b_ada, v_norm1_gain, v_w_in, v_conv_dw_w, v_conv_dw_b, v_conv_ln_g, v_conv_ln_b, v_gm_ln_g, v_gm_ln_b, v_gm_ws, v_gm_bs, v_mix_out_gain, v_w_out, v_norm2_gain, v_w_up, v_ffn_dw_w, v_ffn_dw_b, v_w_down, v_final_gain):
    s = x.shape[1]
    ax, ay, ac = _place()
    me = 4 * ax + 2 * ay + ac
    n_ada = w_ada.shape[2]
    n_cw = conv_dw_w.shape[2]
    x2d = x[0]
    target = loss_target[0]
    pad_sh = lambda a: jnp.pad(a, [(0, 0)] * (a.ndim - 1) + [(0, PSH - NSH)])

    gather_a, token_a = _start_gather("gather_in_out", [w_in[0].astype(BF16), w_out[0].astype(BF16)], me)
    gather_b, token_b = _start_gather("gather_up_down", [pad_sh(w_up[0].astype(BF16)), w_down[0].astype(BF16)], me)

    c_all, cw_all, fw_all = _all_gather("gather_small", [c + (token_a[0, 0] + token_b[0, 0]), conv_dw_w[0], ffn_dw_w[0]])
    conv_w = jnp.pad(jnp.transpose(cw_all, (1, 0, 2)).reshape(KC, DC), ((0, 32 - KC), (0, 0)))
    ffn_w = jnp.transpose(pad_sh(fw_all), (1, 0, 2)).reshape(KF, 2 * PFF)
    ffn_b = pad_sh(ffn_dw_b.reshape(NDEV, NSH)).reshape(1, 2 * PFF)
    ffn_wb = jnp.concatenate([ffn_w, ffn_b, jnp.zeros((8 - KF - 1, 2 * PFF), F32)], axis=0)

    b_cols = lax.dynamic_slice(b_ada, (0, me * n_ada), (1, n_ada))
    (mod_all,) = _all_gather("gather_mod", [_mod_part(c_all, w_ada, b_cols)])
    mod = lax.dynamic_index_in_dim(mod_all, me, axis=1, keepdims=False).reshape(6, D)
    sh1, sc1, gt1, sh2, sc2, gt2 = [mod[k:k + 1] for k in range(6)]
    vecs = jnp.concatenate([norm1_gain, sh1, sc1, gt1, norm2_gain, sh2, sc2, gt2, mix_out_gain,
                            final_gain.reshape(1, D), jnp.zeros((6, D), F32)], axis=0)
    v512 = jnp.concatenate([conv_dw_b, conv_ln_g, conv_ln_b, gm_ln_g, gm_ln_b, jnp.zeros((3, DC), F32)], axis=0)
    bs_exp = jnp.repeat(jnp.transpose(gm_bs[0]), HD, axis=1)
    gm_ws_t = jnp.swapaxes(gm_ws[0], 1, 2)

    tm_in = min(512, s)
    tm = min(256, s)
    w_in_g, w_out_g = _finish_gather("gather_in_out", gather_a, vecs)
    w_out_b = w_out_g.reshape(D, D)
    z, a0, h1_t = _fwd_in(x2d, vecs, w_in_g, tm_in)
    a1, sp, x2, o1, h2, h2_t = _fwd_mid(a0, z, x2d, vecs, v512, conv_w, gm_ws, bs_exp, w_out_b, tm)
    w_up_g, w_down_g = _finish_gather("gather_up_down", gather_b, h2)
    w_down_p = jnp.pad(w_down_g.reshape(4, NSH, D), ((0, 0), (0, PSH - NSH), (0, 0)))
    up_pre, f_t, o2, dx3, acc_f = _fwd_ffn(h2, x2, target, vecs, ffn_wb, w_up_g, w_down_p, tm)
    loss = lax.psum(jnp.sum(acc_f[1]), ("x", "y", "c"))

    core = ac.reshape(1).astype(jnp.int32)
    mychip = 2 * ax + ay

    def to_pairs(named):
        g4s = [g.reshape((4, 2) + g.shape[1:]) for _, g in named]
        from_sibling = _sibling_swap("rs_sibling_" + named[0][0], g4s)
        return [_pair_add("rs_pair_add_" + t[0], g4, rv, core) for t, g4, rv in zip(named, g4s, from_sibling)]

    dx2, dup_pre, do2, acc_b, acc_fw = _bwd_ffn(dx3, o2, x2, up_pre, vecs, ffn_wb, w_up_g, w_down_p, tm)
    dw_up = _mm_col_slots("dw_up", h2_t, dup_pre, PSH, NSH, min(2048, s))
    dw_down = _mm_row_slots("dw_down", f_t, do2, PSH, NSH, min(2048, s)).reshape(NDEV, w_down.shape[1], D)
    exchange_ffn, token_x = _start_exchange("rs_chips_ffn", to_pairs([("w_up", dw_up), ("w_down", dw_down)]), mychip)
    gx, dz, y_t, do1, acc_m, acc_p, dcw, dws, dbs_t = _bwd_mid(
        dx2, x2d, o1, z, a0, a1, sp, vecs + token_x[0, 0], v512, conv_w, gm_ws[0], gm_ws_t, w_out_b, w_in_g, tm)
    dw_in = _mm_all_slots("dw_in", h1_t, dz, w_in.shape[2], min(1024, s))
    dw_out = _mm_all_slots("dw_out", y_t, do1, D, min(2048, s)).reshape(NDEV, w_out.shape[1], D)

    rows = jnp.concatenate([acc_m[1:3], acc_m[0:1], acc_b[1:3], acc_b[0:1], acc_m[3:5], acc_b[3:4], acc_f[0:1],
                            jnp.zeros((6, D), F32)], axis=0)
    rows_all, p_all, ws_all, bst_all, fwg_all, cwg_all = _all_gather(
        "gather_small_grads", [rows, acc_p, dws, dbs_t, acc_fw, dcw])
    (g_b_ada, g_n1, g_mog, g_n2, g_gf, g_cb, g_clg, g_clb, g_glg, g_glb, g_ws, g_bs, fw_sum, cw_sum) = _sum_small(
        rows_all, p_all, ws_all, bst_all, fwg_all, cwg_all)
    g_fb = fw_sum[3].reshape(NDEV, PSH)[:, :NSH].reshape(ffn_dw_b.shape)
    g_fw = lax.dynamic_index_in_dim(fw_sum[:KF].reshape(KF, NDEV, PSH), me, axis=1, keepdims=False)[:, :NSH]
    g_fw = g_fw.reshape(ffn_dw_w.shape)
    g_cw = lax.dynamic_slice(cw_sum, (0, me * n_cw), (KC, n_cw)).reshape(conv_dw_w.shape)
    small = [
        (b_ada, g_b_ada, m_b_ada, v_b_ada), (norm1_gain, g_n1, m_norm1_gain, v_norm1_gain),
        (conv_dw_w, g_cw, m_conv_dw_w, v_conv_dw_w), (conv_dw_b, g_cb, m_conv_dw_b, v_conv_dw_b),
        (conv_ln_g, g_clg, m_conv_ln_g, v_conv_ln_g), (conv_ln_b, g_clb, m_conv_ln_b, v_conv_ln_b),
        (gm_ln_g, g_glg, m_gm_ln_g, v_gm_ln_g), (gm_ln_b, g_glb, m_gm_ln_b, v_gm_ln_b),
        (gm_ws, g_ws, m_gm_ws, v_gm_ws), (gm_bs, g_bs, m_gm_bs, v_gm_bs),
        (mix_out_gain, g_mog, m_mix_out_gain, v_mix_out_gain), (norm2_gain, g_n2, m_norm2_gain, v_norm2_gain),
        (ffn_dw_w, g_fw, m_ffn_dw_w, v_ffn_dw_w), (ffn_dw_b, g_fb, m_ffn_dw_b, v_ffn_dw_b),
        (final_gain, g_gf, m_final_gain, v_final_gain)]
    small_out = _adam_small(small)
    res = {}
    for name, q, o in zip(("b_ada", "norm1_gain", "conv_dw_w", "conv_dw_b", "conv_ln_g", "conv_ln_b", "gm_ln_g",
                           "gm_ln_b", "gm_ws", "gm_bs", "mix_out_gain", "norm2_gain", "ffn_dw_w", "ffn_dw_b",
                           "final_gain"), small, small_out):
        res[name] = (q[1],) + o

    dmod_all = rows_all[:, :6].reshape(NDEV, 6 * D)
    dm_cols = lax.dynamic_slice(dmod_all, (0, me * n_ada), (NDEV, n_ada))
    g_ada = _ada_grad(jnp.transpose(c_all[:, 0, :]), dm_cols)
    res["w_ada"] = (g_ada,) + tuple(_adam3("adam_ada", w_ada, g_ada, m_w_ada, v_w_ada))

    from_chips = list(_chip_exchange("rs_chips_mix", to_pairs([("w_in", dw_in), ("w_out", dw_out)])))
    from_chips += list(_finish_exchange("rs_chips_ffn", exchange_ffn, gx))
    big = [("w_in", w_in, m_w_in, v_w_in), ("w_out", w_out, m_w_out, v_w_out),
           ("w_up", w_up, m_w_up, v_w_up), ("w_down", w_down, m_w_down, v_w_down)]
    for t, parts in zip(big, from_chips):
        res[t[0]] = tuple(_sum_adam("rs_sum_adam_" + t[0], parts, t[1], t[2], t[3]))

    order = ("w_ada", "b_ada", "norm1_gain", "w_in", "conv_dw_w", "conv_dw_b", "conv_ln_g", "conv_ln_b", "gm_ln_g",
             "gm_ln_b", "gm_ws", "gm_bs", "mix_out_gain", "w_out", "norm2_gain", "w_up", "ffn_dw_w", "ffn_dw_b",
             "w_down", "final_gain")
    return (loss, gx.reshape(x.shape), *[res[n][0] for n in order], *[res[n][1] for n in order],
            *[res[n][2] for n in order], *[res[n][3] for n in order])
```

```python
import jax
import jax.numpy as jnp
from jax import lax
from jax.experimental import pallas as pl
from jax.experimental.pallas import tpu as pltpu

F32 = jnp.float32
BF16 = jnp.bfloat16
NDEV = 8
D = 1024
DC = 512
DFF = 2816
NSH = 704
PSH = 768
PFF = 4 * PSH
KC = 31
KF = 3
CHUNK = 128
NH = 8
HD = 64
HALO_C = 32
HALO_F = 8
LANES = 128
RMS_EPS = 1e-6
LN_EPS = 1e-5
ADAM_LR = 0.001
ADAM_B1 = 0.9
ADAM_B2 = 0.999
ADAM_EPS = 1e-08
ADAM_WD = 0.01
ADAM_STEP = 10
GELU_K = 0.7978845608028654
GELU_C = 0.044715

MESH = pl.DeviceIdType.MESH
ANY = pl.BlockSpec(memory_space=pl.ANY)

G1, SH1, SC1, GT1, G2, SH2, SC2, GT2, MOG, GF = range(10)
CB, CLG, CLB, GLG, GLB = range(5)


def _full(shape):
    return pl.BlockSpec(shape, lambda *_: (0,) * len(shape))


def _arb(n=1):
    return pltpu.CompilerParams(dimension_semantics=("arbitrary",) * n)


def _row(ref, r):
    return ref[pl.ds(r, 1), :]


def _colsum(v):
    return jnp.sum(v, axis=0, keepdims=True)


def _rowmean(v):
    return jnp.mean(v, axis=-1, keepdims=True)


def _rms(x):
    r = lax.rsqrt(_rowmean(x * x) + RMS_EPS)
    return x * r, r


def _rms_bwd(dxn, xn, r):
    return r * (dxn - xn * _rowmean(dxn * xn))


def _ln(x):
    mu = _rowmean(x)
    xc = x - mu
    rstd = lax.rsqrt(_rowmean(xc * xc) + LN_EPS)
    return xc * rstd, rstd


def _ln_bwd(dxh, xhat, rstd):
    return rstd * (dxh - _rowmean(dxh) - xhat * _rowmean(dxh * xhat))


def _sigmoid(x):
    return 0.5 * jnp.tanh(0.5 * x) + 0.5


def _gelu(x):
    t = jnp.tanh(GELU_K * (x + GELU_C * x * x * x))
    return 0.5 * x * (1.0 + t), t


def _gelu_grad(x, t):
    return 0.5 * (1.0 + t) + 0.5 * x * (1.0 - t * t) * (GELU_K * (1.0 + 3.0 * GELU_C * x * x))


def _dot(a, b):
    return jnp.dot(a, b, preferred_element_type=F32)


def _dot_nt(a, b):
    return lax.dot_general(a, b, (((1,), (1,)), ((), ())), preferred_element_type=F32)


def _dot_tn(a, b):
    return lax.dot_general(a, b, (((0,), (0,)), ((), ())), preferred_element_type=F32)


def _shift_up(e, s):
    n = e.shape[0]
    return pltpu.roll(e, (n - s) % n, 0)


def _place():
    return lax.axis_index("x"), lax.axis_index("y"), lax.axis_index("c")


def _all_gather(name, xs):
    n = len(xs)

    def body(*refs):
        x_refs, out_refs = refs[:n], refs[n:2 * n]
        send_sems, recv_sems, local_sems = refs[2 * n:]
        x, y, c = _place()
        me, sibling = (x, y, c), (x, y, 1 - c)
        chips = [(1 - x, y), (x, 1 - y), (1 - x, 1 - y)]

        def copy(a, k, block, to, own=False):
            px, py, pc = block
            slot = out_refs[a].at[4 * px + 2 * py + pc]
            return pltpu.make_async_remote_copy(
                src_ref=x_refs[a] if own else slot, dst_ref=slot,
                send_sem=send_sems.at[7 * a + k], recv_sem=recv_sems.at[7 * a + k], device_id=to, device_id_type=MESH)

        mine = [pltpu.make_async_copy(x_refs[a], out_refs[a].at[4 * x + 2 * y + c], local_sems.at[a]) for a in range(n)]
        for cp in mine:
            cp.start()
        first = []
        for a in range(n):
            first.append(copy(a, 0, me, sibling, own=True))
            first += [copy(a, 1 + j, me, (*chip, c), own=True) for j, chip in enumerate(chips)]
        for cp in first:
            cp.start()
        passed = []
        for j, chip in enumerate(chips):
            for a in range(n):
                copy(a, 1 + j, (*chip, c), me).wait_recv()
                cp = copy(a, 4 + j, (*chip, c), sibling)
                cp.start()
                passed.append(cp)
        for a in range(n):
            copy(a, 0, sibling, me).wait_recv()
            for j, chip in enumerate(chips):
                copy(a, 4 + j, (*chip, 1 - c), me).wait_recv()
        for cp in first + passed:
            cp.wait_send()
        for cp in mine:
            cp.wait()

    return pl.pallas_call(
        body, name=name, out_shape=[jax.ShapeDtypeStruct((NDEV,) + a.shape, a.dtype) for a in xs],
        in_specs=[ANY] * n, out_specs=[ANY] * n,
        scratch_shapes=[pltpu.SemaphoreType.DMA((7 * n,)), pltpu.SemaphoreType.DMA((7 * n,)),
                        pltpu.SemaphoreType.DMA((n,))],
    )(*xs)


def _sibling_swap(name, g4s):
    n = len(g4s)

    def body(*refs):
        g_refs, out_refs = refs[:n], refs[n:2 * n]
        send_sems, recv_sems = refs[2 * n:]
        x, y, c = _place()
        cps = [pltpu.make_async_remote_copy(
            src_ref=g_refs[a].at[k, 1 - c], dst_ref=out_refs[a].at[k],
            send_sem=send_sems.at[4 * a + k], recv_sem=recv_sems.at[4 * a + k],
            device_id=(x, y, 1 - c), device_id_type=MESH) for a in range(n) for k in range(4)]
        for cp in cps:
            cp.start()
        for cp in cps:
            cp.wait()

    return pl.pallas_call(
        body, name=name, out_shape=[jax.ShapeDtypeStruct((4,) + g.shape[2:], g.dtype) for g in g4s],
        in_specs=[ANY] * n, out_specs=[ANY] * n,
        scratch_shapes=[pltpu.SemaphoreType.DMA((4 * n,)), pltpu.SemaphoreType.DMA((4 * n,))],
    )(*g4s)


def _chip_exchange(name, hs):
    n = len(hs)

    def body(*refs):
        h_refs, out_refs = refs[:n], refs[n:2 * n]
        send_sems, recv_sems, local_sems = refs[2 * n:]
        x, y, c = _place()
        mychip = 2 * x + y
        chips = [(1 - x, y), (x, 1 - y), (1 - x, 1 - y)]

        def copy(a, j, arriving):
            px, py = chips[j]
            theirs = 2 * px + py
            return pltpu.make_async_remote_copy(
                src_ref=h_refs[a].at[mychip if arriving else theirs],
                dst_ref=out_refs[a].at[theirs if arriving else mychip],
                send_sem=send_sems.at[3 * a + j], recv_sem=recv_sems.at[3 * a + j],
                device_id=(px, py, c), device_id_type=MESH)

        mine = [pltpu.make_async_copy(h_refs[a].at[mychip], out_refs[a].at[mychip], local_sems.at[a]) for a in range(n)]
        for cp in mine:
            cp.start()
        cps = [copy(a, j, False) for a in range(n) for j in range(3)]
        for cp in cps:
            cp.start()
        for a in range(n):
            for j in range(3):
                copy(a, j, True).wait_recv()
        for cp in cps:
            cp.wait_send()
        for cp in mine:
            cp.wait()

    return pl.pallas_call(
        body, name=name, out_shape=[jax.ShapeDtypeStruct(h.shape, h.dtype) for h in hs],
        in_specs=[ANY] * n, out_specs=[ANY] * n,
        scratch_shapes=[pltpu.SemaphoreType.DMA((3 * n,)), pltpu.SemaphoreType.DMA((3 * n,)),
                        pltpu.SemaphoreType.DMA((n,))],
    )(*hs)


HBM = pl.BlockSpec(memory_space=pltpu.HBM)
SEM = pl.BlockSpec(memory_space=pltpu.SEMAPHORE)
EFFECT = pltpu.SideEffectType.DATAFLOW_SIDE_EFFECTING


def _in_hbm(a):
    return pltpu.with_memory_space_constraint(a, pltpu.HBM)


def _split_start(name, bufs, copies):
    n = len(bufs)

    def body(*refs):
        for cp in copies(refs[:n], refs[n], refs[n + 1]):
            cp.start()
        refs[-1][...] = jnp.zeros_like(refs[-1])

    out = pl.pallas_call(
        body, name=name,
        out_shape=(pltpu.SemaphoreType.DMA((copies.count,)), pltpu.SemaphoreType.DMA((copies.count,)),
                   *[pltpu.HBM(a.shape, a.dtype) for a in bufs], jax.ShapeDtypeStruct((8, LANES), F32)),
        in_specs=[HBM] * n, out_specs=(SEM, SEM, *[HBM] * n, pl.BlockSpec(memory_space=pltpu.VMEM)),
        input_output_aliases={i: 2 + i for i in range(n)},
        compiler_params=pltpu.CompilerParams(has_side_effects=EFFECT),
    )(*[_in_hbm(a) for a in bufs])
    return (out[0], out[1], list(out[2:2 + n])), out[-1]


def _split_wait(name, handle, copies, after):
    send_sems, recv_sems, bufs = handle
    n = len(bufs)

    def body(*refs):
        for cp in copies(refs[:n], refs[n], refs[n + 1]):
            cp.wait_send()
            cp.wait_recv()

    out = pl.pallas_call(
        body, name=name, out_shape=tuple(pltpu.HBM(a.shape, a.dtype) for a in bufs),
        in_specs=[HBM] * n + [SEM, SEM, pl.BlockSpec(memory_space=pl.ANY)], out_specs=tuple([HBM] * n),
        input_output_aliases={i: i for i in range(n)},
        compiler_params=pltpu.CompilerParams(has_side_effects=EFFECT),
    )(*bufs, send_sems, recv_sems, after)
    return list(out)


class _GatherFirstCopies:
    def __init__(self, n):
        self.n, self.count = n, 4 * n

    def __call__(self, refs, send_sems, recv_sems):
        x, y, c = _place()
        peers = [(x, y, 1 - c), (1 - x, y, c), (x, 1 - y, c), (1 - x, 1 - y, c)]
        return [pltpu.make_async_remote_copy(
            src_ref=refs[a], dst_ref=refs[self.n + a].at[4 * x + 2 * y + c],
            send_sem=send_sems.at[4 * a + k], recv_sem=recv_sems.at[4 * a + k], device_id=peer, device_id_type=MESH)
            for a in range(self.n) for k, peer in enumerate(peers)]


class _GatherDirectCopies:
    def __init__(self, n):
        self.n, self.count = n, 7 * n

    def __call__(self, refs, send_sems, recv_sems):
        x, y, c = _place()
        flip = lambda v, bit: 1 - v if bit else v
        return [pltpu.make_async_remote_copy(
            src_ref=refs[a], dst_ref=refs[self.n + a].at[4 * x + 2 * y + c],
            send_sem=send_sems.at[7 * a + r - 1], recv_sem=recv_sems.at[7 * a + r - 1],
            device_id=(flip(x, r & 4), flip(y, r & 2), flip(c, r & 1)), device_id_type=MESH)
            for a in range(self.n) for r in range(1, NDEV)]


class _GatherPassCopies:
    def __init__(self, n):
        self.n, self.count = n, 3 * n

    def __call__(self, refs, send_sems, recv_sems):
        x, y, c = _place()
        cps = []
        for a in range(self.n):
            for j, (px, py) in enumerate([(1 - x, y), (x, 1 - y), (1 - x, 1 - y)]):
                slot = refs[a].at[4 * px + 2 * py + c]
                cps.append(pltpu.make_async_remote_copy(
                    src_ref=slot, dst_ref=slot, send_sem=send_sems.at[3 * a + j], recv_sem=recv_sems.at[3 * a + j],
                    device_id=(x, y, 1 - c), device_id_type=MESH))
        return cps


class _ExchangeCopies:
    def __init__(self, n):
        self.n, self.count = n, 3 * n

    def __call__(self, refs, send_sems, recv_sems):
        x, y, c = _place()
        cps = []
        for a in range(self.n):
            for j, (px, py) in enumerate([(1 - x, y), (x, 1 - y), (1 - x, 1 - y)]):
                cps.append(pltpu.make_async_remote_copy(
                    src_ref=refs[a].at[2 * px + py], dst_ref=refs[self.n + a].at[2 * x + y],
                    send_sem=send_sems.at[3 * a + j], recv_sem=recv_sems.at[3 * a + j],
                    device_id=(px, py, c), device_id_type=MESH))
        return cps


def _own_slot(nslot, src, index):
    land = lax.empty((nslot,) + src.shape, src.dtype)
    return lax.dynamic_update_slice(land, src[None], (index,) + (0,) * src.ndim)


def _start_gather(tag, xs, me):
    lands = [_own_slot(NDEV, a, me) for a in xs]
    return _split_start(tag + "_start", list(xs) + lands, _GatherFirstCopies(len(xs)))


def _finish_gather(tag, handle, after):
    n = len(handle[2]) // 2
    lands = _split_wait(tag + "_wait", handle, _GatherFirstCopies(n), after)[n:]
    passing, token = _split_start(tag + "_pass", lands, _GatherPassCopies(n))
    return _split_wait(tag + "_pass_wait", passing, _GatherPassCopies(n), token)


def _start_direct_gather(tag, xs, me):
    lands = [_own_slot(NDEV, a, me) for a in xs]
    return _split_start(tag + "_start", list(xs) + lands, _GatherDirectCopies(len(xs)))


def _finish_direct_gather(tag, handle, after):
    n = len(handle[2]) // 2
    return _split_wait(tag + "_wait", handle, _GatherDirectCopies(n), after)[n:]


def _start_exchange(tag, hs, mychip):
    lands = [_own_slot(4, lax.dynamic_index_in_dim(h, mychip, 0, keepdims=False), mychip) for h in hs]
    return _split_start(tag + "_start", list(hs) + lands, _ExchangeCopies(len(hs)))


def _finish_exchange(tag, handle, after):
    n = len(handle[2]) // 2
    return _split_wait(tag + "_wait", handle, _ExchangeCopies(n), after)[n:]


def _mod_part(c_all, w_ada, b_cols):
    ncol = w_ada.shape[2]

    def body(c_ref, w_ref, b_ref, o_ref):
        cv = c_ref[:, 0, :]
        ca = cv * _sigmoid(cv)
        o_ref[...] = _dot(ca.astype(BF16), w_ref[0].astype(BF16)) + b_ref[...]

    return pl.pallas_call(body, name="mod_part", out_shape=jax.ShapeDtypeStruct((NDEV, ncol), F32))(
        c_all, w_ada, b_cols)


def _ada_grad(c_all_t, dmod_cols):
    ncol = dmod_cols.shape[1]

    def body(ct_ref, dm_ref, o_ref):
        ct = ct_ref[...]
        ca = ct * _sigmoid(ct)
        acc = jnp.zeros((D, ncol), F32)
        for b in range(NDEV):
            acc = acc + ca[:, b:b + 1] * dm_ref[pl.ds(b, 1), :]
        o_ref[0] = acc

    return pl.pallas_call(body, name="ada_grad", out_shape=jax.ShapeDtypeStruct((1, D, ncol), F32))(
        c_all_t, dmod_cols)


def _fwd_in(x2d, vecs, w_in_g, tm):
    s = x2d.shape[0]
    nc = w_in_g.shape[2]

    def body(x_ref, v_ref, w_ref, z_ref, a0_ref, h1t_ref):
        xn, _ = _rms(x_ref[...])
        h = (xn * _row(v_ref, G1)) * (1.0 + _row(v_ref, SC1)) + _row(v_ref, SH1)
        hb = h.astype(BF16)
        h1t_ref[...] = hb.T
        for d in range(NDEV):
            z_ref[:, pl.ds(d * nc, nc)] = _dot(hb, w_ref[d])
        a0_ref[...] = z_ref[:, :DC] * _sigmoid(z_ref[:, DC:2 * DC])

    return pl.pallas_call(
        body, name="fwd_in", grid=(s // tm,),
        in_specs=[pl.BlockSpec((tm, D), lambda i: (i, 0)), _full((16, D)), _full((NDEV, D, nc))],
        out_specs=[pl.BlockSpec((tm, 4 * DC), lambda i: (i, 0)), pl.BlockSpec((tm, DC), lambda i: (i, 0)),
                   pl.BlockSpec((D, tm), lambda i: (0, i))],
        out_shape=[jax.ShapeDtypeStruct((s, 4 * DC), F32), jax.ShapeDtypeStruct((s, DC), F32),
                   jax.ShapeDtypeStruct((D, s), BF16)],
        compiler_params=_arb(),
    )(x2d, vecs, w_in_g)


def _causal_mask(lower):
    r = lax.broadcasted_iota(jnp.int32, (CHUNK, CHUNK), 0)
    c = lax.broadcasted_iota(jnp.int32, (CHUNK, CHUNK), 1)
    return (r >= c) if lower else (r <= c)


def _first_head_lanes():
    return lax.broadcasted_iota(jnp.int32, (CHUNK, CHUNK), 1) < HD


def _fwd_mid(a0, z, x2d, vecs, v512, conv_w, gm_ws, bs_exp, w_out_b, tm):
    s = x2d.shape[0]
    hb = tm // HALO_C

    def body(a0_ref, halo_ref, zg_ref, x_ref, v_ref, p_ref, cw_ref, ws_ref, bs_ref, wo_ref,
             a1_ref, sp_ref, x2_ref, o1_ref, h2_ref, h2t_ref):
        i = pl.program_id(0)
        for c0 in range(0, DC, LANES):
            cols = pl.ds(c0, LANES)
            halo = halo_ref[:, cols]
            e = jnp.concatenate([jnp.where(i > 0, halo, jnp.zeros_like(halo)), a0_ref[:, cols]], axis=0)
            acc = jnp.broadcast_to(p_ref[pl.ds(CB, 1), cols], (tm, LANES))
            for k in range(KC):
                acc = acc + _shift_up(e, HALO_C - (KC - 1) + k)[:tm, :] * cw_ref[pl.ds(k, 1), cols]
            a1_ref[:, cols] = acc
        xh, _ = _ln(a1_ref[...])
        a2 = xh * _row(p_ref, CLG) + _row(p_ref, CLB)
        a3 = a2 * _sigmoid(a2)
        gu, _ = _gelu(zg_ref[:, :DC])
        gvg, _ = _gelu(zg_ref[:, DC:])
        vh, _ = _ln(gvg)
        gvn = (vh * _row(p_ref, GLG) + _row(p_ref, GLB)).astype(BF16)
        low = _causal_mask(True)
        first = _first_head_lanes()
        wm = [jnp.where(low, ws_ref[0, h], 0.0).astype(BF16) for h in range(NH)]
        for n in range(tm // CHUNK):
            for p in range(NH // 2):
                v = gvn[n * CHUNK:(n + 1) * CHUNK, p * CHUNK:(p + 1) * CHUNK]
                blk = jnp.where(first, _dot(wm[2 * p], v), _dot(wm[2 * p + 1], v))
                sp_ref[pl.ds(n * CHUNK, CHUNK), pl.ds(p * CHUNK, CHUNK)] = blk + bs_ref[:, pl.ds(p * CHUNK, CHUNK)]
        g = gu * sp_ref[...]
        an, _ = _rms(a3)
        gn, _ = _rms(g)
        mog = _row(v_ref, MOG)
        y = jnp.concatenate([an * mog[:, :DC], gn * mog[:, DC:]], axis=1).astype(BF16)
        o1 = _dot(y, wo_ref[...])
        o1_ref[...] = o1
        x2 = x_ref[...] + _row(v_ref, GT1) * o1
        x2_ref[...] = x2
        xn2, _ = _rms(x2)
        h2 = (xn2 * _row(v_ref, G2)) * (1.0 + _row(v_ref, SC2)) + _row(v_ref, SH2)
        h2b = h2.astype(BF16)
        h2_ref[...] = h2b
        h2t_ref[...] = h2b.T

    tile = lambda w: pl.BlockSpec((tm, w), lambda i: (i, 0))
    return pl.pallas_call(
        body, name="fwd_mid", grid=(s // tm,),
        in_specs=[tile(DC), pl.BlockSpec((HALO_C, DC), lambda i: (jnp.maximum(i * hb - 1, 0), 0)),
                  pl.BlockSpec((tm, 2 * DC), lambda i: (i, 1)), tile(D), _full((16, D)), _full((8, DC)),
                  _full((32, DC)), _full((1, NH, CHUNK, CHUNK)), _full((CHUNK, DC)), _full((D, D))],
        out_specs=[tile(DC), tile(DC), tile(D), tile(D), tile(D), pl.BlockSpec((D, tm), lambda i: (0, i))],
        out_shape=[jax.ShapeDtypeStruct((s, DC), F32), jax.ShapeDtypeStruct((s, DC), F32),
                   jax.ShapeDtypeStruct((s, D), F32), jax.ShapeDtypeStruct((s, D), F32),
                   jax.ShapeDtypeStruct((s, D), BF16), jax.ShapeDtypeStruct((D, s), BF16)],
        compiler_params=_arb(),
    )(a0, a0, z, x2d, vecs, v512, conv_w, gm_ws, bs_exp, w_out_b)


def _ffn_conv(fw_ref, cols, p2, p1, pre):
    return (fw_ref[pl.ds(3, 1), cols] + fw_ref[pl.ds(0, 1), cols] * p2
            + fw_ref[pl.ds(1, 1), cols] * p1 + fw_ref[pl.ds(2, 1), cols] * pre)


def _fwd_ffn(h2, x2, target, vecs, ffn_wb, w_up_g, w_down_p, tm):
    s = x2.shape[0]

    def body(h2_ref, x2_ref, t_ref, v_ref, fw_ref, wu_hbm, wd_hbm,
             up_ref, ft_ref, o2_ref, dx3_ref, acc_ref, wu, wd, carry):
        i = pl.program_id(0)

        @pl.when(i == 0)
        def _():
            pltpu.sync_copy(wu_hbm, wu)
            pltpu.sync_copy(wd_hbm, wd)
            carry[...] = jnp.zeros_like(carry)
            acc_ref[...] = jnp.zeros_like(acc_ref)

        h2v = h2_ref[...]
        o2 = jnp.zeros((tm, D), F32)
        for j in range(4):
            conv = []
            for sh in (j, 4 + j):
                cols = pl.ds(sh * PSH, PSH)
                pre = _dot(h2v, wu[sh])
                up_ref[:, cols] = pre
                e = jnp.concatenate([carry[:, cols], pre], axis=0)
                carry[:, cols] = pre[tm - HALO_F:, :]
                conv.append(_ffn_conv(fw_ref, cols, pltpu.roll(e, 2, 0)[HALO_F:, :],
                                      pltpu.roll(e, 1, 0)[HALO_F:, :], pre))
            val, gate = conv
            f = ((gate * _sigmoid(gate)) * val).astype(BF16)
            ft_ref[pl.ds(j * PSH, PSH), :] = f.T
            o2 = o2 + _dot(f, wd[j])
        o2_ref[...] = o2
        x3 = x2_ref[...] + _row(v_ref, GT2) * o2
        xn3, r3 = _rms(x3)
        gf = _row(v_ref, GF)
        diff = xn3 * gf - t_ref[...]
        acc_ref[pl.ds(1, 1), :] += _colsum(diff * diff) * (0.5 / D)
        dout = diff * (1.0 / D)
        acc_ref[pl.ds(0, 1), :] += _colsum(dout * xn3)
        dx3_ref[...] = _rms_bwd(dout * gf, xn3, r3)

    tile = lambda w: pl.BlockSpec((tm, w), lambda i: (i, 0))
    return pl.pallas_call(
        body, name="fwd_ffn", grid=(s // tm,),
        in_specs=[tile(D), tile(D), tile(D), _full((16, D)), _full((8, 2 * PFF)), ANY, ANY],
        out_specs=[tile(2 * PFF), pl.BlockSpec((PFF, tm), lambda i: (0, i)), tile(D), tile(D), _full((8, D))],
        out_shape=[jax.ShapeDtypeStruct((s, 2 * PFF), F32), jax.ShapeDtypeStruct((PFF, s), BF16),
                   jax.ShapeDtypeStruct((s, D), F32), jax.ShapeDtypeStruct((s, D), F32),
                   jax.ShapeDtypeStruct((8, D), F32)],
        scratch_shapes=[pltpu.VMEM((NDEV, D, PSH), BF16), pltpu.VMEM((4, PSH, D), BF16),
                        pltpu.VMEM((HALO_F, 2 * PFF), F32)],
        compiler_params=_arb(),
    )(h2, x2, target, vecs, ffn_wb, w_up_g, w_down_p)


def _bwd_ffn(dx3, o2, x2, up_pre, vecs, ffn_wb, w_up_g, w_down_p, tm):
    s = x2.shape[0]
    nt = s // tm
    hb = tm // HALO_F

    def body(dx3_ref, o2_ref, x2_ref, up_ref, halo_ref, v_ref, fw_ref, wu_hbm, wd_hbm,
             dx2_ref, dup_ref, do2_ref, acc_ref, accf_ref, wu, wd, carry):
        i = pl.program_id(0)
        r = nt - 1 - i

        @pl.when(i == 0)
        def _():
            pltpu.sync_copy(wu_hbm, wu)
            pltpu.sync_copy(wd_hbm, wd)
            carry[...] = jnp.zeros_like(carry)
            acc_ref[...] = jnp.zeros_like(acc_ref)
            accf_ref[...] = jnp.zeros_like(accf_ref)

        dx3v = dx3_ref[...]
        do2 = (dx3v * _row(v_ref, GT2)).astype(BF16)
        do2_ref[...] = do2
        acc_ref[pl.ds(0, 1), :] += _colsum(dx3v * o2_ref[...])
        dh2 = jnp.zeros((tm, D), F32)
        for j in range(4):
            df = _dot_nt(do2, wd[j])
            shifted, conv = [], []
            for sh in (j, 4 + j):
                cols = pl.ds(sh * PSH, PSH)
                pre = up_ref[:, cols]
                hl = halo_ref[:, cols]
                e = jnp.concatenate([jnp.where(r > 0, hl, jnp.zeros_like(hl)), pre], axis=0)
                p2 = pltpu.roll(e, 2, 0)[HALO_F:, :]
                p1 = pltpu.roll(e, 1, 0)[HALO_F:, :]
                shifted.append((p2, p1, pre))
                conv.append(_ffn_conv(fw_ref, cols, p2, p1, pre))
            val, gate = conv
            sg = _sigmoid(gate)
            dups = (df * (gate * sg), df * val * (sg * (1.0 + gate * (1.0 - sg))))
            for (sh, dup, (p2, p1, pre)) in zip((j, 4 + j), dups, shifted):
                cols = pl.ds(sh * PSH, PSH)
                accf_ref[pl.ds(3, 1), cols] += _colsum(dup)
                accf_ref[pl.ds(0, 1), cols] += _colsum(dup * p2)
                accf_ref[pl.ds(1, 1), cols] += _colsum(dup * p1)
                accf_ref[pl.ds(2, 1), cols] += _colsum(dup * pre)
                e = jnp.concatenate([dup, carry[:, cols]], axis=0)
                carry[:, cols] = dup[:HALO_F, :]
                dpre = (fw_ref[pl.ds(0, 1), cols] * _shift_up(e, 2)[:tm, :]
                        + fw_ref[pl.ds(1, 1), cols] * _shift_up(e, 1)[:tm, :]
                        + fw_ref[pl.ds(2, 1), cols] * dup).astype(BF16)
                dup_ref[:, cols] = dpre
                dh2 = dh2 + _dot_nt(dpre, wu[sh])
        xn2, r2 = _rms(x2_ref[...])
        g2 = _row(v_ref, G2)
        sc = 1.0 + _row(v_ref, SC2)
        acc_ref[pl.ds(1, 1), :] += _colsum(dh2)
        acc_ref[pl.ds(2, 1), :] += _colsum(dh2 * (xn2 * g2))
        acc_ref[pl.ds(3, 1), :] += _colsum(dh2 * sc * xn2)
        dx2_ref[...] = dx3v + _rms_bwd(dh2 * sc * g2, xn2, r2)

    tile = lambda w: pl.BlockSpec((tm, w), lambda i: (nt - 1 - i, 0))
    return pl.pallas_call(
        body, name="bwd_ffn", grid=(nt,),
        in_specs=[tile(D), tile(D), tile(D), tile(2 * PFF),
                  pl.BlockSpec((HALO_F, 2 * PFF), lambda i: (jnp.maximum((nt - 1 - i) * hb - 1, 0), 0)),
                  _full((16, D)), _full((8, 2 * PFF)), ANY, ANY],
        out_specs=[tile(D), tile(2 * PFF), tile(D), _full((8, D)), _full((8, 2 * PFF))],
        out_shape=[jax.ShapeDtypeStruct((s, D), F32), jax.ShapeDtypeStruct((s, 2 * PFF), BF16),
                   jax.ShapeDtypeStruct((s, D), BF16), jax.ShapeDtypeStruct((8, D), F32),
                   jax.ShapeDtypeStruct((8, 2 * PFF), F32)],
        scratch_shapes=[pltpu.VMEM((NDEV, D, PSH), BF16), pltpu.VMEM((4, PSH, D), BF16),
                        pltpu.VMEM((HALO_F, 2 * PFF), F32)],
        compiler_params=_arb(),
    )(dx3, o2, x2, up_pre, up_pre, vecs, ffn_wb, w_up_g, w_down_p)


def _bwd_mid(dx2, x2d, o1, z, a0, a1, sp, vecs, v512, conv_w, gm_ws, gm_ws_t, w_out_b, w_in_g, tm):
    s = x2d.shape[0]
    nt = s // tm
    hb = tm // HALO_C
    nc = w_in_g.shape[2]

    def body(dx2_ref, x_ref, o1_ref, z_ref, a0_ref, halo_ref, a1_ref, sp_ref, v_ref, p_ref, cw_ref, ws_ref, wst_ref,
             wo_ref, wi_ref, gx_ref, dz_ref, yt_ref, do1_ref, acc_ref, accp_ref, dcw_ref, dws_ref, dbst_ref,
             dbs_s, carry, da1_s, dsp_s, dgvn_s):
        i = pl.program_id(0)
        r = nt - 1 - i

        @pl.when(i == 0)
        def _():
            for ref in (carry, dbs_s, acc_ref, accp_ref, dcw_ref, dws_ref, dbst_ref):
                ref[...] = jnp.zeros_like(ref)

        dx2v = dx2_ref[...]
        do1 = (dx2v * _row(v_ref, GT1)).astype(BF16)
        do1_ref[...] = do1
        acc_ref[pl.ds(0, 1), :] += _colsum(dx2v * o1_ref[...])
        dy = _dot_nt(do1, wo_ref[...])
        mog = _row(v_ref, MOG)

        xh, rstd = _ln(a1_ref[...])
        clg = _row(p_ref, CLG)
        a2 = xh * clg + _row(p_ref, CLB)
        s2 = _sigmoid(a2)
        a3 = a2 * s2
        an, ra = _rms(a3)
        dya = dy[:, :DC]
        da3 = _rms_bwd(dya * mog[:, :DC], an, ra)
        da2 = da3 * (s2 * (1.0 + a2 * (1.0 - s2)))
        accp_ref[pl.ds(CLB, 1), :] += _colsum(da2)
        accp_ref[pl.ds(CLG, 1), :] += _colsum(da2 * xh)
        da1 = _ln_bwd(da2 * clg, xh, rstd)
        accp_ref[pl.ds(CB, 1), :] += _colsum(da1)
        da1_s[...] = da1
        for c0 in range(0, DC, LANES):
            cols = pl.ds(c0, LANES)
            d = da1_s[:, cols]
            e = jnp.concatenate([d, carry[:, cols]], axis=0)
            carry[:, cols] = d[:HALO_C, :]
            acc = jnp.zeros((tm, LANES), F32)
            for j in range(KC):
                acc = acc + _shift_up(e, j)[:tm, :] * cw_ref[pl.ds(KC - 1 - j, 1), cols]
            sgc = _sigmoid(z_ref[:, pl.ds(DC + c0, LANES)])
            dz_ref[:, cols] = (acc * sgc).astype(BF16)
            dz_ref[:, pl.ds(DC + c0, LANES)] = (acc * z_ref[:, cols] * sgc * (1.0 - sgc)).astype(BF16)
            halo = halo_ref[:, cols]
            ea = jnp.concatenate([jnp.where(r > 0, halo, jnp.zeros_like(halo)), a0_ref[:, cols]], axis=0)
            for k in range(KC):
                dcw_ref[pl.ds(k, 1), cols] += _colsum(d * _shift_up(ea, HALO_C - (KC - 1) + k)[:tm, :])

        gu_pre = z_ref[:, 2 * DC:3 * DC]
        gv_pre = z_ref[:, 3 * DC:]
        gu, tu = _gelu(gu_pre)
        gvg, tv = _gelu(gv_pre)
        vh, vrstd = _ln(gvg)
        glg = _row(p_ref, GLG)
        gvn = (vh * glg + _row(p_ref, GLB)).astype(BF16)
        spv = sp_ref[...]
        g = gu * spv
        gn, rg = _rms(g)
        yt_ref[...] = jnp.concatenate([an * mog[:, :DC], gn * mog[:, DC:]], axis=1).astype(BF16).T
        acc_ref[pl.ds(4, 1), :] += jnp.concatenate([_colsum(dya * an), _colsum(dy[:, DC:] * gn)], axis=1)
        dg = _rms_bwd(dy[:, DC:] * mog[:, DC:], gn, rg)
        dz_ref[:, pl.ds(2 * DC, DC)] = (dg * spv * _gelu_grad(gu_pre, tu)).astype(BF16)
        dsp_s[...] = dg * gu
        upper = _causal_mask(False)
        first = _first_head_lanes()
        wmt = [jnp.where(upper, wst_ref[h], 0.0).astype(BF16) for h in range(NH)]
        for n in range(tm // CHUNK):
            rows = pl.ds(n * CHUNK, CHUNK)
            for p in range(NH // 2):
                cols = pl.ds(p * CHUNK, CHUNK)
                dsp = dsp_s[rows, cols]
                dbs_s[:, cols] += dsp
                da = jnp.where(first, dsp, 0.0).astype(BF16)
                db = jnp.where(first, 0.0, dsp).astype(BF16)
                v = gvn[n * CHUNK:(n + 1) * CHUNK, p * CHUNK:(p + 1) * CHUNK]
                dws_ref[2 * p] += _dot_nt(da, v)
                dws_ref[2 * p + 1] += _dot_nt(db, v)
                dgvn_s[rows, cols] = _dot(wmt[2 * p], da) + _dot(wmt[2 * p + 1], db)
        dgvn = dgvn_s[...]
        accp_ref[pl.ds(GLB, 1), :] += _colsum(dgvn)
        accp_ref[pl.ds(GLG, 1), :] += _colsum(dgvn * vh)
        dgvg = _ln_bwd(dgvn * glg, vh, vrstd)
        dz_ref[:, pl.ds(3 * DC, DC)] = (dgvg * _gelu_grad(gv_pre, tv)).astype(BF16)

        dh1 = jnp.zeros((tm, D), F32)
        for d in range(NDEV):
            dh1 = dh1 + _dot_nt(dz_ref[:, pl.ds(d * nc, nc)], wi_ref[d])
        xn, r1 = _rms(x_ref[...])
        g1 = _row(v_ref, G1)
        sc = 1.0 + _row(v_ref, SC1)
        acc_ref[pl.ds(1, 1), :] += _colsum(dh1)
        acc_ref[pl.ds(2, 1), :] += _colsum(dh1 * (xn * g1))
        acc_ref[pl.ds(3, 1), :] += _colsum(dh1 * sc * xn)
        gx_ref[...] = dx2v + _rms_bwd(dh1 * sc * g1, xn, r1)

        @pl.when(i == nt - 1)
        def _():
            low = _causal_mask(True)
            for h in range(NH):
                dws_ref[h] = jnp.where(low, dws_ref[h], 0.0)
            lane = lax.broadcasted_iota(jnp.int32, (CHUNK, CHUNK), 1)
            out = jnp.zeros((CHUNK, CHUNK), F32)
            for h in range(NH):
                hs = jnp.sum(dbs_s[:, pl.ds((h // 2) * CHUNK, CHUNK)]
                             * ((lane >= (h % 2) * HD) & (lane < (h % 2 + 1) * HD)).astype(F32),
                             axis=1, keepdims=True)
                out = jnp.where(lane == h, hs, out)
            dbst_ref[...] = out

    tile = lambda w: pl.BlockSpec((tm, w), lambda i: (nt - 1 - i, 0))
    return pl.pallas_call(
        body, name="bwd_mid", grid=(nt,),
        in_specs=[tile(D), tile(D), tile(D), tile(4 * DC), tile(DC),
                  pl.BlockSpec((HALO_C, DC), lambda i: (jnp.maximum((nt - 1 - i) * hb - 1, 0), 0)),
                  tile(DC), tile(DC), _full((16, D)), _full((8, DC)), _full((32, DC)),
                  _full((NH, CHUNK, CHUNK)), _full((NH, CHUNK, CHUNK)), _full((D, D)), _full((NDEV, D, nc))],
        out_specs=[tile(D), tile(4 * DC), pl.BlockSpec((D, tm), lambda i: (0, nt - 1 - i)), tile(D),
                   _full((16, D)), _full((8, DC)), _full((32, DC)),
                   _full((NH, CHUNK, CHUNK)), _full((CHUNK, CHUNK))],
        out_shape=[jax.ShapeDtypeStruct((s, D), F32), jax.ShapeDtypeStruct((s, 4 * DC), BF16),
                   jax.ShapeDtypeStruct((D, s), BF16), jax.ShapeDtypeStruct((s, D), BF16),
                   jax.ShapeDtypeStruct((16, D), F32), jax.ShapeDtypeStruct((8, DC), F32),
                   jax.ShapeDtypeStruct((32, DC), F32), jax.ShapeDtypeStruct((NH, CHUNK, CHUNK), F32),
                   jax.ShapeDtypeStruct((CHUNK, CHUNK), F32)],
        scratch_shapes=[pltpu.VMEM((CHUNK, DC), F32), pltpu.VMEM((HALO_C, DC), F32), pltpu.VMEM((tm, DC), F32),
                        pltpu.VMEM((tm, DC), F32), pltpu.VMEM((tm, DC), F32)],
        compiler_params=_arb(),
    )(dx2, x2d, o1, z, a0, a0, a1, sp, vecs, v512, conv_w, gm_ws, gm_ws_t, w_out_b, w_in_g)


def _mm_all_slots(name, at, b, bw, tk):
    k1, s = at.shape
    nslot = b.shape[1] // bw

    def body(a_ref, b_ref, o_ref):
        @pl.when(pl.program_id(0) == 0)
        def _():
            o_ref[...] = jnp.zeros_like(o_ref)

        t = _dot(a_ref[...], b_ref[...])
        for j in range(nslot):
            o_ref[j] += t[:, j * bw:(j + 1) * bw]

    return pl.pallas_call(
        body, name=name, grid=(s // tk,),
        in_specs=[pl.BlockSpec((k1, tk), lambda k: (0, k)), pl.BlockSpec((tk, nslot * bw), lambda k: (k, 0))],
        out_specs=_full((nslot, k1, bw)), out_shape=jax.ShapeDtypeStruct((nslot, k1, bw), F32),
        compiler_params=_arb(),
    )(at, b)


def _mm_col_slots(name, at, b, bw, ow, tk):
    k1, s = at.shape
    nslot = b.shape[1] // bw

    def body(a_ref, b_ref, o_ref):
        @pl.when(pl.program_id(1) == 0)
        def _():
            o_ref[...] = jnp.zeros_like(o_ref)

        o_ref[...] += _dot(a_ref[...], b_ref[...])[:, :ow]

    return pl.pallas_call(
        body, name=name, grid=(nslot, s // tk),
        in_specs=[pl.BlockSpec((k1, tk), lambda j, k: (0, k)), pl.BlockSpec((tk, bw), lambda j, k: (k, j))],
        out_specs=pl.BlockSpec((None, k1, ow), lambda j, k: (j, 0, 0)),
        out_shape=jax.ShapeDtypeStruct((nslot, k1, ow), F32),
        compiler_params=pltpu.CompilerParams(dimension_semantics=("parallel", "arbitrary")),
    )(at, b)


def _mm_row_slots(name, at, b, ah, oh, tk):
    s, k2 = b.shape
    nslot = at.shape[0] // ah

    def body(a_ref, b_ref, o_ref):
        @pl.when(pl.program_id(1) == 0)
        def _():
            o_ref[...] = jnp.zeros_like(o_ref)

        o_ref[...] += _dot(a_ref[...], b_ref[...])[:oh, :]

    return pl.pallas_call(
        body, name=name, grid=(nslot, s // tk),
        in_specs=[pl.BlockSpec((ah, tk), lambda i, k: (i, k)), pl.BlockSpec((tk, k2), lambda i, k: (k, 0))],
        out_specs=pl.BlockSpec((None, oh, k2), lambda i, k: (i, 0, 0)),
        out_shape=jax.ShapeDtypeStruct((nslot, oh, k2), F32),
        compiler_params=pltpu.CompilerParams(dimension_semantics=("parallel", "arbitrary")),
    )(at, b)


def _adam_math(w, g, m, v):
    m = ADAM_B1 * m + (1.0 - ADAM_B1) * g
    v = ADAM_B2 * v + (1.0 - ADAM_B2) * (g * g)
    m_hat = m / (1.0 - ADAM_B1 ** ADAM_STEP)
    v_hat = v / (1.0 - ADAM_B2 ** ADAM_STEP)
    delta = -ADAM_LR * (m_hat / (jnp.sqrt(v_hat) + ADAM_EPS) + ADAM_WD * w)
    return delta, m, v


def _row_block(rows, cols):
    tr = rows
    while tr * cols * 4 > (2 << 20) and tr % 32 == 0:
        tr //= 2
    return tr


def _adam3(name, w, g, m, v):
    _, rows, cols = w.shape
    tr = _row_block(rows, cols)

    def body(w_ref, g_ref, m_ref, v_ref, d_ref, mo_ref, vo_ref):
        d_ref[...], mo_ref[...], vo_ref[...] = _adam_math(w_ref[...], g_ref[...], m_ref[...], v_ref[...])

    spec = pl.BlockSpec((1, tr, cols), lambda i: (0, i, 0))
    return pl.pallas_call(
        body, name=name, grid=(rows // tr,), in_specs=[spec] * 4, out_specs=[spec] * 3,
        out_shape=[jax.ShapeDtypeStruct(w.shape, F32)] * 3, compiler_params=_arb(),
    )(w, g, m, v)


def _sum_adam(name, parts, w, m, v):
    n, rows, cols = parts.shape
    tr = _row_block(rows, cols)

    def body(p_ref, w_ref, m_ref, v_ref, g_ref, d_ref, mo_ref, vo_ref):
        g = p_ref[0].astype(F32)
        for k in range(1, n):
            g = g + p_ref[k].astype(F32)
        g_ref[0] = g
        d_ref[0], mo_ref[0], vo_ref[0] = _adam_math(w_ref[0], g, m_ref[0], v_ref[0])

    spec = pl.BlockSpec((1, tr, cols), lambda i: (0, i, 0))
    return pl.pallas_call(
        body, name=name, grid=(rows // tr,),
        in_specs=[pl.BlockSpec((n, tr, cols), lambda i: (0, i, 0))] + [spec] * 3, out_specs=[spec] * 4,
        out_shape=[jax.ShapeDtypeStruct(w.shape, F32)] * 4, compiler_params=_arb(),
    )(parts, w, m, v)


def _pair_add(name, g4, recv, core):
    _, _, rows, cols = g4.shape
    tr = _row_block(rows, cols)

    def body(c_ref, a_ref, b_ref, o_ref):
        o_ref[...] = (a_ref[...] + b_ref[...]).astype(BF16)

    return pl.pallas_call(
        body, name=name,
        grid_spec=pltpu.PrefetchScalarGridSpec(
            num_scalar_prefetch=1, grid=(4, rows // tr),
            in_specs=[pl.BlockSpec((None, None, tr, cols), lambda k, i, c_ref: (k, c_ref[0], i, 0)),
                      pl.BlockSpec((None, tr, cols), lambda k, i, c_ref: (k, i, 0))],
            out_specs=pl.BlockSpec((None, tr, cols), lambda k, i, c_ref: (k, i, 0))),
        out_shape=jax.ShapeDtypeStruct((4, rows, cols), BF16), compiler_params=_arb(2),
    )(core, g4, recv)


def _sum_small(rows_all, p_all, ws_all, bst_all, fw_all, cw_all):
    def body(a_ref, p_ref, ws_ref, bst_ref, fw_ref, cw_ref,
             g_b_ada, g_n1, g_mog, g_n2, g_gf, g_cb, g_clg, g_clb, g_glg, g_glb, g_ws, g_bs, fw_sum, cw_sum):
        def total(ref):
            t = ref[0]
            for k in range(1, NDEV):
                t = t + ref[k]
            return t

        a = total(a_ref)
        g_b_ada[...] = jnp.concatenate([a[k:k + 1, :] for k in range(6)], axis=1)
        g_n1[...] = a[6:7, :]
        g_mog[...] = a[7:8, :]
        g_n2[...] = a[8:9, :]
        g_gf[...] = a[9:10, :].reshape(D)
        p = total(p_ref)
        for k, ref in zip((CB, CLG, CLB, GLG, GLB), (g_cb, g_clg, g_clb, g_glg, g_glb)):
            ref[...] = p[k:k + 1, :]
        g_ws[0] = total(ws_ref)
        g_bs[0] = jnp.transpose(total(bst_ref))[:NH, :]
        fw_sum[...] = total(fw_ref)
        cw_sum[...] = total(cw_ref)

    vec = lambda n: jax.ShapeDtypeStruct((1, n), F32)
    return pl.pallas_call(
        body, name="sum_small_grads",
        out_shape=[vec(6 * D), vec(D), vec(D), vec(D), jax.ShapeDtypeStruct((D,), F32),
                   vec(DC), vec(DC), vec(DC), vec(DC), vec(DC),
                   jax.ShapeDtypeStruct((1, NH, CHUNK, CHUNK), F32), jax.ShapeDtypeStruct((1, NH, CHUNK), F32),
                   jax.ShapeDtypeStruct((8, 2 * PFF), F32), jax.ShapeDtypeStruct((32, DC), F32)],
    )(rows_all, p_all, ws_all, bst_all, fw_all, cw_all)


def _adam_small(quads):
    n = len(quads)

    def body(*refs):
        ins, outs = refs[:4 * n], refs[4 * n:]
        for q in range(n):
            w, g, m, v = (r[...] for r in ins[4 * q:4 * q + 4])
            outs[3 * q][...], outs[3 * q + 1][...], outs[3 * q + 2][...] = _adam_math(w, g, m, v)

    flat = [a for q in quads for a in q]
    outs = pl.pallas_call(
        body, name="adam_small",
        out_shape=[jax.ShapeDtypeStruct(q[0].shape, F32) for q in quads for _ in range(3)],
    )(*flat)
    return [tuple(outs[3 * q:3 * q + 3]) for q in range(n)]


def kernel(x, c, w_ada, b_ada, norm1_gain, w_in, conv_dw_w, conv_dw_b, conv_ln_g, conv_ln_b, gm_ln_g, gm_ln_b, gm_ws, gm_bs, mix_out_gain, w_out, norm2_gain, w_up, ffn_dw_w, ffn_dw_b, w_down, final_gain, loss_target, m_w_ada, m_b_ada, m_norm1_gain, m_w_in, m_conv_dw_w, m_conv_dw_b, m_conv_ln_g, m_conv_ln_b, m_gm_ln_g, m_gm_ln_b, m_gm_ws, m_gm_bs, m_mix_out_gain, m_w_out, m_norm2_gain, m_w_up, m_ffn_dw_w, m_ffn_dw_b, m_w_down, m_final_gain, v_w_ada, v_b_ada, v_norm1_gain, v_w_in, v_conv_dw_w, v_conv_dw_b, v_conv_ln_g, v_conv_ln_b, v_gm_ln_g, v_gm_ln_b, v_gm_ws, v_gm_bs, v_mix_out_gain, v_w_out, v_norm2_gain, v_w_up, v_ffn_dw_w, v_ffn_dw_b, v_w_down, v_final_gain):
    s = x.shape[1]
    ax, ay, ac = _place()
    me = 4 * ax + 2 * ay + ac
    n_ada = w_ada.shape[2]
    n_cw = conv_dw_w.shape[2]
    x2d = x[0]
    target = loss_target[0]
    pad_sh = lambda a: jnp.pad(a, [(0, 0)] * (a.ndim - 1) + [(0, PSH - NSH)])

    c_all, cw_all, fw_all = _all_gather("gather_small", [c, conv_dw_w[0], ffn_dw_w[0]])
    conv_w = jnp.pad(jnp.transpose(cw_all, (1, 0, 2)).reshape(KC, DC), ((0, 32 - KC), (0, 0)))
    ffn_w = jnp.transpose(pad_sh(fw_all), (1, 0, 2)).reshape(KF, 2 * PFF)
    ffn_b = pad_sh(ffn_dw_b.reshape(NDEV, NSH)).reshape(1, 2 * PFF)
    ffn_wb = jnp.concatenate([ffn_w, ffn_b, jnp.zeros((8 - KF - 1, 2 * PFF), F32)], axis=0)

    b_cols = lax.dynamic_slice(b_ada, (0, me * n_ada), (1, n_ada))
    (mod_all,) = _all_gather("gather_mod", [_mod_part(c_all, w_ada, b_cols)])
    shards, mod_all = lax.optimization_barrier((
        (w_in[0].astype(BF16), w_out[0].astype(BF16), pad_sh(w_up[0].astype(BF16)), w_down[0].astype(BF16)), mod_all))
    gather_in, token_a = _start_direct_gather("gather_in", [shards[0]], me)
    gather_out, token_b = _start_direct_gather("gather_out", [shards[1]], me)
    gather_ffn, token_c = _start_gather("gather_up_down", [shards[2], shards[3]], me)
    mod = lax.dynamic_index_in_dim(mod_all, me, axis=1, keepdims=False).reshape(6, D)
    sh1, sc1, gt1, sh2, sc2, gt2 = [mod[k:k + 1] for k in range(6)]
    vecs = jnp.concatenate([norm1_gain, sh1, sc1, gt1, norm2_gain, sh2, sc2, gt2, mix_out_gain,
                            final_gain.reshape(1, D), jnp.zeros((6, D), F32)], axis=0)
    vecs = vecs + (token_a[0, 0] + token_b[0, 0] + token_c[0, 0])
    v512 = jnp.concatenate([conv_dw_b, conv_ln_g, conv_ln_b, gm_ln_g, gm_ln_b, jnp.zeros((3, DC), F32)], axis=0)
    bs_exp = jnp.repeat(jnp.transpose(gm_bs[0]), HD, axis=1)
    gm_ws_t = jnp.swapaxes(gm_ws[0], 1, 2)

    tm_in = min(512, s)
    tm = min(256, s)
    (w_in_g,) = _finish_direct_gather("gather_in", gather_in, vecs)
    z, a0, h1_t = _fwd_in(x2d, vecs, w_in_g, tm_in)
    (w_out_g,) = _finish_direct_gather("gather_out", gather_out, a0)
    w_out_b = w_out_g.reshape(D, D)
    a1, sp, x2, o1, h2, h2_t = _fwd_mid(a0, z, x2d, vecs, v512, conv_w, gm_ws, bs_exp, w_out_b, tm)
    w_up_g, w_down_g = _finish_gather("gather_up_down", gather_ffn, h2)
    w_down_p = jnp.pad(w_down_g.reshape(4, NSH, D), ((0, 0), (0, PSH - NSH), (0, 0)))
    up_pre, f_t, o2, dx3, acc_f = _fwd_ffn(h2, x2, target, vecs, ffn_wb, w_up_g, w_down_p, tm)
    loss = lax.psum(jnp.sum(acc_f[1]), ("x", "y", "c"))

    core = ac.reshape(1).astype(jnp.int32)
    mychip = 2 * ax + ay

    def to_pairs(named):
        g4s = [g.reshape((4, 2) + g.shape[1:]) for _, g in named]
        from_sibling = _sibling_swap("rs_sibling_" + named[0][0], g4s)
        return [_pair_add("rs_pair_add_" + t[0], g4, rv, core) for t, g4, rv in zip(named, g4s, from_sibling)]

    dx2, dup_pre, do2, acc_b, acc_fw = _bwd_ffn(dx3, o2, x2, up_pre, vecs, ffn_wb, w_up_g, w_down_p, tm)
    dw_up = _mm_col_slots("dw_up", h2_t, dup_pre, PSH, NSH, min(2048, s))
    dw_down = _mm_row_slots("dw_down", f_t, do2, PSH, NSH, min(2048, s)).reshape(NDEV, w_down.shape[1], D)
    exchange_ffn, token_x = _start_exchange("rs_chips_ffn", to_pairs([("w_up", dw_up), ("w_down", dw_down)]), mychip)
    gx, dz, y_t, do1, acc_m, acc_p, dcw, dws, dbs_t = _bwd_mid(
        dx2, x2d, o1, z, a0, a1, sp, vecs + token_x[0, 0], v512, conv_w, gm_ws[0], gm_ws_t, w_out_b, w_in_g, tm)
    rows = jnp.concatenate([acc_m[1:3], acc_m[0:1], acc_b[1:3], acc_b[0:1], acc_m[3:5], acc_b[3:4], acc_f[0:1],
                            jnp.zeros((6, D), F32)], axis=0)
    small_gather, token_s = _start_direct_gather("gather_small_grads", [rows, acc_p, dws, dbs_t, acc_fw, dcw], me)
    h1_t, _ = lax.optimization_barrier((h1_t, token_s))
    dw_in = _mm_all_slots("dw_in", h1_t, dz, w_in.shape[2], min(1024, s))
    dw_out = _mm_all_slots("dw_out", y_t, do1, D, min(2048, s)).reshape(NDEV, w_out.shape[1], D)
    exchange_mix, token_m = _start_exchange("rs_chips_mix", to_pairs([("w_in", dw_in), ("w_out", dw_out)]), mychip)

    rows_all, p_all, ws_all, bst_all, fwg_all, cwg_all = _finish_direct_gather(
        "gather_small_grads", small_gather, token_m)
    (g_b_ada, g_n1, g_mog, g_n2, g_gf, g_cb, g_clg, g_clb, g_glg, g_glb, g_ws, g_bs, fw_sum, cw_sum) = _sum_small(
        rows_all, p_all, ws_all, bst_all, fwg_all, cwg_all)
    g_fb = fw_sum[3].reshape(NDEV, PSH)[:, :NSH].reshape(ffn_dw_b.shape)
    g_fw = lax.dynamic_index_in_dim(fw_sum[:KF].reshape(KF, NDEV, PSH), me, axis=1, keepdims=False)[:, :NSH]
    g_fw = g_fw.reshape(ffn_dw_w.shape)
    g_cw = lax.dynamic_slice(cw_sum, (0, me * n_cw), (KC, n_cw)).reshape(conv_dw_w.shape)
    small = [
        (b_ada, g_b_ada, m_b_ada, v_b_ada), (norm1_gain, g_n1, m_norm1_gain, v_norm1_gain),
        (conv_dw_w, g_cw, m_conv_dw_w, v_conv_dw_w), (conv_dw_b, g_cb, m_conv_dw_b, v_conv_dw_b),
        (conv_ln_g, g_clg, m_conv_ln_g, v_conv_ln_g), (conv_ln_b, g_clb, m_conv_ln_b, v_conv_ln_b),
        (gm_ln_g, g_glg, m_gm_ln_g, v_gm_ln_g), (gm_ln_b, g_glb, m_gm_ln_b, v_gm_ln_b),
        (gm_ws, g_ws, m_gm_ws, v_gm_ws), (gm_bs, g_bs, m_gm_bs, v_gm_bs),
        (mix_out_gain, g_mog, m_mix_out_gain, v_mix_out_gain), (norm2_gain, g_n2, m_norm2_gain, v_norm2_gain),
        (ffn_dw_w, g_fw, m_ffn_dw_w, v_ffn_dw_w), (ffn_dw_b, g_fb, m_ffn_dw_b, v_ffn_dw_b),
        (final_gain, g_gf, m_final_gain, v_final_gain)]
    small_out = _adam_small(small)
    res = {}
    for name, q, o in zip(("b_ada", "norm1_gain", "conv_dw_w", "conv_dw_b", "conv_ln_g", "conv_ln_b", "gm_ln_g",
                           "gm_ln_b", "gm_ws", "gm_bs", "mix_out_gain", "norm2_gain", "ffn_dw_w", "ffn_dw_b",
                           "final_gain"), small, small_out):
        res[name] = (q[1],) + o

    dmod_all = rows_all[:, :6].reshape(NDEV, 6 * D)
    dm_cols = lax.dynamic_slice(dmod_all, (0, me * n_ada), (NDEV, n_ada))
    g_ada = _ada_grad(jnp.transpose(c_all[:, 0, :]), dm_cols)
    res["w_ada"] = (g_ada,) + tuple(_adam3("adam_ada", w_ada, g_ada, m_w_ada, v_w_ada))

    big = [("w_up", w_up, m_w_up, v_w_up), ("w_down", w_down, m_w_down, v_w_down),
           ("w_in", w_in, m_w_in, v_w_in), ("w_out", w_out, m_w_out, v_w_out)]
    from_chips = list(_finish_exchange("rs_chips_ffn", exchange_ffn, res["w_ada"][1]))
    for t, parts in zip(big[:2], from_chips):
        res[t[0]] = tuple(_sum_adam("rs_sum_adam_" + t[0], parts, t[1], t[2], t[3]))
    from_chips = list(_finish_exchange("rs_chips_mix", exchange_mix, res["w_down"][1]))
    for t, parts in zip(big[2:], from_chips):
        res[t[0]] = tuple(_sum_adam("rs_sum_adam_" + t[0], parts, t[1], t[2], t[3]))

    order = ("w_ada", "b_ada", "norm1_gain", "w_in", "conv_dw_w", "conv_dw_b", "conv_ln_g", "conv_ln_b", "gm_ln_g",
             "gm_ln_b", "gm_ws", "gm_bs", "mix_out_gain", "w_out", "norm2_gain", "w_up", "ffn_dw_w", "ffn_dw_b",
             "w_down", "final_gain")
    return (loss, gx.reshape(x.shape), *[res[n][0] for n in order], *[res[n][1] for n in order],
            *[res[n][2] for n in order], *[res[n][3] for n in order])
```

```python
import functools

import jax
import jax.numpy as jnp
from jax import lax
from jax.experimental import pallas as pl
from jax.experimental.pallas import tpu as pltpu

F32 = jnp.float32
BF16 = jnp.bfloat16
NDEV = 8
D = 1024
DC = 512
DFF = 2816
NSH = 704
PSH = 768
PFF = 4 * PSH
KC = 31
KF = 3
CHUNK = 128
NH = 8
HD = 64
HALO_C = 32
HALO_F = 8
LANES = 128
RMS_EPS = 1e-6
LN_EPS = 1e-5
ADAM_LR = 0.001
ADAM_B1 = 0.9
ADAM_B2 = 0.999
ADAM_EPS = 1e-08
ADAM_WD = 0.01
ADAM_STEP = 10
GELU_K = 0.7978845608028654
GELU_C = 0.044715

MESH = pl.DeviceIdType.MESH
ANY = pl.BlockSpec(memory_space=pl.ANY)

G1, SH1, SC1, GT1, G2, SH2, SC2, GT2, MOG, GF = range(10)
CB, CLG, CLB, GLG, GLB = range(5)


def _full(shape):
    return pl.BlockSpec(shape, lambda *_: (0,) * len(shape))


def _arb(n=1):
    return pltpu.CompilerParams(dimension_semantics=("arbitrary",) * n)


def _row(ref, r):
    return ref[pl.ds(r, 1), :]


def _colsum(v):
    return jnp.sum(v, axis=0, keepdims=True)


def _rowmean(v):
    return jnp.mean(v, axis=-1, keepdims=True)


def _rms(x):
    r = lax.rsqrt(_rowmean(x * x) + RMS_EPS)
    return x * r, r


def _rms_bwd(dxn, xn, r):
    return r * (dxn - xn * _rowmean(dxn * xn))


def _ln(x):
    mu = _rowmean(x)
    xc = x - mu
    rstd = lax.rsqrt(_rowmean(xc * xc) + LN_EPS)
    return xc * rstd, rstd


def _ln_bwd(dxh, xhat, rstd):
    return rstd * (dxh - _rowmean(dxh) - xhat * _rowmean(dxh * xhat))


def _sigmoid(x):
    return 0.5 * jnp.tanh(0.5 * x) + 0.5


def _gelu(x):
    t = jnp.tanh(GELU_K * (x + GELU_C * x * x * x))
    return 0.5 * x * (1.0 + t), t


def _gelu_grad(x, t):
    return 0.5 * (1.0 + t) + 0.5 * x * (1.0 - t * t) * (GELU_K * (1.0 + 3.0 * GELU_C * x * x))


def _dot(a, b):
    return jnp.dot(a, b, preferred_element_type=F32)


def _dot_nt(a, b):
    return lax.dot_general(a, b, (((1,), (1,)), ((), ())), preferred_element_type=F32)


def _shift_up(e, s):
    n = e.shape[0]
    return pltpu.roll(e, (n - s) % n, 0)


def _place():
    return lax.axis_index("x"), lax.axis_index("y"), lax.axis_index("c")


def _all_gather(name, xs):
    n = len(xs)

    def body(*refs):
        x_refs, out_refs = refs[:n], refs[n:2 * n]
        send_sems, recv_sems, local_sems = refs[2 * n:]
        x, y, c = _place()
        me, sibling = (x, y, c), (x, y, 1 - c)
        chips = [(1 - x, y), (x, 1 - y), (1 - x, 1 - y)]

        def copy(a, k, block, to, own=False):
            px, py, pc = block
            slot = out_refs[a].at[4 * px + 2 * py + pc]
            return pltpu.make_async_remote_copy(
                src_ref=x_refs[a] if own else slot, dst_ref=slot,
                send_sem=send_sems.at[7 * a + k], recv_sem=recv_sems.at[7 * a + k], device_id=to, device_id_type=MESH)

        mine = [pltpu.make_async_copy(x_refs[a], out_refs[a].at[4 * x + 2 * y + c], local_sems.at[a]) for a in range(n)]
        for cp in mine:
            cp.start()
        first = []
        for a in range(n):
            first.append(copy(a, 0, me, sibling, own=True))
            first += [copy(a, 1 + j, me, (*chip, c), own=True) for j, chip in enumerate(chips)]
        for cp in first:
            cp.start()
        passed = []
        for j, chip in enumerate(chips):
            for a in range(n):
                copy(a, 1 + j, (*chip, c), me).wait_recv()
                cp = copy(a, 4 + j, (*chip, c), sibling)
                cp.start()
                passed.append(cp)
        for a in range(n):
            copy(a, 0, sibling, me).wait_recv()
            for j, chip in enumerate(chips):
                copy(a, 4 + j, (*chip, 1 - c), me).wait_recv()
        for cp in first + passed:
            cp.wait_send()
        for cp in mine:
            cp.wait()

    return pl.pallas_call(
        body, name=name, out_shape=[jax.ShapeDtypeStruct((NDEV,) + a.shape, a.dtype) for a in xs],
        in_specs=[ANY] * n, out_specs=[ANY] * n,
        scratch_shapes=[pltpu.SemaphoreType.DMA((7 * n,)), pltpu.SemaphoreType.DMA((7 * n,)),
                        pltpu.SemaphoreType.DMA((n,))],
    )(*xs)


def _sibling_swap(name, g4s):
    n = len(g4s)

    def body(*refs):
        g_refs, out_refs = refs[:n], refs[n:2 * n]
        send_sems, recv_sems = refs[2 * n:]
        x, y, c = _place()
        cps = [pltpu.make_async_remote_copy(
            src_ref=g_refs[a].at[k, 1 - c], dst_ref=out_refs[a].at[k],
            send_sem=send_sems.at[4 * a + k], recv_sem=recv_sems.at[4 * a + k],
            device_id=(x, y, 1 - c), device_id_type=MESH) for a in range(n) for k in range(4)]
        for cp in cps:
            cp.start()
        for cp in cps:
            cp.wait()

    return pl.pallas_call(
        body, name=name, out_shape=[jax.ShapeDtypeStruct((4,) + g.shape[2:], g.dtype) for g in g4s],
        in_specs=[ANY] * n, out_specs=[ANY] * n,
        scratch_shapes=[pltpu.SemaphoreType.DMA((4 * n,)), pltpu.SemaphoreType.DMA((4 * n,))],
    )(*g4s)


HBM = pl.BlockSpec(memory_space=pltpu.HBM)
SEM = pl.BlockSpec(memory_space=pltpu.SEMAPHORE)
EFFECT = pltpu.SideEffectType.DATAFLOW_SIDE_EFFECTING


def _in_hbm(a):
    return pltpu.with_memory_space_constraint(a, pltpu.HBM)


def _split_start(name, bufs, copies):
    n = len(bufs)

    def body(*refs):
        for cp in copies(refs[:n], refs[n], refs[n + 1]):
            cp.start()
        refs[-1][...] = jnp.zeros_like(refs[-1])

    out = pl.pallas_call(
        body, name=name,
        out_shape=(pltpu.SemaphoreType.DMA((copies.count,)), pltpu.SemaphoreType.DMA((copies.count,)),
                   *[pltpu.HBM(a.shape, a.dtype) for a in bufs], jax.ShapeDtypeStruct((8, LANES), F32)),
        in_specs=[HBM] * n, out_specs=(SEM, SEM, *[HBM] * n, pl.BlockSpec(memory_space=pltpu.VMEM)),
        input_output_aliases={i: 2 + i for i in range(n)},
        compiler_params=pltpu.CompilerParams(has_side_effects=EFFECT),
    )(*[_in_hbm(a) for a in bufs])
    return (out[0], out[1], list(out[2:2 + n])), out[-1]


def _split_wait(name, handle, copies, after):
    send_sems, recv_sems, bufs = handle
    n = len(bufs)

    def body(*refs):
        for cp in copies(refs[:n], refs[n], refs[n + 1]):
            cp.wait_send()
            cp.wait_recv()

    out = pl.pallas_call(
        body, name=name, out_shape=tuple(pltpu.HBM(a.shape, a.dtype) for a in bufs),
        in_specs=[HBM] * n + [SEM, SEM, pl.BlockSpec(memory_space=pl.ANY)], out_specs=tuple([HBM] * n),
        input_output_aliases={i: i for i in range(n)},
        compiler_params=pltpu.CompilerParams(has_side_effects=EFFECT),
    )(*bufs, send_sems, recv_sems, after)
    return list(out)


class _GatherFirstCopies:
    def __init__(self, n):
        self.n, self.count = n, 4 * n

    def __call__(self, refs, send_sems, recv_sems):
        x, y, c = _place()
        peers = [(x, y, 1 - c), (1 - x, y, c), (x, 1 - y, c), (1 - x, 1 - y, c)]
        return [pltpu.make_async_remote_copy(
            src_ref=refs[a], dst_ref=refs[self.n + a].at[4 * x + 2 * y + c],
            send_sem=send_sems.at[4 * a + k], recv_sem=recv_sems.at[4 * a + k], device_id=peer, device_id_type=MESH)
            for a in range(self.n) for k, peer in enumerate(peers)]


class _GatherDirectCopies:
    def __init__(self, n):
        self.n, self.count = n, 7 * n

    def __call__(self, refs, send_sems, recv_sems):
        x, y, c = _place()
        flip = lambda v, bit: 1 - v if bit else v
        return [pltpu.make_async_remote_copy(
            src_ref=refs[a], dst_ref=refs[self.n + a].at[4 * x + 2 * y + c],
            send_sem=send_sems.at[7 * a + r - 1], recv_sem=recv_sems.at[7 * a + r - 1],
            device_id=(flip(x, r & 4), flip(y, r & 2), flip(c, r & 1)), device_id_type=MESH)
            for a in range(self.n) for r in range(1, NDEV)]


class _GatherPassCopies:
    def __init__(self, n):
        self.n, self.count = n, 3 * n

    def __call__(self, refs, send_sems, recv_sems):
        x, y, c = _place()
        cps = []
        for a in range(self.n):
            for j, (px, py) in enumerate([(1 - x, y), (x, 1 - y), (1 - x, 1 - y)]):
                slot = refs[a].at[4 * px + 2 * py + c]
                cps.append(pltpu.make_async_remote_copy(
                    src_ref=slot, dst_ref=slot, send_sem=send_sems.at[3 * a + j], recv_sem=recv_sems.at[3 * a + j],
                    device_id=(x, y, 1 - c), device_id_type=MESH))
        return cps


class _ExchangeCopies:
    def __init__(self, n):
        self.n, self.count = n, 3 * n

    def __call__(self, refs, send_sems, recv_sems):
        x, y, c = _place()
        cps = []
        for a in range(self.n):
            for j, (px, py) in enumerate([(1 - x, y), (x, 1 - y), (1 - x, 1 - y)]):
                cps.append(pltpu.make_async_remote_copy(
                    src_ref=refs[a].at[2 * px + py], dst_ref=refs[self.n + a].at[2 * x + y],
                    send_sem=send_sems.at[3 * a + j], recv_sem=recv_sems.at[3 * a + j],
                    device_id=(px, py, c), device_id_type=MESH))
        return cps


def _own_slot(nslot, src, index):
    land = lax.empty((nslot,) + src.shape, src.dtype)
    return lax.dynamic_update_slice(land, src[None], (index,) + (0,) * src.ndim)


def _start_gather(tag, xs, me):
    lands = [_own_slot(NDEV, a, me) for a in xs]
    return _split_start(tag + "_start", list(xs) + lands, _GatherFirstCopies(len(xs)))


def _pass_gather(tag, handle, after):
    n = len(handle[2]) // 2
    lands = _split_wait(tag + "_wait", handle, _GatherFirstCopies(n), after)[n:]
    return _split_start(tag + "_pass", lands, _GatherPassCopies(n))


def _end_gather(tag, passing, after):
    return _split_wait(tag + "_pass_wait", passing, _GatherPassCopies(len(passing[2])), after)


def _finish_gather(tag, handle, after):
    passing, token = _pass_gather(tag, handle, after)
    return _end_gather(tag, passing, token)


def _start_direct_gather(tag, xs, me):
    lands = [_own_slot(NDEV, a, me) for a in xs]
    return _split_start(tag + "_start", list(xs) + lands, _GatherDirectCopies(len(xs)))


def _finish_direct_gather(tag, handle, after):
    n = len(handle[2]) // 2
    return _split_wait(tag + "_wait", handle, _GatherDirectCopies(n), after)[n:]


def _start_exchange(tag, hs, mychip):
    lands = [_own_slot(4, lax.dynamic_index_in_dim(h, mychip, 0, keepdims=False), mychip) for h in hs]
    return _split_start(tag + "_start", list(hs) + lands, _ExchangeCopies(len(hs)))


def _finish_exchange(tag, handle, after):
    n = len(handle[2]) // 2
    return _split_wait(tag + "_wait", handle, _ExchangeCopies(n), after)[n:]


def _mod_part(c_all, w_ada, b_cols):
    ncol = w_ada.shape[2]

    def body(c_ref, w_ref, b_ref, o_ref):
        cv = c_ref[:, 0, :]
        ca = cv * _sigmoid(cv)
        o_ref[...] = _dot(ca.astype(BF16), w_ref[0].astype(BF16)) + b_ref[...]

    return pl.pallas_call(body, name="mod_part", out_shape=jax.ShapeDtypeStruct((NDEV, ncol), F32))(
        c_all, w_ada, b_cols)


def _ada_grad(c_all_t, dmod_cols):
    ncol = dmod_cols.shape[1]

    def body(ct_ref, dm_ref, o_ref):
        ct = ct_ref[...]
        ca = ct * _sigmoid(ct)
        acc = jnp.zeros((D, ncol), F32)
        for b in range(NDEV):
            acc = acc + ca[:, b:b + 1] * dm_ref[pl.ds(b, 1), :]
        o_ref[0] = acc

    return pl.pallas_call(body, name="ada_grad", out_shape=jax.ShapeDtypeStruct((1, D, ncol), F32))(
        c_all_t, dmod_cols)


def _fwd_in(x2d, vecs, w_in_g, tm):
    s = x2d.shape[0]
    nc = w_in_g.shape[2]

    def body(x_ref, v_ref, w_ref, z_ref, a0_ref, h1t_ref):
        xn, _ = _rms(x_ref[...])
        h = (xn * _row(v_ref, G1)) * (1.0 + _row(v_ref, SC1)) + _row(v_ref, SH1)
        hb = h.astype(BF16)
        h1t_ref[...] = hb.T
        for d in range(NDEV):
            z_ref[:, pl.ds(d * nc, nc)] = _dot(hb, w_ref[d])
        a0_ref[...] = z_ref[:, :DC] * _sigmoid(z_ref[:, DC:2 * DC])

    return pl.pallas_call(
        body, name="fwd_in", grid=(s // tm,),
        in_specs=[pl.BlockSpec((tm, D), lambda i: (i, 0)), _full((16, D)), _full((NDEV, D, nc))],
        out_specs=[pl.BlockSpec((tm, 4 * DC), lambda i: (i, 0)), pl.BlockSpec((tm, DC), lambda i: (i, 0)),
                   pl.BlockSpec((D, tm), lambda i: (0, i))],
        out_shape=[jax.ShapeDtypeStruct((s, 4 * DC), F32), jax.ShapeDtypeStruct((s, DC), F32),
                   jax.ShapeDtypeStruct((D, s), BF16)],
        compiler_params=_arb(),
    )(x2d, vecs, w_in_g)


def _causal_mask(lower):
    r = lax.broadcasted_iota(jnp.int32, (CHUNK, CHUNK), 0)
    c = lax.broadcasted_iota(jnp.int32, (CHUNK, CHUNK), 1)
    return (r >= c) if lower else (r <= c)


def _first_head_lanes():
    return lax.broadcasted_iota(jnp.int32, (CHUNK, CHUNK), 1) < HD


def _fwd_mid(a0, z, x2d, vecs, v512, conv_w, gm_ws, bs_exp, w_out_b, tm):
    s = x2d.shape[0]
    hb = tm // HALO_C

    def body(a0_ref, halo_ref, zg_ref, x_ref, v_ref, p_ref, cw_ref, ws_ref, bs_ref, wo_ref,
             a1_ref, sp_ref, x2_ref, o1_ref, h2_ref, h2t_ref):
        i = pl.program_id(0)
        for c0 in range(0, DC, LANES):
            cols = pl.ds(c0, LANES)
            halo = halo_ref[:, cols]
            e = jnp.concatenate([jnp.where(i > 0, halo, jnp.zeros_like(halo)), a0_ref[:, cols]], axis=0)
            acc = jnp.broadcast_to(p_ref[pl.ds(CB, 1), cols], (tm, LANES))
            for k in range(KC):
                acc = acc + _shift_up(e, HALO_C - (KC - 1) + k)[:tm, :] * cw_ref[pl.ds(k, 1), cols]
            a1_ref[:, cols] = acc
        xh, _ = _ln(a1_ref[...])
        a2 = xh * _row(p_ref, CLG) + _row(p_ref, CLB)
        a3 = a2 * _sigmoid(a2)
        gu, _ = _gelu(zg_ref[:, :DC])
        gvg, _ = _gelu(zg_ref[:, DC:])
        vh, _ = _ln(gvg)
        gvn = (vh * _row(p_ref, GLG) + _row(p_ref, GLB)).astype(BF16)
        low = _causal_mask(True)
        first = _first_head_lanes()
        wm = [jnp.where(low, ws_ref[0, h], 0.0).astype(BF16) for h in range(NH)]
        for n in range(tm // CHUNK):
            for p in range(NH // 2):
                v = gvn[n * CHUNK:(n + 1) * CHUNK, p * CHUNK:(p + 1) * CHUNK]
                blk = jnp.where(first, _dot(wm[2 * p], v), _dot(wm[2 * p + 1], v))
                sp_ref[pl.ds(n * CHUNK, CHUNK), pl.ds(p * CHUNK, CHUNK)] = blk + bs_ref[:, pl.ds(p * CHUNK, CHUNK)]
        g = gu * sp_ref[...]
        an, _ = _rms(a3)
        gn, _ = _rms(g)
        mog = _row(v_ref, MOG)
        y = jnp.concatenate([an * mog[:, :DC], gn * mog[:, DC:]], axis=1).astype(BF16)
        o1 = _dot(y, wo_ref[...])
        o1_ref[...] = o1
        x2 = x_ref[...] + _row(v_ref, GT1) * o1
        x2_ref[...] = x2
        xn2, _ = _rms(x2)
        h2 = (xn2 * _row(v_ref, G2)) * (1.0 + _row(v_ref, SC2)) + _row(v_ref, SH2)
        h2b = h2.astype(BF16)
        h2_ref[...] = h2b
        h2t_ref[...] = h2b.T

    tile = lambda w: pl.BlockSpec((tm, w), lambda i: (i, 0))
    return pl.pallas_call(
        body, name="fwd_mid", grid=(s // tm,),
        in_specs=[tile(DC), pl.BlockSpec((HALO_C, DC), lambda i: (jnp.maximum(i * hb - 1, 0), 0)),
                  pl.BlockSpec((tm, 2 * DC), lambda i: (i, 1)), tile(D), _full((16, D)), _full((8, DC)),
                  _full((32, DC)), _full((1, NH, CHUNK, CHUNK)), _full((CHUNK, DC)), _full((D, D))],
        out_specs=[tile(DC), tile(DC), tile(D), tile(D), tile(D), pl.BlockSpec((D, tm), lambda i: (0, i))],
        out_shape=[jax.ShapeDtypeStruct((s, DC), F32), jax.ShapeDtypeStruct((s, DC), F32),
                   jax.ShapeDtypeStruct((s, D), F32), jax.ShapeDtypeStruct((s, D), F32),
                   jax.ShapeDtypeStruct((s, D), BF16), jax.ShapeDtypeStruct((D, s), BF16)],
        compiler_params=_arb(),
    )(a0, a0, z, x2d, vecs, v512, conv_w, gm_ws, bs_exp, w_out_b)


def _ffn_conv(fw_ref, cols, p2, p1, pre):
    return (fw_ref[pl.ds(3, 1), cols] + fw_ref[pl.ds(0, 1), cols] * p2
            + fw_ref[pl.ds(1, 1), cols] * p1 + fw_ref[pl.ds(2, 1), cols] * pre)


def _fwd_ffn(h2, x2, target, vecs, ffn_wb, w_up_g, w_down_p, tm):
    s = x2.shape[0]

    def body(h2_ref, x2_ref, t_ref, v_ref, fw_ref, wu_hbm, wd_hbm,
             up_ref, dx3_ref, acc_ref, wu, wd, carry):
        i = pl.program_id(0)

        @pl.when(i == 0)
        def _():
            pltpu.sync_copy(wu_hbm, wu)
            pltpu.sync_copy(wd_hbm, wd)
            carry[...] = jnp.zeros_like(carry)
            acc_ref[...] = jnp.zeros_like(acc_ref)

        h2v = h2_ref[...]
        o2 = jnp.zeros((tm, D), F32)
        for j in range(4):
            conv = []
            for sh in (j, 4 + j):
                cols = pl.ds(sh * PSH, PSH)
                pre = _dot(h2v, wu[sh])
                up_ref[:, cols] = pre
                e = jnp.concatenate([carry[:, cols], pre], axis=0)
                carry[:, cols] = pre[tm - HALO_F:, :]
                conv.append(_ffn_conv(fw_ref, cols, pltpu.roll(e, 2, 0)[HALO_F:, :],
                                      pltpu.roll(e, 1, 0)[HALO_F:, :], pre))
            val, gate = conv
            f = ((gate * _sigmoid(gate)) * val).astype(BF16)
            o2 = o2 + _dot(f, wd[j])
        x3 = x2_ref[...] + _row(v_ref, GT2) * o2
        xn3, r3 = _rms(x3)
        gf = _row(v_ref, GF)
        diff = xn3 * gf - t_ref[...]
        acc_ref[pl.ds(1, 1), :] += _colsum(diff * diff) * (0.5 / D)
        dout = diff * (1.0 / D)
        acc_ref[pl.ds(0, 1), :] += _colsum(dout * xn3)
        dx3 = _rms_bwd(dout * gf, xn3, r3)
        dx3_ref[...] = dx3
        acc_ref[pl.ds(2, 1), :] += _colsum(dx3 * o2)

    tile = lambda w: pl.BlockSpec((tm, w), lambda i: (i, 0))
    return pl.pallas_call(
        body, name="fwd_ffn", grid=(s // tm,),
        in_specs=[tile(D), tile(D), tile(D), _full((16, D)), _full((8, 2 * PFF)), ANY, ANY],
        out_specs=[tile(2 * PFF), tile(D), _full((8, D))],
        out_shape=[jax.ShapeDtypeStruct((s, 2 * PFF), F32), jax.ShapeDtypeStruct((s, D), F32),
                   jax.ShapeDtypeStruct((8, D), F32)],
        scratch_shapes=[pltpu.VMEM((NDEV, D, PSH), BF16), pltpu.VMEM((4, PSH, D), BF16),
                        pltpu.VMEM((HALO_F, 2 * PFF), F32)],
        compiler_params=_arb(),
    )(h2, x2, target, vecs, ffn_wb, w_up_g, w_down_p)


def _bwd_ffn(dx3, up_pre, h2_t, vecs, ffn_wb, w_up_g, w_down_p, tm):
    s = dx3.shape[0]
    nt = s // tm
    hb = tm // HALO_F

    def body(dx3_ref, up_ref, upg_ref, halo_ref, halog_ref, h2t_ref, v_ref, fw_ref, fwg_ref, wu_ref, wug_ref, wd_ref,
             dh2_ref, dwu_ref, dwd_ref, accf_ref, carry):
        i = pl.program_id(1)
        r = nt - 1 - i

        @pl.when(i == 0)
        def _():
            for ref in (carry, dwu_ref, dwd_ref, accf_ref):
                ref[...] = jnp.zeros_like(ref)

        do2 = (dx3_ref[...] * _row(v_ref, GT2)).astype(BF16)
        df = _dot_nt(do2, wd_ref[...])
        shifted, conv = [], []
        for pre_ref, hl_ref, w_ref in ((up_ref, halo_ref, fw_ref), (upg_ref, halog_ref, fwg_ref)):
            pre = pre_ref[...]
            hl = hl_ref[...]
            e = jnp.concatenate([jnp.where(r > 0, hl, jnp.zeros_like(hl)), pre], axis=0)
            p2 = pltpu.roll(e, 2, 0)[HALO_F:, :]
            p1 = pltpu.roll(e, 1, 0)[HALO_F:, :]
            shifted.append((p2, p1, pre))
            conv.append(_row(w_ref, 3) + _row(w_ref, 0) * p2 + _row(w_ref, 1) * p1 + _row(w_ref, 2) * pre)
        val, gate = conv
        sg = _sigmoid(gate)
        sl = gate * sg
        f_t = (sl * val).astype(BF16).T
        dwd_ref[...] += _dot(f_t, do2)[:NSH, :]
        dups = (df * sl, df * val * (sg * (1.0 + gate * (1.0 - sg))))
        h2t = h2t_ref[...]
        dh2 = jnp.zeros((tm, D), F32)
        for half, (dup, (p2, p1, pre), w_ref, wmat_ref) in enumerate(
                zip(dups, shifted, (fw_ref, fwg_ref), (wu_ref, wug_ref))):
            cols = pl.ds(half * PSH, PSH)
            accf_ref[half, pl.ds(3, 1), :] += _colsum(dup)
            accf_ref[half, pl.ds(0, 1), :] += _colsum(dup * p2)
            accf_ref[half, pl.ds(1, 1), :] += _colsum(dup * p1)
            accf_ref[half, pl.ds(2, 1), :] += _colsum(dup * pre)
            e = jnp.concatenate([dup, carry[:, cols]], axis=0)
            carry[:, cols] = dup[:HALO_F, :]
            dpre = (_row(w_ref, 0) * _shift_up(e, 2)[:tm, :] + _row(w_ref, 1) * _shift_up(e, 1)[:tm, :]
                    + _row(w_ref, 2) * dup).astype(BF16)
            dwu_ref[half] += _dot(h2t, dpre)[:, :NSH]
            dh2 = dh2 + _dot_nt(dpre, wmat_ref[...])
        dh2_ref[...] = dh2

    rev = lambda j, i: nt - 1 - i
    halo = lambda j, i: jnp.maximum((nt - 1 - i) * hb - 1, 0)
    in_specs = [
        pl.BlockSpec((tm, D), lambda j, i: (rev(j, i), 0)),
        pl.BlockSpec((tm, PSH), lambda j, i: (rev(j, i), j)), pl.BlockSpec((tm, PSH), lambda j, i: (rev(j, i), 4 + j)),
        pl.BlockSpec((HALO_F, PSH), lambda j, i: (halo(j, i), j)),
        pl.BlockSpec((HALO_F, PSH), lambda j, i: (halo(j, i), 4 + j)),
        pl.BlockSpec((D, tm), lambda j, i: (0, rev(j, i))), _full((16, D)),
        pl.BlockSpec((8, PSH), lambda j, i: (0, j)), pl.BlockSpec((8, PSH), lambda j, i: (0, 4 + j)),
        pl.BlockSpec((None, D, PSH), lambda j, i: (j, 0, 0)), pl.BlockSpec((None, D, PSH), lambda j, i: (4 + j, 0, 0)),
        pl.BlockSpec((None, PSH, D), lambda j, i: (j, 0, 0))]
    dh2, dw_up, dw_down, accf = pl.pallas_call(
        body, name="bwd_ffn", grid=(4, nt), in_specs=in_specs,
        out_specs=[pl.BlockSpec((None, tm, D), lambda j, i: (j, rev(j, i), 0)),
                   pl.BlockSpec((2, None, D, NSH), lambda j, i: (0, j, 0, 0)),
                   pl.BlockSpec((None, NSH, D), lambda j, i: (j, 0, 0)),
                   pl.BlockSpec((2, None, 8, PSH), lambda j, i: (0, j, 0, 0))],
        out_shape=[jax.ShapeDtypeStruct((4, s, D), F32), jax.ShapeDtypeStruct((2, 4, D, NSH), F32),
                   jax.ShapeDtypeStruct((4, NSH, D), F32), jax.ShapeDtypeStruct((2, 4, 8, PSH), F32)],
        scratch_shapes=[pltpu.VMEM((HALO_F, 2 * PSH), F32)],
        compiler_params=_arb(2),
    )(dx3, up_pre, up_pre, up_pre, up_pre, h2_t, vecs, ffn_wb, ffn_wb, w_up_g, w_up_g, w_down_p)
    return dh2, dw_up.reshape(NDEV, D, NSH), dw_down, accf


def _bwd_mid(dh2, dx3, x2, x2d, o1, z, a0, a1, sp, vecs, v512, conv_w, gm_ws, gm_ws_t, w_out_b, w_in_g, tm):
    s = x2d.shape[0]
    nt = s // tm
    hb = tm // HALO_C
    nc = w_in_g.shape[2]

    def body(dh2a_ref, dh2b_ref, dh2c_ref, dh2d_ref, dx3_ref, x2_ref, x_ref, o1_ref, z_ref, a0_ref, halo_ref, a1_ref, sp_ref, v_ref, p_ref, cw_ref,
             ws_ref, wst_ref, wo_ref, wi_ref, gx_ref, dz_ref, yt_ref, do1_ref, acc_ref, accp_ref, dcw_ref, dws_ref,
             dbst_ref, dbs_s, carry, da1_s, dsp_s, dgvn_s):
        i = pl.program_id(0)
        r = nt - 1 - i

        @pl.when(i == 0)
        def _():
            for ref in (carry, dbs_s, acc_ref, accp_ref, dcw_ref, dws_ref, dbst_ref):
                ref[...] = jnp.zeros_like(ref)

        dh2v = (dh2a_ref[...] + dh2b_ref[...]) + (dh2c_ref[...] + dh2d_ref[...])
        xn2, r2 = _rms(x2_ref[...])
        g2 = _row(v_ref, G2)
        sc2 = 1.0 + _row(v_ref, SC2)
        acc_ref[pl.ds(5, 1), :] += _colsum(dh2v)
        acc_ref[pl.ds(6, 1), :] += _colsum(dh2v * (xn2 * g2))
        acc_ref[pl.ds(7, 1), :] += _colsum(dh2v * sc2 * xn2)
        dx2v = dx3_ref[...] + _rms_bwd(dh2v * sc2 * g2, xn2, r2)
        do1 = (dx2v * _row(v_ref, GT1)).astype(BF16)
        do1_ref[...] = do1
        acc_ref[pl.ds(0, 1), :] += _colsum(dx2v * o1_ref[...])
        dy = _dot_nt(do1, wo_ref[...])
        mog = _row(v_ref, MOG)

        xh, rstd = _ln(a1_ref[...])
        clg = _row(p_ref, CLG)
        a2 = xh * clg + _row(p_ref, CLB)
        s2 = _sigmoid(a2)
        a3 = a2 * s2
        an, ra = _rms(a3)
        dya = dy[:, :DC]
        da3 = _rms_bwd(dya * mog[:, :DC], an, ra)
        da2 = da3 * (s2 * (1.0 + a2 * (1.0 - s2)))
        accp_ref[pl.ds(CLB, 1), :] += _colsum(da2)
        accp_ref[pl.ds(CLG, 1), :] += _colsum(da2 * xh)
        da1 = _ln_bwd(da2 * clg, xh, rstd)
        accp_ref[pl.ds(CB, 1), :] += _colsum(da1)
        da1_s[...] = da1
        for c0 in range(0, DC, LANES):
            cols = pl.ds(c0, LANES)
            d = da1_s[:, cols]
            e = jnp.concatenate([d, carry[:, cols]], axis=0)
            carry[:, cols] = d[:HALO_C, :]
            acc = jnp.zeros((tm, LANES), F32)
            for j in range(KC):
                acc = acc + _shift_up(e, j)[:tm, :] * cw_ref[pl.ds(KC - 1 - j, 1), cols]
            sgc = _sigmoid(z_ref[:, pl.ds(DC + c0, LANES)])
            dz_ref[:, cols] = (acc * sgc).astype(BF16)
            dz_ref[:, pl.ds(DC + c0, LANES)] = (acc * z_ref[:, cols] * sgc * (1.0 - sgc)).astype(BF16)
            halo = halo_ref[:, cols]
            ea = jnp.concatenate([jnp.where(r > 0, halo, jnp.zeros_like(halo)), a0_ref[:, cols]], axis=0)
            for k in range(KC):
                dcw_ref[pl.ds(k, 1), cols] += _colsum(d * _shift_up(ea, HALO_C - (KC - 1) + k)[:tm, :])

        gu_pre = z_ref[:, 2 * DC:3 * DC]
        gv_pre = z_ref[:, 3 * DC:]
        gu, tu = _gelu(gu_pre)
        gvg, tv = _gelu(gv_pre)
        vh, vrstd = _ln(gvg)
        glg = _row(p_ref, GLG)
        gvn = (vh * glg + _row(p_ref, GLB)).astype(BF16)
        spv = sp_ref[...]
        g = gu * spv
        gn, rg = _rms(g)
        yt_ref[...] = jnp.concatenate([an * mog[:, :DC], gn * mog[:, DC:]], axis=1).astype(BF16).T
        acc_ref[pl.ds(4, 1), :] += jnp.concatenate([_colsum(dya * an), _colsum(dy[:, DC:] * gn)], axis=1)
        dg = _rms_bwd(dy[:, DC:] * mog[:, DC:], gn, rg)
        dz_ref[:, pl.ds(2 * DC, DC)] = (dg * spv * _gelu_grad(gu_pre, tu)).astype(BF16)
        dsp_s[...] = dg * gu
        upper = _causal_mask(False)
        first = _first_head_lanes()
        wmt = [jnp.where(upper, wst_ref[h], 0.0).astype(BF16) for h in range(NH)]
        for n in range(tm // CHUNK):
            rows = pl.ds(n * CHUNK, CHUNK)
            for p in range(NH // 2):
                cols = pl.ds(p * CHUNK, CHUNK)
                dsp = dsp_s[rows, cols]
                dbs_s[:, cols] += dsp
                da = jnp.where(first, dsp, 0.0).astype(BF16)
                db = jnp.where(first, 0.0, dsp).astype(BF16)
                v = gvn[n * CHUNK:(n + 1) * CHUNK, p * CHUNK:(p + 1) * CHUNK]
                dws_ref[2 * p] += _dot_nt(da, v)
                dws_ref[2 * p + 1] += _dot_nt(db, v)
                dgvn_s[rows, cols] = _dot(wmt[2 * p], da) + _dot(wmt[2 * p + 1], db)
        dgvn = dgvn_s[...]
        accp_ref[pl.ds(GLB, 1), :] += _colsum(dgvn)
        accp_ref[pl.ds(GLG, 1), :] += _colsum(dgvn * vh)
        dgvg = _ln_bwd(dgvn * glg, vh, vrstd)
        dz_ref[:, pl.ds(3 * DC, DC)] = (dgvg * _gelu_grad(gv_pre, tv)).astype(BF16)

        dh1 = jnp.zeros((tm, D), F32)
        for d in range(NDEV):
            dh1 = dh1 + _dot_nt(dz_ref[:, pl.ds(d * nc, nc)], wi_ref[d])
        xn, r1 = _rms(x_ref[...])
        g1 = _row(v_ref, G1)
        sc = 1.0 + _row(v_ref, SC1)
        acc_ref[pl.ds(1, 1), :] += _colsum(dh1)
        acc_ref[pl.ds(2, 1), :] += _colsum(dh1 * (xn * g1))
        acc_ref[pl.ds(3, 1), :] += _colsum(dh1 * sc * xn)
        gx_ref[...] = dx2v + _rms_bwd(dh1 * sc * g1, xn, r1)

        @pl.when(i == nt - 1)
        def _():
            low = _causal_mask(True)
            for h in range(NH):
                dws_ref[h] = jnp.where(low, dws_ref[h], 0.0)
            lane = lax.broadcasted_iota(jnp.int32, (CHUNK, CHUNK), 1)
            out = jnp.zeros((CHUNK, CHUNK), F32)
            for h in range(NH):
                hs = jnp.sum(dbs_s[:, pl.ds((h // 2) * CHUNK, CHUNK)]
                             * ((lane >= (h % 2) * HD) & (lane < (h % 2 + 1) * HD)).astype(F32),
                             axis=1, keepdims=True)
                out = jnp.where(lane == h, hs, out)
            dbst_ref[...] = out

    tile = lambda w: pl.BlockSpec((tm, w), lambda i: (nt - 1 - i, 0))
    return pl.pallas_call(
        body, name="bwd_mid", grid=(nt,),
        in_specs=[pl.BlockSpec((None, tm, D), functools.partial(lambda k, i: (k, nt - 1 - i, 0), k)) for k in range(4)]
        + [tile(D), tile(D), tile(D), tile(D), tile(4 * DC), tile(DC),
                  pl.BlockSpec((HALO_C, DC), lambda i: (jnp.maximum((nt - 1 - i) * hb - 1, 0), 0)),
                  tile(DC), tile(DC), _full((16, D)), _full((8, DC)), _full((32, DC)),
                  _full((NH, CHUNK, CHUNK)), _full((NH, CHUNK, CHUNK)), _full((D, D)), _full((NDEV, D, nc))],
        out_specs=[tile(D), tile(4 * DC), pl.BlockSpec((D, tm), lambda i: (0, nt - 1 - i)), tile(D),
                   _full((16, D)), _full((8, DC)), _full((32, DC)),
                   _full((NH, CHUNK, CHUNK)), _full((CHUNK, CHUNK))],
        out_shape=[jax.ShapeDtypeStruct((s, D), F32), jax.ShapeDtypeStruct((s, 4 * DC), BF16),
                   jax.ShapeDtypeStruct((D, s), BF16), jax.ShapeDtypeStruct((s, D), BF16),
                   jax.ShapeDtypeStruct((16, D), F32), jax.ShapeDtypeStruct((8, DC), F32),
                   jax.ShapeDtypeStruct((32, DC), F32), jax.ShapeDtypeStruct((NH, CHUNK, CHUNK), F32),
                   jax.ShapeDtypeStruct((CHUNK, CHUNK), F32)],
        scratch_shapes=[pltpu.VMEM((CHUNK, DC), F32), pltpu.VMEM((HALO_C, DC), F32), pltpu.VMEM((tm, DC), F32),
                        pltpu.VMEM((tm, DC), F32), pltpu.VMEM((tm, DC), F32)],
        compiler_params=_arb(),
    )(dh2, dh2, dh2, dh2, dx3, x2, x2d, o1, z, a0, a0, a1, sp, vecs, v512, conv_w, gm_ws, gm_ws_t, w_out_b, w_in_g)


def _mm_all_slots(name, at, b, bw, tk):
    k1, s = at.shape
    nslot = b.shape[1] // bw

    def body(a_ref, b_ref, o_ref):
        @pl.when(pl.program_id(0) == 0)
        def _():
            o_ref[...] = jnp.zeros_like(o_ref)

        t = _dot(a_ref[...], b_ref[...])
        for j in range(nslot):
            o_ref[j] += t[:, j * bw:(j + 1) * bw]

    return pl.pallas_call(
        body, name=name, grid=(s // tk,),
        in_specs=[pl.BlockSpec((k1, tk), lambda k: (0, k)), pl.BlockSpec((tk, nslot * bw), lambda k: (k, 0))],
        out_specs=_full((nslot, k1, bw)), out_shape=jax.ShapeDtypeStruct((nslot, k1, bw), F32),
        compiler_params=_arb(),
    )(at, b)


def _adam_math(w, g, m, v):
    m = ADAM_B1 * m + (1.0 - ADAM_B1) * g
    v = ADAM_B2 * v + (1.0 - ADAM_B2) * (g * g)
    m_hat = m / (1.0 - ADAM_B1 ** ADAM_STEP)
    v_hat = v / (1.0 - ADAM_B2 ** ADAM_STEP)
    delta = -ADAM_LR * (m_hat / (jnp.sqrt(v_hat) + ADAM_EPS) + ADAM_WD * w)
    return delta, m, v


def _row_block(rows, cols):
    tr = rows
    while tr * cols * 4 > (2 << 20) and tr % 32 == 0:
        tr //= 2
    return tr


def _adam3(name, w, g, m, v):
    _, rows, cols = w.shape
    tr = _row_block(rows, cols)

    def body(w_ref, g_ref, m_ref, v_ref, d_ref, mo_ref, vo_ref):
        d_ref[...], mo_ref[...], vo_ref[...] = _adam_math(w_ref[...], g_ref[...], m_ref[...], v_ref[...])

    spec = pl.BlockSpec((1, tr, cols), lambda i: (0, i, 0))
    return pl.pallas_call(
        body, name=name, grid=(rows // tr,), in_specs=[spec] * 4, out_specs=[spec] * 3,
        out_shape=[jax.ShapeDtypeStruct(w.shape, F32)] * 3, compiler_params=_arb(),
    )(w, g, m, v)


def _sum_adam(name, parts, w, m, v):
    n, rows, cols = parts.shape
    tr = _row_block(rows, cols)

    def body(p_ref, w_ref, m_ref, v_ref, g_ref, d_ref, mo_ref, vo_ref):
        g = p_ref[0].astype(F32)
        for k in range(1, n):
            g = g + p_ref[k].astype(F32)
        g_ref[0] = g
        d_ref[0], mo_ref[0], vo_ref[0] = _adam_math(w_ref[0], g, m_ref[0], v_ref[0])

    spec = pl.BlockSpec((1, tr, cols), lambda i: (0, i, 0))
    return pl.pallas_call(
        body, name=name, grid=(rows // tr,),
        in_specs=[pl.BlockSpec((n, tr, cols), lambda i: (0, i, 0))] + [spec] * 3, out_specs=[spec] * 4,
        out_shape=[jax.ShapeDtypeStruct(w.shape, F32)] * 4, compiler_params=_arb(),
    )(parts, w, m, v)


def _pair_add(name, g4, recv, core):
    _, _, rows, cols = g4.shape
    tr = _row_block(rows, cols)

    def body(c_ref, a_ref, b_ref, o_ref):
        o_ref[...] = (a_ref[...] + b_ref[...]).astype(BF16)

    return pl.pallas_call(
        body, name=name,
        grid_spec=pltpu.PrefetchScalarGridSpec(
            num_scalar_prefetch=1, grid=(4, rows // tr),
            in_specs=[pl.BlockSpec((None, None, tr, cols), lambda k, i, c_ref: (k, c_ref[0], i, 0)),
                      pl.BlockSpec((None, tr, cols), lambda k, i, c_ref: (k, i, 0))],
            out_specs=pl.BlockSpec((None, tr, cols), lambda k, i, c_ref: (k, i, 0))),
        out_shape=jax.ShapeDtypeStruct((4, rows, cols), BF16), compiler_params=_arb(2),
    )(core, g4, recv)


def _sum_small(rows_all, p_all, ws_all, bst_all, fw_all, cw_all):
    def body(a_ref, p_ref, ws_ref, bst_ref, fw_ref, cw_ref,
             g_b_ada, g_n1, g_mog, g_n2, g_gf, g_cb, g_clg, g_clb, g_glg, g_glb, g_ws, g_bs, fw_sum, cw_sum):
        def total(ref):
            t = ref[0]
            for k in range(1, NDEV):
                t = t + ref[k]
            return t

        a = total(a_ref)
        g_b_ada[...] = jnp.concatenate([a[k:k + 1, :] for k in range(6)], axis=1)
        g_n1[...] = a[6:7, :]
        g_mog[...] = a[7:8, :]
        g_n2[...] = a[8:9, :]
        g_gf[...] = a[9:10, :].reshape(D)
        p = total(p_ref)
        for k, ref in zip((CB, CLG, CLB, GLG, GLB), (g_cb, g_clg, g_clb, g_glg, g_glb)):
            ref[...] = p[k:k + 1, :]
        g_ws[0] = total(ws_ref)
        g_bs[0] = jnp.transpose(total(bst_ref))[:NH, :]
        fw_sum[...] = total(fw_ref)
        cw_sum[...] = total(cw_ref)

    vec = lambda n: jax.ShapeDtypeStruct((1, n), F32)
    return pl.pallas_call(
        body, name="sum_small_grads",
        out_shape=[vec(6 * D), vec(D), vec(D), vec(D), jax.ShapeDtypeStruct((D,), F32),
                   vec(DC), vec(DC), vec(DC), vec(DC), vec(DC),
                   jax.ShapeDtypeStruct((1, NH, CHUNK, CHUNK), F32), jax.ShapeDtypeStruct((1, NH, CHUNK), F32),
                   jax.ShapeDtypeStruct((8, 2 * PFF), F32), jax.ShapeDtypeStruct((32, DC), F32)],
    )(rows_all, p_all, ws_all, bst_all, fw_all, cw_all)


def _adam_small(quads):
    n = len(quads)

    def body(*refs):
        ins, outs = refs[:4 * n], refs[4 * n:]
        for q in range(n):
            w, g, m, v = (r[...] for r in ins[4 * q:4 * q + 4])
            outs[3 * q][...], outs[3 * q + 1][...], outs[3 * q + 2][...] = _adam_math(w, g, m, v)

    flat = [a for q in quads for a in q]
    outs = pl.pallas_call(
        body, name="adam_small",
        out_shape=[jax.ShapeDtypeStruct(q[0].shape, F32) for q in quads for _ in range(3)],
    )(*flat)
    return [tuple(outs[3 * q:3 * q + 3]) for q in range(n)]


def kernel(x, c, w_ada, b_ada, norm1_gain, w_in, conv_dw_w, conv_dw_b, conv_ln_g, conv_ln_b, gm_ln_g, gm_ln_b, gm_ws, gm_bs, mix_out_gain, w_out, norm2_gain, w_up, ffn_dw_w, ffn_dw_b, w_down, final_gain, loss_target, m_w_ada, m_b_ada, m_norm1_gain, m_w_in, m_conv_dw_w, m_conv_dw_b, m_conv_ln_g, m_conv_ln_b, m_gm_ln_g, m_gm_ln_b, m_gm_ws, m_gm_bs, m_mix_out_gain, m_w_out, m_norm2_gain, m_w_up, m_ffn_dw_w, m_ffn_dw_b, m_w_down, m_final_gain, v_w_ada, v_b_ada, v_norm1_gain, v_w_in, v_conv_dw_w, v_conv_dw_b, v_conv_ln_g, v_conv_ln_b, v_gm_ln_g, v_gm_ln_b, v_gm_ws, v_gm_bs, v_mix_out_gain, v_w_out, v_norm2_gain, v_w_up, v_ffn_dw_w, v_ffn_dw_b, v_w_down, v_final_gain):
    s = x.shape[1]
    ax, ay, ac = _place()
    me = 4 * ax + 2 * ay + ac
    n_ada = w_ada.shape[2]
    n_cw = conv_dw_w.shape[2]
    x2d = x[0]
    target = loss_target[0]
    pad_sh = lambda a: jnp.pad(a, [(0, 0)] * (a.ndim - 1) + [(0, PSH - NSH)])

    gather_in, token_a = _start_gather("gather_in", [w_in[0].astype(BF16)], me)

    c_all, cw_all, fw_all = _all_gather("gather_small", [c + token_a[0, 0], conv_dw_w[0], ffn_dw_w[0]])
    conv_w = jnp.pad(jnp.transpose(cw_all, (1, 0, 2)).reshape(KC, DC), ((0, 32 - KC), (0, 0)))
    ffn_w = jnp.transpose(pad_sh(fw_all), (1, 0, 2)).reshape(KF, 2 * PFF)
    ffn_b = pad_sh(ffn_dw_b.reshape(NDEV, NSH)).reshape(1, 2 * PFF)
    ffn_wb = jnp.concatenate([ffn_w, ffn_b, jnp.zeros((8 - KF - 1, 2 * PFF), F32)], axis=0)

    b_cols = lax.dynamic_slice(b_ada, (0, me * n_ada), (1, n_ada))
    (mod_all,) = _all_gather("gather_mod", [_mod_part(c_all, w_ada, b_cols)])
    shards, mod_all = lax.optimization_barrier((
        (w_out[0].astype(BF16), pad_sh(w_up[0].astype(BF16)), w_down[0].astype(BF16)), mod_all))
    gather_out, token_b = _start_direct_gather("gather_out", [shards[0]], me)
    gather_ffn, token_c = _start_gather("gather_up_down", [shards[1], shards[2]], me)
    mod = lax.dynamic_index_in_dim(mod_all, me, axis=1, keepdims=False).reshape(6, D)
    sh1, sc1, gt1, sh2, sc2, gt2 = [mod[k:k + 1] for k in range(6)]
    vecs = jnp.concatenate([norm1_gain, sh1, sc1, gt1, norm2_gain, sh2, sc2, gt2, mix_out_gain,
                            final_gain.reshape(1, D), jnp.zeros((6, D), F32)], axis=0)
    vecs = vecs + (token_b[0, 0] + token_c[0, 0])
    v512 = jnp.concatenate([conv_dw_b, conv_ln_g, conv_ln_b, gm_ln_g, gm_ln_b, jnp.zeros((3, DC), F32)], axis=0)
    bs_exp = jnp.repeat(jnp.transpose(gm_bs[0]), HD, axis=1)
    gm_ws_t = jnp.swapaxes(gm_ws[0], 1, 2)

    tm_in = min(512, s)
    tm = min(256, s)
    (w_in_g,) = _finish_gather("gather_in", gather_in, vecs)
    z, a0, h1_t = _fwd_in(x2d, vecs, w_in_g, tm_in)
    (w_out_g,) = _finish_direct_gather("gather_out", gather_out, a0)
    w_out_b = w_out_g.reshape(D, D)
    a1, sp, x2, o1, h2, h2_t = _fwd_mid(a0, z, x2d, vecs, v512, conv_w, gm_ws, bs_exp, w_out_b, tm)
    w_up_g, w_down_g = _finish_gather("gather_up_down", gather_ffn, h2)
    w_down_p = jnp.pad(w_down_g.reshape(4, NSH, D), ((0, 0), (0, PSH - NSH), (0, 0)))
    up_pre, dx3, acc_f = _fwd_ffn(h2, x2, target, vecs, ffn_wb, w_up_g, w_down_p, tm)
    loss = lax.psum(jnp.sum(acc_f[1]), ("x", "y", "c"))

    core = ac.reshape(1).astype(jnp.int32)
    mychip = 2 * ax + ay

    def to_pairs(named):
        g4s = [g.reshape((4, 2) + g.shape[1:]) for _, g in named]
        from_sibling = _sibling_swap("rs_sibling_" + named[0][0], g4s)
        return [_pair_add("rs_pair_add_" + t[0], g4, rv, core) for t, g4, rv in zip(named, g4s, from_sibling)]

    dh2, dw_up, dw_down, acc_fw = _bwd_ffn(dx3, up_pre, h2_t, vecs, ffn_wb, w_up_g, w_down_p, tm)
    acc_fw = jnp.transpose(acc_fw, (2, 0, 1, 3)).reshape(8, 2 * PFF)
    exchange_ffn, token_x = _start_exchange("rs_chips_ffn", to_pairs(
        [("w_up", dw_up), ("w_down", dw_down.reshape(NDEV, w_down.shape[1], D))]), mychip)
    gx, dz, y_t, do1, acc_m, acc_p, dcw, dws, dbs_t = _bwd_mid(
        dh2, dx3, x2, x2d, o1, z, a0, a1, sp, vecs + token_x[0, 0], v512, conv_w, gm_ws[0], gm_ws_t, w_out_b, w_in_g, tm)
    rows = jnp.concatenate([acc_m[1:3], acc_m[0:1], acc_m[5:7], acc_f[2:3], acc_m[3:5], acc_m[7:8], acc_f[0:1],
                            jnp.zeros((6, D), F32)], axis=0)
    small_gather, token_s = _start_gather("gather_small_grads", [rows, acc_p, dws, dbs_t, acc_fw, dcw], me)
    h1_t, _ = lax.optimization_barrier((h1_t, token_s))
    dw_in = _mm_all_slots("dw_in", h1_t, dz, w_in.shape[2], min(1024, s))
    small_pass, token_p = _pass_gather("gather_small_grads", small_gather, dw_in)
    y_t, _ = lax.optimization_barrier((y_t, token_p))
    dw_out = _mm_all_slots("dw_out", y_t, do1, D, min(2048, s)).reshape(NDEV, w_out.shape[1], D)
    exchange_mix, token_m = _start_exchange("rs_chips_mix", to_pairs([("w_in", dw_in), ("w_out", dw_out)]), mychip)

    rows_all, p_all, ws_all, bst_all, fwg_all, cwg_all = _end_gather("gather_small_grads", small_pass, token_m)
    (g_b_ada, g_n1, g_mog, g_n2, g_gf, g_cb, g_clg, g_clb, g_glg, g_glb, g_ws, g_bs, fw_sum, cw_sum) = _sum_small(
        rows_all, p_all, ws_all, bst_all, fwg_all, cwg_all)
    g_fb = fw_sum[3].reshape(NDEV, PSH)[:, :NSH].reshape(ffn_dw_b.shape)
    g_fw = lax.dynamic_index_in_dim(fw_sum[:KF].reshape(KF, NDEV, PSH), me, axis=1, keepdims=False)[:, :NSH]
    g_fw = g_fw.reshape(ffn_dw_w.shape)
    g_cw = lax.dynamic_slice(cw_sum, (0, me * n_cw), (KC, n_cw)).reshape(conv_dw_w.shape)
    small = [
        (b_ada, g_b_ada, m_b_ada, v_b_ada), (norm1_gain, g_n1, m_norm1_gain, v_norm1_gain),
        (conv_dw_w, g_cw, m_conv_dw_w, v_conv_dw_w), (conv_dw_b, g_cb, m_conv_dw_b, v_conv_dw_b),
        (conv_ln_g, g_clg, m_conv_ln_g, v_conv_ln_g), (conv_ln_b, g_clb, m_conv_ln_b, v_conv_ln_b),
        (gm_ln_g, g_glg, m_gm_ln_g, v_gm_ln_g), (gm_ln_b, g_glb, m_gm_ln_b, v_gm_ln_b),
        (gm_ws, g_ws, m_gm_ws, v_gm_ws), (gm_bs, g_bs, m_gm_bs, v_gm_bs),
        (mix_out_gain, g_mog, m_mix_out_gain, v_mix_out_gain), (norm2_gain, g_n2, m_norm2_gain, v_norm2_gain),
        (ffn_dw_w, g_fw, m_ffn_dw_w, v_ffn_dw_w), (ffn_dw_b, g_fb, m_ffn_dw_b, v_ffn_dw_b),
        (final_gain, g_gf, m_final_gain, v_final_gain)]
    small_out = _adam_small(small)
    res = {}
    for name, q, o in zip(("b_ada", "norm1_gain", "conv_dw_w", "conv_dw_b", "conv_ln_g", "conv_ln_b", "gm_ln_g",
                           "gm_ln_b", "gm_ws", "gm_bs", "mix_out_gain", "norm2_gain", "ffn_dw_w", "ffn_dw_b",
                           "final_gain"), small, small_out):
        res[name] = (q[1],) + o

    dmod_all = rows_all[:, :6].reshape(NDEV, 6 * D)
    dm_cols = lax.dynamic_slice(dmod_all, (0, me * n_ada), (NDEV, n_ada))
    g_ada = _ada_grad(jnp.transpose(c_all[:, 0, :]), dm_cols)
    res["w_ada"] = (g_ada,) + tuple(_adam3("adam_ada", w_ada, g_ada, m_w_ada, v_w_ada))

    big = [("w_up", w_up, m_w_up, v_w_up), ("w_down", w_down, m_w_down, v_w_down),
           ("w_in", w_in, m_w_in, v_w_in), ("w_out", w_out, m_w_out, v_w_out)]
    from_chips = list(_finish_exchange("rs_chips_ffn", exchange_ffn, res["w_ada"][1]))
    for t, parts in zip(big[:2], from_chips):
        res[t[0]] = tuple(_sum_adam("rs_sum_adam_" + t[0], parts, t[1], t[2], t[3]))
    from_chips = list(_finish_exchange("rs_chips_mix", exchange_mix, res["w_down"][1]))
    for t, parts in zip(big[2:], from_chips):
        res[t[0]] = tuple(_sum_adam("rs_sum_adam_" + t[0], parts, t[1], t[2], t[3]))

    order = ("w_ada", "b_ada", "norm1_gain", "w_in", "conv_dw_w", "conv_dw_b", "conv_ln_g", "conv_ln_b", "gm_ln_g",
             "gm_ln_b", "gm_ws", "gm_bs", "mix_out_gain", "w_out", "norm2_gain", "w_up", "ffn_dw_w", "ffn_dw_b",
             "w_down", "final_gain")
    return (loss, gx.reshape(x.shape), *[res[n][0] for n in order], *[res[n][1] for n in order],
            *[res[n][2] for n in order], *[res[n][3] for n in order])
```

```python
import functools

import jax
import jax.numpy as jnp
from jax import lax
from jax.experimental import pallas as pl
from jax.experimental.pallas import tpu as pltpu

F32 = jnp.float32
BF16 = jnp.bfloat16
NDEV = 8
D = 1024
DC = 512
DFF = 2816
NSH = 704
PSH = 768
PFF = 4 * PSH
KC = 31
KF = 3
CHUNK = 128
NH = 8
HD = 64
HALO_C = 32
HALO_F = 8
LANES = 128
RMS_EPS = 1e-6
LN_EPS = 1e-5
ADAM_LR = 0.001
ADAM_B1 = 0.9
ADAM_B2 = 0.999
ADAM_EPS = 1e-08
ADAM_WD = 0.01
ADAM_STEP = 10
GELU_K = 0.7978845608028654
GELU_C = 0.044715

MESH = pl.DeviceIdType.MESH
ANY = pl.BlockSpec(memory_space=pl.ANY)

G1, SH1, SC1, GT1, G2, SH2, SC2, GT2, MOG, GF = range(10)
CB, CLG, CLB, GLG, GLB = range(5)


def _full(shape):
    return pl.BlockSpec(shape, lambda *_: (0,) * len(shape))


def _arb(n=1):
    return pltpu.CompilerParams(dimension_semantics=("arbitrary",) * n)


def _row(ref, r):
    return ref[pl.ds(r, 1), :]


def _colsum(v):
    return jnp.sum(v, axis=0, keepdims=True)


def _rowmean(v):
    return jnp.mean(v, axis=-1, keepdims=True)


def _rms(x):
    r = lax.rsqrt(_rowmean(x * x) + RMS_EPS)
    return x * r, r


def _rms_bwd(dxn, xn, r):
    return r * (dxn - xn * _rowmean(dxn * xn))


def _ln(x):
    mu = _rowmean(x)
    xc = x - mu
    rstd = lax.rsqrt(_rowmean(xc * xc) + LN_EPS)
    return xc * rstd, rstd


def _ln_bwd(dxh, xhat, rstd):
    return rstd * (dxh - _rowmean(dxh) - xhat * _rowmean(dxh * xhat))


def _sigmoid(x):
    return 0.5 * jnp.tanh(0.5 * x) + 0.5


def _gelu(x):
    t = jnp.tanh(GELU_K * (x + GELU_C * x * x * x))
    return 0.5 * x * (1.0 + t), t


def _gelu_grad(x, t):
    return 0.5 * (1.0 + t) + 0.5 * x * (1.0 - t * t) * (GELU_K * (1.0 + 3.0 * GELU_C * x * x))


def _dot(a, b):
    return jnp.dot(a, b, preferred_element_type=F32)


def _dot_nt(a, b):
    return lax.dot_general(a, b, (((1,), (1,)), ((), ())), preferred_element_type=F32)


def _shift_up(e, s):
    n = e.shape[0]
    return pltpu.roll(e, (n - s) % n, 0)


def _place():
    return lax.axis_index("x"), lax.axis_index("y"), lax.axis_index("c")


def _all_gather(name, xs):
    n = len(xs)

    def body(*refs):
        x_refs, out_refs = refs[:n], refs[n:2 * n]
        send_sems, recv_sems, local_sems = refs[2 * n:]
        x, y, c = _place()
        me, sibling = (x, y, c), (x, y, 1 - c)
        chips = [(1 - x, y), (x, 1 - y), (1 - x, 1 - y)]

        def copy(a, k, block, to, own=False):
            px, py, pc = block
            slot = out_refs[a].at[4 * px + 2 * py + pc]
            return pltpu.make_async_remote_copy(
                src_ref=x_refs[a] if own else slot, dst_ref=slot,
                send_sem=send_sems.at[7 * a + k], recv_sem=recv_sems.at[7 * a + k], device_id=to, device_id_type=MESH)

        mine = [pltpu.make_async_copy(x_refs[a], out_refs[a].at[4 * x + 2 * y + c], local_sems.at[a]) for a in range(n)]
        for cp in mine:
            cp.start()
        first = []
        for a in range(n):
            first.append(copy(a, 0, me, sibling, own=True))
            first += [copy(a, 1 + j, me, (*chip, c), own=True) for j, chip in enumerate(chips)]
        for cp in first:
            cp.start()
        passed = []
        for j, chip in enumerate(chips):
            for a in range(n):
                copy(a, 1 + j, (*chip, c), me).wait_recv()
                cp = copy(a, 4 + j, (*chip, c), sibling)
                cp.start()
                passed.append(cp)
        for a in range(n):
            copy(a, 0, sibling, me).wait_recv()
            for j, chip in enumerate(chips):
                copy(a, 4 + j, (*chip, 1 - c), me).wait_recv()
        for cp in first + passed:
            cp.wait_send()
        for cp in mine:
            cp.wait()

    return pl.pallas_call(
        body, name=name, out_shape=[jax.ShapeDtypeStruct((NDEV,) + a.shape, a.dtype) for a in xs],
        in_specs=[ANY] * n, out_specs=[ANY] * n,
        scratch_shapes=[pltpu.SemaphoreType.DMA((7 * n,)), pltpu.SemaphoreType.DMA((7 * n,)),
                        pltpu.SemaphoreType.DMA((n,))],
    )(*xs)


def _sibling_swap(name, g4s):
    n = len(g4s)

    def body(*refs):
        g_refs, out_refs = refs[:n], refs[n:2 * n]
        send_sems, recv_sems = refs[2 * n:]
        x, y, c = _place()
        cps = [pltpu.make_async_remote_copy(
            src_ref=g_refs[a].at[k, 1 - c], dst_ref=out_refs[a].at[k],
            send_sem=send_sems.at[4 * a + k], recv_sem=recv_sems.at[4 * a + k],
            device_id=(x, y, 1 - c), device_id_type=MESH) for a in range(n) for k in range(4)]
        for cp in cps:
            cp.start()
        for cp in cps:
            cp.wait()

    return pl.pallas_call(
        body, name=name, out_shape=[jax.ShapeDtypeStruct((4,) + g.shape[2:], g.dtype) for g in g4s],
        in_specs=[ANY] * n, out_specs=[ANY] * n,
        scratch_shapes=[pltpu.SemaphoreType.DMA((4 * n,)), pltpu.SemaphoreType.DMA((4 * n,))],
    )(*g4s)


HBM = pl.BlockSpec(memory_space=pltpu.HBM)
SEM = pl.BlockSpec(memory_space=pltpu.SEMAPHORE)
EFFECT = pltpu.SideEffectType.DATAFLOW_SIDE_EFFECTING


def _in_hbm(a):
    return pltpu.with_memory_space_constraint(a, pltpu.HBM)


def _split_start(name, bufs, copies):
    n = len(bufs)

    def body(*refs):
        for cp in copies(refs[:n], refs[n], refs[n + 1]):
            cp.start()
        refs[-1][...] = jnp.zeros_like(refs[-1])

    out = pl.pallas_call(
        body, name=name,
        out_shape=(pltpu.SemaphoreType.DMA((copies.count,)), pltpu.SemaphoreType.DMA((copies.count,)),
                   *[pltpu.HBM(a.shape, a.dtype) for a in bufs], jax.ShapeDtypeStruct((8, LANES), F32)),
        in_specs=[HBM] * n, out_specs=(SEM, SEM, *[HBM] * n, pl.BlockSpec(memory_space=pltpu.VMEM)),
        input_output_aliases={i: 2 + i for i in range(n)},
        compiler_params=pltpu.CompilerParams(has_side_effects=EFFECT),
    )(*[_in_hbm(a) for a in bufs])
    return (out[0], out[1], list(out[2:2 + n])), out[-1]


def _split_wait(name, handle, copies, after):
    send_sems, recv_sems, bufs = handle
    n = len(bufs)

    def body(*refs):
        for cp in copies(refs[:n], refs[n], refs[n + 1]):
            cp.wait_send()
            cp.wait_recv()

    out = pl.pallas_call(
        body, name=name, out_shape=tuple(pltpu.HBM(a.shape, a.dtype) for a in bufs),
        in_specs=[HBM] * n + [SEM, SEM, pl.BlockSpec(memory_space=pl.ANY)], out_specs=tuple([HBM] * n),
        input_output_aliases={i: i for i in range(n)},
        compiler_params=pltpu.CompilerParams(has_side_effects=EFFECT),
    )(*bufs, send_sems, recv_sems, after)
    return list(out)


class _GatherFirstCopies:
    def __init__(self, n):
        self.n, self.count = n, 4 * n

    def __call__(self, refs, send_sems, recv_sems):
        x, y, c = _place()
        peers = [(x, y, 1 - c), (1 - x, y, c), (x, 1 - y, c), (1 - x, 1 - y, c)]
        return [pltpu.make_async_remote_copy(
            src_ref=refs[a], dst_ref=refs[self.n + a].at[4 * x + 2 * y + c],
            send_sem=send_sems.at[4 * a + k], recv_sem=recv_sems.at[4 * a + k], device_id=peer, device_id_type=MESH)
            for a in range(self.n) for k, peer in enumerate(peers)]


class _GatherPassCopies:
    def __init__(self, n):
        self.n, self.count = n, 3 * n

    def __call__(self, refs, send_sems, recv_sems):
        x, y, c = _place()
        cps = []
        for a in range(self.n):
            for j, (px, py) in enumerate([(1 - x, y), (x, 1 - y), (1 - x, 1 - y)]):
                slot = refs[a].at[4 * px + 2 * py + c]
                cps.append(pltpu.make_async_remote_copy(
                    src_ref=slot, dst_ref=slot, send_sem=send_sems.at[3 * a + j], recv_sem=recv_sems.at[3 * a + j],
                    device_id=(x, y, 1 - c), device_id_type=MESH))
        return cps


class _ExchangeCopies:
    def __init__(self, n):
        self.n, self.count = n, 3 * n

    def __call__(self, refs, send_sems, recv_sems):
        x, y, c = _place()
        cps = []
        for a in range(self.n):
            for j, (px, py) in enumerate([(1 - x, y), (x, 1 - y), (1 - x, 1 - y)]):
                cps.append(pltpu.make_async_remote_copy(
                    src_ref=refs[a].at[2 * px + py], dst_ref=refs[self.n + a].at[2 * x + y],
                    send_sem=send_sems.at[3 * a + j], recv_sem=recv_sems.at[3 * a + j],
                    device_id=(px, py, c), device_id_type=MESH))
        return cps


def _own_slot(nslot, src, index):
    land = lax.empty((nslot,) + src.shape, src.dtype)
    return lax.dynamic_update_slice(land, src[None], (index,) + (0,) * src.ndim)


def _start_gather(tag, xs, me):
    lands = [_own_slot(NDEV, a, me) for a in xs]
    return _split_start(tag + "_start", list(xs) + lands, _GatherFirstCopies(len(xs)))


def _pass_gather(tag, handle, after):
    n = len(handle[2]) // 2
    lands = _split_wait(tag + "_wait", handle, _GatherFirstCopies(n), after)[n:]
    return _split_start(tag + "_pass", lands, _GatherPassCopies(n))


def _end_gather(tag, passing, after):
    return _split_wait(tag + "_pass_wait", passing, _GatherPassCopies(len(passing[2])), after)


def _finish_gather(tag, handle, after):
    passing, token = _pass_gather(tag, handle, after)
    return _end_gather(tag, passing, token)


def _start_exchange(tag, hs, mychip):
    lands = [_own_slot(4, lax.dynamic_index_in_dim(h, mychip, 0, keepdims=False), mychip) for h in hs]
    return _split_start(tag + "_start", list(hs) + lands, _ExchangeCopies(len(hs)))


def _finish_exchange(tag, handle, after):
    n = len(handle[2]) // 2
    return _split_wait(tag + "_wait", handle, _ExchangeCopies(n), after)[n:]


def _mod_part(c_all, w_ada, b_cols):
    ncol = w_ada.shape[2]

    def body(c_ref, w_ref, b_ref, o_ref):
        cv = c_ref[:, 0, :]
        ca = cv * _sigmoid(cv)
        o_ref[...] = _dot(ca.astype(BF16), w_ref[0].astype(BF16)) + b_ref[...]

    return pl.pallas_call(body, name="mod_part", out_shape=jax.ShapeDtypeStruct((NDEV, ncol), F32))(
        c_all, w_ada, b_cols)


def _ada_grad(c_all_t, dmod_cols):
    ncol = dmod_cols.shape[1]

    def body(ct_ref, dm_ref, o_ref):
        ct = ct_ref[...]
        ca = ct * _sigmoid(ct)
        acc = jnp.zeros((D, ncol), F32)
        for b in range(NDEV):
            acc = acc + ca[:, b:b + 1] * dm_ref[pl.ds(b, 1), :]
        o_ref[0] = acc

    return pl.pallas_call(body, name="ada_grad", out_shape=jax.ShapeDtypeStruct((1, D, ncol), F32))(
        c_all_t, dmod_cols)


def _fwd_in(x2d, vecs, w_in_g, tm):
    s = x2d.shape[0]
    nc = w_in_g.shape[2]

    def body(x_ref, v_ref, w_ref, z_ref, a0_ref, h1t_ref):
        xn, _ = _rms(x_ref[...])
        h = (xn * _row(v_ref, G1)) * (1.0 + _row(v_ref, SC1)) + _row(v_ref, SH1)
        hb = h.astype(BF16)
        h1t_ref[...] = hb.T
        for d in range(NDEV):
            z_ref[:, pl.ds(d * nc, nc)] = _dot(hb, w_ref[d])
        a0_ref[...] = z_ref[:, :DC] * _sigmoid(z_ref[:, DC:2 * DC])

    return pl.pallas_call(
        body, name="fwd_in", grid=(s // tm,),
        in_specs=[pl.BlockSpec((tm, D), lambda i: (i, 0)), _full((16, D)), _full((NDEV, D, nc))],
        out_specs=[pl.BlockSpec((tm, 4 * DC), lambda i: (i, 0)), pl.BlockSpec((tm, DC), lambda i: (i, 0)),
                   pl.BlockSpec((D, tm), lambda i: (0, i))],
        out_shape=[jax.ShapeDtypeStruct((s, 4 * DC), F32), jax.ShapeDtypeStruct((s, DC), F32),
                   jax.ShapeDtypeStruct((D, s), BF16)],
        compiler_params=_arb(),
    )(x2d, vecs, w_in_g)


def _causal_mask(lower):
    r = lax.broadcasted_iota(jnp.int32, (CHUNK, CHUNK), 0)
    c = lax.broadcasted_iota(jnp.int32, (CHUNK, CHUNK), 1)
    return (r >= c) if lower else (r <= c)


def _first_head_lanes():
    return lax.broadcasted_iota(jnp.int32, (CHUNK, CHUNK), 1) < HD


def _fwd_mid(a0, z, x2d, vecs, v512, conv_w, gm_ws, bs_exp, w_out_b, tm):
    s = x2d.shape[0]
    hb = tm // HALO_C

    def body(a0_ref, halo_ref, zg_ref, x_ref, v_ref, p_ref, cw_ref, ws_ref, bs_ref, wo_ref,
             a1_ref, sp_ref, x2_ref, o1_ref, h2_ref, h2t_ref):
        i = pl.program_id(0)
        for c0 in range(0, DC, LANES):
            cols = pl.ds(c0, LANES)
            halo = halo_ref[:, cols]
            e = jnp.concatenate([jnp.where(i > 0, halo, jnp.zeros_like(halo)), a0_ref[:, cols]], axis=0)
            acc = jnp.broadcast_to(p_ref[pl.ds(CB, 1), cols], (tm, LANES))
            for k in range(KC):
                acc = acc + _shift_up(e, HALO_C - (KC - 1) + k)[:tm, :] * cw_ref[pl.ds(k, 1), cols]
            a1_ref[:, cols] = acc
        xh, _ = _ln(a1_ref[...])
        a2 = xh * _row(p_ref, CLG) + _row(p_ref, CLB)
        a3 = a2 * _sigmoid(a2)
        gu, _ = _gelu(zg_ref[:, :DC])
        gvg, _ = _gelu(zg_ref[:, DC:])
        vh, _ = _ln(gvg)
        gvn = (vh * _row(p_ref, GLG) + _row(p_ref, GLB)).astype(BF16)
        low = _causal_mask(True)
        first = _first_head_lanes()
        wm = [jnp.where(low, ws_ref[0, h], 0.0).astype(BF16) for h in range(NH)]
        for n in range(tm // CHUNK):
            for p in range(NH // 2):
                v = gvn[n * CHUNK:(n + 1) * CHUNK, p * CHUNK:(p + 1) * CHUNK]
                blk = jnp.where(first, _dot(wm[2 * p], v), _dot(wm[2 * p + 1], v))
                sp_ref[pl.ds(n * CHUNK, CHUNK), pl.ds(p * CHUNK, CHUNK)] = blk + bs_ref[:, pl.ds(p * CHUNK, CHUNK)]
        g = gu * sp_ref[...]
        an, _ = _rms(a3)
        gn, _ = _rms(g)
        mog = _row(v_ref, MOG)
        y = jnp.concatenate([an * mog[:, :DC], gn * mog[:, DC:]], axis=1).astype(BF16)
        o1 = _dot(y, wo_ref[...])
        o1_ref[...] = o1
        x2 = x_ref[...] + _row(v_ref, GT1) * o1
        x2_ref[...] = x2
        xn2, _ = _rms(x2)
        h2 = (xn2 * _row(v_ref, G2)) * (1.0 + _row(v_ref, SC2)) + _row(v_ref, SH2)
        h2b = h2.astype(BF16)
        h2_ref[...] = h2b
        h2t_ref[...] = h2b.T

    tile = lambda w: pl.BlockSpec((tm, w), lambda i: (i, 0))
    return pl.pallas_call(
        body, name="fwd_mid", grid=(s // tm,),
        in_specs=[tile(DC), pl.BlockSpec((HALO_C, DC), lambda i: (jnp.maximum(i * hb - 1, 0), 0)),
                  pl.BlockSpec((tm, 2 * DC), lambda i: (i, 1)), tile(D), _full((16, D)), _full((8, DC)),
                  _full((32, DC)), _full((1, NH, CHUNK, CHUNK)), _full((CHUNK, DC)), _full((D, D))],
        out_specs=[tile(DC), tile(DC), tile(D), tile(D), tile(D), pl.BlockSpec((D, tm), lambda i: (0, i))],
        out_shape=[jax.ShapeDtypeStruct((s, DC), F32), jax.ShapeDtypeStruct((s, DC), F32),
                   jax.ShapeDtypeStruct((s, D), F32), jax.ShapeDtypeStruct((s, D), F32),
                   jax.ShapeDtypeStruct((s, D), BF16), jax.ShapeDtypeStruct((D, s), BF16)],
        compiler_params=_arb(),
    )(a0, a0, z, x2d, vecs, v512, conv_w, gm_ws, bs_exp, w_out_b)


def _ffn_conv(fw_ref, cols, p2, p1, pre):
    return (fw_ref[pl.ds(3, 1), cols] + fw_ref[pl.ds(0, 1), cols] * p2
            + fw_ref[pl.ds(1, 1), cols] * p1 + fw_ref[pl.ds(2, 1), cols] * pre)


def _fwd_ffn(h2, x2, target, vecs, ffn_wb, w_up_g, w_down_p, tm):
    s = x2.shape[0]

    def body(h2_ref, x2_ref, t_ref, v_ref, fw_ref, wu_hbm, wd_hbm,
             up_ref, dx3_ref, acc_ref, wu, wd, carry):
        i = pl.program_id(0)

        @pl.when(i == 0)
        def _():
            pltpu.sync_copy(wu_hbm, wu)
            pltpu.sync_copy(wd_hbm, wd)
            carry[...] = jnp.zeros_like(carry)
            acc_ref[...] = jnp.zeros_like(acc_ref)

        h2v = h2_ref[...]
        o2 = jnp.zeros((tm, D), F32)
        for j in range(4):
            conv = []
            for sh in (j, 4 + j):
                cols = pl.ds(sh * PSH, PSH)
                pre = _dot(h2v, wu[sh])
                up_ref[:, cols] = pre
                e = jnp.concatenate([carry[:, cols], pre], axis=0)
                carry[:, cols] = pre[tm - HALO_F:, :]
                conv.append(_ffn_conv(fw_ref, cols, pltpu.roll(e, 2, 0)[HALO_F:, :],
                                      pltpu.roll(e, 1, 0)[HALO_F:, :], pre))
            val, gate = conv
            f = ((gate * _sigmoid(gate)) * val).astype(BF16)
            o2 = o2 + _dot(f, wd[j])
        x3 = x2_ref[...] + _row(v_ref, GT2) * o2
        xn3, r3 = _rms(x3)
        gf = _row(v_ref, GF)
        diff = xn3 * gf - t_ref[...]
        acc_ref[pl.ds(1, 1), :] += _colsum(diff * diff) * (0.5 / D)
        dout = diff * (1.0 / D)
        acc_ref[pl.ds(0, 1), :] += _colsum(dout * xn3)
        dx3 = _rms_bwd(dout * gf, xn3, r3)
        dx3_ref[...] = dx3
        acc_ref[pl.ds(2, 1), :] += _colsum(dx3 * o2)

    tile = lambda w: pl.BlockSpec((tm, w), lambda i: (i, 0))
    return pl.pallas_call(
        body, name="fwd_ffn", grid=(s // tm,),
        in_specs=[tile(D), tile(D), tile(D), _full((16, D)), _full((8, 2 * PFF)), ANY, ANY],
        out_specs=[tile(2 * PFF), tile(D), _full((8, D))],
        out_shape=[jax.ShapeDtypeStruct((s, 2 * PFF), F32), jax.ShapeDtypeStruct((s, D), F32),
                   jax.ShapeDtypeStruct((8, D), F32)],
        scratch_shapes=[pltpu.VMEM((NDEV, D, PSH), BF16), pltpu.VMEM((4, PSH, D), BF16),
                        pltpu.VMEM((HALO_F, 2 * PFF), F32)],
        compiler_params=_arb(),
    )(h2, x2, target, vecs, ffn_wb, w_up_g, w_down_p)


def _bwd_ffn(dx3, up_pre, h2_t, vecs, ffn_wb, w_up_g, w_down_p, tm):
    s = dx3.shape[0]
    nt = s // tm
    hb = tm // HALO_F

    def body(dx3_ref, up_ref, upg_ref, halo_ref, halog_ref, h2t_ref, v_ref, fw_ref, fwg_ref, wu_ref, wug_ref, wd_ref,
             dh2_ref, dwu_ref, dwd_ref, accf_ref, carry):
        i = pl.program_id(1)
        r = nt - 1 - i

        @pl.when(i == 0)
        def _():
            for ref in (carry, dwu_ref, dwd_ref, accf_ref):
                ref[...] = jnp.zeros_like(ref)

        do2 = (dx3_ref[...] * _row(v_ref, GT2)).astype(BF16)
        df = _dot_nt(do2, wd_ref[...])
        shifted, conv = [], []
        for pre_ref, hl_ref, w_ref in ((up_ref, halo_ref, fw_ref), (upg_ref, halog_ref, fwg_ref)):
            pre = pre_ref[...]
            hl = hl_ref[...]
            e = jnp.concatenate([jnp.where(r > 0, hl, jnp.zeros_like(hl)), pre], axis=0)
            p2 = pltpu.roll(e, 2, 0)[HALO_F:, :]
            p1 = pltpu.roll(e, 1, 0)[HALO_F:, :]
            shifted.append((p2, p1, pre))
            conv.append(_row(w_ref, 3) + _row(w_ref, 0) * p2 + _row(w_ref, 1) * p1 + _row(w_ref, 2) * pre)
        val, gate = conv
        sg = _sigmoid(gate)
        sl = gate * sg
        f_t = (sl * val).astype(BF16).T
        dwd_ref[...] += _dot(f_t, do2)[:NSH, :]
        dups = (df * sl, df * val * (sg * (1.0 + gate * (1.0 - sg))))
        h2t = h2t_ref[...]
        dh2 = jnp.zeros((tm, D), F32)
        for half, (dup, (p2, p1, pre), w_ref, wmat_ref) in enumerate(
                zip(dups, shifted, (fw_ref, fwg_ref), (wu_ref, wug_ref))):
            cols = pl.ds(half * PSH, PSH)
            accf_ref[half, pl.ds(3, 1), :] += _colsum(dup)
            accf_ref[half, pl.ds(0, 1), :] += _colsum(dup * p2)
            accf_ref[half, pl.ds(1, 1), :] += _colsum(dup * p1)
            accf_ref[half, pl.ds(2, 1), :] += _colsum(dup * pre)
            e = jnp.concatenate([dup, carry[:, cols]], axis=0)
            carry[:, cols] = dup[:HALO_F, :]
            dpre = (_row(w_ref, 0) * _shift_up(e, 2)[:tm, :] + _row(w_ref, 1) * _shift_up(e, 1)[:tm, :]
                    + _row(w_ref, 2) * dup).astype(BF16)
            dwu_ref[half] += _dot(h2t, dpre)[:, :NSH]
            dh2 = dh2 + _dot_nt(dpre, wmat_ref[...])
        dh2_ref[...] = dh2

    rev = lambda j, i: nt - 1 - i
    halo = lambda j, i: jnp.maximum((nt - 1 - i) * hb - 1, 0)
    in_specs = [
        pl.BlockSpec((tm, D), lambda j, i: (rev(j, i), 0)),
        pl.BlockSpec((tm, PSH), lambda j, i: (rev(j, i), j)), pl.BlockSpec((tm, PSH), lambda j, i: (rev(j, i), 4 + j)),
        pl.BlockSpec((HALO_F, PSH), lambda j, i: (halo(j, i), j)),
        pl.BlockSpec((HALO_F, PSH), lambda j, i: (halo(j, i), 4 + j)),
        pl.BlockSpec((D, tm), lambda j, i: (0, rev(j, i))), _full((16, D)),
        pl.BlockSpec((8, PSH), lambda j, i: (0, j)), pl.BlockSpec((8, PSH), lambda j, i: (0, 4 + j)),
        pl.BlockSpec((None, D, PSH), lambda j, i: (j, 0, 0)), pl.BlockSpec((None, D, PSH), lambda j, i: (4 + j, 0, 0)),
        pl.BlockSpec((None, PSH, D), lambda j, i: (j, 0, 0))]
    dh2, dw_up, dw_down, accf = pl.pallas_call(
        body, name="bwd_ffn", grid=(4, nt), in_specs=in_specs,
        out_specs=[pl.BlockSpec((None, tm, D), lambda j, i: (j, rev(j, i), 0)),
                   pl.BlockSpec((2, None, D, NSH), lambda j, i: (0, j, 0, 0)),
                   pl.BlockSpec((None, NSH, D), lambda j, i: (j, 0, 0)),
                   pl.BlockSpec((2, None, 8, PSH), lambda j, i: (0, j, 0, 0))],
        out_shape=[jax.ShapeDtypeStruct((4, s, D), F32), jax.ShapeDtypeStruct((2, 4, D, NSH), F32),
                   jax.ShapeDtypeStruct((4, NSH, D), F32), jax.ShapeDtypeStruct((2, 4, 8, PSH), F32)],
        scratch_shapes=[pltpu.VMEM((HALO_F, 2 * PSH), F32)],
        compiler_params=_arb(2),
    )(dx3, up_pre, up_pre, up_pre, up_pre, h2_t, vecs, ffn_wb, ffn_wb, w_up_g, w_up_g, w_down_p)
    return dh2, dw_up.reshape(NDEV, D, NSH), dw_down, accf


def _bwd_mid(dh2, dx3, x2, x2d, o1, z, a0, a1, sp, vecs, v512, conv_w, gm_ws, gm_ws_t, w_out_b, w_in_g, tm):
    s = x2d.shape[0]
    nt = s // tm
    hb = tm // HALO_C
    nc = w_in_g.shape[2]

    def body(dh2a_ref, dh2b_ref, dh2c_ref, dh2d_ref, dx3_ref, x2_ref, x_ref, o1_ref, z_ref, a0_ref, halo_ref, a1_ref, sp_ref, v_ref, p_ref, cw_ref,
             ws_ref, wst_ref, wo_ref, wi_ref, gx_ref, dz_ref, yt_ref, do1_ref, acc_ref, accp_ref, dcw_ref, dws_ref,
             dbst_ref, dbs_s, carry, da1_s, dsp_s, dgvn_s):
        i = pl.program_id(0)
        r = nt - 1 - i

        @pl.when(i == 0)
        def _():
            for ref in (carry, dbs_s, acc_ref, accp_ref, dcw_ref, dws_ref, dbst_ref):
                ref[...] = jnp.zeros_like(ref)

        dh2v = (dh2a_ref[...] + dh2b_ref[...]) + (dh2c_ref[...] + dh2d_ref[...])
        xn2, r2 = _rms(x2_ref[...])
        g2 = _row(v_ref, G2)
        sc2 = 1.0 + _row(v_ref, SC2)
        acc_ref[pl.ds(5, 1), :] += _colsum(dh2v)
        acc_ref[pl.ds(6, 1), :] += _colsum(dh2v * (xn2 * g2))
        acc_ref[pl.ds(7, 1), :] += _colsum(dh2v * sc2 * xn2)
        dx2v = dx3_ref[...] + _rms_bwd(dh2v * sc2 * g2, xn2, r2)
        do1 = (dx2v * _row(v_ref, GT1)).astype(BF16)
        do1_ref[...] = do1
        acc_ref[pl.ds(0, 1), :] += _colsum(dx2v * o1_ref[...])
        dy = _dot_nt(do1, wo_ref[...])
        mog = _row(v_ref, MOG)

        xh, rstd = _ln(a1_ref[...])
        clg = _row(p_ref, CLG)
        a2 = xh * clg + _row(p_ref, CLB)
        s2 = _sigmoid(a2)
        a3 = a2 * s2
        an, ra = _rms(a3)
        dya = dy[:, :DC]
        da3 = _rms_bwd(dya * mog[:, :DC], an, ra)
        da2 = da3 * (s2 * (1.0 + a2 * (1.0 - s2)))
        accp_ref[pl.ds(CLB, 1), :] += _colsum(da2)
        accp_ref[pl.ds(CLG, 1), :] += _colsum(da2 * xh)
        da1 = _ln_bwd(da2 * clg, xh, rstd)
        accp_ref[pl.ds(CB, 1), :] += _colsum(da1)
        da1_s[...] = da1
        for c0 in range(0, DC, LANES):
            cols = pl.ds(c0, LANES)
            d = da1_s[:, cols]
            e = jnp.concatenate([d, carry[:, cols]], axis=0)
            carry[:, cols] = d[:HALO_C, :]
            acc = jnp.zeros((tm, LANES), F32)
            for j in range(KC):
                acc = acc + _shift_up(e, j)[:tm, :] * cw_ref[pl.ds(KC - 1 - j, 1), cols]
            sgc = _sigmoid(z_ref[:, pl.ds(DC + c0, LANES)])
            dz_ref[:, cols] = (acc * sgc).astype(BF16)
            dz_ref[:, pl.ds(DC + c0, LANES)] = (acc * z_ref[:, cols] * sgc * (1.0 - sgc)).astype(BF16)
            halo = halo_ref[:, cols]
            ea = jnp.concatenate([jnp.where(r > 0, halo, jnp.zeros_like(halo)), a0_ref[:, cols]], axis=0)
            for k in range(KC):
                dcw_ref[pl.ds(k, 1), cols] += _colsum(d * _shift_up(ea, HALO_C - (KC - 1) + k)[:tm, :])

        gu_pre = z_ref[:, 2 * DC:3 * DC]
        gv_pre = z_ref[:, 3 * DC:]
        gu, tu = _gelu(gu_pre)
        gvg, tv = _gelu(gv_pre)
        vh, vrstd = _ln(gvg)
        glg = _row(p_ref, GLG)
        gvn = (vh * glg + _row(p_ref, GLB)).astype(BF16)
        spv = sp_ref[...]
        g = gu * spv
        gn, rg = _rms(g)
        yt_ref[...] = jnp.concatenate([an * mog[:, :DC], gn * mog[:, DC:]], axis=1).astype(BF16).T
        acc_ref[pl.ds(4, 1), :] += jnp.concatenate([_colsum(dya * an), _colsum(dy[:, DC:] * gn)], axis=1)
        dg = _rms_bwd(dy[:, DC:] * mog[:, DC:], gn, rg)
        dz_ref[:, pl.ds(2 * DC, DC)] = (dg * spv * _gelu_grad(gu_pre, tu)).astype(BF16)
        dsp_s[...] = dg * gu
        upper = _causal_mask(False)
        first = _first_head_lanes()
        wmt = [jnp.where(upper, wst_ref[h], 0.0).astype(BF16) for h in range(NH)]
        for n in range(tm // CHUNK):
            rows = pl.ds(n * CHUNK, CHUNK)
            for p in range(NH // 2):
                cols = pl.ds(p * CHUNK, CHUNK)
                dsp = dsp_s[rows, cols]
                dbs_s[:, cols] += dsp
                da = jnp.where(first, dsp, 0.0).astype(BF16)
                db = jnp.where(first, 0.0, dsp).astype(BF16)
                v = gvn[n * CHUNK:(n + 1) * CHUNK, p * CHUNK:(p + 1) * CHUNK]
                dws_ref[2 * p] += _dot_nt(da, v)
                dws_ref[2 * p + 1] += _dot_nt(db, v)
                dgvn_s[rows, cols] = _dot(wmt[2 * p], da) + _dot(wmt[2 * p + 1], db)
        dgvn = dgvn_s[...]
        accp_ref[pl.ds(GLB, 1), :] += _colsum(dgvn)
        accp_ref[pl.ds(GLG, 1), :] += _colsum(dgvn * vh)
        dgvg = _ln_bwd(dgvn * glg, vh, vrstd)
        dz_ref[:, pl.ds(3 * DC, DC)] = (dgvg * _gelu_grad(gv_pre, tv)).astype(BF16)

        dh1 = jnp.zeros((tm, D), F32)
        for d in range(NDEV):
            dh1 = dh1 + _dot_nt(dz_ref[:, pl.ds(d * nc, nc)], wi_ref[d])
        xn, r1 = _rms(x_ref[...])
        g1 = _row(v_ref, G1)
        sc = 1.0 + _row(v_ref, SC1)
        acc_ref[pl.ds(1, 1), :] += _colsum(dh1)
        acc_ref[pl.ds(2, 1), :] += _colsum(dh1 * (xn * g1))
        acc_ref[pl.ds(3, 1), :] += _colsum(dh1 * sc * xn)
        gx_ref[...] = dx2v + _rms_bwd(dh1 * sc * g1, xn, r1)

        @pl.when(i == nt - 1)
        def _():
            low = _causal_mask(True)
            for h in range(NH):
                dws_ref[h] = jnp.where(low, dws_ref[h], 0.0)
            lane = lax.broadcasted_iota(jnp.int32, (CHUNK, CHUNK), 1)
            out = jnp.zeros((CHUNK, CHUNK), F32)
            for h in range(NH):
                hs = jnp.sum(dbs_s[:, pl.ds((h // 2) * CHUNK, CHUNK)]
                             * ((lane >= (h % 2) * HD) & (lane < (h % 2 + 1) * HD)).astype(F32),
                             axis=1, keepdims=True)
                out = jnp.where(lane == h, hs, out)
            dbst_ref[...] = out

    tile = lambda w: pl.BlockSpec((tm, w), lambda i: (nt - 1 - i, 0))
    return pl.pallas_call(
        body, name="bwd_mid", grid=(nt,),
        in_specs=[pl.BlockSpec((None, tm, D), functools.partial(lambda k, i: (k, nt - 1 - i, 0), k)) for k in range(4)]
        + [tile(D), tile(D), tile(D), tile(D), tile(4 * DC), tile(DC),
                  pl.BlockSpec((HALO_C, DC), lambda i: (jnp.maximum((nt - 1 - i) * hb - 1, 0), 0)),
                  tile(DC), tile(DC), _full((16, D)), _full((8, DC)), _full((32, DC)),
                  _full((NH, CHUNK, CHUNK)), _full((NH, CHUNK, CHUNK)), _full((D, D)), _full((NDEV, D, nc))],
        out_specs=[tile(D), tile(4 * DC), pl.BlockSpec((D, tm), lambda i: (0, nt - 1 - i)), tile(D),
                   _full((16, D)), _full((8, DC)), _full((32, DC)),
                   _full((NH, CHUNK, CHUNK)), _full((CHUNK, CHUNK))],
        out_shape=[jax.ShapeDtypeStruct((s, D), F32), jax.ShapeDtypeStruct((s, 4 * DC), BF16),
                   jax.ShapeDtypeStruct((D, s), BF16), jax.ShapeDtypeStruct((s, D), BF16),
                   jax.ShapeDtypeStruct((16, D), F32), jax.ShapeDtypeStruct((8, DC), F32),
                   jax.ShapeDtypeStruct((32, DC), F32), jax.ShapeDtypeStruct((NH, CHUNK, CHUNK), F32),
                   jax.ShapeDtypeStruct((CHUNK, CHUNK), F32)],
        scratch_shapes=[pltpu.VMEM((CHUNK, DC), F32), pltpu.VMEM((HALO_C, DC), F32), pltpu.VMEM((tm, DC), F32),
                        pltpu.VMEM((tm, DC), F32), pltpu.VMEM((tm, DC), F32)],
        compiler_params=_arb(),
    )(dh2, dh2, dh2, dh2, dx3, x2, x2d, o1, z, a0, a0, a1, sp, vecs, v512, conv_w, gm_ws, gm_ws_t, w_out_b, w_in_g)


def _mm_all_slots(name, at, b, bw, tk, after):
    k1, s = at.shape
    nslot = b.shape[1] // bw

    def body(a_ref, b_ref, after_ref, o_ref):
        @pl.when(pl.program_id(0) == 0)
        def _():
            o_ref[...] = jnp.zeros_like(o_ref)

        t = _dot(a_ref[...], b_ref[...])
        for j in range(nslot):
            o_ref[j] += t[:, j * bw:(j + 1) * bw]

    return pl.pallas_call(
        body, name=name, grid=(s // tk,),
        in_specs=[pl.BlockSpec((k1, tk), lambda k: (0, k)), pl.BlockSpec((tk, nslot * bw), lambda k: (k, 0)), ANY],
        out_specs=_full((nslot, k1, bw)), out_shape=jax.ShapeDtypeStruct((nslot, k1, bw), F32),
        compiler_params=_arb(),
    )(at, b, after)


def _adam_math(w, g, m, v):
    m = ADAM_B1 * m + (1.0 - ADAM_B1) * g
    v = ADAM_B2 * v + (1.0 - ADAM_B2) * (g * g)
    m_hat = m / (1.0 - ADAM_B1 ** ADAM_STEP)
    v_hat = v / (1.0 - ADAM_B2 ** ADAM_STEP)
    delta = -ADAM_LR * (m_hat / (jnp.sqrt(v_hat) + ADAM_EPS) + ADAM_WD * w)
    return delta, m, v


def _row_block(rows, cols):
    tr = rows
    while tr * cols * 4 > (2 << 20) and tr % 32 == 0:
        tr //= 2
    return tr


def _adam3(name, w, g, m, v):
    _, rows, cols = w.shape
    tr = _row_block(rows, cols)

    def body(w_ref, g_ref, m_ref, v_ref, d_ref, mo_ref, vo_ref):
        d_ref[...], mo_ref[...], vo_ref[...] = _adam_math(w_ref[...], g_ref[...], m_ref[...], v_ref[...])

    spec = pl.BlockSpec((1, tr, cols), lambda i: (0, i, 0))
    return pl.pallas_call(
        body, name=name, grid=(rows // tr,), in_specs=[spec] * 4, out_specs=[spec] * 3,
        out_shape=[jax.ShapeDtypeStruct(w.shape, F32)] * 3, compiler_params=_arb(),
    )(w, g, m, v)


def _sum_adam(name, parts, w, m, v):
    n, rows, cols = parts.shape
    tr = _row_block(rows, cols)

    def body(p_ref, w_ref, m_ref, v_ref, g_ref, d_ref, mo_ref, vo_ref):
        g = p_ref[0].astype(F32)
        for k in range(1, n):
            g = g + p_ref[k].astype(F32)
        g_ref[0] = g
        d_ref[0], mo_ref[0], vo_ref[0] = _adam_math(w_ref[0], g, m_ref[0], v_ref[0])

    spec = pl.BlockSpec((1, tr, cols), lambda i: (0, i, 0))
    return pl.pallas_call(
        body, name=name, grid=(rows // tr,),
        in_specs=[pl.BlockSpec((n, tr, cols), lambda i: (0, i, 0))] + [spec] * 3, out_specs=[spec] * 4,
        out_shape=[jax.ShapeDtypeStruct(w.shape, F32)] * 4, compiler_params=_arb(),
    )(parts, w, m, v)


def _pair_add(name, g4, recv, core):
    _, _, rows, cols = g4.shape
    tr = _row_block(rows, cols)

    def body(c_ref, a_ref, b_ref, o_ref):
        o_ref[...] = (a_ref[...] + b_ref[...]).astype(BF16)

    return pl.pallas_call(
        body, name=name,
        grid_spec=pltpu.PrefetchScalarGridSpec(
            num_scalar_prefetch=1, grid=(4, rows // tr),
            in_specs=[pl.BlockSpec((None, None, tr, cols), lambda k, i, c_ref: (k, c_ref[0], i, 0)),
                      pl.BlockSpec((None, tr, cols), lambda k, i, c_ref: (k, i, 0))],
            out_specs=pl.BlockSpec((None, tr, cols), lambda k, i, c_ref: (k, i, 0))),
        out_shape=jax.ShapeDtypeStruct((4, rows, cols), BF16), compiler_params=_arb(2),
    )(core, g4, recv)


def _sum_small(rows_all, p_all, ws_all, bst_all, fw_all, cw_all):
    def body(a_ref, p_ref, ws_ref, bst_ref, fw_ref, cw_ref,
             g_b_ada, g_n1, g_mog, g_n2, g_gf, loss_cols, g_cb, g_clg, g_clb, g_glg, g_glb, g_ws, g_bs, fw_sum,
             cw_sum):
        def total(ref):
            t = ref[0]
            for k in range(1, NDEV):
                t = t + ref[k]
            return t

        a = total(a_ref)
        g_b_ada[...] = jnp.concatenate([a[k:k + 1, :] for k in range(6)], axis=1)
        g_n1[...] = a[6:7, :]
        g_mog[...] = a[7:8, :]
        g_n2[...] = a[8:9, :]
        g_gf[...] = a[9:10, :].reshape(D)
        loss_cols[...] = a[10:11, :]
        p = total(p_ref)
        for k, ref in zip((CB, CLG, CLB, GLG, GLB), (g_cb, g_clg, g_clb, g_glg, g_glb)):
            ref[...] = p[k:k + 1, :]
        g_ws[0] = total(ws_ref)
        g_bs[0] = jnp.transpose(total(bst_ref))[:NH, :]
        fw_sum[...] = total(fw_ref)
        cw_sum[...] = total(cw_ref)

    vec = lambda n: jax.ShapeDtypeStruct((1, n), F32)
    return pl.pallas_call(
        body, name="sum_small_grads",
        out_shape=[vec(6 * D), vec(D), vec(D), vec(D), jax.ShapeDtypeStruct((D,), F32), vec(D),
                   vec(DC), vec(DC), vec(DC), vec(DC), vec(DC),
                   jax.ShapeDtypeStruct((1, NH, CHUNK, CHUNK), F32), jax.ShapeDtypeStruct((1, NH, CHUNK), F32),
                   jax.ShapeDtypeStruct((8, 2 * PFF), F32), jax.ShapeDtypeStruct((32, DC), F32)],
    )(rows_all, p_all, ws_all, bst_all, fw_all, cw_all)


def _adam_small(quads):
    n = len(quads)

    def body(*refs):
        ins, outs = refs[:4 * n], refs[4 * n:]
        for q in range(n):
            w, g, m, v = (r[...] for r in ins[4 * q:4 * q + 4])
            outs[3 * q][...], outs[3 * q + 1][...], outs[3 * q + 2][...] = _adam_math(w, g, m, v)

    flat = [a for q in quads for a in q]
    outs = pl.pallas_call(
        body, name="adam_small",
        out_shape=[jax.ShapeDtypeStruct(q[0].shape, F32) for q in quads for _ in range(3)],
    )(*flat)
    return [tuple(outs[3 * q:3 * q + 3]) for q in range(n)]


def kernel(x, c, w_ada, b_ada, norm1_gain, w_in, conv_dw_w, conv_dw_b, conv_ln_g, conv_ln_b, gm_ln_g, gm_ln_b, gm_ws, gm_bs, mix_out_gain, w_out, norm2_gain, w_up, ffn_dw_w, ffn_dw_b, w_down, final_gain, loss_target, m_w_ada, m_b_ada, m_norm1_gain, m_w_in, m_conv_dw_w, m_conv_dw_b, m_conv_ln_g, m_conv_ln_b, m_gm_ln_g, m_gm_ln_b, m_gm_ws, m_gm_bs, m_mix_out_gain, m_w_out, m_norm2_gain, m_w_up, m_ffn_dw_w, m_ffn_dw_b, m_w_down, m_final_gain, v_w_ada, v_b_ada, v_norm1_gain, v_w_in, v_conv_dw_w, v_conv_dw_b, v_conv_ln_g, v_conv_ln_b, v_gm_ln_g, v_gm_ln_b, v_gm_ws, v_gm_bs, v_mix_out_gain, v_w_out, v_norm2_gain, v_w_up, v_ffn_dw_w, v_ffn_dw_b, v_w_down, v_final_gain):
    s = x.shape[1]
    ax, ay, ac = _place()
    me = 4 * ax + 2 * ay + ac
    n_ada = w_ada.shape[2]
    n_cw = conv_dw_w.shape[2]
    x2d = x[0]
    target = loss_target[0]
    pad_sh = lambda a: jnp.pad(a, [(0, 0)] * (a.ndim - 1) + [(0, PSH - NSH)])

    gather_in, token_a = _start_gather("gather_in_out", [w_in[0].astype(BF16), w_out[0].astype(BF16)], me)

    c_all, cw_all, fw_all = _all_gather("gather_small", [c + token_a[0, 0], conv_dw_w[0], ffn_dw_w[0]])
    conv_w = jnp.pad(jnp.transpose(cw_all, (1, 0, 2)).reshape(KC, DC), ((0, 32 - KC), (0, 0)))
    ffn_w = jnp.transpose(pad_sh(fw_all), (1, 0, 2)).reshape(KF, 2 * PFF)
    ffn_b = pad_sh(ffn_dw_b.reshape(NDEV, NSH)).reshape(1, 2 * PFF)
    ffn_wb = jnp.concatenate([ffn_w, ffn_b, jnp.zeros((8 - KF - 1, 2 * PFF), F32)], axis=0)

    b_cols = lax.dynamic_slice(b_ada, (0, me * n_ada), (1, n_ada))
    (mod_all,) = _all_gather("gather_mod", [_mod_part(c_all, w_ada, b_cols)])
    shards, mod_all = lax.optimization_barrier(((pad_sh(w_up[0].astype(BF16)), w_down[0].astype(BF16)), mod_all))
    gather_ffn, token_c = _start_gather("gather_up_down", list(shards), me)
    mod = lax.dynamic_index_in_dim(mod_all, me, axis=1, keepdims=False).reshape(6, D)
    sh1, sc1, gt1, sh2, sc2, gt2 = [mod[k:k + 1] for k in range(6)]
    vecs = jnp.concatenate([norm1_gain, sh1, sc1, gt1, norm2_gain, sh2, sc2, gt2, mix_out_gain,
                            final_gain.reshape(1, D), jnp.zeros((6, D), F32)], axis=0)
    vecs = vecs + token_c[0, 0]
    v512 = jnp.concatenate([conv_dw_b, conv_ln_g, conv_ln_b, gm_ln_g, gm_ln_b, jnp.zeros((3, DC), F32)], axis=0)
    bs_exp = jnp.repeat(jnp.transpose(gm_bs[0]), HD, axis=1)
    gm_ws_t = jnp.swapaxes(gm_ws[0], 1, 2)

    tm_in = min(512, s)
    tm = min(256, s)
    w_in_g, w_out_g = _finish_gather("gather_in_out", gather_in, vecs)
    w_out_b = w_out_g.reshape(D, D)
    z, a0, h1_t = _fwd_in(x2d, vecs, w_in_g, tm_in)
    a1, sp, x2, o1, h2, h2_t = _fwd_mid(a0, z, x2d, vecs, v512, conv_w, gm_ws, bs_exp, w_out_b, tm)
    w_up_g, w_down_g = _finish_gather("gather_up_down", gather_ffn, h2)
    w_down_p = jnp.pad(w_down_g.reshape(4, NSH, D), ((0, 0), (0, PSH - NSH), (0, 0)))
    up_pre, dx3, acc_f = _fwd_ffn(h2, x2, target, vecs, ffn_wb, w_up_g, w_down_p, tm)

    core = ac.reshape(1).astype(jnp.int32)
    mychip = 2 * ax + ay

    def to_pairs(named):
        g4s = [g.reshape((4, 2) + g.shape[1:]) for _, g in named]
        from_sibling = _sibling_swap("rs_sibling_" + named[0][0], g4s)
        return [_pair_add("rs_pair_add_" + t[0], g4, rv, core) for t, g4, rv in zip(named, g4s, from_sibling)]

    dh2, dw_up, dw_down, acc_fw = _bwd_ffn(dx3, up_pre, h2_t, vecs, ffn_wb, w_up_g, w_down_p, tm)
    acc_fw = jnp.transpose(acc_fw, (2, 0, 1, 3)).reshape(8, 2 * PFF)
    exchange_ffn, token_x = _start_exchange("rs_chips_ffn", to_pairs(
        [("w_up", dw_up), ("w_down", dw_down.reshape(NDEV, w_down.shape[1], D))]), mychip)
    gx, dz, y_t, do1, acc_m, acc_p, dcw, dws, dbs_t = _bwd_mid(
        dh2, dx3, x2, x2d, o1, z, a0, a1, sp, vecs + token_x[0, 0], v512, conv_w, gm_ws[0], gm_ws_t, w_out_b, w_in_g, tm)
    rows = jnp.concatenate([acc_m[1:3], acc_m[0:1], acc_m[5:7], acc_f[2:3], acc_m[3:5], acc_m[7:8], acc_f[0:2],
                            jnp.zeros((5, D), F32)], axis=0)
    small_gather, token_s = _start_gather("gather_small_grads", [rows, acc_p, dws, dbs_t, acc_fw, dcw], me)
    dw_in = _mm_all_slots("dw_in", h1_t, dz, w_in.shape[2], min(1024, s), token_s)
    small_pass, token_p = _pass_gather("gather_small_grads", small_gather, dw_in)
    dw_out = _mm_all_slots("dw_out", y_t, do1, D, min(2048, s), token_p).reshape(NDEV, w_out.shape[1], D)
    exchange_mix, token_m = _start_exchange("rs_chips_mix", to_pairs([("w_in", dw_in), ("w_out", dw_out)]), mychip)

    rows_all, p_all, ws_all, bst_all, fwg_all, cwg_all = _end_gather("gather_small_grads", small_pass, token_m)
    (g_b_ada, g_n1, g_mog, g_n2, g_gf, loss_cols, g_cb, g_clg, g_clb, g_glg, g_glb, g_ws, g_bs, fw_sum,
     cw_sum) = _sum_small(rows_all, p_all, ws_all, bst_all, fwg_all, cwg_all)
    loss = jnp.sum(loss_cols)
    g_fb = fw_sum[3].reshape(NDEV, PSH)[:, :NSH].reshape(ffn_dw_b.shape)
    g_fw = lax.dynamic_index_in_dim(fw_sum[:KF].reshape(KF, NDEV, PSH), me, axis=1, keepdims=False)[:, :NSH]
    g_fw = g_fw.reshape(ffn_dw_w.shape)
    g_cw = lax.dynamic_slice(cw_sum, (0, me * n_cw), (KC, n_cw)).reshape(conv_dw_w.shape)
    small = [
        (b_ada, g_b_ada, m_b_ada, v_b_ada), (norm1_gain, g_n1, m_norm1_gain, v_norm1_gain),
        (conv_dw_w, g_cw, m_conv_dw_w, v_conv_dw_w), (conv_dw_b, g_cb, m_conv_dw_b, v_conv_dw_b),
        (conv_ln_g, g_clg, m_conv_ln_g, v_conv_ln_g), (conv_ln_b, g_clb, m_conv_ln_b, v_conv_ln_b),
        (gm_ln_g, g_glg, m_gm_ln_g, v_gm_ln_g), (gm_ln_b, g_glb, m_gm_ln_b, v_gm_ln_b),
        (gm_ws, g_ws, m_gm_ws, v_gm_ws), (gm_bs, g_bs, m_gm_bs, v_gm_bs),
        (mix_out_gain, g_mog, m_mix_out_gain, v_mix_out_gain), (norm2_gain, g_n2, m_norm2_gain, v_norm2_gain),
        (ffn_dw_w, g_fw, m_ffn_dw_w, v_ffn_dw_w), (ffn_dw_b, g_fb, m_ffn_dw_b, v_ffn_dw_b),
        (final_gain, g_gf, m_final_gain, v_final_gain)]
    small_out = _adam_small(small)
    res = {}
    for name, q, o in zip(("b_ada", "norm1_gain", "conv_dw_w", "conv_dw_b", "conv_ln_g", "conv_ln_b", "gm_ln_g",
                           "gm_ln_b", "gm_ws", "gm_bs", "mix_out_gain", "norm2_gain", "ffn_dw_w", "ffn_dw_b",
                           "final_gain"), small, small_out):
        res[name] = (q[1],) + o

    dmod_all = rows_all[:, :6].reshape(NDEV, 6 * D)
    dm_cols = lax.dynamic_slice(dmod_all, (0, me * n_ada), (NDEV, n_ada))
    g_ada = _ada_grad(jnp.transpose(c_all[:, 0, :]), dm_cols)
    res["w_ada"] = (g_ada,) + tuple(_adam3("adam_ada", w_ada, g_ada, m_w_ada, v_w_ada))

    big = [("w_up", w_up, m_w_up, v_w_up), ("w_down", w_down, m_w_down, v_w_down),
           ("w_in", w_in, m_w_in, v_w_in), ("w_out", w_out, m_w_out, v_w_out)]
    from_chips = list(_finish_exchange("rs_chips_ffn", exchange_ffn, res["w_ada"][1]))
    for t, parts in zip(big[:2], from_chips):
        res[t[0]] = tuple(_sum_adam("rs_sum_adam_" + t[0], parts, t[1], t[2], t[3]))
    from_chips = list(_finish_exchange("rs_chips_mix", exchange_mix, res["w_down"][1]))
    for t, parts in zip(big[2:], from_chips):
        res[t[0]] = tuple(_sum_adam("rs_sum_adam_" + t[0], parts, t[1], t[2], t[3]))

    order = ("w_ada", "b_ada", "norm1_gain", "w_in", "conv_dw_w", "conv_dw_b", "conv_ln_g", "conv_ln_b", "gm_ln_g",
             "gm_ln_b", "gm_ws", "gm_bs", "mix_out_gain", "w_out", "norm2_gain", "w_up", "ffn_dw_w", "ffn_dw_b",
             "w_down", "final_gain")
    return (loss, gx.reshape(x.shape), *[res[n][0] for n in order], *[res[n][1] for n in order],
            *[res[n][2] for n in order], *[res[n][3] for n in order])
```

```python
import functools

import jax
import jax.numpy as jnp
from jax import lax
from jax.experimental import pallas as pl
from jax.experimental.pallas import tpu as pltpu

F32 = jnp.float32
BF16 = jnp.bfloat16
NDEV = 8
D = 1024
DC = 512
DFF = 2816
NSH = 704
PSH = 768
PFF = 4 * PSH
KC = 31
KF = 3
CHUNK = 128
NH = 8
HD = 64
HALO_C = 32
HALO_F = 8
LANES = 128
RMS_EPS = 1e-6
LN_EPS = 1e-5
ADAM_LR = 0.001
ADAM_B1 = 0.9
ADAM_B2 = 0.999
ADAM_EPS = 1e-08
ADAM_WD = 0.01
ADAM_STEP = 10
GELU_K = 0.7978845608028654
GELU_C = 0.044715

MESH = pl.DeviceIdType.MESH
ANY = pl.BlockSpec(memory_space=pl.ANY)

G1, SH1, SC1, GT1, G2, SH2, SC2, GT2, MOG, GF = range(10)
CB, CLG, CLB, GLG, GLB = range(5)


def _full(shape):
    return pl.BlockSpec(shape, lambda *_: (0,) * len(shape))


def _arb(n=1):
    return pltpu.CompilerParams(dimension_semantics=("arbitrary",) * n)


def _row(ref, r):
    return ref[pl.ds(r, 1), :]


def _colsum(v):
    return jnp.sum(v, axis=0, keepdims=True)


def _rowmean(v):
    return jnp.mean(v, axis=-1, keepdims=True)


def _rms(x):
    r = lax.rsqrt(_rowmean(x * x) + RMS_EPS)
    return x * r, r


def _rms_bwd(dxn, xn, r):
    return r * (dxn - xn * _rowmean(dxn * xn))


def _ln(x):
    mu = _rowmean(x)
    xc = x - mu
    rstd = lax.rsqrt(_rowmean(xc * xc) + LN_EPS)
    return xc * rstd, rstd


def _ln_bwd(dxh, xhat, rstd):
    return rstd * (dxh - _rowmean(dxh) - xhat * _rowmean(dxh * xhat))


def _sigmoid(x):
    return 0.5 * jnp.tanh(0.5 * x) + 0.5


def _gelu(x):
    t = jnp.tanh(GELU_K * (x + GELU_C * x * x * x))
    return 0.5 * x * (1.0 + t), t


def _gelu_grad(x, t):
    return 0.5 * (1.0 + t) + 0.5 * x * (1.0 - t * t) * (GELU_K * (1.0 + 3.0 * GELU_C * x * x))


def _dot(a, b):
    return jnp.dot(a, b, preferred_element_type=F32)


def _dot_nt(a, b):
    return lax.dot_general(a, b, (((1,), (1,)), ((), ())), preferred_element_type=F32)


def _shift_up(e, s):
    n = e.shape[0]
    return pltpu.roll(e, (n - s) % n, 0)


def _place():
    return lax.axis_index("x"), lax.axis_index("y"), lax.axis_index("c")


def _all_gather(name, xs):
    n = len(xs)

    def body(*refs):
        x_refs, out_refs = refs[:n], refs[n:2 * n]
        send_sems, recv_sems, local_sems = refs[2 * n:]
        x, y, c = _place()
        me, sibling = (x, y, c), (x, y, 1 - c)
        chips = [(1 - x, y), (x, 1 - y), (1 - x, 1 - y)]

        def copy(a, k, block, to, own=False):
            px, py, pc = block
            slot = out_refs[a].at[4 * px + 2 * py + pc]
            return pltpu.make_async_remote_copy(
                src_ref=x_refs[a] if own else slot, dst_ref=slot,
                send_sem=send_sems.at[7 * a + k], recv_sem=recv_sems.at[7 * a + k], device_id=to, device_id_type=MESH)

        mine = [pltpu.make_async_copy(x_refs[a], out_refs[a].at[4 * x + 2 * y + c], local_sems.at[a]) for a in range(n)]
        for cp in mine:
            cp.start()
        first = []
        for a in range(n):
            first.append(copy(a, 0, me, sibling, own=True))
            first += [copy(a, 1 + j, me, (*chip, c), own=True) for j, chip in enumerate(chips)]
        for cp in first:
            cp.start()
        passed = []
        for j, chip in enumerate(chips):
            for a in range(n):
                copy(a, 1 + j, (*chip, c), me).wait_recv()
                cp = copy(a, 4 + j, (*chip, c), sibling)
                cp.start()
                passed.append(cp)
        for a in range(n):
            copy(a, 0, sibling, me).wait_recv()
            for j, chip in enumerate(chips):
                copy(a, 4 + j, (*chip, 1 - c), me).wait_recv()
        for cp in first + passed:
            cp.wait_send()
        for cp in mine:
            cp.wait()

    return pl.pallas_call(
        body, name=name, out_shape=[jax.ShapeDtypeStruct((NDEV,) + a.shape, a.dtype) for a in xs],
        in_specs=[ANY] * n, out_specs=[ANY] * n,
        scratch_shapes=[pltpu.SemaphoreType.DMA((7 * n,)), pltpu.SemaphoreType.DMA((7 * n,)),
                        pltpu.SemaphoreType.DMA((n,))],
    )(*xs)


def _sibling_swap(name, g4s):
    n = len(g4s)

    def body(*refs):
        g_refs, out_refs = refs[:n], refs[n:2 * n]
        send_sems, recv_sems = refs[2 * n:]
        x, y, c = _place()
        cps = [pltpu.make_async_remote_copy(
            src_ref=g_refs[a].at[k, 1 - c], dst_ref=out_refs[a].at[k],
            send_sem=send_sems.at[4 * a + k], recv_sem=recv_sems.at[4 * a + k],
            device_id=(x, y, 1 - c), device_id_type=MESH) for a in range(n) for k in range(4)]
        for cp in cps:
            cp.start()
        for cp in cps:
            cp.wait()

    return pl.pallas_call(
        body, name=name, out_shape=[jax.ShapeDtypeStruct((4,) + g.shape[2:], g.dtype) for g in g4s],
        in_specs=[ANY] * n, out_specs=[ANY] * n,
        scratch_shapes=[pltpu.SemaphoreType.DMA((4 * n,)), pltpu.SemaphoreType.DMA((4 * n,))],
    )(*g4s)


HBM = pl.BlockSpec(memory_space=pltpu.HBM)
SEM = pl.BlockSpec(memory_space=pltpu.SEMAPHORE)
EFFECT = pltpu.SideEffectType.DATAFLOW_SIDE_EFFECTING


def _in_hbm(a):
    return pltpu.with_memory_space_constraint(a, pltpu.HBM)


def _split_start(name, bufs, copies):
    n = len(bufs)

    def body(*refs):
        for cp in copies(refs[:n], refs[n], refs[n + 1]):
            cp.start()
        refs[-1][...] = jnp.zeros_like(refs[-1])

    out = pl.pallas_call(
        body, name=name,
        out_shape=(pltpu.SemaphoreType.DMA((copies.count,)), pltpu.SemaphoreType.DMA((copies.count,)),
                   *[pltpu.HBM(a.shape, a.dtype) for a in bufs], jax.ShapeDtypeStruct((8, LANES), F32)),
        in_specs=[HBM] * n, out_specs=(SEM, SEM, *[HBM] * n, pl.BlockSpec(memory_space=pltpu.VMEM)),
        input_output_aliases={i: 2 + i for i in range(n)},
        compiler_params=pltpu.CompilerParams(has_side_effects=EFFECT),
    )(*[_in_hbm(a) for a in bufs])
    return (out[0], out[1], list(out[2:2 + n])), out[-1]


def _split_wait(name, handle, copies, after):
    send_sems, recv_sems, bufs = handle
    n = len(bufs)

    def body(*refs):
        for cp in copies(refs[:n], refs[n], refs[n + 1]):
            cp.wait_send()
            cp.wait_recv()

    out = pl.pallas_call(
        body, name=name, out_shape=tuple(pltpu.HBM(a.shape, a.dtype) for a in bufs),
        in_specs=[HBM] * n + [SEM, SEM, pl.BlockSpec(memory_space=pl.ANY)], out_specs=tuple([HBM] * n),
        input_output_aliases={i: i for i in range(n)},
        compiler_params=pltpu.CompilerParams(has_side_effects=EFFECT),
    )(*bufs, send_sems, recv_sems, after)
    return list(out)


class _GatherFirstCopies:
    def __init__(self, n):
        self.n, self.count = n, 4 * n

    def __call__(self, refs, send_sems, recv_sems):
        x, y, c = _place()
        peers = [(x, y, 1 - c), (1 - x, y, c), (x, 1 - y, c), (1 - x, 1 - y, c)]
        return [pltpu.make_async_remote_copy(
            src_ref=refs[a], dst_ref=refs[self.n + a].at[4 * x + 2 * y + c],
            send_sem=send_sems.at[4 * a + k], recv_sem=recv_sems.at[4 * a + k], device_id=peer, device_id_type=MESH)
            for a in range(self.n) for k, peer in enumerate(peers)]


class _GatherPassCopies:
    def __init__(self, n):
        self.n, self.count = n, 3 * n

    def __call__(self, refs, send_sems, recv_sems):
        x, y, c = _place()
        cps = []
        for a in range(self.n):
            for j, (px, py) in enumerate([(1 - x, y), (x, 1 - y), (1 - x, 1 - y)]):
                slot = refs[a].at[4 * px + 2 * py + c]
                cps.append(pltpu.make_async_remote_copy(
                    src_ref=slot, dst_ref=slot, send_sem=send_sems.at[3 * a + j], recv_sem=recv_sems.at[3 * a + j],
                    device_id=(x, y, 1 - c), device_id_type=MESH))
        return cps


class _ExchangeCopies:
    def __init__(self, n):
        self.n, self.count = n, 3 * n

    def __call__(self, refs, send_sems, recv_sems):
        x, y, c = _place()
        cps = []
        for a in range(self.n):
            for j, (px, py) in enumerate([(1 - x, y), (x, 1 - y), (1 - x, 1 - y)]):
                cps.append(pltpu.make_async_remote_copy(
                    src_ref=refs[a].at[2 * px + py], dst_ref=refs[self.n + a].at[2 * x + y],
                    send_sem=send_sems.at[3 * a + j], recv_sem=recv_sems.at[3 * a + j],
                    device_id=(px, py, c), device_id_type=MESH))
        return cps


def _own_slot(nslot, src, index):
    land = lax.empty((nslot,) + src.shape, src.dtype)
    return lax.dynamic_update_slice(land, src[None], (index,) + (0,) * src.ndim)


def _start_gather(tag, xs, me):
    lands = [_own_slot(NDEV, a, me) for a in xs]
    return _split_start(tag + "_start", list(xs) + lands, _GatherFirstCopies(len(xs)))


def _pass_gather(tag, handle, after):
    n = len(handle[2]) // 2
    lands = _split_wait(tag + "_wait", handle, _GatherFirstCopies(n), after)[n:]
    return _split_start(tag + "_pass", lands, _GatherPassCopies(n))


def _end_gather(tag, passing, after):
    return _split_wait(tag + "_pass_wait", passing, _GatherPassCopies(len(passing[2])), after)


def _finish_gather(tag, handle, after):
    passing, token = _pass_gather(tag, handle, after)
    return _end_gather(tag, passing, token)


def _start_exchange(tag, hs, mychip):
    lands = [_own_slot(4, lax.dynamic_index_in_dim(h, mychip, 0, keepdims=False), mychip) for h in hs]
    return _split_start(tag + "_start", list(hs) + lands, _ExchangeCopies(len(hs)))


def _finish_exchange(tag, handle, after):
    n = len(handle[2]) // 2
    return _split_wait(tag + "_wait", handle, _ExchangeCopies(n), after)[n:]


def _mod_part(c_all, w_ada, b_cols):
    ncol = w_ada.shape[2]

    def body(c_ref, w_ref, b_ref, o_ref):
        cv = c_ref[:, 0, :]
        ca = cv * _sigmoid(cv)
        o_ref[...] = _dot(ca.astype(BF16), w_ref[0].astype(BF16)) + b_ref[...]

    return pl.pallas_call(body, name="mod_part", out_shape=jax.ShapeDtypeStruct((NDEV, ncol), F32))(
        c_all, w_ada, b_cols)


def _ada_grad(c_all_t, dmod_cols):
    ncol = dmod_cols.shape[1]

    def body(ct_ref, dm_ref, o_ref):
        ct = ct_ref[...]
        ca = ct * _sigmoid(ct)
        acc = jnp.zeros((D, ncol), F32)
        for b in range(NDEV):
            acc = acc + ca[:, b:b + 1] * dm_ref[pl.ds(b, 1), :]
        o_ref[0] = acc

    return pl.pallas_call(body, name="ada_grad", out_shape=jax.ShapeDtypeStruct((1, D, ncol), F32))(
        c_all_t, dmod_cols)


def _fwd_in(x2d, vecs, w_in_g, tm):
    s = x2d.shape[0]
    nc = w_in_g.shape[2]

    def body(x_ref, v_ref, w_ref, z_ref, a0_ref, h1t_ref):
        xn, _ = _rms(x_ref[...])
        h = (xn * _row(v_ref, G1)) * (1.0 + _row(v_ref, SC1)) + _row(v_ref, SH1)
        hb = h.astype(BF16)
        h1t_ref[...] = hb.T
        for d in range(NDEV):
            z_ref[:, pl.ds(d * nc, nc)] = _dot(hb, w_ref[d])
        a0_ref[...] = z_ref[:, :DC] * _sigmoid(z_ref[:, DC:2 * DC])

    return pl.pallas_call(
        body, name="fwd_in", grid=(s // tm,),
        in_specs=[pl.BlockSpec((tm, D), lambda i: (i, 0)), _full((16, D)), _full((NDEV, D, nc))],
        out_specs=[pl.BlockSpec((tm, 4 * DC), lambda i: (i, 0)), pl.BlockSpec((tm, DC), lambda i: (i, 0)),
                   pl.BlockSpec((D, tm), lambda i: (0, i))],
        out_shape=[jax.ShapeDtypeStruct((s, 4 * DC), F32), jax.ShapeDtypeStruct((s, DC), F32),
                   jax.ShapeDtypeStruct((D, s), BF16)],
        compiler_params=_arb(),
    )(x2d, vecs, w_in_g)


def _causal_mask(lower):
    r = lax.broadcasted_iota(jnp.int32, (CHUNK, CHUNK), 0)
    c = lax.broadcasted_iota(jnp.int32, (CHUNK, CHUNK), 1)
    return (r >= c) if lower else (r <= c)


def _first_head_lanes():
    return lax.broadcasted_iota(jnp.int32, (CHUNK, CHUNK), 1) < HD


def _fwd_mid(a0, z, x2d, vecs, v512, conv_w, gm_ws, bs_exp, w_out_b, tm):
    s = x2d.shape[0]
    hb = tm // HALO_C

    def body(a0_ref, halo_ref, zg_ref, x_ref, v_ref, p_ref, cw_ref, ws_ref, bs_ref, wo_ref,
             a1_ref, sp_ref, x2_ref, o1_ref, h2_ref):
        i = pl.program_id(0)
        for c0 in range(0, DC, LANES):
            cols = pl.ds(c0, LANES)
            halo = halo_ref[:, cols]
            e = jnp.concatenate([jnp.where(i > 0, halo, jnp.zeros_like(halo)), a0_ref[:, cols]], axis=0)
            acc = jnp.broadcast_to(p_ref[pl.ds(CB, 1), cols], (tm, LANES))
            for k in range(KC):
                acc = acc + _shift_up(e, HALO_C - (KC - 1) + k)[:tm, :] * cw_ref[pl.ds(k, 1), cols]
            a1_ref[:, cols] = acc
        xh, _ = _ln(a1_ref[...])
        a2 = xh * _row(p_ref, CLG) + _row(p_ref, CLB)
        a3 = a2 * _sigmoid(a2)
        gu, _ = _gelu(zg_ref[:, :DC])
        gvg, _ = _gelu(zg_ref[:, DC:])
        vh, _ = _ln(gvg)
        gvn = (vh * _row(p_ref, GLG) + _row(p_ref, GLB)).astype(BF16)
        low = _causal_mask(True)
        first = _first_head_lanes()
        wm = [jnp.where(low, ws_ref[0, h], 0.0).astype(BF16) for h in range(NH)]
        for n in range(tm // CHUNK):
            for p in range(NH // 2):
                v = gvn[n * CHUNK:(n + 1) * CHUNK, p * CHUNK:(p + 1) * CHUNK]
                blk = jnp.where(first, _dot(wm[2 * p], v), _dot(wm[2 * p + 1], v))
                sp_ref[pl.ds(n * CHUNK, CHUNK), pl.ds(p * CHUNK, CHUNK)] = blk + bs_ref[:, pl.ds(p * CHUNK, CHUNK)]
        g = gu * sp_ref[...]
        an, _ = _rms(a3)
        gn, _ = _rms(g)
        mog = _row(v_ref, MOG)
        y = jnp.concatenate([an * mog[:, :DC], gn * mog[:, DC:]], axis=1).astype(BF16)
        o1 = _dot(y, wo_ref[...])
        o1_ref[...] = o1
        x2 = x_ref[...] + _row(v_ref, GT1) * o1
        x2_ref[...] = x2
        xn2, _ = _rms(x2)
        h2 = (xn2 * _row(v_ref, G2)) * (1.0 + _row(v_ref, SC2)) + _row(v_ref, SH2)
        h2_ref[...] = h2.astype(BF16)

    tile = lambda w: pl.BlockSpec((tm, w), lambda i: (i, 0))
    return pl.pallas_call(
        body, name="fwd_mid", grid=(s // tm,),
        in_specs=[tile(DC), pl.BlockSpec((HALO_C, DC), lambda i: (jnp.maximum(i * hb - 1, 0), 0)),
                  pl.BlockSpec((tm, 2 * DC), lambda i: (i, 1)), tile(D), _full((16, D)), _full((8, DC)),
                  _full((32, DC)), _full((1, NH, CHUNK, CHUNK)), _full((CHUNK, DC)), _full((D, D))],
        out_specs=[tile(DC), tile(DC), tile(D), tile(D), tile(D)],
        out_shape=[jax.ShapeDtypeStruct((s, DC), F32), jax.ShapeDtypeStruct((s, DC), F32),
                   jax.ShapeDtypeStruct((s, D), F32), jax.ShapeDtypeStruct((s, D), F32),
                   jax.ShapeDtypeStruct((s, D), BF16)],
        compiler_params=_arb(),
    )(a0, a0, z, x2d, vecs, v512, conv_w, gm_ws, bs_exp, w_out_b)


def _ffn_conv(fw_ref, cols, p2, p1, pre):
    return (fw_ref[pl.ds(3, 1), cols] + fw_ref[pl.ds(0, 1), cols] * p2
            + fw_ref[pl.ds(1, 1), cols] * p1 + fw_ref[pl.ds(2, 1), cols] * pre)


def _fwd_ffn(h2, x2, target, vecs, ffn_wb, w_up_t, w_down_p, tm):
    s = x2.shape[0]

    def body(h2_ref, x2_ref, t_ref, v_ref, fw_ref, wu_hbm, wd_hbm,
             up_ref, vg_ref, dx3_ref, acc_ref, wu, wd, carry):
        i = pl.program_id(0)

        @pl.when(i == 0)
        def _():
            pltpu.sync_copy(wu_hbm, wu)
            pltpu.sync_copy(wd_hbm, wd)
            carry[...] = jnp.zeros_like(carry)
            acc_ref[...] = jnp.zeros_like(acc_ref)

        h2v = h2_ref[...]
        o2 = jnp.zeros((tm, D), F32)
        for j in range(4):
            conv = []
            for sh in (j, 4 + j):
                cols = pl.ds(sh * PSH, PSH)
                pre = _dot_nt(h2v, wu[sh])
                up_ref[:, cols] = pre.astype(BF16)
                e = jnp.concatenate([carry[:, cols], pre], axis=0)
                carry[:, cols] = pre[tm - HALO_F:, :]
                conv.append(_ffn_conv(fw_ref, cols, pltpu.roll(e, 2, 0)[HALO_F:, :],
                                      pltpu.roll(e, 1, 0)[HALO_F:, :], pre))
            val, gate = conv
            vg_ref[:, pl.ds(j * PSH, PSH)] = val.astype(BF16)
            vg_ref[:, pl.ds((4 + j) * PSH, PSH)] = gate.astype(BF16)
            f = ((gate * _sigmoid(gate)) * val).astype(BF16)
            o2 = o2 + _dot(f, wd[j])
        x3 = x2_ref[...] + _row(v_ref, GT2) * o2
        xn3, r3 = _rms(x3)
        gf = _row(v_ref, GF)
        diff = xn3 * gf - t_ref[...]
        acc_ref[pl.ds(1, 1), :] += _colsum(diff * diff) * (0.5 / D)
        dout = diff * (1.0 / D)
        acc_ref[pl.ds(0, 1), :] += _colsum(dout * xn3)
        dx3 = _rms_bwd(dout * gf, xn3, r3)
        dx3_ref[...] = dx3
        acc_ref[pl.ds(2, 1), :] += _colsum(dx3 * o2)

    tile = lambda w: pl.BlockSpec((tm, w), lambda i: (i, 0))
    return pl.pallas_call(
        body, name="fwd_ffn", grid=(s // tm,),
        in_specs=[tile(D), tile(D), tile(D), _full((16, D)), _full((8, 2 * PFF)), ANY, ANY],
        out_specs=[tile(2 * PFF), tile(2 * PFF), tile(D), _full((8, D))],
        out_shape=[jax.ShapeDtypeStruct((s, 2 * PFF), BF16), jax.ShapeDtypeStruct((s, 2 * PFF), BF16),
                   jax.ShapeDtypeStruct((s, D), F32), jax.ShapeDtypeStruct((8, D), F32)],
        scratch_shapes=[pltpu.VMEM((NDEV, PSH, D), BF16), pltpu.VMEM((4, PSH, D), BF16),
                        pltpu.VMEM((HALO_F, 2 * PFF), F32)],
        compiler_params=_arb(),
    )(h2, x2, target, vecs, ffn_wb, w_up_t, w_down_p)


def _bwd_ffn(dx3, up_pre, vg, h2, vecs, ffn_wb, w_up_t, w_down_p, tm):
    s = dx3.shape[0]
    nt = s // tm

    def body(dx3_ref, up_ref, upg_ref, val_ref, gate_ref, h2_ref, v_ref, fw_ref, fwg_ref, wu_ref, wug_ref, wd_ref,
             dh2_ref, dwu_ref, dwd_ref, accf_ref, carry):
        i = pl.program_id(1)

        @pl.when(i == 0)
        def _():
            for ref in (carry, dwu_ref, dwd_ref, accf_ref):
                ref[...] = jnp.zeros_like(ref)

        do2 = (dx3_ref[...] * _row(v_ref, GT2)).astype(BF16)
        df = _dot_nt(do2, wd_ref[...])
        val = val_ref[...].astype(F32)
        gate = gate_ref[...].astype(F32)
        sg = _sigmoid(gate)
        sl = gate * sg
        f_t = (sl * val).astype(BF16).T
        dwd_ref[...] += _dot(f_t, do2)[:NSH, :]
        dups = (df * sl, df * val * (sg * (1.0 + gate * (1.0 - sg))))
        h2v = h2_ref[...]
        dh2 = jnp.zeros((tm, D), F32)
        for half, (dup, pre_ref, w_ref, wmat_ref) in enumerate(
                zip(dups, (up_ref, upg_ref), (fw_ref, fwg_ref), (wu_ref, wug_ref))):
            cols = pl.ds(half * PSH, PSH)
            e = jnp.concatenate([dup, carry[:, cols]], axis=0)
            carry[:, cols] = dup[:HALO_F, :]
            d1 = _shift_up(e, 1)[:tm, :]
            d2 = _shift_up(e, 2)[:tm, :]
            pre = pre_ref[...].astype(F32)
            accf_ref[half, pl.ds(3, 1), :] += _colsum(dup)
            accf_ref[half, pl.ds(0, 1), :] += _colsum(d2 * pre)
            accf_ref[half, pl.ds(1, 1), :] += _colsum(d1 * pre)
            accf_ref[half, pl.ds(2, 1), :] += _colsum(dup * pre)
            dpre = (_row(w_ref, 0) * d2 + _row(w_ref, 1) * d1 + _row(w_ref, 2) * dup).astype(BF16)
            dwu_ref[half] += _dot(dpre.T, h2v)[:NSH, :]
            dh2 = dh2 + _dot(dpre, wmat_ref[...])
        dh2_ref[...] = dh2

    rev = lambda j, i: nt - 1 - i
    in_specs = [
        pl.BlockSpec((tm, D), lambda j, i: (rev(j, i), 0)),
        pl.BlockSpec((tm, PSH), lambda j, i: (rev(j, i), j)), pl.BlockSpec((tm, PSH), lambda j, i: (rev(j, i), 4 + j)),
        pl.BlockSpec((tm, PSH), lambda j, i: (rev(j, i), j)), pl.BlockSpec((tm, PSH), lambda j, i: (rev(j, i), 4 + j)),
        pl.BlockSpec((tm, D), lambda j, i: (rev(j, i), 0)), _full((16, D)),
        pl.BlockSpec((8, PSH), lambda j, i: (0, j)), pl.BlockSpec((8, PSH), lambda j, i: (0, 4 + j)),
        pl.BlockSpec((None, PSH, D), lambda j, i: (j, 0, 0)), pl.BlockSpec((None, PSH, D), lambda j, i: (4 + j, 0, 0)),
        pl.BlockSpec((None, PSH, D), lambda j, i: (j, 0, 0))]
    dh2, dw_up, dw_down, accf = pl.pallas_call(
        body, name="bwd_ffn", grid=(4, nt), in_specs=in_specs,
        out_specs=[pl.BlockSpec((None, tm, D), lambda j, i: (j, rev(j, i), 0)),
                   pl.BlockSpec((2, None, NSH, D), lambda j, i: (0, j, 0, 0)),
                   pl.BlockSpec((None, NSH, D), lambda j, i: (j, 0, 0)),
                   pl.BlockSpec((2, None, 8, PSH), lambda j, i: (0, j, 0, 0))],
        out_shape=[jax.ShapeDtypeStruct((4, s, D), F32), jax.ShapeDtypeStruct((2, 4, NSH, D), F32),
                   jax.ShapeDtypeStruct((4, NSH, D), F32), jax.ShapeDtypeStruct((2, 4, 8, PSH), F32)],
        scratch_shapes=[pltpu.VMEM((HALO_F, 2 * PSH), F32)],
        compiler_params=_arb(2),
    )(dx3, up_pre, up_pre, vg, vg, h2, vecs, ffn_wb, ffn_wb, w_up_t, w_up_t, w_down_p)
    return dh2, dw_up.reshape(NDEV, NSH, D), dw_down, accf


def _bwd_mid(dh2, dx3, x2, x2d, o1, z, a0, a1, sp, vecs, v512, conv_w, gm_ws, gm_ws_t, w_out_b, w_in_g, tm):
    s = x2d.shape[0]
    nt = s // tm
    hb = tm // HALO_C
    nc = w_in_g.shape[2]

    def body(dh2a_ref, dh2b_ref, dh2c_ref, dh2d_ref, dx3_ref, x2_ref, x_ref, o1_ref, z_ref, a0_ref, halo_ref, a1_ref, sp_ref, v_ref, p_ref, cw_ref,
             ws_ref, wst_ref, wo_ref, wi_ref, gx_ref, dz_ref, yt_ref, do1_ref, acc_ref, accp_ref, dcw_ref, dws_ref,
             dbst_ref, dbs_s, carry, da1_s, dsp_s, dgvn_s):
        i = pl.program_id(0)
        r = nt - 1 - i

        @pl.when(i == 0)
        def _():
            for ref in (carry, dbs_s, acc_ref, accp_ref, dcw_ref, dws_ref, dbst_ref):
                ref[...] = jnp.zeros_like(ref)

        dh2v = (dh2a_ref[...] + dh2b_ref[...]) + (dh2c_ref[...] + dh2d_ref[...])
        xn2, r2 = _rms(x2_ref[...])
        g2 = _row(v_ref, G2)
        sc2 = 1.0 + _row(v_ref, SC2)
        acc_ref[pl.ds(5, 1), :] += _colsum(dh2v)
        acc_ref[pl.ds(6, 1), :] += _colsum(dh2v * (xn2 * g2))
        acc_ref[pl.ds(7, 1), :] += _colsum(dh2v * sc2 * xn2)
        dx2v = dx3_ref[...] + _rms_bwd(dh2v * sc2 * g2, xn2, r2)
        do1 = (dx2v * _row(v_ref, GT1)).astype(BF16)
        do1_ref[...] = do1
        acc_ref[pl.ds(0, 1), :] += _colsum(dx2v * o1_ref[...])
        dy = _dot_nt(do1, wo_ref[...])
        mog = _row(v_ref, MOG)

        xh, rstd = _ln(a1_ref[...])
        clg = _row(p_ref, CLG)
        a2 = xh * clg + _row(p_ref, CLB)
        s2 = _sigmoid(a2)
        a3 = a2 * s2
        an, ra = _rms(a3)
        dya = dy[:, :DC]
        da3 = _rms_bwd(dya * mog[:, :DC], an, ra)
        da2 = da3 * (s2 * (1.0 + a2 * (1.0 - s2)))
        accp_ref[pl.ds(CLB, 1), :] += _colsum(da2)
        accp_ref[pl.ds(CLG, 1), :] += _colsum(da2 * xh)
        da1 = _ln_bwd(da2 * clg, xh, rstd)
        accp_ref[pl.ds(CB, 1), :] += _colsum(da1)
        da1_s[...] = da1
        for c0 in range(0, DC, LANES):
            cols = pl.ds(c0, LANES)
            d = da1_s[:, cols]
            e = jnp.concatenate([d, carry[:, cols]], axis=0)
            carry[:, cols] = d[:HALO_C, :]
            acc = jnp.zeros((tm, LANES), F32)
            for j in range(KC):
                acc = acc + _shift_up(e, j)[:tm, :] * cw_ref[pl.ds(KC - 1 - j, 1), cols]
            sgc = _sigmoid(z_ref[:, pl.ds(DC + c0, LANES)])
            dz_ref[:, cols] = (acc * sgc).astype(BF16)
            dz_ref[:, pl.ds(DC + c0, LANES)] = (acc * z_ref[:, cols] * sgc * (1.0 - sgc)).astype(BF16)
            halo = halo_ref[:, cols]
            ea = jnp.concatenate([jnp.where(r > 0, halo, jnp.zeros_like(halo)), a0_ref[:, cols]], axis=0)
            for k in range(KC):
                dcw_ref[pl.ds(k, 1), cols] += _colsum(d * _shift_up(ea, HALO_C - (KC - 1) + k)[:tm, :])

        gu_pre = z_ref[:, 2 * DC:3 * DC]
        gv_pre = z_ref[:, 3 * DC:]
        gu, tu = _gelu(gu_pre)
        gvg, tv = _gelu(gv_pre)
        vh, vrstd = _ln(gvg)
        glg = _row(p_ref, GLG)
        gvn = (vh * glg + _row(p_ref, GLB)).astype(BF16)
        spv = sp_ref[...]
        g = gu * spv
        gn, rg = _rms(g)
        yt_ref[...] = jnp.concatenate([an * mog[:, :DC], gn * mog[:, DC:]], axis=1).astype(BF16).T
        acc_ref[pl.ds(4, 1), :] += jnp.concatenate([_colsum(dya * an), _colsum(dy[:, DC:] * gn)], axis=1)
        dg = _rms_bwd(dy[:, DC:] * mog[:, DC:], gn, rg)
        dz_ref[:, pl.ds(2 * DC, DC)] = (dg * spv * _gelu_grad(gu_pre, tu)).astype(BF16)
        dsp_s[...] = dg * gu
        upper = _causal_mask(False)
        first = _first_head_lanes()
        wmt = [jnp.where(upper, wst_ref[h], 0.0).astype(BF16) for h in range(NH)]
        for n in range(tm // CHUNK):
            rows = pl.ds(n * CHUNK, CHUNK)
            for p in range(NH // 2):
                cols = pl.ds(p * CHUNK, CHUNK)
                dsp = dsp_s[rows, cols]
                dbs_s[:, cols] += dsp
                da = jnp.where(first, dsp, 0.0).astype(BF16)
                db = jnp.where(first, 0.0, dsp).astype(BF16)
                v = gvn[n * CHUNK:(n + 1) * CHUNK, p * CHUNK:(p + 1) * CHUNK]
                dws_ref[2 * p] += _dot_nt(da, v)
                dws_ref[2 * p + 1] += _dot_nt(db, v)
                dgvn_s[rows, cols] = _dot(wmt[2 * p], da) + _dot(wmt[2 * p + 1], db)
        dgvn = dgvn_s[...]
        accp_ref[pl.ds(GLB, 1), :] += _colsum(dgvn)
        accp_ref[pl.ds(GLG, 1), :] += _colsum(dgvn * vh)
        dgvg = _ln_bwd(dgvn * glg, vh, vrstd)
        dz_ref[:, pl.ds(3 * DC, DC)] = (dgvg * _gelu_grad(gv_pre, tv)).astype(BF16)

        dh1 = jnp.zeros((tm, D), F32)
        for d in range(NDEV):
            dh1 = dh1 + _dot_nt(dz_ref[:, pl.ds(d * nc, nc)], wi_ref[d])
        xn, r1 = _rms(x_ref[...])
        g1 = _row(v_ref, G1)
        sc = 1.0 + _row(v_ref, SC1)
        acc_ref[pl.ds(1, 1), :] += _colsum(dh1)
        acc_ref[pl.ds(2, 1), :] += _colsum(dh1 * (xn * g1))
        acc_ref[pl.ds(3, 1), :] += _colsum(dh1 * sc * xn)
        gx_ref[...] = dx2v + _rms_bwd(dh1 * sc * g1, xn, r1)

        @pl.when(i == nt - 1)
        def _():
            low = _causal_mask(True)
            for h in range(NH):
                dws_ref[h] = jnp.where(low, dws_ref[h], 0.0)
            lane = lax.broadcasted_iota(jnp.int32, (CHUNK, CHUNK), 1)
            out = jnp.zeros((CHUNK, CHUNK), F32)
            for h in range(NH):
                hs = jnp.sum(dbs_s[:, pl.ds((h // 2) * CHUNK, CHUNK)]
                             * ((lane >= (h % 2) * HD) & (lane < (h % 2 + 1) * HD)).astype(F32),
                             axis=1, keepdims=True)
                out = jnp.where(lane == h, hs, out)
            dbst_ref[...] = out

    tile = lambda w: pl.BlockSpec((tm, w), lambda i: (nt - 1 - i, 0))
    return pl.pallas_call(
        body, name="bwd_mid", grid=(nt,),
        in_specs=[pl.BlockSpec((None, tm, D), functools.partial(lambda k, i: (k, nt - 1 - i, 0), k)) for k in range(4)]
        + [tile(D), tile(D), tile(D), tile(D), tile(4 * DC), tile(DC),
                  pl.BlockSpec((HALO_C, DC), lambda i: (jnp.maximum((nt - 1 - i) * hb - 1, 0), 0)),
                  tile(DC), tile(DC), _full((16, D)), _full((8, DC)), _full((32, DC)),
                  _full((NH, CHUNK, CHUNK)), _full((NH, CHUNK, CHUNK)), _full((D, D)), _full((NDEV, D, nc))],
        out_specs=[tile(D), tile(4 * DC), pl.BlockSpec((D, tm), lambda i: (0, nt - 1 - i)), tile(D),
                   _full((16, D)), _full((8, DC)), _full((32, DC)),
                   _full((NH, CHUNK, CHUNK)), _full((CHUNK, CHUNK))],
        out_shape=[jax.ShapeDtypeStruct((s, D), F32), jax.ShapeDtypeStruct((s, 4 * DC), BF16),
                   jax.ShapeDtypeStruct((D, s), BF16), jax.ShapeDtypeStruct((s, D), BF16),
                   jax.ShapeDtypeStruct((16, D), F32), jax.ShapeDtypeStruct((8, DC), F32),
                   jax.ShapeDtypeStruct((32, DC), F32), jax.ShapeDtypeStruct((NH, CHUNK, CHUNK), F32),
                   jax.ShapeDtypeStruct((CHUNK, CHUNK), F32)],
        scratch_shapes=[pltpu.VMEM((CHUNK, DC), F32), pltpu.VMEM((HALO_C, DC), F32), pltpu.VMEM((tm, DC), F32),
                        pltpu.VMEM((tm, DC), F32), pltpu.VMEM((tm, DC), F32)],
        compiler_params=_arb(),
    )(dh2, dh2, dh2, dh2, dx3, x2, x2d, o1, z, a0, a0, a1, sp, vecs, v512, conv_w, gm_ws, gm_ws_t, w_out_b, w_in_g)


def _mm_all_slots(name, at, b, bw, tk, after):
    k1, s = at.shape
    nslot = b.shape[1] // bw

    def body(a_ref, b_ref, after_ref, o_ref):
        @pl.when(pl.program_id(0) == 0)
        def _():
            o_ref[...] = jnp.zeros_like(o_ref)

        t = _dot(a_ref[...], b_ref[...])
        for j in range(nslot):
            o_ref[j] += t[:, j * bw:(j + 1) * bw]

    return pl.pallas_call(
        body, name=name, grid=(s // tk,),
        in_specs=[pl.BlockSpec((k1, tk), lambda k: (0, k)), pl.BlockSpec((tk, nslot * bw), lambda k: (k, 0)), ANY],
        out_specs=_full((nslot, k1, bw)), out_shape=jax.ShapeDtypeStruct((nslot, k1, bw), F32),
        compiler_params=_arb(),
    )(at, b, after)


def _adam_math(w, g, m, v):
    m = ADAM_B1 * m + (1.0 - ADAM_B1) * g
    v = ADAM_B2 * v + (1.0 - ADAM_B2) * (g * g)
    m_hat = m / (1.0 - ADAM_B1 ** ADAM_STEP)
    v_hat = v / (1.0 - ADAM_B2 ** ADAM_STEP)
    delta = -ADAM_LR * (m_hat / (jnp.sqrt(v_hat) + ADAM_EPS) + ADAM_WD * w)
    return delta, m, v


def _row_block(rows, cols):
    tr = rows
    while tr * cols * 4 > (2 << 20) and tr % 32 == 0:
        tr //= 2
    return tr


def _adam3(name, w, g, m, v):
    _, rows, cols = w.shape
    tr = _row_block(rows, cols)

    def body(w_ref, g_ref, m_ref, v_ref, d_ref, mo_ref, vo_ref):
        d_ref[...], mo_ref[...], vo_ref[...] = _adam_math(w_ref[...], g_ref[...], m_ref[...], v_ref[...])

    spec = pl.BlockSpec((1, tr, cols), lambda i: (0, i, 0))
    return pl.pallas_call(
        body, name=name, grid=(rows // tr,), in_specs=[spec] * 4, out_specs=[spec] * 3,
        out_shape=[jax.ShapeDtypeStruct(w.shape, F32)] * 3, compiler_params=_arb(),
    )(w, g, m, v)


def _sum_adam(name, parts, w, m, v):
    n, rows, cols = parts.shape
    tr = _row_block(rows, cols)

    def body(p_ref, w_ref, m_ref, v_ref, g_ref, d_ref, mo_ref, vo_ref):
        g = p_ref[0].astype(F32)
        for k in range(1, n):
            g = g + p_ref[k].astype(F32)
        g_ref[0] = g
        d_ref[0], mo_ref[0], vo_ref[0] = _adam_math(w_ref[0], g, m_ref[0], v_ref[0])

    spec = pl.BlockSpec((1, tr, cols), lambda i: (0, i, 0))
    return pl.pallas_call(
        body, name=name, grid=(rows // tr,),
        in_specs=[pl.BlockSpec((n, tr, cols), lambda i: (0, i, 0))] + [spec] * 3, out_specs=[spec] * 4,
        out_shape=[jax.ShapeDtypeStruct(w.shape, F32)] * 4, compiler_params=_arb(),
    )(parts, w, m, v)


def _pair_add(name, g4, recv, core):
    _, _, rows, cols = g4.shape
    tr = _row_block(rows, cols)

    def body(c_ref, a_ref, b_ref, o_ref):
        o_ref[...] = (a_ref[...] + b_ref[...]).astype(BF16)

    return pl.pallas_call(
        body, name=name,
        grid_spec=pltpu.PrefetchScalarGridSpec(
            num_scalar_prefetch=1, grid=(4, rows // tr),
            in_specs=[pl.BlockSpec((None, None, tr, cols), lambda k, i, c_ref: (k, c_ref[0], i, 0)),
                      pl.BlockSpec((None, tr, cols), lambda k, i, c_ref: (k, i, 0))],
            out_specs=pl.BlockSpec((None, tr, cols), lambda k, i, c_ref: (k, i, 0))),
        out_shape=jax.ShapeDtypeStruct((4, rows, cols), BF16), compiler_params=_arb(2),
    )(core, g4, recv)


def _sum_small(rows_all, p_all, ws_all, bst_all, fw_all, cw_all):
    def body(a_ref, p_ref, ws_ref, bst_ref, fw_ref, cw_ref,
             g_b_ada, g_n1, g_mog, g_n2, g_gf, loss_cols, g_cb, g_clg, g_clb, g_glg, g_glb, g_ws, g_bs, fw_sum,
             cw_sum):
        def total(ref):
            t = ref[0]
            for k in range(1, NDEV):
                t = t + ref[k]
            return t

        a = total(a_ref)
        g_b_ada[...] = jnp.concatenate([a[k:k + 1, :] for k in range(6)], axis=1)
        g_n1[...] = a[6:7, :]
        g_mog[...] = a[7:8, :]
        g_n2[...] = a[8:9, :]
        g_gf[...] = a[9:10, :].reshape(D)
        loss_cols[...] = a[10:11, :]
        p = total(p_ref)
        for k, ref in zip((CB, CLG, CLB, GLG, GLB), (g_cb, g_clg, g_clb, g_glg, g_glb)):
            ref[...] = p[k:k + 1, :]
        g_ws[0] = total(ws_ref)
        g_bs[0] = jnp.transpose(total(bst_ref))[:NH, :]
        fw_sum[...] = total(fw_ref)
        cw_sum[...] = total(cw_ref)

    vec = lambda n: jax.ShapeDtypeStruct((1, n), F32)
    return pl.pallas_call(
        body, name="sum_small_grads",
        out_shape=[vec(6 * D), vec(D), vec(D), vec(D), jax.ShapeDtypeStruct((D,), F32), vec(D),
                   vec(DC), vec(DC), vec(DC), vec(DC), vec(DC),
                   jax.ShapeDtypeStruct((1, NH, CHUNK, CHUNK), F32), jax.ShapeDtypeStruct((1, NH, CHUNK), F32),
                   jax.ShapeDtypeStruct((8, 2 * PFF), F32), jax.ShapeDtypeStruct((32, DC), F32)],
    )(rows_all, p_all, ws_all, bst_all, fw_all, cw_all)


def _adam_small(quads):
    n = len(quads)

    def body(*refs):
        ins, outs = refs[:4 * n], refs[4 * n:]
        for q in range(n):
            w, g, m, v = (r[...] for r in ins[4 * q:4 * q + 4])
            outs[3 * q][...], outs[3 * q + 1][...], outs[3 * q + 2][...] = _adam_math(w, g, m, v)

    flat = [a for q in quads for a in q]
    outs = pl.pallas_call(
        body, name="adam_small",
        out_shape=[jax.ShapeDtypeStruct(q[0].shape, F32) for q in quads for _ in range(3)],
    )(*flat)
    return [tuple(outs[3 * q:3 * q + 3]) for q in range(n)]


def kernel(x, c, w_ada, b_ada, norm1_gain, w_in, conv_dw_w, conv_dw_b, conv_ln_g, conv_ln_b, gm_ln_g, gm_ln_b, gm_ws, gm_bs, mix_out_gain, w_out, norm2_gain, w_up, ffn_dw_w, ffn_dw_b, w_down, final_gain, loss_target, m_w_ada, m_b_ada, m_norm1_gain, m_w_in, m_conv_dw_w, m_conv_dw_b, m_conv_ln_g, m_conv_ln_b, m_gm_ln_g, m_gm_ln_b, m_gm_ws, m_gm_bs, m_mix_out_gain, m_w_out, m_norm2_gain, m_w_up, m_ffn_dw_w, m_ffn_dw_b, m_w_down, m_final_gain, v_w_ada, v_b_ada, v_norm1_gain, v_w_in, v_conv_dw_w, v_conv_dw_b, v_conv_ln_g, v_conv_ln_b, v_gm_ln_g, v_gm_ln_b, v_gm_ws, v_gm_bs, v_mix_out_gain, v_w_out, v_norm2_gain, v_w_up, v_ffn_dw_w, v_ffn_dw_b, v_w_down, v_final_gain):
    s = x.shape[1]
    ax, ay, ac = _place()
    me = 4 * ax + 2 * ay + ac
    n_ada = w_ada.shape[2]
    n_cw = conv_dw_w.shape[2]
    x2d = x[0]
    target = loss_target[0]
    pad_sh = lambda a: jnp.pad(a, [(0, 0)] * (a.ndim - 1) + [(0, PSH - NSH)])

    gather_in, token_a = _start_gather("gather_in_out", [w_in[0].astype(BF16), w_out[0].astype(BF16)], me)

    c_all, cw_all, fw_all = _all_gather("gather_small", [c + token_a[0, 0], conv_dw_w[0], ffn_dw_w[0]])
    conv_w = jnp.pad(jnp.transpose(cw_all, (1, 0, 2)).reshape(KC, DC), ((0, 32 - KC), (0, 0)))
    ffn_w = jnp.transpose(pad_sh(fw_all), (1, 0, 2)).reshape(KF, 2 * PFF)
    ffn_b = pad_sh(ffn_dw_b.reshape(NDEV, NSH)).reshape(1, 2 * PFF)
    ffn_wb = jnp.concatenate([ffn_w, ffn_b, jnp.zeros((8 - KF - 1, 2 * PFF), F32)], axis=0)

    b_cols = lax.dynamic_slice(b_ada, (0, me * n_ada), (1, n_ada))
    (mod_all,) = _all_gather("gather_mod", [_mod_part(c_all, w_ada, b_cols)])
    up_t = lambda a: jnp.swapaxes(a, 1, 2)
    w_up_shard = jnp.pad(up_t(w_up)[0].astype(BF16), ((0, PSH - NSH), (0, 0)))
    shards, mod_all = lax.optimization_barrier(((w_up_shard, w_down[0].astype(BF16)), mod_all))
    gather_ffn, token_c = _start_gather("gather_up_down", list(shards), me)
    mod = lax.dynamic_index_in_dim(mod_all, me, axis=1, keepdims=False).reshape(6, D)
    sh1, sc1, gt1, sh2, sc2, gt2 = [mod[k:k + 1] for k in range(6)]
    vecs = jnp.concatenate([norm1_gain, sh1, sc1, gt1, norm2_gain, sh2, sc2, gt2, mix_out_gain,
                            final_gain.reshape(1, D), jnp.zeros((6, D), F32)], axis=0)
    vecs = vecs + token_c[0, 0]
    v512 = jnp.concatenate([conv_dw_b, conv_ln_g, conv_ln_b, gm_ln_g, gm_ln_b, jnp.zeros((3, DC), F32)], axis=0)
    bs_exp = jnp.repeat(jnp.transpose(gm_bs[0]), HD, axis=1)
    gm_ws_t = jnp.swapaxes(gm_ws[0], 1, 2)

    tm_in = min(512, s)
    tm = min(256, s)
    w_in_g, w_out_g = _finish_gather("gather_in_out", gather_in, vecs)
    w_out_b = w_out_g.reshape(D, D)
    z, a0, h1_t = _fwd_in(x2d, vecs, w_in_g, tm_in)
    a1, sp, x2, o1, h2 = _fwd_mid(a0, z, x2d, vecs, v512, conv_w, gm_ws, bs_exp, w_out_b, tm)
    w_up_t, w_down_g = _finish_gather("gather_up_down", gather_ffn, h2)
    w_down_p = jnp.pad(w_down_g.reshape(4, NSH, D), ((0, 0), (0, PSH - NSH), (0, 0)))
    up_pre, vg, dx3, acc_f = _fwd_ffn(h2, x2, target, vecs, ffn_wb, w_up_t, w_down_p, tm)

    core = ac.reshape(1).astype(jnp.int32)
    mychip = 2 * ax + ay

    def to_pairs(named):
        g4s = [g.reshape((4, 2) + g.shape[1:]) for _, g in named]
        from_sibling = _sibling_swap("rs_sibling_" + named[0][0], g4s)
        return [_pair_add("rs_pair_add_" + t[0], g4, rv, core) for t, g4, rv in zip(named, g4s, from_sibling)]

    dh2, dw_up, dw_down, acc_fw = _bwd_ffn(dx3, up_pre, vg, h2, vecs, ffn_wb, w_up_t, w_down_p, tm)
    acc_fw = jnp.transpose(acc_fw, (2, 0, 1, 3)).reshape(8, 2 * PFF)
    exchange_ffn, token_x = _start_exchange("rs_chips_ffn", to_pairs(
        [("w_up", dw_up), ("w_down", dw_down.reshape(NDEV, w_down.shape[1], D))]), mychip)
    gx, dz, y_t, do1, acc_m, acc_p, dcw, dws, dbs_t = _bwd_mid(
        dh2, dx3, x2, x2d, o1, z, a0, a1, sp, vecs + token_x[0, 0], v512, conv_w, gm_ws[0], gm_ws_t, w_out_b, w_in_g, tm)
    rows = jnp.concatenate([acc_m[1:3], acc_m[0:1], acc_m[5:7], acc_f[2:3], acc_m[3:5], acc_m[7:8], acc_f[0:2],
                            jnp.zeros((5, D), F32)], axis=0)
    small_gather, token_s = _start_gather("gather_small_grads", [rows, acc_p, dws, dbs_t, acc_fw, dcw], me)
    dw_in = _mm_all_slots("dw_in", h1_t, dz, w_in.shape[2], min(1024, s), token_s)
    small_pass, token_p = _pass_gather("gather_small_grads", small_gather, dw_in)
    dw_out = _mm_all_slots("dw_out", y_t, do1, D, min(2048, s), token_p).reshape(NDEV, w_out.shape[1], D)
    exchange_mix, token_m = _start_exchange("rs_chips_mix", to_pairs([("w_in", dw_in), ("w_out", dw_out)]), mychip)

    rows_all, p_all, ws_all, bst_all, fwg_all, cwg_all = _end_gather("gather_small_grads", small_pass, token_m)
    (g_b_ada, g_n1, g_mog, g_n2, g_gf, loss_cols, g_cb, g_clg, g_clb, g_glg, g_glb, g_ws, g_bs, fw_sum,
     cw_sum) = _sum_small(rows_all, p_all, ws_all, bst_all, fwg_all, cwg_all)
    loss = jnp.sum(loss_cols)
    g_fb = fw_sum[3].reshape(NDEV, PSH)[:, :NSH].reshape(ffn_dw_b.shape)
    g_fw = lax.dynamic_index_in_dim(fw_sum[:KF].reshape(KF, NDEV, PSH), me, axis=1, keepdims=False)[:, :NSH]
    g_fw = g_fw.reshape(ffn_dw_w.shape)
    g_cw = lax.dynamic_slice(cw_sum, (0, me * n_cw), (KC, n_cw)).reshape(conv_dw_w.shape)
    small = [
        (b_ada, g_b_ada, m_b_ada, v_b_ada), (norm1_gain, g_n1, m_norm1_gain, v_norm1_gain),
        (conv_dw_w, g_cw, m_conv_dw_w, v_conv_dw_w), (conv_dw_b, g_cb, m_conv_dw_b, v_conv_dw_b),
        (conv_ln_g, g_clg, m_conv_ln_g, v_conv_ln_g), (conv_ln_b, g_clb, m_conv_ln_b, v_conv_ln_b),
        (gm_ln_g, g_glg, m_gm_ln_g, v_gm_ln_g), (gm_ln_b, g_glb, m_gm_ln_b, v_gm_ln_b),
        (gm_ws, g_ws, m_gm_ws, v_gm_ws), (gm_bs, g_bs, m_gm_bs, v_gm_bs),
        (mix_out_gain, g_mog, m_mix_out_gain, v_mix_out_gain), (norm2_gain, g_n2, m_norm2_gain, v_norm2_gain),
        (ffn_dw_w, g_fw, m_ffn_dw_w, v_ffn_dw_w), (ffn_dw_b, g_fb, m_ffn_dw_b, v_ffn_dw_b),
        (final_gain, g_gf, m_final_gain, v_final_gain)]
    small_out = _adam_small(small)
    res = {}
    for name, q, o in zip(("b_ada", "norm1_gain", "conv_dw_w", "conv_dw_b", "conv_ln_g", "conv_ln_b", "gm_ln_g",
                           "gm_ln_b", "gm_ws", "gm_bs", "mix_out_gain", "norm2_gain", "ffn_dw_w", "ffn_dw_b",
                           "final_gain"), small, small_out):
        res[name] = (q[1],) + o

    dmod_all = rows_all[:, :6].reshape(NDEV, 6 * D)
    dm_cols = lax.dynamic_slice(dmod_all, (0, me * n_ada), (NDEV, n_ada))
    g_ada = _ada_grad(jnp.transpose(c_all[:, 0, :]), dm_cols)
    res["w_ada"] = (g_ada,) + tuple(_adam3("adam_ada", w_ada, g_ada, m_w_ada, v_w_ada))

    big = [("w_up", up_t(w_up), up_t(m_w_up), up_t(v_w_up)), ("w_down", w_down, m_w_down, v_w_down),
           ("w_in", w_in, m_w_in, v_w_in), ("w_out", w_out, m_w_out, v_w_out)]
    from_chips = list(_finish_exchange("rs_chips_ffn", exchange_ffn, res["w_ada"][1]))
    for t, parts in zip(big[:2], from_chips):
        res[t[0]] = tuple(_sum_adam("rs_sum_adam_" + t[0], parts, t[1], t[2], t[3]))
    res["w_up"] = tuple(up_t(a) for a in res["w_up"])
    from_chips = list(_finish_exchange("rs_chips_mix", exchange_mix, res["w_down"][1]))
    for t, parts in zip(big[2:], from_chips):
        res[t[0]] = tuple(_sum_adam("rs_sum_adam_" + t[0], parts, t[1], t[2], t[3]))

    order = ("w_ada", "b_ada", "norm1_gain", "w_in", "conv_dw_w", "conv_dw_b", "conv_ln_g", "conv_ln_b", "gm_ln_g",
             "gm_ln_b", "gm_ws", "gm_bs", "mix_out_gain", "w_out", "norm2_gain", "w_up", "ffn_dw_w", "ffn_dw_b",
             "w_down", "final_gain")
    return (loss, gx.reshape(x.shape), *[res[n][0] for n in order], *[res[n][1] for n in order],
            *[res[n][2] for n in order], *[res[n][3] for n in order])
```

```python
import functools

import jax
import jax.numpy as jnp
from jax import lax
from jax.experimental import pallas as pl
from jax.experimental.pallas import tpu as pltpu

F32 = jnp.float32
BF16 = jnp.bfloat16
NDEV = 8
D = 1024
DC = 512
DFF = 2816
NSH = 704
PSH = 768
PFF = 4 * PSH
KC = 31
KF = 3
CHUNK = 128
NH = 8
HD = 64
HALO_C = 32
HALO_F = 8
LANES = 128
RMS_EPS = 1e-6
LN_EPS = 1e-5
ADAM_LR = 0.001
ADAM_B1 = 0.9
ADAM_B2 = 0.999
ADAM_EPS = 1e-08
ADAM_WD = 0.01
ADAM_STEP = 10
GELU_K = 0.7978845608028654
GELU_C = 0.044715

MESH = pl.DeviceIdType.MESH
ANY = pl.BlockSpec(memory_space=pl.ANY)

G1, SH1, SC1, GT1, G2, SH2, SC2, GT2, MOG, GF = range(10)
CB, CLG, CLB, GLG, GLB = range(5)


def _full(shape):
    return pl.BlockSpec(shape, lambda *_: (0,) * len(shape))


def _arb(n=1):
    return pltpu.CompilerParams(dimension_semantics=("arbitrary",) * n)


def _row(ref, r):
    return ref[pl.ds(r, 1), :]


def _colsum(v):
    return jnp.sum(v, axis=0, keepdims=True)


def _rowmean(v):
    return jnp.mean(v, axis=-1, keepdims=True)


def _rms(x):
    r = lax.rsqrt(_rowmean(x * x) + RMS_EPS)
    return x * r, r


def _rms_bwd(dxn, xn, r):
    return r * (dxn - xn * _rowmean(dxn * xn))


def _ln(x):
    mu = _rowmean(x)
    xc = x - mu
    rstd = lax.rsqrt(_rowmean(xc * xc) + LN_EPS)
    return xc * rstd, rstd


def _ln_bwd(dxh, xhat, rstd):
    return rstd * (dxh - _rowmean(dxh) - xhat * _rowmean(dxh * xhat))


def _sigmoid(x):
    return 0.5 * jnp.tanh(0.5 * x) + 0.5


def _gelu(x):
    t = jnp.tanh(GELU_K * (x + GELU_C * x * x * x))
    return 0.5 * x * (1.0 + t), t


def _gelu_grad(x, t):
    return 0.5 * (1.0 + t) + 0.5 * x * (1.0 - t * t) * (GELU_K * (1.0 + 3.0 * GELU_C * x * x))


def _dot(a, b):
    return jnp.dot(a, b, preferred_element_type=F32)


def _dot_nt(a, b):
    return lax.dot_general(a, b, (((1,), (1,)), ((), ())), preferred_element_type=F32)


def _shift_up(e, s):
    n = e.shape[0]
    return pltpu.roll(e, (n - s) % n, 0)


def _place():
    return lax.axis_index("x"), lax.axis_index("y"), lax.axis_index("c")


def _all_gather(name, xs):
    n = len(xs)

    def body(*refs):
        x_refs, out_refs = refs[:n], refs[n:2 * n]
        send_sems, recv_sems, local_sems = refs[2 * n:]
        x, y, c = _place()
        me, sibling = (x, y, c), (x, y, 1 - c)
        chips = [(1 - x, y), (x, 1 - y), (1 - x, 1 - y)]

        def copy(a, k, block, to, own=False):
            px, py, pc = block
            slot = out_refs[a].at[4 * px + 2 * py + pc]
            return pltpu.make_async_remote_copy(
                src_ref=x_refs[a] if own else slot, dst_ref=slot,
                send_sem=send_sems.at[7 * a + k], recv_sem=recv_sems.at[7 * a + k], device_id=to, device_id_type=MESH)

        mine = [pltpu.make_async_copy(x_refs[a], out_refs[a].at[4 * x + 2 * y + c], local_sems.at[a]) for a in range(n)]
        for cp in mine:
            cp.start()
        first = []
        for a in range(n):
            first.append(copy(a, 0, me, sibling, own=True))
            first += [copy(a, 1 + j, me, (*chip, c), own=True) for j, chip in enumerate(chips)]
        for cp in first:
            cp.start()
        passed = []
        for j, chip in enumerate(chips):
            for a in range(n):
                copy(a, 1 + j, (*chip, c), me).wait_recv()
                cp = copy(a, 4 + j, (*chip, c), sibling)
                cp.start()
                passed.append(cp)
        for a in range(n):
            copy(a, 0, sibling, me).wait_recv()
            for j, chip in enumerate(chips):
                copy(a, 4 + j, (*chip, 1 - c), me).wait_recv()
        for cp in first + passed:
            cp.wait_send()
        for cp in mine:
            cp.wait()

    return pl.pallas_call(
        body, name=name, out_shape=[jax.ShapeDtypeStruct((NDEV,) + a.shape, a.dtype) for a in xs],
        in_specs=[ANY] * n, out_specs=[ANY] * n,
        scratch_shapes=[pltpu.SemaphoreType.DMA((7 * n,)), pltpu.SemaphoreType.DMA((7 * n,)),
                        pltpu.SemaphoreType.DMA((n,))],
    )(*xs)


def _sibling_swap(name, g4s):
    n = len(g4s)

    def body(*refs):
        g_refs, out_refs = refs[:n], refs[n:2 * n]
        send_sems, recv_sems = refs[2 * n:]
        x, y, c = _place()
        cps = [pltpu.make_async_remote_copy(
            src_ref=g_refs[a].at[k, 1 - c], dst_ref=out_refs[a].at[k],
            send_sem=send_sems.at[4 * a + k], recv_sem=recv_sems.at[4 * a + k],
            device_id=(x, y, 1 - c), device_id_type=MESH) for a in range(n) for k in range(4)]
        for cp in cps:
            cp.start()
        for cp in cps:
            cp.wait()

    return pl.pallas_call(
        body, name=name, out_shape=[jax.ShapeDtypeStruct((4,) + g.shape[2:], g.dtype) for g in g4s],
        in_specs=[ANY] * n, out_specs=[ANY] * n,
        scratch_shapes=[pltpu.SemaphoreType.DMA((4 * n,)), pltpu.SemaphoreType.DMA((4 * n,))],
    )(*g4s)


HBM = pl.BlockSpec(memory_space=pltpu.HBM)
SEM = pl.BlockSpec(memory_space=pltpu.SEMAPHORE)
EFFECT = pltpu.SideEffectType.DATAFLOW_SIDE_EFFECTING


def _in_hbm(a):
    return pltpu.with_memory_space_constraint(a, pltpu.HBM)


def _split_start(name, bufs, copies):
    n = len(bufs)

    def body(*refs):
        for cp in copies(refs[:n], refs[n], refs[n + 1]):
            cp.start()
        refs[-1][...] = jnp.zeros_like(refs[-1])

    out = pl.pallas_call(
        body, name=name,
        out_shape=(pltpu.SemaphoreType.DMA((copies.count,)), pltpu.SemaphoreType.DMA((copies.count,)),
                   *[pltpu.HBM(a.shape, a.dtype) for a in bufs], jax.ShapeDtypeStruct((8, LANES), F32)),
        in_specs=[HBM] * n, out_specs=(SEM, SEM, *[HBM] * n, pl.BlockSpec(memory_space=pltpu.VMEM)),
        input_output_aliases={i: 2 + i for i in range(n)},
        compiler_params=pltpu.CompilerParams(has_side_effects=EFFECT),
    )(*[_in_hbm(a) for a in bufs])
    return (out[0], out[1], list(out[2:2 + n])), out[-1]


def _split_wait(name, handle, copies, after):
    send_sems, recv_sems, bufs = handle
    n = len(bufs)

    def body(*refs):
        for cp in copies(refs[:n], refs[n], refs[n + 1]):
            cp.wait_send()
            cp.wait_recv()

    out = pl.pallas_call(
        body, name=name, out_shape=tuple(pltpu.HBM(a.shape, a.dtype) for a in bufs),
        in_specs=[HBM] * n + [SEM, SEM, pl.BlockSpec(memory_space=pl.ANY)], out_specs=tuple([HBM] * n),
        input_output_aliases={i: i for i in range(n)},
        compiler_params=pltpu.CompilerParams(has_side_effects=EFFECT),
    )(*bufs, send_sems, recv_sems, after)
    return list(out)


class _GatherFirstCopies:
    def __init__(self, n):
        self.n, self.count = n, 4 * n

    def __call__(self, refs, send_sems, recv_sems):
        x, y, c = _place()
        peers = [(x, y, 1 - c), (1 - x, y, c), (x, 1 - y, c), (1 - x, 1 - y, c)]
        return [pltpu.make_async_remote_copy(
            src_ref=refs[a], dst_ref=refs[self.n + a].at[4 * x + 2 * y + c],
            send_sem=send_sems.at[4 * a + k], recv_sem=recv_sems.at[4 * a + k], device_id=peer, device_id_type=MESH)
            for a in range(self.n) for k, peer in enumerate(peers)]


class _GatherPassCopies:
    def __init__(self, n):
        self.n, self.count = n, 3 * n

    def __call__(self, refs, send_sems, recv_sems):
        x, y, c = _place()
        cps = []
        for a in range(self.n):
            for j, (px, py) in enumerate([(1 - x, y), (x, 1 - y), (1 - x, 1 - y)]):
                slot = refs[a].at[4 * px + 2 * py + c]
                cps.append(pltpu.make_async_remote_copy(
                    src_ref=slot, dst_ref=slot, send_sem=send_sems.at[3 * a + j], recv_sem=recv_sems.at[3 * a + j],
                    device_id=(x, y, 1 - c), device_id_type=MESH))
        return cps


class _ExchangeCopies:
    def __init__(self, n):
        self.n, self.count = n, 3 * n

    def __call__(self, refs, send_sems, recv_sems):
        x, y, c = _place()
        cps = []
        for a in range(self.n):
            for j, (px, py) in enumerate([(1 - x, y), (x, 1 - y), (1 - x, 1 - y)]):
                cps.append(pltpu.make_async_remote_copy(
                    src_ref=refs[a].at[2 * px + py], dst_ref=refs[self.n + a].at[2 * x + y],
                    send_sem=send_sems.at[3 * a + j], recv_sem=recv_sems.at[3 * a + j],
                    device_id=(px, py, c), device_id_type=MESH))
        return cps


def _own_slot(nslot, src, index):
    land = lax.empty((nslot,) + src.shape, src.dtype)
    return lax.dynamic_update_slice(land, src[None], (index,) + (0,) * src.ndim)


def _start_gather(tag, xs, me):
    lands = [_own_slot(NDEV, a, me) for a in xs]
    return _split_start(tag + "_start", list(xs) + lands, _GatherFirstCopies(len(xs)))


def _pass_gather(tag, handle, after):
    n = len(handle[2]) // 2
    lands = _split_wait(tag + "_wait", handle, _GatherFirstCopies(n), after)[n:]
    return _split_start(tag + "_pass", lands, _GatherPassCopies(n))


def _end_gather(tag, passing, after):
    return _split_wait(tag + "_pass_wait", passing, _GatherPassCopies(len(passing[2])), after)


def _finish_gather(tag, handle, after):
    passing, token = _pass_gather(tag, handle, after)
    return _end_gather(tag, passing, token)


def _start_exchange(tag, hs, mychip):
    lands = [_own_slot(4, lax.dynamic_index_in_dim(h, mychip, 0, keepdims=False), mychip) for h in hs]
    return _split_start(tag + "_start", list(hs) + lands, _ExchangeCopies(len(hs)))


def _finish_exchange(tag, handle, after):
    n = len(handle[2]) // 2
    return _split_wait(tag + "_wait", handle, _ExchangeCopies(n), after)[n:]


def _mod_part(c_all, w_ada, b_cols):
    ncol = w_ada.shape[2]

    def body(c_ref, w_ref, b_ref, o_ref):
        cv = c_ref[:, 0, :]
        ca = cv * _sigmoid(cv)
        o_ref[...] = _dot(ca.astype(BF16), w_ref[0].astype(BF16)) + b_ref[...]

    return pl.pallas_call(body, name="mod_part", out_shape=jax.ShapeDtypeStruct((NDEV, ncol), F32))(
        c_all, w_ada, b_cols)


def _ada_grad(c_all_t, dmod_cols):
    ncol = dmod_cols.shape[1]

    def body(ct_ref, dm_ref, o_ref):
        ct = ct_ref[...]
        ca = ct * _sigmoid(ct)
        acc = jnp.zeros((D, ncol), F32)
        for b in range(NDEV):
            acc = acc + ca[:, b:b + 1] * dm_ref[pl.ds(b, 1), :]
        o_ref[0] = acc

    return pl.pallas_call(body, name="ada_grad", out_shape=jax.ShapeDtypeStruct((1, D, ncol), F32))(
        c_all_t, dmod_cols)


def _fwd_in(x2d, vecs, w_in_g, tm):
    s = x2d.shape[0]
    nc = w_in_g.shape[2]

    def body(x_ref, v_ref, w_ref, z_ref, a0_ref, h1t_ref):
        xn, _ = _rms(x_ref[...])
        h = (xn * _row(v_ref, G1)) * (1.0 + _row(v_ref, SC1)) + _row(v_ref, SH1)
        hb = h.astype(BF16)
        h1t_ref[...] = hb.T
        for d in range(NDEV):
            z_ref[:, pl.ds(d * nc, nc)] = _dot(hb, w_ref[d])
        a0_ref[...] = z_ref[:, :DC] * _sigmoid(z_ref[:, DC:2 * DC])

    return pl.pallas_call(
        body, name="fwd_in", grid=(s // tm,),
        in_specs=[pl.BlockSpec((tm, D), lambda i: (i, 0)), _full((16, D)), _full((NDEV, D, nc))],
        out_specs=[pl.BlockSpec((tm, 4 * DC), lambda i: (i, 0)), pl.BlockSpec((tm, DC), lambda i: (i, 0)),
                   pl.BlockSpec((D, tm), lambda i: (0, i))],
        out_shape=[jax.ShapeDtypeStruct((s, 4 * DC), F32), jax.ShapeDtypeStruct((s, DC), F32),
                   jax.ShapeDtypeStruct((D, s), BF16)],
        compiler_params=_arb(),
    )(x2d, vecs, w_in_g)


def _causal_mask(lower):
    r = lax.broadcasted_iota(jnp.int32, (CHUNK, CHUNK), 0)
    c = lax.broadcasted_iota(jnp.int32, (CHUNK, CHUNK), 1)
    return (r >= c) if lower else (r <= c)


def _first_head_lanes():
    return lax.broadcasted_iota(jnp.int32, (CHUNK, CHUNK), 1) < HD


def _fwd_mid(a0, z, x2d, vecs, v512, conv_w, gm_ws, bs_exp, w_out_b, tm):
    s = x2d.shape[0]
    hb = tm // HALO_C

    def body(a0_ref, halo_ref, zg_ref, x_ref, v_ref, p_ref, cw_ref, ws_ref, bs_ref, wo_ref,
             a1_ref, sp_ref, x2_ref, o1_ref, h2_ref):
        i = pl.program_id(0)
        for c0 in range(0, DC, LANES):
            cols = pl.ds(c0, LANES)
            halo = halo_ref[:, cols]
            e = jnp.concatenate([jnp.where(i > 0, halo, jnp.zeros_like(halo)), a0_ref[:, cols]], axis=0)
            acc = jnp.broadcast_to(p_ref[pl.ds(CB, 1), cols], (tm, LANES))
            for k in range(KC):
                acc = acc + _shift_up(e, HALO_C - (KC - 1) + k)[:tm, :] * cw_ref[pl.ds(k, 1), cols]
            a1_ref[:, cols] = acc
        xh, _ = _ln(a1_ref[...])
        a2 = xh * _row(p_ref, CLG) + _row(p_ref, CLB)
        a3 = a2 * _sigmoid(a2)
        gu, _ = _gelu(zg_ref[:, :DC])
        gvg, _ = _gelu(zg_ref[:, DC:])
        vh, _ = _ln(gvg)
        gvn = (vh * _row(p_ref, GLG) + _row(p_ref, GLB)).astype(BF16)
        low = _causal_mask(True)
        first = _first_head_lanes()
        wm = [jnp.where(low, ws_ref[0, h], 0.0).astype(BF16) for h in range(NH)]
        for n in range(tm // CHUNK):
            for p in range(NH // 2):
                v = gvn[n * CHUNK:(n + 1) * CHUNK, p * CHUNK:(p + 1) * CHUNK]
                blk = jnp.where(first, _dot(wm[2 * p], v), _dot(wm[2 * p + 1], v))
                sp_ref[pl.ds(n * CHUNK, CHUNK), pl.ds(p * CHUNK, CHUNK)] = blk + bs_ref[:, pl.ds(p * CHUNK, CHUNK)]
        g = gu * sp_ref[...]
        an, _ = _rms(a3)
        gn, _ = _rms(g)
        mog = _row(v_ref, MOG)
        y = jnp.concatenate([an * mog[:, :DC], gn * mog[:, DC:]], axis=1).astype(BF16)
        o1 = _dot(y, wo_ref[...])
        o1_ref[...] = o1
        x2 = x_ref[...] + _row(v_ref, GT1) * o1
        x2_ref[...] = x2
        xn2, _ = _rms(x2)
        h2 = (xn2 * _row(v_ref, G2)) * (1.0 + _row(v_ref, SC2)) + _row(v_ref, SH2)
        h2_ref[...] = h2.astype(BF16)

    tile = lambda w: pl.BlockSpec((tm, w), lambda i: (i, 0))
    return pl.pallas_call(
        body, name="fwd_mid", grid=(s // tm,),
        in_specs=[tile(DC), pl.BlockSpec((HALO_C, DC), lambda i: (jnp.maximum(i * hb - 1, 0), 0)),
                  pl.BlockSpec((tm, 2 * DC), lambda i: (i, 1)), tile(D), _full((16, D)), _full((8, DC)),
                  _full((32, DC)), _full((1, NH, CHUNK, CHUNK)), _full((CHUNK, DC)), _full((D, D))],
        out_specs=[tile(DC), tile(DC), tile(D), tile(D), tile(D)],
        out_shape=[jax.ShapeDtypeStruct((s, DC), F32), jax.ShapeDtypeStruct((s, DC), F32),
                   jax.ShapeDtypeStruct((s, D), F32), jax.ShapeDtypeStruct((s, D), F32),
                   jax.ShapeDtypeStruct((s, D), BF16)],
        compiler_params=_arb(),
    )(a0, a0, z, x2d, vecs, v512, conv_w, gm_ws, bs_exp, w_out_b)


def _ffn_conv(fw_ref, cols, p2, p1, pre):
    return (fw_ref[pl.ds(3, 1), cols] + fw_ref[pl.ds(0, 1), cols] * p2
            + fw_ref[pl.ds(1, 1), cols] * p1 + fw_ref[pl.ds(2, 1), cols] * pre)


def _fwd_ffn(h2, x2, target, vecs, ffn_wb, w_up_t, w_down_p, tm):
    s = x2.shape[0]

    def body(h2_ref, x2_ref, t_ref, v_ref, fw_ref, wu_hbm, wd_hbm,
             up_ref, vg_ref, dx3_ref, acc_ref, wu, wd, carry, stage):
        i = pl.program_id(0)

        @pl.when(i == 0)
        def _():
            for sh in range(NDEV):
                pltpu.sync_copy(wu_hbm.at[sh], stage)
                wu[sh] = stage[...].T
            pltpu.sync_copy(wd_hbm, wd)
            carry[...] = jnp.zeros_like(carry)
            acc_ref[...] = jnp.zeros_like(acc_ref)

        h2v = h2_ref[...]
        o2 = jnp.zeros((tm, D), F32)
        for j in range(4):
            conv = []
            for sh in (j, 4 + j):
                cols = pl.ds(sh * PSH, PSH)
                pre = _dot(h2v, wu[sh])
                up_ref[:, cols] = pre.astype(BF16)
                e = jnp.concatenate([carry[:, cols], pre], axis=0)
                carry[:, cols] = pre[tm - HALO_F:, :]
                conv.append(_ffn_conv(fw_ref, cols, pltpu.roll(e, 2, 0)[HALO_F:, :],
                                      pltpu.roll(e, 1, 0)[HALO_F:, :], pre))
            val, gate = conv
            vg_ref[:, pl.ds(j * PSH, PSH)] = val.astype(BF16)
            vg_ref[:, pl.ds((4 + j) * PSH, PSH)] = gate.astype(BF16)
            f = ((gate * _sigmoid(gate)) * val).astype(BF16)
            o2 = o2 + _dot(f, wd[j])
        x3 = x2_ref[...] + _row(v_ref, GT2) * o2
        xn3, r3 = _rms(x3)
        gf = _row(v_ref, GF)
        diff = xn3 * gf - t_ref[...]
        acc_ref[pl.ds(1, 1), :] += _colsum(diff * diff) * (0.5 / D)
        dout = diff * (1.0 / D)
        acc_ref[pl.ds(0, 1), :] += _colsum(dout * xn3)
        dx3 = _rms_bwd(dout * gf, xn3, r3)
        dx3_ref[...] = dx3
        acc_ref[pl.ds(2, 1), :] += _colsum(dx3 * o2)

    tile = lambda w: pl.BlockSpec((tm, w), lambda i: (i, 0))
    return pl.pallas_call(
        body, name="fwd_ffn", grid=(s // tm,),
        in_specs=[tile(D), tile(D), tile(D), _full((16, D)), _full((8, 2 * PFF)), ANY, ANY],
        out_specs=[tile(2 * PFF), tile(2 * PFF), tile(D), _full((8, D))],
        out_shape=[jax.ShapeDtypeStruct((s, 2 * PFF), BF16), jax.ShapeDtypeStruct((s, 2 * PFF), BF16),
                   jax.ShapeDtypeStruct((s, D), F32), jax.ShapeDtypeStruct((8, D), F32)],
        scratch_shapes=[pltpu.VMEM((NDEV, D, PSH), BF16), pltpu.VMEM((4, PSH, D), BF16),
                        pltpu.VMEM((HALO_F, 2 * PFF), F32), pltpu.VMEM((PSH, D), BF16)],
        compiler_params=_arb(),
    )(h2, x2, target, vecs, ffn_wb, w_up_t, w_down_p)


def _bwd_ffn(dx3, up_pre, vg, h2, vecs, ffn_wb, w_up_t, w_down_p, tm):
    s = dx3.shape[0]
    nt = s // tm

    def body(dx3_ref, up_ref, upg_ref, val_ref, gate_ref, h2_ref, v_ref, fw_ref, fwg_ref, wu_ref, wug_ref, wd_ref,
             dh2_ref, dwu_ref, dwd_ref, accf_ref, carry):
        i = pl.program_id(1)

        @pl.when(i == 0)
        def _():
            for ref in (carry, dwu_ref, dwd_ref, accf_ref):
                ref[...] = jnp.zeros_like(ref)

        do2 = (dx3_ref[...] * _row(v_ref, GT2)).astype(BF16)
        df = _dot_nt(do2, wd_ref[...])
        val = val_ref[...].astype(F32)
        gate = gate_ref[...].astype(F32)
        sg = _sigmoid(gate)
        sl = gate * sg
        f_t = (sl * val).astype(BF16).T
        dwd_ref[...] += _dot(f_t, do2)[:NSH, :]
        dups = (df * sl, df * val * (sg * (1.0 + gate * (1.0 - sg))))
        h2v = h2_ref[...]
        dh2 = jnp.zeros((tm, D), F32)
        for half, (dup, pre_ref, w_ref, wmat_ref) in enumerate(
                zip(dups, (up_ref, upg_ref), (fw_ref, fwg_ref), (wu_ref, wug_ref))):
            cols = pl.ds(half * PSH, PSH)
            e = jnp.concatenate([dup, carry[:, cols]], axis=0)
            carry[:, cols] = dup[:HALO_F, :]
            d1 = _shift_up(e, 1)[:tm, :]
            d2 = _shift_up(e, 2)[:tm, :]
            pre = pre_ref[...].astype(F32)
            accf_ref[half, pl.ds(3, 1), :] += _colsum(dup)
            accf_ref[half, pl.ds(0, 1), :] += _colsum(d2 * pre)
            accf_ref[half, pl.ds(1, 1), :] += _colsum(d1 * pre)
            accf_ref[half, pl.ds(2, 1), :] += _colsum(dup * pre)
            dpre = (_row(w_ref, 0) * d2 + _row(w_ref, 1) * d1 + _row(w_ref, 2) * dup).astype(BF16)
            dwu_ref[half] += _dot(dpre.T, h2v)[:NSH, :]
            dh2 = dh2 + _dot(dpre, wmat_ref[...])
        dh2_ref[...] = dh2

    rev = lambda j, i: nt - 1 - i
    in_specs = [
        pl.BlockSpec((tm, D), lambda j, i: (rev(j, i), 0)),
        pl.BlockSpec((tm, PSH), lambda j, i: (rev(j, i), j)), pl.BlockSpec((tm, PSH), lambda j, i: (rev(j, i), 4 + j)),
        pl.BlockSpec((tm, PSH), lambda j, i: (rev(j, i), j)), pl.BlockSpec((tm, PSH), lambda j, i: (rev(j, i), 4 + j)),
        pl.BlockSpec((tm, D), lambda j, i: (rev(j, i), 0)), _full((16, D)),
        pl.BlockSpec((8, PSH), lambda j, i: (0, j)), pl.BlockSpec((8, PSH), lambda j, i: (0, 4 + j)),
        pl.BlockSpec((None, PSH, D), lambda j, i: (j, 0, 0)), pl.BlockSpec((None, PSH, D), lambda j, i: (4 + j, 0, 0)),
        pl.BlockSpec((None, PSH, D), lambda j, i: (j, 0, 0))]
    dh2, dw_up, dw_down, accf = pl.pallas_call(
        body, name="bwd_ffn", grid=(4, nt), in_specs=in_specs,
        out_specs=[pl.BlockSpec((None, tm, D), lambda j, i: (j, rev(j, i), 0)),
                   pl.BlockSpec((2, None, NSH, D), lambda j, i: (0, j, 0, 0)),
                   pl.BlockSpec((None, NSH, D), lambda j, i: (j, 0, 0)),
                   pl.BlockSpec((2, None, 8, PSH), lambda j, i: (0, j, 0, 0))],
        out_shape=[jax.ShapeDtypeStruct((4, s, D), F32), jax.ShapeDtypeStruct((2, 4, NSH, D), F32),
                   jax.ShapeDtypeStruct((4, NSH, D), F32), jax.ShapeDtypeStruct((2, 4, 8, PSH), F32)],
        scratch_shapes=[pltpu.VMEM((HALO_F, 2 * PSH), F32)],
        compiler_params=_arb(2),
    )(dx3, up_pre, up_pre, vg, vg, h2, vecs, ffn_wb, ffn_wb, w_up_t, w_up_t, w_down_p)
    return dh2, dw_up.reshape(NDEV, NSH, D), dw_down, accf


def _bwd_mid(dh2, dx3, x2, x2d, o1, z, a0, a1, sp, vecs, v512, conv_w, gm_ws, gm_ws_t, w_out_b, w_in_g, tm):
    s = x2d.shape[0]
    nt = s // tm
    hb = tm // HALO_C
    nc = w_in_g.shape[2]

    def body(dh2a_ref, dh2b_ref, dh2c_ref, dh2d_ref, dx3_ref, x2_ref, x_ref, o1_ref, z_ref, a0_ref, halo_ref, a1_ref, sp_ref, v_ref, p_ref, cw_ref,
             ws_ref, wst_ref, wo_ref, wi_ref, gx_ref, dz_ref, yt_ref, do1_ref, acc_ref, accp_ref, dcw_ref, dws_ref,
             dbst_ref, dbs_s, carry, da1_s, dsp_s, dgvn_s):
        i = pl.program_id(0)
        r = nt - 1 - i

        @pl.when(i == 0)
        def _():
            for ref in (carry, dbs_s, acc_ref, accp_ref, dcw_ref, dws_ref, dbst_ref):
                ref[...] = jnp.zeros_like(ref)

        dh2v = (dh2a_ref[...] + dh2b_ref[...]) + (dh2c_ref[...] + dh2d_ref[...])
        xn2, r2 = _rms(x2_ref[...])
        g2 = _row(v_ref, G2)
        sc2 = 1.0 + _row(v_ref, SC2)
        acc_ref[pl.ds(5, 1), :] += _colsum(dh2v)
        acc_ref[pl.ds(6, 1), :] += _colsum(dh2v * (xn2 * g2))
        acc_ref[pl.ds(7, 1), :] += _colsum(dh2v * sc2 * xn2)
        dx2v = dx3_ref[...] + _rms_bwd(dh2v * sc2 * g2, xn2, r2)
        do1 = (dx2v * _row(v_ref, GT1)).astype(BF16)
        do1_ref[...] = do1
        acc_ref[pl.ds(0, 1), :] += _colsum(dx2v * o1_ref[...])
        dy = _dot_nt(do1, wo_ref[...])
        mog = _row(v_ref, MOG)

        xh, rstd = _ln(a1_ref[...])
        clg = _row(p_ref, CLG)
        a2 = xh * clg + _row(p_ref, CLB)
        s2 = _sigmoid(a2)
        a3 = a2 * s2
        an, ra = _rms(a3)
        dya = dy[:, :DC]
        da3 = _rms_bwd(dya * mog[:, :DC], an, ra)
        da2 = da3 * (s2 * (1.0 + a2 * (1.0 - s2)))
        accp_ref[pl.ds(CLB, 1), :] += _colsum(da2)
        accp_ref[pl.ds(CLG, 1), :] += _colsum(da2 * xh)
        da1 = _ln_bwd(da2 * clg, xh, rstd)
        accp_ref[pl.ds(CB, 1), :] += _colsum(da1)
        da1_s[...] = da1
        for c0 in range(0, DC, LANES):
            cols = pl.ds(c0, LANES)
            d = da1_s[:, cols]
            e = jnp.concatenate([d, carry[:, cols]], axis=0)
            carry[:, cols] = d[:HALO_C, :]
            acc = jnp.zeros((tm, LANES), F32)
            for j in range(KC):
                acc = acc + _shift_up(e, j)[:tm, :] * cw_ref[pl.ds(KC - 1 - j, 1), cols]
            sgc = _sigmoid(z_ref[:, pl.ds(DC + c0, LANES)])
            dz_ref[:, cols] = (acc * sgc).astype(BF16)
            dz_ref[:, pl.ds(DC + c0, LANES)] = (acc * z_ref[:, cols] * sgc * (1.0 - sgc)).astype(BF16)
            halo = halo_ref[:, cols]
            ea = jnp.concatenate([jnp.where(r > 0, halo, jnp.zeros_like(halo)), a0_ref[:, cols]], axis=0)
            for k in range(KC):
                dcw_ref[pl.ds(k, 1), cols] += _colsum(d * _shift_up(ea, HALO_C - (KC - 1) + k)[:tm, :])

        gu_pre = z_ref[:, 2 * DC:3 * DC]
        gv_pre = z_ref[:, 3 * DC:]
        gu, tu = _gelu(gu_pre)
        gvg, tv = _gelu(gv_pre)
        vh, vrstd = _ln(gvg)
        glg = _row(p_ref, GLG)
        gvn = (vh * glg + _row(p_ref, GLB)).astype(BF16)
        spv = sp_ref[...]
        g = gu * spv
        gn, rg = _rms(g)
        yt_ref[...] = jnp.concatenate([an * mog[:, :DC], gn * mog[:, DC:]], axis=1).astype(BF16).T
        acc_ref[pl.ds(4, 1), :] += jnp.concatenate([_colsum(dya * an), _colsum(dy[:, DC:] * gn)], axis=1)
        dg = _rms_bwd(dy[:, DC:] * mog[:, DC:], gn, rg)
        dz_ref[:, pl.ds(2 * DC, DC)] = (dg * spv * _gelu_grad(gu_pre, tu)).astype(BF16)
        dsp_s[...] = dg * gu
        upper = _causal_mask(False)
        first = _first_head_lanes()
        wmt = [jnp.where(upper, wst_ref[h], 0.0).astype(BF16) for h in range(NH)]
        for n in range(tm // CHUNK):
            rows = pl.ds(n * CHUNK, CHUNK)
            for p in range(NH // 2):
                cols = pl.ds(p * CHUNK, CHUNK)
                dsp = dsp_s[rows, cols]
                dbs_s[:, cols] += dsp
                da = jnp.where(first, dsp, 0.0).astype(BF16)
                db = jnp.where(first, 0.0, dsp).astype(BF16)
                v = gvn[n * CHUNK:(n + 1) * CHUNK, p * CHUNK:(p + 1) * CHUNK]
                dws_ref[2 * p] += _dot_nt(da, v)
                dws_ref[2 * p + 1] += _dot_nt(db, v)
                dgvn_s[rows, cols] = _dot(wmt[2 * p], da) + _dot(wmt[2 * p + 1], db)
        dgvn = dgvn_s[...]
        accp_ref[pl.ds(GLB, 1), :] += _colsum(dgvn)
        accp_ref[pl.ds(GLG, 1), :] += _colsum(dgvn * vh)
        dgvg = _ln_bwd(dgvn * glg, vh, vrstd)
        dz_ref[:, pl.ds(3 * DC, DC)] = (dgvg * _gelu_grad(gv_pre, tv)).astype(BF16)

        dh1 = jnp.zeros((tm, D), F32)
        for d in range(NDEV):
            dh1 = dh1 + _dot_nt(dz_ref[:, pl.ds(d * nc, nc)], wi_ref[d])
        xn, r1 = _rms(x_ref[...])
        g1 = _row(v_ref, G1)
        sc = 1.0 + _row(v_ref, SC1)
        acc_ref[pl.ds(1, 1), :] += _colsum(dh1)
        acc_ref[pl.ds(2, 1), :] += _colsum(dh1 * (xn * g1))
        acc_ref[pl.ds(3, 1), :] += _colsum(dh1 * sc * xn)
        gx_ref[...] = dx2v + _rms_bwd(dh1 * sc * g1, xn, r1)

        @pl.when(i == nt - 1)
        def _():
            low = _causal_mask(True)
            for h in range(NH):
                dws_ref[h] = jnp.where(low, dws_ref[h], 0.0)
            lane = lax.broadcasted_iota(jnp.int32, (CHUNK, CHUNK), 1)
            out = jnp.zeros((CHUNK, CHUNK), F32)
            for h in range(NH):
                hs = jnp.sum(dbs_s[:, pl.ds((h // 2) * CHUNK, CHUNK)]
                             * ((lane >= (h % 2) * HD) & (lane < (h % 2 + 1) * HD)).astype(F32),
                             axis=1, keepdims=True)
                out = jnp.where(lane == h, hs, out)
            dbst_ref[...] = out

    tile = lambda w: pl.BlockSpec((tm, w), lambda i: (nt - 1 - i, 0))
    return pl.pallas_call(
        body, name="bwd_mid", grid=(nt,),
        in_specs=[pl.BlockSpec((None, tm, D), functools.partial(lambda k, i: (k, nt - 1 - i, 0), k)) for k in range(4)]
        + [tile(D), tile(D), tile(D), tile(D), tile(4 * DC), tile(DC),
                  pl.BlockSpec((HALO_C, DC), lambda i: (jnp.maximum((nt - 1 - i) * hb - 1, 0), 0)),
                  tile(DC), tile(DC), _full((16, D)), _full((8, DC)), _full((32, DC)),
                  _full((NH, CHUNK, CHUNK)), _full((NH, CHUNK, CHUNK)), _full((D, D)), _full((NDEV, D, nc))],
        out_specs=[tile(D), tile(4 * DC), pl.BlockSpec((D, tm), lambda i: (0, nt - 1 - i)), tile(D),
                   _full((16, D)), _full((8, DC)), _full((32, DC)),
                   _full((NH, CHUNK, CHUNK)), _full((CHUNK, CHUNK))],
        out_shape=[jax.ShapeDtypeStruct((s, D), F32), jax.ShapeDtypeStruct((s, 4 * DC), BF16),
                   jax.ShapeDtypeStruct((D, s), BF16), jax.ShapeDtypeStruct((s, D), BF16),
                   jax.ShapeDtypeStruct((16, D), F32), jax.ShapeDtypeStruct((8, DC), F32),
                   jax.ShapeDtypeStruct((32, DC), F32), jax.ShapeDtypeStruct((NH, CHUNK, CHUNK), F32),
                   jax.ShapeDtypeStruct((CHUNK, CHUNK), F32)],
        scratch_shapes=[pltpu.VMEM((CHUNK, DC), F32), pltpu.VMEM((HALO_C, DC), F32), pltpu.VMEM((tm, DC), F32),
                        pltpu.VMEM((tm, DC), F32), pltpu.VMEM((tm, DC), F32)],
        compiler_params=_arb(),
    )(dh2, dh2, dh2, dh2, dx3, x2, x2d, o1, z, a0, a0, a1, sp, vecs, v512, conv_w, gm_ws, gm_ws_t, w_out_b, w_in_g)


def _mm_all_slots(name, at, b, bw, tk, after):
    k1, s = at.shape
    nslot = b.shape[1] // bw

    def body(a_ref, b_ref, after_ref, o_ref):
        @pl.when(pl.program_id(0) == 0)
        def _():
            o_ref[...] = jnp.zeros_like(o_ref)

        t = _dot(a_ref[...], b_ref[...])
        for j in range(nslot):
            o_ref[j] += t[:, j * bw:(j + 1) * bw]

    return pl.pallas_call(
        body, name=name, grid=(s // tk,),
        in_specs=[pl.BlockSpec((k1, tk), lambda k: (0, k)), pl.BlockSpec((tk, nslot * bw), lambda k: (k, 0)), ANY],
        out_specs=_full((nslot, k1, bw)), out_shape=jax.ShapeDtypeStruct((nslot, k1, bw), F32),
        compiler_params=_arb(),
    )(at, b, after)


def _adam_math(w, g, m, v):
    m = ADAM_B1 * m + (1.0 - ADAM_B1) * g
    v = ADAM_B2 * v + (1.0 - ADAM_B2) * (g * g)
    m_hat = m / (1.0 - ADAM_B1 ** ADAM_STEP)
    v_hat = v / (1.0 - ADAM_B2 ** ADAM_STEP)
    delta = -ADAM_LR * (m_hat / (jnp.sqrt(v_hat) + ADAM_EPS) + ADAM_WD * w)
    return delta, m, v


def _row_block(rows, cols):
    tr = rows
    while tr * cols * 4 > (2 << 20) and tr % 32 == 0:
        tr //= 2
    return tr


def _adam3(name, w, g, m, v):
    _, rows, cols = w.shape
    tr = _row_block(rows, cols)

    def body(w_ref, g_ref, m_ref, v_ref, d_ref, mo_ref, vo_ref):
        d_ref[...], mo_ref[...], vo_ref[...] = _adam_math(w_ref[...], g_ref[...], m_ref[...], v_ref[...])

    spec = pl.BlockSpec((1, tr, cols), lambda i: (0, i, 0))
    return pl.pallas_call(
        body, name=name, grid=(rows // tr,), in_specs=[spec] * 4, out_specs=[spec] * 3,
        out_shape=[jax.ShapeDtypeStruct(w.shape, F32)] * 3, compiler_params=_arb(),
    )(w, g, m, v)


def _sum_adam(name, parts, w, m, v):
    n, rows, cols = parts.shape
    tr = _row_block(rows, cols)

    def body(p_ref, w_ref, m_ref, v_ref, g_ref, d_ref, mo_ref, vo_ref):
        g = p_ref[0].astype(F32)
        for k in range(1, n):
            g = g + p_ref[k].astype(F32)
        g_ref[0] = g
        d_ref[0], mo_ref[0], vo_ref[0] = _adam_math(w_ref[0], g, m_ref[0], v_ref[0])

    spec = pl.BlockSpec((1, tr, cols), lambda i: (0, i, 0))
    return pl.pallas_call(
        body, name=name, grid=(rows // tr,),
        in_specs=[pl.BlockSpec((n, tr, cols), lambda i: (0, i, 0))] + [spec] * 3, out_specs=[spec] * 4,
        out_shape=[jax.ShapeDtypeStruct(w.shape, F32)] * 4, compiler_params=_arb(),
    )(parts, w, m, v)


def _pair_add(name, g4, recv, core):
    _, _, rows, cols = g4.shape
    tr = _row_block(rows, cols)

    def body(c_ref, a_ref, b_ref, o_ref):
        o_ref[...] = (a_ref[...] + b_ref[...]).astype(BF16)

    return pl.pallas_call(
        body, name=name,
        grid_spec=pltpu.PrefetchScalarGridSpec(
            num_scalar_prefetch=1, grid=(4, rows // tr),
            in_specs=[pl.BlockSpec((None, None, tr, cols), lambda k, i, c_ref: (k, c_ref[0], i, 0)),
                      pl.BlockSpec((None, tr, cols), lambda k, i, c_ref: (k, i, 0))],
            out_specs=pl.BlockSpec((None, tr, cols), lambda k, i, c_ref: (k, i, 0))),
        out_shape=jax.ShapeDtypeStruct((4, rows, cols), BF16), compiler_params=_arb(2),
    )(core, g4, recv)


def _sum_small(rows_all, p_all, ws_all, bst_all, fw_all, cw_all):
    def body(a_ref, p_ref, ws_ref, bst_ref, fw_ref, cw_ref,
             g_b_ada, g_n1, g_mog, g_n2, g_gf, loss_cols, g_cb, g_clg, g_clb, g_glg, g_glb, g_ws, g_bs, fw_sum,
             cw_sum):
        def total(ref):
            t = ref[0]
            for k in range(1, NDEV):
                t = t + ref[k]
            return t

        a = total(a_ref)
        g_b_ada[...] = jnp.concatenate([a[k:k + 1, :] for k in range(6)], axis=1)
        g_n1[...] = a[6:7, :]
        g_mog[...] = a[7:8, :]
        g_n2[...] = a[8:9, :]
        g_gf[...] = a[9:10, :].reshape(D)
        loss_cols[...] = a[10:11, :]
        p = total(p_ref)
        for k, ref in zip((CB, CLG, CLB, GLG, GLB), (g_cb, g_clg, g_clb, g_glg, g_glb)):
            ref[...] = p[k:k + 1, :]
        g_ws[0] = total(ws_ref)
        g_bs[0] = jnp.transpose(total(bst_ref))[:NH, :]
        fw_sum[...] = total(fw_ref)
        cw_sum[...] = total(cw_ref)

    vec = lambda n: jax.ShapeDtypeStruct((1, n), F32)
    return pl.pallas_call(
        body, name="sum_small_grads",
        out_shape=[vec(6 * D), vec(D), vec(D), vec(D), jax.ShapeDtypeStruct((D,), F32), vec(D),
                   vec(DC), vec(DC), vec(DC), vec(DC), vec(DC),
                   jax.ShapeDtypeStruct((1, NH, CHUNK, CHUNK), F32), jax.ShapeDtypeStruct((1, NH, CHUNK), F32),
                   jax.ShapeDtypeStruct((8, 2 * PFF), F32), jax.ShapeDtypeStruct((32, DC), F32)],
    )(rows_all, p_all, ws_all, bst_all, fw_all, cw_all)


def _adam_small(quads):
    n = len(quads)

    def body(*refs):
        ins, outs = refs[:4 * n], refs[4 * n:]
        for q in range(n):
            w, g, m, v = (r[...] for r in ins[4 * q:4 * q + 4])
            outs[3 * q][...], outs[3 * q + 1][...], outs[3 * q + 2][...] = _adam_math(w, g, m, v)

    flat = [a for q in quads for a in q]
    outs = pl.pallas_call(
        body, name="adam_small",
        out_shape=[jax.ShapeDtypeStruct(q[0].shape, F32) for q in quads for _ in range(3)],
    )(*flat)
    return [tuple(outs[3 * q:3 * q + 3]) for q in range(n)]


def kernel(x, c, w_ada, b_ada, norm1_gain, w_in, conv_dw_w, conv_dw_b, conv_ln_g, conv_ln_b, gm_ln_g, gm_ln_b, gm_ws, gm_bs, mix_out_gain, w_out, norm2_gain, w_up, ffn_dw_w, ffn_dw_b, w_down, final_gain, loss_target, m_w_ada, m_b_ada, m_norm1_gain, m_w_in, m_conv_dw_w, m_conv_dw_b, m_conv_ln_g, m_conv_ln_b, m_gm_ln_g, m_gm_ln_b, m_gm_ws, m_gm_bs, m_mix_out_gain, m_w_out, m_norm2_gain, m_w_up, m_ffn_dw_w, m_ffn_dw_b, m_w_down, m_final_gain, v_w_ada, v_b_ada, v_norm1_gain, v_w_in, v_conv_dw_w, v_conv_dw_b, v_conv_ln_g, v_conv_ln_b, v_gm_ln_g, v_gm_ln_b, v_gm_ws, v_gm_bs, v_mix_out_gain, v_w_out, v_norm2_gain, v_w_up, v_ffn_dw_w, v_ffn_dw_b, v_w_down, v_final_gain):
    s = x.shape[1]
    ax, ay, ac = _place()
    me = 4 * ax + 2 * ay + ac
    n_ada = w_ada.shape[2]
    n_cw = conv_dw_w.shape[2]
    x2d = x[0]
    target = loss_target[0]
    pad_sh = lambda a: jnp.pad(a, [(0, 0)] * (a.ndim - 1) + [(0, PSH - NSH)])

    gather_in, token_a = _start_gather("gather_in_out", [w_in[0].astype(BF16), w_out[0].astype(BF16)], me)

    c_all, cw_all, fw_all = _all_gather("gather_small", [c + token_a[0, 0], conv_dw_w[0], ffn_dw_w[0]])
    conv_w = jnp.pad(jnp.transpose(cw_all, (1, 0, 2)).reshape(KC, DC), ((0, 32 - KC), (0, 0)))
    ffn_w = jnp.transpose(pad_sh(fw_all), (1, 0, 2)).reshape(KF, 2 * PFF)
    ffn_b = pad_sh(ffn_dw_b.reshape(NDEV, NSH)).reshape(1, 2 * PFF)
    ffn_wb = jnp.concatenate([ffn_w, ffn_b, jnp.zeros((8 - KF - 1, 2 * PFF), F32)], axis=0)

    b_cols = lax.dynamic_slice(b_ada, (0, me * n_ada), (1, n_ada))
    (mod_all,) = _all_gather("gather_mod", [_mod_part(c_all, w_ada, b_cols)])
    up_t = lambda a: jnp.swapaxes(a, 1, 2)
    w_up_shard = jnp.pad(up_t(w_up)[0].astype(BF16), ((0, PSH - NSH), (0, 0)))
    shards, mod_all = lax.optimization_barrier(((w_up_shard, w_down[0].astype(BF16)), mod_all))
    gather_ffn, token_c = _start_gather("gather_up_down", list(shards), me)
    mod = lax.dynamic_index_in_dim(mod_all, me, axis=1, keepdims=False).reshape(6, D)
    sh1, sc1, gt1, sh2, sc2, gt2 = [mod[k:k + 1] for k in range(6)]
    vecs = jnp.concatenate([norm1_gain, sh1, sc1, gt1, norm2_gain, sh2, sc2, gt2, mix_out_gain,
                            final_gain.reshape(1, D), jnp.zeros((6, D), F32)], axis=0)
    vecs = vecs + token_c[0, 0]
    v512 = jnp.concatenate([conv_dw_b, conv_ln_g, conv_ln_b, gm_ln_g, gm_ln_b, jnp.zeros((3, DC), F32)], axis=0)
    bs_exp = jnp.repeat(jnp.transpose(gm_bs[0]), HD, axis=1)
    gm_ws_t = jnp.swapaxes(gm_ws[0], 1, 2)

    tm_in = min(512, s)
    tm = min(256, s)
    w_in_g, w_out_g = _finish_gather("gather_in_out", gather_in, vecs)
    w_out_b = w_out_g.reshape(D, D)
    z, a0, h1_t = _fwd_in(x2d, vecs, w_in_g, tm_in)
    a1, sp, x2, o1, h2 = _fwd_mid(a0, z, x2d, vecs, v512, conv_w, gm_ws, bs_exp, w_out_b, tm)
    w_up_t, w_down_g = _finish_gather("gather_up_down", gather_ffn, h2)
    w_down_p = jnp.pad(w_down_g.reshape(4, NSH, D), ((0, 0), (0, PSH - NSH), (0, 0)))
    up_pre, vg, dx3, acc_f = _fwd_ffn(h2, x2, target, vecs, ffn_wb, w_up_t, w_down_p, tm)

    core = ac.reshape(1).astype(jnp.int32)
    mychip = 2 * ax + ay

    def to_pairs(named):
        g4s = [g.reshape((4, 2) + g.shape[1:]) for _, g in named]
        from_sibling = _sibling_swap("rs_sibling_" + named[0][0], g4s)
        return [_pair_add("rs_pair_add_" + t[0], g4, rv, core) for t, g4, rv in zip(named, g4s, from_sibling)]

    dh2, dw_up, dw_down, acc_fw = _bwd_ffn(dx3, up_pre, vg, h2, vecs, ffn_wb, w_up_t, w_down_p, tm_in)
    acc_fw = jnp.transpose(acc_fw, (2, 0, 1, 3)).reshape(8, 2 * PFF)
    exchange_ffn, token_x = _start_exchange("rs_chips_ffn", to_pairs(
        [("w_up", dw_up), ("w_down", dw_down.reshape(NDEV, w_down.shape[1], D))]), mychip)
    gx, dz, y_t, do1, acc_m, acc_p, dcw, dws, dbs_t = _bwd_mid(
        dh2, dx3, x2, x2d, o1, z, a0, a1, sp, vecs + token_x[0, 0], v512, conv_w, gm_ws[0], gm_ws_t, w_out_b, w_in_g, tm)
    rows = jnp.concatenate([acc_m[1:3], acc_m[0:1], acc_m[5:7], acc_f[2:3], acc_m[3:5], acc_m[7:8], acc_f[0:2],
                            jnp.zeros((5, D), F32)], axis=0)
    small_gather, token_s = _start_gather("gather_small_grads", [rows, acc_p, dws, dbs_t, acc_fw, dcw], me)
    dw_in = _mm_all_slots("dw_in", h1_t, dz, w_in.shape[2], min(1024, s), token_s)
    small_pass, token_p = _pass_gather("gather_small_grads", small_gather, dw_in)
    dw_out = _mm_all_slots("dw_out", y_t, do1, D, min(2048, s), token_p).reshape(NDEV, w_out.shape[1], D)
    exchange_mix, token_m = _start_exchange("rs_chips_mix", to_pairs([("w_in", dw_in), ("w_out", dw_out)]), mychip)

    rows_all, p_all, ws_all, bst_all, fwg_all, cwg_all = _end_gather("gather_small_grads", small_pass, token_m)
    (g_b_ada, g_n1, g_mog, g_n2, g_gf, loss_cols, g_cb, g_clg, g_clb, g_glg, g_glb, g_ws, g_bs, fw_sum,
     cw_sum) = _sum_small(rows_all, p_all, ws_all, bst_all, fwg_all, cwg_all)
    loss = jnp.sum(loss_cols)
    g_fb = fw_sum[3].reshape(NDEV, PSH)[:, :NSH].reshape(ffn_dw_b.shape)
    g_fw = lax.dynamic_index_in_dim(fw_sum[:KF].reshape(KF, NDEV, PSH), me, axis=1, keepdims=False)[:, :NSH]
    g_fw = g_fw.reshape(ffn_dw_w.shape)
    g_cw = lax.dynamic_slice(cw_sum, (0, me * n_cw), (KC, n_cw)).reshape(conv_dw_w.shape)
    small = [
        (b_ada, g_b_ada, m_b_ada, v_b_ada), (norm1_gain, g_n1, m_norm1_gain, v_norm1_gain),
        (conv_dw_w, g_cw, m_conv_dw_w, v_conv_dw_w), (conv_dw_b, g_cb, m_conv_dw_b, v_conv_dw_b),
        (conv_ln_g, g_clg, m_conv_ln_g, v_conv_ln_g), (conv_ln_b, g_clb, m_conv_ln_b, v_conv_ln_b),
        (gm_ln_g, g_glg, m_gm_ln_g, v_gm_ln_g), (gm_ln_b, g_glb, m_gm_ln_b, v_gm_ln_b),
        (gm_ws, g_ws, m_gm_ws, v_gm_ws), (gm_bs, g_bs, m_gm_bs, v_gm_bs),
        (mix_out_gain, g_mog, m_mix_out_gain, v_mix_out_gain), (norm2_gain, g_n2, m_norm2_gain, v_norm2_gain),
        (ffn_dw_w, g_fw, m_ffn_dw_w, v_ffn_dw_w), (ffn_dw_b, g_fb, m_ffn_dw_b, v_ffn_dw_b),
        (final_gain, g_gf, m_final_gain, v_final_gain)]
    small_out = _adam_small(small)
    res = {}
    for name, q, o in zip(("b_ada", "norm1_gain", "conv_dw_w", "conv_dw_b", "conv_ln_g", "conv_ln_b", "gm_ln_g",
                           "gm_ln_b", "gm_ws", "gm_bs", "mix_out_gain", "norm2_gain", "ffn_dw_w", "ffn_dw_b",
                           "final_gain"), small, small_out):
        res[name] = (q[1],) + o

    dmod_all = rows_all[:, :6].reshape(NDEV, 6 * D)
    dm_cols = lax.dynamic_slice(dmod_all, (0, me * n_ada), (NDEV, n_ada))
    g_ada = _ada_grad(jnp.transpose(c_all[:, 0, :]), dm_cols)
    res["w_ada"] = (g_ada,) + tuple(_adam3("adam_ada", w_ada, g_ada, m_w_ada, v_w_ada))

    big = [("w_up", up_t(w_up), up_t(m_w_up), up_t(v_w_up)), ("w_down", w_down, m_w_down, v_w_down),
           ("w_in", w_in, m_w_in, v_w_in), ("w_out", w_out, m_w_out, v_w_out)]
    from_chips = list(_finish_exchange("rs_chips_ffn", exchange_ffn, res["w_ada"][1]))
    for t, parts in zip(big[:2], from_chips):
        res[t[0]] = tuple(_sum_adam("rs_sum_adam_" + t[0], parts, t[1], t[2], t[3]))
    res["w_up"] = tuple(up_t(a) for a in res["w_up"])
    from_chips = list(_finish_exchange("rs_chips_mix", exchange_mix, res["w_down"][1]))
    for t, parts in zip(big[2:], from_chips):
        res[t[0]] = tuple(_sum_adam("rs_sum_adam_" + t[0], parts, t[1], t[2], t[3]))

    order = ("w_ada", "b_ada", "norm1_gain", "w_in", "conv_dw_w", "conv_dw_b", "conv_ln_g", "conv_ln_b", "gm_ln_g",
             "gm_ln_b", "gm_ws", "gm_bs", "mix_out_gain", "w_out", "norm2_gain", "w_up", "ffn_dw_w", "ffn_dw_b",
             "w_down", "final_gain")
    return (loss, gx.reshape(x.shape), *[res[n][0] for n in order], *[res[n][1] for n in order],
            *[res[n][2] for n in order], *[res[n][3] for n in order])
```

```python
import functools

import jax
import jax.numpy as jnp
from jax import lax
from jax.experimental import pallas as pl
from jax.experimental.pallas import tpu as pltpu

F32 = jnp.float32
BF16 = jnp.bfloat16
NDEV = 8
D = 1024
DC = 512
DFF = 2816
NSH = 704
PSH = 768
PFF = 4 * PSH
KC = 31
KF = 3
CHUNK = 128
NH = 8
HD = 64
HALO_C = 32
HALO_F = 8
LANES = 128
SUB = 8
VROWS = 16
CW_ROWS = 32
TILE_BIG = 512
ROW_BLOCK_BYTES = 2 << 20
TILE = 256
TK_IN = 1024
TK_OUT = 2048
RMS_EPS = 1e-6
LN_EPS = 1e-5
ADAM_LR = 0.001
ADAM_B1 = 0.9
ADAM_B2 = 0.999
ADAM_EPS = 1e-08
ADAM_WD = 0.01
ADAM_STEP = 10
GELU_K = 0.7978845608028654
GELU_C = 0.044715

MESH = pl.DeviceIdType.MESH
ANY = pl.BlockSpec(memory_space=pl.ANY)

G1, SH1, SC1, GT1, G2, SH2, SC2, GT2, MOG, GF = range(10)
CB, CLG, CLB, GLG, GLB = range(5)


def _full(shape):
    return pl.BlockSpec(shape, lambda *_: (0,) * len(shape))


def _arb(n=1):
    return pltpu.CompilerParams(dimension_semantics=("arbitrary",) * n)


def _row(ref, r):
    return ref[pl.ds(r, 1), :]


def _colsum(v):
    return jnp.sum(v, axis=0, keepdims=True)


def _rowmean(v):
    return jnp.mean(v, axis=-1, keepdims=True)


def _rms(x):
    r = lax.rsqrt(_rowmean(x * x) + RMS_EPS)
    return x * r, r


def _rms_bwd(dxn, xn, r):
    return r * (dxn - xn * _rowmean(dxn * xn))


def _ln(x):
    mu = _rowmean(x)
    xc = x - mu
    rstd = lax.rsqrt(_rowmean(xc * xc) + LN_EPS)
    return xc * rstd, rstd


def _ln_bwd(dxh, xhat, rstd):
    return rstd * (dxh - _rowmean(dxh) - xhat * _rowmean(dxh * xhat))


def _sigmoid(x):
    return 0.5 * jnp.tanh(0.5 * x) + 0.5


def _gelu(x):
    t = jnp.tanh(GELU_K * (x + GELU_C * x * x * x))
    return 0.5 * x * (1.0 + t), t


def _gelu_grad(x, t):
    return 0.5 * (1.0 + t) + 0.5 * x * (1.0 - t * t) * (GELU_K * (1.0 + 3.0 * GELU_C * x * x))


def _dot(a, b):
    return jnp.dot(a, b, preferred_element_type=F32)


def _dot_nt(a, b):
    return lax.dot_general(a, b, (((1,), (1,)), ((), ())), preferred_element_type=F32)


def _shift_up(e, s):
    n = e.shape[0]
    return pltpu.roll(e, (n - s) % n, 0)


def _place():
    return lax.axis_index("x"), lax.axis_index("y"), lax.axis_index("c")


def _all_gather(name, xs):
    n = len(xs)

    def body(*refs):
        x_refs, out_refs = refs[:n], refs[n:2 * n]
        send_sems, recv_sems, local_sems = refs[2 * n:]
        x, y, c = _place()
        me, sibling = (x, y, c), (x, y, 1 - c)
        chips = [(1 - x, y), (x, 1 - y), (1 - x, 1 - y)]

        def copy(a, k, block, to, own=False):
            px, py, pc = block
            slot = out_refs[a].at[4 * px + 2 * py + pc]
            return pltpu.make_async_remote_copy(
                src_ref=x_refs[a] if own else slot, dst_ref=slot,
                send_sem=send_sems.at[7 * a + k], recv_sem=recv_sems.at[7 * a + k], device_id=to, device_id_type=MESH)

        mine = [pltpu.make_async_copy(x_refs[a], out_refs[a].at[4 * x + 2 * y + c], local_sems.at[a]) for a in range(n)]
        for cp in mine:
            cp.start()
        first = []
        for a in range(n):
            first.append(copy(a, 0, me, sibling, own=True))
            first += [copy(a, 1 + j, me, (*chip, c), own=True) for j, chip in enumerate(chips)]
        for cp in first:
            cp.start()
        passed = []
        for j, chip in enumerate(chips):
            for a in range(n):
                copy(a, 1 + j, (*chip, c), me).wait_recv()
                cp = copy(a, 4 + j, (*chip, c), sibling)
                cp.start()
                passed.append(cp)
        for a in range(n):
            copy(a, 0, sibling, me).wait_recv()
            for j, chip in enumerate(chips):
                copy(a, 4 + j, (*chip, 1 - c), me).wait_recv()
        for cp in first + passed:
            cp.wait_send()
        for cp in mine:
            cp.wait()

    return pl.pallas_call(
        body, name=name, out_shape=[jax.ShapeDtypeStruct((NDEV,) + a.shape, a.dtype) for a in xs],
        in_specs=[ANY] * n, out_specs=[ANY] * n,
        scratch_shapes=[pltpu.SemaphoreType.DMA((7 * n,)), pltpu.SemaphoreType.DMA((7 * n,)),
                        pltpu.SemaphoreType.DMA((n,))],
    )(*xs)


def _sibling_swap(name, g4s):
    n = len(g4s)

    def body(*refs):
        g_refs, out_refs = refs[:n], refs[n:2 * n]
        send_sems, recv_sems = refs[2 * n:]
        x, y, c = _place()
        cps = [pltpu.make_async_remote_copy(
            src_ref=g_refs[a].at[k, 1 - c], dst_ref=out_refs[a].at[k],
            send_sem=send_sems.at[4 * a + k], recv_sem=recv_sems.at[4 * a + k],
            device_id=(x, y, 1 - c), device_id_type=MESH) for a in range(n) for k in range(4)]
        for cp in cps:
            cp.start()
        for cp in cps:
            cp.wait()

    return pl.pallas_call(
        body, name=name, out_shape=[jax.ShapeDtypeStruct((4,) + g.shape[2:], g.dtype) for g in g4s],
        in_specs=[ANY] * n, out_specs=[ANY] * n,
        scratch_shapes=[pltpu.SemaphoreType.DMA((4 * n,)), pltpu.SemaphoreType.DMA((4 * n,))],
    )(*g4s)


HBM = pl.BlockSpec(memory_space=pltpu.HBM)
SEM = pl.BlockSpec(memory_space=pltpu.SEMAPHORE)
EFFECT = pltpu.SideEffectType.DATAFLOW_SIDE_EFFECTING


def _in_hbm(a):
    return pltpu.with_memory_space_constraint(a, pltpu.HBM)


def _split_start(name, bufs, copies):
    n = len(bufs)

    def body(*refs):
        for cp in copies(refs[:n], refs[n], refs[n + 1]):
            cp.start()
        refs[-1][...] = jnp.zeros_like(refs[-1])

    out = pl.pallas_call(
        body, name=name,
        out_shape=(pltpu.SemaphoreType.DMA((copies.count,)), pltpu.SemaphoreType.DMA((copies.count,)),
                   *[pltpu.HBM(a.shape, a.dtype) for a in bufs], jax.ShapeDtypeStruct((SUB, LANES), F32)),
        in_specs=[HBM] * n, out_specs=(SEM, SEM, *[HBM] * n, pl.BlockSpec(memory_space=pltpu.VMEM)),
        input_output_aliases={i: 2 + i for i in range(n)},
        compiler_params=pltpu.CompilerParams(has_side_effects=EFFECT),
    )(*[_in_hbm(a) for a in bufs])
    return (out[0], out[1], list(out[2:2 + n])), out[-1]


def _split_wait(name, handle, copies, after):
    send_sems, recv_sems, bufs = handle
    n = len(bufs)

    def body(*refs):
        for cp in copies(refs[:n], refs[n], refs[n + 1]):
            cp.wait_send()
            cp.wait_recv()

    out = pl.pallas_call(
        body, name=name, out_shape=tuple(pltpu.HBM(a.shape, a.dtype) for a in bufs),
        in_specs=[HBM] * n + [SEM, SEM, pl.BlockSpec(memory_space=pl.ANY)], out_specs=tuple([HBM] * n),
        input_output_aliases={i: i for i in range(n)},
        compiler_params=pltpu.CompilerParams(has_side_effects=EFFECT),
    )(*bufs, send_sems, recv_sems, after)
    return list(out)


class _GatherFirstCopies:
    def __init__(self, n):
        self.n, self.count = n, 4 * n

    def __call__(self, refs, send_sems, recv_sems):
        x, y, c = _place()
        peers = [(x, y, 1 - c), (1 - x, y, c), (x, 1 - y, c), (1 - x, 1 - y, c)]
        return [pltpu.make_async_remote_copy(
            src_ref=refs[a], dst_ref=refs[self.n + a].at[4 * x + 2 * y + c],
            send_sem=send_sems.at[4 * a + k], recv_sem=recv_sems.at[4 * a + k], device_id=peer, device_id_type=MESH)
            for a in range(self.n) for k, peer in enumerate(peers)]


class _GatherPassCopies:
    def __init__(self, n):
        self.n, self.count = n, 3 * n

    def __call__(self, refs, send_sems, recv_sems):
        x, y, c = _place()
        cps = []
        for a in range(self.n):
            for j, (px, py) in enumerate([(1 - x, y), (x, 1 - y), (1 - x, 1 - y)]):
                slot = refs[a].at[4 * px + 2 * py + c]
                cps.append(pltpu.make_async_remote_copy(
                    src_ref=slot, dst_ref=slot, send_sem=send_sems.at[3 * a + j], recv_sem=recv_sems.at[3 * a + j],
                    device_id=(x, y, 1 - c), device_id_type=MESH))
        return cps


class _ExchangeCopies:
    def __init__(self, n):
        self.n, self.count = n, 3 * n

    def __call__(self, refs, send_sems, recv_sems):
        x, y, c = _place()
        cps = []
        for a in range(self.n):
            for j, (px, py) in enumerate([(1 - x, y), (x, 1 - y), (1 - x, 1 - y)]):
                cps.append(pltpu.make_async_remote_copy(
                    src_ref=refs[a].at[2 * px + py], dst_ref=refs[self.n + a].at[2 * x + y],
                    send_sem=send_sems.at[3 * a + j], recv_sem=recv_sems.at[3 * a + j],
                    device_id=(px, py, c), device_id_type=MESH))
        return cps


class _SwapCopies:
    def __init__(self, n):
        self.n, self.count = n, 4 * n

    def __call__(self, refs, send_sems, recv_sems):
        x, y, c = _place()
        return [pltpu.make_async_remote_copy(
            src_ref=refs[a].at[k, 1 - c], dst_ref=refs[self.n + a].at[k],
            send_sem=send_sems.at[4 * a + k], recv_sem=recv_sems.at[4 * a + k],
            device_id=(x, y, 1 - c), device_id_type=MESH) for a in range(self.n) for k in range(4)]


def _start_swap(tag, g4s):
    lands = [lax.empty((4,) + g.shape[2:], g.dtype) for g in g4s]
    return _split_start(tag + "_start", list(g4s) + lands, _SwapCopies(len(g4s)))


def _finish_swap(tag, handle, after):
    n = len(handle[2]) // 2
    return _split_wait(tag + "_wait", handle, _SwapCopies(n), after)[n:]


def _own_slot(nslot, src, index):
    land = lax.empty((nslot,) + src.shape, src.dtype)
    return lax.dynamic_update_slice(land, src[None], (index,) + (0,) * src.ndim)


def _start_gather(tag, xs, me):
    lands = [_own_slot(NDEV, a, me) for a in xs]
    return _split_start(tag + "_start", list(xs) + lands, _GatherFirstCopies(len(xs)))


def _pass_gather(tag, handle, after):
    n = len(handle[2]) // 2
    lands = _split_wait(tag + "_wait", handle, _GatherFirstCopies(n), after)[n:]
    return _split_start(tag + "_pass", lands, _GatherPassCopies(n))


def _end_gather(tag, passing, after):
    return _split_wait(tag + "_pass_wait", passing, _GatherPassCopies(len(passing[2])), after)


def _finish_gather(tag, handle, after):
    passing, token = _pass_gather(tag, handle, after)
    return _end_gather(tag, passing, token)


def _start_exchange(tag, hs, mychip):
    lands = [_own_slot(4, lax.dynamic_index_in_dim(h, mychip, 0, keepdims=False), mychip) for h in hs]
    return _split_start(tag + "_start", list(hs) + lands, _ExchangeCopies(len(hs)))


def _finish_exchange(tag, handle, after):
    n = len(handle[2]) // 2
    return _split_wait(tag + "_wait", handle, _ExchangeCopies(n), after)[n:]


def _mod_part(c_all, w_ada, b_cols):
    ncol = w_ada.shape[2]

    def body(c_ref, w_ref, b_ref, o_ref):
        cv = c_ref[:, 0, :]
        ca = cv * _sigmoid(cv)
        o_ref[...] = _dot(ca.astype(BF16), w_ref[0].astype(BF16)) + b_ref[...]

    return pl.pallas_call(body, name="mod_part", out_shape=jax.ShapeDtypeStruct((NDEV, ncol), F32))(
        c_all, w_ada, b_cols)


def _ada_grad(c_all_t, dmod_cols):
    ncol = dmod_cols.shape[1]

    def body(ct_ref, dm_ref, o_ref):
        ct = ct_ref[...]
        ca = ct * _sigmoid(ct)
        acc = jnp.zeros((D, ncol), F32)
        for b in range(NDEV):
            acc = acc + ca[:, b:b + 1] * dm_ref[pl.ds(b, 1), :]
        o_ref[0] = acc

    return pl.pallas_call(body, name="ada_grad", out_shape=jax.ShapeDtypeStruct((1, D, ncol), F32))(
        c_all_t, dmod_cols)


def _fwd_in(x2d, vecs, w_in_g, tm):
    s = x2d.shape[0]
    nc = w_in_g.shape[2]

    def body(x_ref, v_ref, w_ref, z_ref, a0_ref, h1t_ref):
        xn, _ = _rms(x_ref[...])
        h = (xn * _row(v_ref, G1)) * (1.0 + _row(v_ref, SC1)) + _row(v_ref, SH1)
        hb = h.astype(BF16)
        h1t_ref[...] = hb.T
        for d in range(NDEV):
            z_ref[:, pl.ds(d * nc, nc)] = _dot(hb, w_ref[d])
        a0_ref[...] = z_ref[:, :DC] * _sigmoid(z_ref[:, DC:2 * DC])

    return pl.pallas_call(
        body, name="fwd_in", grid=(s // tm,),
        in_specs=[pl.BlockSpec((tm, D), lambda i: (i, 0)), _full((VROWS, D)), _full((NDEV, D, nc))],
        out_specs=[pl.BlockSpec((tm, 4 * DC), lambda i: (i, 0)), pl.BlockSpec((tm, DC), lambda i: (i, 0)),
                   pl.BlockSpec((D, tm), lambda i: (0, i))],
        out_shape=[jax.ShapeDtypeStruct((s, 4 * DC), F32), jax.ShapeDtypeStruct((s, DC), F32),
                   jax.ShapeDtypeStruct((D, s), BF16)],
        compiler_params=_arb(),
    )(x2d, vecs, w_in_g)


def _causal_mask(lower):
    r = lax.broadcasted_iota(jnp.int32, (CHUNK, CHUNK), 0)
    c = lax.broadcasted_iota(jnp.int32, (CHUNK, CHUNK), 1)
    return (r >= c) if lower else (r <= c)


def _first_head_lanes():
    return lax.broadcasted_iota(jnp.int32, (CHUNK, CHUNK), 1) < HD


def _fwd_mid(a0, z, x2d, vecs, v512, conv_w, gm_ws, bs_exp, w_out_b, tm):
    s = x2d.shape[0]
    hb = tm // HALO_C

    def body(a0_ref, halo_ref, zg_ref, x_ref, v_ref, p_ref, cw_ref, ws_ref, bs_ref, wo_ref,
             a1_ref, sp_ref, x2_ref, o1_ref, h2_ref):
        i = pl.program_id(0)
        for c0 in range(0, DC, LANES):
            cols = pl.ds(c0, LANES)
            halo = halo_ref[:, cols]
            e = jnp.concatenate([jnp.where(i > 0, halo, jnp.zeros_like(halo)), a0_ref[:, cols]], axis=0)
            acc = jnp.broadcast_to(p_ref[pl.ds(CB, 1), cols], (tm, LANES))
            for k in range(KC):
                acc = acc + _shift_up(e, HALO_C - (KC - 1) + k)[:tm, :] * cw_ref[pl.ds(k, 1), cols]
            a1_ref[:, cols] = acc
        xh, _ = _ln(a1_ref[...])
        a2 = xh * _row(p_ref, CLG) + _row(p_ref, CLB)
        a3 = a2 * _sigmoid(a2)
        gu, _ = _gelu(zg_ref[:, :DC])
        gvg, _ = _gelu(zg_ref[:, DC:])
        vh, _ = _ln(gvg)
        gvn = (vh * _row(p_ref, GLG) + _row(p_ref, GLB)).astype(BF16)
        low = _causal_mask(True)
        first = _first_head_lanes()
        wm = [jnp.where(low, ws_ref[0, h], 0.0).astype(BF16) for h in range(NH)]
        for n in range(tm // CHUNK):
            for p in range(NH // 2):
                v = gvn[n * CHUNK:(n + 1) * CHUNK, p * CHUNK:(p + 1) * CHUNK]
                blk = jnp.where(first, _dot(wm[2 * p], v), _dot(wm[2 * p + 1], v))
                sp_ref[pl.ds(n * CHUNK, CHUNK), pl.ds(p * CHUNK, CHUNK)] = blk + bs_ref[:, pl.ds(p * CHUNK, CHUNK)]
        g = gu * sp_ref[...]
        an, _ = _rms(a3)
        gn, _ = _rms(g)
        mog = _row(v_ref, MOG)
        y = jnp.concatenate([an * mog[:, :DC], gn * mog[:, DC:]], axis=1).astype(BF16)
        o1 = _dot(y, wo_ref[...])
        o1_ref[...] = o1
        x2 = x_ref[...] + _row(v_ref, GT1) * o1
        x2_ref[...] = x2
        xn2, _ = _rms(x2)
        h2 = (xn2 * _row(v_ref, G2)) * (1.0 + _row(v_ref, SC2)) + _row(v_ref, SH2)
        h2_ref[...] = h2.astype(BF16)

    tile = lambda w: pl.BlockSpec((tm, w), lambda i: (i, 0))
    return pl.pallas_call(
        body, name="fwd_mid", grid=(s // tm,),
        in_specs=[tile(DC), pl.BlockSpec((HALO_C, DC), lambda i: (jnp.maximum(i * hb - 1, 0), 0)),
                  pl.BlockSpec((tm, 2 * DC), lambda i: (i, 1)), tile(D), _full((VROWS, D)), _full((SUB, DC)),
                  _full((CW_ROWS, DC)), _full((1, NH, CHUNK, CHUNK)), _full((CHUNK, DC)), _full((D, D))],
        out_specs=[tile(DC), tile(DC), tile(D), tile(D), tile(D)],
        out_shape=[jax.ShapeDtypeStruct((s, DC), F32), jax.ShapeDtypeStruct((s, DC), F32),
                   jax.ShapeDtypeStruct((s, D), F32), jax.ShapeDtypeStruct((s, D), F32),
                   jax.ShapeDtypeStruct((s, D), BF16)],
        compiler_params=_arb(),
    )(a0, a0, z, x2d, vecs, v512, conv_w, gm_ws, bs_exp, w_out_b)


def _ffn_conv(fw_ref, cols, p2, p1, pre):
    return (fw_ref[pl.ds(3, 1), cols] + fw_ref[pl.ds(0, 1), cols] * p2
            + fw_ref[pl.ds(1, 1), cols] * p1 + fw_ref[pl.ds(2, 1), cols] * pre)


def _fwd_ffn(h2, x2, target, vecs, ffn_wb, w_up_t, w_down_p, tm):
    s = x2.shape[0]

    def body(h2_ref, x2_ref, t_ref, v_ref, fw_ref, wu_hbm, wd_hbm,
             up_ref, vg_ref, dx3_ref, acc_ref, wu, wd, carry, stage):
        i = pl.program_id(0)

        @pl.when(i == 0)
        def _():
            for sh in range(NDEV):
                pltpu.sync_copy(wu_hbm.at[sh], stage)
                wu[sh] = stage[...].T
            pltpu.sync_copy(wd_hbm, wd)
            carry[...] = jnp.zeros_like(carry)
            acc_ref[...] = jnp.zeros_like(acc_ref)

        h2v = h2_ref[...]
        o2 = jnp.zeros((tm, D), F32)
        for j in range(4):
            conv = []
            for sh in (j, 4 + j):
                cols = pl.ds(sh * PSH, PSH)
                pre = _dot(h2v, wu[sh])
                up_ref[:, cols] = pre.astype(BF16)
                e = jnp.concatenate([carry[:, cols], pre], axis=0)
                carry[:, cols] = pre[tm - HALO_F:, :]
                conv.append(_ffn_conv(fw_ref, cols, pltpu.roll(e, 2, 0)[HALO_F:, :],
                                      pltpu.roll(e, 1, 0)[HALO_F:, :], pre))
            val, gate = conv
            vg_ref[:, pl.ds(j * PSH, PSH)] = val.astype(BF16)
            vg_ref[:, pl.ds((4 + j) * PSH, PSH)] = gate.astype(BF16)
            f = ((gate * _sigmoid(gate)) * val).astype(BF16)
            o2 = o2 + _dot(f, wd[j])
        x3 = x2_ref[...] + _row(v_ref, GT2) * o2
        xn3, r3 = _rms(x3)
        gf = _row(v_ref, GF)
        diff = xn3 * gf - t_ref[...]
        acc_ref[pl.ds(1, 1), :] += _colsum(diff * diff) * (0.5 / D)
        dout = diff * (1.0 / D)
        acc_ref[pl.ds(0, 1), :] += _colsum(dout * xn3)
        dx3 = _rms_bwd(dout * gf, xn3, r3)
        dx3_ref[...] = dx3
        acc_ref[pl.ds(2, 1), :] += _colsum(dx3 * o2)

    tile = lambda w: pl.BlockSpec((tm, w), lambda i: (i, 0))
    return pl.pallas_call(
        body, name="fwd_ffn", grid=(s // tm,),
        in_specs=[tile(D), tile(D), tile(D), _full((VROWS, D)), _full((SUB, 2 * PFF)), ANY, ANY],
        out_specs=[tile(2 * PFF), tile(2 * PFF), tile(D), _full((SUB, D))],
        out_shape=[jax.ShapeDtypeStruct((s, 2 * PFF), BF16), jax.ShapeDtypeStruct((s, 2 * PFF), BF16),
                   jax.ShapeDtypeStruct((s, D), F32), jax.ShapeDtypeStruct((SUB, D), F32)],
        scratch_shapes=[pltpu.VMEM((NDEV, D, PSH), BF16), pltpu.VMEM((4, PSH, D), BF16),
                        pltpu.VMEM((HALO_F, 2 * PFF), F32), pltpu.VMEM((PSH, D), BF16)],
        compiler_params=_arb(),
    )(h2, x2, target, vecs, ffn_wb, w_up_t, w_down_p)


def _bwd_ffn(dx3, up_pre, vg, h2, vecs, ffn_wb, w_up_t, w_down_p, tm):
    s = dx3.shape[0]
    nt = s // tm

    def body(dx3_ref, up_ref, upg_ref, val_ref, gate_ref, h2_ref, v_ref, fw_ref, fwg_ref, wu_ref, wug_ref, wd_ref,
             dh2_ref, dwu_ref, dwd_ref, accf_ref, carry):
        i = pl.program_id(1)

        @pl.when(i == 0)
        def _():
            for ref in (carry, dwu_ref, dwd_ref, accf_ref):
                ref[...] = jnp.zeros_like(ref)

        do2 = (dx3_ref[...] * _row(v_ref, GT2)).astype(BF16)
        df = _dot_nt(do2, wd_ref[...])
        val = val_ref[...].astype(F32)
        gate = gate_ref[...].astype(F32)
        sg = _sigmoid(gate)
        sl = gate * sg
        f_t = (sl * val).astype(BF16).T
        dwd_ref[...] += _dot(f_t, do2)[:NSH, :]
        dups = (df * sl, df * val * (sg * (1.0 + gate * (1.0 - sg))))
        h2v = h2_ref[...]
        dh2 = jnp.zeros((tm, D), F32)
        for half, (dup, pre_ref, w_ref, wmat_ref) in enumerate(
                zip(dups, (up_ref, upg_ref), (fw_ref, fwg_ref), (wu_ref, wug_ref))):
            cols = pl.ds(half * PSH, PSH)
            e = jnp.concatenate([dup, carry[:, cols]], axis=0)
            carry[:, cols] = dup[:HALO_F, :]
            d1 = _shift_up(e, 1)[:tm, :]
            d2 = _shift_up(e, 2)[:tm, :]
            pre = pre_ref[...].astype(F32)
            accf_ref[half, pl.ds(3, 1), :] += _colsum(dup)
            accf_ref[half, pl.ds(0, 1), :] += _colsum(d2 * pre)
            accf_ref[half, pl.ds(1, 1), :] += _colsum(d1 * pre)
            accf_ref[half, pl.ds(2, 1), :] += _colsum(dup * pre)
            dpre = (_row(w_ref, 0) * d2 + _row(w_ref, 1) * d1 + _row(w_ref, 2) * dup).astype(BF16)
            dwu_ref[half] += _dot(dpre.T, h2v)[:NSH, :]
            dh2 = dh2 + _dot(dpre, wmat_ref[...])
        dh2_ref[...] = dh2

    rev = lambda j, i: nt - 1 - i
    in_specs = [
        pl.BlockSpec((tm, D), lambda j, i: (rev(j, i), 0)),
        pl.BlockSpec((tm, PSH), lambda j, i: (rev(j, i), j)), pl.BlockSpec((tm, PSH), lambda j, i: (rev(j, i), 4 + j)),
        pl.BlockSpec((tm, PSH), lambda j, i: (rev(j, i), j)), pl.BlockSpec((tm, PSH), lambda j, i: (rev(j, i), 4 + j)),
        pl.BlockSpec((tm, D), lambda j, i: (rev(j, i), 0)), _full((VROWS, D)),
        pl.BlockSpec((SUB, PSH), lambda j, i: (0, j)), pl.BlockSpec((SUB, PSH), lambda j, i: (0, 4 + j)),
        pl.BlockSpec((None, PSH, D), lambda j, i: (j, 0, 0)), pl.BlockSpec((None, PSH, D), lambda j, i: (4 + j, 0, 0)),
        pl.BlockSpec((None, PSH, D), lambda j, i: (j, 0, 0))]
    dh2, dw_up, dw_down, accf = pl.pallas_call(
        body, name="bwd_ffn", grid=(4, nt), in_specs=in_specs,
        out_specs=[pl.BlockSpec((None, tm, D), lambda j, i: (j, rev(j, i), 0)),
                   pl.BlockSpec((2, None, NSH, D), lambda j, i: (0, j, 0, 0)),
                   pl.BlockSpec((None, NSH, D), lambda j, i: (j, 0, 0)),
                   pl.BlockSpec((2, None, SUB, PSH), lambda j, i: (0, j, 0, 0))],
        out_shape=[jax.ShapeDtypeStruct((4, s, D), F32), jax.ShapeDtypeStruct((2, 4, NSH, D), F32),
                   jax.ShapeDtypeStruct((4, NSH, D), F32), jax.ShapeDtypeStruct((2, 4, SUB, PSH), F32)],
        scratch_shapes=[pltpu.VMEM((HALO_F, 2 * PSH), F32)],
        compiler_params=_arb(2),
    )(dx3, up_pre, up_pre, vg, vg, h2, vecs, ffn_wb, ffn_wb, w_up_t, w_up_t, w_down_p)
    return dh2, dw_up.reshape(NDEV, NSH, D), dw_down, accf


def _bwd_mid(dh2, dx3, x2, x2d, o1, z, a0, a1, sp, vecs, v512, conv_w, gm_ws, gm_ws_t, w_out_b, w_in_g, tm):
    s = x2d.shape[0]
    nt = s // tm
    hb = tm // HALO_C
    nc = w_in_g.shape[2]

    def body(dh2a_ref, dh2b_ref, dh2c_ref, dh2d_ref, dx3_ref, x2_ref, x_ref, o1_ref, z_ref, a0_ref, halo_ref, a1_ref, sp_ref, v_ref, p_ref, cw_ref,
             ws_ref, wst_ref, wo_ref, wi_ref, gx_ref, dz_ref, yt_ref, do1_ref, acc_ref, accp_ref, dcw_ref, dws_ref,
             dbst_ref, dbs_s, carry, da1_s, dsp_s, dgvn_s):
        i = pl.program_id(0)
        r = nt - 1 - i

        @pl.when(i == 0)
        def _():
            for ref in (carry, dbs_s, acc_ref, accp_ref, dcw_ref, dws_ref, dbst_ref):
                ref[...] = jnp.zeros_like(ref)

        dh2v = (dh2a_ref[...] + dh2b_ref[...]) + (dh2c_ref[...] + dh2d_ref[...])
        xn2, r2 = _rms(x2_ref[...])
        g2 = _row(v_ref, G2)
        sc2 = 1.0 + _row(v_ref, SC2)
        acc_ref[pl.ds(5, 1), :] += _colsum(dh2v)
        acc_ref[pl.ds(6, 1), :] += _colsum(dh2v * (xn2 * g2))
        acc_ref[pl.ds(7, 1), :] += _colsum(dh2v * sc2 * xn2)
        dx2v = dx3_ref[...] + _rms_bwd(dh2v * sc2 * g2, xn2, r2)
        do1 = (dx2v * _row(v_ref, GT1)).astype(BF16)
        do1_ref[...] = do1
        acc_ref[pl.ds(0, 1), :] += _colsum(dx2v * o1_ref[...])
        dy = _dot_nt(do1, wo_ref[...])
        mog = _row(v_ref, MOG)

        xh, rstd = _ln(a1_ref[...])
        clg = _row(p_ref, CLG)
        a2 = xh * clg + _row(p_ref, CLB)
        s2 = _sigmoid(a2)
        a3 = a2 * s2
        an, ra = _rms(a3)
        dya = dy[:, :DC]
        da3 = _rms_bwd(dya * mog[:, :DC], an, ra)
        da2 = da3 * (s2 * (1.0 + a2 * (1.0 - s2)))
        accp_ref[pl.ds(CLB, 1), :] += _colsum(da2)
        accp_ref[pl.ds(CLG, 1), :] += _colsum(da2 * xh)
        da1 = _ln_bwd(da2 * clg, xh, rstd)
        accp_ref[pl.ds(CB, 1), :] += _colsum(da1)
        da1_s[...] = da1
        for c0 in range(0, DC, LANES):
            cols = pl.ds(c0, LANES)
            d = da1_s[:, cols]
            e = jnp.concatenate([d, carry[:, cols]], axis=0)
            carry[:, cols] = d[:HALO_C, :]
            acc = jnp.zeros((tm, LANES), F32)
            for j in range(KC):
                acc = acc + _shift_up(e, j)[:tm, :] * cw_ref[pl.ds(KC - 1 - j, 1), cols]
            sgc = _sigmoid(z_ref[:, pl.ds(DC + c0, LANES)])
            dz_ref[:, cols] = (acc * sgc).astype(BF16)
            dz_ref[:, pl.ds(DC + c0, LANES)] = (acc * z_ref[:, cols] * sgc * (1.0 - sgc)).astype(BF16)
            halo = halo_ref[:, cols]
            ea = jnp.concatenate([jnp.where(r > 0, halo, jnp.zeros_like(halo)), a0_ref[:, cols]], axis=0)
            for k in range(KC):
                dcw_ref[pl.ds(k, 1), cols] += _colsum(d * _shift_up(ea, HALO_C - (KC - 1) + k)[:tm, :])

        gu_pre = z_ref[:, 2 * DC:3 * DC]
        gv_pre = z_ref[:, 3 * DC:]
        gu, tu = _gelu(gu_pre)
        gvg, tv = _gelu(gv_pre)
        vh, vrstd = _ln(gvg)
        glg = _row(p_ref, GLG)
        gvn = (vh * glg + _row(p_ref, GLB)).astype(BF16)
        spv = sp_ref[...]
        g = gu * spv
        gn, rg = _rms(g)
        yt_ref[...] = jnp.concatenate([an * mog[:, :DC], gn * mog[:, DC:]], axis=1).astype(BF16).T
        acc_ref[pl.ds(4, 1), :] += jnp.concatenate([_colsum(dya * an), _colsum(dy[:, DC:] * gn)], axis=1)
        dg = _rms_bwd(dy[:, DC:] * mog[:, DC:], gn, rg)
        dz_ref[:, pl.ds(2 * DC, DC)] = (dg * spv * _gelu_grad(gu_pre, tu)).astype(BF16)
        dsp_s[...] = dg * gu
        upper = _causal_mask(False)
        first = _first_head_lanes()
        wmt = [jnp.where(upper, wst_ref[h], 0.0).astype(BF16) for h in range(NH)]
        for n in range(tm // CHUNK):
            rows = pl.ds(n * CHUNK, CHUNK)
            for p in range(NH // 2):
                cols = pl.ds(p * CHUNK, CHUNK)
                dsp = dsp_s[rows, cols]
                dbs_s[:, cols] += dsp
                da = jnp.where(first, dsp, 0.0).astype(BF16)
                db = jnp.where(first, 0.0, dsp).astype(BF16)
                v = gvn[n * CHUNK:(n + 1) * CHUNK, p * CHUNK:(p + 1) * CHUNK]
                dws_ref[2 * p] += _dot_nt(da, v)
                dws_ref[2 * p + 1] += _dot_nt(db, v)
                dgvn_s[rows, cols] = _dot(wmt[2 * p], da) + _dot(wmt[2 * p + 1], db)
        dgvn = dgvn_s[...]
        accp_ref[pl.ds(GLB, 1), :] += _colsum(dgvn)
        accp_ref[pl.ds(GLG, 1), :] += _colsum(dgvn * vh)
        dgvg = _ln_bwd(dgvn * glg, vh, vrstd)
        dz_ref[:, pl.ds(3 * DC, DC)] = (dgvg * _gelu_grad(gv_pre, tv)).astype(BF16)

        dh1 = jnp.zeros((tm, D), F32)
        for d in range(NDEV):
            dh1 = dh1 + _dot_nt(dz_ref[:, pl.ds(d * nc, nc)], wi_ref[d])
        xn, r1 = _rms(x_ref[...])
        g1 = _row(v_ref, G1)
        sc = 1.0 + _row(v_ref, SC1)
        acc_ref[pl.ds(1, 1), :] += _colsum(dh1)
        acc_ref[pl.ds(2, 1), :] += _colsum(dh1 * (xn * g1))
        acc_ref[pl.ds(3, 1), :] += _colsum(dh1 * sc * xn)
        gx_ref[...] = dx2v + _rms_bwd(dh1 * sc * g1, xn, r1)

        @pl.when(i == nt - 1)
        def _():
            low = _causal_mask(True)
            for h in range(NH):
                dws_ref[h] = jnp.where(low, dws_ref[h], 0.0)
            lane = lax.broadcasted_iota(jnp.int32, (CHUNK, CHUNK), 1)
            out = jnp.zeros((CHUNK, CHUNK), F32)
            for h in range(NH):
                hs = jnp.sum(dbs_s[:, pl.ds((h // 2) * CHUNK, CHUNK)]
                             * ((lane >= (h % 2) * HD) & (lane < (h % 2 + 1) * HD)).astype(F32),
                             axis=1, keepdims=True)
                out = jnp.where(lane == h, hs, out)
            dbst_ref[...] = out

    tile = lambda w: pl.BlockSpec((tm, w), lambda i: (nt - 1 - i, 0))
    return pl.pallas_call(
        body, name="bwd_mid", grid=(nt,),
        in_specs=[pl.BlockSpec((None, tm, D), functools.partial(lambda k, i: (k, nt - 1 - i, 0), k)) for k in range(4)]
        + [tile(D), tile(D), tile(D), tile(D), tile(4 * DC), tile(DC),
                  pl.BlockSpec((HALO_C, DC), lambda i: (jnp.maximum((nt - 1 - i) * hb - 1, 0), 0)),
                  tile(DC), tile(DC), _full((VROWS, D)), _full((SUB, DC)), _full((CW_ROWS, DC)),
                  _full((NH, CHUNK, CHUNK)), _full((NH, CHUNK, CHUNK)), _full((D, D)), _full((NDEV, D, nc))],
        out_specs=[tile(D), tile(4 * DC), pl.BlockSpec((D, tm), lambda i: (0, nt - 1 - i)), tile(D),
                   _full((VROWS, D)), _full((SUB, DC)), _full((CW_ROWS, DC)),
                   _full((NH, CHUNK, CHUNK)), _full((CHUNK, CHUNK))],
        out_shape=[jax.ShapeDtypeStruct((s, D), F32), jax.ShapeDtypeStruct((s, 4 * DC), BF16),
                   jax.ShapeDtypeStruct((D, s), BF16), jax.ShapeDtypeStruct((s, D), BF16),
                   jax.ShapeDtypeStruct((VROWS, D), F32), jax.ShapeDtypeStruct((SUB, DC), F32),
                   jax.ShapeDtypeStruct((CW_ROWS, DC), F32), jax.ShapeDtypeStruct((NH, CHUNK, CHUNK), F32),
                   jax.ShapeDtypeStruct((CHUNK, CHUNK), F32)],
        scratch_shapes=[pltpu.VMEM((CHUNK, DC), F32), pltpu.VMEM((HALO_C, DC), F32), pltpu.VMEM((tm, DC), F32),
                        pltpu.VMEM((tm, DC), F32), pltpu.VMEM((tm, DC), F32)],
        compiler_params=_arb(),
    )(dh2, dh2, dh2, dh2, dx3, x2, x2d, o1, z, a0, a0, a1, sp, vecs, v512, conv_w, gm_ws, gm_ws_t, w_out_b, w_in_g)


def _mm_all_slots(name, at, b, bw, tk, after):
    k1, s = at.shape
    nslot = b.shape[1] // bw

    def body(a_ref, b_ref, after_ref, o_ref):
        @pl.when(pl.program_id(0) == 0)
        def _():
            o_ref[...] = jnp.zeros_like(o_ref)

        t = _dot(a_ref[...], b_ref[...])
        for j in range(nslot):
            o_ref[j] += t[:, j * bw:(j + 1) * bw]

    return pl.pallas_call(
        body, name=name, grid=(s // tk,),
        in_specs=[pl.BlockSpec((k1, tk), lambda k: (0, k)), pl.BlockSpec((tk, nslot * bw), lambda k: (k, 0)), ANY],
        out_specs=_full((nslot, k1, bw)), out_shape=jax.ShapeDtypeStruct((nslot, k1, bw), F32),
        compiler_params=_arb(),
    )(at, b, after)


def _adam_math(w, g, m, v):
    m = ADAM_B1 * m + (1.0 - ADAM_B1) * g
    v = ADAM_B2 * v + (1.0 - ADAM_B2) * (g * g)
    m_hat = m / (1.0 - ADAM_B1 ** ADAM_STEP)
    v_hat = v / (1.0 - ADAM_B2 ** ADAM_STEP)
    delta = -ADAM_LR * (m_hat / (jnp.sqrt(v_hat) + ADAM_EPS) + ADAM_WD * w)
    return delta, m, v


def _row_block(rows, cols):
    tr = rows
    while tr * cols * 4 > ROW_BLOCK_BYTES and tr % (4 * SUB) == 0:
        tr //= 2
    return tr


def _adam3(name, w, g, m, v):
    _, rows, cols = w.shape
    tr = _row_block(rows, cols)

    def body(w_ref, g_ref, m_ref, v_ref, d_ref, mo_ref, vo_ref):
        d_ref[...], mo_ref[...], vo_ref[...] = _adam_math(w_ref[...], g_ref[...], m_ref[...], v_ref[...])

    spec = pl.BlockSpec((1, tr, cols), lambda i: (0, i, 0))
    return pl.pallas_call(
        body, name=name, grid=(rows // tr,), in_specs=[spec] * 4, out_specs=[spec] * 3,
        out_shape=[jax.ShapeDtypeStruct(w.shape, F32)] * 3, compiler_params=_arb(),
    )(w, g, m, v)


def _sum_adam(name, parts, w, m, v):
    n, rows, cols = parts.shape
    tr = _row_block(rows, cols)

    def body(p_ref, w_ref, m_ref, v_ref, g_ref, d_ref, mo_ref, vo_ref):
        g = p_ref[0].astype(F32)
        for k in range(1, n):
            g = g + p_ref[k].astype(F32)
        g_ref[0] = g
        d_ref[0], mo_ref[0], vo_ref[0] = _adam_math(w_ref[0], g, m_ref[0], v_ref[0])

    spec = pl.BlockSpec((1, tr, cols), lambda i: (0, i, 0))
    return pl.pallas_call(
        body, name=name, grid=(rows // tr,),
        in_specs=[pl.BlockSpec((n, tr, cols), lambda i: (0, i, 0))] + [spec] * 3, out_specs=[spec] * 4,
        out_shape=[jax.ShapeDtypeStruct(w.shape, F32)] * 4, compiler_params=_arb(),
    )(parts, w, m, v)


def _pair_add(name, g4, recv, core):
    _, _, rows, cols = g4.shape
    tr = _row_block(rows, cols)

    def body(c_ref, a_ref, b_ref, o_ref):
        o_ref[...] = (a_ref[...] + b_ref[...]).astype(BF16)

    return pl.pallas_call(
        body, name=name,
        grid_spec=pltpu.PrefetchScalarGridSpec(
            num_scalar_prefetch=1, grid=(4, rows // tr),
            in_specs=[pl.BlockSpec((None, None, tr, cols), lambda k, i, c_ref: (k, c_ref[0], i, 0)),
                      pl.BlockSpec((None, tr, cols), lambda k, i, c_ref: (k, i, 0))],
            out_specs=pl.BlockSpec((None, tr, cols), lambda k, i, c_ref: (k, i, 0))),
        out_shape=jax.ShapeDtypeStruct((4, rows, cols), BF16), compiler_params=_arb(2),
    )(core, g4, recv)


def _sum_small(rows_all, p_all, ws_all, bst_all, fw_all, cw_all):
    def body(a_ref, p_ref, ws_ref, bst_ref, fw_ref, cw_ref,
             g_b_ada, g_n1, g_mog, g_n2, g_gf, loss_cols, g_cb, g_clg, g_clb, g_glg, g_glb, g_ws, g_bs, fw_sum,
             cw_sum):
        def total(ref):
            t = ref[0]
            for k in range(1, NDEV):
                t = t + ref[k]
            return t

        a = total(a_ref)
        g_b_ada[...] = jnp.concatenate([a[k:k + 1, :] for k in range(6)], axis=1)
        g_n1[...] = a[6:7, :]
        g_mog[...] = a[7:8, :]
        g_n2[...] = a[8:9, :]
        g_gf[...] = a[9:10, :].reshape(D)
        loss_cols[...] = a[10:11, :]
        p = total(p_ref)
        for k, ref in zip((CB, CLG, CLB, GLG, GLB), (g_cb, g_clg, g_clb, g_glg, g_glb)):
            ref[...] = p[k:k + 1, :]
        g_ws[0] = total(ws_ref)
        g_bs[0] = jnp.transpose(total(bst_ref))[:NH, :]
        fw_sum[...] = total(fw_ref)
        cw_sum[...] = total(cw_ref)

    vec = lambda n: jax.ShapeDtypeStruct((1, n), F32)
    return pl.pallas_call(
        body, name="sum_small_grads",
        out_shape=[vec(6 * D), vec(D), vec(D), vec(D), jax.ShapeDtypeStruct((D,), F32), vec(D),
                   vec(DC), vec(DC), vec(DC), vec(DC), vec(DC),
                   jax.ShapeDtypeStruct((1, NH, CHUNK, CHUNK), F32), jax.ShapeDtypeStruct((1, NH, CHUNK), F32),
                   jax.ShapeDtypeStruct((SUB, 2 * PFF), F32), jax.ShapeDtypeStruct((CW_ROWS, DC), F32)],
    )(rows_all, p_all, ws_all, bst_all, fw_all, cw_all)


def _adam_small(quads):
    n = len(quads)

    def body(*refs):
        ins, outs = refs[:4 * n], refs[4 * n:]
        for q in range(n):
            w, g, m, v = (r[...] for r in ins[4 * q:4 * q + 4])
            outs[3 * q][...], outs[3 * q + 1][...], outs[3 * q + 2][...] = _adam_math(w, g, m, v)

    flat = [a for q in quads for a in q]
    outs = pl.pallas_call(
        body, name="adam_small",
        out_shape=[jax.ShapeDtypeStruct(q[0].shape, F32) for q in quads for _ in range(3)],
    )(*flat)
    return [tuple(outs[3 * q:3 * q + 3]) for q in range(n)]


def kernel(x, c, w_ada, b_ada, norm1_gain, w_in, conv_dw_w, conv_dw_b, conv_ln_g, conv_ln_b, gm_ln_g, gm_ln_b, gm_ws, gm_bs, mix_out_gain, w_out, norm2_gain, w_up, ffn_dw_w, ffn_dw_b, w_down, final_gain, loss_target, m_w_ada, m_b_ada, m_norm1_gain, m_w_in, m_conv_dw_w, m_conv_dw_b, m_conv_ln_g, m_conv_ln_b, m_gm_ln_g, m_gm_ln_b, m_gm_ws, m_gm_bs, m_mix_out_gain, m_w_out, m_norm2_gain, m_w_up, m_ffn_dw_w, m_ffn_dw_b, m_w_down, m_final_gain, v_w_ada, v_b_ada, v_norm1_gain, v_w_in, v_conv_dw_w, v_conv_dw_b, v_conv_ln_g, v_conv_ln_b, v_gm_ln_g, v_gm_ln_b, v_gm_ws, v_gm_bs, v_mix_out_gain, v_w_out, v_norm2_gain, v_w_up, v_ffn_dw_w, v_ffn_dw_b, v_w_down, v_final_gain):
    s = x.shape[1]
    ax, ay, ac = _place()
    me = 4 * ax + 2 * ay + ac
    n_ada = w_ada.shape[2]
    n_cw = conv_dw_w.shape[2]
    x2d = x[0]
    target = loss_target[0]
    pad_sh = lambda a: jnp.pad(a, [(0, 0)] * (a.ndim - 1) + [(0, PSH - NSH)])

    gather_in, token_a = _start_gather("gather_in_out", [w_in[0].astype(BF16), w_out[0].astype(BF16)], me)

    c_all, cw_all, fw_all = _all_gather("gather_small", [c + token_a[0, 0], conv_dw_w[0], ffn_dw_w[0]])
    conv_w = jnp.pad(jnp.transpose(cw_all, (1, 0, 2)).reshape(KC, DC), ((0, CW_ROWS - KC), (0, 0)))
    ffn_w = jnp.transpose(pad_sh(fw_all), (1, 0, 2)).reshape(KF, 2 * PFF)
    ffn_b = pad_sh(ffn_dw_b.reshape(NDEV, NSH)).reshape(1, 2 * PFF)
    ffn_wb = jnp.concatenate([ffn_w, ffn_b, jnp.zeros((SUB - KF - 1, 2 * PFF), F32)], axis=0)

    b_cols = lax.dynamic_slice(b_ada, (0, me * n_ada), (1, n_ada))
    (mod_all,) = _all_gather("gather_mod", [_mod_part(c_all, w_ada, b_cols)])
    up_t = lambda a: jnp.swapaxes(a, 1, 2)
    w_up_shard = jnp.pad(up_t(w_up)[0].astype(BF16), ((0, PSH - NSH), (0, 0)))
    shards, mod_all = lax.optimization_barrier(((w_up_shard, w_down[0].astype(BF16)), mod_all))
    gather_ffn, token_c = _start_gather("gather_up_down", list(shards), me)
    mod = lax.dynamic_index_in_dim(mod_all, me, axis=1, keepdims=False).reshape(6, D)
    sh1, sc1, gt1, sh2, sc2, gt2 = [mod[k:k + 1] for k in range(6)]
    vecs = jnp.concatenate([norm1_gain, sh1, sc1, gt1, norm2_gain, sh2, sc2, gt2, mix_out_gain,
                            final_gain.reshape(1, D), jnp.zeros((6, D), F32)], axis=0)
    vecs = vecs + token_c[0, 0]
    v512 = jnp.concatenate([conv_dw_b, conv_ln_g, conv_ln_b, gm_ln_g, gm_ln_b, jnp.zeros((3, DC), F32)], axis=0)
    bs_exp = jnp.repeat(jnp.transpose(gm_bs[0]), HD, axis=1)
    gm_ws_t = jnp.swapaxes(gm_ws[0], 1, 2)

    tm_big, tm = min(TILE_BIG, s), min(TILE, s)
    w_in_g, w_out_g = _finish_gather("gather_in_out", gather_in, vecs)
    w_out_b = w_out_g.reshape(D, D)
    z, a0, h1_t = _fwd_in(x2d, vecs, w_in_g, tm_big)
    a1, sp, x2, o1, h2 = _fwd_mid(a0, z, x2d, vecs, v512, conv_w, gm_ws, bs_exp, w_out_b, tm)
    w_up_t, w_down_g = _finish_gather("gather_up_down", gather_ffn, h2)
    w_down_p = jnp.pad(w_down_g.reshape(4, NSH, D), ((0, 0), (0, PSH - NSH), (0, 0)))
    up_pre, vg, dx3, acc_f = _fwd_ffn(h2, x2, target, vecs, ffn_wb, w_up_t, w_down_p, tm)

    core = ac.reshape(1).astype(jnp.int32)
    mychip = 2 * ax + ay

    def to_pairs(named):
        g4s = [g.reshape((4, 2) + g.shape[1:]) for _, g in named]
        from_sibling = _sibling_swap("rs_sibling_" + named[0][0], g4s)
        return [_pair_add("rs_pair_add_" + t[0], g4, rv, core) for t, g4, rv in zip(named, g4s, from_sibling)]

    dh2, dw_up, dw_down, acc_fw = _bwd_ffn(dx3, up_pre, vg, h2, vecs, ffn_wb, w_up_t, w_down_p, tm_big)
    acc_fw = jnp.transpose(acc_fw, (2, 0, 1, 3)).reshape(SUB, 2 * PFF)
    ffn_g4s = [g.reshape((4, 2) + g.shape[1:]) for g in (dw_up, dw_down.reshape(NDEV, w_down.shape[1], D))]
    swap_ffn, token_w = _start_swap("rs_sibling_ffn", ffn_g4s)
    gx, dz, y_t, do1, acc_m, acc_p, dcw, dws, dbs_t = _bwd_mid(
        dh2, dx3, x2, x2d, o1, z, a0, a1, sp, vecs + token_w[0, 0], v512, conv_w, gm_ws[0], gm_ws_t, w_out_b, w_in_g, tm)
    ffn_pairs = [_pair_add("rs_pair_add_" + nm, g4, rv, core) for nm, g4, rv in zip(
        ("w_up", "w_down"), ffn_g4s, _finish_swap("rs_sibling_ffn", swap_ffn, gx))]
    exchange_ffn, token_x = _start_exchange("rs_chips_ffn", ffn_pairs, mychip)
    rows = jnp.concatenate([acc_m[1:3], acc_m[0:1], acc_m[5:7], acc_f[2:3], acc_m[3:5], acc_m[7:8], acc_f[0:2],
                            jnp.zeros((5, D), F32)], axis=0) + token_x[0, 0]
    small_gather, token_s = _start_gather("gather_small_grads", [rows, acc_p, dws, dbs_t, acc_fw, dcw], me)
    dw_in = _mm_all_slots("dw_in", h1_t, dz, w_in.shape[2], min(TK_IN, s), token_s)
    small_pass, token_p = _pass_gather("gather_small_grads", small_gather, dw_in)
    dw_out = _mm_all_slots("dw_out", y_t, do1, D, min(TK_OUT, s), token_p).reshape(NDEV, w_out.shape[1], D)
    exchange_mix, token_m = _start_exchange("rs_chips_mix", to_pairs([("w_in", dw_in), ("w_out", dw_out)]), mychip)

    rows_all, p_all, ws_all, bst_all, fwg_all, cwg_all = _end_gather("gather_small_grads", small_pass, token_m)
    (g_b_ada, g_n1, g_mog, g_n2, g_gf, loss_cols, g_cb, g_clg, g_clb, g_glg, g_glb, g_ws, g_bs, fw_sum,
     cw_sum) = _sum_small(rows_all, p_all, ws_all, bst_all, fwg_all, cwg_all)
    loss = jnp.sum(loss_cols)
    g_fb = fw_sum[3].reshape(NDEV, PSH)[:, :NSH].reshape(ffn_dw_b.shape)
    g_fw = lax.dynamic_index_in_dim(fw_sum[:KF].reshape(KF, NDEV, PSH), me, axis=1, keepdims=False)[:, :NSH]
    g_fw = g_fw.reshape(ffn_dw_w.shape)
    g_cw = lax.dynamic_slice(cw_sum, (0, me * n_cw), (KC, n_cw)).reshape(conv_dw_w.shape)
    small = [
        (b_ada, g_b_ada, m_b_ada, v_b_ada), (norm1_gain, g_n1, m_norm1_gain, v_norm1_gain),
        (conv_dw_w, g_cw, m_conv_dw_w, v_conv_dw_w), (conv_dw_b, g_cb, m_conv_dw_b, v_conv_dw_b),
        (conv_ln_g, g_clg, m_conv_ln_g, v_conv_ln_g), (conv_ln_b, g_clb, m_conv_ln_b, v_conv_ln_b),
        (gm_ln_g, g_glg, m_gm_ln_g, v_gm_ln_g), (gm_ln_b, g_glb, m_gm_ln_b, v_gm_ln_b),
        (gm_ws, g_ws, m_gm_ws, v_gm_ws), (gm_bs, g_bs, m_gm_bs, v_gm_bs),
        (mix_out_gain, g_mog, m_mix_out_gain, v_mix_out_gain), (norm2_gain, g_n2, m_norm2_gain, v_norm2_gain),
        (ffn_dw_w, g_fw, m_ffn_dw_w, v_ffn_dw_w), (ffn_dw_b, g_fb, m_ffn_dw_b, v_ffn_dw_b),
        (final_gain, g_gf, m_final_gain, v_final_gain)]
    small_out = _adam_small(small)
    res = {}
    for name, q, o in zip(("b_ada", "norm1_gain", "conv_dw_w", "conv_dw_b", "conv_ln_g", "conv_ln_b", "gm_ln_g",
                           "gm_ln_b", "gm_ws", "gm_bs", "mix_out_gain", "norm2_gain", "ffn_dw_w", "ffn_dw_b",
                           "final_gain"), small, small_out):
        res[name] = (q[1],) + o

    dmod_all = rows_all[:, :6].reshape(NDEV, 6 * D)
    dm_cols = lax.dynamic_slice(dmod_all, (0, me * n_ada), (NDEV, n_ada))
    g_ada = _ada_grad(jnp.transpose(c_all[:, 0, :]), dm_cols)
    res["w_ada"] = (g_ada,) + tuple(_adam3("adam_ada", w_ada, g_ada, m_w_ada, v_w_ada))

    big = [("w_up", up_t(w_up), up_t(m_w_up), up_t(v_w_up)), ("w_down", w_down, m_w_down, v_w_down),
           ("w_in", w_in, m_w_in, v_w_in), ("w_out", w_out, m_w_out, v_w_out)]
    from_chips = list(_finish_exchange("rs_chips_ffn", exchange_ffn, res["w_ada"][1]))
    for t, parts in zip(big[:2], from_chips):
        res[t[0]] = tuple(_sum_adam("rs_sum_adam_" + t[0], parts, t[1], t[2], t[3]))
    res["w_up"] = tuple(up_t(a) for a in res["w_up"])
    from_chips = list(_finish_exchange("rs_chips_mix", exchange_mix, res["w_down"][1]))
    for t, parts in zip(big[2:], from_chips):
        res[t[0]] = tuple(_sum_adam("rs_sum_adam_" + t[0], parts, t[1], t[2], t[3]))

    order = ("w_ada", "b_ada", "norm1_gain", "w_in", "conv_dw_w", "conv_dw_b", "conv_ln_g", "conv_ln_b", "gm_ln_g",
             "gm_ln_b", "gm_ws", "gm_bs", "mix_out_gain", "w_out", "norm2_gain", "w_up", "ffn_dw_w", "ffn_dw_b",
             "w_down", "final_gain")
    return (loss, gx.reshape(x.shape), *[res[n][0] for n in order], *[res[n][1] for n in order],
            *[res[n][2] for n in order], *[res[n][3] for n in order])
```

```python
import functools

import jax
import jax.numpy as jnp
from jax import lax
from jax.experimental import pallas as pl
from jax.experimental.pallas import tpu as pltpu

F32 = jnp.float32
BF16 = jnp.bfloat16
NDEV = 8
D = 1024
DC = 512
DFF = 2816
NSH = 704
PSH = 768
PFF = 4 * PSH
KC = 31
KF = 3
CHUNK = 128
NH = 8
HD = 64
HALO_C = 32
HALO_F = 8
LANES = 128
SUB = 8
VROWS = 16
CW_ROWS = 32
TILE_BIG = 512
ROW_BLOCK_BYTES = 2 << 20
TILE = 256
TK_IN = 1024
TK_OUT = 2048
RMS_EPS = 1e-6
LN_EPS = 1e-5
ADAM_LR = 0.001
ADAM_B1 = 0.9
ADAM_B2 = 0.999
ADAM_EPS = 1e-08
ADAM_WD = 0.01
ADAM_STEP = 10
GELU_K = 0.7978845608028654
GELU_C = 0.044715

MESH = pl.DeviceIdType.MESH
ANY = pl.BlockSpec(memory_space=pl.ANY)

G1, SH1, SC1, GT1, G2, SH2, SC2, GT2, MOG, GF = range(10)
CB, CLG, CLB, GLG, GLB = range(5)


def _full(shape):
    return pl.BlockSpec(shape, lambda *_: (0,) * len(shape))


def _arb(n=1):
    return pltpu.CompilerParams(dimension_semantics=("arbitrary",) * n)


def _row(ref, r):
    return ref[pl.ds(r, 1), :]


def _colsum(v):
    return jnp.sum(v, axis=0, keepdims=True)


def _rowmean(v):
    return jnp.mean(v, axis=-1, keepdims=True)


def _rms(x):
    r = lax.rsqrt(_rowmean(x * x) + RMS_EPS)
    return x * r, r


def _rms_bwd(dxn, xn, r):
    return r * (dxn - xn * _rowmean(dxn * xn))


def _ln(x):
    mu = _rowmean(x)
    xc = x - mu
    rstd = lax.rsqrt(_rowmean(xc * xc) + LN_EPS)
    return xc * rstd, rstd


def _ln_bwd(dxh, xhat, rstd):
    return rstd * (dxh - _rowmean(dxh) - xhat * _rowmean(dxh * xhat))


def _sigmoid(x):
    return 0.5 * jnp.tanh(0.5 * x) + 0.5


def _gelu(x):
    t = jnp.tanh(GELU_K * (x + GELU_C * x * x * x))
    return 0.5 * x * (1.0 + t), t


def _gelu_grad(x, t):
    return 0.5 * (1.0 + t) + 0.5 * x * (1.0 - t * t) * (GELU_K * (1.0 + 3.0 * GELU_C * x * x))


def _dot(a, b):
    return jnp.dot(a, b, preferred_element_type=F32)


def _dot_nt(a, b):
    return lax.dot_general(a, b, (((1,), (1,)), ((), ())), preferred_element_type=F32)


def _shift_up(e, s):
    n = e.shape[0]
    return pltpu.roll(e, (n - s) % n, 0)


def _place():
    return lax.axis_index("x"), lax.axis_index("y"), lax.axis_index("c")


def _all_gather(name, xs):
    n = len(xs)

    def body(*refs):
        x_refs, out_refs = refs[:n], refs[n:2 * n]
        send_sems, recv_sems, local_sems = refs[2 * n:]
        x, y, c = _place()
        me, sibling = (x, y, c), (x, y, 1 - c)
        chips = [(1 - x, y), (x, 1 - y), (1 - x, 1 - y)]

        def copy(a, k, block, to, own=False):
            px, py, pc = block
            slot = out_refs[a].at[4 * px + 2 * py + pc]
            return pltpu.make_async_remote_copy(
                src_ref=x_refs[a] if own else slot, dst_ref=slot,
                send_sem=send_sems.at[7 * a + k], recv_sem=recv_sems.at[7 * a + k], device_id=to, device_id_type=MESH)

        mine = [pltpu.make_async_copy(x_refs[a], out_refs[a].at[4 * x + 2 * y + c], local_sems.at[a]) for a in range(n)]
        for cp in mine:
            cp.start()
        first = []
        for a in range(n):
            first.append(copy(a, 0, me, sibling, own=True))
            first += [copy(a, 1 + j, me, (*chip, c), own=True) for j, chip in enumerate(chips)]
        for cp in first:
            cp.start()
        passed = []
        for j, chip in enumerate(chips):
            for a in range(n):
                copy(a, 1 + j, (*chip, c), me).wait_recv()
                cp = copy(a, 4 + j, (*chip, c), sibling)
                cp.start()
                passed.append(cp)
        for a in range(n):
            copy(a, 0, sibling, me).wait_recv()
            for j, chip in enumerate(chips):
                copy(a, 4 + j, (*chip, 1 - c), me).wait_recv()
        for cp in first + passed:
            cp.wait_send()
        for cp in mine:
            cp.wait()

    return pl.pallas_call(
        body, name=name, out_shape=[jax.ShapeDtypeStruct((NDEV,) + a.shape, a.dtype) for a in xs],
        in_specs=[ANY] * n, out_specs=[ANY] * n,
        scratch_shapes=[pltpu.SemaphoreType.DMA((7 * n,)), pltpu.SemaphoreType.DMA((7 * n,)),
                        pltpu.SemaphoreType.DMA((n,))],
    )(*xs)


def _sibling_swap(name, g4s):
    n = len(g4s)

    def body(*refs):
        g_refs, out_refs = refs[:n], refs[n:2 * n]
        send_sems, recv_sems = refs[2 * n:]
        x, y, c = _place()
        cps = [pltpu.make_async_remote_copy(
            src_ref=g_refs[a].at[k, 1 - c], dst_ref=out_refs[a].at[k],
            send_sem=send_sems.at[4 * a + k], recv_sem=recv_sems.at[4 * a + k],
            device_id=(x, y, 1 - c), device_id_type=MESH) for a in range(n) for k in range(4)]
        for cp in cps:
            cp.start()
        for cp in cps:
            cp.wait()

    return pl.pallas_call(
        body, name=name, out_shape=[jax.ShapeDtypeStruct((4,) + g.shape[2:], g.dtype) for g in g4s],
        in_specs=[ANY] * n, out_specs=[ANY] * n,
        scratch_shapes=[pltpu.SemaphoreType.DMA((4 * n,)), pltpu.SemaphoreType.DMA((4 * n,))],
    )(*g4s)


HBM = pl.BlockSpec(memory_space=pltpu.HBM)
SEM = pl.BlockSpec(memory_space=pltpu.SEMAPHORE)
EFFECT = pltpu.SideEffectType.DATAFLOW_SIDE_EFFECTING


def _in_hbm(a):
    return pltpu.with_memory_space_constraint(a, pltpu.HBM)


def _split_start(name, bufs, copies):
    n = len(bufs)

    def body(*refs):
        for cp in copies(refs[:n], refs[n], refs[n + 1]):
            cp.start()
        refs[-1][...] = jnp.zeros_like(refs[-1])

    out = pl.pallas_call(
        body, name=name,
        out_shape=(pltpu.SemaphoreType.DMA((copies.count,)), pltpu.SemaphoreType.DMA((copies.count,)),
                   *[pltpu.HBM(a.shape, a.dtype) for a in bufs], jax.ShapeDtypeStruct((SUB, LANES), F32)),
        in_specs=[HBM] * n, out_specs=(SEM, SEM, *[HBM] * n, pl.BlockSpec(memory_space=pltpu.VMEM)),
        input_output_aliases={i: 2 + i for i in range(n)},
        compiler_params=pltpu.CompilerParams(has_side_effects=EFFECT),
    )(*[_in_hbm(a) for a in bufs])
    return (out[0], out[1], list(out[2:2 + n])), out[-1]


def _split_wait(name, handle, copies, after):
    send_sems, recv_sems, bufs = handle
    n = len(bufs)

    def body(*refs):
        for cp in copies(refs[:n], refs[n], refs[n + 1]):
            cp.wait_send()
            cp.wait_recv()

    out = pl.pallas_call(
        body, name=name, out_shape=tuple(pltpu.HBM(a.shape, a.dtype) for a in bufs),
        in_specs=[HBM] * n + [SEM, SEM, pl.BlockSpec(memory_space=pl.ANY)], out_specs=tuple([HBM] * n),
        input_output_aliases={i: i for i in range(n)},
        compiler_params=pltpu.CompilerParams(has_side_effects=EFFECT),
    )(*bufs, send_sems, recv_sems, after)
    return list(out)


class _GatherFirstCopies:
    def __init__(self, n):
        self.n, self.count = n, 4 * n

    def __call__(self, refs, send_sems, recv_sems):
        x, y, c = _place()
        peers = [(x, y, 1 - c), (1 - x, y, c), (x, 1 - y, c), (1 - x, 1 - y, c)]
        return [pltpu.make_async_remote_copy(
            src_ref=refs[a], dst_ref=refs[self.n + a].at[4 * x + 2 * y + c],
            send_sem=send_sems.at[4 * a + k], recv_sem=recv_sems.at[4 * a + k], device_id=peer, device_id_type=MESH)
            for a in range(self.n) for k, peer in enumerate(peers)]


class _GatherPassCopies:
    def __init__(self, n):
        self.n, self.count = n, 3 * n

    def __call__(self, refs, send_sems, recv_sems):
        x, y, c = _place()
        cps = []
        for a in range(self.n):
            for j, (px, py) in enumerate([(1 - x, y), (x, 1 - y), (1 - x, 1 - y)]):
                slot = refs[a].at[4 * px + 2 * py + c]
                cps.append(pltpu.make_async_remote_copy(
                    src_ref=slot, dst_ref=slot, send_sem=send_sems.at[3 * a + j], recv_sem=recv_sems.at[3 * a + j],
                    device_id=(x, y, 1 - c), device_id_type=MESH))
        return cps


class _ExchangeCopies:
    def __init__(self, n):
        self.n, self.count = n, 3 * n

    def __call__(self, refs, send_sems, recv_sems):
        x, y, c = _place()
        cps = []
        for a in range(self.n):
            for j, (px, py) in enumerate([(1 - x, y), (x, 1 - y), (1 - x, 1 - y)]):
                cps.append(pltpu.make_async_remote_copy(
                    src_ref=refs[a].at[2 * px + py], dst_ref=refs[self.n + a].at[2 * x + y],
                    send_sem=send_sems.at[3 * a + j], recv_sem=recv_sems.at[3 * a + j],
                    device_id=(px, py, c), device_id_type=MESH))
        return cps


def _own_slot(nslot, src, index):
    land = lax.empty((nslot,) + src.shape, src.dtype)
    return lax.dynamic_update_slice(land, src[None], (index,) + (0,) * src.ndim)


def _start_gather(tag, xs, me):
    lands = [_own_slot(NDEV, a, me) for a in xs]
    return _split_start(tag + "_start", list(xs) + lands, _GatherFirstCopies(len(xs)))


def _pass_gather(tag, handle, after):
    n = len(handle[2]) // 2
    lands = _split_wait(tag + "_wait", handle, _GatherFirstCopies(n), after)[n:]
    return _split_start(tag + "_pass", lands, _GatherPassCopies(n))


def _end_gather(tag, passing, after):
    return _split_wait(tag + "_pass_wait", passing, _GatherPassCopies(len(passing[2])), after)


def _finish_gather(tag, handle, after):
    passing, token = _pass_gather(tag, handle, after)
    return _end_gather(tag, passing, token)


def _start_exchange(tag, hs, mychip):
    lands = [_own_slot(4, lax.dynamic_index_in_dim(h, mychip, 0, keepdims=False), mychip) for h in hs]
    return _split_start(tag + "_start", list(hs) + lands, _ExchangeCopies(len(hs)))


def _finish_exchange(tag, handle, after):
    n = len(handle[2]) // 2
    return _split_wait(tag + "_wait", handle, _ExchangeCopies(n), after)[n:]


def _mod_part(c_all, w_ada, b_cols):
    ncol = w_ada.shape[2]

    def body(c_ref, w_ref, b_ref, o_ref):
        cv = c_ref[:, 0, :]
        ca = cv * _sigmoid(cv)
        o_ref[...] = _dot(ca.astype(BF16), w_ref[0].astype(BF16)) + b_ref[...]

    return pl.pallas_call(body, name="mod_part", out_shape=jax.ShapeDtypeStruct((NDEV, ncol), F32))(
        c_all, w_ada, b_cols)


def _ada_grad(c_all_t, dmod_cols):
    ncol = dmod_cols.shape[1]

    def body(ct_ref, dm_ref, o_ref):
        ct = ct_ref[...]
        ca = ct * _sigmoid(ct)
        acc = jnp.zeros((D, ncol), F32)
        for b in range(NDEV):
            acc = acc + ca[:, b:b + 1] * dm_ref[pl.ds(b, 1), :]
        o_ref[0] = acc

    return pl.pallas_call(body, name="ada_grad", out_shape=jax.ShapeDtypeStruct((1, D, ncol), F32))(
        c_all_t, dmod_cols)


def _fwd_in(x2d, vecs, w_in_g, tm):
    s = x2d.shape[0]
    nc = w_in_g.shape[2]

    def body(x_ref, v_ref, w_ref, z_ref, a0_ref, h1t_ref):
        xn, _ = _rms(x_ref[...])
        h = (xn * _row(v_ref, G1)) * (1.0 + _row(v_ref, SC1)) + _row(v_ref, SH1)
        hb = h.astype(BF16)
        h1t_ref[...] = hb.T
        for d in range(NDEV):
            z_ref[:, pl.ds(d * nc, nc)] = _dot(hb, w_ref[d])
        a0_ref[...] = z_ref[:, :DC] * _sigmoid(z_ref[:, DC:2 * DC])

    return pl.pallas_call(
        body, name="fwd_in", grid=(s // tm,),
        in_specs=[pl.BlockSpec((tm, D), lambda i: (i, 0)), _full((VROWS, D)), _full((NDEV, D, nc))],
        out_specs=[pl.BlockSpec((tm, 4 * DC), lambda i: (i, 0)), pl.BlockSpec((tm, DC), lambda i: (i, 0)),
                   pl.BlockSpec((D, tm), lambda i: (0, i))],
        out_shape=[jax.ShapeDtypeStruct((s, 4 * DC), F32), jax.ShapeDtypeStruct((s, DC), F32),
                   jax.ShapeDtypeStruct((D, s), BF16)],
        compiler_params=_arb(),
    )(x2d, vecs, w_in_g)


def _causal_mask(lower):
    r = lax.broadcasted_iota(jnp.int32, (CHUNK, CHUNK), 0)
    c = lax.broadcasted_iota(jnp.int32, (CHUNK, CHUNK), 1)
    return (r >= c) if lower else (r <= c)


def _first_head_lanes():
    return lax.broadcasted_iota(jnp.int32, (CHUNK, CHUNK), 1) < HD


def _fwd_mid(a0, z, x2d, vecs, v512, conv_w, gm_ws, bs_exp, w_out_b, tm):
    s = x2d.shape[0]
    hb = tm // HALO_C

    def body(a0_ref, halo_ref, zg_ref, x_ref, v_ref, p_ref, cw_ref, ws_ref, bs_ref, wo_ref,
             a1_ref, sp_ref, x2_ref, o1_ref, h2_ref):
        i = pl.program_id(0)
        for c0 in range(0, DC, LANES):
            cols = pl.ds(c0, LANES)
            halo = halo_ref[:, cols]
            e = jnp.concatenate([jnp.where(i > 0, halo, jnp.zeros_like(halo)), a0_ref[:, cols]], axis=0)
            acc = jnp.broadcast_to(p_ref[pl.ds(CB, 1), cols], (tm, LANES))
            for k in range(KC):
                acc = acc + _shift_up(e, HALO_C - (KC - 1) + k)[:tm, :] * cw_ref[pl.ds(k, 1), cols]
            a1_ref[:, cols] = acc
        xh, _ = _ln(a1_ref[...])
        a2 = xh * _row(p_ref, CLG) + _row(p_ref, CLB)
        a3 = a2 * _sigmoid(a2)
        gu, _ = _gelu(zg_ref[:, :DC])
        gvg, _ = _gelu(zg_ref[:, DC:])
        vh, _ = _ln(gvg)
        gvn = (vh * _row(p_ref, GLG) + _row(p_ref, GLB)).astype(BF16)
        low = _causal_mask(True)
        first = _first_head_lanes()
        wm = [jnp.where(low, ws_ref[0, h], 0.0).astype(BF16) for h in range(NH)]
        for n in range(tm // CHUNK):
            for p in range(NH // 2):
                v = gvn[n * CHUNK:(n + 1) * CHUNK, p * CHUNK:(p + 1) * CHUNK]
                blk = jnp.where(first, _dot(wm[2 * p], v), _dot(wm[2 * p + 1], v))
                sp_ref[pl.ds(n * CHUNK, CHUNK), pl.ds(p * CHUNK, CHUNK)] = blk + bs_ref[:, pl.ds(p * CHUNK, CHUNK)]
        g = gu * sp_ref[...]
        an, _ = _rms(a3)
        gn, _ = _rms(g)
        mog = _row(v_ref, MOG)
        y = jnp.concatenate([an * mog[:, :DC], gn * mog[:, DC:]], axis=1).astype(BF16)
        o1 = _dot(y, wo_ref[...])
        o1_ref[...] = o1
        x2 = x_ref[...] + _row(v_ref, GT1) * o1
        x2_ref[...] = x2
        xn2, _ = _rms(x2)
        h2 = (xn2 * _row(v_ref, G2)) * (1.0 + _row(v_ref, SC2)) + _row(v_ref, SH2)
        h2_ref[...] = h2.astype(BF16)

    tile = lambda w: pl.BlockSpec((tm, w), lambda i: (i, 0))
    return pl.pallas_call(
        body, name="fwd_mid", grid=(s // tm,),
        in_specs=[tile(DC), pl.BlockSpec((HALO_C, DC), lambda i: (jnp.maximum(i * hb - 1, 0), 0)),
                  pl.BlockSpec((tm, 2 * DC), lambda i: (i, 1)), tile(D), _full((VROWS, D)), _full((SUB, DC)),
                  _full((CW_ROWS, DC)), _full((1, NH, CHUNK, CHUNK)), _full((CHUNK, DC)), _full((D, D))],
        out_specs=[tile(DC), tile(DC), tile(D), tile(D), tile(D)],
        out_shape=[jax.ShapeDtypeStruct((s, DC), F32), jax.ShapeDtypeStruct((s, DC), F32),
                   jax.ShapeDtypeStruct((s, D), F32), jax.ShapeDtypeStruct((s, D), F32),
                   jax.ShapeDtypeStruct((s, D), BF16)],
        compiler_params=_arb(),
    )(a0, a0, z, x2d, vecs, v512, conv_w, gm_ws, bs_exp, w_out_b)


def _ffn_conv(fw_ref, cols, p2, p1, pre):
    return (fw_ref[pl.ds(3, 1), cols] + fw_ref[pl.ds(0, 1), cols] * p2
            + fw_ref[pl.ds(1, 1), cols] * p1 + fw_ref[pl.ds(2, 1), cols] * pre)


def _fwd_ffn(h2, x2, target, vecs, ffn_wb, w_up_t, w_down_p, tm):
    s = x2.shape[0]

    def body(h2_ref, x2_ref, t_ref, v_ref, fw_ref, wu_hbm, wd_hbm,
             up_ref, vg_ref, dx3_ref, acc_ref, wu, wd, carry, stage):
        i = pl.program_id(0)

        @pl.when(i == 0)
        def _():
            for sh in range(NDEV):
                pltpu.sync_copy(wu_hbm.at[sh], stage)
                wu[sh] = stage[...].T
            pltpu.sync_copy(wd_hbm, wd)
            carry[...] = jnp.zeros_like(carry)
            acc_ref[...] = jnp.zeros_like(acc_ref)

        h2v = h2_ref[...]
        o2 = jnp.zeros((tm, D), F32)
        for j in range(4):
            conv = []
            for sh in (j, 4 + j):
                cols = pl.ds(sh * PSH, PSH)
                pre = _dot(h2v, wu[sh])
                up_ref[:, cols] = pre.astype(BF16)
                e = jnp.concatenate([carry[:, cols], pre], axis=0)
                carry[:, cols] = pre[tm - HALO_F:, :]
                conv.append(_ffn_conv(fw_ref, cols, pltpu.roll(e, 2, 0)[HALO_F:, :],
                                      pltpu.roll(e, 1, 0)[HALO_F:, :], pre))
            val, gate = conv
            vg_ref[:, pl.ds(j * PSH, PSH)] = val.astype(BF16)
            vg_ref[:, pl.ds((4 + j) * PSH, PSH)] = gate.astype(BF16)
            f = ((gate * _sigmoid(gate)) * val).astype(BF16)
            o2 = o2 + _dot(f, wd[j])
        x3 = x2_ref[...] + _row(v_ref, GT2) * o2
        xn3, r3 = _rms(x3)
        gf = _row(v_ref, GF)
        diff = xn3 * gf - t_ref[...]
        acc_ref[pl.ds(1, 1), :] += _colsum(diff * diff) * (0.5 / D)
        dout = diff * (1.0 / D)
        acc_ref[pl.ds(0, 1), :] += _colsum(dout * xn3)
        dx3 = _rms_bwd(dout * gf, xn3, r3)
        dx3_ref[...] = dx3
        acc_ref[pl.ds(2, 1), :] += _colsum(dx3 * o2)

    tile = lambda w: pl.BlockSpec((tm, w), lambda i: (i, 0))
    return pl.pallas_call(
        body, name="fwd_ffn", grid=(s // tm,),
        in_specs=[tile(D), tile(D), tile(D), _full((VROWS, D)), _full((SUB, 2 * PFF)), ANY, ANY],
        out_specs=[tile(2 * PFF), tile(2 * PFF), tile(D), _full((SUB, D))],
        out_shape=[jax.ShapeDtypeStruct((s, 2 * PFF), BF16), jax.ShapeDtypeStruct((s, 2 * PFF), BF16),
                   jax.ShapeDtypeStruct((s, D), F32), jax.ShapeDtypeStruct((SUB, D), F32)],
        scratch_shapes=[pltpu.VMEM((NDEV, D, PSH), BF16), pltpu.VMEM((4, PSH, D), BF16),
                        pltpu.VMEM((HALO_F, 2 * PFF), F32), pltpu.VMEM((PSH, D), BF16)],
        compiler_params=_arb(),
    )(h2, x2, target, vecs, ffn_wb, w_up_t, w_down_p)


def _bwd_ffn(dx3, up_pre, vg, h2, vecs, ffn_wb, w_up_t, w_down_p, tm):
    s = dx3.shape[0]
    nt = s // tm

    def body(dx3_ref, up_ref, upg_ref, val_ref, gate_ref, h2_ref, v_ref, fw_ref, fwg_ref, wu_ref, wug_ref, wd_ref,
             dh2_ref, dwu_ref, dwd_ref, accf_ref, carry):
        i = pl.program_id(1)

        @pl.when(i == 0)
        def _():
            for ref in (carry, dwu_ref, dwd_ref, accf_ref):
                ref[...] = jnp.zeros_like(ref)

        do2 = (dx3_ref[...] * _row(v_ref, GT2)).astype(BF16)
        df = _dot_nt(do2, wd_ref[...])
        val = val_ref[...].astype(F32)
        gate = gate_ref[...].astype(F32)
        sg = _sigmoid(gate)
        sl = gate * sg
        f_t = (sl * val).astype(BF16).T
        dwd_ref[...] += _dot(f_t, do2)[:NSH, :]
        dups = (df * sl, df * val * (sg * (1.0 + gate * (1.0 - sg))))
        h2v = h2_ref[...]
        dh2 = jnp.zeros((tm, D), F32)
        for half, (dup, pre_ref, w_ref, wmat_ref) in enumerate(
                zip(dups, (up_ref, upg_ref), (fw_ref, fwg_ref), (wu_ref, wug_ref))):
            cols = pl.ds(half * PSH, PSH)
            e = jnp.concatenate([dup, carry[:, cols]], axis=0)
            carry[:, cols] = dup[:HALO_F, :]
            d1 = _shift_up(e, 1)[:tm, :]
            d2 = _shift_up(e, 2)[:tm, :]
            pre = pre_ref[...].astype(F32)
            accf_ref[half, pl.ds(3, 1), :] += _colsum(dup)
            accf_ref[half, pl.ds(0, 1), :] += _colsum(d2 * pre)
            accf_ref[half, pl.ds(1, 1), :] += _colsum(d1 * pre)
            accf_ref[half, pl.ds(2, 1), :] += _colsum(dup * pre)
            dpre = (_row(w_ref, 0) * d2 + _row(w_ref, 1) * d1 + _row(w_ref, 2) * dup).astype(BF16)
            dwu_ref[half] += _dot(dpre.T, h2v)[:NSH, :]
            dh2 = dh2 + _dot(dpre, wmat_ref[...])
        dh2_ref[...] = dh2

    rev = lambda j, i: nt - 1 - i
    in_specs = [
        pl.BlockSpec((tm, D), lambda j, i: (rev(j, i), 0)),
        pl.BlockSpec((tm, PSH), lambda j, i: (rev(j, i), j)), pl.BlockSpec((tm, PSH), lambda j, i: (rev(j, i), 4 + j)),
        pl.BlockSpec((tm, PSH), lambda j, i: (rev(j, i), j)), pl.BlockSpec((tm, PSH), lambda j, i: (rev(j, i), 4 + j)),
        pl.BlockSpec((tm, D), lambda j, i: (rev(j, i), 0)), _full((VROWS, D)),
        pl.BlockSpec((SUB, PSH), lambda j, i: (0, j)), pl.BlockSpec((SUB, PSH), lambda j, i: (0, 4 + j)),
        pl.BlockSpec((None, PSH, D), lambda j, i: (j, 0, 0)), pl.BlockSpec((None, PSH, D), lambda j, i: (4 + j, 0, 0)),
        pl.BlockSpec((None, PSH, D), lambda j, i: (j, 0, 0))]
    dh2, dw_up, dw_down, accf = pl.pallas_call(
        body, name="bwd_ffn", grid=(4, nt), in_specs=in_specs,
        out_specs=[pl.BlockSpec((None, tm, D), lambda j, i: (j, rev(j, i), 0)),
                   pl.BlockSpec((2, None, NSH, D), lambda j, i: (0, j, 0, 0)),
                   pl.BlockSpec((None, NSH, D), lambda j, i: (j, 0, 0)),
                   pl.BlockSpec((2, None, SUB, PSH), lambda j, i: (0, j, 0, 0))],
        out_shape=[jax.ShapeDtypeStruct((4, s, D), F32), jax.ShapeDtypeStruct((2, 4, NSH, D), F32),
                   jax.ShapeDtypeStruct((4, NSH, D), F32), jax.ShapeDtypeStruct((2, 4, SUB, PSH), F32)],
        scratch_shapes=[pltpu.VMEM((HALO_F, 2 * PSH), F32)],
        compiler_params=_arb(2),
    )(dx3, up_pre, up_pre, vg, vg, h2, vecs, ffn_wb, ffn_wb, w_up_t, w_up_t, w_down_p)
    return dh2, dw_up.reshape(NDEV, NSH, D), dw_down, accf


def _bwd_mid(dh2, dx3, x2, x2d, o1, z, a0, a1, sp, vecs, v512, conv_w, gm_ws, gm_ws_t, w_out_b, w_in_g, tm):
    s = x2d.shape[0]
    nt = s // tm
    hb = tm // HALO_C
    nc = w_in_g.shape[2]

    def body(dh2a_ref, dh2b_ref, dh2c_ref, dh2d_ref, dx3_ref, x2_ref, x_ref, o1_ref, z_ref, a0_ref, halo_ref, a1_ref, sp_ref, v_ref, p_ref, cw_ref,
             ws_ref, wst_ref, wo_ref, wi_ref, gx_ref, dz_ref, yt_ref, do1_ref, acc_ref, accp_ref, dcw_ref, dws_ref,
             dbst_ref, dbs_s, carry, da1_s, dsp_s, dgvn_s):
        i = pl.program_id(0)
        r = nt - 1 - i

        @pl.when(i == 0)
        def _():
            for ref in (carry, dbs_s, acc_ref, accp_ref, dcw_ref, dws_ref, dbst_ref):
                ref[...] = jnp.zeros_like(ref)

        dh2v = (dh2a_ref[...] + dh2b_ref[...]) + (dh2c_ref[...] + dh2d_ref[...])
        xn2, r2 = _rms(x2_ref[...])
        g2 = _row(v_ref, G2)
        sc2 = 1.0 + _row(v_ref, SC2)
        acc_ref[pl.ds(5, 1), :] += _colsum(dh2v)
        acc_ref[pl.ds(6, 1), :] += _colsum(dh2v * (xn2 * g2))
        acc_ref[pl.ds(7, 1), :] += _colsum(dh2v * sc2 * xn2)
        dx2v = dx3_ref[...] + _rms_bwd(dh2v * sc2 * g2, xn2, r2)
        do1 = (dx2v * _row(v_ref, GT1)).astype(BF16)
        do1_ref[...] = do1
        acc_ref[pl.ds(0, 1), :] += _colsum(dx2v * o1_ref[...])
        dy = _dot_nt(do1, wo_ref[...])
        mog = _row(v_ref, MOG)

        xh, rstd = _ln(a1_ref[...])
        clg = _row(p_ref, CLG)
        a2 = xh * clg + _row(p_ref, CLB)
        s2 = _sigmoid(a2)
        a3 = a2 * s2
        an, ra = _rms(a3)
        dya = dy[:, :DC]
        da3 = _rms_bwd(dya * mog[:, :DC], an, ra)
        da2 = da3 * (s2 * (1.0 + a2 * (1.0 - s2)))
        accp_ref[pl.ds(CLB, 1), :] += _colsum(da2)
        accp_ref[pl.ds(CLG, 1), :] += _colsum(da2 * xh)
        da1 = _ln_bwd(da2 * clg, xh, rstd)
        accp_ref[pl.ds(CB, 1), :] += _colsum(da1)
        da1_s[...] = da1
        for c0 in range(0, DC, LANES):
            cols = pl.ds(c0, LANES)
            d = da1_s[:, cols]
            e = jnp.concatenate([d, carry[:, cols]], axis=0)
            carry[:, cols] = d[:HALO_C, :]
            acc = jnp.zeros((tm, LANES), F32)
            for j in range(KC):
                acc = acc + _shift_up(e, j)[:tm, :] * cw_ref[pl.ds(KC - 1 - j, 1), cols]
            sgc = _sigmoid(z_ref[:, pl.ds(DC + c0, LANES)])
            dz_ref[:, cols] = (acc * sgc).astype(BF16)
            dz_ref[:, pl.ds(DC + c0, LANES)] = (acc * z_ref[:, cols] * sgc * (1.0 - sgc)).astype(BF16)
            halo = halo_ref[:, cols]
            ea = jnp.concatenate([jnp.where(r > 0, halo, jnp.zeros_like(halo)), a0_ref[:, cols]], axis=0)
            for k in range(KC):
                dcw_ref[pl.ds(k, 1), cols] += _colsum(d * _shift_up(ea, HALO_C - (KC - 1) + k)[:tm, :])

        gu_pre = z_ref[:, 2 * DC:3 * DC]
        gv_pre = z_ref[:, 3 * DC:]
        gu, tu = _gelu(gu_pre)
        gvg, tv = _gelu(gv_pre)
        vh, vrstd = _ln(gvg)
        glg = _row(p_ref, GLG)
        gvn = (vh * glg + _row(p_ref, GLB)).astype(BF16)
        spv = sp_ref[...]
        g = gu * spv
        gn, rg = _rms(g)
        yt_ref[...] = jnp.concatenate([an * mog[:, :DC], gn * mog[:, DC:]], axis=1).astype(BF16).T
        acc_ref[pl.ds(4, 1), :] += jnp.concatenate([_colsum(dya * an), _colsum(dy[:, DC:] * gn)], axis=1)
        dg = _rms_bwd(dy[:, DC:] * mog[:, DC:], gn, rg)
        dz_ref[:, pl.ds(2 * DC, DC)] = (dg * spv * _gelu_grad(gu_pre, tu)).astype(BF16)
        dsp_s[...] = dg * gu
        upper = _causal_mask(False)
        first = _first_head_lanes()
        wmt = [jnp.where(upper, wst_ref[h], 0.0).astype(BF16) for h in range(NH)]
        for n in range(tm // CHUNK):
            rows = pl.ds(n * CHUNK, CHUNK)
            for p in range(NH // 2):
                cols = pl.ds(p * CHUNK, CHUNK)
                dsp = dsp_s[rows, cols]
                dbs_s[:, cols] += dsp
                da = jnp.where(first, dsp, 0.0).astype(BF16)
                db = jnp.where(first, 0.0, dsp).astype(BF16)
                v = gvn[n * CHUNK:(n + 1) * CHUNK, p * CHUNK:(p + 1) * CHUNK]
                dws_ref[2 * p] += _dot_nt(da, v)
                dws_ref[2 * p + 1] += _dot_nt(db, v)
                dgvn_s[rows, cols] = _dot(wmt[2 * p], da) + _dot(wmt[2 * p + 1], db)
        dgvn = dgvn_s[...]
        accp_ref[pl.ds(GLB, 1), :] += _colsum(dgvn)
        accp_ref[pl.ds(GLG, 1), :] += _colsum(dgvn * vh)
        dgvg = _ln_bwd(dgvn * glg, vh, vrstd)
        dz_ref[:, pl.ds(3 * DC, DC)] = (dgvg * _gelu_grad(gv_pre, tv)).astype(BF16)

        dh1 = jnp.zeros((tm, D), F32)
        for d in range(NDEV):
            dh1 = dh1 + _dot_nt(dz_ref[:, pl.ds(d * nc, nc)], wi_ref[d])
        xn, r1 = _rms(x_ref[...])
        g1 = _row(v_ref, G1)
        sc = 1.0 + _row(v_ref, SC1)
        acc_ref[pl.ds(1, 1), :] += _colsum(dh1)
        acc_ref[pl.ds(2, 1), :] += _colsum(dh1 * (xn * g1))
        acc_ref[pl.ds(3, 1), :] += _colsum(dh1 * sc * xn)
        gx_ref[...] = dx2v + _rms_bwd(dh1 * sc * g1, xn, r1)

        @pl.when(i == nt - 1)
        def _():
            low = _causal_mask(True)
            for h in range(NH):
                dws_ref[h] = jnp.where(low, dws_ref[h], 0.0)
            lane = lax.broadcasted_iota(jnp.int32, (CHUNK, CHUNK), 1)
            out = jnp.zeros((CHUNK, CHUNK), F32)
            for h in range(NH):
                hs = jnp.sum(dbs_s[:, pl.ds((h // 2) * CHUNK, CHUNK)]
                             * ((lane >= (h % 2) * HD) & (lane < (h % 2 + 1) * HD)).astype(F32),
                             axis=1, keepdims=True)
                out = jnp.where(lane == h, hs, out)
            dbst_ref[...] = out

    tile = lambda w: pl.BlockSpec((tm, w), lambda i: (nt - 1 - i, 0))
    return pl.pallas_call(
        body, name="bwd_mid", grid=(nt,),
        in_specs=[pl.BlockSpec((None, tm, D), functools.partial(lambda k, i: (k, nt - 1 - i, 0), k)) for k in range(4)]
        + [tile(D), tile(D), tile(D), tile(D), tile(4 * DC), tile(DC),
                  pl.BlockSpec((HALO_C, DC), lambda i: (jnp.maximum((nt - 1 - i) * hb - 1, 0), 0)),
                  tile(DC), tile(DC), _full((VROWS, D)), _full((SUB, DC)), _full((CW_ROWS, DC)),
                  _full((NH, CHUNK, CHUNK)), _full((NH, CHUNK, CHUNK)), _full((D, D)), _full((NDEV, D, nc))],
        out_specs=[tile(D), tile(4 * DC), pl.BlockSpec((D, tm), lambda i: (0, nt - 1 - i)), tile(D),
                   _full((VROWS, D)), _full((SUB, DC)), _full((CW_ROWS, DC)),
                   _full((NH, CHUNK, CHUNK)), _full((CHUNK, CHUNK))],
        out_shape=[jax.ShapeDtypeStruct((s, D), F32), jax.ShapeDtypeStruct((s, 4 * DC), BF16),
                   jax.ShapeDtypeStruct((D, s), BF16), jax.ShapeDtypeStruct((s, D), BF16),
                   jax.ShapeDtypeStruct((VROWS, D), F32), jax.ShapeDtypeStruct((SUB, DC), F32),
                   jax.ShapeDtypeStruct((CW_ROWS, DC), F32), jax.ShapeDtypeStruct((NH, CHUNK, CHUNK), F32),
                   jax.ShapeDtypeStruct((CHUNK, CHUNK), F32)],
        scratch_shapes=[pltpu.VMEM((CHUNK, DC), F32), pltpu.VMEM((HALO_C, DC), F32), pltpu.VMEM((tm, DC), F32),
                        pltpu.VMEM((tm, DC), F32), pltpu.VMEM((tm, DC), F32)],
        compiler_params=_arb(),
    )(dh2, dh2, dh2, dh2, dx3, x2, x2d, o1, z, a0, a0, a1, sp, vecs, v512, conv_w, gm_ws, gm_ws_t, w_out_b, w_in_g)


def _mm_all_slots(name, at, b, bw, tk, after):
    k1, s = at.shape
    nslot = b.shape[1] // bw

    def body(a_ref, b_ref, after_ref, o_ref):
        @pl.when(pl.program_id(0) == 0)
        def _():
            o_ref[...] = jnp.zeros_like(o_ref)

        t = _dot(a_ref[...], b_ref[...])
        for j in range(nslot):
            o_ref[j] += t[:, j * bw:(j + 1) * bw]

    return pl.pallas_call(
        body, name=name, grid=(s // tk,),
        in_specs=[pl.BlockSpec((k1, tk), lambda k: (0, k)), pl.BlockSpec((tk, nslot * bw), lambda k: (k, 0)), ANY],
        out_specs=_full((nslot, k1, bw)), out_shape=jax.ShapeDtypeStruct((nslot, k1, bw), F32),
        compiler_params=_arb(),
    )(at, b, after)


def _adam_math(w, g, m, v):
    m = ADAM_B1 * m + (1.0 - ADAM_B1) * g
    v = ADAM_B2 * v + (1.0 - ADAM_B2) * (g * g)
    m_hat = m / (1.0 - ADAM_B1 ** ADAM_STEP)
    v_hat = v / (1.0 - ADAM_B2 ** ADAM_STEP)
    delta = -ADAM_LR * (m_hat / (jnp.sqrt(v_hat) + ADAM_EPS) + ADAM_WD * w)
    return delta, m, v


def _row_block(rows, cols):
    tr = rows
    while tr * cols * 4 > ROW_BLOCK_BYTES and tr % (4 * SUB) == 0:
        tr //= 2
    return tr


def _adam3(name, w, g, m, v):
    _, rows, cols = w.shape
    tr = _row_block(rows, cols)

    def body(w_ref, g_ref, m_ref, v_ref, d_ref, mo_ref, vo_ref):
        d_ref[...], mo_ref[...], vo_ref[...] = _adam_math(w_ref[...], g_ref[...], m_ref[...], v_ref[...])

    spec = pl.BlockSpec((1, tr, cols), lambda i: (0, i, 0))
    return pl.pallas_call(
        body, name=name, grid=(rows // tr,), in_specs=[spec] * 4, out_specs=[spec] * 3,
        out_shape=[jax.ShapeDtypeStruct(w.shape, F32)] * 3, compiler_params=_arb(),
    )(w, g, m, v)


def _sum_adam(name, parts, w, m, v):
    n, rows, cols = parts.shape
    tr = _row_block(rows, cols)

    def body(p_ref, w_ref, m_ref, v_ref, g_ref, d_ref, mo_ref, vo_ref):
        g = p_ref[0].astype(F32)
        for k in range(1, n):
            g = g + p_ref[k].astype(F32)
        g_ref[0] = g
        d_ref[0], mo_ref[0], vo_ref[0] = _adam_math(w_ref[0], g, m_ref[0], v_ref[0])

    spec = pl.BlockSpec((1, tr, cols), lambda i: (0, i, 0))
    return pl.pallas_call(
        body, name=name, grid=(rows // tr,),
        in_specs=[pl.BlockSpec((n, tr, cols), lambda i: (0, i, 0))] + [spec] * 3, out_specs=[spec] * 4,
        out_shape=[jax.ShapeDtypeStruct(w.shape, F32)] * 4, compiler_params=_arb(),
    )(parts, w, m, v)


def _pair_add(name, g4, recv, core):
    _, _, rows, cols = g4.shape
    tr = _row_block(rows, cols)

    def body(c_ref, a_ref, b_ref, o_ref):
        o_ref[...] = (a_ref[...] + b_ref[...]).astype(BF16)

    return pl.pallas_call(
        body, name=name,
        grid_spec=pltpu.PrefetchScalarGridSpec(
            num_scalar_prefetch=1, grid=(4, rows // tr),
            in_specs=[pl.BlockSpec((None, None, tr, cols), lambda k, i, c_ref: (k, c_ref[0], i, 0)),
                      pl.BlockSpec((None, tr, cols), lambda k, i, c_ref: (k, i, 0))],
            out_specs=pl.BlockSpec((None, tr, cols), lambda k, i, c_ref: (k, i, 0))),
        out_shape=jax.ShapeDtypeStruct((4, rows, cols), BF16), compiler_params=_arb(2),
    )(core, g4, recv)


def _sum_small(rows_all, p_all, ws_all, bst_all, fw_all, cw_all):
    def body(a_ref, p_ref, ws_ref, bst_ref, fw_ref, cw_ref,
             g_b_ada, g_n1, g_mog, g_n2, g_gf, loss_cols, g_cb, g_clg, g_clb, g_glg, g_glb, g_ws, g_bs, fw_sum,
             cw_sum):
        def total(ref):
            t = ref[0]
            for k in range(1, NDEV):
                t = t + ref[k]
            return t

        a = total(a_ref)
        g_b_ada[...] = jnp.concatenate([a[k:k + 1, :] for k in range(6)], axis=1)
        g_n1[...] = a[6:7, :]
        g_mog[...] = a[7:8, :]
        g_n2[...] = a[8:9, :]
        g_gf[...] = a[9:10, :].reshape(D)
        loss_cols[...] = a[10:11, :]
        p = total(p_ref)
        for k, ref in zip((CB, CLG, CLB, GLG, GLB), (g_cb, g_clg, g_clb, g_glg, g_glb)):
            ref[...] = p[k:k + 1, :]
        g_ws[0] = total(ws_ref)
        g_bs[0] = jnp.transpose(total(bst_ref))[:NH, :]
        fw_sum[...] = total(fw_ref)
        cw_sum[...] = total(cw_ref)

    vec = lambda n: jax.ShapeDtypeStruct((1, n), F32)
    return pl.pallas_call(
        body, name="sum_small_grads",
        out_shape=[vec(6 * D), vec(D), vec(D), vec(D), jax.ShapeDtypeStruct((D,), F32), vec(D),
                   vec(DC), vec(DC), vec(DC), vec(DC), vec(DC),
                   jax.ShapeDtypeStruct((1, NH, CHUNK, CHUNK), F32), jax.ShapeDtypeStruct((1, NH, CHUNK), F32),
                   jax.ShapeDtypeStruct((SUB, 2 * PFF), F32), jax.ShapeDtypeStruct((CW_ROWS, DC), F32)],
    )(rows_all, p_all, ws_all, bst_all, fw_all, cw_all)


def _adam_small(quads):
    n = len(quads)

    def body(*refs):
        ins, outs = refs[:4 * n], refs[4 * n:]
        for q in range(n):
            w, g, m, v = (r[...] for r in ins[4 * q:4 * q + 4])
            outs[3 * q][...], outs[3 * q + 1][...], outs[3 * q + 2][...] = _adam_math(w, g, m, v)

    flat = [a for q in quads for a in q]
    outs = pl.pallas_call(
        body, name="adam_small",
        out_shape=[jax.ShapeDtypeStruct(q[0].shape, F32) for q in quads for _ in range(3)],
    )(*flat)
    return [tuple(outs[3 * q:3 * q + 3]) for q in range(n)]


def kernel(x, c, w_ada, b_ada, norm1_gain, w_in, conv_dw_w, conv_dw_b, conv_ln_g, conv_ln_b, gm_ln_g, gm_ln_b, gm_ws, gm_bs, mix_out_gain, w_out, norm2_gain, w_up, ffn_dw_w, ffn_dw_b, w_down, final_gain, loss_target, m_w_ada, m_b_ada, m_norm1_gain, m_w_in, m_conv_dw_w, m_conv_dw_b, m_conv_ln_g, m_conv_ln_b, m_gm_ln_g, m_gm_ln_b, m_gm_ws, m_gm_bs, m_mix_out_gain, m_w_out, m_norm2_gain, m_w_up, m_ffn_dw_w, m_ffn_dw_b, m_w_down, m_final_gain, v_w_ada, v_b_ada, v_norm1_gain, v_w_in, v_conv_dw_w, v_conv_dw_b, v_conv_ln_g, v_conv_ln_b, v_gm_ln_g, v_gm_ln_b, v_gm_ws, v_gm_bs, v_mix_out_gain, v_w_out, v_norm2_gain, v_w_up, v_ffn_dw_w, v_ffn_dw_b, v_w_down, v_final_gain):
    s = x.shape[1]
    ax, ay, ac = _place()
    me = 4 * ax + 2 * ay + ac
    n_ada = w_ada.shape[2]
    n_cw = conv_dw_w.shape[2]
    x2d = x[0]
    target = loss_target[0]
    pad_sh = lambda a: jnp.pad(a, [(0, 0)] * (a.ndim - 1) + [(0, PSH - NSH)])

    c_all, cw_all, fw_all = _all_gather("gather_small", [c, conv_dw_w[0], ffn_dw_w[0]])

    first_shards, c_all = lax.optimization_barrier(((w_in[0].astype(BF16), w_out[0].astype(BF16)), c_all))
    gather_in, token_a = _start_gather("gather_in_out", list(first_shards), me)
    c_all = c_all + token_a[0, 0]
    conv_w = jnp.pad(jnp.transpose(cw_all, (1, 0, 2)).reshape(KC, DC), ((0, CW_ROWS - KC), (0, 0)))
    ffn_w = jnp.transpose(pad_sh(fw_all), (1, 0, 2)).reshape(KF, 2 * PFF)
    ffn_b = pad_sh(ffn_dw_b.reshape(NDEV, NSH)).reshape(1, 2 * PFF)
    ffn_wb = jnp.concatenate([ffn_w, ffn_b, jnp.zeros((SUB - KF - 1, 2 * PFF), F32)], axis=0)

    b_cols = lax.dynamic_slice(b_ada, (0, me * n_ada), (1, n_ada))
    (mod_all,) = _all_gather("gather_mod", [_mod_part(c_all, w_ada, b_cols)])
    up_t = lambda a: jnp.swapaxes(a, 1, 2)
    w_up_shard = jnp.pad(up_t(w_up)[0].astype(BF16), ((0, PSH - NSH), (0, 0)))
    shards, mod_all = lax.optimization_barrier(((w_up_shard, w_down[0].astype(BF16)), mod_all))
    gather_ffn, token_c = _start_gather("gather_up_down", list(shards), me)
    mod = lax.dynamic_index_in_dim(mod_all, me, axis=1, keepdims=False).reshape(6, D)
    sh1, sc1, gt1, sh2, sc2, gt2 = [mod[k:k + 1] for k in range(6)]
    vecs = jnp.concatenate([norm1_gain, sh1, sc1, gt1, norm2_gain, sh2, sc2, gt2, mix_out_gain,
                            final_gain.reshape(1, D), jnp.zeros((6, D), F32)], axis=0)
    vecs = vecs + token_c[0, 0]
    v512 = jnp.concatenate([conv_dw_b, conv_ln_g, conv_ln_b, gm_ln_g, gm_ln_b, jnp.zeros((3, DC), F32)], axis=0)
    bs_exp = jnp.repeat(jnp.transpose(gm_bs[0]), HD, axis=1)
    gm_ws_t = jnp.swapaxes(gm_ws[0], 1, 2)

    tm_big, tm = min(TILE_BIG, s), min(TILE, s)
    w_in_g, w_out_g = _finish_gather("gather_in_out", gather_in, vecs)
    w_out_b = w_out_g.reshape(D, D)
    z, a0, h1_t = _fwd_in(x2d, vecs, w_in_g, tm_big)
    a1, sp, x2, o1, h2 = _fwd_mid(a0, z, x2d, vecs, v512, conv_w, gm_ws, bs_exp, w_out_b, tm)
    w_up_t, w_down_g = _finish_gather("gather_up_down", gather_ffn, h2)
    w_down_p = jnp.pad(w_down_g.reshape(4, NSH, D), ((0, 0), (0, PSH - NSH), (0, 0)))
    up_pre, vg, dx3, acc_f = _fwd_ffn(h2, x2, target, vecs, ffn_wb, w_up_t, w_down_p, tm)

    core = ac.reshape(1).astype(jnp.int32)
    mychip = 2 * ax + ay

    def to_pairs(named):
        g4s = [g.reshape((4, 2) + g.shape[1:]) for _, g in named]
        from_sibling = _sibling_swap("rs_sibling_" + named[0][0], g4s)
        return [_pair_add("rs_pair_add_" + t[0], g4, rv, core) for t, g4, rv in zip(named, g4s, from_sibling)]

    dh2, dw_up, dw_down, acc_fw = _bwd_ffn(dx3, up_pre, vg, h2, vecs, ffn_wb, w_up_t, w_down_p, tm_big)
    acc_fw = jnp.transpose(acc_fw, (2, 0, 1, 3)).reshape(SUB, 2 * PFF)
    exchange_ffn, token_x = _start_exchange("rs_chips_ffn", to_pairs(
        [("w_up", dw_up), ("w_down", dw_down.reshape(NDEV, w_down.shape[1], D))]), mychip)
    gx, dz, y_t, do1, acc_m, acc_p, dcw, dws, dbs_t = _bwd_mid(
        dh2, dx3, x2, x2d, o1, z, a0, a1, sp, vecs + token_x[0, 0], v512, conv_w, gm_ws[0], gm_ws_t, w_out_b, w_in_g, tm)
    rows = jnp.concatenate([acc_m[1:3], acc_m[0:1], acc_m[5:7], acc_f[2:3], acc_m[3:5], acc_m[7:8], acc_f[0:2],
                            jnp.zeros((5, D), F32)], axis=0)
    small_gather, token_s = _start_gather("gather_small_grads", [rows, acc_p, dws, dbs_t, acc_fw, dcw], me)
    dw_in = _mm_all_slots("dw_in", h1_t, dz, w_in.shape[2], min(TK_IN, s), token_s)
    small_pass, token_p = _pass_gather("gather_small_grads", small_gather, dw_in)
    dw_out = _mm_all_slots("dw_out", y_t, do1, D, min(TK_OUT, s), token_p).reshape(NDEV, w_out.shape[1], D)
    exchange_mix, token_m = _start_exchange("rs_chips_mix", to_pairs([("w_in", dw_in), ("w_out", dw_out)]), mychip)

    rows_all, p_all, ws_all, bst_all, fwg_all, cwg_all = _end_gather("gather_small_grads", small_pass, token_m)
    (g_b_ada, g_n1, g_mog, g_n2, g_gf, loss_cols, g_cb, g_clg, g_clb, g_glg, g_glb, g_ws, g_bs, fw_sum,
     cw_sum) = _sum_small(rows_all, p_all, ws_all, bst_all, fwg_all, cwg_all)
    loss = jnp.sum(loss_cols)
    g_fb = fw_sum[3].reshape(NDEV, PSH)[:, :NSH].reshape(ffn_dw_b.shape)
    g_fw = lax.dynamic_index_in_dim(fw_sum[:KF].reshape(KF, NDEV, PSH), me, axis=1, keepdims=False)[:, :NSH]
    g_fw = g_fw.reshape(ffn_dw_w.shape)
    g_cw = lax.dynamic_slice(cw_sum, (0, me * n_cw), (KC, n_cw)).reshape(conv_dw_w.shape)
    small = [
        (b_ada, g_b_ada, m_b_ada, v_b_ada), (norm1_gain, g_n1, m_norm1_gain, v_norm1_gain),
        (conv_dw_w, g_cw, m_conv_dw_w, v_conv_dw_w), (conv_dw_b, g_cb, m_conv_dw_b, v_conv_dw_b),
        (conv_ln_g, g_clg, m_conv_ln_g, v_conv_ln_g), (conv_ln_b, g_clb, m_conv_ln_b, v_conv_ln_b),
        (gm_ln_g, g_glg, m_gm_ln_g, v_gm_ln_g), (gm_ln_b, g_glb, m_gm_ln_b, v_gm_ln_b),
        (gm_ws, g_ws, m_gm_ws, v_gm_ws), (gm_bs, g_bs, m_gm_bs, v_gm_bs),
        (mix_out_gain, g_mog, m_mix_out_gain, v_mix_out_gain), (norm2_gain, g_n2, m_norm2_gain, v_norm2_gain),
        (ffn_dw_w, g_fw, m_ffn_dw_w, v_ffn_dw_w), (ffn_dw_b, g_fb, m_ffn_dw_b, v_ffn_dw_b),
        (final_gain, g_gf, m_final_gain, v_final_gain)]
    small_out = _adam_small(small)
    res = {}
    for name, q, o in zip(("b_ada", "norm1_gain", "conv_dw_w", "conv_dw_b", "conv_ln_g", "conv_ln_b", "gm_ln_g",
                           "gm_ln_b", "gm_ws", "gm_bs", "mix_out_gain", "norm2_gain", "ffn_dw_w", "ffn_dw_b",
                           "final_gain"), small, small_out):
        res[name] = (q[1],) + o

    dmod_all = rows_all[:, :6].reshape(NDEV, 6 * D)
    dm_cols = lax.dynamic_slice(dmod_all, (0, me * n_ada), (NDEV, n_ada))
    g_ada = _ada_grad(jnp.transpose(c_all[:, 0, :]), dm_cols)
    res["w_ada"] = (g_ada,) + tuple(_adam3("adam_ada", w_ada, g_ada, m_w_ada, v_w_ada))

    big = [("w_up", up_t(w_up), up_t(m_w_up), up_t(v_w_up)), ("w_down", w_down, m_w_down, v_w_down),
           ("w_in", w_in, m_w_in, v_w_in), ("w_out", w_out, m_w_out, v_w_out)]
    from_chips = list(_finish_exchange("rs_chips_ffn", exchange_ffn, res["w_ada"][1]))
    for t, parts in zip(big[:2], from_chips):
        res[t[0]] = tuple(_sum_adam("rs_sum_adam_" + t[0], parts, t[1], t[2], t[3]))
    res["w_up"] = tuple(up_t(a) for a in res["w_up"])
    from_chips = list(_finish_exchange("rs_chips_mix", exchange_mix, res["w_down"][1]))
    for t, parts in zip(big[2:], from_chips):
        res[t[0]] = tuple(_sum_adam("rs_sum_adam_" + t[0], parts, t[1], t[2], t[3]))

    order = ("w_ada", "b_ada", "norm1_gain", "w_in", "conv_dw_w", "conv_dw_b", "conv_ln_g", "conv_ln_b", "gm_ln_g",
             "gm_ln_b", "gm_ws", "gm_bs", "mix_out_gain", "w_out", "norm2_gain", "w_up", "ffn_dw_w", "ffn_dw_b",
             "w_down", "final_gain")
    return (loss, gx.reshape(x.shape), *[res[n][0] for n in order], *[res[n][1] for n in order],
            *[res[n][2] for n in order], *[res[n][3] for n in order])
```

```python
import functools

import jax
import jax.numpy as jnp
from jax import lax
from jax.experimental import pallas as pl
from jax.experimental.pallas import tpu as pltpu

F32 = jnp.float32
BF16 = jnp.bfloat16
NDEV = 8
D = 1024
DC = 512
DFF = 2816
NSH = 704
PSH = 768
PFF = 4 * PSH
KC = 31
KF = 3
CHUNK = 128
NH = 8
HD = 64
HALO_C = 32
HALO_F = 8
LANES = 128
SUB = 8
VROWS = 16
CW_ROWS = 32
TILE_BIG = 512
ROW_BLOCK_BYTES = 2 << 20
TILE = 256
TK_IN = 2048
TK_OUT = 4096
RMS_EPS = 1e-6
LN_EPS = 1e-5
ADAM_LR = 0.001
ADAM_B1 = 0.9
ADAM_B2 = 0.999
ADAM_EPS = 1e-08
ADAM_WD = 0.01
ADAM_STEP = 10
GELU_K = 0.7978845608028654
GELU_C = 0.044715

MESH = pl.DeviceIdType.MESH
ANY = pl.BlockSpec(memory_space=pl.ANY)

G1, SH1, SC1, GT1, G2, SH2, SC2, GT2, MOG, GF = range(10)
CB, CLG, CLB, GLG, GLB = range(5)


def _full(shape):
    return pl.BlockSpec(shape, lambda *_: (0,) * len(shape))


def _arb(n=1):
    return pltpu.CompilerParams(dimension_semantics=("arbitrary",) * n)


def _row(ref, r):
    return ref[pl.ds(r, 1), :]


def _colsum(v):
    return jnp.sum(v, axis=0, keepdims=True)


def _rowmean(v):
    return jnp.mean(v, axis=-1, keepdims=True)


def _rms(x):
    r = lax.rsqrt(_rowmean(x * x) + RMS_EPS)
    return x * r, r


def _rms_bwd(dxn, xn, r):
    return r * (dxn - xn * _rowmean(dxn * xn))


def _ln(x):
    mu = _rowmean(x)
    xc = x - mu
    rstd = lax.rsqrt(_rowmean(xc * xc) + LN_EPS)
    return xc * rstd, rstd


def _ln_bwd(dxh, xhat, rstd):
    return rstd * (dxh - _rowmean(dxh) - xhat * _rowmean(dxh * xhat))


def _sigmoid(x):
    return 0.5 * jnp.tanh(0.5 * x) + 0.5


def _gelu(x):
    t = jnp.tanh(GELU_K * (x + GELU_C * x * x * x))
    return 0.5 * x * (1.0 + t), t


def _gelu_grad(x, t):
    return 0.5 * (1.0 + t) + 0.5 * x * (1.0 - t * t) * (GELU_K * (1.0 + 3.0 * GELU_C * x * x))


def _dot(a, b):
    return jnp.dot(a, b, preferred_element_type=F32)


def _dot_nt(a, b):
    return lax.dot_general(a, b, (((1,), (1,)), ((), ())), preferred_element_type=F32)


def _shift_up(e, s):
    n = e.shape[0]
    return pltpu.roll(e, (n - s) % n, 0)


def _place():
    return lax.axis_index("x"), lax.axis_index("y"), lax.axis_index("c")


def _all_gather(name, xs):
    n = len(xs)

    def body(*refs):
        x_refs, out_refs = refs[:n], refs[n:2 * n]
        send_sems, recv_sems, local_sems = refs[2 * n:]
        x, y, c = _place()
        me, sibling = (x, y, c), (x, y, 1 - c)
        chips = [(1 - x, y), (x, 1 - y), (1 - x, 1 - y)]

        def copy(a, k, block, to, own=False):
            px, py, pc = block
            slot = out_refs[a].at[4 * px + 2 * py + pc]
            return pltpu.make_async_remote_copy(
                src_ref=x_refs[a] if own else slot, dst_ref=slot,
                send_sem=send_sems.at[7 * a + k], recv_sem=recv_sems.at[7 * a + k], device_id=to, device_id_type=MESH)

        mine = [pltpu.make_async_copy(x_refs[a], out_refs[a].at[4 * x + 2 * y + c], local_sems.at[a]) for a in range(n)]
        for cp in mine:
            cp.start()
        first = []
        for a in range(n):
            first.append(copy(a, 0, me, sibling, own=True))
            first += [copy(a, 1 + j, me, (*chip, c), own=True) for j, chip in enumerate(chips)]
        for cp in first:
            cp.start()
        passed = []
        for j, chip in enumerate(chips):
            for a in range(n):
                copy(a, 1 + j, (*chip, c), me).wait_recv()
                cp = copy(a, 4 + j, (*chip, c), sibling)
                cp.start()
                passed.append(cp)
        for a in range(n):
            copy(a, 0, sibling, me).wait_recv()
            for j, chip in enumerate(chips):
                copy(a, 4 + j, (*chip, 1 - c), me).wait_recv()
        for cp in first + passed:
            cp.wait_send()
        for cp in mine:
            cp.wait()

    return pl.pallas_call(
        body, name=name, out_shape=[jax.ShapeDtypeStruct((NDEV,) + a.shape, a.dtype) for a in xs],
        in_specs=[ANY] * n, out_specs=[ANY] * n,
        scratch_shapes=[pltpu.SemaphoreType.DMA((7 * n,)), pltpu.SemaphoreType.DMA((7 * n,)),
                        pltpu.SemaphoreType.DMA((n,))],
    )(*xs)


def _sibling_swap(name, g4s):
    n = len(g4s)

    def body(*refs):
        g_refs, out_refs = refs[:n], refs[n:2 * n]
        send_sems, recv_sems = refs[2 * n:]
        x, y, c = _place()
        cps = [pltpu.make_async_remote_copy(
            src_ref=g_refs[a].at[k, 1 - c], dst_ref=out_refs[a].at[k],
            send_sem=send_sems.at[4 * a + k], recv_sem=recv_sems.at[4 * a + k],
            device_id=(x, y, 1 - c), device_id_type=MESH) for a in range(n) for k in range(4)]
        for cp in cps:
            cp.start()
        for cp in cps:
            cp.wait()

    return pl.pallas_call(
        body, name=name, out_shape=[jax.ShapeDtypeStruct((4,) + g.shape[2:], g.dtype) for g in g4s],
        in_specs=[ANY] * n, out_specs=[ANY] * n,
        scratch_shapes=[pltpu.SemaphoreType.DMA((4 * n,)), pltpu.SemaphoreType.DMA((4 * n,))],
    )(*g4s)


HBM = pl.BlockSpec(memory_space=pltpu.HBM)
SEM = pl.BlockSpec(memory_space=pltpu.SEMAPHORE)
EFFECT = pltpu.SideEffectType.DATAFLOW_SIDE_EFFECTING


def _in_hbm(a):
    return pltpu.with_memory_space_constraint(a, pltpu.HBM)


def _split_start(name, bufs, copies):
    n = len(bufs)

    def body(*refs):
        for cp in copies(refs[:n], refs[n], refs[n + 1]):
            cp.start()
        refs[-1][...] = jnp.zeros_like(refs[-1])

    out = pl.pallas_call(
        body, name=name,
        out_shape=(pltpu.SemaphoreType.DMA((copies.count,)), pltpu.SemaphoreType.DMA((copies.count,)),
                   *[pltpu.HBM(a.shape, a.dtype) for a in bufs], jax.ShapeDtypeStruct((SUB, LANES), F32)),
        in_specs=[HBM] * n, out_specs=(SEM, SEM, *[HBM] * n, pl.BlockSpec(memory_space=pltpu.VMEM)),
        input_output_aliases={i: 2 + i for i in range(n)},
        compiler_params=pltpu.CompilerParams(has_side_effects=EFFECT),
    )(*[_in_hbm(a) for a in bufs])
    return (out[0], out[1], list(out[2:2 + n])), out[-1]


def _split_wait(name, handle, copies, after):
    send_sems, recv_sems, bufs = handle
    n = len(bufs)

    def body(*refs):
        for cp in copies(refs[:n], refs[n], refs[n + 1]):
            cp.wait_send()
            cp.wait_recv()

    out = pl.pallas_call(
        body, name=name, out_shape=tuple(pltpu.HBM(a.shape, a.dtype) for a in bufs),
        in_specs=[HBM] * n + [SEM, SEM, pl.BlockSpec(memory_space=pl.ANY)], out_specs=tuple([HBM] * n),
        input_output_aliases={i: i for i in range(n)},
        compiler_params=pltpu.CompilerParams(has_side_effects=EFFECT),
    )(*bufs, send_sems, recv_sems, after)
    return list(out)


class _GatherFirstCopies:
    def __init__(self, n):
        self.n, self.count = n, 4 * n

    def __call__(self, refs, send_sems, recv_sems):
        x, y, c = _place()
        peers = [(x, y, 1 - c), (1 - x, y, c), (x, 1 - y, c), (1 - x, 1 - y, c)]
        return [pltpu.make_async_remote_copy(
            src_ref=refs[a], dst_ref=refs[self.n + a].at[4 * x + 2 * y + c],
            send_sem=send_sems.at[4 * a + k], recv_sem=recv_sems.at[4 * a + k], device_id=peer, device_id_type=MESH)
            for a in range(self.n) for k, peer in enumerate(peers)]


class _GatherPassCopies:
    def __init__(self, n):
        self.n, self.count = n, 3 * n

    def __call__(self, refs, send_sems, recv_sems):
        x, y, c = _place()
        cps = []
        for a in range(self.n):
            for j, (px, py) in enumerate([(1 - x, y), (x, 1 - y), (1 - x, 1 - y)]):
                slot = refs[a].at[4 * px + 2 * py + c]
                cps.append(pltpu.make_async_remote_copy(
                    src_ref=slot, dst_ref=slot, send_sem=send_sems.at[3 * a + j], recv_sem=recv_sems.at[3 * a + j],
                    device_id=(x, y, 1 - c), device_id_type=MESH))
        return cps


class _ExchangeCopies:
    def __init__(self, n):
        self.n, self.count = n, 3 * n

    def __call__(self, refs, send_sems, recv_sems):
        x, y, c = _place()
        cps = []
        for a in range(self.n):
            for j, (px, py) in enumerate([(1 - x, y), (x, 1 - y), (1 - x, 1 - y)]):
                cps.append(pltpu.make_async_remote_copy(
                    src_ref=refs[a].at[2 * px + py], dst_ref=refs[self.n + a].at[2 * x + y],
                    send_sem=send_sems.at[3 * a + j], recv_sem=recv_sems.at[3 * a + j],
                    device_id=(px, py, c), device_id_type=MESH))
        return cps


def _own_slot(nslot, src, index):
    land = lax.empty((nslot,) + src.shape, src.dtype)
    return lax.dynamic_update_slice(land, src[None], (index,) + (0,) * src.ndim)


def _start_gather(tag, xs, me):
    lands = [_own_slot(NDEV, a, me) for a in xs]
    return _split_start(tag + "_start", list(xs) + lands, _GatherFirstCopies(len(xs)))


def _pass_gather(tag, handle, after):
    n = len(handle[2]) // 2
    lands = _split_wait(tag + "_wait", handle, _GatherFirstCopies(n), after)[n:]
    return _split_start(tag + "_pass", lands, _GatherPassCopies(n))


def _end_gather(tag, passing, after):
    return _split_wait(tag + "_pass_wait", passing, _GatherPassCopies(len(passing[2])), after)


def _finish_gather(tag, handle, after):
    passing, token = _pass_gather(tag, handle, after)
    return _end_gather(tag, passing, token)


def _start_exchange(tag, hs, mychip):
    lands = [_own_slot(4, lax.dynamic_index_in_dim(h, mychip, 0, keepdims=False), mychip) for h in hs]
    return _split_start(tag + "_start", list(hs) + lands, _ExchangeCopies(len(hs)))


def _finish_exchange(tag, handle, after):
    n = len(handle[2]) // 2
    return _split_wait(tag + "_wait", handle, _ExchangeCopies(n), after)[n:]


def _mod_part(c_all, w_ada, b_cols):
    ncol = w_ada.shape[2]

    def body(c_ref, w_ref, b_ref, o_ref):
        cv = c_ref[:, 0, :]
        ca = cv * _sigmoid(cv)
        o_ref[...] = _dot(ca.astype(BF16), w_ref[0].astype(BF16)) + b_ref[...]

    return pl.pallas_call(body, name="mod_part", out_shape=jax.ShapeDtypeStruct((NDEV, ncol), F32))(
        c_all, w_ada, b_cols)


def _ada_grad(c_all_t, dmod_cols):
    ncol = dmod_cols.shape[1]

    def body(ct_ref, dm_ref, o_ref):
        ct = ct_ref[...]
        ca = ct * _sigmoid(ct)
        acc = jnp.zeros((D, ncol), F32)
        for b in range(NDEV):
            acc = acc + ca[:, b:b + 1] * dm_ref[pl.ds(b, 1), :]
        o_ref[0] = acc

    return pl.pallas_call(body, name="ada_grad", out_shape=jax.ShapeDtypeStruct((1, D, ncol), F32))(
        c_all_t, dmod_cols)


def _fwd_in(x2d, vecs, w_in_g, tm):
    s = x2d.shape[0]
    nc = w_in_g.shape[2]

    def body(x_ref, v_ref, w_ref, z_ref, a0_ref, h1t_ref):
        xn, _ = _rms(x_ref[...])
        h = (xn * _row(v_ref, G1)) * (1.0 + _row(v_ref, SC1)) + _row(v_ref, SH1)
        hb = h.astype(BF16)
        h1t_ref[...] = hb.T
        for d in range(NDEV):
            z_ref[:, pl.ds(d * nc, nc)] = _dot(hb, w_ref[d])
        a0_ref[...] = z_ref[:, :DC] * _sigmoid(z_ref[:, DC:2 * DC])

    return pl.pallas_call(
        body, name="fwd_in", grid=(s // tm,),
        in_specs=[pl.BlockSpec((tm, D), lambda i: (i, 0)), _full((VROWS, D)), _full((NDEV, D, nc))],
        out_specs=[pl.BlockSpec((tm, 4 * DC), lambda i: (i, 0)), pl.BlockSpec((tm, DC), lambda i: (i, 0)),
                   pl.BlockSpec((D, tm), lambda i: (0, i))],
        out_shape=[jax.ShapeDtypeStruct((s, 4 * DC), F32), jax.ShapeDtypeStruct((s, DC), F32),
                   jax.ShapeDtypeStruct((D, s), BF16)],
        compiler_params=_arb(),
    )(x2d, vecs, w_in_g)


def _causal_mask(lower):
    r = lax.broadcasted_iota(jnp.int32, (CHUNK, CHUNK), 0)
    c = lax.broadcasted_iota(jnp.int32, (CHUNK, CHUNK), 1)
    return (r >= c) if lower else (r <= c)


def _first_head_lanes():
    return lax.broadcasted_iota(jnp.int32, (CHUNK, CHUNK), 1) < HD


def _fwd_mid(a0, z, x2d, vecs, v512, conv_w, gm_ws, bs_exp, w_out_b, tm):
    s = x2d.shape[0]
    hb = tm // HALO_C

    def body(a0_ref, halo_ref, zg_ref, x_ref, v_ref, p_ref, cw_ref, ws_ref, bs_ref, wo_ref,
             a1_ref, sp_ref, x2_ref, o1_ref, h2_ref):
        i = pl.program_id(0)
        for c0 in range(0, DC, LANES):
            cols = pl.ds(c0, LANES)
            halo = halo_ref[:, cols]
            e = jnp.concatenate([jnp.where(i > 0, halo, jnp.zeros_like(halo)), a0_ref[:, cols]], axis=0)
            acc = jnp.broadcast_to(p_ref[pl.ds(CB, 1), cols], (tm, LANES))
            for k in range(KC):
                acc = acc + _shift_up(e, HALO_C - (KC - 1) + k)[:tm, :] * cw_ref[pl.ds(k, 1), cols]
            a1_ref[:, cols] = acc
        xh, _ = _ln(a1_ref[...])
        a2 = xh * _row(p_ref, CLG) + _row(p_ref, CLB)
        a3 = a2 * _sigmoid(a2)
        gu, _ = _gelu(zg_ref[:, :DC])
        gvg, _ = _gelu(zg_ref[:, DC:])
        vh, _ = _ln(gvg)
        gvn = (vh * _row(p_ref, GLG) + _row(p_ref, GLB)).astype(BF16)
        low = _causal_mask(True)
        first = _first_head_lanes()
        wm = [jnp.where(low, ws_ref[0, h], 0.0).astype(BF16) for h in range(NH)]
        for n in range(tm // CHUNK):
            for p in range(NH // 2):
                v = gvn[n * CHUNK:(n + 1) * CHUNK, p * CHUNK:(p + 1) * CHUNK]
                blk = jnp.where(first, _dot(wm[2 * p], v), _dot(wm[2 * p + 1], v))
                sp_ref[pl.ds(n * CHUNK, CHUNK), pl.ds(p * CHUNK, CHUNK)] = blk + bs_ref[:, pl.ds(p * CHUNK, CHUNK)]
        g = gu * sp_ref[...]
        an, _ = _rms(a3)
        gn, _ = _rms(g)
        mog = _row(v_ref, MOG)
        y = jnp.concatenate([an * mog[:, :DC], gn * mog[:, DC:]], axis=1).astype(BF16)
        o1 = _dot(y, wo_ref[...])
        o1_ref[...] = o1
        x2 = x_ref[...] + _row(v_ref, GT1) * o1
        x2_ref[...] = x2
        xn2, _ = _rms(x2)
        h2 = (xn2 * _row(v_ref, G2)) * (1.0 + _row(v_ref, SC2)) + _row(v_ref, SH2)
        h2_ref[...] = h2.astype(BF16)

    tile = lambda w: pl.BlockSpec((tm, w), lambda i: (i, 0))
    return pl.pallas_call(
        body, name="fwd_mid", grid=(s // tm,),
        in_specs=[tile(DC), pl.BlockSpec((HALO_C, DC), lambda i: (jnp.maximum(i * hb - 1, 0), 0)),
                  pl.BlockSpec((tm, 2 * DC), lambda i: (i, 1)), tile(D), _full((VROWS, D)), _full((SUB, DC)),
                  _full((CW_ROWS, DC)), _full((1, NH, CHUNK, CHUNK)), _full((CHUNK, DC)), _full((D, D))],
        out_specs=[tile(DC), tile(DC), tile(D), tile(D), tile(D)],
        out_shape=[jax.ShapeDtypeStruct((s, DC), F32), jax.ShapeDtypeStruct((s, DC), F32),
                   jax.ShapeDtypeStruct((s, D), F32), jax.ShapeDtypeStruct((s, D), F32),
                   jax.ShapeDtypeStruct((s, D), BF16)],
        compiler_params=_arb(),
    )(a0, a0, z, x2d, vecs, v512, conv_w, gm_ws, bs_exp, w_out_b)


def _ffn_conv(fw_ref, cols, p2, p1, pre):
    return (fw_ref[pl.ds(3, 1), cols] + fw_ref[pl.ds(0, 1), cols] * p2
            + fw_ref[pl.ds(1, 1), cols] * p1 + fw_ref[pl.ds(2, 1), cols] * pre)


def _fwd_ffn(h2, x2, target, vecs, ffn_wb, w_up_t, w_down_p, tm):
    s = x2.shape[0]

    def body(h2_ref, x2_ref, t_ref, v_ref, fw_ref, wu_hbm, wd_hbm,
             up_ref, vg_ref, dx3_ref, acc_ref, wu, wd, carry, stage):
        i = pl.program_id(0)

        @pl.when(i == 0)
        def _():
            for sh in range(NDEV):
                pltpu.sync_copy(wu_hbm.at[sh], stage)
                wu[sh] = stage[...].T
            pltpu.sync_copy(wd_hbm, wd)
            carry[...] = jnp.zeros_like(carry)
            acc_ref[...] = jnp.zeros_like(acc_ref)

        h2v = h2_ref[...]
        o2 = jnp.zeros((tm, D), F32)
        for j in range(4):
            conv = []
            for sh in (j, 4 + j):
                cols = pl.ds(sh * PSH, PSH)
                pre = _dot(h2v, wu[sh])
                up_ref[:, cols] = pre.astype(BF16)
                e = jnp.concatenate([carry[:, cols], pre], axis=0)
                carry[:, cols] = pre[tm - HALO_F:, :]
                conv.append(_ffn_conv(fw_ref, cols, pltpu.roll(e, 2, 0)[HALO_F:, :],
                                      pltpu.roll(e, 1, 0)[HALO_F:, :], pre))
            val, gate = conv
            vg_ref[:, pl.ds(j * PSH, PSH)] = val.astype(BF16)
            vg_ref[:, pl.ds((4 + j) * PSH, PSH)] = gate.astype(BF16)
            f = ((gate * _sigmoid(gate)) * val).astype(BF16)
            o2 = o2 + _dot(f, wd[j])
        x3 = x2_ref[...] + _row(v_ref, GT2) * o2
        xn3, r3 = _rms(x3)
        gf = _row(v_ref, GF)
        diff = xn3 * gf - t_ref[...]
        acc_ref[pl.ds(1, 1), :] += _colsum(diff * diff) * (0.5 / D)
        dout = diff * (1.0 / D)
        acc_ref[pl.ds(0, 1), :] += _colsum(dout * xn3)
        dx3 = _rms_bwd(dout * gf, xn3, r3)
        dx3_ref[...] = dx3
        acc_ref[pl.ds(2, 1), :] += _colsum(dx3 * o2)

    tile = lambda w: pl.BlockSpec((tm, w), lambda i: (i, 0))
    return pl.pallas_call(
        body, name="fwd_ffn", grid=(s // tm,),
        in_specs=[tile(D), tile(D), tile(D), _full((VROWS, D)), _full((SUB, 2 * PFF)), ANY, ANY],
        out_specs=[tile(2 * PFF), tile(2 * PFF), tile(D), _full((SUB, D))],
        out_shape=[jax.ShapeDtypeStruct((s, 2 * PFF), BF16), jax.ShapeDtypeStruct((s, 2 * PFF), BF16),
                   jax.ShapeDtypeStruct((s, D), F32), jax.ShapeDtypeStruct((SUB, D), F32)],
        scratch_shapes=[pltpu.VMEM((NDEV, D, PSH), BF16), pltpu.VMEM((4, PSH, D), BF16),
                        pltpu.VMEM((HALO_F, 2 * PFF), F32), pltpu.VMEM((PSH, D), BF16)],
        compiler_params=_arb(),
    )(h2, x2, target, vecs, ffn_wb, w_up_t, w_down_p)


def _bwd_ffn(dx3, up_pre, vg, h2, vecs, ffn_wb, w_up_t, w_down_p, tm):
    s = dx3.shape[0]
    nt = s // tm

    def body(dx3_ref, up_ref, upg_ref, val_ref, gate_ref, h2_ref, v_ref, fw_ref, fwg_ref, wu_ref, wug_ref, wd_ref,
             dh2_ref, dwu_ref, dwd_ref, accf_ref, carry):
        i = pl.program_id(1)

        @pl.when(i == 0)
        def _():
            for ref in (carry, dwu_ref, dwd_ref, accf_ref):
                ref[...] = jnp.zeros_like(ref)

        do2 = (dx3_ref[...] * _row(v_ref, GT2)).astype(BF16)
        df = _dot_nt(do2, wd_ref[...])
        val = val_ref[...].astype(F32)
        gate = gate_ref[...].astype(F32)
        sg = _sigmoid(gate)
        sl = gate * sg
        f_t = (sl * val).astype(BF16).T
        dwd_ref[...] += _dot(f_t, do2)[:NSH, :]
        dups = (df * sl, df * val * (sg * (1.0 + gate * (1.0 - sg))))
        h2v = h2_ref[...]
        dh2 = jnp.zeros((tm, D), F32)
        for half, (dup, pre_ref, w_ref, wmat_ref) in enumerate(
                zip(dups, (up_ref, upg_ref), (fw_ref, fwg_ref), (wu_ref, wug_ref))):
            cols = pl.ds(half * PSH, PSH)
            e = jnp.concatenate([dup, carry[:, cols]], axis=0)
            carry[:, cols] = dup[:HALO_F, :]
            d1 = _shift_up(e, 1)[:tm, :]
            d2 = _shift_up(e, 2)[:tm, :]
            pre = pre_ref[...].astype(F32)
            accf_ref[half, pl.ds(3, 1), :] += _colsum(dup)
            accf_ref[half, pl.ds(0, 1), :] += _colsum(d2 * pre)
            accf_ref[half, pl.ds(1, 1), :] += _colsum(d1 * pre)
            accf_ref[half, pl.ds(2, 1), :] += _colsum(dup * pre)
            dpre = (_row(w_ref, 0) * d2 + _row(w_ref, 1) * d1 + _row(w_ref, 2) * dup).astype(BF16)
            dwu_ref[half] += _dot(dpre.T, h2v)[:NSH, :]
            dh2 = dh2 + _dot(dpre, wmat_ref[...])
        dh2_ref[...] = dh2

    rev = lambda j, i: nt - 1 - i
    in_specs = [
        pl.BlockSpec((tm, D), lambda j, i: (rev(j, i), 0)),
        pl.BlockSpec((tm, PSH), lambda j, i: (rev(j, i), j)), pl.BlockSpec((tm, PSH), lambda j, i: (rev(j, i), 4 + j)),
        pl.BlockSpec((tm, PSH), lambda j, i: (rev(j, i), j)), pl.BlockSpec((tm, PSH), lambda j, i: (rev(j, i), 4 + j)),
        pl.BlockSpec((tm, D), lambda j, i: (rev(j, i), 0)), _full((VROWS, D)),
        pl.BlockSpec((SUB, PSH), lambda j, i: (0, j)), pl.BlockSpec((SUB, PSH), lambda j, i: (0, 4 + j)),
        pl.BlockSpec((None, PSH, D), lambda j, i: (j, 0, 0)), pl.BlockSpec((None, PSH, D), lambda j, i: (4 + j, 0, 0)),
        pl.BlockSpec((None, PSH, D), lambda j, i: (j, 0, 0))]
    dh2, dw_up, dw_down, accf = pl.pallas_call(
        body, name="bwd_ffn", grid=(4, nt), in_specs=in_specs,
        out_specs=[pl.BlockSpec((None, tm, D), lambda j, i: (j, rev(j, i), 0)),
                   pl.BlockSpec((2, None, NSH, D), lambda j, i: (0, j, 0, 0)),
                   pl.BlockSpec((None, NSH, D), lambda j, i: (j, 0, 0)),
                   pl.BlockSpec((2, None, SUB, PSH), lambda j, i: (0, j, 0, 0))],
        out_shape=[jax.ShapeDtypeStruct((4, s, D), F32), jax.ShapeDtypeStruct((2, 4, NSH, D), F32),
                   jax.ShapeDtypeStruct((4, NSH, D), F32), jax.ShapeDtypeStruct((2, 4, SUB, PSH), F32)],
        scratch_shapes=[pltpu.VMEM((HALO_F, 2 * PSH), F32)],
        compiler_params=_arb(2),
    )(dx3, up_pre, up_pre, vg, vg, h2, vecs, ffn_wb, ffn_wb, w_up_t, w_up_t, w_down_p)
    return dh2, dw_up.reshape(NDEV, NSH, D), dw_down, accf


def _bwd_mid(dh2, dx3, x2, x2d, o1, z, a0, a1, sp, vecs, v512, conv_w, gm_ws, gm_ws_t, w_out_b, w_in_g, tm):
    s = x2d.shape[0]
    nt = s // tm
    nc = w_in_g.shape[2]

    def body(dh2a_ref, dh2b_ref, dh2c_ref, dh2d_ref, dx3_ref, x2_ref, x_ref, o1_ref, z_ref, a0_ref, a1_ref, sp_ref,
             v_ref, p_ref, cw_ref, ws_ref, wst_ref, wo_ref, wi_ref, gx_ref, dz_ref, yt_ref, do1_ref, acc_ref, accp_ref,
             dcw_ref, dws_ref, dbst_ref, dbs_s, carry, da1_s, dsp_s, dgvn_s):
        i = pl.program_id(0)

        @pl.when(i == 0)
        def _():
            for ref in (carry, dbs_s, acc_ref, accp_ref, dcw_ref, dws_ref, dbst_ref):
                ref[...] = jnp.zeros_like(ref)

        dh2v = (dh2a_ref[...] + dh2b_ref[...]) + (dh2c_ref[...] + dh2d_ref[...])
        xn2, r2 = _rms(x2_ref[...])
        g2 = _row(v_ref, G2)
        sc2 = 1.0 + _row(v_ref, SC2)
        acc_ref[pl.ds(5, 1), :] += _colsum(dh2v)
        acc_ref[pl.ds(6, 1), :] += _colsum(dh2v * (xn2 * g2))
        acc_ref[pl.ds(7, 1), :] += _colsum(dh2v * sc2 * xn2)
        dx2v = dx3_ref[...] + _rms_bwd(dh2v * sc2 * g2, xn2, r2)
        do1 = (dx2v * _row(v_ref, GT1)).astype(BF16)
        do1_ref[...] = do1
        acc_ref[pl.ds(0, 1), :] += _colsum(dx2v * o1_ref[...])
        dy = _dot_nt(do1, wo_ref[...])
        mog = _row(v_ref, MOG)

        xh, rstd = _ln(a1_ref[...])
        clg = _row(p_ref, CLG)
        a2 = xh * clg + _row(p_ref, CLB)
        s2 = _sigmoid(a2)
        a3 = a2 * s2
        an, ra = _rms(a3)
        dya = dy[:, :DC]
        da3 = _rms_bwd(dya * mog[:, :DC], an, ra)
        da2 = da3 * (s2 * (1.0 + a2 * (1.0 - s2)))
        accp_ref[pl.ds(CLB, 1), :] += _colsum(da2)
        accp_ref[pl.ds(CLG, 1), :] += _colsum(da2 * xh)
        da1 = _ln_bwd(da2 * clg, xh, rstd)
        accp_ref[pl.ds(CB, 1), :] += _colsum(da1)
        da1_s[...] = da1
        for c0 in range(0, DC, LANES):
            cols = pl.ds(c0, LANES)
            d = da1_s[:, cols]
            e = jnp.concatenate([d, carry[:, cols]], axis=0)
            carry[:, cols] = d[:HALO_C, :]
            a0c = a0_ref[:, cols]
            acc = jnp.zeros((tm, LANES), F32)
            for j in range(KC):
                ahead = _shift_up(e, j)[:tm, :]
                acc = acc + ahead * cw_ref[pl.ds(KC - 1 - j, 1), cols]
                dcw_ref[pl.ds(KC - 1 - j, 1), cols] += _colsum(a0c * ahead)
            sgc = _sigmoid(z_ref[:, pl.ds(DC + c0, LANES)])
            dz_ref[:, cols] = (acc * sgc).astype(BF16)
            dz_ref[:, pl.ds(DC + c0, LANES)] = (acc * z_ref[:, cols] * sgc * (1.0 - sgc)).astype(BF16)

        gu_pre = z_ref[:, 2 * DC:3 * DC]
        gv_pre = z_ref[:, 3 * DC:]
        gu, tu = _gelu(gu_pre)
        gvg, tv = _gelu(gv_pre)
        vh, vrstd = _ln(gvg)
        glg = _row(p_ref, GLG)
        gvn = (vh * glg + _row(p_ref, GLB)).astype(BF16)
        spv = sp_ref[...]
        g = gu * spv
        gn, rg = _rms(g)
        yt_ref[...] = jnp.concatenate([an * mog[:, :DC], gn * mog[:, DC:]], axis=1).astype(BF16).T
        acc_ref[pl.ds(4, 1), :] += jnp.concatenate([_colsum(dya * an), _colsum(dy[:, DC:] * gn)], axis=1)
        dg = _rms_bwd(dy[:, DC:] * mog[:, DC:], gn, rg)
        dz_ref[:, pl.ds(2 * DC, DC)] = (dg * spv * _gelu_grad(gu_pre, tu)).astype(BF16)
        dsp_s[...] = dg * gu
        upper = _causal_mask(False)
        first = _first_head_lanes()
        wmt = [jnp.where(upper, wst_ref[h], 0.0).astype(BF16) for h in range(NH)]
        for n in range(tm // CHUNK):
            rows = pl.ds(n * CHUNK, CHUNK)
            for p in range(NH // 2):
                cols = pl.ds(p * CHUNK, CHUNK)
                dsp = dsp_s[rows, cols]
                dbs_s[:, cols] += dsp
                da = jnp.where(first, dsp, 0.0).astype(BF16)
                db = jnp.where(first, 0.0, dsp).astype(BF16)
                v = gvn[n * CHUNK:(n + 1) * CHUNK, p * CHUNK:(p + 1) * CHUNK]
                dws_ref[2 * p] += _dot_nt(da, v)
                dws_ref[2 * p + 1] += _dot_nt(db, v)
                dgvn_s[rows, cols] = _dot(wmt[2 * p], da) + _dot(wmt[2 * p + 1], db)
        dgvn = dgvn_s[...]
        accp_ref[pl.ds(GLB, 1), :] += _colsum(dgvn)
        accp_ref[pl.ds(GLG, 1), :] += _colsum(dgvn * vh)
        dgvg = _ln_bwd(dgvn * glg, vh, vrstd)
        dz_ref[:, pl.ds(3 * DC, DC)] = (dgvg * _gelu_grad(gv_pre, tv)).astype(BF16)

        dh1 = jnp.zeros((tm, D), F32)
        for d in range(NDEV):
            dh1 = dh1 + _dot_nt(dz_ref[:, pl.ds(d * nc, nc)], wi_ref[d])
        xn, r1 = _rms(x_ref[...])
        g1 = _row(v_ref, G1)
        sc = 1.0 + _row(v_ref, SC1)
        acc_ref[pl.ds(1, 1), :] += _colsum(dh1)
        acc_ref[pl.ds(2, 1), :] += _colsum(dh1 * (xn * g1))
        acc_ref[pl.ds(3, 1), :] += _colsum(dh1 * sc * xn)
        gx_ref[...] = dx2v + _rms_bwd(dh1 * sc * g1, xn, r1)

        @pl.when(i == nt - 1)
        def _():
            low = _causal_mask(True)
            for h in range(NH):
                dws_ref[h] = jnp.where(low, dws_ref[h], 0.0)
            lane = lax.broadcasted_iota(jnp.int32, (CHUNK, CHUNK), 1)
            out = jnp.zeros((CHUNK, CHUNK), F32)
            for h in range(NH):
                hs = jnp.sum(dbs_s[:, pl.ds((h // 2) * CHUNK, CHUNK)]
                             * ((lane >= (h % 2) * HD) & (lane < (h % 2 + 1) * HD)).astype(F32),
                             axis=1, keepdims=True)
                out = jnp.where(lane == h, hs, out)
            dbst_ref[...] = out

    tile = lambda w: pl.BlockSpec((tm, w), lambda i: (nt - 1 - i, 0))
    return pl.pallas_call(
        body, name="bwd_mid", grid=(nt,),
        in_specs=[pl.BlockSpec((None, tm, D), functools.partial(lambda k, i: (k, nt - 1 - i, 0), k)) for k in range(4)]
        + [tile(D), tile(D), tile(D), tile(D), tile(4 * DC), tile(DC),
                  tile(DC), tile(DC), _full((VROWS, D)), _full((SUB, DC)), _full((CW_ROWS, DC)),
                  _full((NH, CHUNK, CHUNK)), _full((NH, CHUNK, CHUNK)), _full((D, D)), _full((NDEV, D, nc))],
        out_specs=[tile(D), tile(4 * DC), pl.BlockSpec((D, tm), lambda i: (0, nt - 1 - i)), tile(D),
                   _full((VROWS, D)), _full((SUB, DC)), _full((CW_ROWS, DC)),
                   _full((NH, CHUNK, CHUNK)), _full((CHUNK, CHUNK))],
        out_shape=[jax.ShapeDtypeStruct((s, D), F32), jax.ShapeDtypeStruct((s, 4 * DC), BF16),
                   jax.ShapeDtypeStruct((D, s), BF16), jax.ShapeDtypeStruct((s, D), BF16),
                   jax.ShapeDtypeStruct((VROWS, D), F32), jax.ShapeDtypeStruct((SUB, DC), F32),
                   jax.ShapeDtypeStruct((CW_ROWS, DC), F32), jax.ShapeDtypeStruct((NH, CHUNK, CHUNK), F32),
                   jax.ShapeDtypeStruct((CHUNK, CHUNK), F32)],
        scratch_shapes=[pltpu.VMEM((CHUNK, DC), F32), pltpu.VMEM((HALO_C, DC), F32), pltpu.VMEM((tm, DC), F32),
                        pltpu.VMEM((tm, DC), F32), pltpu.VMEM((tm, DC), F32)],
        compiler_params=_arb(),
    )(dh2, dh2, dh2, dh2, dx3, x2, x2d, o1, z, a0, a1, sp, vecs, v512, conv_w, gm_ws, gm_ws_t, w_out_b, w_in_g)


def _mm_all_slots(name, at, b, bw, tk, after):
    k1, s = at.shape
    nslot = b.shape[1] // bw

    def body(a_ref, b_ref, after_ref, o_ref):
        @pl.when(pl.program_id(0) == 0)
        def _():
            o_ref[...] = jnp.zeros_like(o_ref)

        t = _dot(a_ref[...], b_ref[...])
        for j in range(nslot):
            o_ref[j] += t[:, j * bw:(j + 1) * bw]

    return pl.pallas_call(
        body, name=name, grid=(s // tk,),
        in_specs=[pl.BlockSpec((k1, tk), lambda k: (0, k)), pl.BlockSpec((tk, nslot * bw), lambda k: (k, 0)), ANY],
        out_specs=_full((nslot, k1, bw)), out_shape=jax.ShapeDtypeStruct((nslot, k1, bw), F32),
        compiler_params=_arb(),
    )(at, b, after)


def _adam_math(w, g, m, v):
    m = ADAM_B1 * m + (1.0 - ADAM_B1) * g
    v = ADAM_B2 * v + (1.0 - ADAM_B2) * (g * g)
    m_hat = m / (1.0 - ADAM_B1 ** ADAM_STEP)
    v_hat = v / (1.0 - ADAM_B2 ** ADAM_STEP)
    delta = -ADAM_LR * (m_hat / (jnp.sqrt(v_hat) + ADAM_EPS) + ADAM_WD * w)
    return delta, m, v


def _row_block(rows, cols):
    tr = rows
    while tr * cols * 4 > ROW_BLOCK_BYTES and tr % (4 * SUB) == 0:
        tr //= 2
    return tr


def _adam3(name, w, g, m, v):
    _, rows, cols = w.shape
    tr = _row_block(rows, cols)

    def body(w_ref, g_ref, m_ref, v_ref, d_ref, mo_ref, vo_ref):
        d_ref[...], mo_ref[...], vo_ref[...] = _adam_math(w_ref[...], g_ref[...], m_ref[...], v_ref[...])

    spec = pl.BlockSpec((1, tr, cols), lambda i: (0, i, 0))
    return pl.pallas_call(
        body, name=name, grid=(rows // tr,), in_specs=[spec] * 4, out_specs=[spec] * 3,
        out_shape=[jax.ShapeDtypeStruct(w.shape, F32)] * 3, compiler_params=_arb(),
    )(w, g, m, v)


def _sum_adam(name, parts, w, m, v):
    n, rows, cols = parts.shape
    tr = _row_block(rows, cols)

    def body(p_ref, w_ref, m_ref, v_ref, g_ref, d_ref, mo_ref, vo_ref):
        g = p_ref[0].astype(F32)
        for k in range(1, n):
            g = g + p_ref[k].astype(F32)
        g_ref[0] = g
        d_ref[0], mo_ref[0], vo_ref[0] = _adam_math(w_ref[0], g, m_ref[0], v_ref[0])

    spec = pl.BlockSpec((1, tr, cols), lambda i: (0, i, 0))
    return pl.pallas_call(
        body, name=name, grid=(rows // tr,),
        in_specs=[pl.BlockSpec((n, tr, cols), lambda i: (0, i, 0))] + [spec] * 3, out_specs=[spec] * 4,
        out_shape=[jax.ShapeDtypeStruct(w.shape, F32)] * 4, compiler_params=_arb(),
    )(parts, w, m, v)


def _pair_add(name, g4, recv, core):
    _, _, rows, cols = g4.shape
    tr = _row_block(rows, cols)

    def body(c_ref, a_ref, b_ref, o_ref):
        o_ref[...] = (a_ref[...] + b_ref[...]).astype(BF16)

    return pl.pallas_call(
        body, name=name,
        grid_spec=pltpu.PrefetchScalarGridSpec(
            num_scalar_prefetch=1, grid=(4, rows // tr),
            in_specs=[pl.BlockSpec((None, None, tr, cols), lambda k, i, c_ref: (k, c_ref[0], i, 0)),
                      pl.BlockSpec((None, tr, cols), lambda k, i, c_ref: (k, i, 0))],
            out_specs=pl.BlockSpec((None, tr, cols), lambda k, i, c_ref: (k, i, 0))),
        out_shape=jax.ShapeDtypeStruct((4, rows, cols), BF16), compiler_params=_arb(2),
    )(core, g4, recv)


def _sum_small(rows_all, p_all, ws_all, bst_all, fw_all, cw_all):
    def body(a_ref, p_ref, ws_ref, bst_ref, fw_ref, cw_ref,
             g_b_ada, g_n1, g_mog, g_n2, g_gf, loss_cols, g_cb, g_clg, g_clb, g_glg, g_glb, g_ws, g_bs, fw_sum,
             cw_sum):
        def total(ref):
            t = ref[0]
            for k in range(1, NDEV):
                t = t + ref[k]
            return t

        a = total(a_ref)
        g_b_ada[...] = jnp.concatenate([a[k:k + 1, :] for k in range(6)], axis=1)
        g_n1[...] = a[6:7, :]
        g_mog[...] = a[7:8, :]
        g_n2[...] = a[8:9, :]
        g_gf[...] = a[9:10, :].reshape(D)
        loss_cols[...] = a[10:11, :]
        p = total(p_ref)
        for k, ref in zip((CB, CLG, CLB, GLG, GLB), (g_cb, g_clg, g_clb, g_glg, g_glb)):
            ref[...] = p[k:k + 1, :]
        g_ws[0] = total(ws_ref)
        g_bs[0] = jnp.transpose(total(bst_ref))[:NH, :]
        fw_sum[...] = total(fw_ref)
        cw_sum[...] = total(cw_ref)

    vec = lambda n: jax.ShapeDtypeStruct((1, n), F32)
    return pl.pallas_call(
        body, name="sum_small_grads",
        out_shape=[vec(6 * D), vec(D), vec(D), vec(D), jax.ShapeDtypeStruct((D,), F32), vec(D),
                   vec(DC), vec(DC), vec(DC), vec(DC), vec(DC),
                   jax.ShapeDtypeStruct((1, NH, CHUNK, CHUNK), F32), jax.ShapeDtypeStruct((1, NH, CHUNK), F32),
                   jax.ShapeDtypeStruct((SUB, 2 * PFF), F32), jax.ShapeDtypeStruct((CW_ROWS, DC), F32)],
    )(rows_all, p_all, ws_all, bst_all, fw_all, cw_all)


def _adam_small(quads):
    n = len(quads)

    def body(*refs):
        ins, outs = refs[:4 * n], refs[4 * n:]
        for q in range(n):
            w, g, m, v = (r[...] for r in ins[4 * q:4 * q + 4])
            outs[3 * q][...], outs[3 * q + 1][...], outs[3 * q + 2][...] = _adam_math(w, g, m, v)

    flat = [a for q in quads for a in q]
    outs = pl.pallas_call(
        body, name="adam_small",
        out_shape=[jax.ShapeDtypeStruct(q[0].shape, F32) for q in quads for _ in range(3)],
    )(*flat)
    return [tuple(outs[3 * q:3 * q + 3]) for q in range(n)]


def kernel(x, c, w_ada, b_ada, norm1_gain, w_in, conv_dw_w, conv_dw_b, conv_ln_g, conv_ln_b, gm_ln_g, gm_ln_b, gm_ws, gm_bs, mix_out_gain, w_out, norm2_gain, w_up, ffn_dw_w, ffn_dw_b, w_down, final_gain, loss_target, m_w_ada, m_b_ada, m_norm1_gain, m_w_in, m_conv_dw_w, m_conv_dw_b, m_conv_ln_g, m_conv_ln_b, m_gm_ln_g, m_gm_ln_b, m_gm_ws, m_gm_bs, m_mix_out_gain, m_w_out, m_norm2_gain, m_w_up, m_ffn_dw_w, m_ffn_dw_b, m_w_down, m_final_gain, v_w_ada, v_b_ada, v_norm1_gain, v_w_in, v_conv_dw_w, v_conv_dw_b, v_conv_ln_g, v_conv_ln_b, v_gm_ln_g, v_gm_ln_b, v_gm_ws, v_gm_bs, v_mix_out_gain, v_w_out, v_norm2_gain, v_w_up, v_ffn_dw_w, v_ffn_dw_b, v_w_down, v_final_gain):
    s = x.shape[1]
    ax, ay, ac = _place()
    me = 4 * ax + 2 * ay + ac
    n_ada = w_ada.shape[2]
    n_cw = conv_dw_w.shape[2]
    x2d = x[0]
    target = loss_target[0]
    pad_sh = lambda a: jnp.pad(a, [(0, 0)] * (a.ndim - 1) + [(0, PSH - NSH)])

    c_all, cw_all, fw_all = _all_gather("gather_small", [c, conv_dw_w[0], ffn_dw_w[0]])

    first_shards, c_all = lax.optimization_barrier(((w_in[0].astype(BF16), w_out[0].astype(BF16)), c_all))
    gather_in, token_a = _start_gather("gather_in_out", list(first_shards), me)
    c_all = c_all + token_a[0, 0]
    conv_w = jnp.pad(jnp.transpose(cw_all, (1, 0, 2)).reshape(KC, DC), ((0, CW_ROWS - KC), (0, 0)))
    ffn_w = jnp.transpose(pad_sh(fw_all), (1, 0, 2)).reshape(KF, 2 * PFF)
    ffn_b = pad_sh(ffn_dw_b.reshape(NDEV, NSH)).reshape(1, 2 * PFF)
    ffn_wb = jnp.concatenate([ffn_w, ffn_b, jnp.zeros((SUB - KF - 1, 2 * PFF), F32)], axis=0)

    b_cols = lax.dynamic_slice(b_ada, (0, me * n_ada), (1, n_ada))
    (mod_all,) = _all_gather("gather_mod", [_mod_part(c_all, w_ada, b_cols)])
    up_t = lambda a: jnp.swapaxes(a, 1, 2)
    w_up_shard = jnp.pad(up_t(w_up)[0].astype(BF16), ((0, PSH - NSH), (0, 0)))
    shards, mod_all = lax.optimization_barrier(((w_up_shard, w_down[0].astype(BF16)), mod_all))
    gather_ffn, token_c = _start_gather("gather_up_down", list(shards), me)
    mod = lax.dynamic_index_in_dim(mod_all, me, axis=1, keepdims=False).reshape(6, D)
    sh1, sc1, gt1, sh2, sc2, gt2 = [mod[k:k + 1] for k in range(6)]
    vecs = jnp.concatenate([norm1_gain, sh1, sc1, gt1, norm2_gain, sh2, sc2, gt2, mix_out_gain,
                            final_gain.reshape(1, D), jnp.zeros((6, D), F32)], axis=0)
    vecs = vecs + token_c[0, 0]
    v512 = jnp.concatenate([conv_dw_b, conv_ln_g, conv_ln_b, gm_ln_g, gm_ln_b, jnp.zeros((3, DC), F32)], axis=0)
    bs_exp = jnp.repeat(jnp.transpose(gm_bs[0]), HD, axis=1)
    gm_ws_t = jnp.swapaxes(gm_ws[0], 1, 2)

    tm_big, tm = min(TILE_BIG, s), min(TILE, s)
    w_in_g, w_out_g = _finish_gather("gather_in_out", gather_in, vecs)
    w_out_b = w_out_g.reshape(D, D)
    z, a0, h1_t = _fwd_in(x2d, vecs, w_in_g, tm_big)
    a1, sp, x2, o1, h2 = _fwd_mid(a0, z, x2d, vecs, v512, conv_w, gm_ws, bs_exp, w_out_b, tm_big)
    w_up_t, w_down_g = _finish_gather("gather_up_down", gather_ffn, h2)
    w_down_p = jnp.pad(w_down_g.reshape(4, NSH, D), ((0, 0), (0, PSH - NSH), (0, 0)))
    up_pre, vg, dx3, acc_f = _fwd_ffn(h2, x2, target, vecs, ffn_wb, w_up_t, w_down_p, tm)

    core = ac.reshape(1).astype(jnp.int32)
    mychip = 2 * ax + ay

    def to_pairs(named):
        g4s = [g.reshape((4, 2) + g.shape[1:]) for _, g in named]
        from_sibling = _sibling_swap("rs_sibling_" + named[0][0], g4s)
        return [_pair_add("rs_pair_add_" + t[0], g4, rv, core) for t, g4, rv in zip(named, g4s, from_sibling)]

    dh2, dw_up, dw_down, acc_fw = _bwd_ffn(dx3, up_pre, vg, h2, vecs, ffn_wb, w_up_t, w_down_p, tm_big)
    acc_fw = jnp.transpose(acc_fw, (2, 0, 1, 3)).reshape(SUB, 2 * PFF)
    exchange_ffn, token_x = _start_exchange("rs_chips_ffn", to_pairs(
        [("w_up", dw_up), ("w_down", dw_down.reshape(NDEV, w_down.shape[1], D))]), mychip)
    gx, dz, y_t, do1, acc_m, acc_p, dcw, dws, dbs_t = _bwd_mid(
        dh2, dx3, x2, x2d, o1, z, a0, a1, sp, vecs + token_x[0, 0], v512, conv_w, gm_ws[0], gm_ws_t, w_out_b, w_in_g, tm)
    rows = jnp.concatenate([acc_m[1:3], acc_m[0:1], acc_m[5:7], acc_f[2:3], acc_m[3:5], acc_m[7:8], acc_f[0:2],
                            jnp.zeros((5, D), F32)], axis=0)
    small_gather, token_s = _start_gather("gather_small_grads", [rows, acc_p, dws, dbs_t, acc_fw, dcw], me)
    dw_in = _mm_all_slots("dw_in", h1_t, dz, w_in.shape[2], min(TK_IN, s), token_s)
    small_pass, token_p = _pass_gather("gather_small_grads", small_gather, dw_in)
    dw_out = _mm_all_slots("dw_out", y_t, do1, D, min(TK_OUT, s), token_p).reshape(NDEV, w_out.shape[1], D)
    exchange_mix, token_m = _start_exchange("rs_chips_mix", to_pairs([("w_in", dw_in), ("w_out", dw_out)]), mychip)

    rows_all, p_all, ws_all, bst_all, fwg_all, cwg_all = _end_gather("gather_small_grads", small_pass, token_m)
    (g_b_ada, g_n1, g_mog, g_n2, g_gf, loss_cols, g_cb, g_clg, g_clb, g_glg, g_glb, g_ws, g_bs, fw_sum,
     cw_sum) = _sum_small(rows_all, p_all, ws_all, bst_all, fwg_all, cwg_all)
    loss = jnp.sum(loss_cols)
    g_fb = fw_sum[3].reshape(NDEV, PSH)[:, :NSH].reshape(ffn_dw_b.shape)
    g_fw = lax.dynamic_index_in_dim(fw_sum[:KF].reshape(KF, NDEV, PSH), me, axis=1, keepdims=False)[:, :NSH]
    g_fw = g_fw.reshape(ffn_dw_w.shape)
    g_cw = lax.dynamic_slice(cw_sum, (0, me * n_cw), (KC, n_cw)).reshape(conv_dw_w.shape)
    small = [
        (b_ada, g_b_ada, m_b_ada, v_b_ada), (norm1_gain, g_n1, m_norm1_gain, v_norm1_gain),
        (conv_dw_w, g_cw, m_conv_dw_w, v_conv_dw_w), (conv_dw_b, g_cb, m_conv_dw_b, v_conv_dw_b),
        (conv_ln_g, g_clg, m_conv_ln_g, v_conv_ln_g), (conv_ln_b, g_clb, m_conv_ln_b, v_conv_ln_b),
        (gm_ln_g, g_glg, m_gm_ln_g, v_gm_ln_g), (gm_ln_b, g_glb, m_gm_ln_b, v_gm_ln_b),
        (gm_ws, g_ws, m_gm_ws, v_gm_ws), (gm_bs, g_bs, m_gm_bs, v_gm_bs),
        (mix_out_gain, g_mog, m_mix_out_gain, v_mix_out_gain), (norm2_gain, g_n2, m_norm2_gain, v_norm2_gain),
        (ffn_dw_w, g_fw, m_ffn_dw_w, v_ffn_dw_w), (ffn_dw_b, g_fb, m_ffn_dw_b, v_ffn_dw_b),
        (final_gain, g_gf, m_final_gain, v_final_gain)]
    small_out = _adam_small(small)
    res = {}
    for name, q, o in zip(("b_ada", "norm1_gain", "conv_dw_w", "conv_dw_b", "conv_ln_g", "conv_ln_b", "gm_ln_g",
                           "gm_ln_b", "gm_ws", "gm_bs", "mix_out_gain", "norm2_gain", "ffn_dw_w", "ffn_dw_b",
                           "final_gain"), small, small_out):
        res[name] = (q[1],) + o

    dmod_all = rows_all[:, :6].reshape(NDEV, 6 * D)
    dm_cols = lax.dynamic_slice(dmod_all, (0, me * n_ada), (NDEV, n_ada))
    g_ada = _ada_grad(jnp.transpose(c_all[:, 0, :]), dm_cols)
    res["w_ada"] = (g_ada,) + tuple(_adam3("adam_ada", w_ada, g_ada, m_w_ada, v_w_ada))

    big = [("w_up", up_t(w_up), up_t(m_w_up), up_t(v_w_up)), ("w_down", w_down, m_w_down, v_w_down),
           ("w_in", w_in, m_w_in, v_w_in), ("w_out", w_out, m_w_out, v_w_out)]
    from_chips = list(_finish_exchange("rs_chips_ffn", exchange_ffn, res["w_ada"][1]))
    for t, parts in zip(big[:2], from_chips):
        res[t[0]] = tuple(_sum_adam("rs_sum_adam_" + t[0], parts, t[1], t[2], t[3]))
    res["w_up"] = tuple(up_t(a) for a in res["w_up"])
    from_chips = list(_finish_exchange("rs_chips_mix", exchange_mix, res["w_down"][1]))
    for t, parts in zip(big[2:], from_chips):
        res[t[0]] = tuple(_sum_adam("rs_sum_adam_" + t[0], parts, t[1], t[2], t[3]))

    order = ("w_ada", "b_ada", "norm1_gain", "w_in", "conv_dw_w", "conv_dw_b", "conv_ln_g", "conv_ln_b", "gm_ln_g",
             "gm_ln_b", "gm_ws", "gm_bs", "mix_out_gain", "w_out", "norm2_gain", "w_up", "ffn_dw_w", "ffn_dw_b",
             "w_down", "final_gain")
    return (loss, gx.reshape(x.shape), *[res[n][0] for n in order], *[res[n][1] for n in order],
            *[res[n][2] for n in order], *[res[n][3] for n in order])
```

```python
import functools

import jax
import jax.numpy as jnp
from jax import lax
from jax.experimental import pallas as pl
from jax.experimental.pallas import tpu as pltpu

F32 = jnp.float32
BF16 = jnp.bfloat16
NDEV = 8
D = 1024
DC = 512
DFF = 2816
NSH = 704
PSH = 768
PFF = 4 * PSH
KC = 31
KF = 3
CHUNK = 128
NH = 8
HD = 64
HALO_C = 32
HALO_F = 8
LANES = 128
SUB = 8
VROWS = 16
CW_ROWS = 32
TILE_BIG = 512
ROW_BLOCK_BYTES = 2 << 20
TILE = 256
TK_IN = 2048
TK_OUT = 4096
RMS_EPS = 1e-6
LN_EPS = 1e-5
ADAM_LR = 0.001
ADAM_B1 = 0.9
ADAM_B2 = 0.999
ADAM_EPS = 1e-08
ADAM_WD = 0.01
ADAM_STEP = 10
GELU_K = 0.7978845608028654
GELU_C = 0.044715

MESH = pl.DeviceIdType.MESH
ANY = pl.BlockSpec(memory_space=pl.ANY)

G1, SH1, SC1, GT1, G2, SH2, SC2, GT2, MOG, GF = range(10)
CB, CLG, CLB, GLG, GLB = range(5)


def _full(shape):
    return pl.BlockSpec(shape, lambda *_: (0,) * len(shape))


def _arb(n=1):
    return pltpu.CompilerParams(dimension_semantics=("arbitrary",) * n)


def _row(ref, r):
    return ref[pl.ds(r, 1), :]


def _colsum(v):
    return jnp.sum(v, axis=0, keepdims=True)


def _rowmean(v):
    return jnp.mean(v, axis=-1, keepdims=True)


def _rms(x):
    r = lax.rsqrt(_rowmean(x * x) + RMS_EPS)
    return x * r, r


def _rms_bwd(dxn, xn, r):
    return r * (dxn - xn * _rowmean(dxn * xn))


def _ln(x):
    mu = _rowmean(x)
    xc = x - mu
    rstd = lax.rsqrt(_rowmean(xc * xc) + LN_EPS)
    return xc * rstd, rstd


def _ln_bwd(dxh, xhat, rstd):
    return rstd * (dxh - _rowmean(dxh) - xhat * _rowmean(dxh * xhat))


def _sigmoid(x):
    return 0.5 * jnp.tanh(0.5 * x) + 0.5


def _gelu(x):
    t = jnp.tanh(GELU_K * (x + GELU_C * x * x * x))
    return 0.5 * x * (1.0 + t), t


def _gelu_grad(x, t):
    return 0.5 * (1.0 + t) + 0.5 * x * (1.0 - t * t) * (GELU_K * (1.0 + 3.0 * GELU_C * x * x))


def _dot(a, b):
    return jnp.dot(a, b, preferred_element_type=F32)


def _dot_nt(a, b):
    return lax.dot_general(a, b, (((1,), (1,)), ((), ())), preferred_element_type=F32)


def _shift_up(e, s):
    n = e.shape[0]
    return pltpu.roll(e, (n - s) % n, 0)


def _place():
    return lax.axis_index("x"), lax.axis_index("y"), lax.axis_index("c")


def _all_gather(name, xs):
    n = len(xs)

    def body(*refs):
        x_refs, out_refs = refs[:n], refs[n:2 * n]
        send_sems, recv_sems, local_sems = refs[2 * n:]
        x, y, c = _place()
        me, sibling = (x, y, c), (x, y, 1 - c)
        chips = [(1 - x, y), (x, 1 - y), (1 - x, 1 - y)]

        def copy(a, k, block, to, own=False):
            px, py, pc = block
            slot = out_refs[a].at[4 * px + 2 * py + pc]
            return pltpu.make_async_remote_copy(
                src_ref=x_refs[a] if own else slot, dst_ref=slot,
                send_sem=send_sems.at[7 * a + k], recv_sem=recv_sems.at[7 * a + k], device_id=to, device_id_type=MESH)

        mine = [pltpu.make_async_copy(x_refs[a], out_refs[a].at[4 * x + 2 * y + c], local_sems.at[a]) for a in range(n)]
        for cp in mine:
            cp.start()
        first = []
        for a in range(n):
            first.append(copy(a, 0, me, sibling, own=True))
            first += [copy(a, 1 + j, me, (*chip, c), own=True) for j, chip in enumerate(chips)]
        for cp in first:
            cp.start()
        passed = []
        for j, chip in enumerate(chips):
            for a in range(n):
                copy(a, 1 + j, (*chip, c), me).wait_recv()
                cp = copy(a, 4 + j, (*chip, c), sibling)
                cp.start()
                passed.append(cp)
        for a in range(n):
            copy(a, 0, sibling, me).wait_recv()
            for j, chip in enumerate(chips):
                copy(a, 4 + j, (*chip, 1 - c), me).wait_recv()
        for cp in first + passed:
            cp.wait_send()
        for cp in mine:
            cp.wait()

    return pl.pallas_call(
        body, name=name, out_shape=[jax.ShapeDtypeStruct((NDEV,) + a.shape, a.dtype) for a in xs],
        in_specs=[ANY] * n, out_specs=[ANY] * n,
        scratch_shapes=[pltpu.SemaphoreType.DMA((7 * n,)), pltpu.SemaphoreType.DMA((7 * n,)),
                        pltpu.SemaphoreType.DMA((n,))],
    )(*xs)


def _sibling_swap(name, hs):
    n = len(hs)

    def body(*refs):
        h_refs, out_refs = refs[:n], refs[n:2 * n]
        send_sems, recv_sems = refs[2 * n:]
        x, y, c = _place()
        cps = [pltpu.make_async_remote_copy(
            src_ref=h_refs[a].at[k], dst_ref=out_refs[a].at[k],
            send_sem=send_sems.at[4 * a + k], recv_sem=recv_sems.at[4 * a + k],
            device_id=(x, y, 1 - c), device_id_type=MESH) for a in range(n) for k in range(4)]
        for cp in cps:
            cp.start()
        for cp in cps:
            cp.wait()

    return pl.pallas_call(
        body, name=name, out_shape=[jax.ShapeDtypeStruct(h.shape, h.dtype) for h in hs],
        in_specs=[ANY] * n, out_specs=[ANY] * n,
        scratch_shapes=[pltpu.SemaphoreType.DMA((4 * n,)), pltpu.SemaphoreType.DMA((4 * n,))],
    )(*hs)


HBM = pl.BlockSpec(memory_space=pltpu.HBM)
SEM = pl.BlockSpec(memory_space=pltpu.SEMAPHORE)
EFFECT = pltpu.SideEffectType.DATAFLOW_SIDE_EFFECTING


def _in_hbm(a):
    return pltpu.with_memory_space_constraint(a, pltpu.HBM)


def _split_start(name, bufs, copies):
    n = len(bufs)

    def body(*refs):
        for cp in copies(refs[:n], refs[n], refs[n + 1]):
            cp.start()
        refs[-1][...] = jnp.zeros_like(refs[-1])

    out = pl.pallas_call(
        body, name=name,
        out_shape=(pltpu.SemaphoreType.DMA((copies.count,)), pltpu.SemaphoreType.DMA((copies.count,)),
                   *[pltpu.HBM(a.shape, a.dtype) for a in bufs], jax.ShapeDtypeStruct((SUB, LANES), F32)),
        in_specs=[HBM] * n, out_specs=(SEM, SEM, *[HBM] * n, pl.BlockSpec(memory_space=pltpu.VMEM)),
        input_output_aliases={i: 2 + i for i in range(n)},
        compiler_params=pltpu.CompilerParams(has_side_effects=EFFECT),
    )(*[_in_hbm(a) for a in bufs])
    return (out[0], out[1], list(out[2:2 + n])), out[-1]


def _split_wait(name, handle, copies, after):
    send_sems, recv_sems, bufs = handle
    n = len(bufs)

    def body(*refs):
        for cp in copies(refs[:n], refs[n], refs[n + 1]):
            cp.wait_send()
            cp.wait_recv()

    out = pl.pallas_call(
        body, name=name, out_shape=tuple(pltpu.HBM(a.shape, a.dtype) for a in bufs),
        in_specs=[HBM] * n + [SEM, SEM, pl.BlockSpec(memory_space=pl.ANY)], out_specs=tuple([HBM] * n),
        input_output_aliases={i: i for i in range(n)},
        compiler_params=pltpu.CompilerParams(has_side_effects=EFFECT),
    )(*bufs, send_sems, recv_sems, after)
    return list(out)


class _GatherFirstCopies:
    def __init__(self, n):
        self.n, self.count = n, 4 * n

    def __call__(self, refs, send_sems, recv_sems):
        x, y, c = _place()
        peers = [(x, y, 1 - c), (1 - x, y, c), (x, 1 - y, c), (1 - x, 1 - y, c)]
        return [pltpu.make_async_remote_copy(
            src_ref=refs[a], dst_ref=refs[self.n + a].at[4 * x + 2 * y + c],
            send_sem=send_sems.at[4 * a + k], recv_sem=recv_sems.at[4 * a + k], device_id=peer, device_id_type=MESH)
            for a in range(self.n) for k, peer in enumerate(peers)]


class _GatherPassCopies:
    def __init__(self, n):
        self.n, self.count = n, 3 * n

    def __call__(self, refs, send_sems, recv_sems):
        x, y, c = _place()
        cps = []
        for a in range(self.n):
            for j, (px, py) in enumerate([(1 - x, y), (x, 1 - y), (1 - x, 1 - y)]):
                slot = refs[a].at[4 * px + 2 * py + c]
                cps.append(pltpu.make_async_remote_copy(
                    src_ref=slot, dst_ref=slot, send_sem=send_sems.at[3 * a + j], recv_sem=recv_sems.at[3 * a + j],
                    device_id=(x, y, 1 - c), device_id_type=MESH))
        return cps


class _ExchangeCopies:
    def __init__(self, n):
        self.n, self.count = n, 3 * n

    def __call__(self, refs, send_sems, recv_sems):
        x, y, c = _place()
        cps = []
        for a in range(self.n):
            for j, (px, py) in enumerate([(1 - x, y), (x, 1 - y), (1 - x, 1 - y)]):
                cps.append(pltpu.make_async_remote_copy(
                    src_ref=refs[a].at[2 * px + py], dst_ref=refs[self.n + a].at[2 * x + y],
                    send_sem=send_sems.at[3 * a + j], recv_sem=recv_sems.at[3 * a + j],
                    device_id=(px, py, c), device_id_type=MESH))
        return cps


def _own_slot(nslot, src, index):
    land = lax.empty((nslot,) + src.shape, src.dtype)
    return lax.dynamic_update_slice(land, src[None], (index,) + (0,) * src.ndim)


def _start_gather(tag, xs, me):
    lands = [_own_slot(NDEV, a, me) for a in xs]
    return _split_start(tag + "_start", list(xs) + lands, _GatherFirstCopies(len(xs)))


def _pass_gather(tag, handle, after):
    n = len(handle[2]) // 2
    lands = _split_wait(tag + "_wait", handle, _GatherFirstCopies(n), after)[n:]
    return _split_start(tag + "_pass", lands, _GatherPassCopies(n))


def _end_gather(tag, passing, after):
    return _split_wait(tag + "_pass_wait", passing, _GatherPassCopies(len(passing[2])), after)


def _finish_gather(tag, handle, after):
    passing, token = _pass_gather(tag, handle, after)
    return _end_gather(tag, passing, token)


def _start_exchange(tag, hs, mychip):
    lands = [_own_slot(4, lax.dynamic_index_in_dim(h, mychip, 0, keepdims=False), mychip) for h in hs]
    return _split_start(tag + "_start", list(hs) + lands, _ExchangeCopies(len(hs)))


def _finish_exchange(tag, handle, after):
    n = len(handle[2]) // 2
    return _split_wait(tag + "_wait", handle, _ExchangeCopies(n), after)[n:]


def _mod_part(c_all, w_ada, b_cols):
    ncol = w_ada.shape[2]

    def body(c_ref, w_ref, b_ref, o_ref):
        cv = c_ref[:, 0, :]
        ca = cv * _sigmoid(cv)
        o_ref[...] = _dot(ca.astype(BF16), w_ref[0].astype(BF16)) + b_ref[...]

    return pl.pallas_call(body, name="mod_part", out_shape=jax.ShapeDtypeStruct((NDEV, ncol), F32))(
        c_all, w_ada, b_cols)


def _ada_grad(c_all_t, dmod_cols):
    ncol = dmod_cols.shape[1]

    def body(ct_ref, dm_ref, o_ref):
        ct = ct_ref[...]
        ca = ct * _sigmoid(ct)
        acc = jnp.zeros((D, ncol), F32)
        for b in range(NDEV):
            acc = acc + ca[:, b:b + 1] * dm_ref[pl.ds(b, 1), :]
        o_ref[0] = acc

    return pl.pallas_call(body, name="ada_grad", out_shape=jax.ShapeDtypeStruct((1, D, ncol), F32))(
        c_all_t, dmod_cols)


def _fwd_in(x2d, vecs, w_in_g, tm):
    s = x2d.shape[0]
    nc = w_in_g.shape[2]

    def body(x_ref, v_ref, w_ref, z_ref, a0_ref, h1t_ref):
        xn, _ = _rms(x_ref[...])
        h = (xn * _row(v_ref, G1)) * (1.0 + _row(v_ref, SC1)) + _row(v_ref, SH1)
        hb = h.astype(BF16)
        h1t_ref[...] = hb.T
        for d in range(NDEV):
            z_ref[:, pl.ds(d * nc, nc)] = _dot(hb, w_ref[d])
        a0_ref[...] = z_ref[:, :DC] * _sigmoid(z_ref[:, DC:2 * DC])

    return pl.pallas_call(
        body, name="fwd_in", grid=(s // tm,),
        in_specs=[pl.BlockSpec((tm, D), lambda i: (i, 0)), _full((VROWS, D)), _full((NDEV, D, nc))],
        out_specs=[pl.BlockSpec((tm, 4 * DC), lambda i: (i, 0)), pl.BlockSpec((tm, DC), lambda i: (i, 0)),
                   pl.BlockSpec((D, tm), lambda i: (0, i))],
        out_shape=[jax.ShapeDtypeStruct((s, 4 * DC), F32), jax.ShapeDtypeStruct((s, DC), F32),
                   jax.ShapeDtypeStruct((D, s), BF16)],
        compiler_params=_arb(),
    )(x2d, vecs, w_in_g)


def _causal_mask(lower):
    r = lax.broadcasted_iota(jnp.int32, (CHUNK, CHUNK), 0)
    c = lax.broadcasted_iota(jnp.int32, (CHUNK, CHUNK), 1)
    return (r >= c) if lower else (r <= c)


def _first_head_lanes():
    return lax.broadcasted_iota(jnp.int32, (CHUNK, CHUNK), 1) < HD


def _fwd_mid(a0, z, x2d, vecs, v512, conv_w, gm_ws, bs_exp, w_out_b, tm):
    s = x2d.shape[0]
    hb = tm // HALO_C

    def body(a0_ref, halo_ref, zg_ref, x_ref, v_ref, p_ref, cw_ref, ws_ref, bs_ref, wo_ref,
             a1_ref, sp_ref, x2_ref, o1_ref, h2_ref):
        i = pl.program_id(0)
        for c0 in range(0, DC, LANES):
            cols = pl.ds(c0, LANES)
            halo = halo_ref[:, cols]
            e = jnp.concatenate([jnp.where(i > 0, halo, jnp.zeros_like(halo)), a0_ref[:, cols]], axis=0)
            acc = jnp.broadcast_to(p_ref[pl.ds(CB, 1), cols], (tm, LANES))
            for k in range(KC):
                acc = acc + _shift_up(e, HALO_C - (KC - 1) + k)[:tm, :] * cw_ref[pl.ds(k, 1), cols]
            a1_ref[:, cols] = acc
        xh, _ = _ln(a1_ref[...])
        a2 = xh * _row(p_ref, CLG) + _row(p_ref, CLB)
        a3 = a2 * _sigmoid(a2)
        gu, _ = _gelu(zg_ref[:, :DC])
        gvg, _ = _gelu(zg_ref[:, DC:])
        vh, _ = _ln(gvg)
        gvn = (vh * _row(p_ref, GLG) + _row(p_ref, GLB)).astype(BF16)
        low = _causal_mask(True)
        first = _first_head_lanes()
        wm = [jnp.where(low, ws_ref[0, h], 0.0).astype(BF16) for h in range(NH)]
        for n in range(tm // CHUNK):
            for p in range(NH // 2):
                v = gvn[n * CHUNK:(n + 1) * CHUNK, p * CHUNK:(p + 1) * CHUNK]
                blk = jnp.where(first, _dot(wm[2 * p], v), _dot(wm[2 * p + 1], v))
                sp_ref[pl.ds(n * CHUNK, CHUNK), pl.ds(p * CHUNK, CHUNK)] = blk + bs_ref[:, pl.ds(p * CHUNK, CHUNK)]
        g = gu * sp_ref[...]
        an, _ = _rms(a3)
        gn, _ = _rms(g)
        mog = _row(v_ref, MOG)
        y = jnp.concatenate([an * mog[:, :DC], gn * mog[:, DC:]], axis=1).astype(BF16)
        o1 = _dot(y, wo_ref[...])
        o1_ref[...] = o1
        x2 = x_ref[...] + _row(v_ref, GT1) * o1
        x2_ref[...] = x2
        xn2, _ = _rms(x2)
        h2 = (xn2 * _row(v_ref, G2)) * (1.0 + _row(v_ref, SC2)) + _row(v_ref, SH2)
        h2_ref[...] = h2.astype(BF16)

    tile = lambda w: pl.BlockSpec((tm, w), lambda i: (i, 0))
    return pl.pallas_call(
        body, name="fwd_mid", grid=(s // tm,),
        in_specs=[tile(DC), pl.BlockSpec((HALO_C, DC), lambda i: (jnp.maximum(i * hb - 1, 0), 0)),
                  pl.BlockSpec((tm, 2 * DC), lambda i: (i, 1)), tile(D), _full((VROWS, D)), _full((SUB, DC)),
                  _full((CW_ROWS, DC)), _full((1, NH, CHUNK, CHUNK)), _full((CHUNK, DC)), _full((D, D))],
        out_specs=[tile(DC), tile(DC), tile(D), tile(D), tile(D)],
        out_shape=[jax.ShapeDtypeStruct((s, DC), F32), jax.ShapeDtypeStruct((s, DC), F32),
                   jax.ShapeDtypeStruct((s, D), F32), jax.ShapeDtypeStruct((s, D), F32),
                   jax.ShapeDtypeStruct((s, D), BF16)],
        compiler_params=_arb(),
    )(a0, a0, z, x2d, vecs, v512, conv_w, gm_ws, bs_exp, w_out_b)


def _ffn_conv(fw_ref, cols, p2, p1, pre):
    return (fw_ref[pl.ds(3, 1), cols] + fw_ref[pl.ds(0, 1), cols] * p2
            + fw_ref[pl.ds(1, 1), cols] * p1 + fw_ref[pl.ds(2, 1), cols] * pre)


def _fwd_ffn(h2, x2, target, vecs, ffn_wb, w_up_t, w_down_p, tm):
    s = x2.shape[0]

    def body(h2_ref, x2_ref, t_ref, v_ref, fw_ref, wu_hbm, wd_hbm,
             up_ref, vg_ref, dx3_ref, acc_ref, wu, wd, carry, stage):
        i = pl.program_id(0)

        @pl.when(i == 0)
        def _():
            for sh in range(NDEV):
                pltpu.sync_copy(wu_hbm.at[sh], stage)
                wu[sh] = stage[...].T
            pltpu.sync_copy(wd_hbm, wd)
            carry[...] = jnp.zeros_like(carry)
            acc_ref[...] = jnp.zeros_like(acc_ref)

        h2v = h2_ref[...]
        o2 = jnp.zeros((tm, D), F32)
        for j in range(4):
            conv = []
            for sh in (j, 4 + j):
                cols = pl.ds(sh * PSH, PSH)
                pre = _dot(h2v, wu[sh])
                up_ref[:, cols] = pre.astype(BF16)
                e = jnp.concatenate([carry[:, cols], pre], axis=0)
                carry[:, cols] = pre[tm - HALO_F:, :]
                conv.append(_ffn_conv(fw_ref, cols, pltpu.roll(e, 2, 0)[HALO_F:, :],
                                      pltpu.roll(e, 1, 0)[HALO_F:, :], pre))
            val, gate = conv
            vg_ref[:, pl.ds(j * PSH, PSH)] = val.astype(BF16)
            vg_ref[:, pl.ds((4 + j) * PSH, PSH)] = gate.astype(BF16)
            f = ((gate * _sigmoid(gate)) * val).astype(BF16)
            o2 = o2 + _dot(f, wd[j])
        x3 = x2_ref[...] + _row(v_ref, GT2) * o2
        xn3, r3 = _rms(x3)
        gf = _row(v_ref, GF)
        diff = xn3 * gf - t_ref[...]
        acc_ref[pl.ds(1, 1), :] += _colsum(diff * diff) * (0.5 / D)
        dout = diff * (1.0 / D)
        acc_ref[pl.ds(0, 1), :] += _colsum(dout * xn3)
        dx3 = _rms_bwd(dout * gf, xn3, r3)
        dx3_ref[...] = dx3
        acc_ref[pl.ds(2, 1), :] += _colsum(dx3 * o2)

    tile = lambda w: pl.BlockSpec((tm, w), lambda i: (i, 0))
    return pl.pallas_call(
        body, name="fwd_ffn", grid=(s // tm,),
        in_specs=[tile(D), tile(D), tile(D), _full((VROWS, D)), _full((SUB, 2 * PFF)), ANY, ANY],
        out_specs=[tile(2 * PFF), tile(2 * PFF), tile(D), _full((SUB, D))],
        out_shape=[jax.ShapeDtypeStruct((s, 2 * PFF), BF16), jax.ShapeDtypeStruct((s, 2 * PFF), BF16),
                   jax.ShapeDtypeStruct((s, D), F32), jax.ShapeDtypeStruct((SUB, D), F32)],
        scratch_shapes=[pltpu.VMEM((NDEV, D, PSH), BF16), pltpu.VMEM((4, PSH, D), BF16),
                        pltpu.VMEM((HALO_F, 2 * PFF), F32), pltpu.VMEM((PSH, D), BF16)],
        compiler_params=_arb(),
    )(h2, x2, target, vecs, ffn_wb, w_up_t, w_down_p)


def _bwd_ffn(dx3, up_pre, vg, h2, vecs, ffn_wb, w_up_t, w_down_p, tm):
    s = dx3.shape[0]
    nt = s // tm

    def body(dx3_ref, up_ref, upg_ref, val_ref, gate_ref, h2_ref, v_ref, fw_ref, fwg_ref, wu_ref, wug_ref, wd_ref,
             dh2_ref, dwu_ref, dwd_ref, accf_ref, carry):
        i = pl.program_id(1)

        @pl.when(i == 0)
        def _():
            for ref in (carry, dwu_ref, dwd_ref, accf_ref):
                ref[...] = jnp.zeros_like(ref)

        do2 = (dx3_ref[...] * _row(v_ref, GT2)).astype(BF16)
        df = _dot_nt(do2, wd_ref[...])
        val = val_ref[...].astype(F32)
        gate = gate_ref[...].astype(F32)
        sg = _sigmoid(gate)
        sl = gate * sg
        f_t = (sl * val).astype(BF16).T
        dwd_ref[...] += _dot(f_t, do2)[:NSH, :]
        dups = (df * sl, df * val * (sg * (1.0 + gate * (1.0 - sg))))
        h2v = h2_ref[...]
        dh2 = jnp.zeros((tm, D), F32)
        for half, (dup, pre_ref, w_ref, wmat_ref) in enumerate(
                zip(dups, (up_ref, upg_ref), (fw_ref, fwg_ref), (wu_ref, wug_ref))):
            cols = pl.ds(half * PSH, PSH)
            e = jnp.concatenate([dup, carry[:, cols]], axis=0)
            carry[:, cols] = dup[:HALO_F, :]
            d1 = _shift_up(e, 1)[:tm, :]
            d2 = _shift_up(e, 2)[:tm, :]
            pre = pre_ref[...].astype(F32)
            accf_ref[half, pl.ds(3, 1), :] += _colsum(dup)
            accf_ref[half, pl.ds(0, 1), :] += _colsum(d2 * pre)
            accf_ref[half, pl.ds(1, 1), :] += _colsum(d1 * pre)
            accf_ref[half, pl.ds(2, 1), :] += _colsum(dup * pre)
            dpre = (_row(w_ref, 0) * d2 + _row(w_ref, 1) * d1 + _row(w_ref, 2) * dup).astype(BF16)
            dwu_ref[half] += _dot(dpre.T, h2v)[:NSH, :]
            dh2 = dh2 + _dot(dpre, wmat_ref[...])
        dh2_ref[...] = dh2

    rev = lambda j, i: nt - 1 - i
    in_specs = [
        pl.BlockSpec((tm, D), lambda j, i: (rev(j, i), 0)),
        pl.BlockSpec((tm, PSH), lambda j, i: (rev(j, i), j)), pl.BlockSpec((tm, PSH), lambda j, i: (rev(j, i), 4 + j)),
        pl.BlockSpec((tm, PSH), lambda j, i: (rev(j, i), j)), pl.BlockSpec((tm, PSH), lambda j, i: (rev(j, i), 4 + j)),
        pl.BlockSpec((tm, D), lambda j, i: (rev(j, i), 0)), _full((VROWS, D)),
        pl.BlockSpec((SUB, PSH), lambda j, i: (0, j)), pl.BlockSpec((SUB, PSH), lambda j, i: (0, 4 + j)),
        pl.BlockSpec((None, PSH, D), lambda j, i: (j, 0, 0)), pl.BlockSpec((None, PSH, D), lambda j, i: (4 + j, 0, 0)),
        pl.BlockSpec((None, PSH, D), lambda j, i: (j, 0, 0))]
    dh2, dw_up, dw_down, accf = pl.pallas_call(
        body, name="bwd_ffn", grid=(4, nt), in_specs=in_specs,
        out_specs=[pl.BlockSpec((None, tm, D), lambda j, i: (j, rev(j, i), 0)),
                   pl.BlockSpec((2, None, NSH, D), lambda j, i: (0, j, 0, 0)),
                   pl.BlockSpec((None, NSH, D), lambda j, i: (j, 0, 0)),
                   pl.BlockSpec((2, None, SUB, PSH), lambda j, i: (0, j, 0, 0))],
        out_shape=[jax.ShapeDtypeStruct((4, s, D), F32), jax.ShapeDtypeStruct((2, 4, NSH, D), F32),
                   jax.ShapeDtypeStruct((4, NSH, D), F32), jax.ShapeDtypeStruct((2, 4, SUB, PSH), F32)],
        scratch_shapes=[pltpu.VMEM((HALO_F, 2 * PSH), F32)],
        compiler_params=_arb(2),
    )(dx3, up_pre, up_pre, vg, vg, h2, vecs, ffn_wb, ffn_wb, w_up_t, w_up_t, w_down_p)
    return dh2, dw_up.reshape(NDEV, NSH, D), dw_down, accf


def _bwd_mid(dh2, dx3, x2, x2d, o1, z, a0, a1, sp, vecs, v512, conv_w, gm_ws, gm_ws_t, w_out_b, w_in_g, tm):
    s = x2d.shape[0]
    nt = s // tm
    nc = w_in_g.shape[2]

    def body(dh2a_ref, dh2b_ref, dh2c_ref, dh2d_ref, dx3_ref, x2_ref, x_ref, o1_ref, z_ref, a0_ref, a1_ref, sp_ref,
             v_ref, p_ref, cw_ref, ws_ref, wst_ref, wo_ref, wi_ref, gx_ref, dz_ref, yt_ref, do1_ref, acc_ref, accp_ref,
             dcw_ref, dws_ref, dbst_ref, dbs_s, carry, da1_s, dsp_s, dgvn_s):
        i = pl.program_id(0)

        @pl.when(i == 0)
        def _():
            for ref in (carry, dbs_s, acc_ref, accp_ref, dcw_ref, dws_ref, dbst_ref):
                ref[...] = jnp.zeros_like(ref)

        dh2v = (dh2a_ref[...] + dh2b_ref[...]) + (dh2c_ref[...] + dh2d_ref[...])
        xn2, r2 = _rms(x2_ref[...])
        g2 = _row(v_ref, G2)
        sc2 = 1.0 + _row(v_ref, SC2)
        acc_ref[pl.ds(5, 1), :] += _colsum(dh2v)
        acc_ref[pl.ds(6, 1), :] += _colsum(dh2v * (xn2 * g2))
        acc_ref[pl.ds(7, 1), :] += _colsum(dh2v * sc2 * xn2)
        dx2v = dx3_ref[...] + _rms_bwd(dh2v * sc2 * g2, xn2, r2)
        do1 = (dx2v * _row(v_ref, GT1)).astype(BF16)
        do1_ref[...] = do1
        acc_ref[pl.ds(0, 1), :] += _colsum(dx2v * o1_ref[...])
        dy = _dot_nt(do1, wo_ref[...])
        mog = _row(v_ref, MOG)

        xh, rstd = _ln(a1_ref[...])
        clg = _row(p_ref, CLG)
        a2 = xh * clg + _row(p_ref, CLB)
        s2 = _sigmoid(a2)
        a3 = a2 * s2
        an, ra = _rms(a3)
        dya = dy[:, :DC]
        da3 = _rms_bwd(dya * mog[:, :DC], an, ra)
        da2 = da3 * (s2 * (1.0 + a2 * (1.0 - s2)))
        accp_ref[pl.ds(CLB, 1), :] += _colsum(da2)
        accp_ref[pl.ds(CLG, 1), :] += _colsum(da2 * xh)
        da1 = _ln_bwd(da2 * clg, xh, rstd)
        accp_ref[pl.ds(CB, 1), :] += _colsum(da1)
        da1_s[...] = da1
        for c0 in range(0, DC, LANES):
            cols = pl.ds(c0, LANES)
            d = da1_s[:, cols]
            e = jnp.concatenate([d, carry[:, cols]], axis=0)
            carry[:, cols] = d[:HALO_C, :]
            a0c = a0_ref[:, cols]
            acc = jnp.zeros((tm, LANES), F32)
            for j in range(KC):
                ahead = _shift_up(e, j)[:tm, :]
                acc = acc + ahead * cw_ref[pl.ds(KC - 1 - j, 1), cols]
                dcw_ref[pl.ds(KC - 1 - j, 1), cols] += _colsum(a0c * ahead)
            sgc = _sigmoid(z_ref[:, pl.ds(DC + c0, LANES)])
            dz_ref[:, cols] = (acc * sgc).astype(BF16)
            dz_ref[:, pl.ds(DC + c0, LANES)] = (acc * z_ref[:, cols] * sgc * (1.0 - sgc)).astype(BF16)

        gu_pre = z_ref[:, 2 * DC:3 * DC]
        gv_pre = z_ref[:, 3 * DC:]
        gu, tu = _gelu(gu_pre)
        gvg, tv = _gelu(gv_pre)
        vh, vrstd = _ln(gvg)
        glg = _row(p_ref, GLG)
        gvn = (vh * glg + _row(p_ref, GLB)).astype(BF16)
        spv = sp_ref[...]
        g = gu * spv
        gn, rg = _rms(g)
        yt_ref[...] = jnp.concatenate([an * mog[:, :DC], gn * mog[:, DC:]], axis=1).astype(BF16).T
        acc_ref[pl.ds(4, 1), :] += jnp.concatenate([_colsum(dya * an), _colsum(dy[:, DC:] * gn)], axis=1)
        dg = _rms_bwd(dy[:, DC:] * mog[:, DC:], gn, rg)
        dz_ref[:, pl.ds(2 * DC, DC)] = (dg * spv * _gelu_grad(gu_pre, tu)).astype(BF16)
        dsp_s[...] = dg * gu
        upper = _causal_mask(False)
        first = _first_head_lanes()
        wmt = [jnp.where(upper, wst_ref[h], 0.0).astype(BF16) for h in range(NH)]
        for n in range(tm // CHUNK):
            rows = pl.ds(n * CHUNK, CHUNK)
            for p in range(NH // 2):
                cols = pl.ds(p * CHUNK, CHUNK)
                dsp = dsp_s[rows, cols]
                dbs_s[:, cols] += dsp
                da = jnp.where(first, dsp, 0.0).astype(BF16)
                db = jnp.where(first, 0.0, dsp).astype(BF16)
                v = gvn[n * CHUNK:(n + 1) * CHUNK, p * CHUNK:(p + 1) * CHUNK]
                dws_ref[2 * p] += _dot_nt(da, v)
                dws_ref[2 * p + 1] += _dot_nt(db, v)
                dgvn_s[rows, cols] = _dot(wmt[2 * p], da) + _dot(wmt[2 * p + 1], db)
        dgvn = dgvn_s[...]
        accp_ref[pl.ds(GLB, 1), :] += _colsum(dgvn)
        accp_ref[pl.ds(GLG, 1), :] += _colsum(dgvn * vh)
        dgvg = _ln_bwd(dgvn * glg, vh, vrstd)
        dz_ref[:, pl.ds(3 * DC, DC)] = (dgvg * _gelu_grad(gv_pre, tv)).astype(BF16)

        dh1 = jnp.zeros((tm, D), F32)
        for d in range(NDEV):
            dh1 = dh1 + _dot_nt(dz_ref[:, pl.ds(d * nc, nc)], wi_ref[d])
        xn, r1 = _rms(x_ref[...])
        g1 = _row(v_ref, G1)
        sc = 1.0 + _row(v_ref, SC1)
        acc_ref[pl.ds(1, 1), :] += _colsum(dh1)
        acc_ref[pl.ds(2, 1), :] += _colsum(dh1 * (xn * g1))
        acc_ref[pl.ds(3, 1), :] += _colsum(dh1 * sc * xn)
        gx_ref[...] = dx2v + _rms_bwd(dh1 * sc * g1, xn, r1)

        @pl.when(i == nt - 1)
        def _():
            low = _causal_mask(True)
            for h in range(NH):
                dws_ref[h] = jnp.where(low, dws_ref[h], 0.0)
            lane = lax.broadcasted_iota(jnp.int32, (CHUNK, CHUNK), 1)
            out = jnp.zeros((CHUNK, CHUNK), F32)
            for h in range(NH):
                hs = jnp.sum(dbs_s[:, pl.ds((h // 2) * CHUNK, CHUNK)]
                             * ((lane >= (h % 2) * HD) & (lane < (h % 2 + 1) * HD)).astype(F32),
                             axis=1, keepdims=True)
                out = jnp.where(lane == h, hs, out)
            dbst_ref[...] = out

    tile = lambda w: pl.BlockSpec((tm, w), lambda i: (nt - 1 - i, 0))
    return pl.pallas_call(
        body, name="bwd_mid", grid=(nt,),
        in_specs=[pl.BlockSpec((None, tm, D), functools.partial(lambda k, i: (k, nt - 1 - i, 0), k)) for k in range(4)]
        + [tile(D), tile(D), tile(D), tile(D), tile(4 * DC), tile(DC),
                  tile(DC), tile(DC), _full((VROWS, D)), _full((SUB, DC)), _full((CW_ROWS, DC)),
                  _full((NH, CHUNK, CHUNK)), _full((NH, CHUNK, CHUNK)), _full((D, D)), _full((NDEV, D, nc))],
        out_specs=[tile(D), tile(4 * DC), pl.BlockSpec((D, tm), lambda i: (0, nt - 1 - i)), tile(D),
                   _full((VROWS, D)), _full((SUB, DC)), _full((CW_ROWS, DC)),
                   _full((NH, CHUNK, CHUNK)), _full((CHUNK, CHUNK))],
        out_shape=[jax.ShapeDtypeStruct((s, D), F32), jax.ShapeDtypeStruct((s, 4 * DC), BF16),
                   jax.ShapeDtypeStruct((D, s), BF16), jax.ShapeDtypeStruct((s, D), BF16),
                   jax.ShapeDtypeStruct((VROWS, D), F32), jax.ShapeDtypeStruct((SUB, DC), F32),
                   jax.ShapeDtypeStruct((CW_ROWS, DC), F32), jax.ShapeDtypeStruct((NH, CHUNK, CHUNK), F32),
                   jax.ShapeDtypeStruct((CHUNK, CHUNK), F32)],
        scratch_shapes=[pltpu.VMEM((CHUNK, DC), F32), pltpu.VMEM((HALO_C, DC), F32), pltpu.VMEM((tm, DC), F32),
                        pltpu.VMEM((tm, DC), F32), pltpu.VMEM((tm, DC), F32)],
        compiler_params=_arb(),
    )(dh2, dh2, dh2, dh2, dx3, x2, x2d, o1, z, a0, a1, sp, vecs, v512, conv_w, gm_ws, gm_ws_t, w_out_b, w_in_g)


def _mm_all_slots(name, at, b, bw, tk, after):
    k1, s = at.shape
    nslot = b.shape[1] // bw

    def body(a_ref, b_ref, after_ref, o_ref):
        @pl.when(pl.program_id(0) == 0)
        def _():
            o_ref[...] = jnp.zeros_like(o_ref)

        t = _dot(a_ref[...], b_ref[...])
        for j in range(nslot):
            o_ref[j] += t[:, j * bw:(j + 1) * bw]

    return pl.pallas_call(
        body, name=name, grid=(s // tk,),
        in_specs=[pl.BlockSpec((k1, tk), lambda k: (0, k)), pl.BlockSpec((tk, nslot * bw), lambda k: (k, 0)), ANY],
        out_specs=_full((nslot, k1, bw)), out_shape=jax.ShapeDtypeStruct((nslot, k1, bw), F32),
        compiler_params=_arb(),
    )(at, b, after)


def _adam_math(w, g, m, v):
    m = ADAM_B1 * m + (1.0 - ADAM_B1) * g
    v = ADAM_B2 * v + (1.0 - ADAM_B2) * (g * g)
    m_hat = m / (1.0 - ADAM_B1 ** ADAM_STEP)
    v_hat = v / (1.0 - ADAM_B2 ** ADAM_STEP)
    delta = -ADAM_LR * (m_hat / (jnp.sqrt(v_hat) + ADAM_EPS) + ADAM_WD * w)
    return delta, m, v


def _row_block(rows, cols):
    tr = rows
    while tr * cols * 4 > ROW_BLOCK_BYTES and tr % (4 * SUB) == 0:
        tr //= 2
    return tr


def _adam3(name, w, g, m, v):
    _, rows, cols = w.shape
    tr = _row_block(rows, cols)

    def body(w_ref, g_ref, m_ref, v_ref, d_ref, mo_ref, vo_ref):
        d_ref[...], mo_ref[...], vo_ref[...] = _adam_math(w_ref[...], g_ref[...], m_ref[...], v_ref[...])

    spec = pl.BlockSpec((1, tr, cols), lambda i: (0, i, 0))
    return pl.pallas_call(
        body, name=name, grid=(rows // tr,), in_specs=[spec] * 4, out_specs=[spec] * 3,
        out_shape=[jax.ShapeDtypeStruct(w.shape, F32)] * 3, compiler_params=_arb(),
    )(w, g, m, v)


def _sum_adam(name, parts, w, m, v):
    n, rows, cols = parts.shape
    tr = _row_block(rows, cols)

    def body(p_ref, w_ref, m_ref, v_ref, g_ref, d_ref, mo_ref, vo_ref):
        g = p_ref[0].astype(F32)
        for k in range(1, n):
            g = g + p_ref[k].astype(F32)
        g_ref[0] = g
        d_ref[0], mo_ref[0], vo_ref[0] = _adam_math(w_ref[0], g, m_ref[0], v_ref[0])

    spec = pl.BlockSpec((1, tr, cols), lambda i: (0, i, 0))
    return pl.pallas_call(
        body, name=name, grid=(rows // tr,),
        in_specs=[pl.BlockSpec((n, tr, cols), lambda i: (0, i, 0))] + [spec] * 3, out_specs=[spec] * 4,
        out_shape=[jax.ShapeDtypeStruct(w.shape, F32)] * 4, compiler_params=_arb(),
    )(parts, w, m, v)


def _other_half(name, g4, other):
    _, _, rows, cols = g4.shape
    tr = _row_block(rows, cols)

    def body(c_ref, a_ref, o_ref):
        o_ref[...] = a_ref[...].astype(BF16)

    return pl.pallas_call(
        body, name=name,
        grid_spec=pltpu.PrefetchScalarGridSpec(
            num_scalar_prefetch=1, grid=(4, rows // tr),
            in_specs=[pl.BlockSpec((None, None, tr, cols), lambda k, i, c_ref: (k, c_ref[0], i, 0))],
            out_specs=pl.BlockSpec((None, tr, cols), lambda k, i, c_ref: (k, i, 0))),
        out_shape=jax.ShapeDtypeStruct((4, rows, cols), BF16), compiler_params=_arb(2),
    )(other, g4)


def _pair_add(name, g4, recv, core):
    _, _, rows, cols = g4.shape
    tr = _row_block(rows, cols)

    def body(c_ref, a_ref, b_ref, o_ref):
        o_ref[...] = (a_ref[...] + b_ref[...].astype(F32)).astype(BF16)

    return pl.pallas_call(
        body, name=name,
        grid_spec=pltpu.PrefetchScalarGridSpec(
            num_scalar_prefetch=1, grid=(4, rows // tr),
            in_specs=[pl.BlockSpec((None, None, tr, cols), lambda k, i, c_ref: (k, c_ref[0], i, 0)),
                      pl.BlockSpec((None, tr, cols), lambda k, i, c_ref: (k, i, 0))],
            out_specs=pl.BlockSpec((None, tr, cols), lambda k, i, c_ref: (k, i, 0))),
        out_shape=jax.ShapeDtypeStruct((4, rows, cols), BF16), compiler_params=_arb(2),
    )(core, g4, recv)


def _sum_small(rows_all, p_all, ws_all, bst_all, fw_all, cw_all):
    def body(a_ref, p_ref, ws_ref, bst_ref, fw_ref, cw_ref,
             g_b_ada, g_n1, g_mog, g_n2, g_gf, loss_cols, g_cb, g_clg, g_clb, g_glg, g_glb, g_ws, g_bs, fw_sum,
             cw_sum):
        def total(ref):
            t = ref[0]
            for k in range(1, NDEV):
                t = t + ref[k]
            return t

        a = total(a_ref)
        g_b_ada[...] = jnp.concatenate([a[k:k + 1, :] for k in range(6)], axis=1)
        g_n1[...] = a[6:7, :]
        g_mog[...] = a[7:8, :]
        g_n2[...] = a[8:9, :]
        g_gf[...] = a[9:10, :].reshape(D)
        loss_cols[...] = a[10:11, :]
        p = total(p_ref)
        for k, ref in zip((CB, CLG, CLB, GLG, GLB), (g_cb, g_clg, g_clb, g_glg, g_glb)):
            ref[...] = p[k:k + 1, :]
        g_ws[0] = total(ws_ref)
        g_bs[0] = jnp.transpose(total(bst_ref))[:NH, :]
        fw_sum[...] = total(fw_ref)
        cw_sum[...] = total(cw_ref)

    vec = lambda n: jax.ShapeDtypeStruct((1, n), F32)
    return pl.pallas_call(
        body, name="sum_small_grads",
        out_shape=[vec(6 * D), vec(D), vec(D), vec(D), jax.ShapeDtypeStruct((D,), F32), vec(D),
                   vec(DC), vec(DC), vec(DC), vec(DC), vec(DC),
                   jax.ShapeDtypeStruct((1, NH, CHUNK, CHUNK), F32), jax.ShapeDtypeStruct((1, NH, CHUNK), F32),
                   jax.ShapeDtypeStruct((SUB, 2 * PFF), F32), jax.ShapeDtypeStruct((CW_ROWS, DC), F32)],
    )(rows_all, p_all, ws_all, bst_all, fw_all, cw_all)


def _adam_small(quads):
    n = len(quads)

    def body(*refs):
        ins, outs = refs[:4 * n], refs[4 * n:]
        for q in range(n):
            w, g, m, v = (r[...] for r in ins[4 * q:4 * q + 4])
            outs[3 * q][...], outs[3 * q + 1][...], outs[3 * q + 2][...] = _adam_math(w, g, m, v)

    flat = [a for q in quads for a in q]
    outs = pl.pallas_call(
        body, name="adam_small",
        out_shape=[jax.ShapeDtypeStruct(q[0].shape, F32) for q in quads for _ in range(3)],
    )(*flat)
    return [tuple(outs[3 * q:3 * q + 3]) for q in range(n)]


def kernel(x, c, w_ada, b_ada, norm1_gain, w_in, conv_dw_w, conv_dw_b, conv_ln_g, conv_ln_b, gm_ln_g, gm_ln_b, gm_ws, gm_bs, mix_out_gain, w_out, norm2_gain, w_up, ffn_dw_w, ffn_dw_b, w_down, final_gain, loss_target, m_w_ada, m_b_ada, m_norm1_gain, m_w_in, m_conv_dw_w, m_conv_dw_b, m_conv_ln_g, m_conv_ln_b, m_gm_ln_g, m_gm_ln_b, m_gm_ws, m_gm_bs, m_mix_out_gain, m_w_out, m_norm2_gain, m_w_up, m_ffn_dw_w, m_ffn_dw_b, m_w_down, m_final_gain, v_w_ada, v_b_ada, v_norm1_gain, v_w_in, v_conv_dw_w, v_conv_dw_b, v_conv_ln_g, v_conv_ln_b, v_gm_ln_g, v_gm_ln_b, v_gm_ws, v_gm_bs, v_mix_out_gain, v_w_out, v_norm2_gain, v_w_up, v_ffn_dw_w, v_ffn_dw_b, v_w_down, v_final_gain):
    s = x.shape[1]
    ax, ay, ac = _place()
    me = 4 * ax + 2 * ay + ac
    n_ada = w_ada.shape[2]
    n_cw = conv_dw_w.shape[2]
    x2d = x[0]
    target = loss_target[0]
    pad_sh = lambda a: jnp.pad(a, [(0, 0)] * (a.ndim - 1) + [(0, PSH - NSH)])

    c_all, cw_all, fw_all = _all_gather("gather_small", [c, conv_dw_w[0], ffn_dw_w[0]])

    first_shards, c_all = lax.optimization_barrier(((w_in[0].astype(BF16), w_out[0].astype(BF16)), c_all))
    gather_in, token_a = _start_gather("gather_in_out", list(first_shards), me)
    c_all = c_all + token_a[0, 0]
    conv_w = jnp.pad(jnp.transpose(cw_all, (1, 0, 2)).reshape(KC, DC), ((0, CW_ROWS - KC), (0, 0)))
    ffn_w = jnp.transpose(pad_sh(fw_all), (1, 0, 2)).reshape(KF, 2 * PFF)
    ffn_b = pad_sh(ffn_dw_b.reshape(NDEV, NSH)).reshape(1, 2 * PFF)
    ffn_wb = jnp.concatenate([ffn_w, ffn_b, jnp.zeros((SUB - KF - 1, 2 * PFF), F32)], axis=0)

    b_cols = lax.dynamic_slice(b_ada, (0, me * n_ada), (1, n_ada))
    (mod_all,) = _all_gather("gather_mod", [_mod_part(c_all, w_ada, b_cols)])
    up_t = lambda a: jnp.swapaxes(a, 1, 2)
    w_up_shard = jnp.pad(up_t(w_up)[0].astype(BF16), ((0, PSH - NSH), (0, 0)))
    shards, mod_all = lax.optimization_barrier(((w_up_shard, w_down[0].astype(BF16)), mod_all))
    gather_ffn, token_c = _start_gather("gather_up_down", list(shards), me)
    mod = lax.dynamic_index_in_dim(mod_all, me, axis=1, keepdims=False).reshape(6, D)
    sh1, sc1, gt1, sh2, sc2, gt2 = [mod[k:k + 1] for k in range(6)]
    vecs = jnp.concatenate([norm1_gain, sh1, sc1, gt1, norm2_gain, sh2, sc2, gt2, mix_out_gain,
                            final_gain.reshape(1, D), jnp.zeros((6, D), F32)], axis=0)
    vecs = vecs + token_c[0, 0]
    v512 = jnp.concatenate([conv_dw_b, conv_ln_g, conv_ln_b, gm_ln_g, gm_ln_b, jnp.zeros((3, DC), F32)], axis=0)
    bs_exp = jnp.repeat(jnp.transpose(gm_bs[0]), HD, axis=1)
    gm_ws_t = jnp.swapaxes(gm_ws[0], 1, 2)

    tm_big, tm = min(TILE_BIG, s), min(TILE, s)
    w_in_g, w_out_g = _finish_gather("gather_in_out", gather_in, vecs)
    w_out_b = w_out_g.reshape(D, D)
    z, a0, h1_t = _fwd_in(x2d, vecs, w_in_g, tm_big)
    a1, sp, x2, o1, h2 = _fwd_mid(a0, z, x2d, vecs, v512, conv_w, gm_ws, bs_exp, w_out_b, tm_big)
    w_up_t, w_down_g = _finish_gather("gather_up_down", gather_ffn, h2)
    w_down_p = jnp.pad(w_down_g.reshape(4, NSH, D), ((0, 0), (0, PSH - NSH), (0, 0)))
    up_pre, vg, dx3, acc_f = _fwd_ffn(h2, x2, target, vecs, ffn_wb, w_up_t, w_down_p, tm)

    core = ac.reshape(1).astype(jnp.int32)
    mychip = 2 * ax + ay

    other = 1 - core

    def to_pairs(named):
        g4s = [g.reshape((4, 2) + g.shape[1:]) for _, g in named]
        halves = [_other_half("rs_other_half_" + t[0], g4, other) for t, g4 in zip(named, g4s)]
        from_sibling = _sibling_swap("rs_sibling_" + named[0][0], halves)
        return [_pair_add("rs_pair_add_" + t[0], g4, rv, core) for t, g4, rv in zip(named, g4s, from_sibling)]

    dh2, dw_up, dw_down, acc_fw = _bwd_ffn(dx3, up_pre, vg, h2, vecs, ffn_wb, w_up_t, w_down_p, tm_big)
    acc_fw = jnp.transpose(acc_fw, (2, 0, 1, 3)).reshape(SUB, 2 * PFF)
    exchange_ffn, token_x = _start_exchange("rs_chips_ffn", to_pairs(
        [("w_up", dw_up), ("w_down", dw_down.reshape(NDEV, w_down.shape[1], D))]), mychip)
    gx, dz, y_t, do1, acc_m, acc_p, dcw, dws, dbs_t = _bwd_mid(
        dh2, dx3, x2, x2d, o1, z, a0, a1, sp, vecs + token_x[0, 0], v512, conv_w, gm_ws[0], gm_ws_t, w_out_b, w_in_g, tm)
    rows = jnp.concatenate([acc_m[1:3], acc_m[0:1], acc_m[5:7], acc_f[2:3], acc_m[3:5], acc_m[7:8], acc_f[0:2],
                            jnp.zeros((5, D), F32)], axis=0)
    small_gather, token_s = _start_gather("gather_small_grads", [rows, acc_p, dws, dbs_t, acc_fw, dcw], me)
    dw_in = _mm_all_slots("dw_in", h1_t, dz, w_in.shape[2], min(TK_IN, s), token_s)
    small_pass, token_p = _pass_gather("gather_small_grads", small_gather, dw_in)
    dw_out = _mm_all_slots("dw_out", y_t, do1, D, min(TK_OUT, s), token_p).reshape(NDEV, w_out.shape[1], D)
    exchange_mix, token_m = _start_exchange("rs_chips_mix", to_pairs([("w_in", dw_in), ("w_out", dw_out)]), mychip)

    rows_all, p_all, ws_all, bst_all, fwg_all, cwg_all = _end_gather("gather_small_grads", small_pass, token_m)
    (g_b_ada, g_n1, g_mog, g_n2, g_gf, loss_cols, g_cb, g_clg, g_clb, g_glg, g_glb, g_ws, g_bs, fw_sum,
     cw_sum) = _sum_small(rows_all, p_all, ws_all, bst_all, fwg_all, cwg_all)
    loss = jnp.sum(loss_cols)
    g_fb = fw_sum[3].reshape(NDEV, PSH)[:, :NSH].reshape(ffn_dw_b.shape)
    g_fw = lax.dynamic_index_in_dim(fw_sum[:KF].reshape(KF, NDEV, PSH), me, axis=1, keepdims=False)[:, :NSH]
    g_fw = g_fw.reshape(ffn_dw_w.shape)
    g_cw = lax.dynamic_slice(cw_sum, (0, me * n_cw), (KC, n_cw)).reshape(conv_dw_w.shape)
    small = [
        (b_ada, g_b_ada, m_b_ada, v_b_ada), (norm1_gain, g_n1, m_norm1_gain, v_norm1_gain),
        (conv_dw_w, g_cw, m_conv_dw_w, v_conv_dw_w), (conv_dw_b, g_cb, m_conv_dw_b, v_conv_dw_b),
        (conv_ln_g, g_clg, m_conv_ln_g, v_conv_ln_g), (conv_ln_b, g_clb, m_conv_ln_b, v_conv_ln_b),
        (gm_ln_g, g_glg, m_gm_ln_g, v_gm_ln_g), (gm_ln_b, g_glb, m_gm_ln_b, v_gm_ln_b),
        (gm_ws, g_ws, m_gm_ws, v_gm_ws), (gm_bs, g_bs, m_gm_bs, v_gm_bs),
        (mix_out_gain, g_mog, m_mix_out_gain, v_mix_out_gain), (norm2_gain, g_n2, m_norm2_gain, v_norm2_gain),
        (ffn_dw_w, g_fw, m_ffn_dw_w, v_ffn_dw_w), (ffn_dw_b, g_fb, m_ffn_dw_b, v_ffn_dw_b),
        (final_gain, g_gf, m_final_gain, v_final_gain)]
    small_out = _adam_small(small)
    res = {}
    for name, q, o in zip(("b_ada", "norm1_gain", "conv_dw_w", "conv_dw_b", "conv_ln_g", "conv_ln_b", "gm_ln_g",
                           "gm_ln_b", "gm_ws", "gm_bs", "mix_out_gain", "norm2_gain", "ffn_dw_w", "ffn_dw_b",
                           "final_gain"), small, small_out):
        res[name] = (q[1],) + o

    dmod_all = rows_all[:, :6].reshape(NDEV, 6 * D)
    dm_cols = lax.dynamic_slice(dmod_all, (0, me * n_ada), (NDEV, n_ada))
    g_ada = _ada_grad(jnp.transpose(c_all[:, 0, :]), dm_cols)
    res["w_ada"] = (g_ada,) + tuple(_adam3("adam_ada", w_ada, g_ada, m_w_ada, v_w_ada))

    big = [("w_up", up_t(w_up), up_t(m_w_up), up_t(v_w_up)), ("w_down", w_down, m_w_down, v_w_down),
           ("w_in", w_in, m_w_in, v_w_in), ("w_out", w_out, m_w_out, v_w_out)]
    from_chips = list(_finish_exchange("rs_chips_ffn", exchange_ffn, res["w_ada"][1]))
    for t, parts in zip(big[:2], from_chips):
        res[t[0]] = tuple(_sum_adam("rs_sum_adam_" + t[0], parts, t[1], t[2], t[3]))
    res["w_up"] = tuple(up_t(a) for a in res["w_up"])
    from_chips = list(_finish_exchange("rs_chips_mix", exchange_mix, res["w_down"][1]))
    for t, parts in zip(big[2:], from_chips):
        res[t[0]] = tuple(_sum_adam("rs_sum_adam_" + t[0], parts, t[1], t[2], t[3]))

    order = ("w_ada", "b_ada", "norm1_gain", "w_in", "conv_dw_w", "conv_dw_b", "conv_ln_g", "conv_ln_b", "gm_ln_g",
             "gm_ln_b", "gm_ws", "gm_bs", "mix_out_gain", "w_out", "norm2_gain", "w_up", "ffn_dw_w", "ffn_dw_b",
             "w_down", "final_gain")
    return (loss, gx.reshape(x.shape), *[res[n][0] for n in order], *[res[n][1] for n in order],
            *[res[n][2] for n in order], *[res[n][3] for n in order])
```

```python
import functools

import jax
import jax.numpy as jnp
from jax import lax
from jax.experimental import pallas as pl
from jax.experimental.pallas import tpu as pltpu

F32 = jnp.float32
BF16 = jnp.bfloat16
NDEV = 8
D = 1024
DC = 512
DFF = 2816
NSH = 704
PSH = 768
PFF = 4 * PSH
KC = 31
KF = 3
CHUNK = 128
NH = 8
HD = 64
HALO_C = 32
HALO_F = 8
LANES = 128
SUB = 8
VROWS = 16
CW_ROWS = 32
TILE_BIG = 512
ROW_BLOCK_BYTES = 2 << 20
TILE = 256
TK_IN = 2048
TK_OUT = 4096
RMS_EPS = 1e-6
LN_EPS = 1e-5
ADAM_LR = 0.001
ADAM_B1 = 0.9
ADAM_B2 = 0.999
ADAM_EPS = 1e-08
ADAM_WD = 0.01
ADAM_STEP = 10
GELU_K = 0.7978845608028654
GELU_C = 0.044715

MESH = pl.DeviceIdType.MESH
ANY = pl.BlockSpec(memory_space=pl.ANY)

G1, SH1, SC1, GT1, G2, SH2, SC2, GT2, MOG, GF = range(10)
CB, CLG, CLB, GLG, GLB = range(5)


def _full(shape):
    return pl.BlockSpec(shape, lambda *_: (0,) * len(shape))


def _arb(n=1):
    return pltpu.CompilerParams(dimension_semantics=("arbitrary",) * n)


def _row(ref, r):
    return ref[pl.ds(r, 1), :]


def _colsum(v):
    return jnp.sum(v, axis=0, keepdims=True)


def _rowmean(v):
    return jnp.mean(v, axis=-1, keepdims=True)


def _rms(x):
    r = lax.rsqrt(_rowmean(x * x) + RMS_EPS)
    return x * r, r


def _rms_bwd(dxn, xn, r):
    return r * (dxn - xn * _rowmean(dxn * xn))


def _ln(x):
    mu = _rowmean(x)
    xc = x - mu
    rstd = lax.rsqrt(_rowmean(xc * xc) + LN_EPS)
    return xc * rstd, rstd


def _ln_bwd(dxh, xhat, rstd):
    return rstd * (dxh - _rowmean(dxh) - xhat * _rowmean(dxh * xhat))


def _sigmoid(x):
    return 0.5 * jnp.tanh(0.5 * x) + 0.5


def _gelu(x):
    t = jnp.tanh(GELU_K * (x + GELU_C * x * x * x))
    return 0.5 * x * (1.0 + t), t


def _gelu_grad(x, t):
    return 0.5 * (1.0 + t) + 0.5 * x * (1.0 - t * t) * (GELU_K * (1.0 + 3.0 * GELU_C * x * x))


def _dot(a, b):
    return jnp.dot(a, b, preferred_element_type=F32)


def _dot_nt(a, b):
    return lax.dot_general(a, b, (((1,), (1,)), ((), ())), preferred_element_type=F32)


def _shift_up(e, s):
    n = e.shape[0]
    return pltpu.roll(e, (n - s) % n, 0)


def _place():
    return lax.axis_index("x"), lax.axis_index("y"), lax.axis_index("c")


def _all_gather(name, xs):
    n = len(xs)

    def body(*refs):
        x_refs, out_refs = refs[:n], refs[n:2 * n]
        send_sems, recv_sems, local_sems = refs[2 * n:]
        x, y, c = _place()
        me, sibling = (x, y, c), (x, y, 1 - c)
        chips = [(1 - x, y), (x, 1 - y), (1 - x, 1 - y)]

        def copy(a, k, block, to, own=False):
            px, py, pc = block
            slot = out_refs[a].at[4 * px + 2 * py + pc]
            return pltpu.make_async_remote_copy(
                src_ref=x_refs[a] if own else slot, dst_ref=slot,
                send_sem=send_sems.at[7 * a + k], recv_sem=recv_sems.at[7 * a + k], device_id=to, device_id_type=MESH)

        mine = [pltpu.make_async_copy(x_refs[a], out_refs[a].at[4 * x + 2 * y + c], local_sems.at[a]) for a in range(n)]
        for cp in mine:
            cp.start()
        first = []
        for a in range(n):
            first.append(copy(a, 0, me, sibling, own=True))
            first += [copy(a, 1 + j, me, (*chip, c), own=True) for j, chip in enumerate(chips)]
        for cp in first:
            cp.start()
        passed = []
        for j, chip in enumerate(chips):
            for a in range(n):
                copy(a, 1 + j, (*chip, c), me).wait_recv()
                cp = copy(a, 4 + j, (*chip, c), sibling)
                cp.start()
                passed.append(cp)
        for a in range(n):
            copy(a, 0, sibling, me).wait_recv()
            for j, chip in enumerate(chips):
                copy(a, 4 + j, (*chip, 1 - c), me).wait_recv()
        for cp in first + passed:
            cp.wait_send()
        for cp in mine:
            cp.wait()

    return pl.pallas_call(
        body, name=name, out_shape=[jax.ShapeDtypeStruct((NDEV,) + a.shape, a.dtype) for a in xs],
        in_specs=[ANY] * n, out_specs=[ANY] * n,
        scratch_shapes=[pltpu.SemaphoreType.DMA((7 * n,)), pltpu.SemaphoreType.DMA((7 * n,)),
                        pltpu.SemaphoreType.DMA((n,))],
    )(*xs)


def _sibling_swap(name, hs):
    n = len(hs)

    def body(*refs):
        h_refs, out_refs = refs[:n], refs[n:2 * n]
        send_sems, recv_sems = refs[2 * n:]
        x, y, c = _place()
        cps = [pltpu.make_async_remote_copy(
            src_ref=h_refs[a].at[k], dst_ref=out_refs[a].at[k],
            send_sem=send_sems.at[4 * a + k], recv_sem=recv_sems.at[4 * a + k],
            device_id=(x, y, 1 - c), device_id_type=MESH) for a in range(n) for k in range(4)]
        for cp in cps:
            cp.start()
        for cp in cps:
            cp.wait()

    return pl.pallas_call(
        body, name=name, out_shape=[jax.ShapeDtypeStruct(h.shape, h.dtype) for h in hs],
        in_specs=[ANY] * n, out_specs=[ANY] * n,
        scratch_shapes=[pltpu.SemaphoreType.DMA((4 * n,)), pltpu.SemaphoreType.DMA((4 * n,))],
    )(*hs)


HBM = pl.BlockSpec(memory_space=pltpu.HBM)
SEM = pl.BlockSpec(memory_space=pltpu.SEMAPHORE)
EFFECT = pltpu.SideEffectType.DATAFLOW_SIDE_EFFECTING


def _in_hbm(a):
    return pltpu.with_memory_space_constraint(a, pltpu.HBM)


def _split_start(name, bufs, copies):
    n = len(bufs)

    def body(*refs):
        for cp in copies(refs[:n], refs[n], refs[n + 1]):
            cp.start()
        refs[-1][...] = jnp.zeros_like(refs[-1])

    out = pl.pallas_call(
        body, name=name,
        out_shape=(pltpu.SemaphoreType.DMA((copies.count,)), pltpu.SemaphoreType.DMA((copies.count,)),
                   *[pltpu.HBM(a.shape, a.dtype) for a in bufs], jax.ShapeDtypeStruct((SUB, LANES), F32)),
        in_specs=[HBM] * n, out_specs=(SEM, SEM, *[HBM] * n, pl.BlockSpec(memory_space=pltpu.VMEM)),
        input_output_aliases={i: 2 + i for i in range(n)},
        compiler_params=pltpu.CompilerParams(has_side_effects=EFFECT),
    )(*[_in_hbm(a) for a in bufs])
    return (out[0], out[1], list(out[2:2 + n])), out[-1]


def _split_wait(name, handle, copies, after):
    send_sems, recv_sems, bufs = handle
    n = len(bufs)

    def body(*refs):
        for cp in copies(refs[:n], refs[n], refs[n + 1]):
            cp.wait_send()
            cp.wait_recv()

    out = pl.pallas_call(
        body, name=name, out_shape=tuple(pltpu.HBM(a.shape, a.dtype) for a in bufs),
        in_specs=[HBM] * n + [SEM, SEM, pl.BlockSpec(memory_space=pl.ANY)], out_specs=tuple([HBM] * n),
        input_output_aliases={i: i for i in range(n)},
        compiler_params=pltpu.CompilerParams(has_side_effects=EFFECT),
    )(*bufs, send_sems, recv_sems, after)
    return list(out)


class _GatherFirstCopies:
    def __init__(self, n):
        self.n, self.count = n, 4 * n

    def __call__(self, refs, send_sems, recv_sems):
        x, y, c = _place()
        peers = [(x, y, 1 - c), (1 - x, y, c), (x, 1 - y, c), (1 - x, 1 - y, c)]
        return [pltpu.make_async_remote_copy(
            src_ref=refs[a], dst_ref=refs[self.n + a].at[4 * x + 2 * y + c],
            send_sem=send_sems.at[4 * a + k], recv_sem=recv_sems.at[4 * a + k], device_id=peer, device_id_type=MESH)
            for a in range(self.n) for k, peer in enumerate(peers)]


class _GatherPassCopies:
    def __init__(self, n):
        self.n, self.count = n, 3 * n

    def __call__(self, refs, send_sems, recv_sems):
        x, y, c = _place()
        cps = []
        for a in range(self.n):
            for j, (px, py) in enumerate([(1 - x, y), (x, 1 - y), (1 - x, 1 - y)]):
                slot = refs[a].at[4 * px + 2 * py + c]
                cps.append(pltpu.make_async_remote_copy(
                    src_ref=slot, dst_ref=slot, send_sem=send_sems.at[3 * a + j], recv_sem=recv_sems.at[3 * a + j],
                    device_id=(x, y, 1 - c), device_id_type=MESH))
        return cps


class _ExchangeCopies:
    def __init__(self, n):
        self.n, self.count = n, 3 * n

    def __call__(self, refs, send_sems, recv_sems):
        x, y, c = _place()
        cps = []
        for a in range(self.n):
            for j, (px, py) in enumerate([(1 - x, y), (x, 1 - y), (1 - x, 1 - y)]):
                cps.append(pltpu.make_async_remote_copy(
                    src_ref=refs[a].at[2 * px + py], dst_ref=refs[self.n + a].at[2 * x + y],
                    send_sem=send_sems.at[3 * a + j], recv_sem=recv_sems.at[3 * a + j],
                    device_id=(px, py, c), device_id_type=MESH))
        return cps


def _own_slot(nslot, src, index):
    land = lax.empty((nslot,) + src.shape, src.dtype)
    return lax.dynamic_update_slice(land, src[None], (index,) + (0,) * src.ndim)


def _start_gather(tag, xs, me):
    lands = [_own_slot(NDEV, a, me) for a in xs]
    return _split_start(tag + "_start", list(xs) + lands, _GatherFirstCopies(len(xs)))


def _pass_gather(tag, handle, after):
    n = len(handle[2]) // 2
    lands = _split_wait(tag + "_wait", handle, _GatherFirstCopies(n), after)[n:]
    return _split_start(tag + "_pass", lands, _GatherPassCopies(n))


def _end_gather(tag, passing, after):
    return _split_wait(tag + "_pass_wait", passing, _GatherPassCopies(len(passing[2])), after)


def _finish_gather(tag, handle, after):
    passing, token = _pass_gather(tag, handle, after)
    return _end_gather(tag, passing, token)


def _start_exchange(tag, hs, mychip):
    lands = [_own_slot(4, lax.dynamic_index_in_dim(h, mychip, 0, keepdims=False), mychip) for h in hs]
    return _split_start(tag + "_start", list(hs) + lands, _ExchangeCopies(len(hs)))


def _finish_exchange(tag, handle, after):
    n = len(handle[2]) // 2
    return _split_wait(tag + "_wait", handle, _ExchangeCopies(n), after)[n:]


def _mod_part(c_all, w_ada, b_cols):
    ncol = w_ada.shape[2]

    def body(c_ref, w_ref, b_ref, o_ref):
        cv = c_ref[:, 0, :]
        ca = cv * _sigmoid(cv)
        o_ref[...] = _dot(ca.astype(BF16), w_ref[0].astype(BF16)) + b_ref[...]

    return pl.pallas_call(body, name="mod_part", out_shape=jax.ShapeDtypeStruct((NDEV, ncol), F32))(
        c_all, w_ada, b_cols)


def _ada_grad(c_all_t, dmod_cols):
    ncol = dmod_cols.shape[1]

    def body(ct_ref, dm_ref, o_ref):
        ct = ct_ref[...]
        ca = ct * _sigmoid(ct)
        acc = jnp.zeros((D, ncol), F32)
        for b in range(NDEV):
            acc = acc + ca[:, b:b + 1] * dm_ref[pl.ds(b, 1), :]
        o_ref[0] = acc

    return pl.pallas_call(body, name="ada_grad", out_shape=jax.ShapeDtypeStruct((1, D, ncol), F32))(
        c_all_t, dmod_cols)


def _fwd_in(x2d, vecs, w_in_g, tm):
    s = x2d.shape[0]
    nc = w_in_g.shape[2]

    def body(x_ref, v_ref, w_ref, z_ref, a0_ref, h1t_ref):
        xn, _ = _rms(x_ref[...])
        h = (xn * _row(v_ref, G1)) * (1.0 + _row(v_ref, SC1)) + _row(v_ref, SH1)
        hb = h.astype(BF16)
        h1t_ref[...] = hb.T
        for d in range(NDEV):
            z_ref[:, pl.ds(d * nc, nc)] = _dot(hb, w_ref[d])
        a0_ref[...] = z_ref[:, :DC] * _sigmoid(z_ref[:, DC:2 * DC])

    return pl.pallas_call(
        body, name="fwd_in", grid=(s // tm,),
        in_specs=[pl.BlockSpec((tm, D), lambda i: (i, 0)), _full((VROWS, D)), _full((NDEV, D, nc))],
        out_specs=[pl.BlockSpec((tm, 4 * DC), lambda i: (i, 0)), pl.BlockSpec((tm, DC), lambda i: (i, 0)),
                   pl.BlockSpec((D, tm), lambda i: (0, i))],
        out_shape=[jax.ShapeDtypeStruct((s, 4 * DC), F32), jax.ShapeDtypeStruct((s, DC), F32),
                   jax.ShapeDtypeStruct((D, s), BF16)],
        compiler_params=_arb(),
    )(x2d, vecs, w_in_g)


def _causal_mask(lower):
    r = lax.broadcasted_iota(jnp.int32, (CHUNK, CHUNK), 0)
    c = lax.broadcasted_iota(jnp.int32, (CHUNK, CHUNK), 1)
    return (r >= c) if lower else (r <= c)


def _first_head_lanes():
    return lax.broadcasted_iota(jnp.int32, (CHUNK, CHUNK), 1) < HD


def _fwd_mid(a0, z, x2d, vecs, v512, conv_w, gm_ws, bs_exp, w_out_b, tm):
    s = x2d.shape[0]
    hb = tm // HALO_C

    def body(a0_ref, halo_ref, zg_ref, x_ref, v_ref, p_ref, cw_ref, ws_ref, bs_ref, wo_ref,
             xh_ref, sp_ref, x2_ref, o1_ref, h2_ref, gu_ref, dgu_ref, vh_ref, dgv_ref, st_ref, a1_s):
        i = pl.program_id(0)
        for c0 in range(0, DC, LANES):
            cols = pl.ds(c0, LANES)
            halo = halo_ref[:, cols]
            e = jnp.concatenate([jnp.where(i > 0, halo, jnp.zeros_like(halo)), a0_ref[:, cols]], axis=0)
            acc = jnp.broadcast_to(p_ref[pl.ds(CB, 1), cols], (tm, LANES))
            for k in range(KC):
                acc = acc + _shift_up(e, HALO_C - (KC - 1) + k)[:tm, :] * cw_ref[pl.ds(k, 1), cols]
            a1_s[:, cols] = acc
        xh, rstd = _ln(a1_s[...])
        xh_ref[...] = xh
        a2 = xh * _row(p_ref, CLG) + _row(p_ref, CLB)
        a3 = a2 * _sigmoid(a2)
        gu_pre = zg_ref[:, :DC]
        gu, tu = _gelu(gu_pre)
        gu_ref[...] = gu
        dgu_ref[...] = _gelu_grad(gu_pre, tu)
        gv_pre = zg_ref[:, DC:]
        gvg, tv = _gelu(gv_pre)
        dgv_ref[...] = _gelu_grad(gv_pre, tv)
        vh, vrstd = _ln(gvg)
        vh_ref[...] = vh
        st_ref[...] = jnp.concatenate([jnp.broadcast_to(rstd, (tm, LANES)), jnp.broadcast_to(vrstd, (tm, LANES))], axis=1)
        gvn = (vh * _row(p_ref, GLG) + _row(p_ref, GLB)).astype(BF16)
        low = _causal_mask(True)
        first = _first_head_lanes()
        wm = [jnp.where(low, ws_ref[0, h], 0.0).astype(BF16) for h in range(NH)]
        for n in range(tm // CHUNK):
            for p in range(NH // 2):
                v = gvn[n * CHUNK:(n + 1) * CHUNK, p * CHUNK:(p + 1) * CHUNK]
                blk = jnp.where(first, _dot(wm[2 * p], v), _dot(wm[2 * p + 1], v))
                sp_ref[pl.ds(n * CHUNK, CHUNK), pl.ds(p * CHUNK, CHUNK)] = blk + bs_ref[:, pl.ds(p * CHUNK, CHUNK)]
        g = gu * sp_ref[...]
        an, _ = _rms(a3)
        gn, _ = _rms(g)
        mog = _row(v_ref, MOG)
        y = jnp.concatenate([an * mog[:, :DC], gn * mog[:, DC:]], axis=1).astype(BF16)
        o1 = _dot(y, wo_ref[...])
        o1_ref[...] = o1
        x2 = x_ref[...] + _row(v_ref, GT1) * o1
        x2_ref[...] = x2
        xn2, _ = _rms(x2)
        h2 = (xn2 * _row(v_ref, G2)) * (1.0 + _row(v_ref, SC2)) + _row(v_ref, SH2)
        h2_ref[...] = h2.astype(BF16)

    tile = lambda w: pl.BlockSpec((tm, w), lambda i: (i, 0))
    return pl.pallas_call(
        body, name="fwd_mid", grid=(s // tm,),
        in_specs=[tile(DC), pl.BlockSpec((HALO_C, DC), lambda i: (jnp.maximum(i * hb - 1, 0), 0)),
                  pl.BlockSpec((tm, 2 * DC), lambda i: (i, 1)), tile(D), _full((VROWS, D)), _full((SUB, DC)),
                  _full((CW_ROWS, DC)), _full((1, NH, CHUNK, CHUNK)), _full((CHUNK, DC)), _full((D, D))],
        out_specs=[tile(DC), tile(DC), tile(D), tile(D), tile(D), tile(DC), tile(DC), tile(DC), tile(DC), tile(2 * LANES)],
        out_shape=[jax.ShapeDtypeStruct((s, DC), F32), jax.ShapeDtypeStruct((s, DC), F32),
                   jax.ShapeDtypeStruct((s, D), F32), jax.ShapeDtypeStruct((s, D), F32),
                   jax.ShapeDtypeStruct((s, D), BF16)] + [jax.ShapeDtypeStruct((s, DC), F32)] * 4
        + [jax.ShapeDtypeStruct((s, 2 * LANES), F32)],
        scratch_shapes=[pltpu.VMEM((tm, DC), F32)],
        compiler_params=_arb(),
    )(a0, a0, z, x2d, vecs, v512, conv_w, gm_ws, bs_exp, w_out_b)


def _ffn_conv(fw_ref, cols, p2, p1, pre):
    return (fw_ref[pl.ds(3, 1), cols] + fw_ref[pl.ds(0, 1), cols] * p2
            + fw_ref[pl.ds(1, 1), cols] * p1 + fw_ref[pl.ds(2, 1), cols] * pre)


def _fwd_ffn(h2, x2, target, vecs, ffn_wb, w_up_t, w_down_p, tm):
    s = x2.shape[0]

    def body(h2_ref, x2_ref, t_ref, v_ref, fw_ref, wu_hbm, wd_hbm,
             up_ref, vg_ref, dx3_ref, acc_ref, wu, wd, carry, stage):
        i = pl.program_id(0)

        @pl.when(i == 0)
        def _():
            for sh in range(NDEV):
                pltpu.sync_copy(wu_hbm.at[sh], stage)
                wu[sh] = stage[...].T
            pltpu.sync_copy(wd_hbm, wd)
            carry[...] = jnp.zeros_like(carry)
            acc_ref[...] = jnp.zeros_like(acc_ref)

        h2v = h2_ref[...]
        o2 = jnp.zeros((tm, D), F32)
        for j in range(4):
            conv = []
            for sh in (j, 4 + j):
                cols = pl.ds(sh * PSH, PSH)
                pre = _dot(h2v, wu[sh])
                up_ref[:, cols] = pre.astype(BF16)
                e = jnp.concatenate([carry[:, cols], pre], axis=0)
                carry[:, cols] = pre[tm - HALO_F:, :]
                conv.append(_ffn_conv(fw_ref, cols, pltpu.roll(e, 2, 0)[HALO_F:, :],
                                      pltpu.roll(e, 1, 0)[HALO_F:, :], pre))
            val, gate = conv
            vg_ref[:, pl.ds(j * PSH, PSH)] = val.astype(BF16)
            vg_ref[:, pl.ds((4 + j) * PSH, PSH)] = gate.astype(BF16)
            f = ((gate * _sigmoid(gate)) * val).astype(BF16)
            o2 = o2 + _dot(f, wd[j])
        x3 = x2_ref[...] + _row(v_ref, GT2) * o2
        xn3, r3 = _rms(x3)
        gf = _row(v_ref, GF)
        diff = xn3 * gf - t_ref[...]
        acc_ref[pl.ds(1, 1), :] += _colsum(diff * diff) * (0.5 / D)
        dout = diff * (1.0 / D)
        acc_ref[pl.ds(0, 1), :] += _colsum(dout * xn3)
        dx3 = _rms_bwd(dout * gf, xn3, r3)
        dx3_ref[...] = dx3
        acc_ref[pl.ds(2, 1), :] += _colsum(dx3 * o2)

    tile = lambda w: pl.BlockSpec((tm, w), lambda i: (i, 0))
    return pl.pallas_call(
        body, name="fwd_ffn", grid=(s // tm,),
        in_specs=[tile(D), tile(D), tile(D), _full((VROWS, D)), _full((SUB, 2 * PFF)), ANY, ANY],
        out_specs=[tile(2 * PFF), tile(2 * PFF), tile(D), _full((SUB, D))],
        out_shape=[jax.ShapeDtypeStruct((s, 2 * PFF), BF16), jax.ShapeDtypeStruct((s, 2 * PFF), BF16),
                   jax.ShapeDtypeStruct((s, D), F32), jax.ShapeDtypeStruct((SUB, D), F32)],
        scratch_shapes=[pltpu.VMEM((NDEV, D, PSH), BF16), pltpu.VMEM((4, PSH, D), BF16),
                        pltpu.VMEM((HALO_F, 2 * PFF), F32), pltpu.VMEM((PSH, D), BF16)],
        compiler_params=_arb(),
    )(h2, x2, target, vecs, ffn_wb, w_up_t, w_down_p)


def _bwd_ffn(dx3, up_pre, vg, h2, vecs, ffn_wb, w_up_t, w_down_p, tm):
    s = dx3.shape[0]
    nt = s // tm

    def body(dx3_ref, up_ref, upg_ref, val_ref, gate_ref, h2_ref, v_ref, fw_ref, fwg_ref, wu_ref, wug_ref, wd_ref,
             dh2_ref, dwu_ref, dwd_ref, accf_ref, carry):
        i = pl.program_id(1)

        @pl.when(i == 0)
        def _():
            for ref in (carry, dwu_ref, dwd_ref, accf_ref):
                ref[...] = jnp.zeros_like(ref)

        do2 = (dx3_ref[...] * _row(v_ref, GT2)).astype(BF16)
        df = _dot_nt(do2, wd_ref[...])
        val = val_ref[...].astype(F32)
        gate = gate_ref[...].astype(F32)
        sg = _sigmoid(gate)
        sl = gate * sg
        f_t = (sl * val).astype(BF16).T
        dwd_ref[...] += _dot(f_t, do2)[:NSH, :]
        dups = (df * sl, df * val * (sg * (1.0 + gate * (1.0 - sg))))
        h2v = h2_ref[...]
        dh2 = jnp.zeros((tm, D), F32)
        for half, (dup, pre_ref, w_ref, wmat_ref) in enumerate(
                zip(dups, (up_ref, upg_ref), (fw_ref, fwg_ref), (wu_ref, wug_ref))):
            cols = pl.ds(half * PSH, PSH)
            e = jnp.concatenate([dup, carry[:, cols]], axis=0)
            carry[:, cols] = dup[:HALO_F, :]
            d1 = _shift_up(e, 1)[:tm, :]
            d2 = _shift_up(e, 2)[:tm, :]
            pre = pre_ref[...].astype(F32)
            accf_ref[half, pl.ds(3, 1), :] += _colsum(dup)
            accf_ref[half, pl.ds(0, 1), :] += _colsum(d2 * pre)
            accf_ref[half, pl.ds(1, 1), :] += _colsum(d1 * pre)
            accf_ref[half, pl.ds(2, 1), :] += _colsum(dup * pre)
            dpre = (_row(w_ref, 0) * d2 + _row(w_ref, 1) * d1 + _row(w_ref, 2) * dup).astype(BF16)
            dwu_ref[half] += _dot(dpre.T, h2v)[:NSH, :]
            dh2 = dh2 + _dot(dpre, wmat_ref[...])
        dh2_ref[...] = dh2

    rev = lambda j, i: nt - 1 - i
    in_specs = [
        pl.BlockSpec((tm, D), lambda j, i: (rev(j, i), 0)),
        pl.BlockSpec((tm, PSH), lambda j, i: (rev(j, i), j)), pl.BlockSpec((tm, PSH), lambda j, i: (rev(j, i), 4 + j)),
        pl.BlockSpec((tm, PSH), lambda j, i: (rev(j, i), j)), pl.BlockSpec((tm, PSH), lambda j, i: (rev(j, i), 4 + j)),
        pl.BlockSpec((tm, D), lambda j, i: (rev(j, i), 0)), _full((VROWS, D)),
        pl.BlockSpec((SUB, PSH), lambda j, i: (0, j)), pl.BlockSpec((SUB, PSH), lambda j, i: (0, 4 + j)),
        pl.BlockSpec((None, PSH, D), lambda j, i: (j, 0, 0)), pl.BlockSpec((None, PSH, D), lambda j, i: (4 + j, 0, 0)),
        pl.BlockSpec((None, PSH, D), lambda j, i: (j, 0, 0))]
    dh2, dw_up, dw_down, accf = pl.pallas_call(
        body, name="bwd_ffn", grid=(4, nt), in_specs=in_specs,
        out_specs=[pl.BlockSpec((None, tm, D), lambda j, i: (j, rev(j, i), 0)),
                   pl.BlockSpec((2, None, NSH, D), lambda j, i: (0, j, 0, 0)),
                   pl.BlockSpec((None, NSH, D), lambda j, i: (j, 0, 0)),
                   pl.BlockSpec((2, None, SUB, PSH), lambda j, i: (0, j, 0, 0))],
        out_shape=[jax.ShapeDtypeStruct((4, s, D), F32), jax.ShapeDtypeStruct((2, 4, NSH, D), F32),
                   jax.ShapeDtypeStruct((4, NSH, D), F32), jax.ShapeDtypeStruct((2, 4, SUB, PSH), F32)],
        scratch_shapes=[pltpu.VMEM((HALO_F, 2 * PSH), F32)],
        compiler_params=_arb(2),
    )(dx3, up_pre, up_pre, vg, vg, h2, vecs, ffn_wb, ffn_wb, w_up_t, w_up_t, w_down_p)
    return dh2, dw_up.reshape(NDEV, NSH, D), dw_down, accf


def _bwd_mid(dh2, dx3, x2, x2d, o1, z, a0, xh_a, sp, gu, dgu, vh, dgv, st, vecs, v512, conv_w, gm_ws, gm_ws_t, w_out_b,
             w_in_g, tm):
    s = x2d.shape[0]
    nt = s // tm
    nc = w_in_g.shape[2]

    def body(dh2a_ref, dh2b_ref, dh2c_ref, dh2d_ref, dx3_ref, x2_ref, x_ref, o1_ref, z_ref, a0_ref, xh_ref, sp_ref,
             gu_ref, dgu_ref, vh_ref, dgv_ref, st_ref, v_ref, p_ref, cw_ref, ws_ref, wst_ref, wo_ref, wi_ref, gx_ref, dz_ref, yt_ref, do1_ref, acc_ref, accp_ref,
             dcw_ref, dws_ref, dbst_ref, dbs_s, carry, da1_s, dsp_s, dgvn_s):
        i = pl.program_id(0)

        @pl.when(i == 0)
        def _():
            for ref in (carry, dbs_s, acc_ref, accp_ref, dcw_ref, dws_ref, dbst_ref):
                ref[...] = jnp.zeros_like(ref)

        dh2v = (dh2a_ref[...] + dh2b_ref[...]) + (dh2c_ref[...] + dh2d_ref[...])
        xn2, r2 = _rms(x2_ref[...])
        g2 = _row(v_ref, G2)
        sc2 = 1.0 + _row(v_ref, SC2)
        acc_ref[pl.ds(5, 1), :] += _colsum(dh2v)
        acc_ref[pl.ds(6, 1), :] += _colsum(dh2v * (xn2 * g2))
        acc_ref[pl.ds(7, 1), :] += _colsum(dh2v * sc2 * xn2)
        dx2v = dx3_ref[...] + _rms_bwd(dh2v * sc2 * g2, xn2, r2)
        do1 = (dx2v * _row(v_ref, GT1)).astype(BF16)
        do1_ref[...] = do1
        acc_ref[pl.ds(0, 1), :] += _colsum(dx2v * o1_ref[...])
        dy = _dot_nt(do1, wo_ref[...])
        mog = _row(v_ref, MOG)

        lane_tiles = DC // LANES
        xh = xh_ref[...]
        rstd = jnp.concatenate([st_ref[:, :LANES]] * lane_tiles, axis=1)
        clg = _row(p_ref, CLG)
        a2 = xh * clg + _row(p_ref, CLB)
        s2 = _sigmoid(a2)
        a3 = a2 * s2
        an, ra = _rms(a3)
        dya = dy[:, :DC]
        da3 = _rms_bwd(dya * mog[:, :DC], an, ra)
        da2 = da3 * (s2 * (1.0 + a2 * (1.0 - s2)))
        accp_ref[pl.ds(CLB, 1), :] += _colsum(da2)
        accp_ref[pl.ds(CLG, 1), :] += _colsum(da2 * xh)
        da1 = _ln_bwd(da2 * clg, xh, rstd)
        accp_ref[pl.ds(CB, 1), :] += _colsum(da1)
        da1_s[...] = da1
        for c0 in range(0, DC, LANES):
            cols = pl.ds(c0, LANES)
            d = da1_s[:, cols]
            e = jnp.concatenate([d, carry[:, cols]], axis=0)
            carry[:, cols] = d[:HALO_C, :]
            a0c = a0_ref[:, cols]
            acc = jnp.zeros((tm, LANES), F32)
            for j in range(KC):
                ahead = _shift_up(e, j)[:tm, :]
                acc = acc + ahead * cw_ref[pl.ds(KC - 1 - j, 1), cols]
                dcw_ref[pl.ds(KC - 1 - j, 1), cols] += _colsum(a0c * ahead)
            sgc = _sigmoid(z_ref[:, pl.ds(DC + c0, LANES)])
            dz_ref[:, cols] = (acc * sgc).astype(BF16)
            dz_ref[:, pl.ds(DC + c0, LANES)] = (acc * z_ref[:, cols] * sgc * (1.0 - sgc)).astype(BF16)

        gu = gu_ref[...]
        vh = vh_ref[...]
        vrstd = jnp.concatenate([st_ref[:, LANES:]] * lane_tiles, axis=1)
        glg = _row(p_ref, GLG)
        gvn = (vh * glg + _row(p_ref, GLB)).astype(BF16)
        spv = sp_ref[...]
        g = gu * spv
        gn, rg = _rms(g)
        yt_ref[...] = jnp.concatenate([an * mog[:, :DC], gn * mog[:, DC:]], axis=1).astype(BF16).T
        acc_ref[pl.ds(4, 1), :] += jnp.concatenate([_colsum(dya * an), _colsum(dy[:, DC:] * gn)], axis=1)
        dg = _rms_bwd(dy[:, DC:] * mog[:, DC:], gn, rg)
        dz_ref[:, pl.ds(2 * DC, DC)] = (dg * spv * dgu_ref[...]).astype(BF16)
        dsp_s[...] = dg * gu
        upper = _causal_mask(False)
        first = _first_head_lanes()
        wmt = [jnp.where(upper, wst_ref[h], 0.0).astype(BF16) for h in range(NH)]
        for n in range(tm // CHUNK):
            rows = pl.ds(n * CHUNK, CHUNK)
            for p in range(NH // 2):
                cols = pl.ds(p * CHUNK, CHUNK)
                dsp = dsp_s[rows, cols]
                dbs_s[:, cols] += dsp
                da = jnp.where(first, dsp, 0.0).astype(BF16)
                db = jnp.where(first, 0.0, dsp).astype(BF16)
                v = gvn[n * CHUNK:(n + 1) * CHUNK, p * CHUNK:(p + 1) * CHUNK]
                dws_ref[2 * p] += _dot_nt(da, v)
                dws_ref[2 * p + 1] += _dot_nt(db, v)
                dgvn_s[rows, cols] = _dot(wmt[2 * p], da) + _dot(wmt[2 * p + 1], db)
        dgvn = dgvn_s[...]
        accp_ref[pl.ds(GLB, 1), :] += _colsum(dgvn)
        accp_ref[pl.ds(GLG, 1), :] += _colsum(dgvn * vh)
        dgvg = _ln_bwd(dgvn * glg, vh, vrstd)
        dz_ref[:, pl.ds(3 * DC, DC)] = (dgvg * dgv_ref[...]).astype(BF16)

        dh1 = jnp.zeros((tm, D), F32)
        for d in range(NDEV):
            dh1 = dh1 + _dot_nt(dz_ref[:, pl.ds(d * nc, nc)], wi_ref[d])
        xn, r1 = _rms(x_ref[...])
        g1 = _row(v_ref, G1)
        sc = 1.0 + _row(v_ref, SC1)
        acc_ref[pl.ds(1, 1), :] += _colsum(dh1)
        acc_ref[pl.ds(2, 1), :] += _colsum(dh1 * (xn * g1))
        acc_ref[pl.ds(3, 1), :] += _colsum(dh1 * sc * xn)
        gx_ref[...] = dx2v + _rms_bwd(dh1 * sc * g1, xn, r1)

        @pl.when(i == nt - 1)
        def _():
            low = _causal_mask(True)
            for h in range(NH):
                dws_ref[h] = jnp.where(low, dws_ref[h], 0.0)
            lane = lax.broadcasted_iota(jnp.int32, (CHUNK, CHUNK), 1)
            out = jnp.zeros((CHUNK, CHUNK), F32)
            for h in range(NH):
                hs = jnp.sum(dbs_s[:, pl.ds((h // 2) * CHUNK, CHUNK)]
                             * ((lane >= (h % 2) * HD) & (lane < (h % 2 + 1) * HD)).astype(F32),
                             axis=1, keepdims=True)
                out = jnp.where(lane == h, hs, out)
            dbst_ref[...] = out

    tile = lambda w: pl.BlockSpec((tm, w), lambda i: (nt - 1 - i, 0))
    return pl.pallas_call(
        body, name="bwd_mid", grid=(nt,),
        in_specs=[pl.BlockSpec((None, tm, D), functools.partial(lambda k, i: (k, nt - 1 - i, 0), k)) for k in range(4)]
        + [tile(D), tile(D), tile(D), tile(D), tile(2 * DC), tile(DC),
                  tile(DC), tile(DC), tile(DC), tile(DC), tile(DC), tile(DC), tile(2 * LANES), _full((VROWS, D)), _full((SUB, DC)), _full((CW_ROWS, DC)),
                  _full((NH, CHUNK, CHUNK)), _full((NH, CHUNK, CHUNK)), _full((D, D)), _full((NDEV, D, nc))],
        out_specs=[tile(D), tile(4 * DC), pl.BlockSpec((D, tm), lambda i: (0, nt - 1 - i)), tile(D),
                   _full((VROWS, D)), _full((SUB, DC)), _full((CW_ROWS, DC)),
                   _full((NH, CHUNK, CHUNK)), _full((CHUNK, CHUNK))],
        out_shape=[jax.ShapeDtypeStruct((s, D), F32), jax.ShapeDtypeStruct((s, 4 * DC), BF16),
                   jax.ShapeDtypeStruct((D, s), BF16), jax.ShapeDtypeStruct((s, D), BF16),
                   jax.ShapeDtypeStruct((VROWS, D), F32), jax.ShapeDtypeStruct((SUB, DC), F32),
                   jax.ShapeDtypeStruct((CW_ROWS, DC), F32), jax.ShapeDtypeStruct((NH, CHUNK, CHUNK), F32),
                   jax.ShapeDtypeStruct((CHUNK, CHUNK), F32)],
        scratch_shapes=[pltpu.VMEM((CHUNK, DC), F32), pltpu.VMEM((HALO_C, DC), F32), pltpu.VMEM((tm, DC), F32),
                        pltpu.VMEM((tm, DC), F32), pltpu.VMEM((tm, DC), F32)],
        compiler_params=_arb(),
    )(dh2, dh2, dh2, dh2, dx3, x2, x2d, o1, z, a0, xh_a, sp, gu, dgu, vh, dgv, st, vecs, v512, conv_w, gm_ws, gm_ws_t, w_out_b, w_in_g)


def _mm_all_slots(name, at, b, bw, tk, after):
    k1, s = at.shape
    nslot = b.shape[1] // bw

    def body(a_ref, b_ref, after_ref, o_ref):
        @pl.when(pl.program_id(0) == 0)
        def _():
            o_ref[...] = jnp.zeros_like(o_ref)

        t = _dot(a_ref[...], b_ref[...])
        for j in range(nslot):
            o_ref[j] += t[:, j * bw:(j + 1) * bw]

    return pl.pallas_call(
        body, name=name, grid=(s // tk,),
        in_specs=[pl.BlockSpec((k1, tk), lambda k: (0, k)), pl.BlockSpec((tk, nslot * bw), lambda k: (k, 0)), ANY],
        out_specs=_full((nslot, k1, bw)), out_shape=jax.ShapeDtypeStruct((nslot, k1, bw), F32),
        compiler_params=_arb(),
    )(at, b, after)


def _adam_math(w, g, m, v):
    m = ADAM_B1 * m + (1.0 - ADAM_B1) * g
    v = ADAM_B2 * v + (1.0 - ADAM_B2) * (g * g)
    m_hat = m / (1.0 - ADAM_B1 ** ADAM_STEP)
    v_hat = v / (1.0 - ADAM_B2 ** ADAM_STEP)
    delta = -ADAM_LR * (m_hat / (jnp.sqrt(v_hat) + ADAM_EPS) + ADAM_WD * w)
    return delta, m, v


def _row_block(rows, cols):
    tr = rows
    while tr * cols * 4 > ROW_BLOCK_BYTES and tr % (4 * SUB) == 0:
        tr //= 2
    return tr


def _adam3(name, w, g, m, v):
    _, rows, cols = w.shape
    tr = _row_block(rows, cols)

    def body(w_ref, g_ref, m_ref, v_ref, d_ref, mo_ref, vo_ref):
        d_ref[...], mo_ref[...], vo_ref[...] = _adam_math(w_ref[...], g_ref[...], m_ref[...], v_ref[...])

    spec = pl.BlockSpec((1, tr, cols), lambda i: (0, i, 0))
    return pl.pallas_call(
        body, name=name, grid=(rows // tr,), in_specs=[spec] * 4, out_specs=[spec] * 3,
        out_shape=[jax.ShapeDtypeStruct(w.shape, F32)] * 3, compiler_params=_arb(),
    )(w, g, m, v)


def _sum_adam(name, parts, w, m, v):
    n, rows, cols = parts.shape
    tr = _row_block(rows, cols)

    def body(p_ref, w_ref, m_ref, v_ref, g_ref, d_ref, mo_ref, vo_ref):
        g = p_ref[0].astype(F32)
        for k in range(1, n):
            g = g + p_ref[k].astype(F32)
        g_ref[0] = g
        d_ref[0], mo_ref[0], vo_ref[0] = _adam_math(w_ref[0], g, m_ref[0], v_ref[0])

    spec = pl.BlockSpec((1, tr, cols), lambda i: (0, i, 0))
    return pl.pallas_call(
        body, name=name, grid=(rows // tr,),
        in_specs=[pl.BlockSpec((n, tr, cols), lambda i: (0, i, 0))] + [spec] * 3, out_specs=[spec] * 4,
        out_shape=[jax.ShapeDtypeStruct(w.shape, F32)] * 4, compiler_params=_arb(),
    )(parts, w, m, v)


def _other_half(name, g4, other):
    _, _, rows, cols = g4.shape
    tr = _row_block(rows, cols)

    def body(c_ref, a_ref, o_ref):
        o_ref[...] = a_ref[...].astype(BF16)

    return pl.pallas_call(
        body, name=name,
        grid_spec=pltpu.PrefetchScalarGridSpec(
            num_scalar_prefetch=1, grid=(4, rows // tr),
            in_specs=[pl.BlockSpec((None, None, tr, cols), lambda k, i, c_ref: (k, c_ref[0], i, 0))],
            out_specs=pl.BlockSpec((None, tr, cols), lambda k, i, c_ref: (k, i, 0))),
        out_shape=jax.ShapeDtypeStruct((4, rows, cols), BF16), compiler_params=_arb(2),
    )(other, g4)


def _pair_add(name, g4, recv, core):
    _, _, rows, cols = g4.shape
    tr = _row_block(rows, cols)

    def body(c_ref, a_ref, b_ref, o_ref):
        o_ref[...] = (a_ref[...] + b_ref[...].astype(F32)).astype(BF16)

    return pl.pallas_call(
        body, name=name,
        grid_spec=pltpu.PrefetchScalarGridSpec(
            num_scalar_prefetch=1, grid=(4, rows // tr),
            in_specs=[pl.BlockSpec((None, None, tr, cols), lambda k, i, c_ref: (k, c_ref[0], i, 0)),
                      pl.BlockSpec((None, tr, cols), lambda k, i, c_ref: (k, i, 0))],
            out_specs=pl.BlockSpec((None, tr, cols), lambda k, i, c_ref: (k, i, 0))),
        out_shape=jax.ShapeDtypeStruct((4, rows, cols), BF16), compiler_params=_arb(2),
    )(core, g4, recv)


def _sum_small(rows_all, p_all, ws_all, bst_all, fw_all, cw_all):
    def body(a_ref, p_ref, ws_ref, bst_ref, fw_ref, cw_ref,
             g_b_ada, g_n1, g_mog, g_n2, g_gf, loss_cols, g_cb, g_clg, g_clb, g_glg, g_glb, g_ws, g_bs, fw_sum,
             cw_sum):
        def total(ref):
            t = ref[0]
            for k in range(1, NDEV):
                t = t + ref[k]
            return t

        a = total(a_ref)
        g_b_ada[...] = jnp.concatenate([a[k:k + 1, :] for k in range(6)], axis=1)
        g_n1[...] = a[6:7, :]
        g_mog[...] = a[7:8, :]
        g_n2[...] = a[8:9, :]
        g_gf[...] = a[9:10, :].reshape(D)
        loss_cols[...] = a[10:11, :]
        p = total(p_ref)
        for k, ref in zip((CB, CLG, CLB, GLG, GLB), (g_cb, g_clg, g_clb, g_glg, g_glb)):
            ref[...] = p[k:k + 1, :]
        g_ws[0] = total(ws_ref)
        g_bs[0] = jnp.transpose(total(bst_ref))[:NH, :]
        fw_sum[...] = total(fw_ref)
        cw_sum[...] = total(cw_ref)

    vec = lambda n: jax.ShapeDtypeStruct((1, n), F32)
    return pl.pallas_call(
        body, name="sum_small_grads",
        out_shape=[vec(6 * D), vec(D), vec(D), vec(D), jax.ShapeDtypeStruct((D,), F32), vec(D),
                   vec(DC), vec(DC), vec(DC), vec(DC), vec(DC),
                   jax.ShapeDtypeStruct((1, NH, CHUNK, CHUNK), F32), jax.ShapeDtypeStruct((1, NH, CHUNK), F32),
                   jax.ShapeDtypeStruct((SUB, 2 * PFF), F32), jax.ShapeDtypeStruct((CW_ROWS, DC), F32)],
    )(rows_all, p_all, ws_all, bst_all, fw_all, cw_all)


def _adam_small(quads):
    n = len(quads)

    def body(*refs):
        ins, outs = refs[:4 * n], refs[4 * n:]
        for q in range(n):
            w, g, m, v = (r[...] for r in ins[4 * q:4 * q + 4])
            outs[3 * q][...], outs[3 * q + 1][...], outs[3 * q + 2][...] = _adam_math(w, g, m, v)

    flat = [a for q in quads for a in q]
    outs = pl.pallas_call(
        body, name="adam_small",
        out_shape=[jax.ShapeDtypeStruct(q[0].shape, F32) for q in quads for _ in range(3)],
    )(*flat)
    return [tuple(outs[3 * q:3 * q + 3]) for q in range(n)]


def kernel(x, c, w_ada, b_ada, norm1_gain, w_in, conv_dw_w, conv_dw_b, conv_ln_g, conv_ln_b, gm_ln_g, gm_ln_b, gm_ws, gm_bs, mix_out_gain, w_out, norm2_gain, w_up, ffn_dw_w, ffn_dw_b, w_down, final_gain, loss_target, m_w_ada, m_b_ada, m_norm1_gain, m_w_in, m_conv_dw_w, m_conv_dw_b, m_conv_ln_g, m_conv_ln_b, m_gm_ln_g, m_gm_ln_b, m_gm_ws, m_gm_bs, m_mix_out_gain, m_w_out, m_norm2_gain, m_w_up, m_ffn_dw_w, m_ffn_dw_b, m_w_down, m_final_gain, v_w_ada, v_b_ada, v_norm1_gain, v_w_in, v_conv_dw_w, v_conv_dw_b, v_conv_ln_g, v_conv_ln_b, v_gm_ln_g, v_gm_ln_b, v_gm_ws, v_gm_bs, v_mix_out_gain, v_w_out, v_norm2_gain, v_w_up, v_ffn_dw_w, v_ffn_dw_b, v_w_down, v_final_gain):
    s = x.shape[1]
    ax, ay, ac = _place()
    me = 4 * ax + 2 * ay + ac
    n_ada = w_ada.shape[2]
    n_cw = conv_dw_w.shape[2]
    x2d = x[0]
    target = loss_target[0]
    pad_sh = lambda a: jnp.pad(a, [(0, 0)] * (a.ndim - 1) + [(0, PSH - NSH)])

    c_all, cw_all, fw_all = _all_gather("gather_small", [c, conv_dw_w[0], ffn_dw_w[0]])

    first_shards, c_all = lax.optimization_barrier(((w_in[0].astype(BF16), w_out[0].astype(BF16)), c_all))
    gather_in, token_a = _start_gather("gather_in_out", list(first_shards), me)
    c_all = c_all + token_a[0, 0]
    conv_w = jnp.pad(jnp.transpose(cw_all, (1, 0, 2)).reshape(KC, DC), ((0, CW_ROWS - KC), (0, 0)))
    ffn_w = jnp.transpose(pad_sh(fw_all), (1, 0, 2)).reshape(KF, 2 * PFF)
    ffn_b = pad_sh(ffn_dw_b.reshape(NDEV, NSH)).reshape(1, 2 * PFF)
    ffn_wb = jnp.concatenate([ffn_w, ffn_b, jnp.zeros((SUB - KF - 1, 2 * PFF), F32)], axis=0)

    b_cols = lax.dynamic_slice(b_ada, (0, me * n_ada), (1, n_ada))
    (mod_all,) = _all_gather("gather_mod", [_mod_part(c_all, w_ada, b_cols)])
    up_t = lambda a: jnp.swapaxes(a, 1, 2)
    w_up_shard = jnp.pad(up_t(w_up)[0].astype(BF16), ((0, PSH - NSH), (0, 0)))
    shards, mod_all = lax.optimization_barrier(((w_up_shard, w_down[0].astype(BF16)), mod_all))
    gather_ffn, token_c = _start_gather("gather_up_down", list(shards), me)
    mod = lax.dynamic_index_in_dim(mod_all, me, axis=1, keepdims=False).reshape(6, D)
    sh1, sc1, gt1, sh2, sc2, gt2 = [mod[k:k + 1] for k in range(6)]
    vecs = jnp.concatenate([norm1_gain, sh1, sc1, gt1, norm2_gain, sh2, sc2, gt2, mix_out_gain,
                            final_gain.reshape(1, D), jnp.zeros((6, D), F32)], axis=0)
    vecs = vecs + token_c[0, 0]
    v512 = jnp.concatenate([conv_dw_b, conv_ln_g, conv_ln_b, gm_ln_g, gm_ln_b, jnp.zeros((3, DC), F32)], axis=0)
    bs_exp = jnp.repeat(jnp.transpose(gm_bs[0]), HD, axis=1)
    gm_ws_t = jnp.swapaxes(gm_ws[0], 1, 2)

    tm_big, tm = min(TILE_BIG, s), min(TILE, s)
    w_in_g, w_out_g = _finish_gather("gather_in_out", gather_in, vecs)
    w_out_b = w_out_g.reshape(D, D)
    z, a0, h1_t = _fwd_in(x2d, vecs, w_in_g, tm_big)
    xh_a, sp, x2, o1, h2, gu, dgu, vh, dgv, ln_st = _fwd_mid(
        a0, z, x2d, vecs, v512, conv_w, gm_ws, bs_exp, w_out_b, tm_big)
    w_up_t, w_down_g = _finish_gather("gather_up_down", gather_ffn, h2)
    w_down_p = jnp.pad(w_down_g.reshape(4, NSH, D), ((0, 0), (0, PSH - NSH), (0, 0)))
    up_pre, vg, dx3, acc_f = _fwd_ffn(h2, x2, target, vecs, ffn_wb, w_up_t, w_down_p, tm)

    core = ac.reshape(1).astype(jnp.int32)
    mychip = 2 * ax + ay

    other = 1 - core

    def to_pairs(named):
        g4s = [g.reshape((4, 2) + g.shape[1:]) for _, g in named]
        halves = [_other_half("rs_other_half_" + t[0], g4, other) for t, g4 in zip(named, g4s)]
        from_sibling = _sibling_swap("rs_sibling_" + named[0][0], halves)
        return [_pair_add("rs_pair_add_" + t[0], g4, rv, core) for t, g4, rv in zip(named, g4s, from_sibling)]

    dh2, dw_up, dw_down, acc_fw = _bwd_ffn(dx3, up_pre, vg, h2, vecs, ffn_wb, w_up_t, w_down_p, tm_big)
    acc_fw = jnp.transpose(acc_fw, (2, 0, 1, 3)).reshape(SUB, 2 * PFF)
    exchange_ffn, token_x = _start_exchange("rs_chips_ffn", to_pairs(
        [("w_up", dw_up), ("w_down", dw_down.reshape(NDEV, w_down.shape[1], D))]), mychip)
    gx, dz, y_t, do1, acc_m, acc_p, dcw, dws, dbs_t = _bwd_mid(
        dh2, dx3, x2, x2d, o1, z, a0, xh_a, sp, gu, dgu, vh, dgv, ln_st, vecs + token_x[0, 0], v512, conv_w, gm_ws[0],
        gm_ws_t, w_out_b, w_in_g, tm)
    rows = jnp.concatenate([acc_m[1:3], acc_m[0:1], acc_m[5:7], acc_f[2:3], acc_m[3:5], acc_m[7:8], acc_f[0:2],
                            jnp.zeros((5, D), F32)], axis=0)
    small_gather, token_s = _start_gather("gather_small_grads", [rows, acc_p, dws, dbs_t, acc_fw, dcw], me)
    dw_in = _mm_all_slots("dw_in", h1_t, dz, w_in.shape[2], min(TK_IN, s), token_s)
    small_pass, token_p = _pass_gather("gather_small_grads", small_gather, dw_in)
    dw_out = _mm_all_slots("dw_out", y_t, do1, D, min(TK_OUT, s), token_p).reshape(NDEV, w_out.shape[1], D)
    exchange_mix, token_m = _start_exchange("rs_chips_mix", to_pairs([("w_in", dw_in), ("w_out", dw_out)]), mychip)

    rows_all, p_all, ws_all, bst_all, fwg_all, cwg_all = _end_gather("gather_small_grads", small_pass, token_m)
    (g_b_ada, g_n1, g_mog, g_n2, g_gf, loss_cols, g_cb, g_clg, g_clb, g_glg, g_glb, g_ws, g_bs, fw_sum,
     cw_sum) = _sum_small(rows_all, p_all, ws_all, bst_all, fwg_all, cwg_all)
    loss = jnp.sum(loss_cols)
    g_fb = fw_sum[3].reshape(NDEV, PSH)[:, :NSH].reshape(ffn_dw_b.shape)
    g_fw = lax.dynamic_index_in_dim(fw_sum[:KF].reshape(KF, NDEV, PSH), me, axis=1, keepdims=False)[:, :NSH]
    g_fw = g_fw.reshape(ffn_dw_w.shape)
    g_cw = lax.dynamic_slice(cw_sum, (0, me * n_cw), (KC, n_cw)).reshape(conv_dw_w.shape)
    small = [
        (b_ada, g_b_ada, m_b_ada, v_b_ada), (norm1_gain, g_n1, m_norm1_gain, v_norm1_gain),
        (conv_dw_w, g_cw, m_conv_dw_w, v_conv_dw_w), (conv_dw_b, g_cb, m_conv_dw_b, v_conv_dw_b),
        (conv_ln_g, g_clg, m_conv_ln_g, v_conv_ln_g), (conv_ln_b, g_clb, m_conv_ln_b, v_conv_ln_b),
        (gm_ln_g, g_glg, m_gm_ln_g, v_gm_ln_g), (gm_ln_b, g_glb, m_gm_ln_b, v_gm_ln_b),
        (gm_ws, g_ws, m_gm_ws, v_gm_ws), (gm_bs, g_bs, m_gm_bs, v_gm_bs),
        (mix_out_gain, g_mog, m_mix_out_gain, v_mix_out_gain), (norm2_gain, g_n2, m_norm2_gain, v_norm2_gain),
        (ffn_dw_w, g_fw, m_ffn_dw_w, v_ffn_dw_w), (ffn_dw_b, g_fb, m_ffn_dw_b, v_ffn_dw_b),
        (final_gain, g_gf, m_final_gain, v_final_gain)]
    small_out = _adam_small(small)
    res = {}
    for name, q, o in zip(("b_ada", "norm1_gain", "conv_dw_w", "conv_dw_b", "conv_ln_g", "conv_ln_b", "gm_ln_g",
                           "gm_ln_b", "gm_ws", "gm_bs", "mix_out_gain", "norm2_gain", "ffn_dw_w", "ffn_dw_b",
                           "final_gain"), small, small_out):
        res[name] = (q[1],) + o

    dmod_all = rows_all[:, :6].reshape(NDEV, 6 * D)
    dm_cols = lax.dynamic_slice(dmod_all, (0, me * n_ada), (NDEV, n_ada))
    g_ada = _ada_grad(jnp.transpose(c_all[:, 0, :]), dm_cols)
    res["w_ada"] = (g_ada,) + tuple(_adam3("adam_ada", w_ada, g_ada, m_w_ada, v_w_ada))

    big = [("w_up", up_t(w_up), up_t(m_w_up), up_t(v_w_up)), ("w_down", w_down, m_w_down, v_w_down),
           ("w_in", w_in, m_w_in, v_w_in), ("w_out", w_out, m_w_out, v_w_out)]
    from_chips = list(_finish_exchange("rs_chips_ffn", exchange_ffn, res["w_ada"][1]))
    for t, parts in zip(big[:2], from_chips):
        res[t[0]] = tuple(_sum_adam("rs_sum_adam_" + t[0], parts, t[1], t[2], t[3]))
    res["w_up"] = tuple(up_t(a) for a in res["w_up"])
    from_chips = list(_finish_exchange("rs_chips_mix", exchange_mix, res["w_down"][1]))
    for t, parts in zip(big[2:], from_chips):
        res[t[0]] = tuple(_sum_adam("rs_sum_adam_" + t[0], parts, t[1], t[2], t[3]))

    order = ("w_ada", "b_ada", "norm1_gain", "w_in", "conv_dw_w", "conv_dw_b", "conv_ln_g", "conv_ln_b", "gm_ln_g",
             "gm_ln_b", "gm_ws", "gm_bs", "mix_out_gain", "w_out", "norm2_gain", "w_up", "ffn_dw_w", "ffn_dw_b",
             "w_down", "final_gain")
    return (loss, gx.reshape(x.shape), *[res[n][0] for n in order], *[res[n][1] for n in order],
            *[res[n][2] for n in order], *[res[n][3] for n in order])
```

```python
import functools

import jax
import jax.numpy as jnp
from jax import lax
from jax.experimental import pallas as pl
from jax.experimental.pallas import tpu as pltpu

F32 = jnp.float32
BF16 = jnp.bfloat16
NDEV = 8
D = 1024
DC = 512
DFF = 2816
NSH = 704
PSH = 768
PFF = 4 * PSH
KC = 31
KF = 3
CHUNK = 128
NH = 8
HD = 64
HALO_C = 32
HALO_F = 8
LANES = 128
SUB = 8
VROWS = 16
CW_ROWS = 32
TILE_BIG = 512
ROW_BLOCK_BYTES = 2 << 20
TILE = 256
TK_IN = 2048
TK_OUT = 4096
RMS_EPS = 1e-6
LN_EPS = 1e-5
ADAM_LR = 0.001
ADAM_B1 = 0.9
ADAM_B2 = 0.999
ADAM_EPS = 1e-08
ADAM_WD = 0.01
ADAM_STEP = 10
GELU_K = 0.7978845608028654
GELU_C = 0.044715

MESH = pl.DeviceIdType.MESH
ANY = pl.BlockSpec(memory_space=pl.ANY)

G1, SH1, SC1, GT1, G2, SH2, SC2, GT2, MOG, GF = range(10)
CB, CLG, CLB, GLG, GLB = range(5)


def _full(shape):
    return pl.BlockSpec(shape, lambda *_: (0,) * len(shape))


def _arb(n=1):
    return pltpu.CompilerParams(dimension_semantics=("arbitrary",) * n)


def _row(ref, r):
    return ref[pl.ds(r, 1), :]


def _colsum(v):
    return jnp.sum(v, axis=0, keepdims=True)


def _rowmean(v):
    return jnp.mean(v, axis=-1, keepdims=True)


def _rms(x):
    r = lax.rsqrt(_rowmean(x * x) + RMS_EPS)
    return x * r, r


def _rms_bwd(dxn, xn, r):
    return r * (dxn - xn * _rowmean(dxn * xn))


def _ln(x):
    mu = _rowmean(x)
    xc = x - mu
    rstd = lax.rsqrt(_rowmean(xc * xc) + LN_EPS)
    return xc * rstd, rstd


def _ln_bwd(dxh, xhat, rstd):
    return rstd * (dxh - _rowmean(dxh) - xhat * _rowmean(dxh * xhat))


def _sigmoid(x):
    return 0.5 * jnp.tanh(0.5 * x) + 0.5


def _gelu(x):
    t = jnp.tanh(GELU_K * (x + GELU_C * x * x * x))
    return 0.5 * x * (1.0 + t), t


def _gelu_grad(x, t):
    return 0.5 * (1.0 + t) + 0.5 * x * (1.0 - t * t) * (GELU_K * (1.0 + 3.0 * GELU_C * x * x))


def _dot(a, b):
    return jnp.dot(a, b, preferred_element_type=F32)


def _dot_nt(a, b):
    return lax.dot_general(a, b, (((1,), (1,)), ((), ())), preferred_element_type=F32)


def _shift_up(e, s):
    n = e.shape[0]
    return pltpu.roll(e, (n - s) % n, 0)


def _place():
    return lax.axis_index("x"), lax.axis_index("y"), lax.axis_index("c")


def _all_gather(name, xs):
    n = len(xs)

    def body(*refs):
        x_refs, out_refs = refs[:n], refs[n:2 * n]
        send_sems, recv_sems, local_sems = refs[2 * n:]
        x, y, c = _place()
        me, sibling = (x, y, c), (x, y, 1 - c)
        chips = [(1 - x, y), (x, 1 - y), (1 - x, 1 - y)]

        def copy(a, k, block, to, own=False):
            px, py, pc = block
            slot = out_refs[a].at[4 * px + 2 * py + pc]
            return pltpu.make_async_remote_copy(
                src_ref=x_refs[a] if own else slot, dst_ref=slot,
                send_sem=send_sems.at[7 * a + k], recv_sem=recv_sems.at[7 * a + k], device_id=to, device_id_type=MESH)

        mine = [pltpu.make_async_copy(x_refs[a], out_refs[a].at[4 * x + 2 * y + c], local_sems.at[a]) for a in range(n)]
        for cp in mine:
            cp.start()
        first = []
        for a in range(n):
            first.append(copy(a, 0, me, sibling, own=True))
            first += [copy(a, 1 + j, me, (*chip, c), own=True) for j, chip in enumerate(chips)]
        for cp in first:
            cp.start()
        passed = []
        for j, chip in enumerate(chips):
            for a in range(n):
                copy(a, 1 + j, (*chip, c), me).wait_recv()
                cp = copy(a, 4 + j, (*chip, c), sibling)
                cp.start()
                passed.append(cp)
        for a in range(n):
            copy(a, 0, sibling, me).wait_recv()
            for j, chip in enumerate(chips):
                copy(a, 4 + j, (*chip, 1 - c), me).wait_recv()
        for cp in first + passed:
            cp.wait_send()
        for cp in mine:
            cp.wait()

    return pl.pallas_call(
        body, name=name, out_shape=[jax.ShapeDtypeStruct((NDEV,) + a.shape, a.dtype) for a in xs],
        in_specs=[ANY] * n, out_specs=[ANY] * n,
        scratch_shapes=[pltpu.SemaphoreType.DMA((7 * n,)), pltpu.SemaphoreType.DMA((7 * n,)),
                        pltpu.SemaphoreType.DMA((n,))],
    )(*xs)


def _sibling_swap(name, hs):
    n = len(hs)

    def body(*refs):
        h_refs, out_refs = refs[:n], refs[n:2 * n]
        send_sems, recv_sems = refs[2 * n:]
        x, y, c = _place()
        cps = [pltpu.make_async_remote_copy(
            src_ref=h_refs[a].at[k], dst_ref=out_refs[a].at[k],
            send_sem=send_sems.at[4 * a + k], recv_sem=recv_sems.at[4 * a + k],
            device_id=(x, y, 1 - c), device_id_type=MESH) for a in range(n) for k in range(4)]
        for cp in cps:
            cp.start()
        for cp in cps:
            cp.wait()

    return pl.pallas_call(
        body, name=name, out_shape=[jax.ShapeDtypeStruct(h.shape, h.dtype) for h in hs],
        in_specs=[ANY] * n, out_specs=[ANY] * n,
        scratch_shapes=[pltpu.SemaphoreType.DMA((4 * n,)), pltpu.SemaphoreType.DMA((4 * n,))],
    )(*hs)


HBM = pl.BlockSpec(memory_space=pltpu.HBM)
SEM = pl.BlockSpec(memory_space=pltpu.SEMAPHORE)
EFFECT = pltpu.SideEffectType.DATAFLOW_SIDE_EFFECTING


def _in_hbm(a):
    return pltpu.with_memory_space_constraint(a, pltpu.HBM)


def _split_start(name, bufs, copies):
    n = len(bufs)

    def body(*refs):
        for cp in copies(refs[:n], refs[n], refs[n + 1]):
            cp.start()
        refs[-1][...] = jnp.zeros_like(refs[-1])

    out = pl.pallas_call(
        body, name=name,
        out_shape=(pltpu.SemaphoreType.DMA((copies.count,)), pltpu.SemaphoreType.DMA((copies.count,)),
                   *[pltpu.HBM(a.shape, a.dtype) for a in bufs], jax.ShapeDtypeStruct((SUB, LANES), F32)),
        in_specs=[HBM] * n, out_specs=(SEM, SEM, *[HBM] * n, pl.BlockSpec(memory_space=pltpu.VMEM)),
        input_output_aliases={i: 2 + i for i in range(n)},
        compiler_params=pltpu.CompilerParams(has_side_effects=EFFECT),
    )(*[_in_hbm(a) for a in bufs])
    return (out[0], out[1], list(out[2:2 + n])), out[-1]


def _split_wait(name, handle, copies, after):
    send_sems, recv_sems, bufs = handle
    n = len(bufs)

    def body(*refs):
        for cp in copies(refs[:n], refs[n], refs[n + 1]):
            cp.wait_send()
            cp.wait_recv()

    out = pl.pallas_call(
        body, name=name, out_shape=tuple(pltpu.HBM(a.shape, a.dtype) for a in bufs),
        in_specs=[HBM] * n + [SEM, SEM, pl.BlockSpec(memory_space=pl.ANY)], out_specs=tuple([HBM] * n),
        input_output_aliases={i: i for i in range(n)},
        compiler_params=pltpu.CompilerParams(has_side_effects=EFFECT),
    )(*bufs, send_sems, recv_sems, after)
    return list(out)


class _GatherFirstCopies:
    def __init__(self, n):
        self.n, self.count = n, 4 * n

    def __call__(self, refs, send_sems, recv_sems):
        x, y, c = _place()
        peers = [(x, y, 1 - c), (1 - x, y, c), (x, 1 - y, c), (1 - x, 1 - y, c)]
        return [pltpu.make_async_remote_copy(
            src_ref=refs[a], dst_ref=refs[self.n + a].at[4 * x + 2 * y + c],
            send_sem=send_sems.at[4 * a + k], recv_sem=recv_sems.at[4 * a + k], device_id=peer, device_id_type=MESH)
            for a in range(self.n) for k, peer in enumerate(peers)]


class _GatherPassCopies:
    def __init__(self, n):
        self.n, self.count = n, 3 * n

    def __call__(self, refs, send_sems, recv_sems):
        x, y, c = _place()
        cps = []
        for a in range(self.n):
            for j, (px, py) in enumerate([(1 - x, y), (x, 1 - y), (1 - x, 1 - y)]):
                slot = refs[a].at[4 * px + 2 * py + c]
                cps.append(pltpu.make_async_remote_copy(
                    src_ref=slot, dst_ref=slot, send_sem=send_sems.at[3 * a + j], recv_sem=recv_sems.at[3 * a + j],
                    device_id=(x, y, 1 - c), device_id_type=MESH))
        return cps


class _ExchangeCopies:
    def __init__(self, n):
        self.n, self.count = n, 3 * n

    def __call__(self, refs, send_sems, recv_sems):
        x, y, c = _place()
        cps = []
        for a in range(self.n):
            for j, (px, py) in enumerate([(1 - x, y), (x, 1 - y), (1 - x, 1 - y)]):
                cps.append(pltpu.make_async_remote_copy(
                    src_ref=refs[a].at[2 * px + py], dst_ref=refs[self.n + a].at[2 * x + y],
                    send_sem=send_sems.at[3 * a + j], recv_sem=recv_sems.at[3 * a + j],
                    device_id=(px, py, c), device_id_type=MESH))
        return cps


def _own_slot(nslot, src, index):
    land = lax.empty((nslot,) + src.shape, src.dtype)
    return lax.dynamic_update_slice(land, src[None], (index,) + (0,) * src.ndim)


def _start_gather(tag, xs, me):
    lands = [_own_slot(NDEV, a, me) for a in xs]
    return _split_start(tag + "_start", list(xs) + lands, _GatherFirstCopies(len(xs)))


def _pass_gather(tag, handle, after):
    n = len(handle[2]) // 2
    lands = _split_wait(tag + "_wait", handle, _GatherFirstCopies(n), after)[n:]
    return _split_start(tag + "_pass", lands, _GatherPassCopies(n))


def _end_gather(tag, passing, after):
    return _split_wait(tag + "_pass_wait", passing, _GatherPassCopies(len(passing[2])), after)


def _finish_gather(tag, handle, after):
    passing, token = _pass_gather(tag, handle, after)
    return _end_gather(tag, passing, token)


def _start_exchange(tag, hs, mychip):
    lands = [_own_slot(4, lax.dynamic_index_in_dim(h, mychip, 0, keepdims=False), mychip) for h in hs]
    return _split_start(tag + "_start", list(hs) + lands, _ExchangeCopies(len(hs)))


def _finish_exchange(tag, handle, after):
    n = len(handle[2]) // 2
    return _split_wait(tag + "_wait", handle, _ExchangeCopies(n), after)[n:]


def _mod_part(c_all, w_ada, b_cols):
    ncol = w_ada.shape[2]

    def body(c_ref, w_ref, b_ref, o_ref):
        cv = c_ref[:, 0, :]
        ca = cv * _sigmoid(cv)
        o_ref[...] = _dot(ca.astype(BF16), w_ref[0].astype(BF16)) + b_ref[...]

    return pl.pallas_call(body, name="mod_part", out_shape=jax.ShapeDtypeStruct((NDEV, ncol), F32))(
        c_all, w_ada, b_cols)


def _ada_grad(c_all_t, dmod_cols):
    ncol = dmod_cols.shape[1]

    def body(ct_ref, dm_ref, o_ref):
        ct = ct_ref[...]
        ca = ct * _sigmoid(ct)
        acc = jnp.zeros((D, ncol), F32)
        for b in range(NDEV):
            acc = acc + ca[:, b:b + 1] * dm_ref[pl.ds(b, 1), :]
        o_ref[0] = acc

    return pl.pallas_call(body, name="ada_grad", out_shape=jax.ShapeDtypeStruct((1, D, ncol), F32))(
        c_all_t, dmod_cols)


def _fwd_in(x2d, vecs, w_in_g, tm):
    s = x2d.shape[0]
    nc = w_in_g.shape[2]

    def body(x_ref, v_ref, w_ref, z_ref, a0_ref, h1t_ref):
        xn, _ = _rms(x_ref[...])
        h = (xn * _row(v_ref, G1)) * (1.0 + _row(v_ref, SC1)) + _row(v_ref, SH1)
        hb = h.astype(BF16)
        h1t_ref[...] = hb.T
        for d in range(NDEV):
            z_ref[:, pl.ds(d * nc, nc)] = _dot(hb, w_ref[d])
        a0_ref[...] = z_ref[:, :DC] * _sigmoid(z_ref[:, DC:2 * DC])

    return pl.pallas_call(
        body, name="fwd_in", grid=(s // tm,),
        in_specs=[pl.BlockSpec((tm, D), lambda i: (i, 0)), _full((VROWS, D)), _full((NDEV, D, nc))],
        out_specs=[pl.BlockSpec((tm, 4 * DC), lambda i: (i, 0)), pl.BlockSpec((tm, DC), lambda i: (i, 0)),
                   pl.BlockSpec((D, tm), lambda i: (0, i))],
        out_shape=[jax.ShapeDtypeStruct((s, 4 * DC), F32), jax.ShapeDtypeStruct((s, DC), F32),
                   jax.ShapeDtypeStruct((D, s), BF16)],
        compiler_params=_arb(),
    )(x2d, vecs, w_in_g)


def _causal_mask(lower):
    r = lax.broadcasted_iota(jnp.int32, (CHUNK, CHUNK), 0)
    c = lax.broadcasted_iota(jnp.int32, (CHUNK, CHUNK), 1)
    return (r >= c) if lower else (r <= c)


def _first_head_lanes():
    return lax.broadcasted_iota(jnp.int32, (CHUNK, CHUNK), 1) < HD


def _fwd_mid(a0, z, x2d, vecs, v512, conv_w, gm_ws, bs_exp, w_out_b, tm):
    s = x2d.shape[0]
    hb = tm // HALO_C

    def body(a0_ref, halo_ref, zg_ref, x_ref, v_ref, p_ref, cw_ref, ws_ref, bs_ref, wo_ref,
             xh_ref, sp_ref, x2_ref, o1_ref, h2_ref, gu_ref, dgu_ref, vh_ref, dgv_ref, st_ref, h2t_ref, a1_s):
        i = pl.program_id(0)
        for c0 in range(0, DC, LANES):
            cols = pl.ds(c0, LANES)
            halo = halo_ref[:, cols]
            e = jnp.concatenate([jnp.where(i > 0, halo, jnp.zeros_like(halo)), a0_ref[:, cols]], axis=0)
            acc = jnp.broadcast_to(p_ref[pl.ds(CB, 1), cols], (tm, LANES))
            for k in range(KC):
                acc = acc + _shift_up(e, HALO_C - (KC - 1) + k)[:tm, :] * cw_ref[pl.ds(k, 1), cols]
            a1_s[:, cols] = acc
        xh, rstd = _ln(a1_s[...])
        xh_ref[...] = xh
        a2 = xh * _row(p_ref, CLG) + _row(p_ref, CLB)
        a3 = a2 * _sigmoid(a2)
        gu_pre = zg_ref[:, :DC]
        gu, tu = _gelu(gu_pre)
        gu_ref[...] = gu
        dgu_ref[...] = _gelu_grad(gu_pre, tu)
        gv_pre = zg_ref[:, DC:]
        gvg, tv = _gelu(gv_pre)
        dgv_ref[...] = _gelu_grad(gv_pre, tv)
        vh, vrstd = _ln(gvg)
        vh_ref[...] = vh
        st_ref[...] = jnp.concatenate([jnp.broadcast_to(rstd, (tm, LANES)), jnp.broadcast_to(vrstd, (tm, LANES))], axis=1)
        gvn = (vh * _row(p_ref, GLG) + _row(p_ref, GLB)).astype(BF16)
        low = _causal_mask(True)
        first = _first_head_lanes()
        wm = [jnp.where(low, ws_ref[0, h], 0.0).astype(BF16) for h in range(NH)]
        for n in range(tm // CHUNK):
            for p in range(NH // 2):
                v = gvn[n * CHUNK:(n + 1) * CHUNK, p * CHUNK:(p + 1) * CHUNK]
                blk = jnp.where(first, _dot(wm[2 * p], v), _dot(wm[2 * p + 1], v))
                sp_ref[pl.ds(n * CHUNK, CHUNK), pl.ds(p * CHUNK, CHUNK)] = blk + bs_ref[:, pl.ds(p * CHUNK, CHUNK)]
        g = gu * sp_ref[...]
        an, _ = _rms(a3)
        gn, _ = _rms(g)
        mog = _row(v_ref, MOG)
        y = jnp.concatenate([an * mog[:, :DC], gn * mog[:, DC:]], axis=1).astype(BF16)
        o1 = _dot(y, wo_ref[...])
        o1_ref[...] = o1
        x2 = x_ref[...] + _row(v_ref, GT1) * o1
        x2_ref[...] = x2
        xn2, _ = _rms(x2)
        h2 = (xn2 * _row(v_ref, G2)) * (1.0 + _row(v_ref, SC2)) + _row(v_ref, SH2)
        h2b = h2.astype(BF16)
        h2_ref[...] = h2b
        h2t_ref[...] = h2b.T

    tile = lambda w: pl.BlockSpec((tm, w), lambda i: (i, 0))
    return pl.pallas_call(
        body, name="fwd_mid", grid=(s // tm,),
        in_specs=[tile(DC), pl.BlockSpec((HALO_C, DC), lambda i: (jnp.maximum(i * hb - 1, 0), 0)),
                  pl.BlockSpec((tm, 2 * DC), lambda i: (i, 1)), tile(D), _full((VROWS, D)), _full((SUB, DC)),
                  _full((CW_ROWS, DC)), _full((1, NH, CHUNK, CHUNK)), _full((CHUNK, DC)), _full((D, D))],
        out_specs=[tile(DC), tile(DC), tile(D), tile(D), tile(D), tile(DC), tile(DC), tile(DC), tile(DC), tile(2 * LANES),
                   pl.BlockSpec((D, tm), lambda i: (0, i))],
        out_shape=[jax.ShapeDtypeStruct((s, DC), F32), jax.ShapeDtypeStruct((s, DC), F32),
                   jax.ShapeDtypeStruct((s, D), F32), jax.ShapeDtypeStruct((s, D), F32),
                   jax.ShapeDtypeStruct((s, D), BF16)] + [jax.ShapeDtypeStruct((s, DC), F32)] * 4
        + [jax.ShapeDtypeStruct((s, 2 * LANES), F32), jax.ShapeDtypeStruct((D, s), BF16)],
        scratch_shapes=[pltpu.VMEM((tm, DC), F32)],
        compiler_params=_arb(),
    )(a0, a0, z, x2d, vecs, v512, conv_w, gm_ws, bs_exp, w_out_b)


def _ffn_conv(fw_ref, cols, p2, p1, pre):
    return (fw_ref[pl.ds(3, 1), cols] + fw_ref[pl.ds(0, 1), cols] * p2
            + fw_ref[pl.ds(1, 1), cols] * p1 + fw_ref[pl.ds(2, 1), cols] * pre)


def _fwd_ffn(h2, x2, target, vecs, ffn_wb, w_up_t, w_down_p, tm):
    s = x2.shape[0]

    def body(h2_ref, x2_ref, t_ref, v_ref, fw_ref, wu_hbm, wd_hbm,
             up_ref, vg_ref, dx3_ref, acc_ref, wu, wd, carry, stage):
        i = pl.program_id(0)

        @pl.when(i == 0)
        def _():
            for sh in range(NDEV):
                pltpu.sync_copy(wu_hbm.at[sh], stage)
                wu[sh] = stage[...].T
            pltpu.sync_copy(wd_hbm, wd)
            carry[...] = jnp.zeros_like(carry)
            acc_ref[...] = jnp.zeros_like(acc_ref)

        h2v = h2_ref[...]
        o2 = jnp.zeros((tm, D), F32)
        for j in range(4):
            conv = []
            for sh in (j, 4 + j):
                cols = pl.ds(sh * PSH, PSH)
                pre = _dot(h2v, wu[sh])
                up_ref[:, cols] = pre.astype(BF16)
                e = jnp.concatenate([carry[:, cols], pre], axis=0)
                carry[:, cols] = pre[tm - HALO_F:, :]
                conv.append(_ffn_conv(fw_ref, cols, pltpu.roll(e, 2, 0)[HALO_F:, :],
                                      pltpu.roll(e, 1, 0)[HALO_F:, :], pre))
            val, gate = conv
            vg_ref[:, pl.ds(j * PSH, PSH)] = val.astype(BF16)
            vg_ref[:, pl.ds((4 + j) * PSH, PSH)] = gate.astype(BF16)
            f = ((gate * _sigmoid(gate)) * val).astype(BF16)
            o2 = o2 + _dot(f, wd[j])
        x3 = x2_ref[...] + _row(v_ref, GT2) * o2
        xn3, r3 = _rms(x3)
        gf = _row(v_ref, GF)
        diff = xn3 * gf - t_ref[...]
        acc_ref[pl.ds(1, 1), :] += _colsum(diff * diff) * (0.5 / D)
        dout = diff * (1.0 / D)
        acc_ref[pl.ds(0, 1), :] += _colsum(dout * xn3)
        dx3 = _rms_bwd(dout * gf, xn3, r3)
        dx3_ref[...] = dx3
        acc_ref[pl.ds(2, 1), :] += _colsum(dx3 * o2)

    tile = lambda w: pl.BlockSpec((tm, w), lambda i: (i, 0))
    return pl.pallas_call(
        body, name="fwd_ffn", grid=(s // tm,),
        in_specs=[tile(D), tile(D), tile(D), _full((VROWS, D)), _full((SUB, 2 * PFF)), ANY, ANY],
        out_specs=[tile(2 * PFF), tile(2 * PFF), tile(D), _full((SUB, D))],
        out_shape=[jax.ShapeDtypeStruct((s, 2 * PFF), BF16), jax.ShapeDtypeStruct((s, 2 * PFF), BF16),
                   jax.ShapeDtypeStruct((s, D), F32), jax.ShapeDtypeStruct((SUB, D), F32)],
        scratch_shapes=[pltpu.VMEM((NDEV, D, PSH), BF16), pltpu.VMEM((4, PSH, D), BF16),
                        pltpu.VMEM((HALO_F, 2 * PFF), F32), pltpu.VMEM((PSH, D), BF16)],
        compiler_params=_arb(),
    )(h2, x2, target, vecs, ffn_wb, w_up_t, w_down_p)


def _bwd_ffn(dx3, up_pre, vg, h2_t, vecs, ffn_wb, w_up_t, w_down_p, tm):
    s = dx3.shape[0]
    nt = s // tm

    def body(dx3_ref, up_ref, upg_ref, val_ref, gate_ref, h2t_ref, v_ref, fw_ref, fwg_ref, wu_ref, wug_ref, wd_ref,
             dh2_ref, dwu_ref, dwd_ref, accf_ref, carry):
        i = pl.program_id(1)

        @pl.when(i == 0)
        def _():
            for ref in (carry, dwu_ref, dwd_ref, accf_ref):
                ref[...] = jnp.zeros_like(ref)

        do2 = (dx3_ref[...] * _row(v_ref, GT2)).astype(BF16)
        df = _dot_nt(do2, wd_ref[...])
        val = val_ref[...].astype(F32)
        gate = gate_ref[...].astype(F32)
        sg = _sigmoid(gate)
        sl = gate * sg
        f_t = (sl * val).astype(BF16).T
        dwd_ref[...] += _dot(f_t, do2)[:NSH, :]
        dups = (df * sl, df * val * (sg * (1.0 + gate * (1.0 - sg))))
        h2t = h2t_ref[...]
        dh2 = jnp.zeros((tm, D), F32)
        for half, (dup, pre_ref, w_ref, wmat_ref) in enumerate(
                zip(dups, (up_ref, upg_ref), (fw_ref, fwg_ref), (wu_ref, wug_ref))):
            cols = pl.ds(half * PSH, PSH)
            e = jnp.concatenate([dup, carry[:, cols]], axis=0)
            carry[:, cols] = dup[:HALO_F, :]
            d1 = _shift_up(e, 1)[:tm, :]
            d2 = _shift_up(e, 2)[:tm, :]
            pre = pre_ref[...].astype(F32)
            accf_ref[half, pl.ds(3, 1), :] += _colsum(dup)
            accf_ref[half, pl.ds(0, 1), :] += _colsum(d2 * pre)
            accf_ref[half, pl.ds(1, 1), :] += _colsum(d1 * pre)
            accf_ref[half, pl.ds(2, 1), :] += _colsum(dup * pre)
            dpre = (_row(w_ref, 0) * d2 + _row(w_ref, 1) * d1 + _row(w_ref, 2) * dup).astype(BF16)
            dwu_ref[half] += _dot(h2t, dpre)[:, :NSH]
            dh2 = dh2 + _dot(dpre, wmat_ref[...])
        dh2_ref[...] = dh2

    rev = lambda j, i: nt - 1 - i
    in_specs = [
        pl.BlockSpec((tm, D), lambda j, i: (rev(j, i), 0)),
        pl.BlockSpec((tm, PSH), lambda j, i: (rev(j, i), j)), pl.BlockSpec((tm, PSH), lambda j, i: (rev(j, i), 4 + j)),
        pl.BlockSpec((tm, PSH), lambda j, i: (rev(j, i), j)), pl.BlockSpec((tm, PSH), lambda j, i: (rev(j, i), 4 + j)),
        pl.BlockSpec((D, tm), lambda j, i: (0, rev(j, i))), _full((VROWS, D)),
        pl.BlockSpec((SUB, PSH), lambda j, i: (0, j)), pl.BlockSpec((SUB, PSH), lambda j, i: (0, 4 + j)),
        pl.BlockSpec((None, PSH, D), lambda j, i: (j, 0, 0)), pl.BlockSpec((None, PSH, D), lambda j, i: (4 + j, 0, 0)),
        pl.BlockSpec((None, PSH, D), lambda j, i: (j, 0, 0))]
    dh2, dw_up, dw_down, accf = pl.pallas_call(
        body, name="bwd_ffn", grid=(4, nt), in_specs=in_specs,
        out_specs=[pl.BlockSpec((None, tm, D), lambda j, i: (j, rev(j, i), 0)),
                   pl.BlockSpec((2, None, D, NSH), lambda j, i: (0, j, 0, 0)),
                   pl.BlockSpec((None, NSH, D), lambda j, i: (j, 0, 0)),
                   pl.BlockSpec((2, None, SUB, PSH), lambda j, i: (0, j, 0, 0))],
        out_shape=[jax.ShapeDtypeStruct((4, s, D), F32), jax.ShapeDtypeStruct((2, 4, D, NSH), F32),
                   jax.ShapeDtypeStruct((4, NSH, D), F32), jax.ShapeDtypeStruct((2, 4, SUB, PSH), F32)],
        scratch_shapes=[pltpu.VMEM((HALO_F, 2 * PSH), F32)],
        compiler_params=_arb(2),
    )(dx3, up_pre, up_pre, vg, vg, h2_t, vecs, ffn_wb, ffn_wb, w_up_t, w_up_t, w_down_p)
    return dh2, dw_up.reshape(NDEV, D, NSH), dw_down, accf


def _bwd_mid(dh2, dx3, x2, x2d, o1, z, a0, xh_a, sp, gu, dgu, vh, dgv, st, vecs, v512, conv_w, gm_ws, gm_ws_t, w_out_b,
             w_in_g, tm):
    s = x2d.shape[0]
    nt = s // tm
    nc = w_in_g.shape[2]

    def body(dh2a_ref, dh2b_ref, dh2c_ref, dh2d_ref, dx3_ref, x2_ref, x_ref, o1_ref, z_ref, a0_ref, xh_ref, sp_ref,
             gu_ref, dgu_ref, vh_ref, dgv_ref, st_ref, v_ref, p_ref, cw_ref, ws_ref, wst_ref, wo_ref, wi_ref, gx_ref, dz_ref, yt_ref, do1_ref, acc_ref, accp_ref,
             dcw_ref, dws_ref, dbst_ref, dbs_s, carry, da1_s, dsp_s, dgvn_s):
        i = pl.program_id(0)

        @pl.when(i == 0)
        def _():
            for ref in (carry, dbs_s, acc_ref, accp_ref, dcw_ref, dws_ref, dbst_ref):
                ref[...] = jnp.zeros_like(ref)

        dh2v = (dh2a_ref[...] + dh2b_ref[...]) + (dh2c_ref[...] + dh2d_ref[...])
        xn2, r2 = _rms(x2_ref[...])
        g2 = _row(v_ref, G2)
        sc2 = 1.0 + _row(v_ref, SC2)
        acc_ref[pl.ds(5, 1), :] += _colsum(dh2v)
        acc_ref[pl.ds(6, 1), :] += _colsum(dh2v * (xn2 * g2))
        acc_ref[pl.ds(7, 1), :] += _colsum(dh2v * sc2 * xn2)
        dx2v = dx3_ref[...] + _rms_bwd(dh2v * sc2 * g2, xn2, r2)
        do1 = (dx2v * _row(v_ref, GT1)).astype(BF16)
        do1_ref[...] = do1
        acc_ref[pl.ds(0, 1), :] += _colsum(dx2v * o1_ref[...])
        dy = _dot_nt(do1, wo_ref[...])
        mog = _row(v_ref, MOG)

        lane_tiles = DC // LANES
        xh = xh_ref[...]
        rstd = jnp.concatenate([st_ref[:, :LANES]] * lane_tiles, axis=1)
        clg = _row(p_ref, CLG)
        a2 = xh * clg + _row(p_ref, CLB)
        s2 = _sigmoid(a2)
        a3 = a2 * s2
        an, ra = _rms(a3)
        dya = dy[:, :DC]
        da3 = _rms_bwd(dya * mog[:, :DC], an, ra)
        da2 = da3 * (s2 * (1.0 + a2 * (1.0 - s2)))
        accp_ref[pl.ds(CLB, 1), :] += _colsum(da2)
        accp_ref[pl.ds(CLG, 1), :] += _colsum(da2 * xh)
        da1 = _ln_bwd(da2 * clg, xh, rstd)
        accp_ref[pl.ds(CB, 1), :] += _colsum(da1)
        da1_s[...] = da1
        for c0 in range(0, DC, LANES):
            cols = pl.ds(c0, LANES)
            d = da1_s[:, cols]
            e = jnp.concatenate([d, carry[:, cols]], axis=0)
            carry[:, cols] = d[:HALO_C, :]
            a0c = a0_ref[:, cols]
            acc = jnp.zeros((tm, LANES), F32)
            for j in range(KC):
                ahead = _shift_up(e, j)[:tm, :]
                acc = acc + ahead * cw_ref[pl.ds(KC - 1 - j, 1), cols]
                dcw_ref[pl.ds(KC - 1 - j, 1), cols] += _colsum(a0c * ahead)
            sgc = _sigmoid(z_ref[:, pl.ds(DC + c0, LANES)])
            dz_ref[:, cols] = (acc * sgc).astype(BF16)
            dz_ref[:, pl.ds(DC + c0, LANES)] = (acc * z_ref[:, cols] * sgc * (1.0 - sgc)).astype(BF16)

        gu = gu_ref[...]
        vh = vh_ref[...]
        vrstd = jnp.concatenate([st_ref[:, LANES:]] * lane_tiles, axis=1)
        glg = _row(p_ref, GLG)
        gvn = (vh * glg + _row(p_ref, GLB)).astype(BF16)
        spv = sp_ref[...]
        g = gu * spv
        gn, rg = _rms(g)
        yt_ref[...] = jnp.concatenate([an * mog[:, :DC], gn * mog[:, DC:]], axis=1).astype(BF16).T
        acc_ref[pl.ds(4, 1), :] += jnp.concatenate([_colsum(dya * an), _colsum(dy[:, DC:] * gn)], axis=1)
        dg = _rms_bwd(dy[:, DC:] * mog[:, DC:], gn, rg)
        dz_ref[:, pl.ds(2 * DC, DC)] = (dg * spv * dgu_ref[...]).astype(BF16)
        dsp_s[...] = dg * gu
        upper = _causal_mask(False)
        first = _first_head_lanes()
        wmt = [jnp.where(upper, wst_ref[h], 0.0).astype(BF16) for h in range(NH)]
        for n in range(tm // CHUNK):
            rows = pl.ds(n * CHUNK, CHUNK)
            for p in range(NH // 2):
                cols = pl.ds(p * CHUNK, CHUNK)
                dsp = dsp_s[rows, cols]
                dbs_s[:, cols] += dsp
                da = jnp.where(first, dsp, 0.0).astype(BF16)
                db = jnp.where(first, 0.0, dsp).astype(BF16)
                v = gvn[n * CHUNK:(n + 1) * CHUNK, p * CHUNK:(p + 1) * CHUNK]
                dws_ref[2 * p] += _dot_nt(da, v)
                dws_ref[2 * p + 1] += _dot_nt(db, v)
                dgvn_s[rows, cols] = _dot(wmt[2 * p], da) + _dot(wmt[2 * p + 1], db)
        dgvn = dgvn_s[...]
        accp_ref[pl.ds(GLB, 1), :] += _colsum(dgvn)
        accp_ref[pl.ds(GLG, 1), :] += _colsum(dgvn * vh)
        dgvg = _ln_bwd(dgvn * glg, vh, vrstd)
        dz_ref[:, pl.ds(3 * DC, DC)] = (dgvg * dgv_ref[...]).astype(BF16)

        dh1 = jnp.zeros((tm, D), F32)
        for d in range(NDEV):
            dh1 = dh1 + _dot_nt(dz_ref[:, pl.ds(d * nc, nc)], wi_ref[d])
        xn, r1 = _rms(x_ref[...])
        g1 = _row(v_ref, G1)
        sc = 1.0 + _row(v_ref, SC1)
        acc_ref[pl.ds(1, 1), :] += _colsum(dh1)
        acc_ref[pl.ds(2, 1), :] += _colsum(dh1 * (xn * g1))
        acc_ref[pl.ds(3, 1), :] += _colsum(dh1 * sc * xn)
        gx_ref[...] = dx2v + _rms_bwd(dh1 * sc * g1, xn, r1)

        @pl.when(i == nt - 1)
        def _():
            low = _causal_mask(True)
            for h in range(NH):
                dws_ref[h] = jnp.where(low, dws_ref[h], 0.0)
            lane = lax.broadcasted_iota(jnp.int32, (CHUNK, CHUNK), 1)
            out = jnp.zeros((CHUNK, CHUNK), F32)
            for h in range(NH):
                hs = jnp.sum(dbs_s[:, pl.ds((h // 2) * CHUNK, CHUNK)]
                             * ((lane >= (h % 2) * HD) & (lane < (h % 2 + 1) * HD)).astype(F32),
                             axis=1, keepdims=True)
                out = jnp.where(lane == h, hs, out)
            dbst_ref[...] = out

    tile = lambda w: pl.BlockSpec((tm, w), lambda i: (nt - 1 - i, 0))
    return pl.pallas_call(
        body, name="bwd_mid", grid=(nt,),
        in_specs=[pl.BlockSpec((None, tm, D), functools.partial(lambda k, i: (k, nt - 1 - i, 0), k)) for k in range(4)]
        + [tile(D), tile(D), tile(D), tile(D), tile(2 * DC), tile(DC),
                  tile(DC), tile(DC), tile(DC), tile(DC), tile(DC), tile(DC), tile(2 * LANES), _full((VROWS, D)), _full((SUB, DC)), _full((CW_ROWS, DC)),
                  _full((NH, CHUNK, CHUNK)), _full((NH, CHUNK, CHUNK)), _full((D, D)), _full((NDEV, D, nc))],
        out_specs=[tile(D), tile(4 * DC), pl.BlockSpec((D, tm), lambda i: (0, nt - 1 - i)), tile(D),
                   _full((VROWS, D)), _full((SUB, DC)), _full((CW_ROWS, DC)),
                   _full((NH, CHUNK, CHUNK)), _full((CHUNK, CHUNK))],
        out_shape=[jax.ShapeDtypeStruct((s, D), F32), jax.ShapeDtypeStruct((s, 4 * DC), BF16),
                   jax.ShapeDtypeStruct((D, s), BF16), jax.ShapeDtypeStruct((s, D), BF16),
                   jax.ShapeDtypeStruct((VROWS, D), F32), jax.ShapeDtypeStruct((SUB, DC), F32),
                   jax.ShapeDtypeStruct((CW_ROWS, DC), F32), jax.ShapeDtypeStruct((NH, CHUNK, CHUNK), F32),
                   jax.ShapeDtypeStruct((CHUNK, CHUNK), F32)],
        scratch_shapes=[pltpu.VMEM((CHUNK, DC), F32), pltpu.VMEM((HALO_C, DC), F32), pltpu.VMEM((tm, DC), F32),
                        pltpu.VMEM((tm, DC), F32), pltpu.VMEM((tm, DC), F32)],
        compiler_params=_arb(),
    )(dh2, dh2, dh2, dh2, dx3, x2, x2d, o1, z, a0, xh_a, sp, gu, dgu, vh, dgv, st, vecs, v512, conv_w, gm_ws, gm_ws_t, w_out_b, w_in_g)


def _mm_all_slots(name, at, b, bw, tk, after):
    k1, s = at.shape
    nslot = b.shape[1] // bw

    def body(a_ref, b_ref, after_ref, o_ref):
        @pl.when(pl.program_id(0) == 0)
        def _():
            o_ref[...] = jnp.zeros_like(o_ref)

        t = _dot(a_ref[...], b_ref[...])
        for j in range(nslot):
            o_ref[j] += t[:, j * bw:(j + 1) * bw]

    return pl.pallas_call(
        body, name=name, grid=(s // tk,),
        in_specs=[pl.BlockSpec((k1, tk), lambda k: (0, k)), pl.BlockSpec((tk, nslot * bw), lambda k: (k, 0)), ANY],
        out_specs=_full((nslot, k1, bw)), out_shape=jax.ShapeDtypeStruct((nslot, k1, bw), F32),
        compiler_params=_arb(),
    )(at, b, after)


def _adam_math(w, g, m, v):
    m = ADAM_B1 * m + (1.0 - ADAM_B1) * g
    v = ADAM_B2 * v + (1.0 - ADAM_B2) * (g * g)
    m_hat = m / (1.0 - ADAM_B1 ** ADAM_STEP)
    v_hat = v / (1.0 - ADAM_B2 ** ADAM_STEP)
    delta = -ADAM_LR * (m_hat / (jnp.sqrt(v_hat) + ADAM_EPS) + ADAM_WD * w)
    return delta, m, v


def _row_block(rows, cols):
    tr = rows
    while tr * cols * 4 > ROW_BLOCK_BYTES and tr % (4 * SUB) == 0:
        tr //= 2
    return tr


def _adam3(name, w, g, m, v):
    _, rows, cols = w.shape
    tr = _row_block(rows, cols)

    def body(w_ref, g_ref, m_ref, v_ref, d_ref, mo_ref, vo_ref):
        d_ref[...], mo_ref[...], vo_ref[...] = _adam_math(w_ref[...], g_ref[...], m_ref[...], v_ref[...])

    spec = pl.BlockSpec((1, tr, cols), lambda i: (0, i, 0))
    return pl.pallas_call(
        body, name=name, grid=(rows // tr,), in_specs=[spec] * 4, out_specs=[spec] * 3,
        out_shape=[jax.ShapeDtypeStruct(w.shape, F32)] * 3, compiler_params=_arb(),
    )(w, g, m, v)


def _sum_adam(name, parts, w, m, v):
    n, rows, cols = parts.shape
    tr = _row_block(rows, cols)

    def body(p_ref, w_ref, m_ref, v_ref, g_ref, d_ref, mo_ref, vo_ref):
        g = p_ref[0].astype(F32)
        for k in range(1, n):
            g = g + p_ref[k].astype(F32)
        g_ref[0] = g
        d_ref[0], mo_ref[0], vo_ref[0] = _adam_math(w_ref[0], g, m_ref[0], v_ref[0])

    spec = pl.BlockSpec((1, tr, cols), lambda i: (0, i, 0))
    return pl.pallas_call(
        body, name=name, grid=(rows // tr,),
        in_specs=[pl.BlockSpec((n, tr, cols), lambda i: (0, i, 0))] + [spec] * 3, out_specs=[spec] * 4,
        out_shape=[jax.ShapeDtypeStruct(w.shape, F32)] * 4, compiler_params=_arb(),
    )(parts, w, m, v)


def _other_half(name, g4, other, transpose=False):
    _, _, rows, cols = g4.shape
    tr = rows if transpose else _row_block(rows, cols)
    oshape = (cols, rows) if transpose else (tr, cols)

    def body(c_ref, a_ref, o_ref):
        a = a_ref[...]
        o_ref[...] = (a.T if transpose else a).astype(BF16)

    return pl.pallas_call(
        body, name=name,
        grid_spec=pltpu.PrefetchScalarGridSpec(
            num_scalar_prefetch=1, grid=(4, rows // tr),
            in_specs=[pl.BlockSpec((None, None, tr, cols), lambda k, i, c_ref: (k, c_ref[0], i, 0))],
            out_specs=pl.BlockSpec((None,) + oshape, lambda k, i, c_ref: (k, i, 0))),
        out_shape=jax.ShapeDtypeStruct((4, cols, rows) if transpose else (4, rows, cols), BF16),
        compiler_params=_arb(2),
    )(other, g4)


def _pair_add(name, g4, recv, core, transpose=False):
    _, _, rows, cols = g4.shape
    tr = rows if transpose else _row_block(rows, cols)
    oshape = (cols, rows) if transpose else (tr, cols)

    def body(c_ref, a_ref, b_ref, o_ref):
        a = a_ref[...]
        o_ref[...] = ((a.T if transpose else a) + b_ref[...].astype(F32)).astype(BF16)

    return pl.pallas_call(
        body, name=name,
        grid_spec=pltpu.PrefetchScalarGridSpec(
            num_scalar_prefetch=1, grid=(4, rows // tr),
            in_specs=[pl.BlockSpec((None, None, tr, cols), lambda k, i, c_ref: (k, c_ref[0], i, 0)),
                      pl.BlockSpec((None,) + oshape, lambda k, i, c_ref: (k, i, 0))],
            out_specs=pl.BlockSpec((None,) + oshape, lambda k, i, c_ref: (k, i, 0))),
        out_shape=jax.ShapeDtypeStruct(recv.shape, BF16), compiler_params=_arb(2),
    )(core, g4, recv)


def _sum_small(rows_all, p_all, ws_all, bst_all, fw_all, cw_all):
    def body(a_ref, p_ref, ws_ref, bst_ref, fw_ref, cw_ref,
             g_b_ada, g_n1, g_mog, g_n2, g_gf, loss_cols, g_cb, g_clg, g_clb, g_glg, g_glb, g_ws, g_bs, fw_sum,
             cw_sum):
        def total(ref):
            t = ref[0]
            for k in range(1, NDEV):
                t = t + ref[k]
            return t

        a = total(a_ref)
        g_b_ada[...] = jnp.concatenate([a[k:k + 1, :] for k in range(6)], axis=1)
        g_n1[...] = a[6:7, :]
        g_mog[...] = a[7:8, :]
        g_n2[...] = a[8:9, :]
        g_gf[...] = a[9:10, :].reshape(D)
        loss_cols[...] = a[10:11, :]
        p = total(p_ref)
        for k, ref in zip((CB, CLG, CLB, GLG, GLB), (g_cb, g_clg, g_clb, g_glg, g_glb)):
            ref[...] = p[k:k + 1, :]
        g_ws[0] = total(ws_ref)
        g_bs[0] = jnp.transpose(total(bst_ref))[:NH, :]
        fw_sum[...] = total(fw_ref)
        cw_sum[...] = total(cw_ref)

    vec = lambda n: jax.ShapeDtypeStruct((1, n), F32)
    return pl.pallas_call(
        body, name="sum_small_grads",
        out_shape=[vec(6 * D), vec(D), vec(D), vec(D), jax.ShapeDtypeStruct((D,), F32), vec(D),
                   vec(DC), vec(DC), vec(DC), vec(DC), vec(DC),
                   jax.ShapeDtypeStruct((1, NH, CHUNK, CHUNK), F32), jax.ShapeDtypeStruct((1, NH, CHUNK), F32),
                   jax.ShapeDtypeStruct((SUB, 2 * PFF), F32), jax.ShapeDtypeStruct((CW_ROWS, DC), F32)],
    )(rows_all, p_all, ws_all, bst_all, fw_all, cw_all)


def _adam_small(quads):
    n = len(quads)

    def body(*refs):
        ins, outs = refs[:4 * n], refs[4 * n:]
        for q in range(n):
            w, g, m, v = (r[...] for r in ins[4 * q:4 * q + 4])
            outs[3 * q][...], outs[3 * q + 1][...], outs[3 * q + 2][...] = _adam_math(w, g, m, v)

    flat = [a for q in quads for a in q]
    outs = pl.pallas_call(
        body, name="adam_small",
        out_shape=[jax.ShapeDtypeStruct(q[0].shape, F32) for q in quads for _ in range(3)],
    )(*flat)
    return [tuple(outs[3 * q:3 * q + 3]) for q in range(n)]


def kernel(x, c, w_ada, b_ada, norm1_gain, w_in, conv_dw_w, conv_dw_b, conv_ln_g, conv_ln_b, gm_ln_g, gm_ln_b, gm_ws, gm_bs, mix_out_gain, w_out, norm2_gain, w_up, ffn_dw_w, ffn_dw_b, w_down, final_gain, loss_target, m_w_ada, m_b_ada, m_norm1_gain, m_w_in, m_conv_dw_w, m_conv_dw_b, m_conv_ln_g, m_conv_ln_b, m_gm_ln_g, m_gm_ln_b, m_gm_ws, m_gm_bs, m_mix_out_gain, m_w_out, m_norm2_gain, m_w_up, m_ffn_dw_w, m_ffn_dw_b, m_w_down, m_final_gain, v_w_ada, v_b_ada, v_norm1_gain, v_w_in, v_conv_dw_w, v_conv_dw_b, v_conv_ln_g, v_conv_ln_b, v_gm_ln_g, v_gm_ln_b, v_gm_ws, v_gm_bs, v_mix_out_gain, v_w_out, v_norm2_gain, v_w_up, v_ffn_dw_w, v_ffn_dw_b, v_w_down, v_final_gain):
    s = x.shape[1]
    ax, ay, ac = _place()
    me = 4 * ax + 2 * ay + ac
    n_ada = w_ada.shape[2]
    n_cw = conv_dw_w.shape[2]
    x2d = x[0]
    target = loss_target[0]
    pad_sh = lambda a: jnp.pad(a, [(0, 0)] * (a.ndim - 1) + [(0, PSH - NSH)])

    c_all, cw_all, fw_all = _all_gather("gather_small", [c, conv_dw_w[0], ffn_dw_w[0]])

    first_shards, c_all = lax.optimization_barrier(((w_in[0].astype(BF16), w_out[0].astype(BF16)), c_all))
    gather_in, token_a = _start_gather("gather_in_out", list(first_shards), me)
    c_all = c_all + token_a[0, 0]
    conv_w = jnp.pad(jnp.transpose(cw_all, (1, 0, 2)).reshape(KC, DC), ((0, CW_ROWS - KC), (0, 0)))
    ffn_w = jnp.transpose(pad_sh(fw_all), (1, 0, 2)).reshape(KF, 2 * PFF)
    ffn_b = pad_sh(ffn_dw_b.reshape(NDEV, NSH)).reshape(1, 2 * PFF)
    ffn_wb = jnp.concatenate([ffn_w, ffn_b, jnp.zeros((SUB - KF - 1, 2 * PFF), F32)], axis=0)

    b_cols = lax.dynamic_slice(b_ada, (0, me * n_ada), (1, n_ada))
    (mod_all,) = _all_gather("gather_mod", [_mod_part(c_all, w_ada, b_cols)])
    up_t = lambda a: jnp.swapaxes(a, 1, 2)
    w_up_shard = jnp.pad(up_t(w_up)[0].astype(BF16), ((0, PSH - NSH), (0, 0)))
    shards, mod_all = lax.optimization_barrier(((w_up_shard, w_down[0].astype(BF16)), mod_all))
    gather_ffn, token_c = _start_gather("gather_up_down", list(shards), me)
    mod = lax.dynamic_index_in_dim(mod_all, me, axis=1, keepdims=False).reshape(6, D)
    sh1, sc1, gt1, sh2, sc2, gt2 = [mod[k:k + 1] for k in range(6)]
    vecs = jnp.concatenate([norm1_gain, sh1, sc1, gt1, norm2_gain, sh2, sc2, gt2, mix_out_gain,
                            final_gain.reshape(1, D), jnp.zeros((6, D), F32)], axis=0)
    vecs = vecs + token_c[0, 0]
    v512 = jnp.concatenate([conv_dw_b, conv_ln_g, conv_ln_b, gm_ln_g, gm_ln_b, jnp.zeros((3, DC), F32)], axis=0)
    bs_exp = jnp.repeat(jnp.transpose(gm_bs[0]), HD, axis=1)
    gm_ws_t = jnp.swapaxes(gm_ws[0], 1, 2)

    tm_big, tm = min(TILE_BIG, s), min(TILE, s)
    w_in_g, w_out_g = _finish_gather("gather_in_out", gather_in, vecs)
    w_out_b = w_out_g.reshape(D, D)
    z, a0, h1_t = _fwd_in(x2d, vecs, w_in_g, tm_big)
    xh_a, sp, x2, o1, h2, gu, dgu, vh, dgv, ln_st, h2_t = _fwd_mid(
        a0, z, x2d, vecs, v512, conv_w, gm_ws, bs_exp, w_out_b, tm_big)
    w_up_t, w_down_g = _finish_gather("gather_up_down", gather_ffn, h2)
    w_down_p = jnp.pad(w_down_g.reshape(4, NSH, D), ((0, 0), (0, PSH - NSH), (0, 0)))
    up_pre, vg, dx3, acc_f = _fwd_ffn(h2, x2, target, vecs, ffn_wb, w_up_t, w_down_p, tm)

    core = ac.reshape(1).astype(jnp.int32)
    mychip = 2 * ax + ay

    other = 1 - core

    def to_pairs(named):
        g4s = [g.reshape((4, 2) + g.shape[1:]) for _, g in named]
        turn = [nm == "w_up" for nm, _ in named]
        halves = [_other_half("rs_other_half_" + t[0], g4, other, tr) for t, g4, tr in zip(named, g4s, turn)]
        from_sibling = _sibling_swap("rs_sibling_" + named[0][0], halves)
        return [_pair_add("rs_pair_add_" + t[0], g4, rv, core, tr)
                for t, g4, rv, tr in zip(named, g4s, from_sibling, turn)]

    dh2, dw_up, dw_down, acc_fw = _bwd_ffn(dx3, up_pre, vg, h2_t, vecs, ffn_wb, w_up_t, w_down_p, tm_big)
    acc_fw = jnp.transpose(acc_fw, (2, 0, 1, 3)).reshape(SUB, 2 * PFF)
    exchange_ffn, token_x = _start_exchange("rs_chips_ffn", to_pairs(
        [("w_up", dw_up), ("w_down", dw_down.reshape(NDEV, w_down.shape[1], D))]), mychip)
    gx, dz, y_t, do1, acc_m, acc_p, dcw, dws, dbs_t = _bwd_mid(
        dh2, dx3, x2, x2d, o1, z, a0, xh_a, sp, gu, dgu, vh, dgv, ln_st, vecs + token_x[0, 0], v512, conv_w, gm_ws[0],
        gm_ws_t, w_out_b, w_in_g, tm)
    rows = jnp.concatenate([acc_m[1:3], acc_m[0:1], acc_m[5:7], acc_f[2:3], acc_m[3:5], acc_m[7:8], acc_f[0:2],
                            jnp.zeros((5, D), F32)], axis=0)
    small_gather, token_s = _start_gather("gather_small_grads", [rows, acc_p, dws, dbs_t, acc_fw, dcw], me)
    dw_in = _mm_all_slots("dw_in", h1_t, dz, w_in.shape[2], min(TK_IN, s), token_s)
    small_pass, token_p = _pass_gather("gather_small_grads", small_gather, dw_in)
    dw_out = _mm_all_slots("dw_out", y_t, do1, D, min(TK_OUT, s), token_p).reshape(NDEV, w_out.shape[1], D)
    exchange_mix, token_m = _start_exchange("rs_chips_mix", to_pairs([("w_in", dw_in), ("w_out", dw_out)]), mychip)

    rows_all, p_all, ws_all, bst_all, fwg_all, cwg_all = _end_gather("gather_small_grads", small_pass, token_m)
    (g_b_ada, g_n1, g_mog, g_n2, g_gf, loss_cols, g_cb, g_clg, g_clb, g_glg, g_glb, g_ws, g_bs, fw_sum,
     cw_sum) = _sum_small(rows_all, p_all, ws_all, bst_all, fwg_all, cwg_all)
    loss = jnp.sum(loss_cols)
    g_fb = fw_sum[3].reshape(NDEV, PSH)[:, :NSH].reshape(ffn_dw_b.shape)
    g_fw = lax.dynamic_index_in_dim(fw_sum[:KF].reshape(KF, NDEV, PSH), me, axis=1, keepdims=False)[:, :NSH]
    g_fw = g_fw.reshape(ffn_dw_w.shape)
    g_cw = lax.dynamic_slice(cw_sum, (0, me * n_cw), (KC, n_cw)).reshape(conv_dw_w.shape)
    small = [
        (b_ada, g_b_ada, m_b_ada, v_b_ada), (norm1_gain, g_n1, m_norm1_gain, v_norm1_gain),
        (conv_dw_w, g_cw, m_conv_dw_w, v_conv_dw_w), (conv_dw_b, g_cb, m_conv_dw_b, v_conv_dw_b),
        (conv_ln_g, g_clg, m_conv_ln_g, v_conv_ln_g), (conv_ln_b, g_clb, m_conv_ln_b, v_conv_ln_b),
        (gm_ln_g, g_glg, m_gm_ln_g, v_gm_ln_g), (gm_ln_b, g_glb, m_gm_ln_b, v_gm_ln_b),
        (gm_ws, g_ws, m_gm_ws, v_gm_ws), (gm_bs, g_bs, m_gm_bs, v_gm_bs),
        (mix_out_gain, g_mog, m_mix_out_gain, v_mix_out_gain), (norm2_gain, g_n2, m_norm2_gain, v_norm2_gain),
        (ffn_dw_w, g_fw, m_ffn_dw_w, v_ffn_dw_w), (ffn_dw_b, g_fb, m_ffn_dw_b, v_ffn_dw_b),
        (final_gain, g_gf, m_final_gain, v_final_gain)]
    small_out = _adam_small(small)
    res = {}
    for name, q, o in zip(("b_ada", "norm1_gain", "conv_dw_w", "conv_dw_b", "conv_ln_g", "conv_ln_b", "gm_ln_g",
                           "gm_ln_b", "gm_ws", "gm_bs", "mix_out_gain", "norm2_gain", "ffn_dw_w", "ffn_dw_b",
                           "final_gain"), small, small_out):
        res[name] = (q[1],) + o

    dmod_all = rows_all[:, :6].reshape(NDEV, 6 * D)
    dm_cols = lax.dynamic_slice(dmod_all, (0, me * n_ada), (NDEV, n_ada))
    g_ada = _ada_grad(jnp.transpose(c_all[:, 0, :]), dm_cols)
    res["w_ada"] = (g_ada,) + tuple(_adam3("adam_ada", w_ada, g_ada, m_w_ada, v_w_ada))

    big = [("w_up", up_t(w_up), up_t(m_w_up), up_t(v_w_up)), ("w_down", w_down, m_w_down, v_w_down),
           ("w_in", w_in, m_w_in, v_w_in), ("w_out", w_out, m_w_out, v_w_out)]
    from_chips = list(_finish_exchange("rs_chips_ffn", exchange_ffn, res["w_ada"][1]))
    for t, parts in zip(big[:2], from_chips):
        res[t[0]] = tuple(_sum_adam("rs_sum_adam_" + t[0], parts, t[1], t[2], t[3]))
    res["w_up"] = tuple(up_t(a) for a in res["w_up"])
    from_chips = list(_finish_exchange("rs_chips_mix", exchange_mix, res["w_down"][1]))
    for t, parts in zip(big[2:], from_chips):
        res[t[0]] = tuple(_sum_adam("rs_sum_adam_" + t[0], parts, t[1], t[2], t[3]))

    order = ("w_ada", "b_ada", "norm1_gain", "w_in", "conv_dw_w", "conv_dw_b", "conv_ln_g", "conv_ln_b", "gm_ln_g",
             "gm_ln_b", "gm_ws", "gm_bs", "mix_out_gain", "w_out", "norm2_gain", "w_up", "ffn_dw_w", "ffn_dw_b",
             "w_down", "final_gain")
    return (loss, gx.reshape(x.shape), *[res[n][0] for n in order], *[res[n][1] for n in order],
            *[res[n][2] for n in order], *[res[n][3] for n in order])
```

```python
import functools

import jax
import jax.numpy as jnp
from jax import lax
from jax.experimental import pallas as pl
from jax.experimental.pallas import tpu as pltpu

F32 = jnp.float32
BF16 = jnp.bfloat16
NDEV = 8
D = 1024
DC = 512
DFF = 2816
NSH = 704
PSH = 768
PFF = 4 * PSH
KC = 31
KF = 3
CHUNK = 128
NH = 8
HD = 64
HALO_C = 32
HALO_F = 8
LANES = 128
SUB = 8
VROWS = 16
CW_ROWS = 32
TILE_BIG = 512
ROW_BLOCK_BYTES = 2 << 20
TILE = 256
DW_OUT_ROWS = 256
RMS_EPS = 1e-6
LN_EPS = 1e-5
ADAM_LR = 0.001
ADAM_B1 = 0.9
ADAM_B2 = 0.999
ADAM_EPS = 1e-08
ADAM_WD = 0.01
ADAM_STEP = 10
GELU_K = 0.7978845608028654
GELU_C = 0.044715

MESH = pl.DeviceIdType.MESH
ANY = pl.BlockSpec(memory_space=pl.ANY)

G1, SH1, SC1, GT1, G2, SH2, SC2, GT2, MOG, GF = range(10)
CB, CLG, CLB, GLG, GLB = range(5)


def _full(shape):
    return pl.BlockSpec(shape, lambda *_: (0,) * len(shape))


def _arb(n=1):
    return pltpu.CompilerParams(dimension_semantics=("arbitrary",) * n)


def _row(ref, r):
    return ref[pl.ds(r, 1), :]


def _colsum(v):
    return jnp.sum(v, axis=0, keepdims=True)


def _rowmean(v):
    return jnp.mean(v, axis=-1, keepdims=True)


def _rms(x):
    r = lax.rsqrt(_rowmean(x * x) + RMS_EPS)
    return x * r, r


def _rms_bwd(dxn, xn, r):
    return r * (dxn - xn * _rowmean(dxn * xn))


def _ln(x):
    mu = _rowmean(x)
    xc = x - mu
    rstd = lax.rsqrt(_rowmean(xc * xc) + LN_EPS)
    return xc * rstd, rstd


def _ln_bwd(dxh, xhat, rstd):
    return rstd * (dxh - _rowmean(dxh) - xhat * _rowmean(dxh * xhat))


def _sigmoid(x):
    return 0.5 * jnp.tanh(0.5 * x) + 0.5


def _gelu(x):
    t = jnp.tanh(GELU_K * (x + GELU_C * x * x * x))
    return 0.5 * x * (1.0 + t), t


def _gelu_grad(x, t):
    return 0.5 * (1.0 + t) + 0.5 * x * (1.0 - t * t) * (GELU_K * (1.0 + 3.0 * GELU_C * x * x))


def _dot(a, b):
    return jnp.dot(a, b, preferred_element_type=F32)


def _dot_nt(a, b):
    return lax.dot_general(a, b, (((1,), (1,)), ((), ())), preferred_element_type=F32)


def _shift_up(e, s):
    n = e.shape[0]
    return pltpu.roll(e, (n - s) % n, 0)


def _place():
    return lax.axis_index("x"), lax.axis_index("y"), lax.axis_index("c")


def _all_gather(name, xs):
    n = len(xs)

    def body(*refs):
        x_refs, out_refs = refs[:n], refs[n:2 * n]
        send_sems, recv_sems, local_sems = refs[2 * n:]
        x, y, c = _place()
        me, sibling = (x, y, c), (x, y, 1 - c)
        chips = [(1 - x, y), (x, 1 - y), (1 - x, 1 - y)]

        def copy(a, k, block, to, own=False):
            px, py, pc = block
            slot = out_refs[a].at[4 * px + 2 * py + pc]
            return pltpu.make_async_remote_copy(
                src_ref=x_refs[a] if own else slot, dst_ref=slot,
                send_sem=send_sems.at[7 * a + k], recv_sem=recv_sems.at[7 * a + k], device_id=to, device_id_type=MESH)

        mine = [pltpu.make_async_copy(x_refs[a], out_refs[a].at[4 * x + 2 * y + c], local_sems.at[a]) for a in range(n)]
        for cp in mine:
            cp.start()
        first = []
        for a in range(n):
            first.append(copy(a, 0, me, sibling, own=True))
            first += [copy(a, 1 + j, me, (*chip, c), own=True) for j, chip in enumerate(chips)]
        for cp in first:
            cp.start()
        passed = []
        for j, chip in enumerate(chips):
            for a in range(n):
                copy(a, 1 + j, (*chip, c), me).wait_recv()
                cp = copy(a, 4 + j, (*chip, c), sibling)
                cp.start()
                passed.append(cp)
        for a in range(n):
            copy(a, 0, sibling, me).wait_recv()
            for j, chip in enumerate(chips):
                copy(a, 4 + j, (*chip, 1 - c), me).wait_recv()
        for cp in first + passed:
            cp.wait_send()
        for cp in mine:
            cp.wait()

    return pl.pallas_call(
        body, name=name, out_shape=[jax.ShapeDtypeStruct((NDEV,) + a.shape, a.dtype) for a in xs],
        in_specs=[ANY] * n, out_specs=[ANY] * n,
        scratch_shapes=[pltpu.SemaphoreType.DMA((7 * n,)), pltpu.SemaphoreType.DMA((7 * n,)),
                        pltpu.SemaphoreType.DMA((n,))],
    )(*xs)


def _sibling_swap(name, hs):
    n = len(hs)

    def body(*refs):
        h_refs, out_refs = refs[:n], refs[n:2 * n]
        send_sems, recv_sems = refs[2 * n:]
        x, y, c = _place()
        cps = [pltpu.make_async_remote_copy(
            src_ref=h_refs[a].at[k], dst_ref=out_refs[a].at[k],
            send_sem=send_sems.at[4 * a + k], recv_sem=recv_sems.at[4 * a + k],
            device_id=(x, y, 1 - c), device_id_type=MESH) for a in range(n) for k in range(4)]
        for cp in cps:
            cp.start()
        for cp in cps:
            cp.wait()

    return pl.pallas_call(
        body, name=name, out_shape=[jax.ShapeDtypeStruct(h.shape, h.dtype) for h in hs],
        in_specs=[ANY] * n, out_specs=[ANY] * n,
        scratch_shapes=[pltpu.SemaphoreType.DMA((4 * n,)), pltpu.SemaphoreType.DMA((4 * n,))],
    )(*hs)


HBM = pl.BlockSpec(memory_space=pltpu.HBM)
SEM = pl.BlockSpec(memory_space=pltpu.SEMAPHORE)
EFFECT = pltpu.SideEffectType.DATAFLOW_SIDE_EFFECTING


def _in_hbm(a):
    return pltpu.with_memory_space_constraint(a, pltpu.HBM)


def _split_start(name, bufs, copies):
    n = len(bufs)

    def body(*refs):
        for cp in copies(refs[:n], refs[n], refs[n + 1]):
            cp.start()
        refs[-1][...] = jnp.zeros_like(refs[-1])

    out = pl.pallas_call(
        body, name=name,
        out_shape=(pltpu.SemaphoreType.DMA((copies.count,)), pltpu.SemaphoreType.DMA((copies.count,)),
                   *[pltpu.HBM(a.shape, a.dtype) for a in bufs], jax.ShapeDtypeStruct((SUB, LANES), F32)),
        in_specs=[HBM] * n, out_specs=(SEM, SEM, *[HBM] * n, pl.BlockSpec(memory_space=pltpu.VMEM)),
        input_output_aliases={i: 2 + i for i in range(n)},
        compiler_params=pltpu.CompilerParams(has_side_effects=EFFECT),
    )(*[_in_hbm(a) for a in bufs])
    return (out[0], out[1], list(out[2:2 + n])), out[-1]


def _split_wait(name, handle, copies, after):
    send_sems, recv_sems, bufs = handle
    n = len(bufs)

    def body(*refs):
        for cp in copies(refs[:n], refs[n], refs[n + 1]):
            cp.wait_send()
            cp.wait_recv()

    out = pl.pallas_call(
        body, name=name, out_shape=tuple(pltpu.HBM(a.shape, a.dtype) for a in bufs),
        in_specs=[HBM] * n + [SEM, SEM, pl.BlockSpec(memory_space=pl.ANY)], out_specs=tuple([HBM] * n),
        input_output_aliases={i: i for i in range(n)},
        compiler_params=pltpu.CompilerParams(has_side_effects=EFFECT),
    )(*bufs, send_sems, recv_sems, after)
    return list(out)


class _GatherFirstCopies:
    def __init__(self, n):
        self.n, self.count = n, 4 * n

    def __call__(self, refs, send_sems, recv_sems):
        x, y, c = _place()
        peers = [(x, y, 1 - c), (1 - x, y, c), (x, 1 - y, c), (1 - x, 1 - y, c)]
        return [pltpu.make_async_remote_copy(
            src_ref=refs[a], dst_ref=refs[self.n + a].at[4 * x + 2 * y + c],
            send_sem=send_sems.at[4 * a + k], recv_sem=recv_sems.at[4 * a + k], device_id=peer, device_id_type=MESH)
            for a in range(self.n) for k, peer in enumerate(peers)]


class _GatherPassCopies:
    def __init__(self, n):
        self.n, self.count = n, 3 * n

    def __call__(self, refs, send_sems, recv_sems):
        x, y, c = _place()
        cps = []
        for a in range(self.n):
            for j, (px, py) in enumerate([(1 - x, y), (x, 1 - y), (1 - x, 1 - y)]):
                slot = refs[a].at[4 * px + 2 * py + c]
                cps.append(pltpu.make_async_remote_copy(
                    src_ref=slot, dst_ref=slot, send_sem=send_sems.at[3 * a + j], recv_sem=recv_sems.at[3 * a + j],
                    device_id=(x, y, 1 - c), device_id_type=MESH))
        return cps


class _ExchangeCopies:
    def __init__(self, n):
        self.n, self.count = n, 3 * n

    def __call__(self, refs, send_sems, recv_sems):
        x, y, c = _place()
        cps = []
        for a in range(self.n):
            for j, (px, py) in enumerate([(1 - x, y), (x, 1 - y), (1 - x, 1 - y)]):
                cps.append(pltpu.make_async_remote_copy(
                    src_ref=refs[a].at[2 * px + py], dst_ref=refs[self.n + a].at[2 * x + y],
                    send_sem=send_sems.at[3 * a + j], recv_sem=recv_sems.at[3 * a + j],
                    device_id=(px, py, c), device_id_type=MESH))
        return cps


def _own_slot(nslot, src, index):
    land = lax.empty((nslot,) + src.shape, src.dtype)
    return lax.dynamic_update_slice(land, src[None], (index,) + (0,) * src.ndim)


def _start_gather(tag, xs, me):
    lands = [_own_slot(NDEV, a, me) for a in xs]
    return _split_start(tag + "_start", list(xs) + lands, _GatherFirstCopies(len(xs)))


def _pass_gather(tag, handle, after):
    n = len(handle[2]) // 2
    lands = _split_wait(tag + "_wait", handle, _GatherFirstCopies(n), after)[n:]
    return _split_start(tag + "_pass", lands, _GatherPassCopies(n))


def _end_gather(tag, passing, after):
    return _split_wait(tag + "_pass_wait", passing, _GatherPassCopies(len(passing[2])), after)


def _finish_gather(tag, handle, after):
    passing, token = _pass_gather(tag, handle, after)
    return _end_gather(tag, passing, token)


def _start_exchange(tag, hs, mychip):
    lands = [_own_slot(4, lax.dynamic_index_in_dim(h, mychip, 0, keepdims=False), mychip) for h in hs]
    return _split_start(tag + "_start", list(hs) + lands, _ExchangeCopies(len(hs)))


def _finish_exchange(tag, handle, after):
    n = len(handle[2]) // 2
    return _split_wait(tag + "_wait", handle, _ExchangeCopies(n), after)[n:]


def _mod_part(c_all, w_ada, b_cols):
    ncol = w_ada.shape[2]

    def body(c_ref, w_ref, b_ref, o_ref):
        cv = c_ref[:, 0, :]
        ca = cv * _sigmoid(cv)
        o_ref[...] = _dot(ca.astype(BF16), w_ref[0].astype(BF16)) + b_ref[...]

    return pl.pallas_call(body, name="mod_part", out_shape=jax.ShapeDtypeStruct((NDEV, ncol), F32))(
        c_all, w_ada, b_cols)


def _ada_grad(c_all_t, dmod_cols):
    ncol = dmod_cols.shape[1]

    def body(ct_ref, dm_ref, o_ref):
        ct = ct_ref[...]
        ca = ct * _sigmoid(ct)
        acc = jnp.zeros((D, ncol), F32)
        for b in range(NDEV):
            acc = acc + ca[:, b:b + 1] * dm_ref[pl.ds(b, 1), :]
        o_ref[0] = acc

    return pl.pallas_call(body, name="ada_grad", out_shape=jax.ShapeDtypeStruct((1, D, ncol), F32))(
        c_all_t, dmod_cols)


def _fwd_in(x2d, vecs, w_in_g, tm):
    s = x2d.shape[0]
    nc = w_in_g.shape[2]

    def body(x_ref, v_ref, w_ref, z_ref, a0_ref, h1t_ref):
        xn, _ = _rms(x_ref[...])
        h = (xn * _row(v_ref, G1)) * (1.0 + _row(v_ref, SC1)) + _row(v_ref, SH1)
        hb = h.astype(BF16)
        h1t_ref[...] = hb.T
        for d in range(NDEV):
            z_ref[:, pl.ds(d * nc, nc)] = _dot(hb, w_ref[d])
        a0_ref[...] = z_ref[:, :DC] * _sigmoid(z_ref[:, DC:2 * DC])

    return pl.pallas_call(
        body, name="fwd_in", grid=(s // tm,),
        in_specs=[pl.BlockSpec((tm, D), lambda i: (i, 0)), _full((VROWS, D)), _full((NDEV, D, nc))],
        out_specs=[pl.BlockSpec((tm, 4 * DC), lambda i: (i, 0)), pl.BlockSpec((tm, DC), lambda i: (i, 0)),
                   pl.BlockSpec((D, tm), lambda i: (0, i))],
        out_shape=[jax.ShapeDtypeStruct((s, 4 * DC), F32), jax.ShapeDtypeStruct((s, DC), F32),
                   jax.ShapeDtypeStruct((D, s), BF16)],
        compiler_params=_arb(),
    )(x2d, vecs, w_in_g)


def _causal_mask(lower):
    r = lax.broadcasted_iota(jnp.int32, (CHUNK, CHUNK), 0)
    c = lax.broadcasted_iota(jnp.int32, (CHUNK, CHUNK), 1)
    return (r >= c) if lower else (r <= c)


def _first_head_lanes():
    return lax.broadcasted_iota(jnp.int32, (CHUNK, CHUNK), 1) < HD


def _fwd_mid(a0, z, x2d, vecs, v512, conv_w, gm_ws, bs_exp, w_out_b, tm):
    s = x2d.shape[0]
    hb = tm // HALO_C

    def body(a0_ref, halo_ref, zg_ref, x_ref, v_ref, p_ref, cw_ref, ws_ref, bs_ref, wo_ref,
             xh_ref, sp_ref, x2_ref, o1_ref, h2_ref, gu_ref, dgu_ref, vh_ref, dgv_ref, st_ref, h2t_ref, a1_s):
        i = pl.program_id(0)
        for c0 in range(0, DC, LANES):
            cols = pl.ds(c0, LANES)
            halo = halo_ref[:, cols]
            e = jnp.concatenate([jnp.where(i > 0, halo, jnp.zeros_like(halo)), a0_ref[:, cols]], axis=0)
            acc = jnp.broadcast_to(p_ref[pl.ds(CB, 1), cols], (tm, LANES))
            for k in range(KC):
                acc = acc + _shift_up(e, HALO_C - (KC - 1) + k)[:tm, :] * cw_ref[pl.ds(k, 1), cols]
            a1_s[:, cols] = acc
        xh, rstd = _ln(a1_s[...])
        xh_ref[...] = xh
        a2 = xh * _row(p_ref, CLG) + _row(p_ref, CLB)
        a3 = a2 * _sigmoid(a2)
        gu_pre = zg_ref[:, :DC]
        gu, tu = _gelu(gu_pre)
        gu_ref[...] = gu
        dgu_ref[...] = _gelu_grad(gu_pre, tu)
        gv_pre = zg_ref[:, DC:]
        gvg, tv = _gelu(gv_pre)
        dgv_ref[...] = _gelu_grad(gv_pre, tv)
        vh, vrstd = _ln(gvg)
        vh_ref[...] = vh
        st_ref[...] = jnp.concatenate([jnp.broadcast_to(rstd, (tm, LANES)), jnp.broadcast_to(vrstd, (tm, LANES))], axis=1)
        gvn = (vh * _row(p_ref, GLG) + _row(p_ref, GLB)).astype(BF16)
        low = _causal_mask(True)
        first = _first_head_lanes()
        wm = [jnp.where(low, ws_ref[0, h], 0.0).astype(BF16) for h in range(NH)]
        for n in range(tm // CHUNK):
            for p in range(NH // 2):
                v = gvn[n * CHUNK:(n + 1) * CHUNK, p * CHUNK:(p + 1) * CHUNK]
                blk = jnp.where(first, _dot(wm[2 * p], v), _dot(wm[2 * p + 1], v))
                sp_ref[pl.ds(n * CHUNK, CHUNK), pl.ds(p * CHUNK, CHUNK)] = blk + bs_ref[:, pl.ds(p * CHUNK, CHUNK)]
        g = gu * sp_ref[...]
        an, _ = _rms(a3)
        gn, _ = _rms(g)
        mog = _row(v_ref, MOG)
        y = jnp.concatenate([an * mog[:, :DC], gn * mog[:, DC:]], axis=1).astype(BF16)
        o1 = _dot(y, wo_ref[...])
        o1_ref[...] = o1
        x2 = x_ref[...] + _row(v_ref, GT1) * o1
        x2_ref[...] = x2
        xn2, _ = _rms(x2)
        h2 = (xn2 * _row(v_ref, G2)) * (1.0 + _row(v_ref, SC2)) + _row(v_ref, SH2)
        h2b = h2.astype(BF16)
        h2_ref[...] = h2b
        h2t_ref[...] = h2b.T

    tile = lambda w: pl.BlockSpec((tm, w), lambda i: (i, 0))
    return pl.pallas_call(
        body, name="fwd_mid", grid=(s // tm,),
        in_specs=[tile(DC), pl.BlockSpec((HALO_C, DC), lambda i: (jnp.maximum(i * hb - 1, 0), 0)),
                  pl.BlockSpec((tm, 2 * DC), lambda i: (i, 1)), tile(D), _full((VROWS, D)), _full((SUB, DC)),
                  _full((CW_ROWS, DC)), _full((1, NH, CHUNK, CHUNK)), _full((CHUNK, DC)), _full((D, D))],
        out_specs=[tile(DC), tile(DC), tile(D), tile(D), tile(D), tile(DC), tile(DC), tile(DC), tile(DC), tile(2 * LANES),
                   pl.BlockSpec((D, tm), lambda i: (0, i))],
        out_shape=[jax.ShapeDtypeStruct((s, DC), F32), jax.ShapeDtypeStruct((s, DC), F32),
                   jax.ShapeDtypeStruct((s, D), F32), jax.ShapeDtypeStruct((s, D), F32),
                   jax.ShapeDtypeStruct((s, D), BF16)] + [jax.ShapeDtypeStruct((s, DC), F32)] * 4
        + [jax.ShapeDtypeStruct((s, 2 * LANES), F32), jax.ShapeDtypeStruct((D, s), BF16)],
        scratch_shapes=[pltpu.VMEM((tm, DC), F32)],
        compiler_params=_arb(),
    )(a0, a0, z, x2d, vecs, v512, conv_w, gm_ws, bs_exp, w_out_b)


def _ffn_conv(fw_ref, cols, p2, p1, pre):
    return (fw_ref[pl.ds(3, 1), cols] + fw_ref[pl.ds(0, 1), cols] * p2
            + fw_ref[pl.ds(1, 1), cols] * p1 + fw_ref[pl.ds(2, 1), cols] * pre)


def _fwd_ffn(h2, x2, target, vecs, ffn_wb, w_up_t, w_down_p, tm):
    s = x2.shape[0]

    def body(h2_ref, x2_ref, t_ref, v_ref, fw_ref, wu_hbm, wd_hbm,
             up_ref, vg_ref, dx3_ref, acc_ref, wu, wd, carry, stage):
        i = pl.program_id(0)

        @pl.when(i == 0)
        def _():
            for sh in range(NDEV):
                pltpu.sync_copy(wu_hbm.at[sh], stage)
                wu[sh] = stage[...].T
            pltpu.sync_copy(wd_hbm, wd)
            carry[...] = jnp.zeros_like(carry)
            acc_ref[...] = jnp.zeros_like(acc_ref)

        h2v = h2_ref[...]
        o2 = jnp.zeros((tm, D), F32)
        for j in range(4):
            conv = []
            for sh in (j, 4 + j):
                cols = pl.ds(sh * PSH, PSH)
                pre = _dot(h2v, wu[sh])
                up_ref[:, cols] = pre.astype(BF16)
                e = jnp.concatenate([carry[:, cols], pre], axis=0)
                carry[:, cols] = pre[tm - HALO_F:, :]
                conv.append(_ffn_conv(fw_ref, cols, pltpu.roll(e, 2, 0)[HALO_F:, :],
                                      pltpu.roll(e, 1, 0)[HALO_F:, :], pre))
            val, gate = conv
            vg_ref[:, pl.ds(j * PSH, PSH)] = val.astype(BF16)
            vg_ref[:, pl.ds((4 + j) * PSH, PSH)] = gate.astype(BF16)
            f = ((gate * _sigmoid(gate)) * val).astype(BF16)
            o2 = o2 + _dot(f, wd[j])
        x3 = x2_ref[...] + _row(v_ref, GT2) * o2
        xn3, r3 = _rms(x3)
        gf = _row(v_ref, GF)
        diff = xn3 * gf - t_ref[...]
        acc_ref[pl.ds(1, 1), :] += _colsum(diff * diff) * (0.5 / D)
        dout = diff * (1.0 / D)
        acc_ref[pl.ds(0, 1), :] += _colsum(dout * xn3)
        dx3 = _rms_bwd(dout * gf, xn3, r3)
        dx3_ref[...] = dx3
        acc_ref[pl.ds(2, 1), :] += _colsum(dx3 * o2)

    tile = lambda w: pl.BlockSpec((tm, w), lambda i: (i, 0))
    return pl.pallas_call(
        body, name="fwd_ffn", grid=(s // tm,),
        in_specs=[tile(D), tile(D), tile(D), _full((VROWS, D)), _full((SUB, 2 * PFF)), ANY, ANY],
        out_specs=[tile(2 * PFF), tile(2 * PFF), tile(D), _full((SUB, D))],
        out_shape=[jax.ShapeDtypeStruct((s, 2 * PFF), BF16), jax.ShapeDtypeStruct((s, 2 * PFF), BF16),
                   jax.ShapeDtypeStruct((s, D), F32), jax.ShapeDtypeStruct((SUB, D), F32)],
        scratch_shapes=[pltpu.VMEM((NDEV, D, PSH), BF16), pltpu.VMEM((4, PSH, D), BF16),
                        pltpu.VMEM((HALO_F, 2 * PFF), F32), pltpu.VMEM((PSH, D), BF16)],
        compiler_params=_arb(),
    )(h2, x2, target, vecs, ffn_wb, w_up_t, w_down_p)


def _bwd_ffn(dx3, up_pre, vg, h2_t, vecs, ffn_wb, w_up_t, w_down_p, tm):
    s = dx3.shape[0]
    nt = s // tm

    def body(dx3_ref, up_ref, upg_ref, val_ref, gate_ref, h2t_ref, v_ref, fw_ref, fwg_ref, wu_ref, wug_ref, wd_ref,
             dh2_ref, dwu_ref, dwd_ref, accf_ref, carry):
        i = pl.program_id(1)

        @pl.when(i == 0)
        def _():
            for ref in (carry, dwu_ref, dwd_ref, accf_ref):
                ref[...] = jnp.zeros_like(ref)

        do2 = (dx3_ref[...] * _row(v_ref, GT2)).astype(BF16)
        df = _dot_nt(do2, wd_ref[...])
        val = val_ref[...].astype(F32)
        gate = gate_ref[...].astype(F32)
        sg = _sigmoid(gate)
        sl = gate * sg
        f_t = (sl * val).astype(BF16).T
        dwd_ref[...] += _dot(f_t, do2)[:NSH, :]
        dups = (df * sl, df * val * (sg * (1.0 + gate * (1.0 - sg))))
        h2t = h2t_ref[...]
        dh2 = jnp.zeros((tm, D), F32)
        for half, (dup, pre_ref, w_ref, wmat_ref) in enumerate(
                zip(dups, (up_ref, upg_ref), (fw_ref, fwg_ref), (wu_ref, wug_ref))):
            cols = pl.ds(half * PSH, PSH)
            e = jnp.concatenate([dup, carry[:, cols]], axis=0)
            carry[:, cols] = dup[:HALO_F, :]
            d1 = _shift_up(e, 1)[:tm, :]
            d2 = _shift_up(e, 2)[:tm, :]
            pre = pre_ref[...].astype(F32)
            accf_ref[half, pl.ds(3, 1), :] += _colsum(dup)
            accf_ref[half, pl.ds(0, 1), :] += _colsum(d2 * pre)
            accf_ref[half, pl.ds(1, 1), :] += _colsum(d1 * pre)
            accf_ref[half, pl.ds(2, 1), :] += _colsum(dup * pre)
            dpre = (_row(w_ref, 0) * d2 + _row(w_ref, 1) * d1 + _row(w_ref, 2) * dup).astype(BF16)
            dwu_ref[half] += _dot(h2t, dpre)[:, :NSH]
            dh2 = dh2 + _dot(dpre, wmat_ref[...])
        dh2_ref[...] = dh2

    rev = lambda j, i: nt - 1 - i
    in_specs = [
        pl.BlockSpec((tm, D), lambda j, i: (rev(j, i), 0)),
        pl.BlockSpec((tm, PSH), lambda j, i: (rev(j, i), j)), pl.BlockSpec((tm, PSH), lambda j, i: (rev(j, i), 4 + j)),
        pl.BlockSpec((tm, PSH), lambda j, i: (rev(j, i), j)), pl.BlockSpec((tm, PSH), lambda j, i: (rev(j, i), 4 + j)),
        pl.BlockSpec((D, tm), lambda j, i: (0, rev(j, i))), _full((VROWS, D)),
        pl.BlockSpec((SUB, PSH), lambda j, i: (0, j)), pl.BlockSpec((SUB, PSH), lambda j, i: (0, 4 + j)),
        pl.BlockSpec((None, PSH, D), lambda j, i: (j, 0, 0)), pl.BlockSpec((None, PSH, D), lambda j, i: (4 + j, 0, 0)),
        pl.BlockSpec((None, PSH, D), lambda j, i: (j, 0, 0))]
    dh2, dw_up, dw_down, accf = pl.pallas_call(
        body, name="bwd_ffn", grid=(4, nt), in_specs=in_specs,
        out_specs=[pl.BlockSpec((None, tm, D), lambda j, i: (j, rev(j, i), 0)),
                   pl.BlockSpec((2, None, D, NSH), lambda j, i: (0, j, 0, 0)),
                   pl.BlockSpec((None, NSH, D), lambda j, i: (j, 0, 0)),
                   pl.BlockSpec((2, None, SUB, PSH), lambda j, i: (0, j, 0, 0))],
        out_shape=[jax.ShapeDtypeStruct((4, s, D), F32), jax.ShapeDtypeStruct((2, 4, D, NSH), F32),
                   jax.ShapeDtypeStruct((4, NSH, D), F32), jax.ShapeDtypeStruct((2, 4, SUB, PSH), F32)],
        scratch_shapes=[pltpu.VMEM((HALO_F, 2 * PSH), F32)],
        compiler_params=_arb(2),
    )(dx3, up_pre, up_pre, vg, vg, h2_t, vecs, ffn_wb, ffn_wb, w_up_t, w_up_t, w_down_p)
    return dh2, dw_up.reshape(NDEV, D, NSH), dw_down, accf


def _bwd_mid(dh2, dx3, x2, x2d, o1, z, a0, xh_a, sp, gu, dgu, vh, dgv, st, vecs, v512, conv_w, gm_ws, gm_ws_t, w_out_b,
             w_in_g, tm):
    s = x2d.shape[0]
    nt = s // tm
    nc = w_in_g.shape[2]

    def body(dh2a_ref, dh2b_ref, dh2c_ref, dh2d_ref, dx3_ref, x2_ref, x_ref, o1_ref, z_ref, a0_ref, xh_ref, sp_ref,
             gu_ref, dgu_ref, vh_ref, dgv_ref, st_ref, v_ref, p_ref, cw_ref, ws_ref, wst_ref, wo_ref, wi_ref, gx_ref, dz_ref, yt_ref, do1_ref, acc_ref, accp_ref,
             dcw_ref, dws_ref, dbst_ref, dbs_s, carry, da1_s, dsp_s, dgvn_s):
        i = pl.program_id(0)

        @pl.when(i == 0)
        def _():
            for ref in (carry, dbs_s, acc_ref, accp_ref, dcw_ref, dws_ref, dbst_ref):
                ref[...] = jnp.zeros_like(ref)

        dh2v = (dh2a_ref[...] + dh2b_ref[...]) + (dh2c_ref[...] + dh2d_ref[...])
        xn2, r2 = _rms(x2_ref[...])
        g2 = _row(v_ref, G2)
        sc2 = 1.0 + _row(v_ref, SC2)
        acc_ref[pl.ds(5, 1), :] += _colsum(dh2v)
        acc_ref[pl.ds(6, 1), :] += _colsum(dh2v * (xn2 * g2))
        acc_ref[pl.ds(7, 1), :] += _colsum(dh2v * sc2 * xn2)
        dx2v = dx3_ref[...] + _rms_bwd(dh2v * sc2 * g2, xn2, r2)
        do1 = (dx2v * _row(v_ref, GT1)).astype(BF16)
        do1_ref[...] = do1
        acc_ref[pl.ds(0, 1), :] += _colsum(dx2v * o1_ref[...])
        dy = _dot_nt(do1, wo_ref[...])
        mog = _row(v_ref, MOG)

        lane_tiles = DC // LANES
        xh = xh_ref[...]
        rstd = jnp.concatenate([st_ref[:, :LANES]] * lane_tiles, axis=1)
        clg = _row(p_ref, CLG)
        a2 = xh * clg + _row(p_ref, CLB)
        s2 = _sigmoid(a2)
        a3 = a2 * s2
        an, ra = _rms(a3)
        dya = dy[:, :DC]
        da3 = _rms_bwd(dya * mog[:, :DC], an, ra)
        da2 = da3 * (s2 * (1.0 + a2 * (1.0 - s2)))
        accp_ref[pl.ds(CLB, 1), :] += _colsum(da2)
        accp_ref[pl.ds(CLG, 1), :] += _colsum(da2 * xh)
        da1 = _ln_bwd(da2 * clg, xh, rstd)
        accp_ref[pl.ds(CB, 1), :] += _colsum(da1)
        da1_s[...] = da1
        for c0 in range(0, DC, LANES):
            cols = pl.ds(c0, LANES)
            d = da1_s[:, cols]
            e = jnp.concatenate([d, carry[:, cols]], axis=0)
            carry[:, cols] = d[:HALO_C, :]
            a0c = a0_ref[:, cols]
            acc = jnp.zeros((tm, LANES), F32)
            for j in range(KC):
                ahead = _shift_up(e, j)[:tm, :]
                acc = acc + ahead * cw_ref[pl.ds(KC - 1 - j, 1), cols]
                dcw_ref[pl.ds(KC - 1 - j, 1), cols] += _colsum(a0c * ahead)
            sgc = _sigmoid(z_ref[:, pl.ds(DC + c0, LANES)])
            dz_ref[:, cols] = (acc * sgc).astype(BF16)
            dz_ref[:, pl.ds(DC + c0, LANES)] = (acc * z_ref[:, cols] * sgc * (1.0 - sgc)).astype(BF16)

        gu = gu_ref[...]
        vh = vh_ref[...]
        vrstd = jnp.concatenate([st_ref[:, LANES:]] * lane_tiles, axis=1)
        glg = _row(p_ref, GLG)
        gvn = (vh * glg + _row(p_ref, GLB)).astype(BF16)
        spv = sp_ref[...]
        g = gu * spv
        gn, rg = _rms(g)
        yt_ref[...] = jnp.concatenate([an * mog[:, :DC], gn * mog[:, DC:]], axis=1).astype(BF16).T
        acc_ref[pl.ds(4, 1), :] += jnp.concatenate([_colsum(dya * an), _colsum(dy[:, DC:] * gn)], axis=1)
        dg = _rms_bwd(dy[:, DC:] * mog[:, DC:], gn, rg)
        dz_ref[:, pl.ds(2 * DC, DC)] = (dg * spv * dgu_ref[...]).astype(BF16)
        dsp_s[...] = dg * gu
        upper = _causal_mask(False)
        first = _first_head_lanes()
        wmt = [jnp.where(upper, wst_ref[h], 0.0).astype(BF16) for h in range(NH)]
        for n in range(tm // CHUNK):
            rows = pl.ds(n * CHUNK, CHUNK)
            for p in range(NH // 2):
                cols = pl.ds(p * CHUNK, CHUNK)
                dsp = dsp_s[rows, cols]
                dbs_s[:, cols] += dsp
                da = jnp.where(first, dsp, 0.0).astype(BF16)
                db = jnp.where(first, 0.0, dsp).astype(BF16)
                v = gvn[n * CHUNK:(n + 1) * CHUNK, p * CHUNK:(p + 1) * CHUNK]
                dws_ref[2 * p] += _dot_nt(da, v)
                dws_ref[2 * p + 1] += _dot_nt(db, v)
                dgvn_s[rows, cols] = _dot(wmt[2 * p], da) + _dot(wmt[2 * p + 1], db)
        dgvn = dgvn_s[...]
        accp_ref[pl.ds(GLB, 1), :] += _colsum(dgvn)
        accp_ref[pl.ds(GLG, 1), :] += _colsum(dgvn * vh)
        dgvg = _ln_bwd(dgvn * glg, vh, vrstd)
        dz_ref[:, pl.ds(3 * DC, DC)] = (dgvg * dgv_ref[...]).astype(BF16)

        dh1 = jnp.zeros((tm, D), F32)
        for d in range(NDEV):
            dh1 = dh1 + _dot_nt(dz_ref[:, pl.ds(d * nc, nc)], wi_ref[d])
        xn, r1 = _rms(x_ref[...])
        g1 = _row(v_ref, G1)
        sc = 1.0 + _row(v_ref, SC1)
        acc_ref[pl.ds(1, 1), :] += _colsum(dh1)
        acc_ref[pl.ds(2, 1), :] += _colsum(dh1 * (xn * g1))
        acc_ref[pl.ds(3, 1), :] += _colsum(dh1 * sc * xn)
        gx_ref[...] = dx2v + _rms_bwd(dh1 * sc * g1, xn, r1)

        @pl.when(i == nt - 1)
        def _():
            low = _causal_mask(True)
            for h in range(NH):
                dws_ref[h] = jnp.where(low, dws_ref[h], 0.0)
            lane = lax.broadcasted_iota(jnp.int32, (CHUNK, CHUNK), 1)
            out = jnp.zeros((CHUNK, CHUNK), F32)
            for h in range(NH):
                hs = jnp.sum(dbs_s[:, pl.ds((h // 2) * CHUNK, CHUNK)]
                             * ((lane >= (h % 2) * HD) & (lane < (h % 2 + 1) * HD)).astype(F32),
                             axis=1, keepdims=True)
                out = jnp.where(lane == h, hs, out)
            dbst_ref[...] = out

    tile = lambda w: pl.BlockSpec((tm, w), lambda i: (nt - 1 - i, 0))
    return pl.pallas_call(
        body, name="bwd_mid", grid=(nt,),
        in_specs=[pl.BlockSpec((None, tm, D), functools.partial(lambda k, i: (k, nt - 1 - i, 0), k)) for k in range(4)]
        + [tile(D), tile(D), tile(D), tile(D), tile(2 * DC), tile(DC),
                  tile(DC), tile(DC), tile(DC), tile(DC), tile(DC), tile(DC), tile(2 * LANES), _full((VROWS, D)), _full((SUB, DC)), _full((CW_ROWS, DC)),
                  _full((NH, CHUNK, CHUNK)), _full((NH, CHUNK, CHUNK)), _full((D, D)), _full((NDEV, D, nc))],
        out_specs=[tile(D), tile(4 * DC), pl.BlockSpec((D, tm), lambda i: (0, nt - 1 - i)), tile(D),
                   _full((VROWS, D)), _full((SUB, DC)), _full((CW_ROWS, DC)),
                   _full((NH, CHUNK, CHUNK)), _full((CHUNK, CHUNK))],
        out_shape=[jax.ShapeDtypeStruct((s, D), F32), jax.ShapeDtypeStruct((s, 4 * DC), BF16),
                   jax.ShapeDtypeStruct((D, s), BF16), jax.ShapeDtypeStruct((s, D), BF16),
                   jax.ShapeDtypeStruct((VROWS, D), F32), jax.ShapeDtypeStruct((SUB, DC), F32),
                   jax.ShapeDtypeStruct((CW_ROWS, DC), F32), jax.ShapeDtypeStruct((NH, CHUNK, CHUNK), F32),
                   jax.ShapeDtypeStruct((CHUNK, CHUNK), F32)],
        scratch_shapes=[pltpu.VMEM((CHUNK, DC), F32), pltpu.VMEM((HALO_C, DC), F32), pltpu.VMEM((tm, DC), F32),
                        pltpu.VMEM((tm, DC), F32), pltpu.VMEM((tm, DC), F32)],
        compiler_params=_arb(),
    )(dh2, dh2, dh2, dh2, dx3, x2, x2d, o1, z, a0, xh_a, sp, gu, dgu, vh, dgv, st, vecs, v512, conv_w, gm_ws, gm_ws_t, w_out_b, w_in_g)


def _mm_all_slots(name, at, b, bw, after):
    k1, s = at.shape
    nslot = b.shape[1] // bw

    def body(a_hbm, b_ref, after_ref, o_ref, a_s):
        @pl.when(pl.program_id(0) == 0)
        def _():
            pltpu.sync_copy(a_hbm, a_s)

        o_ref[...] = _dot(a_s[...], b_ref[...])

    return pl.pallas_call(
        body, name=name, grid=(nslot,),
        in_specs=[ANY, pl.BlockSpec((s, bw), lambda j: (0, j)), ANY],
        out_specs=pl.BlockSpec((None, k1, bw), lambda j: (j, 0, 0)),
        out_shape=jax.ShapeDtypeStruct((nslot, k1, bw), F32),
        scratch_shapes=[pltpu.VMEM((k1, s), BF16)],
        compiler_params=_arb(),
    )(at, b, after)


def _mm_row_blocks(name, at, b, bh, after):
    k1, s = at.shape
    k2 = b.shape[1]

    def body(a_ref, b_hbm, after_ref, o_ref, b_s):
        @pl.when(pl.program_id(0) == 0)
        def _():
            pltpu.sync_copy(b_hbm, b_s)

        o_ref[...] = _dot(a_ref[...], b_s[...])

    return pl.pallas_call(
        body, name=name, grid=(k1 // bh,),
        in_specs=[pl.BlockSpec((bh, s), lambda i: (i, 0)), ANY, ANY],
        out_specs=pl.BlockSpec((bh, k2), lambda i: (i, 0)),
        out_shape=jax.ShapeDtypeStruct((k1, k2), F32),
        scratch_shapes=[pltpu.VMEM((s, k2), BF16)],
        compiler_params=_arb(),
    )(at, b, after)


def _adam_math(w, g, m, v):
    m = ADAM_B1 * m + (1.0 - ADAM_B1) * g
    v = ADAM_B2 * v + (1.0 - ADAM_B2) * (g * g)
    m_hat = m / (1.0 - ADAM_B1 ** ADAM_STEP)
    v_hat = v / (1.0 - ADAM_B2 ** ADAM_STEP)
    delta = -ADAM_LR * (m_hat / (jnp.sqrt(v_hat) + ADAM_EPS) + ADAM_WD * w)
    return delta, m, v


def _row_block(rows, cols):
    tr = rows
    while tr * cols * 4 > ROW_BLOCK_BYTES and tr % (4 * SUB) == 0:
        tr //= 2
    return tr


def _adam3(name, w, g, m, v):
    _, rows, cols = w.shape
    tr = _row_block(rows, cols)

    def body(w_ref, g_ref, m_ref, v_ref, d_ref, mo_ref, vo_ref):
        d_ref[...], mo_ref[...], vo_ref[...] = _adam_math(w_ref[...], g_ref[...], m_ref[...], v_ref[...])

    spec = pl.BlockSpec((1, tr, cols), lambda i: (0, i, 0))
    return pl.pallas_call(
        body, name=name, grid=(rows // tr,), in_specs=[spec] * 4, out_specs=[spec] * 3,
        out_shape=[jax.ShapeDtypeStruct(w.shape, F32)] * 3, compiler_params=_arb(),
    )(w, g, m, v)


def _sum_adam(name, parts, w, m, v):
    n, rows, cols = parts.shape
    tr = _row_block(rows, cols)

    def body(p_ref, w_ref, m_ref, v_ref, g_ref, d_ref, mo_ref, vo_ref):
        g = p_ref[0].astype(F32)
        for k in range(1, n):
            g = g + p_ref[k].astype(F32)
        g_ref[0] = g
        d_ref[0], mo_ref[0], vo_ref[0] = _adam_math(w_ref[0], g, m_ref[0], v_ref[0])

    spec = pl.BlockSpec((1, tr, cols), lambda i: (0, i, 0))
    return pl.pallas_call(
        body, name=name, grid=(rows // tr,),
        in_specs=[pl.BlockSpec((n, tr, cols), lambda i: (0, i, 0))] + [spec] * 3, out_specs=[spec] * 4,
        out_shape=[jax.ShapeDtypeStruct(w.shape, F32)] * 4, compiler_params=_arb(),
    )(parts, w, m, v)


def _other_half(name, g4, other, transpose=False):
    _, _, rows, cols = g4.shape
    tr = rows if transpose else _row_block(rows, cols)
    oshape = (cols, rows) if transpose else (tr, cols)

    def body(c_ref, a_ref, o_ref):
        a = a_ref[...]
        o_ref[...] = (a.T if transpose else a).astype(BF16)

    return pl.pallas_call(
        body, name=name,
        grid_spec=pltpu.PrefetchScalarGridSpec(
            num_scalar_prefetch=1, grid=(4, rows // tr),
            in_specs=[pl.BlockSpec((None, None, tr, cols), lambda k, i, c_ref: (k, c_ref[0], i, 0))],
            out_specs=pl.BlockSpec((None,) + oshape, lambda k, i, c_ref: (k, i, 0))),
        out_shape=jax.ShapeDtypeStruct((4, cols, rows) if transpose else (4, rows, cols), BF16),
        compiler_params=_arb(2),
    )(other, g4)


def _pair_add(name, g4, recv, core, transpose=False):
    _, _, rows, cols = g4.shape
    tr = rows if transpose else _row_block(rows, cols)
    oshape = (cols, rows) if transpose else (tr, cols)

    def body(c_ref, a_ref, b_ref, o_ref):
        a = a_ref[...]
        o_ref[...] = ((a.T if transpose else a) + b_ref[...].astype(F32)).astype(BF16)

    return pl.pallas_call(
        body, name=name,
        grid_spec=pltpu.PrefetchScalarGridSpec(
            num_scalar_prefetch=1, grid=(4, rows // tr),
            in_specs=[pl.BlockSpec((None, None, tr, cols), lambda k, i, c_ref: (k, c_ref[0], i, 0)),
                      pl.BlockSpec((None,) + oshape, lambda k, i, c_ref: (k, i, 0))],
            out_specs=pl.BlockSpec((None,) + oshape, lambda k, i, c_ref: (k, i, 0))),
        out_shape=jax.ShapeDtypeStruct(recv.shape, BF16), compiler_params=_arb(2),
    )(core, g4, recv)


def _sum_small(rows_all, p_all, ws_all, bst_all, fw_all, cw_all):
    def body(a_ref, p_ref, ws_ref, bst_ref, fw_ref, cw_ref,
             g_b_ada, g_n1, g_mog, g_n2, g_gf, loss_cols, g_cb, g_clg, g_clb, g_glg, g_glb, g_ws, g_bs, fw_sum,
             cw_sum):
        def total(ref):
            t = ref[0]
            for k in range(1, NDEV):
                t = t + ref[k]
            return t

        a = total(a_ref)
        g_b_ada[...] = jnp.concatenate([a[k:k + 1, :] for k in range(6)], axis=1)
        g_n1[...] = a[6:7, :]
        g_mog[...] = a[7:8, :]
        g_n2[...] = a[8:9, :]
        g_gf[...] = a[9:10, :].reshape(D)
        loss_cols[...] = a[10:11, :]
        p = total(p_ref)
        for k, ref in zip((CB, CLG, CLB, GLG, GLB), (g_cb, g_clg, g_clb, g_glg, g_glb)):
            ref[...] = p[k:k + 1, :]
        g_ws[0] = total(ws_ref)
        g_bs[0] = jnp.transpose(total(bst_ref))[:NH, :]
        fw_sum[...] = total(fw_ref)
        cw_sum[...] = total(cw_ref)

    vec = lambda n: jax.ShapeDtypeStruct((1, n), F32)
    return pl.pallas_call(
        body, name="sum_small_grads",
        out_shape=[vec(6 * D), vec(D), vec(D), vec(D), jax.ShapeDtypeStruct((D,), F32), vec(D),
                   vec(DC), vec(DC), vec(DC), vec(DC), vec(DC),
                   jax.ShapeDtypeStruct((1, NH, CHUNK, CHUNK), F32), jax.ShapeDtypeStruct((1, NH, CHUNK), F32),
                   jax.ShapeDtypeStruct((SUB, 2 * PFF), F32), jax.ShapeDtypeStruct((CW_ROWS, DC), F32)],
    )(rows_all, p_all, ws_all, bst_all, fw_all, cw_all)


def _adam_small(quads):
    n = len(quads)

    def body(*refs):
        ins, outs = refs[:4 * n], refs[4 * n:]
        for q in range(n):
            w, g, m, v = (r[...] for r in ins[4 * q:4 * q + 4])
            outs[3 * q][...], outs[3 * q + 1][...], outs[3 * q + 2][...] = _adam_math(w, g, m, v)

    flat = [a for q in quads for a in q]
    outs = pl.pallas_call(
        body, name="adam_small",
        out_shape=[jax.ShapeDtypeStruct(q[0].shape, F32) for q in quads for _ in range(3)],
    )(*flat)
    return [tuple(outs[3 * q:3 * q + 3]) for q in range(n)]


def kernel(x, c, w_ada, b_ada, norm1_gain, w_in, conv_dw_w, conv_dw_b, conv_ln_g, conv_ln_b, gm_ln_g, gm_ln_b, gm_ws, gm_bs, mix_out_gain, w_out, norm2_gain, w_up, ffn_dw_w, ffn_dw_b, w_down, final_gain, loss_target, m_w_ada, m_b_ada, m_norm1_gain, m_w_in, m_conv_dw_w, m_conv_dw_b, m_conv_ln_g, m_conv_ln_b, m_gm_ln_g, m_gm_ln_b, m_gm_ws, m_gm_bs, m_mix_out_gain, m_w_out, m_norm2_gain, m_w_up, m_ffn_dw_w, m_ffn_dw_b, m_w_down, m_final_gain, v_w_ada, v_b_ada, v_norm1_gain, v_w_in, v_conv_dw_w, v_conv_dw_b, v_conv_ln_g, v_conv_ln_b, v_gm_ln_g, v_gm_ln_b, v_gm_ws, v_gm_bs, v_mix_out_gain, v_w_out, v_norm2_gain, v_w_up, v_ffn_dw_w, v_ffn_dw_b, v_w_down, v_final_gain):
    s = x.shape[1]
    ax, ay, ac = _place()
    me = 4 * ax + 2 * ay + ac
    n_ada = w_ada.shape[2]
    n_cw = conv_dw_w.shape[2]
    x2d = x[0]
    target = loss_target[0]
    pad_sh = lambda a: jnp.pad(a, [(0, 0)] * (a.ndim - 1) + [(0, PSH - NSH)])

    c_all, cw_all, fw_all = _all_gather("gather_small", [c, conv_dw_w[0], ffn_dw_w[0]])

    first_shards, c_all = lax.optimization_barrier(((w_in[0].astype(BF16), w_out[0].astype(BF16)), c_all))
    gather_in, token_a = _start_gather("gather_in_out", list(first_shards), me)
    c_all = c_all + token_a[0, 0]
    conv_w = jnp.pad(jnp.transpose(cw_all, (1, 0, 2)).reshape(KC, DC), ((0, CW_ROWS - KC), (0, 0)))
    ffn_w = jnp.transpose(pad_sh(fw_all), (1, 0, 2)).reshape(KF, 2 * PFF)
    ffn_b = pad_sh(ffn_dw_b.reshape(NDEV, NSH)).reshape(1, 2 * PFF)
    ffn_wb = jnp.concatenate([ffn_w, ffn_b, jnp.zeros((SUB - KF - 1, 2 * PFF), F32)], axis=0)

    b_cols = lax.dynamic_slice(b_ada, (0, me * n_ada), (1, n_ada))
    (mod_all,) = _all_gather("gather_mod", [_mod_part(c_all, w_ada, b_cols)])
    up_t = lambda a: jnp.swapaxes(a, 1, 2)
    w_up_shard = jnp.pad(up_t(w_up)[0].astype(BF16), ((0, PSH - NSH), (0, 0)))
    shards, mod_all = lax.optimization_barrier(((w_up_shard, w_down[0].astype(BF16)), mod_all))
    gather_ffn, token_c = _start_gather("gather_up_down", list(shards), me)
    mod = lax.dynamic_index_in_dim(mod_all, me, axis=1, keepdims=False).reshape(6, D)
    sh1, sc1, gt1, sh2, sc2, gt2 = [mod[k:k + 1] for k in range(6)]
    vecs = jnp.concatenate([norm1_gain, sh1, sc1, gt1, norm2_gain, sh2, sc2, gt2, mix_out_gain,
                            final_gain.reshape(1, D), jnp.zeros((6, D), F32)], axis=0)
    vecs = vecs + token_c[0, 0]
    v512 = jnp.concatenate([conv_dw_b, conv_ln_g, conv_ln_b, gm_ln_g, gm_ln_b, jnp.zeros((3, DC), F32)], axis=0)
    bs_exp = jnp.repeat(jnp.transpose(gm_bs[0]), HD, axis=1)
    gm_ws_t = jnp.swapaxes(gm_ws[0], 1, 2)

    tm_big, tm = min(TILE_BIG, s), min(TILE, s)
    w_in_g, w_out_g = _finish_gather("gather_in_out", gather_in, vecs)
    w_out_b = w_out_g.reshape(D, D)
    z, a0, h1_t = _fwd_in(x2d, vecs, w_in_g, tm_big)
    xh_a, sp, x2, o1, h2, gu, dgu, vh, dgv, ln_st, h2_t = _fwd_mid(
        a0, z, x2d, vecs, v512, conv_w, gm_ws, bs_exp, w_out_b, tm_big)
    w_up_t, w_down_g = _finish_gather("gather_up_down", gather_ffn, h2)
    w_down_p = jnp.pad(w_down_g.reshape(4, NSH, D), ((0, 0), (0, PSH - NSH), (0, 0)))
    up_pre, vg, dx3, acc_f = _fwd_ffn(h2, x2, target, vecs, ffn_wb, w_up_t, w_down_p, tm)

    core = ac.reshape(1).astype(jnp.int32)
    mychip = 2 * ax + ay

    other = 1 - core

    def to_pairs(named):
        g4s = [g.reshape((4, 2) + g.shape[1:]) for _, g in named]
        turn = [nm == "w_up" for nm, _ in named]
        halves = [_other_half("rs_other_half_" + t[0], g4, other, tr) for t, g4, tr in zip(named, g4s, turn)]
        from_sibling = _sibling_swap("rs_sibling_" + named[0][0], halves)
        return [_pair_add("rs_pair_add_" + t[0], g4, rv, core, tr)
                for t, g4, rv, tr in zip(named, g4s, from_sibling, turn)]

    dh2, dw_up, dw_down, acc_fw = _bwd_ffn(dx3, up_pre, vg, h2_t, vecs, ffn_wb, w_up_t, w_down_p, tm_big)
    acc_fw = jnp.transpose(acc_fw, (2, 0, 1, 3)).reshape(SUB, 2 * PFF)
    exchange_ffn, token_x = _start_exchange("rs_chips_ffn", to_pairs(
        [("w_up", dw_up), ("w_down", dw_down.reshape(NDEV, w_down.shape[1], D))]), mychip)
    gx, dz, y_t, do1, acc_m, acc_p, dcw, dws, dbs_t = _bwd_mid(
        dh2, dx3, x2, x2d, o1, z, a0, xh_a, sp, gu, dgu, vh, dgv, ln_st, vecs + token_x[0, 0], v512, conv_w, gm_ws[0],
        gm_ws_t, w_out_b, w_in_g, tm)
    rows = jnp.concatenate([acc_m[1:3], acc_m[0:1], acc_m[5:7], acc_f[2:3], acc_m[3:5], acc_m[7:8], acc_f[0:2],
                            jnp.zeros((5, D), F32)], axis=0)
    small_gather, token_s = _start_gather("gather_small_grads", [rows, acc_p, dws, dbs_t, acc_fw, dcw], me)
    dw_in = _mm_all_slots("dw_in", h1_t, dz, w_in.shape[2], token_s)
    small_pass, token_p = _pass_gather("gather_small_grads", small_gather, dw_in)
    dw_out = _mm_row_blocks("dw_out", y_t, do1, DW_OUT_ROWS, token_p).reshape(NDEV, w_out.shape[1], D)
    exchange_mix, token_m = _start_exchange("rs_chips_mix", to_pairs([("w_in", dw_in), ("w_out", dw_out)]), mychip)

    rows_all, p_all, ws_all, bst_all, fwg_all, cwg_all = _end_gather("gather_small_grads", small_pass, token_m)
    (g_b_ada, g_n1, g_mog, g_n2, g_gf, loss_cols, g_cb, g_clg, g_clb, g_glg, g_glb, g_ws, g_bs, fw_sum,
     cw_sum) = _sum_small(rows_all, p_all, ws_all, bst_all, fwg_all, cwg_all)
    loss = jnp.sum(loss_cols)
    g_fb = fw_sum[3].reshape(NDEV, PSH)[:, :NSH].reshape(ffn_dw_b.shape)
    g_fw = lax.dynamic_index_in_dim(fw_sum[:KF].reshape(KF, NDEV, PSH), me, axis=1, keepdims=False)[:, :NSH]
    g_fw = g_fw.reshape(ffn_dw_w.shape)
    g_cw = lax.dynamic_slice(cw_sum, (0, me * n_cw), (KC, n_cw)).reshape(conv_dw_w.shape)
    small = [
        (b_ada, g_b_ada, m_b_ada, v_b_ada), (norm1_gain, g_n1, m_norm1_gain, v_norm1_gain),
        (conv_dw_w, g_cw, m_conv_dw_w, v_conv_dw_w), (conv_dw_b, g_cb, m_conv_dw_b, v_conv_dw_b),
        (conv_ln_g, g_clg, m_conv_ln_g, v_conv_ln_g), (conv_ln_b, g_clb, m_conv_ln_b, v_conv_ln_b),
        (gm_ln_g, g_glg, m_gm_ln_g, v_gm_ln_g), (gm_ln_b, g_glb, m_gm_ln_b, v_gm_ln_b),
        (gm_ws, g_ws, m_gm_ws, v_gm_ws), (gm_bs, g_bs, m_gm_bs, v_gm_bs),
        (mix_out_gain, g_mog, m_mix_out_gain, v_mix_out_gain), (norm2_gain, g_n2, m_norm2_gain, v_norm2_gain),
        (ffn_dw_w, g_fw, m_ffn_dw_w, v_ffn_dw_w), (ffn_dw_b, g_fb, m_ffn_dw_b, v_ffn_dw_b),
        (final_gain, g_gf, m_final_gain, v_final_gain)]
    small_out = _adam_small(small)
    res = {}
    for name, q, o in zip(("b_ada", "norm1_gain", "conv_dw_w", "conv_dw_b", "conv_ln_g", "conv_ln_b", "gm_ln_g",
                           "gm_ln_b", "gm_ws", "gm_bs", "mix_out_gain", "norm2_gain", "ffn_dw_w", "ffn_dw_b",
                           "final_gain"), small, small_out):
        res[name] = (q[1],) + o

    dmod_all = rows_all[:, :6].reshape(NDEV, 6 * D)
    dm_cols = lax.dynamic_slice(dmod_all, (0, me * n_ada), (NDEV, n_ada))
    g_ada = _ada_grad(jnp.transpose(c_all[:, 0, :]), dm_cols)
    res["w_ada"] = (g_ada,) + tuple(_adam3("adam_ada", w_ada, g_ada, m_w_ada, v_w_ada))

    big = [("w_up", up_t(w_up), up_t(m_w_up), up_t(v_w_up)), ("w_down", w_down, m_w_down, v_w_down),
           ("w_in", w_in, m_w_in, v_w_in), ("w_out", w_out, m_w_out, v_w_out)]
    from_chips = list(_finish_exchange("rs_chips_ffn", exchange_ffn, res["w_ada"][1]))
    for t, parts in zip(big[:2], from_chips):
        res[t[0]] = tuple(_sum_adam("rs_sum_adam_" + t[0], parts, t[1], t[2], t[3]))
    res["w_up"] = tuple(up_t(a) for a in res["w_up"])
    from_chips = list(_finish_exchange("rs_chips_mix", exchange_mix, res["w_down"][1]))
    for t, parts in zip(big[2:], from_chips):
        res[t[0]] = tuple(_sum_adam("rs_sum_adam_" + t[0], parts, t[1], t[2], t[3]))

    order = ("w_ada", "b_ada", "norm1_gain", "w_in", "conv_dw_w", "conv_dw_b", "conv_ln_g", "conv_ln_b", "gm_ln_g",
             "gm_ln_b", "gm_ws", "gm_bs", "mix_out_gain", "w_out", "norm2_gain", "w_up", "ffn_dw_w", "ffn_dw_b",
             "w_down", "final_gain")
    return (loss, gx.reshape(x.shape), *[res[n][0] for n in order], *[res[n][1] for n in order],
            *[res[n][2] for n in order], *[res[n][3] for n in order])
```

```python
import functools

import jax
import jax.numpy as jnp
from jax import lax
from jax.experimental import pallas as pl
from jax.experimental.pallas import tpu as pltpu

F32 = jnp.float32
BF16 = jnp.bfloat16
NDEV = 8
D = 1024
DC = 512
DFF = 2816
NSH = 704
PSH = 768
PFF = 4 * PSH
KC = 31
KF = 3
CHUNK = 128
NH = 8
HD = 64
HALO_C = 32
HALO_F = 8
LANES = 128
SUB = 8
VROWS = 16
CW_ROWS = 32
TILE_BIG = 512
ROW_BLOCK_BYTES = 2 << 20
TILE = 256
TK_IN = 2048
TK_OUT = 4096
RMS_EPS = 1e-6
LN_EPS = 1e-5
ADAM_LR = 0.001
ADAM_B1 = 0.9
ADAM_B2 = 0.999
ADAM_EPS = 1e-08
ADAM_WD = 0.01
ADAM_STEP = 10
GELU_K = 0.7978845608028654
GELU_C = 0.044715

MESH = pl.DeviceIdType.MESH
ANY = pl.BlockSpec(memory_space=pl.ANY)

G1, SH1, SC1, GT1, G2, SH2, SC2, GT2, MOG, GF = range(10)
CB, CLG, CLB, GLG, GLB = range(5)


def _full(shape):
    return pl.BlockSpec(shape, lambda *_: (0,) * len(shape))


def _arb(n=1):
    return pltpu.CompilerParams(dimension_semantics=("arbitrary",) * n)


def _row(ref, r):
    return ref[pl.ds(r, 1), :]


def _colsum(v):
    return jnp.sum(v, axis=0, keepdims=True)


def _rowmean(v):
    return jnp.mean(v, axis=-1, keepdims=True)


def _rms(x):
    r = lax.rsqrt(_rowmean(x * x) + RMS_EPS)
    return x * r, r


def _rms_bwd(dxn, xn, r):
    return r * (dxn - xn * _rowmean(dxn * xn))


def _ln(x):
    mu = _rowmean(x)
    xc = x - mu
    rstd = lax.rsqrt(_rowmean(xc * xc) + LN_EPS)
    return xc * rstd, rstd


def _ln_bwd(dxh, xhat, rstd):
    return rstd * (dxh - _rowmean(dxh) - xhat * _rowmean(dxh * xhat))


def _sigmoid(x):
    return 0.5 * jnp.tanh(0.5 * x) + 0.5


def _gelu(x):
    t = jnp.tanh(GELU_K * (x + GELU_C * x * x * x))
    return 0.5 * x * (1.0 + t), t


def _gelu_grad(x, t):
    return 0.5 * (1.0 + t) + 0.5 * x * (1.0 - t * t) * (GELU_K * (1.0 + 3.0 * GELU_C * x * x))


def _dot(a, b):
    return jnp.dot(a, b, preferred_element_type=F32)


def _dot_nt(a, b):
    return lax.dot_general(a, b, (((1,), (1,)), ((), ())), preferred_element_type=F32)


def _shift_up(e, s):
    n = e.shape[0]
    return pltpu.roll(e, (n - s) % n, 0)


def _place():
    return lax.axis_index("x"), lax.axis_index("y"), lax.axis_index("c")


def _all_gather(name, xs):
    n = len(xs)

    def body(*refs):
        x_refs, out_refs = refs[:n], refs[n:2 * n]
        send_sems, recv_sems, local_sems = refs[2 * n:]
        x, y, c = _place()
        me, sibling = (x, y, c), (x, y, 1 - c)
        chips = [(1 - x, y), (x, 1 - y), (1 - x, 1 - y)]

        def copy(a, k, block, to, own=False):
            px, py, pc = block
            slot = out_refs[a].at[4 * px + 2 * py + pc]
            return pltpu.make_async_remote_copy(
                src_ref=x_refs[a] if own else slot, dst_ref=slot,
                send_sem=send_sems.at[7 * a + k], recv_sem=recv_sems.at[7 * a + k], device_id=to, device_id_type=MESH)

        mine = [pltpu.make_async_copy(x_refs[a], out_refs[a].at[4 * x + 2 * y + c], local_sems.at[a]) for a in range(n)]
        for cp in mine:
            cp.start()
        first = []
        for a in range(n):
            first.append(copy(a, 0, me, sibling, own=True))
            first += [copy(a, 1 + j, me, (*chip, c), own=True) for j, chip in enumerate(chips)]
        for cp in first:
            cp.start()
        passed = []
        for j, chip in enumerate(chips):
            for a in range(n):
                copy(a, 1 + j, (*chip, c), me).wait_recv()
                cp = copy(a, 4 + j, (*chip, c), sibling)
                cp.start()
                passed.append(cp)
        for a in range(n):
            copy(a, 0, sibling, me).wait_recv()
            for j, chip in enumerate(chips):
                copy(a, 4 + j, (*chip, 1 - c), me).wait_recv()
        for cp in first + passed:
            cp.wait_send()
        for cp in mine:
            cp.wait()

    return pl.pallas_call(
        body, name=name, out_shape=[jax.ShapeDtypeStruct((NDEV,) + a.shape, a.dtype) for a in xs],
        in_specs=[ANY] * n, out_specs=[ANY] * n,
        scratch_shapes=[pltpu.SemaphoreType.DMA((7 * n,)), pltpu.SemaphoreType.DMA((7 * n,)),
                        pltpu.SemaphoreType.DMA((n,))],
    )(*xs)


def _sibling_swap(name, hs):
    n = len(hs)

    def body(*refs):
        h_refs, out_refs = refs[:n], refs[n:2 * n]
        send_sems, recv_sems = refs[2 * n:]
        x, y, c = _place()
        cps = [pltpu.make_async_remote_copy(
            src_ref=h_refs[a].at[k], dst_ref=out_refs[a].at[k],
            send_sem=send_sems.at[4 * a + k], recv_sem=recv_sems.at[4 * a + k],
            device_id=(x, y, 1 - c), device_id_type=MESH) for a in range(n) for k in range(4)]
        for cp in cps:
            cp.start()
        for cp in cps:
            cp.wait()

    return pl.pallas_call(
        body, name=name, out_shape=[jax.ShapeDtypeStruct(h.shape, h.dtype) for h in hs],
        in_specs=[ANY] * n, out_specs=[ANY] * n,
        scratch_shapes=[pltpu.SemaphoreType.DMA((4 * n,)), pltpu.SemaphoreType.DMA((4 * n,))],
    )(*hs)


HBM = pl.BlockSpec(memory_space=pltpu.HBM)
SEM = pl.BlockSpec(memory_space=pltpu.SEMAPHORE)
EFFECT = pltpu.SideEffectType.DATAFLOW_SIDE_EFFECTING


def _in_hbm(a):
    return pltpu.with_memory_space_constraint(a, pltpu.HBM)


def _split_start(name, bufs, copies):
    n = len(bufs)

    def body(*refs):
        for cp in copies(refs[:n], refs[n], refs[n + 1]):
            cp.start()
        refs[-1][...] = jnp.zeros_like(refs[-1])

    out = pl.pallas_call(
        body, name=name,
        out_shape=(pltpu.SemaphoreType.DMA((copies.count,)), pltpu.SemaphoreType.DMA((copies.count,)),
                   *[pltpu.HBM(a.shape, a.dtype) for a in bufs], jax.ShapeDtypeStruct((SUB, LANES), F32)),
        in_specs=[HBM] * n, out_specs=(SEM, SEM, *[HBM] * n, pl.BlockSpec(memory_space=pltpu.VMEM)),
        input_output_aliases={i: 2 + i for i in range(n)},
        compiler_params=pltpu.CompilerParams(has_side_effects=EFFECT),
    )(*[_in_hbm(a) for a in bufs])
    return (out[0], out[1], list(out[2:2 + n])), out[-1]


def _split_wait(name, handle, copies, after):
    send_sems, recv_sems, bufs = handle
    n = len(bufs)

    def body(*refs):
        for cp in copies(refs[:n], refs[n], refs[n + 1]):
            cp.wait_send()
            cp.wait_recv()

    out = pl.pallas_call(
        body, name=name, out_shape=tuple(pltpu.HBM(a.shape, a.dtype) for a in bufs),
        in_specs=[HBM] * n + [SEM, SEM, pl.BlockSpec(memory_space=pl.ANY)], out_specs=tuple([HBM] * n),
        input_output_aliases={i: i for i in range(n)},
        compiler_params=pltpu.CompilerParams(has_side_effects=EFFECT),
    )(*bufs, send_sems, recv_sems, after)
    return list(out)


class _GatherFirstCopies:
    def __init__(self, n):
        self.n, self.count = n, 4 * n

    def __call__(self, refs, send_sems, recv_sems):
        x, y, c = _place()
        peers = [(x, y, 1 - c), (1 - x, y, c), (x, 1 - y, c), (1 - x, 1 - y, c)]
        return [pltpu.make_async_remote_copy(
            src_ref=refs[a], dst_ref=refs[self.n + a].at[4 * x + 2 * y + c],
            send_sem=send_sems.at[4 * a + k], recv_sem=recv_sems.at[4 * a + k], device_id=peer, device_id_type=MESH)
            for a in range(self.n) for k, peer in enumerate(peers)]


class _GatherPassCopies:
    def __init__(self, n):
        self.n, self.count = n, 3 * n

    def __call__(self, refs, send_sems, recv_sems):
        x, y, c = _place()
        cps = []
        for a in range(self.n):
            for j, (px, py) in enumerate([(1 - x, y), (x, 1 - y), (1 - x, 1 - y)]):
                slot = refs[a].at[4 * px + 2 * py + c]
                cps.append(pltpu.make_async_remote_copy(
                    src_ref=slot, dst_ref=slot, send_sem=send_sems.at[3 * a + j], recv_sem=recv_sems.at[3 * a + j],
                    device_id=(x, y, 1 - c), device_id_type=MESH))
        return cps


class _ExchangeCopies:
    def __init__(self, n):
        self.n, self.count = n, 3 * n

    def __call__(self, refs, send_sems, recv_sems):
        x, y, c = _place()
        cps = []
        for a in range(self.n):
            for j, (px, py) in enumerate([(1 - x, y), (x, 1 - y), (1 - x, 1 - y)]):
                cps.append(pltpu.make_async_remote_copy(
                    src_ref=refs[a].at[2 * px + py], dst_ref=refs[self.n + a].at[2 * x + y],
                    send_sem=send_sems.at[3 * a + j], recv_sem=recv_sems.at[3 * a + j],
                    device_id=(px, py, c), device_id_type=MESH))
        return cps


def _own_slot(nslot, src, index):
    land = lax.empty((nslot,) + src.shape, src.dtype)
    return lax.dynamic_update_slice(land, src[None], (index,) + (0,) * src.ndim)


def _start_gather(tag, xs, me):
    lands = [_own_slot(NDEV, a, me) for a in xs]
    return _split_start(tag + "_start", list(xs) + lands, _GatherFirstCopies(len(xs)))


def _pass_gather(tag, handle, after):
    n = len(handle[2]) // 2
    lands = _split_wait(tag + "_wait", handle, _GatherFirstCopies(n), after)[n:]
    return _split_start(tag + "_pass", lands, _GatherPassCopies(n))


def _end_gather(tag, passing, after):
    return _split_wait(tag + "_pass_wait", passing, _GatherPassCopies(len(passing[2])), after)


def _finish_gather(tag, handle, after):
    passing, token = _pass_gather(tag, handle, after)
    return _end_gather(tag, passing, token)


def _start_exchange(tag, hs, mychip):
    lands = [lax.empty(h.shape, h.dtype) for h in hs]
    return _split_start(tag + "_start", list(hs) + lands, _ExchangeCopies(len(hs)))


def _finish_exchange(tag, handle, after):
    n = len(handle[2]) // 2
    bufs = _split_wait(tag + "_wait", handle, _ExchangeCopies(n), after)
    return list(zip(bufs[:n], bufs[n:]))


def _mod_part(c_all, w_ada, b_cols):
    ncol = w_ada.shape[2]

    def body(c_ref, w_ref, b_ref, o_ref):
        cv = c_ref[:, 0, :]
        ca = cv * _sigmoid(cv)
        o_ref[...] = _dot(ca.astype(BF16), w_ref[0].astype(BF16)) + b_ref[...]

    return pl.pallas_call(body, name="mod_part", out_shape=jax.ShapeDtypeStruct((NDEV, ncol), F32))(
        c_all, w_ada, b_cols)


def _ada_grad(c_all_t, dmod_cols):
    ncol = dmod_cols.shape[1]

    def body(ct_ref, dm_ref, o_ref):
        ct = ct_ref[...]
        ca = ct * _sigmoid(ct)
        acc = jnp.zeros((D, ncol), F32)
        for b in range(NDEV):
            acc = acc + ca[:, b:b + 1] * dm_ref[pl.ds(b, 1), :]
        o_ref[0] = acc

    return pl.pallas_call(body, name="ada_grad", out_shape=jax.ShapeDtypeStruct((1, D, ncol), F32))(
        c_all_t, dmod_cols)


def _fwd_in(x2d, vecs, w_in_g, tm):
    s = x2d.shape[0]
    nc = w_in_g.shape[2]

    def body(x_ref, v_ref, w_ref, z_ref, a0_ref, h1t_ref):
        xn, _ = _rms(x_ref[...])
        h = (xn * _row(v_ref, G1)) * (1.0 + _row(v_ref, SC1)) + _row(v_ref, SH1)
        hb = h.astype(BF16)
        h1t_ref[...] = hb.T
        for d in range(NDEV):
            z_ref[:, pl.ds(d * nc, nc)] = _dot(hb, w_ref[d])
        a0_ref[...] = z_ref[:, :DC] * _sigmoid(z_ref[:, DC:2 * DC])

    return pl.pallas_call(
        body, name="fwd_in", grid=(s // tm,),
        in_specs=[pl.BlockSpec((tm, D), lambda i: (i, 0)), _full((VROWS, D)), _full((NDEV, D, nc))],
        out_specs=[pl.BlockSpec((tm, 4 * DC), lambda i: (i, 0)), pl.BlockSpec((tm, DC), lambda i: (i, 0)),
                   pl.BlockSpec((D, tm), lambda i: (0, i))],
        out_shape=[jax.ShapeDtypeStruct((s, 4 * DC), F32), jax.ShapeDtypeStruct((s, DC), F32),
                   jax.ShapeDtypeStruct((D, s), BF16)],
        compiler_params=_arb(),
    )(x2d, vecs, w_in_g)


def _causal_mask(lower):
    r = lax.broadcasted_iota(jnp.int32, (CHUNK, CHUNK), 0)
    c = lax.broadcasted_iota(jnp.int32, (CHUNK, CHUNK), 1)
    return (r >= c) if lower else (r <= c)


def _first_head_lanes():
    return lax.broadcasted_iota(jnp.int32, (CHUNK, CHUNK), 1) < HD


def _fwd_mid(a0, z, x2d, vecs, v512, conv_w, gm_ws, bs_exp, w_out_b, tm):
    s = x2d.shape[0]
    hb = tm // HALO_C

    def body(a0_ref, halo_ref, zg_ref, x_ref, v_ref, p_ref, cw_ref, ws_ref, bs_ref, wo_ref,
             xh_ref, sp_ref, x2_ref, o1_ref, h2_ref, gu_ref, dgu_ref, vh_ref, dgv_ref, st_ref, h2t_ref, a1_s):
        i = pl.program_id(0)
        for c0 in range(0, DC, LANES):
            cols = pl.ds(c0, LANES)
            halo = halo_ref[:, cols]
            e = jnp.concatenate([jnp.where(i > 0, halo, jnp.zeros_like(halo)), a0_ref[:, cols]], axis=0)
            acc = jnp.broadcast_to(p_ref[pl.ds(CB, 1), cols], (tm, LANES))
            for k in range(KC):
                acc = acc + _shift_up(e, HALO_C - (KC - 1) + k)[:tm, :] * cw_ref[pl.ds(k, 1), cols]
            a1_s[:, cols] = acc
        xh, rstd = _ln(a1_s[...])
        xh_ref[...] = xh
        a2 = xh * _row(p_ref, CLG) + _row(p_ref, CLB)
        a3 = a2 * _sigmoid(a2)
        gu_pre = zg_ref[:, :DC]
        gu, tu = _gelu(gu_pre)
        gu_ref[...] = gu
        dgu_ref[...] = _gelu_grad(gu_pre, tu)
        gv_pre = zg_ref[:, DC:]
        gvg, tv = _gelu(gv_pre)
        dgv_ref[...] = _gelu_grad(gv_pre, tv)
        vh, vrstd = _ln(gvg)
        vh_ref[...] = vh
        st_ref[...] = jnp.concatenate([jnp.broadcast_to(rstd, (tm, LANES)), jnp.broadcast_to(vrstd, (tm, LANES))], axis=1)
        gvn = (vh * _row(p_ref, GLG) + _row(p_ref, GLB)).astype(BF16)
        low = _causal_mask(True)
        first = _first_head_lanes()
        wm = [jnp.where(low, ws_ref[0, h], 0.0).astype(BF16) for h in range(NH)]
        for n in range(tm // CHUNK):
            for p in range(NH // 2):
                v = gvn[n * CHUNK:(n + 1) * CHUNK, p * CHUNK:(p + 1) * CHUNK]
                blk = jnp.where(first, _dot(wm[2 * p], v), _dot(wm[2 * p + 1], v))
                sp_ref[pl.ds(n * CHUNK, CHUNK), pl.ds(p * CHUNK, CHUNK)] = blk + bs_ref[:, pl.ds(p * CHUNK, CHUNK)]
        g = gu * sp_ref[...]
        an, _ = _rms(a3)
        gn, _ = _rms(g)
        mog = _row(v_ref, MOG)
        y = jnp.concatenate([an * mog[:, :DC], gn * mog[:, DC:]], axis=1).astype(BF16)
        o1 = _dot(y, wo_ref[...])
        o1_ref[...] = o1
        x2 = x_ref[...] + _row(v_ref, GT1) * o1
        x2_ref[...] = x2
        xn2, _ = _rms(x2)
        h2 = (xn2 * _row(v_ref, G2)) * (1.0 + _row(v_ref, SC2)) + _row(v_ref, SH2)
        h2b = h2.astype(BF16)
        h2_ref[...] = h2b
        h2t_ref[...] = h2b.T

    tile = lambda w: pl.BlockSpec((tm, w), lambda i: (i, 0))
    return pl.pallas_call(
        body, name="fwd_mid", grid=(s // tm,),
        in_specs=[tile(DC), pl.BlockSpec((HALO_C, DC), lambda i: (jnp.maximum(i * hb - 1, 0), 0)),
                  pl.BlockSpec((tm, 2 * DC), lambda i: (i, 1)), tile(D), _full((VROWS, D)), _full((SUB, DC)),
                  _full((CW_ROWS, DC)), _full((1, NH, CHUNK, CHUNK)), _full((CHUNK, DC)), _full((D, D))],
        out_specs=[tile(DC), tile(DC), tile(D), tile(D), tile(D), tile(DC), tile(DC), tile(DC), tile(DC), tile(2 * LANES),
                   pl.BlockSpec((D, tm), lambda i: (0, i))],
        out_shape=[jax.ShapeDtypeStruct((s, DC), F32), jax.ShapeDtypeStruct((s, DC), F32),
                   jax.ShapeDtypeStruct((s, D), F32), jax.ShapeDtypeStruct((s, D), F32),
                   jax.ShapeDtypeStruct((s, D), BF16)] + [jax.ShapeDtypeStruct((s, DC), F32)] * 4
        + [jax.ShapeDtypeStruct((s, 2 * LANES), F32), jax.ShapeDtypeStruct((D, s), BF16)],
        scratch_shapes=[pltpu.VMEM((tm, DC), F32)],
        compiler_params=_arb(),
    )(a0, a0, z, x2d, vecs, v512, conv_w, gm_ws, bs_exp, w_out_b)


def _ffn_conv(fw_ref, cols, p2, p1, pre):
    return (fw_ref[pl.ds(3, 1), cols] + fw_ref[pl.ds(0, 1), cols] * p2
            + fw_ref[pl.ds(1, 1), cols] * p1 + fw_ref[pl.ds(2, 1), cols] * pre)


def _fwd_ffn(h2, x2, target, vecs, ffn_wb, w_up_t, w_down_p, tm):
    s = x2.shape[0]

    def body(h2_ref, x2_ref, t_ref, v_ref, fw_ref, wu_hbm, wd_hbm,
             up_ref, vg_ref, dx3_ref, acc_ref, wu, wd, carry, stage):
        i = pl.program_id(0)

        @pl.when(i == 0)
        def _():
            for sh in range(NDEV):
                pltpu.sync_copy(wu_hbm.at[sh], stage)
                wu[sh] = stage[...].T
            pltpu.sync_copy(wd_hbm, wd)
            carry[...] = jnp.zeros_like(carry)
            acc_ref[...] = jnp.zeros_like(acc_ref)

        h2v = h2_ref[...]
        o2 = jnp.zeros((tm, D), F32)
        for j in range(4):
            conv = []
            for sh in (j, 4 + j):
                cols = pl.ds(sh * PSH, PSH)
                pre = _dot(h2v, wu[sh])
                up_ref[:, cols] = pre.astype(BF16)
                e = jnp.concatenate([carry[:, cols], pre], axis=0)
                carry[:, cols] = pre[tm - HALO_F:, :]
                conv.append(_ffn_conv(fw_ref, cols, pltpu.roll(e, 2, 0)[HALO_F:, :],
                                      pltpu.roll(e, 1, 0)[HALO_F:, :], pre))
            val, gate = conv
            vg_ref[:, pl.ds(j * PSH, PSH)] = val.astype(BF16)
            vg_ref[:, pl.ds((4 + j) * PSH, PSH)] = gate.astype(BF16)
            f = ((gate * _sigmoid(gate)) * val).astype(BF16)
            o2 = o2 + _dot(f, wd[j])
        x3 = x2_ref[...] + _row(v_ref, GT2) * o2
        xn3, r3 = _rms(x3)
        gf = _row(v_ref, GF)
        diff = xn3 * gf - t_ref[...]
        acc_ref[pl.ds(1, 1), :] += _colsum(diff * diff) * (0.5 / D)
        dout = diff * (1.0 / D)
        acc_ref[pl.ds(0, 1), :] += _colsum(dout * xn3)
        dx3 = _rms_bwd(dout * gf, xn3, r3)
        dx3_ref[...] = dx3
        acc_ref[pl.ds(2, 1), :] += _colsum(dx3 * o2)

    tile = lambda w: pl.BlockSpec((tm, w), lambda i: (i, 0))
    return pl.pallas_call(
        body, name="fwd_ffn", grid=(s // tm,),
        in_specs=[tile(D), tile(D), tile(D), _full((VROWS, D)), _full((SUB, 2 * PFF)), ANY, ANY],
        out_specs=[tile(2 * PFF), tile(2 * PFF), tile(D), _full((SUB, D))],
        out_shape=[jax.ShapeDtypeStruct((s, 2 * PFF), BF16), jax.ShapeDtypeStruct((s, 2 * PFF), BF16),
                   jax.ShapeDtypeStruct((s, D), F32), jax.ShapeDtypeStruct((SUB, D), F32)],
        scratch_shapes=[pltpu.VMEM((NDEV, D, PSH), BF16), pltpu.VMEM((4, PSH, D), BF16),
                        pltpu.VMEM((HALO_F, 2 * PFF), F32), pltpu.VMEM((PSH, D), BF16)],
        compiler_params=_arb(),
    )(h2, x2, target, vecs, ffn_wb, w_up_t, w_down_p)


def _bwd_ffn(dx3, up_pre, vg, h2_t, vecs, ffn_wb, w_up_t, w_down_p, tm):
    s = dx3.shape[0]
    nt = s // tm

    def body(dx3_ref, up_ref, upg_ref, val_ref, gate_ref, h2t_ref, v_ref, fw_ref, fwg_ref, wu_ref, wug_ref, wd_ref,
             dh2_ref, dwu_ref, dwd_ref, accf_ref, carry):
        i = pl.program_id(1)

        @pl.when(i == 0)
        def _():
            for ref in (carry, dwu_ref, dwd_ref, accf_ref):
                ref[...] = jnp.zeros_like(ref)

        do2 = (dx3_ref[...] * _row(v_ref, GT2)).astype(BF16)
        df = _dot_nt(do2, wd_ref[...])
        val = val_ref[...].astype(F32)
        gate = gate_ref[...].astype(F32)
        sg = _sigmoid(gate)
        sl = gate * sg
        f_t = (sl * val).astype(BF16).T
        dwd_ref[...] += _dot(f_t, do2)[:NSH, :]
        dups = (df * sl, df * val * (sg * (1.0 + gate * (1.0 - sg))))
        h2t = h2t_ref[...]
        dh2 = jnp.zeros((tm, D), F32)
        for half, (dup, pre_ref, w_ref, wmat_ref) in enumerate(
                zip(dups, (up_ref, upg_ref), (fw_ref, fwg_ref), (wu_ref, wug_ref))):
            cols = pl.ds(half * PSH, PSH)
            e = jnp.concatenate([dup, carry[:, cols]], axis=0)
            carry[:, cols] = dup[:HALO_F, :]
            d1 = _shift_up(e, 1)[:tm, :]
            d2 = _shift_up(e, 2)[:tm, :]
            pre = pre_ref[...].astype(F32)
            accf_ref[half, pl.ds(3, 1), :] += _colsum(dup)
            accf_ref[half, pl.ds(0, 1), :] += _colsum(d2 * pre)
            accf_ref[half, pl.ds(1, 1), :] += _colsum(d1 * pre)
            accf_ref[half, pl.ds(2, 1), :] += _colsum(dup * pre)
            dpre = (_row(w_ref, 0) * d2 + _row(w_ref, 1) * d1 + _row(w_ref, 2) * dup).astype(BF16)
            dwu_ref[half] += _dot(h2t, dpre)[:, :NSH]
            dh2 = dh2 + _dot(dpre, wmat_ref[...])
        dh2_ref[...] = dh2

    rev = lambda j, i: nt - 1 - i
    in_specs = [
        pl.BlockSpec((tm, D), lambda j, i: (rev(j, i), 0)),
        pl.BlockSpec((tm, PSH), lambda j, i: (rev(j, i), j)), pl.BlockSpec((tm, PSH), lambda j, i: (rev(j, i), 4 + j)),
        pl.BlockSpec((tm, PSH), lambda j, i: (rev(j, i), j)), pl.BlockSpec((tm, PSH), lambda j, i: (rev(j, i), 4 + j)),
        pl.BlockSpec((D, tm), lambda j, i: (0, rev(j, i))), _full((VROWS, D)),
        pl.BlockSpec((SUB, PSH), lambda j, i: (0, j)), pl.BlockSpec((SUB, PSH), lambda j, i: (0, 4 + j)),
        pl.BlockSpec((None, PSH, D), lambda j, i: (j, 0, 0)), pl.BlockSpec((None, PSH, D), lambda j, i: (4 + j, 0, 0)),
        pl.BlockSpec((None, PSH, D), lambda j, i: (j, 0, 0))]
    dh2, dw_up, dw_down, accf = pl.pallas_call(
        body, name="bwd_ffn", grid=(4, nt), in_specs=in_specs,
        out_specs=[pl.BlockSpec((None, tm, D), lambda j, i: (j, rev(j, i), 0)),
                   pl.BlockSpec((2, None, D, NSH), lambda j, i: (0, j, 0, 0)),
                   pl.BlockSpec((None, NSH, D), lambda j, i: (j, 0, 0)),
                   pl.BlockSpec((2, None, SUB, PSH), lambda j, i: (0, j, 0, 0))],
        out_shape=[jax.ShapeDtypeStruct((4, s, D), F32), jax.ShapeDtypeStruct((2, 4, D, NSH), F32),
                   jax.ShapeDtypeStruct((4, NSH, D), F32), jax.ShapeDtypeStruct((2, 4, SUB, PSH), F32)],
        scratch_shapes=[pltpu.VMEM((HALO_F, 2 * PSH), F32)],
        compiler_params=_arb(2),
    )(dx3, up_pre, up_pre, vg, vg, h2_t, vecs, ffn_wb, ffn_wb, w_up_t, w_up_t, w_down_p)
    return dh2, dw_up.reshape(NDEV, D, NSH), dw_down, accf


def _bwd_mid(dh2, dx3, x2, x2d, o1, z, a0, xh_a, sp, gu, dgu, vh, dgv, st, vecs, v512, conv_w, gm_ws, gm_ws_t, w_out_b,
             w_in_g, tm):
    s = x2d.shape[0]
    nt = s // tm
    nc = w_in_g.shape[2]

    def body(dh2a_ref, dh2b_ref, dh2c_ref, dh2d_ref, dx3_ref, x2_ref, x_ref, o1_ref, z_ref, a0_ref, xh_ref, sp_ref,
             gu_ref, dgu_ref, vh_ref, dgv_ref, st_ref, v_ref, p_ref, cw_ref, ws_ref, wst_ref, wo_ref, wi_ref, gx_ref, dz_ref, yt_ref, do1_ref, acc_ref, accp_ref,
             dcw_ref, dws_ref, dbst_ref, dbs_s, carry, da1_s, dsp_s, dgvn_s):
        i = pl.program_id(0)

        @pl.when(i == 0)
        def _():
            for ref in (carry, dbs_s, acc_ref, accp_ref, dcw_ref, dws_ref, dbst_ref):
                ref[...] = jnp.zeros_like(ref)

        dh2v = (dh2a_ref[...] + dh2b_ref[...]) + (dh2c_ref[...] + dh2d_ref[...])
        xn2, r2 = _rms(x2_ref[...])
        g2 = _row(v_ref, G2)
        sc2 = 1.0 + _row(v_ref, SC2)
        acc_ref[pl.ds(5, 1), :] += _colsum(dh2v)
        acc_ref[pl.ds(6, 1), :] += _colsum(dh2v * (xn2 * g2))
        acc_ref[pl.ds(7, 1), :] += _colsum(dh2v * sc2 * xn2)
        dx2v = dx3_ref[...] + _rms_bwd(dh2v * sc2 * g2, xn2, r2)
        do1 = (dx2v * _row(v_ref, GT1)).astype(BF16)
        do1_ref[...] = do1
        acc_ref[pl.ds(0, 1), :] += _colsum(dx2v * o1_ref[...])
        dy = _dot_nt(do1, wo_ref[...])
        mog = _row(v_ref, MOG)

        lane_tiles = DC // LANES
        xh = xh_ref[...]
        rstd = jnp.concatenate([st_ref[:, :LANES]] * lane_tiles, axis=1)
        clg = _row(p_ref, CLG)
        a2 = xh * clg + _row(p_ref, CLB)
        s2 = _sigmoid(a2)
        a3 = a2 * s2
        an, ra = _rms(a3)
        dya = dy[:, :DC]
        da3 = _rms_bwd(dya * mog[:, :DC], an, ra)
        da2 = da3 * (s2 * (1.0 + a2 * (1.0 - s2)))
        accp_ref[pl.ds(CLB, 1), :] += _colsum(da2)
        accp_ref[pl.ds(CLG, 1), :] += _colsum(da2 * xh)
        da1 = _ln_bwd(da2 * clg, xh, rstd)
        accp_ref[pl.ds(CB, 1), :] += _colsum(da1)
        da1_s[...] = da1
        for c0 in range(0, DC, LANES):
            cols = pl.ds(c0, LANES)
            d = da1_s[:, cols]
            e = jnp.concatenate([d, carry[:, cols]], axis=0)
            carry[:, cols] = d[:HALO_C, :]
            a0c = a0_ref[:, cols]
            acc = jnp.zeros((tm, LANES), F32)
            for j in range(KC):
                ahead = _shift_up(e, j)[:tm, :]
                acc = acc + ahead * cw_ref[pl.ds(KC - 1 - j, 1), cols]
                dcw_ref[pl.ds(KC - 1 - j, 1), cols] += _colsum(a0c * ahead)
            sgc = _sigmoid(z_ref[:, pl.ds(DC + c0, LANES)])
            dz_ref[:, cols] = (acc * sgc).astype(BF16)
            dz_ref[:, pl.ds(DC + c0, LANES)] = (acc * z_ref[:, cols] * sgc * (1.0 - sgc)).astype(BF16)

        gu = gu_ref[...]
        vh = vh_ref[...]
        vrstd = jnp.concatenate([st_ref[:, LANES:]] * lane_tiles, axis=1)
        glg = _row(p_ref, GLG)
        gvn = (vh * glg + _row(p_ref, GLB)).astype(BF16)
        spv = sp_ref[...]
        g = gu * spv
        gn, rg = _rms(g)
        yt_ref[...] = jnp.concatenate([an * mog[:, :DC], gn * mog[:, DC:]], axis=1).astype(BF16).T
        acc_ref[pl.ds(4, 1), :] += jnp.concatenate([_colsum(dya * an), _colsum(dy[:, DC:] * gn)], axis=1)
        dg = _rms_bwd(dy[:, DC:] * mog[:, DC:], gn, rg)
        dz_ref[:, pl.ds(2 * DC, DC)] = (dg * spv * dgu_ref[...]).astype(BF16)
        dsp_s[...] = dg * gu
        upper = _causal_mask(False)
        first = _first_head_lanes()
        wmt = [jnp.where(upper, wst_ref[h], 0.0).astype(BF16) for h in range(NH)]
        for n in range(tm // CHUNK):
            rows = pl.ds(n * CHUNK, CHUNK)
            for p in range(NH // 2):
                cols = pl.ds(p * CHUNK, CHUNK)
                dsp = dsp_s[rows, cols]
                dbs_s[:, cols] += dsp
                da = jnp.where(first, dsp, 0.0).astype(BF16)
                db = jnp.where(first, 0.0, dsp).astype(BF16)
                v = gvn[n * CHUNK:(n + 1) * CHUNK, p * CHUNK:(p + 1) * CHUNK]
                dws_ref[2 * p] += _dot_nt(da, v)
                dws_ref[2 * p + 1] += _dot_nt(db, v)
                dgvn_s[rows, cols] = _dot(wmt[2 * p], da) + _dot(wmt[2 * p + 1], db)
        dgvn = dgvn_s[...]
        accp_ref[pl.ds(GLB, 1), :] += _colsum(dgvn)
        accp_ref[pl.ds(GLG, 1), :] += _colsum(dgvn * vh)
        dgvg = _ln_bwd(dgvn * glg, vh, vrstd)
        dz_ref[:, pl.ds(3 * DC, DC)] = (dgvg * dgv_ref[...]).astype(BF16)

        dh1 = jnp.zeros((tm, D), F32)
        for d in range(NDEV):
            dh1 = dh1 + _dot_nt(dz_ref[:, pl.ds(d * nc, nc)], wi_ref[d])
        xn, r1 = _rms(x_ref[...])
        g1 = _row(v_ref, G1)
        sc = 1.0 + _row(v_ref, SC1)
        acc_ref[pl.ds(1, 1), :] += _colsum(dh1)
        acc_ref[pl.ds(2, 1), :] += _colsum(dh1 * (xn * g1))
        acc_ref[pl.ds(3, 1), :] += _colsum(dh1 * sc * xn)
        gx_ref[...] = dx2v + _rms_bwd(dh1 * sc * g1, xn, r1)

        @pl.when(i == nt - 1)
        def _():
            low = _causal_mask(True)
            for h in range(NH):
                dws_ref[h] = jnp.where(low, dws_ref[h], 0.0)
            lane = lax.broadcasted_iota(jnp.int32, (CHUNK, CHUNK), 1)
            out = jnp.zeros((CHUNK, CHUNK), F32)
            for h in range(NH):
                hs = jnp.sum(dbs_s[:, pl.ds((h // 2) * CHUNK, CHUNK)]
                             * ((lane >= (h % 2) * HD) & (lane < (h % 2 + 1) * HD)).astype(F32),
                             axis=1, keepdims=True)
                out = jnp.where(lane == h, hs, out)
            dbst_ref[...] = out

    tile = lambda w: pl.BlockSpec((tm, w), lambda i: (nt - 1 - i, 0))
    return pl.pallas_call(
        body, name="bwd_mid", grid=(nt,),
        in_specs=[pl.BlockSpec((None, tm, D), functools.partial(lambda k, i: (k, nt - 1 - i, 0), k)) for k in range(4)]
        + [tile(D), tile(D), tile(D), tile(D), tile(2 * DC), tile(DC),
                  tile(DC), tile(DC), tile(DC), tile(DC), tile(DC), tile(DC), tile(2 * LANES), _full((VROWS, D)), _full((SUB, DC)), _full((CW_ROWS, DC)),
                  _full((NH, CHUNK, CHUNK)), _full((NH, CHUNK, CHUNK)), _full((D, D)), _full((NDEV, D, nc))],
        out_specs=[tile(D), tile(4 * DC), pl.BlockSpec((D, tm), lambda i: (0, nt - 1 - i)), tile(D),
                   _full((VROWS, D)), _full((SUB, DC)), _full((CW_ROWS, DC)),
                   _full((NH, CHUNK, CHUNK)), _full((CHUNK, CHUNK))],
        out_shape=[jax.ShapeDtypeStruct((s, D), F32), jax.ShapeDtypeStruct((s, 4 * DC), BF16),
                   jax.ShapeDtypeStruct((D, s), BF16), jax.ShapeDtypeStruct((s, D), BF16),
                   jax.ShapeDtypeStruct((VROWS, D), F32), jax.ShapeDtypeStruct((SUB, DC), F32),
                   jax.ShapeDtypeStruct((CW_ROWS, DC), F32), jax.ShapeDtypeStruct((NH, CHUNK, CHUNK), F32),
                   jax.ShapeDtypeStruct((CHUNK, CHUNK), F32)],
        scratch_shapes=[pltpu.VMEM((CHUNK, DC), F32), pltpu.VMEM((HALO_C, DC), F32), pltpu.VMEM((tm, DC), F32),
                        pltpu.VMEM((tm, DC), F32), pltpu.VMEM((tm, DC), F32)],
        compiler_params=_arb(),
    )(dh2, dh2, dh2, dh2, dx3, x2, x2d, o1, z, a0, xh_a, sp, gu, dgu, vh, dgv, st, vecs, v512, conv_w, gm_ws, gm_ws_t, w_out_b, w_in_g)


def _mm_all_slots(name, at, b, bw, tk, after):
    k1, s = at.shape
    nslot = b.shape[1] // bw

    def body(a_ref, b_ref, after_ref, o_ref):
        @pl.when(pl.program_id(0) == 0)
        def _():
            o_ref[...] = jnp.zeros_like(o_ref)

        t = _dot(a_ref[...], b_ref[...])
        for j in range(nslot):
            o_ref[j] += t[:, j * bw:(j + 1) * bw]

    return pl.pallas_call(
        body, name=name, grid=(s // tk,),
        in_specs=[pl.BlockSpec((k1, tk), lambda k: (0, k)), pl.BlockSpec((tk, nslot * bw), lambda k: (k, 0)), ANY],
        out_specs=_full((nslot, k1, bw)), out_shape=jax.ShapeDtypeStruct((nslot, k1, bw), F32),
        compiler_params=_arb(),
    )(at, b, after)


def _adam_math(w, g, m, v):
    m = ADAM_B1 * m + (1.0 - ADAM_B1) * g
    v = ADAM_B2 * v + (1.0 - ADAM_B2) * (g * g)
    m_hat = m / (1.0 - ADAM_B1 ** ADAM_STEP)
    v_hat = v / (1.0 - ADAM_B2 ** ADAM_STEP)
    delta = -ADAM_LR * (m_hat / (jnp.sqrt(v_hat) + ADAM_EPS) + ADAM_WD * w)
    return delta, m, v


def _row_block(rows, cols):
    tr = rows
    while tr * cols * 4 > ROW_BLOCK_BYTES and tr % (4 * SUB) == 0:
        tr //= 2
    return tr


def _adam3(name, w, g, m, v):
    _, rows, cols = w.shape
    tr = _row_block(rows, cols)

    def body(w_ref, g_ref, m_ref, v_ref, d_ref, mo_ref, vo_ref):
        d_ref[...], mo_ref[...], vo_ref[...] = _adam_math(w_ref[...], g_ref[...], m_ref[...], v_ref[...])

    spec = pl.BlockSpec((1, tr, cols), lambda i: (0, i, 0))
    return pl.pallas_call(
        body, name=name, grid=(rows // tr,), in_specs=[spec] * 4, out_specs=[spec] * 3,
        out_shape=[jax.ShapeDtypeStruct(w.shape, F32)] * 3, compiler_params=_arb(),
    )(w, g, m, v)


def _sum_adam(name, sent, landed, mychip, w, m, v):
    n, rows, cols = landed.shape
    tr = _row_block(rows, cols)

    def body(mc_ref, p_ref, own_ref, w_ref, m_ref, v_ref, g_ref, d_ref, mo_ref, vo_ref):
        own = own_ref[...].astype(F32)
        g = jnp.zeros((tr, cols), F32)
        for k in range(n):
            g = g + jnp.where(mc_ref[0] == k, own, p_ref[k].astype(F32))
        g_ref[0] = g
        d_ref[0], mo_ref[0], vo_ref[0] = _adam_math(w_ref[0], g, m_ref[0], v_ref[0])

    spec = pl.BlockSpec((1, tr, cols), lambda i, mc: (0, i, 0))
    return pl.pallas_call(
        body, name=name,
        grid_spec=pltpu.PrefetchScalarGridSpec(
            num_scalar_prefetch=1, grid=(rows // tr,),
            in_specs=[pl.BlockSpec((n, tr, cols), lambda i, mc: (0, i, 0)),
                      pl.BlockSpec((None, tr, cols), lambda i, mc: (mc[0], i, 0))] + [spec] * 3,
            out_specs=[spec] * 4),
        out_shape=[jax.ShapeDtypeStruct(w.shape, F32)] * 4, compiler_params=_arb(),
    )(mychip, landed, sent, w, m, v)


def _other_half(name, g4, other, transpose=False):
    _, _, rows, cols = g4.shape
    tr = rows if transpose else _row_block(rows, cols)
    oshape = (cols, rows) if transpose else (tr, cols)

    def body(c_ref, a_ref, o_ref):
        a = a_ref[...]
        o_ref[...] = (a.T if transpose else a).astype(BF16)

    return pl.pallas_call(
        body, name=name,
        grid_spec=pltpu.PrefetchScalarGridSpec(
            num_scalar_prefetch=1, grid=(4, rows // tr),
            in_specs=[pl.BlockSpec((None, None, tr, cols), lambda k, i, c_ref: (k, c_ref[0], i, 0))],
            out_specs=pl.BlockSpec((None,) + oshape, lambda k, i, c_ref: (k, i, 0))),
        out_shape=jax.ShapeDtypeStruct((4, cols, rows) if transpose else (4, rows, cols), BF16),
        compiler_params=_arb(2),
    )(other, g4)


def _pair_add(name, g4, recv, core, transpose=False):
    _, _, rows, cols = g4.shape
    tr = rows if transpose else _row_block(rows, cols)
    oshape = (cols, rows) if transpose else (tr, cols)

    def body(c_ref, a_ref, b_ref, o_ref):
        a = a_ref[...]
        o_ref[...] = ((a.T if transpose else a) + b_ref[...].astype(F32)).astype(BF16)

    return pl.pallas_call(
        body, name=name,
        grid_spec=pltpu.PrefetchScalarGridSpec(
            num_scalar_prefetch=1, grid=(4, rows // tr),
            in_specs=[pl.BlockSpec((None, None, tr, cols), lambda k, i, c_ref: (k, c_ref[0], i, 0)),
                      pl.BlockSpec((None,) + oshape, lambda k, i, c_ref: (k, i, 0))],
            out_specs=pl.BlockSpec((None,) + oshape, lambda k, i, c_ref: (k, i, 0))),
        out_shape=jax.ShapeDtypeStruct(recv.shape, BF16), compiler_params=_arb(2),
    )(core, g4, recv)


def _sum_small(rows_all, p_all, ws_all, bst_all, fw_all, cw_all):
    def body(a_ref, p_ref, ws_ref, bst_ref, fw_ref, cw_ref,
             g_b_ada, g_n1, g_mog, g_n2, g_gf, loss_cols, g_cb, g_clg, g_clb, g_glg, g_glb, g_ws, g_bs, fw_sum,
             cw_sum):
        def total(ref):
            t = ref[0]
            for k in range(1, NDEV):
                t = t + ref[k]
            return t

        a = total(a_ref)
        g_b_ada[...] = jnp.concatenate([a[k:k + 1, :] for k in range(6)], axis=1)
        g_n1[...] = a[6:7, :]
        g_mog[...] = a[7:8, :]
        g_n2[...] = a[8:9, :]
        g_gf[...] = a[9:10, :].reshape(D)
        loss_cols[...] = a[10:11, :]
        p = total(p_ref)
        for k, ref in zip((CB, CLG, CLB, GLG, GLB), (g_cb, g_clg, g_clb, g_glg, g_glb)):
            ref[...] = p[k:k + 1, :]
        g_ws[0] = total(ws_ref)
        g_bs[0] = jnp.transpose(total(bst_ref))[:NH, :]
        fw_sum[...] = total(fw_ref)
        cw_sum[...] = total(cw_ref)

    vec = lambda n: jax.ShapeDtypeStruct((1, n), F32)
    return pl.pallas_call(
        body, name="sum_small_grads",
        out_shape=[vec(6 * D), vec(D), vec(D), vec(D), jax.ShapeDtypeStruct((D,), F32), vec(D),
                   vec(DC), vec(DC), vec(DC), vec(DC), vec(DC),
                   jax.ShapeDtypeStruct((1, NH, CHUNK, CHUNK), F32), jax.ShapeDtypeStruct((1, NH, CHUNK), F32),
                   jax.ShapeDtypeStruct((SUB, 2 * PFF), F32), jax.ShapeDtypeStruct((CW_ROWS, DC), F32)],
    )(rows_all, p_all, ws_all, bst_all, fw_all, cw_all)


def _adam_small(quads):
    n = len(quads)

    def body(*refs):
        ins, outs = refs[:4 * n], refs[4 * n:]
        for q in range(n):
            w, g, m, v = (r[...] for r in ins[4 * q:4 * q + 4])
            outs[3 * q][...], outs[3 * q + 1][...], outs[3 * q + 2][...] = _adam_math(w, g, m, v)

    flat = [a for q in quads for a in q]
    outs = pl.pallas_call(
        body, name="adam_small",
        out_shape=[jax.ShapeDtypeStruct(q[0].shape, F32) for q in quads for _ in range(3)],
    )(*flat)
    return [tuple(outs[3 * q:3 * q + 3]) for q in range(n)]


def kernel(x, c, w_ada, b_ada, norm1_gain, w_in, conv_dw_w, conv_dw_b, conv_ln_g, conv_ln_b, gm_ln_g, gm_ln_b, gm_ws, gm_bs, mix_out_gain, w_out, norm2_gain, w_up, ffn_dw_w, ffn_dw_b, w_down, final_gain, loss_target, m_w_ada, m_b_ada, m_norm1_gain, m_w_in, m_conv_dw_w, m_conv_dw_b, m_conv_ln_g, m_conv_ln_b, m_gm_ln_g, m_gm_ln_b, m_gm_ws, m_gm_bs, m_mix_out_gain, m_w_out, m_norm2_gain, m_w_up, m_ffn_dw_w, m_ffn_dw_b, m_w_down, m_final_gain, v_w_ada, v_b_ada, v_norm1_gain, v_w_in, v_conv_dw_w, v_conv_dw_b, v_conv_ln_g, v_conv_ln_b, v_gm_ln_g, v_gm_ln_b, v_gm_ws, v_gm_bs, v_mix_out_gain, v_w_out, v_norm2_gain, v_w_up, v_ffn_dw_w, v_ffn_dw_b, v_w_down, v_final_gain):
    s = x.shape[1]
    ax, ay, ac = _place()
    me = 4 * ax + 2 * ay + ac
    n_ada = w_ada.shape[2]
    n_cw = conv_dw_w.shape[2]
    x2d = x[0]
    target = loss_target[0]
    pad_sh = lambda a: jnp.pad(a, [(0, 0)] * (a.ndim - 1) + [(0, PSH - NSH)])

    c_all, cw_all, fw_all = _all_gather("gather_small", [c, conv_dw_w[0], ffn_dw_w[0]])

    first_shards, c_all = lax.optimization_barrier(((w_in[0].astype(BF16), w_out[0].astype(BF16)), c_all))
    gather_in, token_a = _start_gather("gather_in_out", list(first_shards), me)
    c_all = c_all + token_a[0, 0]
    conv_w = jnp.pad(jnp.transpose(cw_all, (1, 0, 2)).reshape(KC, DC), ((0, CW_ROWS - KC), (0, 0)))
    ffn_w = jnp.transpose(pad_sh(fw_all), (1, 0, 2)).reshape(KF, 2 * PFF)
    ffn_b = pad_sh(ffn_dw_b.reshape(NDEV, NSH)).reshape(1, 2 * PFF)
    ffn_wb = jnp.concatenate([ffn_w, ffn_b, jnp.zeros((SUB - KF - 1, 2 * PFF), F32)], axis=0)

    b_cols = lax.dynamic_slice(b_ada, (0, me * n_ada), (1, n_ada))
    (mod_all,) = _all_gather("gather_mod", [_mod_part(c_all, w_ada, b_cols)])
    up_t = lambda a: jnp.swapaxes(a, 1, 2)
    w_up_shard = jnp.pad(up_t(w_up)[0].astype(BF16), ((0, PSH - NSH), (0, 0)))
    shards, mod_all = lax.optimization_barrier(((w_up_shard, w_down[0].astype(BF16)), mod_all))
    gather_ffn, token_c = _start_gather("gather_up_down", list(shards), me)
    mod = lax.dynamic_index_in_dim(mod_all, me, axis=1, keepdims=False).reshape(6, D)
    sh1, sc1, gt1, sh2, sc2, gt2 = [mod[k:k + 1] for k in range(6)]
    vecs = jnp.concatenate([norm1_gain, sh1, sc1, gt1, norm2_gain, sh2, sc2, gt2, mix_out_gain,
                            final_gain.reshape(1, D), jnp.zeros((6, D), F32)], axis=0)
    vecs = vecs + token_c[0, 0]
    v512 = jnp.concatenate([conv_dw_b, conv_ln_g, conv_ln_b, gm_ln_g, gm_ln_b, jnp.zeros((3, DC), F32)], axis=0)
    bs_exp = jnp.repeat(jnp.transpose(gm_bs[0]), HD, axis=1)
    gm_ws_t = jnp.swapaxes(gm_ws[0], 1, 2)

    tm_big, tm = min(TILE_BIG, s), min(TILE, s)
    w_in_g, w_out_g = _finish_gather("gather_in_out", gather_in, vecs)
    w_out_b = w_out_g.reshape(D, D)
    z, a0, h1_t = _fwd_in(x2d, vecs, w_in_g, tm_big)
    xh_a, sp, x2, o1, h2, gu, dgu, vh, dgv, ln_st, h2_t = _fwd_mid(
        a0, z, x2d, vecs, v512, conv_w, gm_ws, bs_exp, w_out_b, tm_big)
    w_up_t, w_down_g = _finish_gather("gather_up_down", gather_ffn, h2)
    w_down_p = jnp.pad(w_down_g.reshape(4, NSH, D), ((0, 0), (0, PSH - NSH), (0, 0)))
    up_pre, vg, dx3, acc_f = _fwd_ffn(h2, x2, target, vecs, ffn_wb, w_up_t, w_down_p, tm)

    core = ac.reshape(1).astype(jnp.int32)
    mychip = 2 * ax + ay

    other = 1 - core

    def to_pairs(named):
        g4s = [g.reshape((4, 2) + g.shape[1:]) for _, g in named]
        turn = [nm == "w_up" for nm, _ in named]
        halves = [_other_half("rs_other_half_" + t[0], g4, other, tr) for t, g4, tr in zip(named, g4s, turn)]
        from_sibling = _sibling_swap("rs_sibling_" + named[0][0], halves)
        return [_pair_add("rs_pair_add_" + t[0], g4, rv, core, tr)
                for t, g4, rv, tr in zip(named, g4s, from_sibling, turn)]

    dh2, dw_up, dw_down, acc_fw = _bwd_ffn(dx3, up_pre, vg, h2_t, vecs, ffn_wb, w_up_t, w_down_p, tm_big)
    acc_fw = jnp.transpose(acc_fw, (2, 0, 1, 3)).reshape(SUB, 2 * PFF)
    exchange_ffn, token_x = _start_exchange("rs_chips_ffn", to_pairs(
        [("w_up", dw_up), ("w_down", dw_down.reshape(NDEV, w_down.shape[1], D))]), mychip)
    gx, dz, y_t, do1, acc_m, acc_p, dcw, dws, dbs_t = _bwd_mid(
        dh2, dx3, x2, x2d, o1, z, a0, xh_a, sp, gu, dgu, vh, dgv, ln_st, vecs + token_x[0, 0], v512, conv_w, gm_ws[0],
        gm_ws_t, w_out_b, w_in_g, tm)
    rows = jnp.concatenate([acc_m[1:3], acc_m[0:1], acc_m[5:7], acc_f[2:3], acc_m[3:5], acc_m[7:8], acc_f[0:2],
                            jnp.zeros((5, D), F32)], axis=0)
    small_gather, token_s = _start_gather("gather_small_grads", [rows, acc_p, dws, dbs_t, acc_fw, dcw], me)
    dw_in = _mm_all_slots("dw_in", h1_t, dz, w_in.shape[2], min(TK_IN, s), token_s)
    small_pass, token_p = _pass_gather("gather_small_grads", small_gather, dw_in)
    dw_out = _mm_all_slots("dw_out", y_t, do1, D, min(TK_OUT, s), token_p).reshape(NDEV, w_out.shape[1], D)
    exchange_mix, token_m = _start_exchange("rs_chips_mix", to_pairs([("w_in", dw_in), ("w_out", dw_out)]), mychip)

    rows_all, p_all, ws_all, bst_all, fwg_all, cwg_all = _end_gather("gather_small_grads", small_pass, token_m)
    (g_b_ada, g_n1, g_mog, g_n2, g_gf, loss_cols, g_cb, g_clg, g_clb, g_glg, g_glb, g_ws, g_bs, fw_sum,
     cw_sum) = _sum_small(rows_all, p_all, ws_all, bst_all, fwg_all, cwg_all)
    loss = jnp.sum(loss_cols)
    g_fb = fw_sum[3].reshape(NDEV, PSH)[:, :NSH].reshape(ffn_dw_b.shape)
    g_fw = lax.dynamic_index_in_dim(fw_sum[:KF].reshape(KF, NDEV, PSH), me, axis=1, keepdims=False)[:, :NSH]
    g_fw = g_fw.reshape(ffn_dw_w.shape)
    g_cw = lax.dynamic_slice(cw_sum, (0, me * n_cw), (KC, n_cw)).reshape(conv_dw_w.shape)
    small = [
        (b_ada, g_b_ada, m_b_ada, v_b_ada), (norm1_gain, g_n1, m_norm1_gain, v_norm1_gain),
        (conv_dw_w, g_cw, m_conv_dw_w, v_conv_dw_w), (conv_dw_b, g_cb, m_conv_dw_b, v_conv_dw_b),
        (conv_ln_g, g_clg, m_conv_ln_g, v_conv_ln_g), (conv_ln_b, g_clb, m_conv_ln_b, v_conv_ln_b),
        (gm_ln_g, g_glg, m_gm_ln_g, v_gm_ln_g), (gm_ln_b, g_glb, m_gm_ln_b, v_gm_ln_b),
        (gm_ws, g_ws, m_gm_ws, v_gm_ws), (gm_bs, g_bs, m_gm_bs, v_gm_bs),
        (mix_out_gain, g_mog, m_mix_out_gain, v_mix_out_gain), (norm2_gain, g_n2, m_norm2_gain, v_norm2_gain),
        (ffn_dw_w, g_fw, m_ffn_dw_w, v_ffn_dw_w), (ffn_dw_b, g_fb, m_ffn_dw_b, v_ffn_dw_b),
        (final_gain, g_gf, m_final_gain, v_final_gain)]
    small_out = _adam_small(small)
    res = {}
    for name, q, o in zip(("b_ada", "norm1_gain", "conv_dw_w", "conv_dw_b", "conv_ln_g", "conv_ln_b", "gm_ln_g",
                           "gm_ln_b", "gm_ws", "gm_bs", "mix_out_gain", "norm2_gain", "ffn_dw_w", "ffn_dw_b",
                           "final_gain"), small, small_out):
        res[name] = (q[1],) + o

    dmod_all = rows_all[:, :6].reshape(NDEV, 6 * D)
    dm_cols = lax.dynamic_slice(dmod_all, (0, me * n_ada), (NDEV, n_ada))
    g_ada = _ada_grad(jnp.transpose(c_all[:, 0, :]), dm_cols)
    res["w_ada"] = (g_ada,) + tuple(_adam3("adam_ada", w_ada, g_ada, m_w_ada, v_w_ada))

    big = [("w_up", up_t(w_up), up_t(m_w_up), up_t(v_w_up)), ("w_down", w_down, m_w_down, v_w_down),
           ("w_in", w_in, m_w_in, v_w_in), ("w_out", w_out, m_w_out, v_w_out)]
    chip_id = mychip.reshape(1).astype(jnp.int32)
    for t, (sent, landed) in zip(big[:2], _finish_exchange("rs_chips_ffn", exchange_ffn, res["w_ada"][1])):
        res[t[0]] = tuple(_sum_adam("rs_sum_adam_" + t[0], sent, landed, chip_id, t[1], t[2], t[3]))
    res["w_up"] = tuple(up_t(a) for a in res["w_up"])
    for t, (sent, landed) in zip(big[2:], _finish_exchange("rs_chips_mix", exchange_mix, res["w_down"][1])):
        res[t[0]] = tuple(_sum_adam("rs_sum_adam_" + t[0], sent, landed, chip_id, t[1], t[2], t[3]))

    order = ("w_ada", "b_ada", "norm1_gain", "w_in", "conv_dw_w", "conv_dw_b", "conv_ln_g", "conv_ln_b", "gm_ln_g",
             "gm_ln_b", "gm_ws", "gm_bs", "mix_out_gain", "w_out", "norm2_gain", "w_up", "ffn_dw_w", "ffn_dw_b",
             "w_down", "final_gain")
    return (loss, gx.reshape(x.shape), *[res[n][0] for n in order], *[res[n][1] for n in order],
            *[res[n][2] for n in order], *[res[n][3] for n in order])
```

```python
import functools

import jax
import jax.numpy as jnp
from jax import lax
from jax.experimental import pallas as pl
from jax.experimental.pallas import tpu as pltpu

F32 = jnp.float32
BF16 = jnp.bfloat16
NDEV = 8
D = 1024
DC = 512
DFF = 2816
NSH = 704
PSH = 768
PFF = 4 * PSH
KC = 31
KF = 3
CHUNK = 128
NH = 8
HD = 64
HALO_C = 32
HALO_F = 8
LANES = 128
SUB = 8
VROWS = 16
CW_ROWS = 32
TILE_BIG = 512
ROW_BLOCK_BYTES = 2 << 20
TILE = 256
TK_IN = 2048
TK_OUT = 4096
RMS_EPS = 1e-6
LN_EPS = 1e-5
ADAM_LR = 0.001
ADAM_B1 = 0.9
ADAM_B2 = 0.999
ADAM_EPS = 1e-08
ADAM_WD = 0.01
ADAM_STEP = 10
GELU_K = 0.7978845608028654
GELU_C = 0.044715

MESH = pl.DeviceIdType.MESH
ANY = pl.BlockSpec(memory_space=pl.ANY)

G1, SH1, SC1, GT1, G2, SH2, SC2, GT2, MOG, GF = range(10)
CB, CLG, CLB, GLG, GLB = range(5)


def _full(shape):
    return pl.BlockSpec(shape, lambda *_: (0,) * len(shape))


def _arb(n=1):
    return pltpu.CompilerParams(dimension_semantics=("arbitrary",) * n)


def _row(ref, r):
    return ref[pl.ds(r, 1), :]


def _colsum(v):
    return jnp.sum(v, axis=0, keepdims=True)


def _rowmean(v):
    return jnp.mean(v, axis=-1, keepdims=True)


def _rms(x):
    r = lax.rsqrt(_rowmean(x * x) + RMS_EPS)
    return x * r, r


def _rms_bwd(dxn, xn, r):
    return r * (dxn - xn * _rowmean(dxn * xn))


def _ln(x):
    mu = _rowmean(x)
    xc = x - mu
    rstd = lax.rsqrt(_rowmean(xc * xc) + LN_EPS)
    return xc * rstd, rstd


def _ln_bwd(dxh, xhat, rstd):
    return rstd * (dxh - _rowmean(dxh) - xhat * _rowmean(dxh * xhat))


def _sigmoid(x):
    return 0.5 * jnp.tanh(0.5 * x) + 0.5


def _gelu(x):
    t = jnp.tanh(GELU_K * (x + GELU_C * x * x * x))
    return 0.5 * x * (1.0 + t), t


def _gelu_grad(x, t):
    return 0.5 * (1.0 + t) + 0.5 * x * (1.0 - t * t) * (GELU_K * (1.0 + 3.0 * GELU_C * x * x))


def _dot(a, b):
    return jnp.dot(a, b, preferred_element_type=F32)


def _dot_nt(a, b):
    return lax.dot_general(a, b, (((1,), (1,)), ((), ())), preferred_element_type=F32)


def _shift_up(e, s):
    n = e.shape[0]
    return pltpu.roll(e, (n - s) % n, 0)


def _place():
    return lax.axis_index("x"), lax.axis_index("y"), lax.axis_index("c")


def _all_gather(name, xs):
    n = len(xs)

    def body(*refs):
        x_refs, out_refs = refs[:n], refs[n:2 * n]
        send_sems, recv_sems, local_sems = refs[2 * n:]
        x, y, c = _place()
        me, sibling = (x, y, c), (x, y, 1 - c)
        chips = [(1 - x, y), (x, 1 - y), (1 - x, 1 - y)]

        def copy(a, k, block, to, own=False):
            px, py, pc = block
            slot = out_refs[a].at[4 * px + 2 * py + pc]
            return pltpu.make_async_remote_copy(
                src_ref=x_refs[a] if own else slot, dst_ref=slot,
                send_sem=send_sems.at[7 * a + k], recv_sem=recv_sems.at[7 * a + k], device_id=to, device_id_type=MESH)

        mine = [pltpu.make_async_copy(x_refs[a], out_refs[a].at[4 * x + 2 * y + c], local_sems.at[a]) for a in range(n)]
        for cp in mine:
            cp.start()
        first = []
        for a in range(n):
            first.append(copy(a, 0, me, sibling, own=True))
            first += [copy(a, 1 + j, me, (*chip, c), own=True) for j, chip in enumerate(chips)]
        for cp in first:
            cp.start()
        passed = []
        for j, chip in enumerate(chips):
            for a in range(n):
                copy(a, 1 + j, (*chip, c), me).wait_recv()
                cp = copy(a, 4 + j, (*chip, c), sibling)
                cp.start()
                passed.append(cp)
        for a in range(n):
            copy(a, 0, sibling, me).wait_recv()
            for j, chip in enumerate(chips):
                copy(a, 4 + j, (*chip, 1 - c), me).wait_recv()
        for cp in first + passed:
            cp.wait_send()
        for cp in mine:
            cp.wait()

    return pl.pallas_call(
        body, name=name, out_shape=[jax.ShapeDtypeStruct((NDEV,) + a.shape, a.dtype) for a in xs],
        in_specs=[ANY] * n, out_specs=[ANY] * n,
        scratch_shapes=[pltpu.SemaphoreType.DMA((7 * n,)), pltpu.SemaphoreType.DMA((7 * n,)),
                        pltpu.SemaphoreType.DMA((n,))],
    )(*xs)


def _sibling_swap(name, hs):
    n = len(hs)

    def body(*refs):
        h_refs, out_refs = refs[:n], refs[n:2 * n]
        send_sems, recv_sems = refs[2 * n:]
        x, y, c = _place()
        cps = [pltpu.make_async_remote_copy(
            src_ref=h_refs[a].at[k], dst_ref=out_refs[a].at[k],
            send_sem=send_sems.at[4 * a + k], recv_sem=recv_sems.at[4 * a + k],
            device_id=(x, y, 1 - c), device_id_type=MESH) for a in range(n) for k in range(4)]
        for cp in cps:
            cp.start()
        for cp in cps:
            cp.wait()

    return pl.pallas_call(
        body, name=name, out_shape=[jax.ShapeDtypeStruct(h.shape, h.dtype) for h in hs],
        in_specs=[ANY] * n, out_specs=[ANY] * n,
        scratch_shapes=[pltpu.SemaphoreType.DMA((4 * n,)), pltpu.SemaphoreType.DMA((4 * n,))],
    )(*hs)


HBM = pl.BlockSpec(memory_space=pltpu.HBM)
SEM = pl.BlockSpec(memory_space=pltpu.SEMAPHORE)
EFFECT = pltpu.SideEffectType.DATAFLOW_SIDE_EFFECTING


def _in_hbm(a):
    return pltpu.with_memory_space_constraint(a, pltpu.HBM)


def _split_start(name, bufs, copies):
    n = len(bufs)

    def body(*refs):
        for cp in copies(refs[:n], refs[n], refs[n + 1]):
            cp.start()
        refs[-1][...] = jnp.zeros_like(refs[-1])

    out = pl.pallas_call(
        body, name=name,
        out_shape=(pltpu.SemaphoreType.DMA((copies.count,)), pltpu.SemaphoreType.DMA((copies.count,)),
                   *[pltpu.HBM(a.shape, a.dtype) for a in bufs], jax.ShapeDtypeStruct((SUB, LANES), F32)),
        in_specs=[HBM] * n, out_specs=(SEM, SEM, *[HBM] * n, pl.BlockSpec(memory_space=pltpu.VMEM)),
        input_output_aliases={i: 2 + i for i in range(n)},
        compiler_params=pltpu.CompilerParams(has_side_effects=EFFECT),
    )(*[_in_hbm(a) for a in bufs])
    return (out[0], out[1], list(out[2:2 + n])), out[-1]


def _split_wait(name, handle, copies, after):
    send_sems, recv_sems, bufs = handle
    n = len(bufs)

    def body(*refs):
        for cp in copies(refs[:n], refs[n], refs[n + 1]):
            cp.wait_send()
            cp.wait_recv()

    out = pl.pallas_call(
        body, name=name, out_shape=tuple(pltpu.HBM(a.shape, a.dtype) for a in bufs),
        in_specs=[HBM] * n + [SEM, SEM, pl.BlockSpec(memory_space=pl.ANY)], out_specs=tuple([HBM] * n),
        input_output_aliases={i: i for i in range(n)},
        compiler_params=pltpu.CompilerParams(has_side_effects=EFFECT),
    )(*bufs, send_sems, recv_sems, after)
    return list(out)


class _GatherFirstCopies:
    def __init__(self, n):
        self.n, self.count = n, 4 * n

    def __call__(self, refs, send_sems, recv_sems):
        x, y, c = _place()
        peers = [(x, y, 1 - c), (1 - x, y, c), (x, 1 - y, c), (1 - x, 1 - y, c)]
        return [pltpu.make_async_remote_copy(
            src_ref=refs[a], dst_ref=refs[self.n + a].at[4 * x + 2 * y + c],
            send_sem=send_sems.at[4 * a + k], recv_sem=recv_sems.at[4 * a + k], device_id=peer, device_id_type=MESH)
            for a in range(self.n) for k, peer in enumerate(peers)]


class _GatherPassCopies:
    def __init__(self, n):
        self.n, self.count = n, 3 * n

    def __call__(self, refs, send_sems, recv_sems):
        x, y, c = _place()
        cps = []
        for a in range(self.n):
            for j, (px, py) in enumerate([(1 - x, y), (x, 1 - y), (1 - x, 1 - y)]):
                slot = refs[a].at[4 * px + 2 * py + c]
                cps.append(pltpu.make_async_remote_copy(
                    src_ref=slot, dst_ref=slot, send_sem=send_sems.at[3 * a + j], recv_sem=recv_sems.at[3 * a + j],
                    device_id=(x, y, 1 - c), device_id_type=MESH))
        return cps


class _ExchangeCopies:
    def __init__(self, n):
        self.n, self.count = n, 3 * n

    def __call__(self, refs, send_sems, recv_sems):
        x, y, c = _place()
        cps = []
        for a in range(self.n):
            for j, (px, py) in enumerate([(1 - x, y), (x, 1 - y), (1 - x, 1 - y)]):
                cps.append(pltpu.make_async_remote_copy(
                    src_ref=refs[a].at[2 * px + py], dst_ref=refs[self.n + a].at[2 * x + y],
                    send_sem=send_sems.at[3 * a + j], recv_sem=recv_sems.at[3 * a + j],
                    device_id=(px, py, c), device_id_type=MESH))
        return cps


def _own_slot(nslot, src, index):
    land = lax.empty((nslot,) + src.shape, src.dtype)
    return lax.dynamic_update_slice(land, src[None], (index,) + (0,) * src.ndim)


def _start_gather(tag, xs, me):
    lands = [_own_slot(NDEV, a, me) for a in xs]
    return _split_start(tag + "_start", list(xs) + lands, _GatherFirstCopies(len(xs)))


def _pass_gather(tag, handle, after):
    n = len(handle[2]) // 2
    lands = _split_wait(tag + "_wait", handle, _GatherFirstCopies(n), after)[n:]
    return _split_start(tag + "_pass", lands, _GatherPassCopies(n))


def _end_gather(tag, passing, after):
    return _split_wait(tag + "_pass_wait", passing, _GatherPassCopies(len(passing[2])), after)


def _finish_gather(tag, handle, after):
    passing, token = _pass_gather(tag, handle, after)
    return _end_gather(tag, passing, token)


def _start_exchange(tag, hs, mychip):
    lands = [lax.empty(h.shape, h.dtype) for h in hs]
    return _split_start(tag + "_start", list(hs) + lands, _ExchangeCopies(len(hs)))


def _finish_exchange(tag, handle, after):
    n = len(handle[2]) // 2
    bufs = _split_wait(tag + "_wait", handle, _ExchangeCopies(n), after)
    return list(zip(bufs[:n], bufs[n:]))


def _mod_part(c_all, w_ada, b_cols):
    ncol = w_ada.shape[2]

    def body(c_ref, w_ref, b_ref, o_ref):
        cv = c_ref[:, 0, :]
        ca = cv * _sigmoid(cv)
        o_ref[...] = _dot(ca.astype(BF16), w_ref[0].astype(BF16)) + b_ref[...]

    return pl.pallas_call(body, name="mod_part", out_shape=jax.ShapeDtypeStruct((NDEV, ncol), F32))(
        c_all, w_ada, b_cols)


def _ada_grad_adam(c_all_t, dmod_cols, w, m, v):
    ncol = dmod_cols.shape[1]
    tr = TILE

    def body(ct_ref, dm_ref, w_ref, m_ref, v_ref, g_ref, d_ref, mo_ref, vo_ref):
        ct = ct_ref[...]
        ca = ct * _sigmoid(ct)
        g = jnp.zeros((tr, ncol), F32)
        for b in range(NDEV):
            g = g + ca[:, b:b + 1] * dm_ref[pl.ds(b, 1), :]
        g_ref[0] = g
        d_ref[0], mo_ref[0], vo_ref[0] = _adam_math(w_ref[0], g, m_ref[0], v_ref[0])

    spec = pl.BlockSpec((1, tr, ncol), lambda i: (0, i, 0))
    return pl.pallas_call(
        body, name="ada_grad_adam", grid=(D // tr,),
        in_specs=[pl.BlockSpec((tr, NDEV), lambda i: (i, 0)), _full((NDEV, ncol))] + [spec] * 3,
        out_specs=[spec] * 4, out_shape=[jax.ShapeDtypeStruct(w.shape, F32)] * 4, compiler_params=_arb(),
    )(c_all_t, dmod_cols, w, m, v)


def _fwd_in(x2d, vecs, w_in_g, tm):
    s = x2d.shape[0]
    nc = w_in_g.shape[2]

    def body(x_ref, v_ref, w_ref, z_ref, a0_ref, h1t_ref):
        xn, _ = _rms(x_ref[...])
        h = (xn * _row(v_ref, G1)) * (1.0 + _row(v_ref, SC1)) + _row(v_ref, SH1)
        hb = h.astype(BF16)
        h1t_ref[...] = hb.T
        for d in range(NDEV):
            z_ref[:, pl.ds(d * nc, nc)] = _dot(hb, w_ref[d])
        a0_ref[...] = z_ref[:, :DC] * _sigmoid(z_ref[:, DC:2 * DC])

    return pl.pallas_call(
        body, name="fwd_in", grid=(s // tm,),
        in_specs=[pl.BlockSpec((tm, D), lambda i: (i, 0)), _full((VROWS, D)), _full((NDEV, D, nc))],
        out_specs=[pl.BlockSpec((tm, 4 * DC), lambda i: (i, 0)), pl.BlockSpec((tm, DC), lambda i: (i, 0)),
                   pl.BlockSpec((D, tm), lambda i: (0, i))],
        out_shape=[jax.ShapeDtypeStruct((s, 4 * DC), F32), jax.ShapeDtypeStruct((s, DC), F32),
                   jax.ShapeDtypeStruct((D, s), BF16)],
        compiler_params=_arb(),
    )(x2d, vecs, w_in_g)


def _causal_mask(lower):
    r = lax.broadcasted_iota(jnp.int32, (CHUNK, CHUNK), 0)
    c = lax.broadcasted_iota(jnp.int32, (CHUNK, CHUNK), 1)
    return (r >= c) if lower else (r <= c)


def _first_head_lanes():
    return lax.broadcasted_iota(jnp.int32, (CHUNK, CHUNK), 1) < HD


def _fwd_mid(a0, z, x2d, vecs, v512, conv_w, gm_ws, bs_exp, w_out_b, tm):
    s = x2d.shape[0]
    hb = tm // HALO_C

    def body(a0_ref, halo_ref, zg_ref, x_ref, v_ref, p_ref, cw_ref, ws_ref, bs_ref, wo_ref,
             xh_ref, sp_ref, x2_ref, o1_ref, h2_ref, gu_ref, dgu_ref, vh_ref, dgv_ref, st_ref, h2t_ref, a1_s):
        i = pl.program_id(0)
        for c0 in range(0, DC, LANES):
            cols = pl.ds(c0, LANES)
            halo = halo_ref[:, cols]
            e = jnp.concatenate([jnp.where(i > 0, halo, jnp.zeros_like(halo)), a0_ref[:, cols]], axis=0)
            acc = jnp.broadcast_to(p_ref[pl.ds(CB, 1), cols], (tm, LANES))
            for k in range(KC):
                acc = acc + _shift_up(e, HALO_C - (KC - 1) + k)[:tm, :] * cw_ref[pl.ds(k, 1), cols]
            a1_s[:, cols] = acc
        xh, rstd = _ln(a1_s[...])
        xh_ref[...] = xh
        a2 = xh * _row(p_ref, CLG) + _row(p_ref, CLB)
        a3 = a2 * _sigmoid(a2)
        gu_pre = zg_ref[:, :DC]
        gu, tu = _gelu(gu_pre)
        gu_ref[...] = gu
        dgu_ref[...] = _gelu_grad(gu_pre, tu)
        gv_pre = zg_ref[:, DC:]
        gvg, tv = _gelu(gv_pre)
        dgv_ref[...] = _gelu_grad(gv_pre, tv)
        vh, vrstd = _ln(gvg)
        vh_ref[...] = vh
        st_ref[...] = jnp.concatenate([jnp.broadcast_to(rstd, (tm, LANES)), jnp.broadcast_to(vrstd, (tm, LANES))], axis=1)
        gvn = (vh * _row(p_ref, GLG) + _row(p_ref, GLB)).astype(BF16)
        low = _causal_mask(True)
        first = _first_head_lanes()
        wm = [jnp.where(low, ws_ref[0, h], 0.0).astype(BF16) for h in range(NH)]
        for n in range(tm // CHUNK):
            for p in range(NH // 2):
                v = gvn[n * CHUNK:(n + 1) * CHUNK, p * CHUNK:(p + 1) * CHUNK]
                blk = jnp.where(first, _dot(wm[2 * p], v), _dot(wm[2 * p + 1], v))
                sp_ref[pl.ds(n * CHUNK, CHUNK), pl.ds(p * CHUNK, CHUNK)] = blk + bs_ref[:, pl.ds(p * CHUNK, CHUNK)]
        g = gu * sp_ref[...]
        an, _ = _rms(a3)
        gn, _ = _rms(g)
        mog = _row(v_ref, MOG)
        y = jnp.concatenate([an * mog[:, :DC], gn * mog[:, DC:]], axis=1).astype(BF16)
        o1 = _dot(y, wo_ref[...])
        o1_ref[...] = o1
        x2 = x_ref[...] + _row(v_ref, GT1) * o1
        x2_ref[...] = x2
        xn2, _ = _rms(x2)
        h2 = (xn2 * _row(v_ref, G2)) * (1.0 + _row(v_ref, SC2)) + _row(v_ref, SH2)
        h2b = h2.astype(BF16)
        h2_ref[...] = h2b
        h2t_ref[...] = h2b.T

    tile = lambda w: pl.BlockSpec((tm, w), lambda i: (i, 0))
    return pl.pallas_call(
        body, name="fwd_mid", grid=(s // tm,),
        in_specs=[tile(DC), pl.BlockSpec((HALO_C, DC), lambda i: (jnp.maximum(i * hb - 1, 0), 0)),
                  pl.BlockSpec((tm, 2 * DC), lambda i: (i, 1)), tile(D), _full((VROWS, D)), _full((SUB, DC)),
                  _full((CW_ROWS, DC)), _full((1, NH, CHUNK, CHUNK)), _full((CHUNK, DC)), _full((D, D))],
        out_specs=[tile(DC), tile(DC), tile(D), tile(D), tile(D), tile(DC), tile(DC), tile(DC), tile(DC), tile(2 * LANES),
                   pl.BlockSpec((D, tm), lambda i: (0, i))],
        out_shape=[jax.ShapeDtypeStruct((s, DC), F32), jax.ShapeDtypeStruct((s, DC), F32),
                   jax.ShapeDtypeStruct((s, D), F32), jax.ShapeDtypeStruct((s, D), F32),
                   jax.ShapeDtypeStruct((s, D), BF16)] + [jax.ShapeDtypeStruct((s, DC), F32)] * 4
        + [jax.ShapeDtypeStruct((s, 2 * LANES), F32), jax.ShapeDtypeStruct((D, s), BF16)],
        scratch_shapes=[pltpu.VMEM((tm, DC), F32)],
        compiler_params=_arb(),
    )(a0, a0, z, x2d, vecs, v512, conv_w, gm_ws, bs_exp, w_out_b)


def _ffn_conv(fw_ref, cols, p2, p1, pre):
    return (fw_ref[pl.ds(3, 1), cols] + fw_ref[pl.ds(0, 1), cols] * p2
            + fw_ref[pl.ds(1, 1), cols] * p1 + fw_ref[pl.ds(2, 1), cols] * pre)


def _fwd_ffn(h2, x2, target, vecs, ffn_wb, w_up_t, w_down_p, tm):
    s = x2.shape[0]

    def body(h2_ref, x2_ref, t_ref, v_ref, fw_ref, wu_hbm, wd_hbm,
             up_ref, vg_ref, dx3_ref, acc_ref, wu, wd, carry, stage):
        i = pl.program_id(0)

        @pl.when(i == 0)
        def _():
            for sh in range(NDEV):
                pltpu.sync_copy(wu_hbm.at[sh], stage)
                wu[sh] = stage[...].T
            pltpu.sync_copy(wd_hbm, wd)
            carry[...] = jnp.zeros_like(carry)
            acc_ref[...] = jnp.zeros_like(acc_ref)

        h2v = h2_ref[...]
        o2 = jnp.zeros((tm, D), F32)
        for j in range(4):
            conv = []
            for sh in (j, 4 + j):
                cols = pl.ds(sh * PSH, PSH)
                pre = _dot(h2v, wu[sh])
                up_ref[:, cols] = pre.astype(BF16)
                e = jnp.concatenate([carry[:, cols], pre], axis=0)
                carry[:, cols] = pre[tm - HALO_F:, :]
                conv.append(_ffn_conv(fw_ref, cols, pltpu.roll(e, 2, 0)[HALO_F:, :],
                                      pltpu.roll(e, 1, 0)[HALO_F:, :], pre))
            val, gate = conv
            vg_ref[:, pl.ds(j * PSH, PSH)] = val.astype(BF16)
            vg_ref[:, pl.ds((4 + j) * PSH, PSH)] = gate.astype(BF16)
            f = ((gate * _sigmoid(gate)) * val).astype(BF16)
            o2 = o2 + _dot(f, wd[j])
        x3 = x2_ref[...] + _row(v_ref, GT2) * o2
        xn3, r3 = _rms(x3)
        gf = _row(v_ref, GF)
        diff = xn3 * gf - t_ref[...]
        acc_ref[pl.ds(1, 1), :] += _colsum(diff * diff) * (0.5 / D)
        dout = diff * (1.0 / D)
        acc_ref[pl.ds(0, 1), :] += _colsum(dout * xn3)
        dx3 = _rms_bwd(dout * gf, xn3, r3)
        dx3_ref[...] = dx3
        acc_ref[pl.ds(2, 1), :] += _colsum(dx3 * o2)

    tile = lambda w: pl.BlockSpec((tm, w), lambda i: (i, 0))
    return pl.pallas_call(
        body, name="fwd_ffn", grid=(s // tm,),
        in_specs=[tile(D), tile(D), tile(D), _full((VROWS, D)), _full((SUB, 2 * PFF)), ANY, ANY],
        out_specs=[tile(2 * PFF), tile(2 * PFF), tile(D), _full((SUB, D))],
        out_shape=[jax.ShapeDtypeStruct((s, 2 * PFF), BF16), jax.ShapeDtypeStruct((s, 2 * PFF), BF16),
                   jax.ShapeDtypeStruct((s, D), F32), jax.ShapeDtypeStruct((SUB, D), F32)],
        scratch_shapes=[pltpu.VMEM((NDEV, D, PSH), BF16), pltpu.VMEM((4, PSH, D), BF16),
                        pltpu.VMEM((HALO_F, 2 * PFF), F32), pltpu.VMEM((PSH, D), BF16)],
        compiler_params=_arb(),
    )(h2, x2, target, vecs, ffn_wb, w_up_t, w_down_p)


def _bwd_ffn(dx3, up_pre, vg, h2_t, vecs, ffn_wb, w_up_t, w_down_p, tm):
    s = dx3.shape[0]
    nt = s // tm

    def body(dx3_ref, up_ref, upg_ref, val_ref, gate_ref, h2t_ref, v_ref, fw_ref, fwg_ref, wu_ref, wug_ref, wd_ref,
             dh2_ref, dwu_ref, dwd_ref, accf_ref, carry):
        i = pl.program_id(1)

        @pl.when(i == 0)
        def _():
            for ref in (carry, dwu_ref, dwd_ref, accf_ref):
                ref[...] = jnp.zeros_like(ref)

        do2 = (dx3_ref[...] * _row(v_ref, GT2)).astype(BF16)
        df = _dot_nt(do2, wd_ref[...])
        val = val_ref[...].astype(F32)
        gate = gate_ref[...].astype(F32)
        sg = _sigmoid(gate)
        sl = gate * sg
        f_t = (sl * val).astype(BF16).T
        dwd_ref[...] += _dot(f_t, do2)[:NSH, :]
        dups = (df * sl, df * val * (sg * (1.0 + gate * (1.0 - sg))))
        h2t = h2t_ref[...]
        dh2 = jnp.zeros((tm, D), F32)
        for half, (dup, pre_ref, w_ref, wmat_ref) in enumerate(
                zip(dups, (up_ref, upg_ref), (fw_ref, fwg_ref), (wu_ref, wug_ref))):
            cols = pl.ds(half * PSH, PSH)
            e = jnp.concatenate([dup, carry[:, cols]], axis=0)
            carry[:, cols] = dup[:HALO_F, :]
            d1 = _shift_up(e, 1)[:tm, :]
            d2 = _shift_up(e, 2)[:tm, :]
            pre = pre_ref[...].astype(F32)
            accf_ref[half, pl.ds(3, 1), :] += _colsum(dup)
            accf_ref[half, pl.ds(0, 1), :] += _colsum(d2 * pre)
            accf_ref[half, pl.ds(1, 1), :] += _colsum(d1 * pre)
            accf_ref[half, pl.ds(2, 1), :] += _colsum(dup * pre)
            dpre = (_row(w_ref, 0) * d2 + _row(w_ref, 1) * d1 + _row(w_ref, 2) * dup).astype(BF16)
            dwu_ref[half] += _dot(h2t, dpre)[:, :NSH]
            dh2 = dh2 + _dot(dpre, wmat_ref[...])
        dh2_ref[...] = dh2

    rev = lambda j, i: nt - 1 - i
    in_specs = [
        pl.BlockSpec((tm, D), lambda j, i: (rev(j, i), 0)),
        pl.BlockSpec((tm, PSH), lambda j, i: (rev(j, i), j)), pl.BlockSpec((tm, PSH), lambda j, i: (rev(j, i), 4 + j)),
        pl.BlockSpec((tm, PSH), lambda j, i: (rev(j, i), j)), pl.BlockSpec((tm, PSH), lambda j, i: (rev(j, i), 4 + j)),
        pl.BlockSpec((D, tm), lambda j, i: (0, rev(j, i))), _full((VROWS, D)),
        pl.BlockSpec((SUB, PSH), lambda j, i: (0, j)), pl.BlockSpec((SUB, PSH), lambda j, i: (0, 4 + j)),
        pl.BlockSpec((None, PSH, D), lambda j, i: (j, 0, 0)), pl.BlockSpec((None, PSH, D), lambda j, i: (4 + j, 0, 0)),
        pl.BlockSpec((None, PSH, D), lambda j, i: (j, 0, 0))]
    dh2, dw_up, dw_down, accf = pl.pallas_call(
        body, name="bwd_ffn", grid=(4, nt), in_specs=in_specs,
        out_specs=[pl.BlockSpec((None, tm, D), lambda j, i: (j, rev(j, i), 0)),
                   pl.BlockSpec((2, None, D, NSH), lambda j, i: (0, j, 0, 0)),
                   pl.BlockSpec((None, NSH, D), lambda j, i: (j, 0, 0)),
                   pl.BlockSpec((2, None, SUB, PSH), lambda j, i: (0, j, 0, 0))],
        out_shape=[jax.ShapeDtypeStruct((4, s, D), F32), jax.ShapeDtypeStruct((2, 4, D, NSH), F32),
                   jax.ShapeDtypeStruct((4, NSH, D), F32), jax.ShapeDtypeStruct((2, 4, SUB, PSH), F32)],
        scratch_shapes=[pltpu.VMEM((HALO_F, 2 * PSH), F32)],
        compiler_params=_arb(2),
    )(dx3, up_pre, up_pre, vg, vg, h2_t, vecs, ffn_wb, ffn_wb, w_up_t, w_up_t, w_down_p)
    return dh2, dw_up.reshape(NDEV, D, NSH), dw_down, accf


def _bwd_mid(dh2, dx3, x2, x2d, o1, z, a0, xh_a, sp, gu, dgu, vh, dgv, st, vecs, v512, conv_w, gm_ws, gm_ws_t, w_out_b,
             w_in_g, tm):
    s = x2d.shape[0]
    nt = s // tm
    nc = w_in_g.shape[2]

    def body(dh2a_ref, dh2b_ref, dh2c_ref, dh2d_ref, dx3_ref, x2_ref, x_ref, o1_ref, z_ref, a0_ref, xh_ref, sp_ref,
             gu_ref, dgu_ref, vh_ref, dgv_ref, st_ref, v_ref, p_ref, cw_ref, ws_ref, wst_ref, wo_ref, wi_ref, gx_ref, dz_ref, yt_ref, do1_ref, acc_ref, accp_ref,
             dcw_ref, dws_ref, dbst_ref, dbs_s, carry, da1_s, dsp_s, dgvn_s):
        i = pl.program_id(0)

        @pl.when(i == 0)
        def _():
            for ref in (carry, dbs_s, acc_ref, accp_ref, dcw_ref, dws_ref, dbst_ref):
                ref[...] = jnp.zeros_like(ref)

        dh2v = (dh2a_ref[...] + dh2b_ref[...]) + (dh2c_ref[...] + dh2d_ref[...])
        xn2, r2 = _rms(x2_ref[...])
        g2 = _row(v_ref, G2)
        sc2 = 1.0 + _row(v_ref, SC2)
        acc_ref[pl.ds(5, 1), :] += _colsum(dh2v)
        acc_ref[pl.ds(6, 1), :] += _colsum(dh2v * (xn2 * g2))
        acc_ref[pl.ds(7, 1), :] += _colsum(dh2v * sc2 * xn2)
        dx2v = dx3_ref[...] + _rms_bwd(dh2v * sc2 * g2, xn2, r2)
        do1 = (dx2v * _row(v_ref, GT1)).astype(BF16)
        do1_ref[...] = do1
        acc_ref[pl.ds(0, 1), :] += _colsum(dx2v * o1_ref[...])
        dy = _dot_nt(do1, wo_ref[...])
        mog = _row(v_ref, MOG)

        lane_tiles = DC // LANES
        xh = xh_ref[...]
        rstd = jnp.concatenate([st_ref[:, :LANES]] * lane_tiles, axis=1)
        clg = _row(p_ref, CLG)
        a2 = xh * clg + _row(p_ref, CLB)
        s2 = _sigmoid(a2)
        a3 = a2 * s2
        an, ra = _rms(a3)
        dya = dy[:, :DC]
        da3 = _rms_bwd(dya * mog[:, :DC], an, ra)
        da2 = da3 * (s2 * (1.0 + a2 * (1.0 - s2)))
        accp_ref[pl.ds(CLB, 1), :] += _colsum(da2)
        accp_ref[pl.ds(CLG, 1), :] += _colsum(da2 * xh)
        da1 = _ln_bwd(da2 * clg, xh, rstd)
        accp_ref[pl.ds(CB, 1), :] += _colsum(da1)
        da1_s[...] = da1
        for c0 in range(0, DC, LANES):
            cols = pl.ds(c0, LANES)
            d = da1_s[:, cols]
            e = jnp.concatenate([d, carry[:, cols]], axis=0)
            carry[:, cols] = d[:HALO_C, :]
            a0c = a0_ref[:, cols]
            acc = jnp.zeros((tm, LANES), F32)
            for j in range(KC):
                ahead = _shift_up(e, j)[:tm, :]
                acc = acc + ahead * cw_ref[pl.ds(KC - 1 - j, 1), cols]
                dcw_ref[pl.ds(KC - 1 - j, 1), cols] += _colsum(a0c * ahead)
            sgc = _sigmoid(z_ref[:, pl.ds(DC + c0, LANES)])
            dz_ref[:, cols] = (acc * sgc).astype(BF16)
            dz_ref[:, pl.ds(DC + c0, LANES)] = (acc * z_ref[:, cols] * sgc * (1.0 - sgc)).astype(BF16)

        gu = gu_ref[...]
        vh = vh_ref[...]
        vrstd = jnp.concatenate([st_ref[:, LANES:]] * lane_tiles, axis=1)
        glg = _row(p_ref, GLG)
        gvn = (vh * glg + _row(p_ref, GLB)).astype(BF16)
        spv = sp_ref[...]
        g = gu * spv
        gn, rg = _rms(g)
        yt_ref[...] = jnp.concatenate([an * mog[:, :DC], gn * mog[:, DC:]], axis=1).astype(BF16).T
        acc_ref[pl.ds(4, 1), :] += jnp.concatenate([_colsum(dya * an), _colsum(dy[:, DC:] * gn)], axis=1)
        dg = _rms_bwd(dy[:, DC:] * mog[:, DC:], gn, rg)
        dz_ref[:, pl.ds(2 * DC, DC)] = (dg * spv * dgu_ref[...]).astype(BF16)
        dsp_s[...] = dg * gu
        upper = _causal_mask(False)
        first = _first_head_lanes()
        wmt = [jnp.where(upper, wst_ref[h], 0.0).astype(BF16) for h in range(NH)]
        for n in range(tm // CHUNK):
            rows = pl.ds(n * CHUNK, CHUNK)
            for p in range(NH // 2):
                cols = pl.ds(p * CHUNK, CHUNK)
                dsp = dsp_s[rows, cols]
                dbs_s[:, cols] += dsp
                da = jnp.where(first, dsp, 0.0).astype(BF16)
                db = jnp.where(first, 0.0, dsp).astype(BF16)
                v = gvn[n * CHUNK:(n + 1) * CHUNK, p * CHUNK:(p + 1) * CHUNK]
                dws_ref[2 * p] += _dot_nt(da, v)
                dws_ref[2 * p + 1] += _dot_nt(db, v)
                dgvn_s[rows, cols] = _dot(wmt[2 * p], da) + _dot(wmt[2 * p + 1], db)
        dgvn = dgvn_s[...]
        accp_ref[pl.ds(GLB, 1), :] += _colsum(dgvn)
        accp_ref[pl.ds(GLG, 1), :] += _colsum(dgvn * vh)
        dgvg = _ln_bwd(dgvn * glg, vh, vrstd)
        dz_ref[:, pl.ds(3 * DC, DC)] = (dgvg * dgv_ref[...]).astype(BF16)

        dh1 = jnp.zeros((tm, D), F32)
        for d in range(NDEV):
            dh1 = dh1 + _dot_nt(dz_ref[:, pl.ds(d * nc, nc)], wi_ref[d])
        xn, r1 = _rms(x_ref[...])
        g1 = _row(v_ref, G1)
        sc = 1.0 + _row(v_ref, SC1)
        acc_ref[pl.ds(1, 1), :] += _colsum(dh1)
        acc_ref[pl.ds(2, 1), :] += _colsum(dh1 * (xn * g1))
        acc_ref[pl.ds(3, 1), :] += _colsum(dh1 * sc * xn)
        gx_ref[...] = dx2v + _rms_bwd(dh1 * sc * g1, xn, r1)

        @pl.when(i == nt - 1)
        def _():
            low = _causal_mask(True)
            for h in range(NH):
                dws_ref[h] = jnp.where(low, dws_ref[h], 0.0)
            lane = lax.broadcasted_iota(jnp.int32, (CHUNK, CHUNK), 1)
            out = jnp.zeros((CHUNK, CHUNK), F32)
            for h in range(NH):
                hs = jnp.sum(dbs_s[:, pl.ds((h // 2) * CHUNK, CHUNK)]
                             * ((lane >= (h % 2) * HD) & (lane < (h % 2 + 1) * HD)).astype(F32),
                             axis=1, keepdims=True)
                out = jnp.where(lane == h, hs, out)
            dbst_ref[...] = out

    tile = lambda w: pl.BlockSpec((tm, w), lambda i: (nt - 1 - i, 0))
    return pl.pallas_call(
        body, name="bwd_mid", grid=(nt,),
        in_specs=[pl.BlockSpec((None, tm, D), functools.partial(lambda k, i: (k, nt - 1 - i, 0), k)) for k in range(4)]
        + [tile(D), tile(D), tile(D), tile(D), tile(2 * DC), tile(DC),
                  tile(DC), tile(DC), tile(DC), tile(DC), tile(DC), tile(DC), tile(2 * LANES), _full((VROWS, D)), _full((SUB, DC)), _full((CW_ROWS, DC)),
                  _full((NH, CHUNK, CHUNK)), _full((NH, CHUNK, CHUNK)), _full((D, D)), _full((NDEV, D, nc))],
        out_specs=[tile(D), tile(4 * DC), pl.BlockSpec((D, tm), lambda i: (0, nt - 1 - i)), tile(D),
                   _full((VROWS, D)), _full((SUB, DC)), _full((CW_ROWS, DC)),
                   _full((NH, CHUNK, CHUNK)), _full((CHUNK, CHUNK))],
        out_shape=[jax.ShapeDtypeStruct((s, D), F32), jax.ShapeDtypeStruct((s, 4 * DC), BF16),
                   jax.ShapeDtypeStruct((D, s), BF16), jax.ShapeDtypeStruct((s, D), BF16),
                   jax.ShapeDtypeStruct((VROWS, D), F32), jax.ShapeDtypeStruct((SUB, DC), F32),
                   jax.ShapeDtypeStruct((CW_ROWS, DC), F32), jax.ShapeDtypeStruct((NH, CHUNK, CHUNK), F32),
                   jax.ShapeDtypeStruct((CHUNK, CHUNK), F32)],
        scratch_shapes=[pltpu.VMEM((CHUNK, DC), F32), pltpu.VMEM((HALO_C, DC), F32), pltpu.VMEM((tm, DC), F32),
                        pltpu.VMEM((tm, DC), F32), pltpu.VMEM((tm, DC), F32)],
        compiler_params=_arb(),
    )(dh2, dh2, dh2, dh2, dx3, x2, x2d, o1, z, a0, xh_a, sp, gu, dgu, vh, dgv, st, vecs, v512, conv_w, gm_ws, gm_ws_t, w_out_b, w_in_g)


def _mm_all_slots(name, at, b, bw, tk, after):
    k1, s = at.shape
    nslot = b.shape[1] // bw

    def body(a_ref, b_ref, after_ref, o_ref):
        @pl.when(pl.program_id(0) == 0)
        def _():
            o_ref[...] = jnp.zeros_like(o_ref)

        t = _dot(a_ref[...], b_ref[...])
        for j in range(nslot):
            o_ref[j] += t[:, j * bw:(j + 1) * bw]

    return pl.pallas_call(
        body, name=name, grid=(s // tk,),
        in_specs=[pl.BlockSpec((k1, tk), lambda k: (0, k)), pl.BlockSpec((tk, nslot * bw), lambda k: (k, 0)), ANY],
        out_specs=_full((nslot, k1, bw)), out_shape=jax.ShapeDtypeStruct((nslot, k1, bw), F32),
        compiler_params=_arb(),
    )(at, b, after)


def _adam_math(w, g, m, v):
    m = ADAM_B1 * m + (1.0 - ADAM_B1) * g
    v = ADAM_B2 * v + (1.0 - ADAM_B2) * (g * g)
    m_hat = m / (1.0 - ADAM_B1 ** ADAM_STEP)
    v_hat = v / (1.0 - ADAM_B2 ** ADAM_STEP)
    delta = -ADAM_LR * (m_hat / (jnp.sqrt(v_hat) + ADAM_EPS) + ADAM_WD * w)
    return delta, m, v


def _row_block(rows, cols):
    tr = rows
    while tr * cols * 4 > ROW_BLOCK_BYTES and tr % (4 * SUB) == 0:
        tr //= 2
    return tr


def _sum_adam(name, sent, landed, mychip, w, m, v):
    n, rows, cols = landed.shape
    tr = _row_block(rows, cols)

    def body(mc_ref, p_ref, own_ref, w_ref, m_ref, v_ref, g_ref, d_ref, mo_ref, vo_ref):
        own = own_ref[...].astype(F32)
        g = jnp.zeros((tr, cols), F32)
        for k in range(n):
            g = g + jnp.where(mc_ref[0] == k, own, p_ref[k].astype(F32))
        g_ref[0] = g
        d_ref[0], mo_ref[0], vo_ref[0] = _adam_math(w_ref[0], g, m_ref[0], v_ref[0])

    spec = pl.BlockSpec((1, tr, cols), lambda i, mc: (0, i, 0))
    return pl.pallas_call(
        body, name=name,
        grid_spec=pltpu.PrefetchScalarGridSpec(
            num_scalar_prefetch=1, grid=(rows // tr,),
            in_specs=[pl.BlockSpec((n, tr, cols), lambda i, mc: (0, i, 0)),
                      pl.BlockSpec((None, tr, cols), lambda i, mc: (mc[0], i, 0))] + [spec] * 3,
            out_specs=[spec] * 4),
        out_shape=[jax.ShapeDtypeStruct(w.shape, F32)] * 4, compiler_params=_arb(),
    )(mychip, landed, sent, w, m, v)


def _other_half(name, g4, other, transpose=False):
    _, _, rows, cols = g4.shape
    tr = rows if transpose else _row_block(rows, cols)
    oshape = (cols, rows) if transpose else (tr, cols)

    def body(c_ref, a_ref, o_ref):
        a = a_ref[...]
        o_ref[...] = (a.T if transpose else a).astype(BF16)

    return pl.pallas_call(
        body, name=name,
        grid_spec=pltpu.PrefetchScalarGridSpec(
            num_scalar_prefetch=1, grid=(4, rows // tr),
            in_specs=[pl.BlockSpec((None, None, tr, cols), lambda k, i, c_ref: (k, c_ref[0], i, 0))],
            out_specs=pl.BlockSpec((None,) + oshape, lambda k, i, c_ref: (k, i, 0))),
        out_shape=jax.ShapeDtypeStruct((4, cols, rows) if transpose else (4, rows, cols), BF16),
        compiler_params=_arb(2),
    )(other, g4)


def _pair_add(name, g4, recv, core, transpose=False):
    _, _, rows, cols = g4.shape
    tr = rows if transpose else _row_block(rows, cols)
    oshape = (cols, rows) if transpose else (tr, cols)

    def body(c_ref, a_ref, b_ref, o_ref):
        a = a_ref[...]
        o_ref[...] = ((a.T if transpose else a) + b_ref[...].astype(F32)).astype(BF16)

    return pl.pallas_call(
        body, name=name,
        grid_spec=pltpu.PrefetchScalarGridSpec(
            num_scalar_prefetch=1, grid=(4, rows // tr),
            in_specs=[pl.BlockSpec((None, None, tr, cols), lambda k, i, c_ref: (k, c_ref[0], i, 0)),
                      pl.BlockSpec((None,) + oshape, lambda k, i, c_ref: (k, i, 0))],
            out_specs=pl.BlockSpec((None,) + oshape, lambda k, i, c_ref: (k, i, 0))),
        out_shape=jax.ShapeDtypeStruct(recv.shape, BF16), compiler_params=_arb(2),
    )(core, g4, recv)


def _sum_small(rows_all, p_all, ws_all, bst_all, fw_all, cw_all):
    def body(a_ref, p_ref, ws_ref, bst_ref, fw_ref, cw_ref,
             g_b_ada, g_n1, g_mog, g_n2, g_gf, loss_cols, g_cb, g_clg, g_clb, g_glg, g_glb, g_ws, g_bs, fw_sum,
             cw_sum):
        def total(ref):
            t = ref[0]
            for k in range(1, NDEV):
                t = t + ref[k]
            return t

        a = total(a_ref)
        g_b_ada[...] = jnp.concatenate([a[k:k + 1, :] for k in range(6)], axis=1)
        g_n1[...] = a[6:7, :]
        g_mog[...] = a[7:8, :]
        g_n2[...] = a[8:9, :]
        g_gf[...] = a[9:10, :].reshape(D)
        loss_cols[...] = a[10:11, :]
        p = total(p_ref)
        for k, ref in zip((CB, CLG, CLB, GLG, GLB), (g_cb, g_clg, g_clb, g_glg, g_glb)):
            ref[...] = p[k:k + 1, :]
        g_ws[0] = total(ws_ref)
        g_bs[0] = jnp.transpose(total(bst_ref))[:NH, :]
        fw_sum[...] = total(fw_ref)
        cw_sum[...] = total(cw_ref)

    vec = lambda n: jax.ShapeDtypeStruct((1, n), F32)
    return pl.pallas_call(
        body, name="sum_small_grads",
        out_shape=[vec(6 * D), vec(D), vec(D), vec(D), jax.ShapeDtypeStruct((D,), F32), vec(D),
                   vec(DC), vec(DC), vec(DC), vec(DC), vec(DC),
                   jax.ShapeDtypeStruct((1, NH, CHUNK, CHUNK), F32), jax.ShapeDtypeStruct((1, NH, CHUNK), F32),
                   jax.ShapeDtypeStruct((SUB, 2 * PFF), F32), jax.ShapeDtypeStruct((CW_ROWS, DC), F32)],
    )(rows_all, p_all, ws_all, bst_all, fw_all, cw_all)


def _adam_small(quads):
    n = len(quads)

    def body(*refs):
        ins, outs = refs[:4 * n], refs[4 * n:]
        for q in range(n):
            w, g, m, v = (r[...] for r in ins[4 * q:4 * q + 4])
            outs[3 * q][...], outs[3 * q + 1][...], outs[3 * q + 2][...] = _adam_math(w, g, m, v)

    flat = [a for q in quads for a in q]
    outs = pl.pallas_call(
        body, name="adam_small",
        out_shape=[jax.ShapeDtypeStruct(q[0].shape, F32) for q in quads for _ in range(3)],
    )(*flat)
    return [tuple(outs[3 * q:3 * q + 3]) for q in range(n)]


def kernel(x, c, w_ada, b_ada, norm1_gain, w_in, conv_dw_w, conv_dw_b, conv_ln_g, conv_ln_b, gm_ln_g, gm_ln_b, gm_ws, gm_bs, mix_out_gain, w_out, norm2_gain, w_up, ffn_dw_w, ffn_dw_b, w_down, final_gain, loss_target, m_w_ada, m_b_ada, m_norm1_gain, m_w_in, m_conv_dw_w, m_conv_dw_b, m_conv_ln_g, m_conv_ln_b, m_gm_ln_g, m_gm_ln_b, m_gm_ws, m_gm_bs, m_mix_out_gain, m_w_out, m_norm2_gain, m_w_up, m_ffn_dw_w, m_ffn_dw_b, m_w_down, m_final_gain, v_w_ada, v_b_ada, v_norm1_gain, v_w_in, v_conv_dw_w, v_conv_dw_b, v_conv_ln_g, v_conv_ln_b, v_gm_ln_g, v_gm_ln_b, v_gm_ws, v_gm_bs, v_mix_out_gain, v_w_out, v_norm2_gain, v_w_up, v_ffn_dw_w, v_ffn_dw_b, v_w_down, v_final_gain):
    s = x.shape[1]
    ax, ay, ac = _place()
    me = 4 * ax + 2 * ay + ac
    n_ada = w_ada.shape[2]
    n_cw = conv_dw_w.shape[2]
    x2d = x[0]
    target = loss_target[0]
    pad_sh = lambda a: jnp.pad(a, [(0, 0)] * (a.ndim - 1) + [(0, PSH - NSH)])

    c_all, cw_all, fw_all = _all_gather("gather_small", [c, conv_dw_w[0], ffn_dw_w[0]])

    first_shards, c_all = lax.optimization_barrier(((w_in[0].astype(BF16), w_out[0].astype(BF16)), c_all))
    gather_in, token_a = _start_gather("gather_in_out", list(first_shards), me)
    c_all = c_all + token_a[0, 0]
    conv_w = jnp.pad(jnp.transpose(cw_all, (1, 0, 2)).reshape(KC, DC), ((0, CW_ROWS - KC), (0, 0)))
    ffn_w = jnp.transpose(pad_sh(fw_all), (1, 0, 2)).reshape(KF, 2 * PFF)
    ffn_b = pad_sh(ffn_dw_b.reshape(NDEV, NSH)).reshape(1, 2 * PFF)
    ffn_wb = jnp.concatenate([ffn_w, ffn_b, jnp.zeros((SUB - KF - 1, 2 * PFF), F32)], axis=0)

    b_cols = lax.dynamic_slice(b_ada, (0, me * n_ada), (1, n_ada))
    (mod_all,) = _all_gather("gather_mod", [_mod_part(c_all, w_ada, b_cols)])
    up_t = lambda a: jnp.swapaxes(a, 1, 2)
    w_up_shard = jnp.pad(up_t(w_up)[0].astype(BF16), ((0, PSH - NSH), (0, 0)))
    shards, mod_all = lax.optimization_barrier(((w_up_shard, w_down[0].astype(BF16)), mod_all))
    gather_ffn, token_c = _start_gather("gather_up_down", list(shards), me)
    mod = lax.dynamic_index_in_dim(mod_all, me, axis=1, keepdims=False).reshape(6, D)
    sh1, sc1, gt1, sh2, sc2, gt2 = [mod[k:k + 1] for k in range(6)]
    vecs = jnp.concatenate([norm1_gain, sh1, sc1, gt1, norm2_gain, sh2, sc2, gt2, mix_out_gain,
                            final_gain.reshape(1, D), jnp.zeros((6, D), F32)], axis=0)
    vecs = vecs + token_c[0, 0]
    v512 = jnp.concatenate([conv_dw_b, conv_ln_g, conv_ln_b, gm_ln_g, gm_ln_b, jnp.zeros((3, DC), F32)], axis=0)
    bs_exp = jnp.repeat(jnp.transpose(gm_bs[0]), HD, axis=1)
    gm_ws_t = jnp.swapaxes(gm_ws[0], 1, 2)

    tm_big, tm = min(TILE_BIG, s), min(TILE, s)
    w_in_g, w_out_g = _finish_gather("gather_in_out", gather_in, vecs)
    w_out_b = w_out_g.reshape(D, D)
    z, a0, h1_t = _fwd_in(x2d, vecs, w_in_g, tm_big)
    xh_a, sp, x2, o1, h2, gu, dgu, vh, dgv, ln_st, h2_t = _fwd_mid(
        a0, z, x2d, vecs, v512, conv_w, gm_ws, bs_exp, w_out_b, tm_big)
    w_up_t, w_down_g = _finish_gather("gather_up_down", gather_ffn, h2)
    w_down_p = jnp.pad(w_down_g.reshape(4, NSH, D), ((0, 0), (0, PSH - NSH), (0, 0)))
    up_pre, vg, dx3, acc_f = _fwd_ffn(h2, x2, target, vecs, ffn_wb, w_up_t, w_down_p, tm)

    core = ac.reshape(1).astype(jnp.int32)
    mychip = 2 * ax + ay

    other = 1 - core

    def to_pairs(named):
        g4s = [g.reshape((4, 2) + g.shape[1:]) for _, g in named]
        turn = [nm == "w_up" for nm, _ in named]
        halves = [_other_half("rs_other_half_" + t[0], g4, other, tr) for t, g4, tr in zip(named, g4s, turn)]
        from_sibling = _sibling_swap("rs_sibling_" + named[0][0], halves)
        return [_pair_add("rs_pair_add_" + t[0], g4, rv, core, tr)
                for t, g4, rv, tr in zip(named, g4s, from_sibling, turn)]

    dh2, dw_up, dw_down, acc_fw = _bwd_ffn(dx3, up_pre, vg, h2_t, vecs, ffn_wb, w_up_t, w_down_p, tm_big)
    acc_fw = jnp.transpose(acc_fw, (2, 0, 1, 3)).reshape(SUB, 2 * PFF)
    exchange_ffn, token_x = _start_exchange("rs_chips_ffn", to_pairs(
        [("w_up", dw_up), ("w_down", dw_down.reshape(NDEV, w_down.shape[1], D))]), mychip)
    gx, dz, y_t, do1, acc_m, acc_p, dcw, dws, dbs_t = _bwd_mid(
        dh2, dx3, x2, x2d, o1, z, a0, xh_a, sp, gu, dgu, vh, dgv, ln_st, vecs + token_x[0, 0], v512, conv_w, gm_ws[0],
        gm_ws_t, w_out_b, w_in_g, tm)
    rows = jnp.concatenate([acc_m[1:3], acc_m[0:1], acc_m[5:7], acc_f[2:3], acc_m[3:5], acc_m[7:8], acc_f[0:2],
                            jnp.zeros((5, D), F32)], axis=0)
    small_gather, token_s = _start_gather("gather_small_grads", [rows, acc_p, dws, dbs_t, acc_fw, dcw], me)
    dw_in = _mm_all_slots("dw_in", h1_t, dz, w_in.shape[2], min(TK_IN, s), token_s)
    small_pass, token_p = _pass_gather("gather_small_grads", small_gather, dw_in)
    dw_out = _mm_all_slots("dw_out", y_t, do1, D, min(TK_OUT, s), token_p).reshape(NDEV, w_out.shape[1], D)
    exchange_mix, token_m = _start_exchange("rs_chips_mix", to_pairs([("w_in", dw_in), ("w_out", dw_out)]), mychip)

    rows_all, p_all, ws_all, bst_all, fwg_all, cwg_all = _end_gather("gather_small_grads", small_pass, token_m)
    (g_b_ada, g_n1, g_mog, g_n2, g_gf, loss_cols, g_cb, g_clg, g_clb, g_glg, g_glb, g_ws, g_bs, fw_sum,
     cw_sum) = _sum_small(rows_all, p_all, ws_all, bst_all, fwg_all, cwg_all)
    loss = jnp.sum(loss_cols)
    g_fb = fw_sum[3].reshape(NDEV, PSH)[:, :NSH].reshape(ffn_dw_b.shape)
    g_fw = lax.dynamic_index_in_dim(fw_sum[:KF].reshape(KF, NDEV, PSH), me, axis=1, keepdims=False)[:, :NSH]
    g_fw = g_fw.reshape(ffn_dw_w.shape)
    g_cw = lax.dynamic_slice(cw_sum, (0, me * n_cw), (KC, n_cw)).reshape(conv_dw_w.shape)
    small = [
        (b_ada, g_b_ada, m_b_ada, v_b_ada), (norm1_gain, g_n1, m_norm1_gain, v_norm1_gain),
        (conv_dw_w, g_cw, m_conv_dw_w, v_conv_dw_w), (conv_dw_b, g_cb, m_conv_dw_b, v_conv_dw_b),
        (conv_ln_g, g_clg, m_conv_ln_g, v_conv_ln_g), (conv_ln_b, g_clb, m_conv_ln_b, v_conv_ln_b),
        (gm_ln_g, g_glg, m_gm_ln_g, v_gm_ln_g), (gm_ln_b, g_glb, m_gm_ln_b, v_gm_ln_b),
        (gm_ws, g_ws, m_gm_ws, v_gm_ws), (gm_bs, g_bs, m_gm_bs, v_gm_bs),
        (mix_out_gain, g_mog, m_mix_out_gain, v_mix_out_gain), (norm2_gain, g_n2, m_norm2_gain, v_norm2_gain),
        (ffn_dw_w, g_fw, m_ffn_dw_w, v_ffn_dw_w), (ffn_dw_b, g_fb, m_ffn_dw_b, v_ffn_dw_b),
        (final_gain, g_gf, m_final_gain, v_final_gain)]
    small_out = _adam_small(small)
    res = {}
    for name, q, o in zip(("b_ada", "norm1_gain", "conv_dw_w", "conv_dw_b", "conv_ln_g", "conv_ln_b", "gm_ln_g",
                           "gm_ln_b", "gm_ws", "gm_bs", "mix_out_gain", "norm2_gain", "ffn_dw_w", "ffn_dw_b",
                           "final_gain"), small, small_out):
        res[name] = (q[1],) + o

    dmod_all = rows_all[:, :6].reshape(NDEV, 6 * D)
    dm_cols = lax.dynamic_slice(dmod_all, (0, me * n_ada), (NDEV, n_ada))
    res["w_ada"] = tuple(_ada_grad_adam(jnp.transpose(c_all[:, 0, :]), dm_cols, w_ada, m_w_ada, v_w_ada))

    big = [("w_up", up_t(w_up), up_t(m_w_up), up_t(v_w_up)), ("w_down", w_down, m_w_down, v_w_down),
           ("w_in", w_in, m_w_in, v_w_in), ("w_out", w_out, m_w_out, v_w_out)]
    chip_id = mychip.reshape(1).astype(jnp.int32)
    for t, (sent, landed) in zip(big[:2], _finish_exchange("rs_chips_ffn", exchange_ffn, res["w_ada"][1])):
        res[t[0]] = tuple(_sum_adam("rs_sum_adam_" + t[0], sent, landed, chip_id, t[1], t[2], t[3]))
    res["w_up"] = tuple(up_t(a) for a in res["w_up"])
    for t, (sent, landed) in zip(big[2:], _finish_exchange("rs_chips_mix", exchange_mix, res["w_down"][1])):
        res[t[0]] = tuple(_sum_adam("rs_sum_adam_" + t[0], sent, landed, chip_id, t[1], t[2], t[3]))

    order = ("w_ada", "b_ada", "norm1_gain", "w_in", "conv_dw_w", "conv_dw_b", "conv_ln_g", "conv_ln_b", "gm_ln_g",
             "gm_ln_b", "gm_ws", "gm_bs", "mix_out_gain", "w_out", "norm2_gain", "w_up", "ffn_dw_w", "ffn_dw_b",
             "w_down", "final_gain")
    return (loss, gx.reshape(x.shape), *[res[n][0] for n in order], *[res[n][1] for n in order],
            *[res[n][2] for n in order], *[res[n][3] for n in order])
```
